```python
import math
import jax
import jax.numpy as jnp
from jax import lax
import numpy as np

D_MODEL = 1024
BATCH = 16
SEQ = 256
DEPTH = 2
DEC_BATCH = 4
DEC_SEQ = 1024
PAST_LEN = 512

GRID_W = 64
NORM_EPS = 1e-6
N_EVEN = (DEPTH + 1) // 2
N_ODD = DEPTH // 2
N_MOD = 6
D_FF = 4 * D_MODEL

GDN_HEADS = 4
GDN_DK = 128
GDN_DV = 128
GDN_CHUNK = 64
QKV_CONV_W = 3
GDN_QK_W = GDN_HEADS * GDN_DK
GDN_V_W = GDN_HEADS * GDN_DV
QKV_W = 2 * GDN_QK_W + GDN_V_W
SC_WIDTH = D_MODEL - GDN_V_W
SC_CONV_W = 3
EV_SPLITS = (QKV_W, GDN_V_W, 2 * GDN_HEADS, 2 * GDN_HEADS, SC_WIDTH, SC_WIDTH, SC_WIDTH)
EV_IN = sum(EV_SPLITS)
EV_OUT = GDN_V_W + SC_WIDTH

ATT_HEADS = 8
ATT_KV_HEADS = 2
ATT_GROUP = ATT_HEADS // ATT_KV_HEADS
ATT_HD = 64
ATT_Q_W = ATT_HEADS * ATT_HD
ATT_KV_W = ATT_KV_HEADS * ATT_HD
WINDOW = 128
ATT_BLOCK = 128
ROPE_BASE = 10000.0
NEG_INF = -1e30
RWKV_HEADS = 8
RWKV_HD = 64
RWKV_W = RWKV_HEADS * RWKV_HD
W_LORA = 64
A_LORA = 64
G_LORA = 128
GN_EPS = 64e-5
RWKV_SPLITS = (RWKV_W, RWKV_W, RWKV_W, 2 * W_LORA, 2 * A_LORA, G_LORA)
RWKV_IN = sum(RWKV_SPLITS)
OD_SPLITS = (ATT_Q_W, ATT_KV_W, ATT_KV_W, RWKV_IN)
OD_IN = sum(OD_SPLITS)
OD_OUT = ATT_Q_W + RWKV_W

kernel_name = "hybrid_diffusion_deltanet_conv_swa_rwkv7_step"

F32 = jnp.float32


def split_cols(x, sizes):
    idx = np.cumsum(sizes)[:-1].tolist()
    return jnp.split(x, idx, axis=-1)


def rmsnorm(x, g):
    x32 = x.astype(F32)
    return (x32 * lax.rsqrt(jnp.mean(x32 * x32, -1, keepdims=True) + NORM_EPS) * g).astype(x.dtype)


def l2norm(x):
    x32 = x.astype(F32)
    return x32 * lax.rsqrt(jnp.sum(x32 * x32, -1, keepdims=True) + 1e-6)


def dwconv_centred(x, w):
    K = w.shape[0]
    p = K // 2
    L = x.shape[1]
    xp = jnp.pad(x, ((0, 0), (p, p), (0, 0)))
    return sum(xp[:, i:i + L] * w[i] for i in range(K))


def token_shift_bi(x, mu):
    x_prev = jnp.pad(x, ((0, 0), (1, 0), (0, 0)))[:, :-1]
    x_next = jnp.pad(x, ((0, 0), (0, 1), (0, 0)))[:, 1:]
    return x + mu[0] * (x_prev - x) + mu[1] * (x_next - x)


def softmax_with_sink(s, sink):
    m = jnp.maximum(jnp.max(s, -1, keepdims=True), sink)
    e = jnp.exp(s - m)
    return e / (jnp.sum(e, -1, keepdims=True) + jnp.exp(sink - m))


def rope_2d(x):
    L = x.shape[1]
    rows = L // GRID_W
    row = jnp.repeat(jnp.arange(rows), GRID_W).astype(F32)
    col = jnp.tile(jnp.arange(GRID_W), rows).astype(F32)
    half = x.shape[-1] // 2
    inv = ROPE_BASE ** (-jnp.arange(0, half, 2, dtype=F32) / half)

    def rot(seg, pos):
        ang = pos[:, None] * inv
        cos, sin = jnp.cos(ang)[None, :, None, :], jnp.sin(ang)[None, :, None, :]
        s1, s2 = jnp.split(seg.astype(F32), 2, axis=-1)
        return jnp.concatenate([s1 * cos - s2 * sin, s2 * cos + s1 * sin], -1)

    return jnp.concatenate([rot(x[..., :half], row), rot(x[..., half:], col)], -1).astype(x.dtype)


def gdn_chunked(q, k, v, g, beta, s0):
    B, L, H, _ = q.shape
    DV = v.shape[-1]
    C = GDN_CHUNK
    n = L // C

    def to_chunks(t):
        t = t.astype(F32).reshape(B, n, C, H, *t.shape[3:])
        return jnp.moveaxis(t, (1, 3), (0, 2))

    qc, kc, vc, bc = to_chunks(q), to_chunks(k), to_chunks(v), to_chunks(beta)
    gc = jnp.cumsum(to_chunks(g), axis=-1)
    incl = jnp.tril(jnp.ones((C, C), dtype=bool))
    strict = jnp.tril(jnp.ones((C, C), dtype=bool), -1)
    diff = gc[..., :, None] - gc[..., None, :]
    decay = jnp.where(incl, jnp.exp(jnp.where(incl, diff, 0.0)), 0.0)
    kb = kc * bc[..., None]
    a_mat = jnp.where(strict, jnp.einsum('nbhid,nbhjd->nbhij', kb, kc) * decay, 0.0)
    lhs = a_mat + jnp.eye(C, dtype=F32)
    rhs = jnp.concatenate([vc * bc[..., None], kb * jnp.exp(gc)[..., None]], axis=-1)
    sol = lax.linalg.triangular_solve(lhs, rhs, left_side=True, lower=True, unit_diagonal=True)
    u, w = sol[..., :DV], sol[..., DV:]
    intra = jnp.where(incl, jnp.einsum('nbhid,nbhjd->nbhij', qc, kc) * decay, 0.0)
    q_dec = qc * jnp.exp(gc)[..., None]
    k_dec = kc * jnp.exp(gc[..., -1:] - gc)[..., None]
    g_end = jnp.exp(gc[..., -1])[..., None, None]

    def step(S, xs):
        u_c, w_c, intra_c, qd_c, kd_c, ge_c = xs
        e = u_c - jnp.einsum('bhck,bhkv->bhcv', w_c, S)
        o = jnp.einsum('bhck,bhkv->bhcv', qd_c, S) + jnp.einsum('bhij,bhjv->bhiv', intra_c, e)
        S = S * ge_c + jnp.einsum('bhck,bhcv->bhkv', kd_c, e)
        return S, o

    s_fin, o = lax.scan(step, s0.astype(F32), (u, w, intra, q_dec, k_dec, g_end))
    o = jnp.moveaxis(o, (0, 2), (1, 3)).reshape(B, L, H, DV)
    return o, s_fin


def rwkv7_scan(r, decay, k, v, kk, a, s0):
    xs = tuple(jnp.moveaxis(t.astype(F32), 1, 0) for t in (r, decay, k, v, kk, kk * a))

    def step(S, xt):
        r_t, w_t, k_t, v_t, kk_t, b_t = xt
        sa = jnp.einsum('bhvk,bhk->bhv', S, kk_t)
        S = S * w_t[:, :, None, :] - sa[..., None] * b_t[:, :, None, :] + v_t[..., None] * k_t[:, :, None, :]
        return S, jnp.einsum('bhvk,bhk->bhv', S, r_t)

    s_fin, y = lax.scan(step, s0.astype(F32), xs)
    return jnp.moveaxis(y, 0, 1), s_fin


def dirs(t_fwd, t_bwd):
    return jnp.stack([t_fwd, t_bwd[:, ::-1]])


def even_mixer(h, w_in, w_out, gdn_conv, a_log, dt_bias, gdn_norm, sc_conv, s0_f, s0_b):
    B, L, _ = h.shape
    qkv, z, beta_raw, a_raw, sc_b, sc_c, sc_h = split_cols(h @ w_in, EV_SPLITS)
    qkv = jax.nn.silu(dwconv_centred(qkv, gdn_conv))
    q, k, v = split_cols(qkv, (GDN_QK_W, GDN_QK_W, GDN_V_W))
    q = l2norm(q.reshape(B, L, GDN_HEADS, GDN_DK)) * GDN_DK ** -0.5
    k = l2norm(k.reshape(B, L, GDN_HEADS, GDN_DK))
    v = v.reshape(B, L, GDN_HEADS, GDN_DV).astype(F32)
    beta = jax.nn.sigmoid(beta_raw.reshape(B, L, 2, GDN_HEADS).astype(F32))
    g = -jnp.exp(a_log.astype(F32)) * jax.nn.softplus(a_raw.reshape(B, L, 2, GDN_HEADS).astype(F32) + dt_bias)
    o2, s2 = jax.vmap(gdn_chunked)(dirs(q, q), dirs(k, k), dirs(v, v), dirs(g[:, :, 0], g[:, :, 1]),
                                   dirs(beta[:, :, 0], beta[:, :, 1]), jnp.stack([s0_f, s0_b]).astype(F32))
    o = o2[0] + o2[1][:, ::-1]
    o = rmsnorm(o, gdn_norm) * jax.nn.silu(z.reshape(B, L, GDN_HEADS, GDN_DV).astype(F32))
    sc = sc_b * dwconv_centred(sc_c * sc_h, sc_conv)
    y = jnp.concatenate([o.reshape(B, L, GDN_V_W).astype(h.dtype), sc], -1) @ w_out
    return y, s2[0], s2[1]


def rwkv7_mixer(x, rw, s0_f, s0_b):
    mu, w0, w_up, a0, a_up, g_up, k_k, k_a, r_k, ln_w, ln_b = rw
    B, L, _ = x.shape
    x = token_shift_bi(x, mu).astype(F32)
    r, k, v, wl, al, gl = split_cols(x, RWKV_SPLITS)
    w_log = -jax.nn.softplus(-(w0 + jnp.einsum('bldr,drc->bldc', jnp.tanh(wl.reshape(B, L, 2, W_LORA)), w_up))) - 0.5
    decay = jnp.exp(-jnp.exp(w_log))
    a = jax.nn.sigmoid(a0 + jnp.einsum('bldr,drc->bldc', al.reshape(B, L, 2, A_LORA), a_up))
    gate = jax.nn.sigmoid(gl) @ g_up

    def heads(t):
        return t.reshape(*t.shape[:-1], RWKV_HEADS, RWKV_HD)

    kk = l2norm(heads(k * k_k))
    k2 = k[:, :, None, :] * (1 + (a - 1) * k_a)
    rh, vh = heads(r), heads(v)
    y2, s2 = jax.vmap(rwkv7_scan)(dirs(rh, rh), dirs(heads(decay[:, :, 0]), heads(decay[:, :, 1])),
                                  dirs(heads(k2[:, :, 0]), heads(k2[:, :, 1])), dirs(vh, vh), dirs(kk, kk),
                                  dirs(heads(a[:, :, 0]), heads(a[:, :, 1])), jnp.stack([s0_f, s0_b]).astype(F32))
    y = y2[0] + y2[1][:, ::-1]
    mean = jnp.mean(y, -1, keepdims=True)
    var = jnp.mean(jnp.square(y - mean), -1, keepdims=True)
    y = ((y - mean) * lax.rsqrt(var + GN_EPS)).reshape(B, L, RWKV_W) * ln_w + ln_b
    bonus = jnp.sum(rh[:, :, None] * heads(k2) * r_k, axis=-1, keepdims=True).sum(axis=2) * vh
    y = (y + bonus.reshape(B, L, RWKV_W)) * gate
    return y, s2[0], s2[1]


def odd_project(h, w_in):
    B, L, _ = h.shape
    q, k, v, rw = split_cols(h @ w_in, OD_SPLITS)
    return (q.reshape(B, L, ATT_HEADS, ATT_HD), k.reshape(B, L, ATT_KV_HEADS, ATT_HD),
            v.reshape(B, L, ATT_KV_HEADS, ATT_HD), rw)


def attend_context(q, kc, vc, sink):
    B, L = q.shape[:2]
    qg = q.reshape(B, L, ATT_KV_HEADS, ATT_GROUP, ATT_HD)
    s = jnp.einsum('bqkgd,bkmd->bkgqm', qg, kc, preferred_element_type=F32) * ATT_HD ** -0.5
    p = softmax_with_sink(s, sink.astype(F32).reshape(ATT_KV_HEADS, ATT_GROUP, 1, 1))
    o = jnp.einsum('bkgqm,bkmd->bqkgd', p.astype(vc.dtype), vc)
    return o.reshape(B, L, ATT_Q_W)


def attend_latent(q, k, v, ck, cv, sink):
    B, L = q.shape[:2]
    nb = L // ATT_BLOCK
    qb = q.reshape(B, nb, ATT_BLOCK, ATT_KV_HEADS, ATT_GROUP, ATT_HD)

    def band(t):
        tb = jnp.pad(t, ((0, 0), (ATT_BLOCK, ATT_BLOCK), (0, 0), (0, 0))).reshape(B, nb + 2, ATT_BLOCK, ATT_KV_HEADS, ATT_HD)
        return jnp.concatenate([tb[:, :-2], tb[:, 1:-1], tb[:, 2:]], axis=2)

    kw, vw = band(k), band(v)
    scale = ATT_HD ** -0.5
    s_loc = jnp.einsum('bnqkgd,bnmkd->bnkgqm', qb, kw, preferred_element_type=F32) * scale
    s_ctx = jnp.einsum('bnqkgd,bkmd->bnkgqm', qb, ck, preferred_element_type=F32) * scale
    qi = jnp.arange(ATT_BLOCK)[:, None]
    kj = jnp.arange(3 * ATT_BLOCK)[None, :]
    key_pos = jnp.arange(nb)[:, None, None] * ATT_BLOCK - ATT_BLOCK + kj
    valid = (jnp.abs(kj - ATT_BLOCK - qi) <= WINDOW) & (key_pos >= 0) & (key_pos < L)
    s_loc = jnp.where(valid[None, :, None, None], s_loc, NEG_INF)
    p = softmax_with_sink(jnp.concatenate([s_loc, s_ctx], -1),
                          sink.astype(F32).reshape(ATT_KV_HEADS, ATT_GROUP, 1, 1))
    p_loc, p_ctx = p[..., :3 * ATT_BLOCK], p[..., 3 * ATT_BLOCK:]
    o = (jnp.einsum('bnkgqm,bnmkd->bnqkgd', p_loc.astype(v.dtype), vw)
         + jnp.einsum('bnkgqm,bkmd->bnqkgd', p_ctx.astype(cv.dtype), cv))
    return o.reshape(B, L, ATT_Q_W)


def odd_mixer_context(h, w_in, w_out, sink, rw):
    B = h.shape[0]
    q, k, v, x_rw = odd_project(h, w_in)
    kc, vc = jnp.swapaxes(k, 1, 2), jnp.swapaxes(v, 1, 2)
    att = attend_context(q, kc, vc, sink)
    z0 = jnp.zeros((B, RWKV_HEADS, RWKV_HD, RWKV_HD), F32)
    rwo, s_f, s_b = rwkv7_mixer(x_rw, rw, z0, z0)
    y = jnp.concatenate([att, rwo.astype(att.dtype)], -1) @ w_out
    return y, kc, vc, s_f, s_b


def odd_mixer_latent(h, w_in, w_out, sink, rw, ck, cv, s0_f, s0_b):
    q, k, v, x_rw = odd_project(h, w_in)
    att = attend_latent(rope_2d(q), rope_2d(k), v, ck, cv, sink)
    rwo, _, _ = rwkv7_mixer(x_rw, rw, s0_f, s0_b)
    return jnp.concatenate([att, rwo.astype(att.dtype)], -1) @ w_out


def sq_relu_mlp(h, w1, w2):
    return jnp.square(jax.nn.relu(h @ w1)) @ w2


def setup_inputs(seed: int = 0) -> dict:
    key = jax.random.key(seed)
    ks = iter(jax.random.split(key, 48))

    def nrm(shape, scale=1.0):
        return jax.random.normal(next(ks), shape, F32) * scale

    def unif(shape, lo, hi):
        return jax.random.uniform(next(ks), shape, F32, minval=lo, maxval=hi)

    dt = jnp.exp(unif((N_EVEN, 2, GDN_HEADS), math.log(1e-3), math.log(1e-1)))
    return {
        "x_prompt": nrm((BATCH, SEQ, D_MODEL)),
        "x_sample": nrm((DEC_BATCH, DEC_SEQ, D_MODEL)),
        "state_gdn_fwd": nrm((DEC_BATCH, N_EVEN, GDN_HEADS, GDN_DK, GDN_DV), 0.3),
        "state_gdn_bwd": nrm((DEC_BATCH, N_EVEN, GDN_HEADS, GDN_DK, GDN_DV), 0.3),
        "cache_attn_k": nrm((DEC_BATCH, N_ODD, ATT_KV_HEADS, PAST_LEN, ATT_HD)),
        "cache_attn_v": nrm((DEC_BATCH, N_ODD, ATT_KV_HEADS, PAST_LEN, ATT_HD)),
        "state_rwkv_fwd": nrm((DEC_BATCH, N_ODD, RWKV_HEADS, RWKV_HD, RWKV_HD), 0.5),
        "state_rwkv_bwd": nrm((DEC_BATCH, N_ODD, RWKV_HEADS, RWKV_HD, RWKV_HD), 0.5),
        "c": nrm((DEC_BATCH, D_MODEL)),
        "c_ctx": nrm((D_MODEL,)),
        "mod_w": nrm((DEPTH, D_MODEL, N_MOD * D_MODEL), D_MODEL ** -0.5),
        "mod_b": nrm((DEPTH, N_MOD * D_MODEL), 0.02),
        "norm_mix": 1.0 + nrm((DEPTH, D_MODEL), 0.02),
        "norm_mlp": 1.0 + nrm((DEPTH, D_MODEL), 0.02),
        "mlp_w1": nrm((DEPTH, D_MODEL, D_FF), D_MODEL ** -0.5),
        "mlp_w2": nrm((DEPTH, D_FF, D_MODEL), D_FF ** -0.5),
        "norm_final": 1.0 + nrm((D_MODEL,), 0.02),
        "ev_w_in": nrm((N_EVEN, D_MODEL, EV_IN), D_MODEL ** -0.5),
        "ev_w_out": nrm((N_EVEN, EV_OUT, D_MODEL), EV_OUT ** -0.5),
        "gdn_conv": nrm((N_EVEN, QKV_CONV_W, QKV_W), QKV_CONV_W ** -0.5),
        "gdn_a_log": jnp.log(unif((N_EVEN, 2, GDN_HEADS), 1.0, 16.0)),
        "gdn_dt_bias": dt + jnp.log(-jnp.expm1(-dt)),
        "gdn_norm": 1.0 + nrm((N_EVEN, GDN_DV), 0.02),
        "sc_conv": nrm((N_EVEN, SC_CONV_W, SC_WIDTH), SC_CONV_W ** -0.5),
        "od_w_in": nrm((N_ODD, D_MODEL, OD_IN), D_MODEL ** -0.5),
        "od_w_out": nrm((N_ODD, OD_OUT, D_MODEL), OD_OUT ** -0.5),
        "attn_sink": nrm((N_ODD, ATT_HEADS), 0.5),
        "rwkv_mu": unif((N_ODD, 2, RWKV_IN), 0.0, 0.5),
        "rwkv_w0": unif((N_ODD, 2, RWKV_W), -6.0, -1.0),
        "rwkv_w_up": nrm((N_ODD, 2, W_LORA, RWKV_W), 0.5 * W_LORA ** -0.5),
        "rwkv_a0": nrm((N_ODD, 2, RWKV_W), 0.1),
        "rwkv_a_up": nrm((N_ODD, 2, A_LORA, RWKV_W), A_LORA ** -0.5),
        "rwkv_g_up": nrm((N_ODD, G_LORA, RWKV_W), G_LORA ** -0.5),
        "rwkv_k_k": 0.85 + nrm((N_ODD, RWKV_W), 0.02),
        "rwkv_k_a": 1.0 + nrm((N_ODD, RWKV_W), 0.02),
        "rwkv_r_k": nrm((N_ODD, RWKV_HEADS, RWKV_HD), 0.1),
        "rwkv_ln_w": 1.0 + nrm((N_ODD, RWKV_W), 0.02),
        "rwkv_ln_b": nrm((N_ODD, RWKV_W), 0.02),
    }


def reference(x_prompt, x_sample, state_gdn_fwd, state_gdn_bwd, cache_attn_k, cache_attn_v,
              state_rwkv_fwd, state_rwkv_bwd, c, c_ctx, mod_w, mod_b, norm_mix, norm_mlp, mlp_w1, mlp_w2,
              norm_final, ev_w_in, ev_w_out, gdn_conv, gdn_a_log, gdn_dt_bias, gdn_norm, sc_conv,
              od_w_in, od_w_out, attn_sink, rwkv_mu, rwkv_w0, rwkv_w_up, rwkv_a0, rwkv_a_up, rwkv_g_up,
              rwkv_k_k, rwkv_k_a, rwkv_r_k, rwkv_ln_w, rwkv_ln_b):
    xp, xs = x_prompt, x_sample
    Bp = x_prompt.shape[0]
    gdn_f, gdn_b, att_k, att_v, rw_f, rw_b = [], [], [], [], [], []
    for layer in range(DEPTH):
        mod_p = jnp.split(jax.nn.silu(c_ctx) @ mod_w[layer] + mod_b[layer], N_MOD, axis=-1)
        mod_s = jnp.split((jax.nn.silu(c) @ mod_w[layer] + mod_b[layer])[:, None, :], N_MOD, axis=-1)
        hp = rmsnorm(xp, norm_mix[layer]) * (1 + mod_p[1]) + mod_p[0]
        hs = rmsnorm(xs, norm_mix[layer]) * (1 + mod_s[1]) + mod_s[0]
        if layer % 2 == 0:
            e = layer // 2
            ev = (ev_w_in[e], ev_w_out[e], gdn_conv[e], gdn_a_log[e], gdn_dt_bias[e], gdn_norm[e], sc_conv[e])
            z0 = jnp.zeros((Bp, GDN_HEADS, GDN_DK, GDN_DV), F32)
            yp, s_f, s_b = even_mixer(hp, *ev, z0, z0)
            ys, _, _ = even_mixer(hs, *ev, state_gdn_fwd[:, e], state_gdn_bwd[:, e])
            gdn_f.append(s_f)
            gdn_b.append(s_b)
        else:
            o = layer // 2
            rw = (rwkv_mu[o], rwkv_w0[o], rwkv_w_up[o], rwkv_a0[o], rwkv_a_up[o], rwkv_g_up[o],
                  rwkv_k_k[o], rwkv_k_a[o], rwkv_r_k[o], rwkv_ln_w[o], rwkv_ln_b[o])
            yp, kc, vc, s_f, s_b = odd_mixer_context(hp, od_w_in[o], od_w_out[o], attn_sink[o], rw)
            ys = odd_mixer_latent(hs, od_w_in[o], od_w_out[o], attn_sink[o], rw,
                                  cache_attn_k[:, o], cache_attn_v[:, o], state_rwkv_fwd[:, o], state_rwkv_bwd[:, o])
            att_k.append(kc)
            att_v.append(vc)
            rw_f.append(s_f)
            rw_b.append(s_b)
        xp = xp + mod_p[2] * yp
        xs = xs + mod_s[2] * ys
        hp = rmsnorm(xp, norm_mlp[layer]) * (1 + mod_p[4]) + mod_p[3]
        hs = rmsnorm(xs, norm_mlp[layer]) * (1 + mod_s[4]) + mod_s[3]
        xp = xp + mod_p[5] * sq_relu_mlp(hp, mlp_w1[layer], mlp_w2[layer])
        xs = xs + mod_s[5] * sq_relu_mlp(hs, mlp_w1[layer], mlp_w2[layer])
    y_prompt = rmsnorm(xp, norm_final)
    y_sample = rmsnorm(xs, norm_final)
    return (y_prompt, y_sample, jnp.stack(gdn_f, axis=1), jnp.stack(gdn_b, axis=1),
            jnp.stack(att_k, axis=1), jnp.stack(att_v, axis=1), jnp.stack(rw_f, axis=1), jnp.stack(rw_b, axis=1))
```

```python
import functools

import jax
import jax.numpy as jnp
import numpy as np
from jax import lax
from jax.experimental import pallas as pl
from jax.experimental.pallas import tpu as pltpu

F32 = jnp.float32
BF16 = jnp.bfloat16
HIGHEST = lax.Precision.HIGHEST

D_MODEL = 1024
N_MOD = 6
D_FF = 4 * D_MODEL
NORM_EPS = 1e-6
TOKEN_TILE = 256
MOD_ROWS = 8

GDN_HEADS = 4
GDN_D = 128
GDN_CHUNK = 64
GDN_QKV_W = 3 * GDN_HEADS * GDN_D
SC_WIDTH = 512
EV_IN_PAD = 3712

ATT_HEADS = 8
ATT_KV_HEADS = 2
ATT_GROUP = ATT_HEADS // ATT_KV_HEADS
ATT_HD = 64
ATT_Q_W = ATT_HEADS * ATT_HD
ATT_KV_W = ATT_KV_HEADS * ATT_HD
ATT_W = ATT_Q_W + 2 * ATT_KV_W
WINDOW = 128
ATT_BLOCK = 128
GRID_W = 64
ROPE_BASE = 10000.0
NEG_INF = -1e30

RWKV_HEADS = 8
RWKV_HD = 64
RWKV_W = RWKV_HEADS * RWKV_HD
RWKV_LORA = 64
RWKV_IN = 3 * RWKV_W + 3 * 2 * RWKV_LORA
RWKV_CHUNK = 64
GN_EPS = 64e-5

VMEM_LIMIT = 56 * 1024 * 1024


def _mm(a, b, hi=True):
    if not hi:
        a, b = a.astype(BF16), b.astype(BF16)
    return lax.dot_general(a, b, (((1,), (0,)), ((), ())), precision=HIGHEST if hi else None,
                           preferred_element_type=F32)


def _mm_nt(a, b, hi=True):
    if not hi:
        a, b = a.astype(BF16), b.astype(BF16)
    return lax.dot_general(a, b, (((1,), (1,)), ((), ())), precision=HIGHEST if hi else None,
                           preferred_element_type=F32)


def _mm_tn(a, b, hi=True):
    if not hi:
        a, b = a.astype(BF16), b.astype(BF16)
    return lax.dot_general(a, b, (((0,), (0,)), ((), ())), precision=HIGHEST if hi else None,
                           preferred_element_type=F32)


def _silu(x):
    return x * jax.nn.sigmoid(x)


def _softplus(x):
    return jnp.maximum(x, 0.0) + jnp.log1p(jnp.exp(-jnp.abs(x)))


def _rms(x, w):
    return x * lax.rsqrt(jnp.mean(x * x, axis=-1, keepdims=True) + NORM_EPS) * w


def _tri_masks(n, rev):
    r = lax.broadcasted_iota(jnp.int32, (n, n), 0)
    c = lax.broadcasted_iota(jnp.int32, (n, n), 1)
    if rev:
        return r <= c, r < c
    return r >= c, r > c


def _unit_tri_inv(a):
    n = a.shape[0]
    r = lax.broadcasted_iota(jnp.int32, (n, n), 0)
    c = lax.broadcasted_iota(jnp.int32, (n, n), 1)
    t = jnp.where(r == c, 1.0, 0.0) - jnp.where((r >> 1) == (c >> 1), a, 0.0)
    shift = 1
    while (1 << shift) < n:
        off = jnp.where(((r >> (shift + 1)) == (c >> (shift + 1))) & ((r >> shift) != (c >> shift)), a, 0.0)
        t = t - _mm(t, _mm(off, t))
        shift += 1
    return t


def _shifted_rows(x, prev_row=None, next_row=None):
    n = x.shape[0]
    row = lax.broadcasted_iota(jnp.int32, x.shape, 0)
    xp = jnp.where(row == 0, 0.0 if prev_row is None else prev_row, pltpu.roll(x, 1, 0))
    xn = jnp.where(row == n - 1, 0.0 if next_row is None else next_row, pltpu.roll(x, n - 1, 0))
    return xp, xn


def _conv3(x, w_ref):
    xp, xn = _shifted_rows(x)
    return xp * w_ref[0:1, :] + x * w_ref[1:2, :] + xn * w_ref[2:3, :]


def _mod_kernel(c_ref, w_ref, b_ref, o_ref):
    s = _silu(c_ref[...])
    o_ref[...] = _mm(s, w_ref[...], hi=False) + b_ref[...]


def _modulation(c_rows, mod_w, mod_b):
    depth = mod_w.shape[0]
    nblk = (N_MOD * D_MODEL) // D_MODEL
    out = pl.pallas_call(
        _mod_kernel,
        grid=(depth, nblk),
        in_specs=[
            pl.BlockSpec((MOD_ROWS, D_MODEL), lambda l, j: (0, 0)),
            pl.BlockSpec((None, D_MODEL, D_MODEL), lambda l, j: (l, 0, j)),
            pl.BlockSpec((None, 1, D_MODEL), lambda l, j: (l, 0, j)),
        ],
        out_specs=pl.BlockSpec((None, MOD_ROWS, D_MODEL), lambda l, j: (l, 0, j)),
        out_shape=jax.ShapeDtypeStruct((depth, MOD_ROWS, N_MOD * D_MODEL), F32),
        compiler_params=pltpu.CompilerParams(dimension_semantics=("parallel", "parallel")),
        name="modulation",
    )(c_rows, mod_w, mod_b.reshape(depth, 1, N_MOD * D_MODEL))
    return out.reshape(depth, MOD_ROWS, N_MOD, D_MODEL)


def _mod_spec(layer, tiles_per_seq, row_base, row_step):
    return pl.BlockSpec((None, None, N_MOD, D_MODEL),
                        lambda i: (layer, row_base + (i // tiles_per_seq) * row_step, 0, 0))


def _inproj_kernel(x_ref, mod_ref, nw_ref, w_ref, *o_refs):
    h = _rms(x_ref[...], nw_ref[...])
    h = h * (1.0 + mod_ref[1:2, :]) + mod_ref[0:1, :]
    y = _mm(h, w_ref[...], hi=False)
    off = 0
    for o_ref in o_refs:
        n = o_ref.shape[-1]
        o_ref[...] = y[:, off:off + n]
        off += n


def _inproj(x, mods, norm_w, w_bf16, splits, mod_spec):
    t = x.shape[0]
    n_in = w_bf16.shape[1]
    return pl.pallas_call(
        _inproj_kernel,
        grid=(t // TOKEN_TILE,),
        in_specs=[
            pl.BlockSpec((TOKEN_TILE, D_MODEL), lambda i: (i, 0)),
            mod_spec,
            pl.BlockSpec((1, D_MODEL), lambda i: (0, 0)),
            pl.BlockSpec((D_MODEL, n_in), lambda i: (0, 0)),
        ],
        out_specs=[pl.BlockSpec((TOKEN_TILE, n), lambda i: (i, 0)) for n in splits],
        out_shape=[jax.ShapeDtypeStruct((t, n), F32) for n in splits],
        compiler_params=pltpu.CompilerParams(dimension_semantics=("parallel",), vmem_limit_bytes=VMEM_LIMIT),
        name="inproj",
    )(x, mods, norm_w.reshape(1, D_MODEL), w_bf16)


def _mlp_kernel(a_ref, b_ref, x_ref, mod_ref, nw_ref, woa_ref, wob_ref, w1_ref, w2_ref, nf_ref, o_ref, *, final):
    y = _mm(a_ref[...], woa_ref[...], hi=False) + _mm(b_ref[...], wob_ref[...], hi=False)
    x1 = x_ref[...] + mod_ref[2:3, :] * y
    h = _rms(x1, nw_ref[...])
    h = (h * (1.0 + mod_ref[4:5, :]) + mod_ref[3:4, :]).astype(BF16)
    acc = jnp.zeros(x1.shape, F32)
    for j in range(D_FF // D_MODEL):
        cols = slice(j * D_MODEL, (j + 1) * D_MODEL)
        u = jnp.maximum(_mm(h, w1_ref[:, cols], hi=False), 0.0)
        acc = acc + _mm(u * u, w2_ref[cols, :], hi=False)
    x2 = x1 + mod_ref[5:6, :] * acc
    if final:
        x2 = _rms(x2, nf_ref[...])
    o_ref[...] = x2


def _outproj_mlp(a, b, x, mods, norm_w, w_out, w1, w2, norm_final, mod_spec, final):
    t = x.shape[0]
    half = a.shape[1]
    const = lambda i: (0, 0)
    return pl.pallas_call(
        functools.partial(_mlp_kernel, final=final),
        grid=(t // TOKEN_TILE,),
        in_specs=[
            pl.BlockSpec((TOKEN_TILE, half), lambda i: (i, 0)),
            pl.BlockSpec((TOKEN_TILE, half), lambda i: (i, 0)),
            pl.BlockSpec((TOKEN_TILE, D_MODEL), lambda i: (i, 0)),
            mod_spec,
            pl.BlockSpec((1, D_MODEL), const),
            pl.BlockSpec((half, D_MODEL), const),
            pl.BlockSpec((half, D_MODEL), lambda i: (1, 0)),
            pl.BlockSpec((D_MODEL, D_FF), const),
            pl.BlockSpec((D_FF, D_MODEL), const),
            pl.BlockSpec((1, D_MODEL), const),
        ],
        out_specs=pl.BlockSpec((TOKEN_TILE, D_MODEL), lambda i: (i, 0)),
        out_shape=jax.ShapeDtypeStruct((t, D_MODEL), F32),
        compiler_params=pltpu.CompilerParams(dimension_semantics=("parallel",), vmem_limit_bytes=VMEM_LIMIT),
        name="outproj_mlp",
    )(a, b, x, mods, norm_w.reshape(1, D_MODEL), w_out, w_out, w1, w2, norm_final.reshape(1, D_MODEL))


def _gdn_chunk(q, k, v, beta, g, s, rev):
    c = q.shape[0]
    incl, strict = _tri_masks(c, rev)
    eye = lax.broadcasted_iota(jnp.int32, (c, c), 0) == lax.broadcasted_iota(jnp.int32, (c, c), 1)
    before_col = _tri_masks(c, not rev)[0]
    gc_row = jnp.sum(jnp.where(before_col, jnp.broadcast_to(g, (c, c)), 0.0), axis=0, keepdims=True)
    gc_col = jnp.sum(jnp.where(eye, jnp.broadcast_to(gc_row, (c, c)), 0.0), axis=1, keepdims=True)
    decay = jnp.where(incl, jnp.exp(jnp.where(incl, gc_col - gc_row, 0.0)), 0.0)
    g_tot = jnp.sum(g, axis=0, keepdims=True)
    e_gc = jnp.exp(gc_col)
    kb = k * beta
    a = jnp.where(strict, _mm_nt(kb, k) * decay, 0.0)
    t = _unit_tri_inv(a)
    u = _mm(t, v * beta)
    w = _mm(t, kb * e_gc)
    intra = _mm_nt(q, k) * decay
    e = u - _mm(w, s)
    o = _mm(q * e_gc, s) + _mm(intra, e)
    s_new = s * jnp.exp(g_tot) + _mm_tn(k * jnp.exp(g_tot - gc_col), e)
    return o, s_new


def _gdn_kernel(*refs, seq_len, has_init):
    (q_ref, k_ref, v_ref, z_ref, scb_ref, scc_ref, sch_ref, gate_ref,
     cq_ref, ck_ref, cv_ref, csc_ref, gn_ref, alog_ref, dtb_ref) = refs[:15]
    pos = 15
    if has_init:
        s0f_ref, s0b_ref = refs[pos:pos + 2]
        pos += 2
    o_ref, sco_ref, sf_ref, sb_ref = refs[pos:pos + 4]
    qs, ks, vs, osum, betas, gs = refs[pos + 4:]
    head = pl.program_id(1)
    n_chunks = seq_len // GDN_CHUNK

    def conv_silu(x_ref, c_ref):
        return _silu(_conv3(x_ref[...], c_ref))

    def l2norm(x):
        return x * lax.rsqrt(jnp.sum(x * x, axis=-1, keepdims=True) + 1e-6)

    qs[...] = l2norm(conv_silu(q_ref, cq_ref)) * (GDN_D ** -0.5)
    ks[...] = l2norm(conv_silu(k_ref, ck_ref))
    vs[...] = conv_silu(v_ref, cv_ref)
    gates = gate_ref[...]
    betas[...] = jax.nn.sigmoid(gates)
    gs[...] = -jnp.exp(alog_ref[...]) * _softplus(gates + dtb_ref[...])
    sco_ref[...] = scb_ref[...] * _conv3(scc_ref[...] * sch_ref[...], csc_ref)
    osum[...] = jnp.zeros(osum.shape, F32)

    lane = lax.broadcasted_iota(jnp.int32, (GDN_CHUNK, 128), 1)

    def pick(ref, rows, col):
        return jnp.sum(jnp.where(lane == col, ref[rows, :], 0.0), axis=1, keepdims=True)

    def step(i, carry):
        new = []
        for d, s in enumerate(carry):
            c = i if d == 0 else n_chunks - 1 - i
            rows = pl.ds(pl.multiple_of(c * GDN_CHUNK, GDN_CHUNK), GDN_CHUNK)
            beta = pick(betas, rows, d * GDN_HEADS + head)
            g = pick(gs, rows, 2 * GDN_HEADS + d * GDN_HEADS + head)
            o, s = _gdn_chunk(qs[rows, :], ks[rows, :], vs[rows, :], beta, g, s, rev=(d == 1))
            osum[rows, :] += o
            new.append(s)
        return tuple(new)

    if has_init:
        init = (s0f_ref[...], s0b_ref[...])
    else:
        init = (jnp.zeros((GDN_D, GDN_D), F32), jnp.zeros((GDN_D, GDN_D), F32))
    s_f, s_b = lax.fori_loop(0, n_chunks, step, init)
    sf_ref[...] = s_f
    sb_ref[...] = s_b
    o_ref[...] = _rms(osum[...], gn_ref[...]) * _silu(z_ref[...])


def _gdn_mixer(proj, n_seq, seq_len, conv_w, sc_conv_w, gdn_norm, alog_vec, dtb_vec, s0_f, s0_b):
    has_init = s0_f is not None
    hd = GDN_HEADS

    def col(block):
        return pl.BlockSpec((seq_len, 128), lambda s, h, b=block: (s, b * hd + h))

    def wcol(block):
        return pl.BlockSpec((3, 128), lambda s, h, b=block: (0, b * hd + h))

    vec = pl.BlockSpec((1, 128), lambda s, h: (0, 0))
    state = pl.BlockSpec((None, None, None, GDN_D, GDN_D), lambda s, h: (s, 0, h, 0, 0))
    in_specs = [col(0), col(1), col(2), col(3), col(4), col(5), col(6),
                pl.BlockSpec((seq_len, 128), lambda s, h: (s, 7 * hd)),
                wcol(0), wcol(1), wcol(2), wcol(0), vec, vec, vec]
    args = [proj] * 8 + [conv_w, conv_w, conv_w, sc_conv_w, gdn_norm.reshape(1, 128), alog_vec, dtb_vec]
    if has_init:
        in_specs += [state, state]
        args += [s0_f, s0_b]
    t = n_seq * seq_len
    out_tok = pl.BlockSpec((seq_len, 128), lambda s, h: (s, h))
    return pl.pallas_call(
        functools.partial(_gdn_kernel, seq_len=seq_len, has_init=has_init),
        grid=(n_seq, hd),
        in_specs=in_specs,
        out_specs=[out_tok, out_tok, state, state],
        out_shape=[jax.ShapeDtypeStruct((t, hd * GDN_D), F32), jax.ShapeDtypeStruct((t, SC_WIDTH), F32),
                   jax.ShapeDtypeStruct((n_seq, 1, hd, GDN_D, GDN_D), F32),
                   jax.ShapeDtypeStruct((n_seq, 1, hd, GDN_D, GDN_D), F32)],
        scratch_shapes=[pltpu.VMEM((seq_len, 128), F32) for _ in range(6)],
        compiler_params=pltpu.CompilerParams(dimension_semantics=("parallel", "parallel"),
                                             vmem_limit_bytes=VMEM_LIMIT),
        name="gdn_mixer",
    )(*args)


def _softmax_pv(scores, values, sink):
    m = sink
    for s in scores:
        m = jnp.maximum(m, jnp.max(s, axis=-1, keepdims=True))
    den = jnp.exp(sink - m)
    acc = None
    for s, v in zip(scores, values):
        e = jnp.exp(s - m)
        den = den + jnp.sum(e, axis=-1, keepdims=True)
        pv = _mm(e, v, hi=False)
        acc = pv if acc is None else acc + pv
    return acc / den


def _attn_ctx_kernel(sink_ref, p_ref, o_ref, kc_ref, vc_ref):
    scale = ATT_HD ** -0.5
    for j in range(ATT_KV_HEADS):
        k = p_ref[:, ATT_Q_W + j * ATT_HD:ATT_Q_W + (j + 1) * ATT_HD]
        v = p_ref[:, ATT_Q_W + ATT_KV_W + j * ATT_HD:ATT_Q_W + ATT_KV_W + (j + 1) * ATT_HD]
        kc_ref[j] = k
        vc_ref[j] = v
        for gi in range(ATT_GROUP):
            hh = j * ATT_GROUP + gi
            q = p_ref[:, hh * ATT_HD:(hh + 1) * ATT_HD]
            s = _mm_nt(q, k, hi=False) * scale
            sink = jnp.full((1, 1), sink_ref[hh], F32)
            o_ref[:, hh * ATT_HD:(hh + 1) * ATT_HD] = _softmax_pv([s], [v], sink)


def _attn_context(proj_att, sink, n_seq, seq_len):
    kv = pl.BlockSpec((None, None, ATT_KV_HEADS, seq_len, ATT_HD), lambda b: (b, 0, 0, 0, 0))
    return pl.pallas_call(
        _attn_ctx_kernel,
        grid=(n_seq,),
        in_specs=[pl.BlockSpec(memory_space=pltpu.SMEM),
                  pl.BlockSpec((seq_len, ATT_W), lambda b: (b, 0))],
        out_specs=[pl.BlockSpec((seq_len, ATT_Q_W), lambda b: (b, 0)), kv, kv],
        out_shape=[jax.ShapeDtypeStruct((n_seq * seq_len, ATT_Q_W), F32),
                   jax.ShapeDtypeStruct((n_seq, 1, ATT_KV_HEADS, seq_len, ATT_HD), F32),
                   jax.ShapeDtypeStruct((n_seq, 1, ATT_KV_HEADS, seq_len, ATT_HD), F32)],
        compiler_params=pltpu.CompilerParams(dimension_semantics=("parallel",)),
        name="attn_context",
    )(sink, proj_att)


def _rope_tables(seq_len):
    pos = np.arange(seq_len)
    half = ATT_HD // 2
    inv = ROPE_BASE ** (-np.arange(0, half, 2, dtype=np.float32) / half)
    ang_r = (pos // GRID_W).astype(np.float32)[:, None] * inv
    ang_c = (pos % GRID_W).astype(np.float32)[:, None] * inv
    cos = np.concatenate([np.cos(ang_r), np.cos(ang_r), np.cos(ang_c), np.cos(ang_c)], axis=1)
    sin = np.concatenate([-np.sin(ang_r), np.sin(ang_r), -np.sin(ang_c), np.sin(ang_c)], axis=1)
    return (jnp.asarray(np.tile(cos, (1, 2)), F32), jnp.asarray(np.tile(sin, (1, 2)), F32))


def _rope(x, cos, sin):
    lane = lax.broadcasted_iota(jnp.int32, x.shape, 1)
    partner = jnp.where((lane & 31) < 16, pltpu.roll(x, 128 - 16, 1), pltpu.roll(x, 16, 1))
    return x * cos + partner * sin


def _attn_lat_kernel(sink_ref, p_ref, ck_ref, cv_ref, cos_ref, sin_ref, o_ref, *, seq_len):
    scale = ATT_HD ** -0.5
    qb = pl.program_id(1)
    span = 3 * ATT_BLOCK
    q0 = pl.multiple_of(qb * ATT_BLOCK, ATT_BLOCK)
    k0 = pl.multiple_of(jnp.clip((qb - 1) * ATT_BLOCK, 0, seq_len - span), ATT_BLOCK)
    qrows = pl.ds(q0, ATT_BLOCK)
    krows = pl.ds(k0, span)
    kwin = _rope(p_ref[krows, ATT_Q_W:ATT_Q_W + ATT_KV_W], cos_ref[krows, :], sin_ref[krows, :])
    vwin = p_ref[krows, ATT_Q_W + ATT_KV_W:ATT_W]
    qpos = q0 + lax.broadcasted_iota(jnp.int32, (ATT_BLOCK, span), 0)
    kpos = k0 + lax.broadcasted_iota(jnp.int32, (ATT_BLOCK, span), 1)
    valid = jnp.abs(qpos - kpos) <= WINDOW
    cos_q = cos_ref[qrows, :]
    sin_q = sin_ref[qrows, :]
    for pair in range(ATT_HEADS // 2):
        qpair = _rope(p_ref[qrows, pair * 128:(pair + 1) * 128], cos_q, sin_q)
        for sub in range(2):
            hh = pair * 2 + sub
            j = hh // ATT_GROUP
            q = qpair[:, sub * ATT_HD:(sub + 1) * ATT_HD]
            s_loc = _mm_nt(q, kwin[:, j * ATT_HD:(j + 1) * ATT_HD], hi=False) * scale
            s_loc = jnp.where(valid, s_loc, NEG_INF)
            s_ctx = _mm_nt(q, ck_ref[j], hi=False) * scale
            sink = jnp.full((1, 1), sink_ref[hh], F32)
            o_ref[:, hh * ATT_HD:(hh + 1) * ATT_HD] = _softmax_pv(
                [s_loc, s_ctx], [vwin[:, j * ATT_HD:(j + 1) * ATT_HD], cv_ref[j]], sink)


def _attn_latent(proj_att, sink, cache_k, cache_v, layer, n_seq, seq_len):
    cos, sin = _rope_tables(seq_len)
    past = cache_k.shape[3]
    nqb = seq_len // ATT_BLOCK
    cache = pl.BlockSpec((None, None, ATT_KV_HEADS, past, ATT_HD), lambda b, q: (b, layer, 0, 0, 0))
    table = pl.BlockSpec((seq_len, 128), lambda b, q: (0, 0))
    return pl.pallas_call(
        functools.partial(_attn_lat_kernel, seq_len=seq_len),
        grid=(n_seq, nqb),
        in_specs=[pl.BlockSpec(memory_space=pltpu.SMEM),
                  pl.BlockSpec((seq_len, ATT_W), lambda b, q: (b, 0)),
                  cache, cache, table, table],
        out_specs=pl.BlockSpec((ATT_BLOCK, ATT_Q_W), lambda b, q: (b * nqb + q, 0)),
        out_shape=jax.ShapeDtypeStruct((n_seq * seq_len, ATT_Q_W), F32),
        compiler_params=pltpu.CompilerParams(dimension_semantics=("parallel", "parallel")),
        name="attn_latent",
    )(sink, proj_att, cache_k, cache_v, cos, sin)


def _rwkv_dir_chunk(x_ref, prm, st_ref, ysum, bon, gate, c, d, *, seq_len):
    (mu_ref, w0_ref, wup_ref, a0_ref, aup_ref, gup_ref, kk_ref, ka_ref, rk_ref) = prm
    ch = RWKV_CHUNK
    n_chunks = seq_len // ch
    hd = RWKV_HD
    rev = d == 1
    r0 = pl.multiple_of(c * ch, ch)
    rows = pl.ds(r0, ch)
    x = x_ref[rows, :]
    prev_row = x_ref[pl.ds(jnp.maximum(r0 - 1, 0), 1), :] * (c > 0).astype(F32)
    next_row = x_ref[pl.ds(jnp.minimum(r0 + ch, seq_len - 1), 1), :] * (c < n_chunks - 1).astype(F32)
    xp, xn = _shifted_rows(x, prev_row, next_row)
    xs = x + mu_ref[0:1, :] * (xp - x) + mu_ref[1:2, :] * (xn - x)
    r = xs[:, 0:RWKV_W]
    k = xs[:, RWKV_W:2 * RWKV_W]
    v = xs[:, 2 * RWKV_W:3 * RWKV_W]
    lo = 3 * RWKV_W
    wl = xs[:, lo + d * RWKV_LORA:lo + (d + 1) * RWKV_LORA]
    al = xs[:, lo + 2 * RWKV_LORA + d * RWKV_LORA:lo + 2 * RWKV_LORA + (d + 1) * RWKV_LORA]
    w_log = -_softplus(-(w0_ref[d:d + 1, :] + _mm(jnp.tanh(wl), wup_ref[d]))) - 0.5
    lw = -jnp.exp(w_log)
    a = jax.nn.sigmoid(a0_ref[d:d + 1, :] + _mm(al, aup_ref[d]))
    k2 = k * (1.0 + (a - 1.0) * ka_ref[...])
    kkv = k * kk_ref[...]
    if d == 0:
        gl = xs[:, lo + 4 * RWKV_LORA:lo + 6 * RWKV_LORA]
        gate[rows, :] = _mm(jax.nn.sigmoid(gl), gup_ref[...])
    incl, strict = _tri_masks(ch, rev)
    g_inc = _mm(jnp.where(incl, 1.0, 0.0), lw)
    g_exc = g_inc - lw
    g_tot = jnp.sum(lw, axis=0, keepdims=True)
    r_dec = r * jnp.exp(g_inc)
    e_neg = jnp.exp(-g_inc)
    e_end = jnp.exp(g_tot - g_inc)
    e_exc = jnp.exp(g_exc)
    k_neg = k2 * e_neg
    k_end = k2 * e_end
    dec_tot = jnp.exp(g_tot)
    rkr = r * k2 * rk_ref[...]
    for h in range(RWKV_HEADS):
        hs = slice(h * hd, (h + 1) * hd)
        kk_h = kkv[:, hs]
        kap = kk_h * lax.rsqrt(jnp.sum(kk_h * kk_h, axis=-1, keepdims=True) + 1e-6)
        b_h = kap * a[:, hs]
        kap_dec = kap * e_exc[:, hs]
        b_neg = b_h * e_neg[:, hs]
        b_end = b_h * e_end[:, hs]
        v_h = v[:, hs]
        left = jnp.concatenate([kap_dec, r_dec[:, hs]], axis=0)
        m_b = _mm_nt(left, b_neg)
        m_k = _mm_nt(left, k_neg[:, hs])
        a_ab = jnp.where(strict, m_b[:ch], 0.0)
        a_ak = jnp.where(strict, m_k[:ch], 0.0)
        a_rb = jnp.where(incl, m_b[ch:], 0.0)
        a_rk = jnp.where(incl, m_k[ch:], 0.0)
        t_inv = _unit_tri_inv(a_ab)
        s = st_ref[d, h]
        ls = _mm_nt(left, s)
        u = -_mm(t_inv, ls[:ch] + _mm(a_ak, v_h))
        y_h = ls[ch:] + _mm(a_rb, u) + _mm(a_rk, v_h)
        st_ref[d, h] = s * dec_tot[:, hs] + _mm_tn(u, b_end) + _mm_tn(v_h, k_end[:, hs])
        ysum[rows, hs] += y_h
        bon[rows, hs] += jnp.sum(rkr[:, hs], axis=-1, keepdims=True) * v_h


def _rwkv_kernel(*refs, seq_len, has_init):
    x_ref = refs[0]
    prm = refs[1:10]
    lnw_ref, lnb_ref = refs[10:12]
    pos = 12
    if has_init:
        s0f_ref, s0b_ref = refs[pos:pos + 2]
        pos += 2
    o_ref, sf_ref, sb_ref = refs[pos:pos + 3]
    ysum, bon, gate, st_ref = refs[pos + 3:]
    ch = RWKV_CHUNK
    n_chunks = seq_len // ch
    ysum[...] = jnp.zeros(ysum.shape, F32)
    bon[...] = jnp.zeros(bon.shape, F32)
    if has_init:
        st_ref[0] = s0f_ref[...]
        st_ref[1] = s0b_ref[...]
    else:
        st_ref[...] = jnp.zeros(st_ref.shape, F32)

    def step(i, carry):
        _rwkv_dir_chunk(x_ref, prm, st_ref, ysum, bon, gate, i, 0, seq_len=seq_len)
        _rwkv_dir_chunk(x_ref, prm, st_ref, ysum, bon, gate, n_chunks - 1 - i, 1, seq_len=seq_len)
        return carry

    lax.fori_loop(0, n_chunks, step, 0)
    sf_ref[...] = st_ref[0]
    sb_ref[...] = st_ref[1]

    def finish(i, carry):
        rows = pl.ds(pl.multiple_of(i * ch, ch), ch)
        y = ysum[rows, :]
        parts = []
        for h in range(RWKV_HEADS):
            y_h = y[:, h * RWKV_HD:(h + 1) * RWKV_HD]
            mean = jnp.mean(y_h, axis=-1, keepdims=True)
            cen = y_h - mean
            var = jnp.mean(cen * cen, axis=-1, keepdims=True)
            parts.append(cen * lax.rsqrt(var + GN_EPS))
        yn = jnp.concatenate(parts, axis=1) * lnw_ref[...] + lnb_ref[...]
        o_ref[rows, :] = (yn + bon[rows, :]) * gate[rows, :]
        return carry

    lax.fori_loop(0, n_chunks, finish, 0)


def _rwkv_mixer(x_rw, n_seq, seq_len, params, s0_f, s0_b):
    has_init = s0_f is not None
    (mu, w0, w_up, a0, a_up, g_up, k_k, k_a, r_k, ln_w, ln_b) = params
    row = lambda a: a.reshape(1, RWKV_W)
    args = [x_rw, mu, w0, w_up, a0, a_up, g_up, row(k_k), row(k_a), row(r_k), row(ln_w), row(ln_b)]

    def whole(a):
        nd = a.ndim
        return pl.BlockSpec(a.shape, lambda s, nd=nd: (0,) * nd)

    in_specs = [pl.BlockSpec((seq_len, RWKV_IN), lambda s: (s, 0))] + [whole(a) for a in args[1:]]
    state = pl.BlockSpec((None, None, RWKV_HEADS, RWKV_HD, RWKV_HD), lambda s: (s, 0, 0, 0, 0))
    if has_init:
        in_specs += [state, state]
        args += [s0_f, s0_b]
    st_shape = jax.ShapeDtypeStruct((n_seq, 1, RWKV_HEADS, RWKV_HD, RWKV_HD), F32)
    return pl.pallas_call(
        functools.partial(_rwkv_kernel, seq_len=seq_len, has_init=has_init),
        grid=(n_seq,),
        in_specs=in_specs,
        out_specs=[pl.BlockSpec((seq_len, RWKV_W), lambda s: (s, 0)), state, state],
        out_shape=[jax.ShapeDtypeStruct((n_seq * seq_len, RWKV_W), F32), st_shape, st_shape],
        scratch_shapes=[pltpu.VMEM((seq_len, RWKV_W), F32) for _ in range(3)]
        + [pltpu.VMEM((2, RWKV_HEADS, RWKV_HD, RWKV_HD), F32)],
        compiler_params=pltpu.CompilerParams(dimension_semantics=("parallel",), vmem_limit_bytes=VMEM_LIMIT),
        name="rwkv_mixer",
    )(*args)


def kernel(x_prompt, x_sample, state_gdn_fwd, state_gdn_bwd, cache_attn_k, cache_attn_v, state_rwkv_fwd, state_rwkv_bwd, c, c_ctx, mod_w, mod_b, norm_mix, norm_mlp, mlp_w1, mlp_w2, norm_final, ev_w_in, ev_w_out, gdn_conv, gdn_a_log, gdn_dt_bias, gdn_norm, sc_conv, od_w_in, od_w_out, attn_sink, rwkv_mu, rwkv_w0, rwkv_w_up, rwkv_a0, rwkv_a_up, rwkv_g_up, rwkv_k_k, rwkv_k_a, rwkv_r_k, rwkv_ln_w, rwkv_ln_b):
    bp, lp, _ = x_prompt.shape
    bs, ls, _ = x_sample.shape
    depth = mod_w.shape[0]
    c_rows = jnp.concatenate([c_ctx[None, :], c, jnp.zeros((MOD_ROWS - 1 - bs, D_MODEL), F32)], axis=0)
    mods = _modulation(c_rows, mod_w, mod_b)

    groups = [
        dict(x=x_prompt.reshape(bp * lp, D_MODEL), n=bp, l=lp, latent=False,
             mod=lambda layer: _mod_spec(layer, lp // TOKEN_TILE, 0, 0)),
        dict(x=x_sample.reshape(bs * ls, D_MODEL), n=bs, l=ls, latent=True,
             mod=lambda layer: _mod_spec(layer, ls // TOKEN_TILE, 1, 1)),
    ]
    outs = {}
    for layer in range(depth):
        w1 = mlp_w1[layer].astype(BF16)
        w2 = mlp_w2[layer].astype(BF16)
        final = layer == depth - 1
        if layer % 2 == 0:
            e = layer // 2
            w = ev_w_in[e]
            qkvz = GDN_QKV_W + GDN_HEADS * GDN_D
            n_gate = 4 * GDN_HEADS
            w_in = jnp.concatenate(
                [w[:, :qkvz], w[:, qkvz + n_gate:], w[:, qkvz:qkvz + n_gate],
                 jnp.zeros((D_MODEL, EV_IN_PAD - w.shape[1]), F32)], axis=1).astype(BF16)
            w_out = ev_w_out[e].astype(BF16)
            alog_vec = jnp.zeros((1, 128), F32).at[0, 2 * GDN_HEADS:4 * GDN_HEADS].set(gdn_a_log[e].reshape(-1))
            dtb_vec = jnp.zeros((1, 128), F32).at[0, 2 * GDN_HEADS:4 * GDN_HEADS].set(gdn_dt_bias[e].reshape(-1))
            for grp in groups:
                (proj,) = _inproj(grp["x"], mods, norm_mix[layer], w_in, (EV_IN_PAD,), grp["mod"](layer))
                s0 = (state_gdn_fwd[:, e:e + 1], state_gdn_bwd[:, e:e + 1]) if grp["latent"] else (None, None)
                o, sc, s_f, s_b = _gdn_mixer(proj, grp["n"], grp["l"], gdn_conv[e], sc_conv[e], gdn_norm[e],
                                             alog_vec, dtb_vec, *s0)
                if not grp["latent"]:
                    outs.setdefault("gdn_f", []).append(s_f)
                    outs.setdefault("gdn_b", []).append(s_b)
                grp["x"] = _outproj_mlp(o, sc, grp["x"], mods, norm_mlp[layer], w_out, w1, w2, norm_final,
                                        grp["mod"](layer), final)
        else:
            o_ = layer // 2
            w = od_w_in[o_]
            w_in = jnp.concatenate([w[:, ATT_W:], w[:, :ATT_W]], axis=1).astype(BF16)
            w_out = od_w_out[o_].astype(BF16)
            rw = (rwkv_mu[o_], rwkv_w0[o_], rwkv_w_up[o_], rwkv_a0[o_], rwkv_a_up[o_], rwkv_g_up[o_],
                  rwkv_k_k[o_], rwkv_k_a[o_], rwkv_r_k[o_].reshape(-1), rwkv_ln_w[o_], rwkv_ln_b[o_])
            for grp in groups:
                x_rw, p_att = _inproj(grp["x"], mods, norm_mix[layer], w_in, (RWKV_IN, ATT_W), grp["mod"](layer))
                if grp["latent"]:
                    att = _attn_latent(p_att, attn_sink[o_], cache_attn_k, cache_attn_v, o_, grp["n"], grp["l"])
                    rwo, _, _ = _rwkv_mixer(x_rw, grp["n"], grp["l"], rw,
                                            state_rwkv_fwd[:, o_:o_ + 1], state_rwkv_bwd[:, o_:o_ + 1])
                else:
                    att, kc, vc = _attn_context(p_att, attn_sink[o_], grp["n"], grp["l"])
                    rwo, s_f, s_b = _rwkv_mixer(x_rw, grp["n"], grp["l"], rw, None, None)
                    outs.setdefault("att_k", []).append(kc)
                    outs.setdefault("att_v", []).append(vc)
                    outs.setdefault("rw_f", []).append(s_f)
                    outs.setdefault("rw_b", []).append(s_b)
                grp["x"] = _outproj_mlp(att, rwo, grp["x"], mods, norm_mlp[layer], w_out, w1, w2, norm_final,
                                        grp["mod"](layer), final)
    cat = lambda key: jnp.concatenate(outs[key], axis=1)
    return (groups[0]["x"].reshape(bp, lp, D_MODEL), groups[1]["x"].reshape(bs, ls, D_MODEL),
            cat("gdn_f"), cat("gdn_b"), cat("att_k"), cat("att_v"), cat("rw_f"), cat("rw_b"))
```

```python
import functools

import jax
import jax.numpy as jnp
import numpy as np
from jax import lax
from jax.experimental import pallas as pl
from jax.experimental.pallas import tpu as pltpu

F32 = jnp.float32
BF16 = jnp.bfloat16
HIGHEST = lax.Precision.HIGHEST

D_MODEL = 1024
N_MOD = 6
D_FF = 4 * D_MODEL
NORM_EPS = 1e-6
TOKEN_TILE = 256
MOD_ROWS = 8

GDN_HEADS = 4
GDN_D = 128
GDN_CHUNK = 64
GDN_QKV_W = 3 * GDN_HEADS * GDN_D
SC_WIDTH = 512
EV_IN_PAD = 3712

ATT_HEADS = 8
ATT_KV_HEADS = 2
ATT_GROUP = ATT_HEADS // ATT_KV_HEADS
ATT_HD = 64
ATT_Q_W = ATT_HEADS * ATT_HD
ATT_KV_W = ATT_KV_HEADS * ATT_HD
ATT_W = ATT_Q_W + 2 * ATT_KV_W
WINDOW = 128
ATT_BLOCK = 128
GRID_W = 64
ROPE_BASE = 10000.0
NEG_INF = -1e30

RWKV_HEADS = 8
RWKV_HD = 64
RWKV_W = RWKV_HEADS * RWKV_HD
RWKV_LORA = 64
RWKV_IN = 3 * RWKV_W + 3 * 2 * RWKV_LORA
RWKV_CHUNK = 64
GN_EPS = 64e-5

VMEM_LIMIT = 56 * 1024 * 1024

GDN_PREC = dict(kk="bf", qk="bf", inv="bf", solve="bf", state="bf", out="bf", update="bf")
RWKV_PREC = dict(lora="bf", gate="bf", cumsum="x3", amat="bf", inv="bf", state="bf", solve="x3", out="bf",
                 update="bf")


def _split_bf16(a):
    hi = a.astype(BF16)
    return hi, (a - hi.astype(F32)).astype(BF16)


def _dot(a, b, dims, prec):
    dn = (dims, ((), ()))
    if prec == "hi":
        return lax.dot_general(a, b, dn, precision=HIGHEST, preferred_element_type=F32)
    if prec == "x3":
        ah, al = _split_bf16(a)
        bh, bl = _split_bf16(b)
        one = lambda x, y: lax.dot_general(x, y, dn, preferred_element_type=F32)
        return one(ah, bh) + one(ah, bl) + one(al, bh)
    assert prec == "bf", prec
    return lax.dot_general(a.astype(BF16), b.astype(BF16), dn, preferred_element_type=F32)


def _mm(a, b, prec="hi"):
    return _dot(a, b, ((1,), (0,)), prec)


def _mm_nt(a, b, prec="hi"):
    return _dot(a, b, ((1,), (1,)), prec)


def _mm_tn(a, b, prec="hi"):
    return _dot(a, b, ((0,), (0,)), prec)


def _silu(x):
    return x * jax.nn.sigmoid(x)


def _softplus(x):
    return jnp.maximum(x, 0.0) + jnp.log1p(jnp.exp(-jnp.abs(x)))


def _rms(x, w):
    return x * lax.rsqrt(jnp.mean(x * x, axis=-1, keepdims=True) + NORM_EPS) * w


def _tri_masks(n, rev):
    r = lax.broadcasted_iota(jnp.int32, (n, n), 0)
    c = lax.broadcasted_iota(jnp.int32, (n, n), 1)
    if rev:
        return r <= c, r < c
    return r >= c, r > c


def _unit_tri_inv_many(mats, prec):
    n = mats[0].shape[0]
    r = lax.broadcasted_iota(jnp.int32, (n, n), 0)
    c = lax.broadcasted_iota(jnp.int32, (n, n), 1)
    eye = jnp.where(r == c, 1.0, 0.0)
    ts = [eye - jnp.where((r >> 1) == (c >> 1), a, 0.0) for a in mats]
    shift = 1
    while (1 << shift) < n:
        join = ((r >> (shift + 1)) == (c >> (shift + 1))) & ((r >> shift) != (c >> shift))
        inner = [_mm(jnp.where(join, a, 0.0), t, prec) for a, t in zip(mats, ts)]
        ts = [t - _mm(t, w, prec) for t, w in zip(ts, inner)]
        shift += 1
    return ts


def _unit_tri_inv(a, prec):
    return _unit_tri_inv_many([a], prec)[0]


def _shifted_rows(x, prev_row=None, next_row=None):
    n = x.shape[0]
    row = lax.broadcasted_iota(jnp.int32, x.shape, 0)
    xp = jnp.where(row == 0, 0.0 if prev_row is None else prev_row, pltpu.roll(x, 1, 0))
    xn = jnp.where(row == n - 1, 0.0 if next_row is None else next_row, pltpu.roll(x, n - 1, 0))
    return xp, xn


def _conv3(x, w_ref):
    xp, xn = _shifted_rows(x)
    return xp * w_ref[0:1, :] + x * w_ref[1:2, :] + xn * w_ref[2:3, :]


def _mod_kernel(c_ref, w_ref, b_ref, o_ref):
    s = _silu(c_ref[...])
    o_ref[...] = _mm(s, w_ref[...], prec="bf") + b_ref[...]


def _modulation(c_rows, mod_w, mod_b):
    depth = mod_w.shape[0]
    nblk = (N_MOD * D_MODEL) // D_MODEL
    out = pl.pallas_call(
        _mod_kernel,
        grid=(depth, nblk),
        in_specs=[
            pl.BlockSpec((MOD_ROWS, D_MODEL), lambda l, j: (0, 0)),
            pl.BlockSpec((None, D_MODEL, D_MODEL), lambda l, j: (l, 0, j)),
            pl.BlockSpec((None, 1, D_MODEL), lambda l, j: (l, 0, j)),
        ],
        out_specs=pl.BlockSpec((None, MOD_ROWS, D_MODEL), lambda l, j: (l, 0, j)),
        out_shape=jax.ShapeDtypeStruct((depth, MOD_ROWS, N_MOD * D_MODEL), F32),
        compiler_params=pltpu.CompilerParams(dimension_semantics=("parallel", "parallel")),
        name="modulation",
    )(c_rows, mod_w, mod_b.reshape(depth, 1, N_MOD * D_MODEL))
    return out.reshape(depth, MOD_ROWS, N_MOD, D_MODEL)


def _mod_spec(layer, tiles_per_seq, row_base, row_step):
    return pl.BlockSpec((None, None, N_MOD, D_MODEL),
                        lambda i: (layer, row_base + (i // tiles_per_seq) * row_step, 0, 0))


def _inproj_kernel(x_ref, mod_ref, nw_ref, w_ref, *o_refs):
    h = _rms(x_ref[...], nw_ref[...])
    h = h * (1.0 + mod_ref[1:2, :]) + mod_ref[0:1, :]
    y = _mm(h, w_ref[...], prec="bf")
    off = 0
    for o_ref in o_refs:
        n = o_ref.shape[-1]
        o_ref[...] = y[:, off:off + n]
        off += n


def _inproj(x, mods, norm_w, w_bf16, splits, mod_spec):
    t = x.shape[0]
    n_in = w_bf16.shape[1]
    return pl.pallas_call(
        _inproj_kernel,
        grid=(t // TOKEN_TILE,),
        in_specs=[
            pl.BlockSpec((TOKEN_TILE, D_MODEL), lambda i: (i, 0)),
            mod_spec,
            pl.BlockSpec((1, D_MODEL), lambda i: (0, 0)),
            pl.BlockSpec((D_MODEL, n_in), lambda i: (0, 0)),
        ],
        out_specs=[pl.BlockSpec((TOKEN_TILE, n), lambda i: (i, 0)) for n in splits],
        out_shape=[jax.ShapeDtypeStruct((t, n), F32) for n in splits],
        compiler_params=pltpu.CompilerParams(dimension_semantics=("parallel",), vmem_limit_bytes=VMEM_LIMIT),
        name="inproj",
    )(x, mods, norm_w.reshape(1, D_MODEL), w_bf16)


def _mlp_kernel(a_ref, b_ref, x_ref, mod_ref, nw_ref, woa_ref, wob_ref, w1_ref, w2_ref, nf_ref, o_ref, *, final):
    y = _mm(a_ref[...], woa_ref[...], prec="bf") + _mm(b_ref[...], wob_ref[...], prec="bf")
    x1 = x_ref[...] + mod_ref[2:3, :] * y
    h = _rms(x1, nw_ref[...])
    h = (h * (1.0 + mod_ref[4:5, :]) + mod_ref[3:4, :]).astype(BF16)
    acc = jnp.zeros(x1.shape, F32)
    for j in range(D_FF // D_MODEL):
        cols = slice(j * D_MODEL, (j + 1) * D_MODEL)
        u = jnp.maximum(_mm(h, w1_ref[:, cols], prec="bf"), 0.0)
        acc = acc + _mm(u * u, w2_ref[cols, :], prec="bf")
    x2 = x1 + mod_ref[5:6, :] * acc
    if final:
        x2 = _rms(x2, nf_ref[...])
    o_ref[...] = x2


def _outproj_mlp(a, b, x, mods, norm_w, w_out, w1, w2, norm_final, mod_spec, final):
    t = x.shape[0]
    half = a.shape[1]
    const = lambda i: (0, 0)
    return pl.pallas_call(
        functools.partial(_mlp_kernel, final=final),
        grid=(t // TOKEN_TILE,),
        in_specs=[
            pl.BlockSpec((TOKEN_TILE, half), lambda i: (i, 0)),
            pl.BlockSpec((TOKEN_TILE, half), lambda i: (i, 0)),
            pl.BlockSpec((TOKEN_TILE, D_MODEL), lambda i: (i, 0)),
            mod_spec,
            pl.BlockSpec((1, D_MODEL), const),
            pl.BlockSpec((half, D_MODEL), const),
            pl.BlockSpec((half, D_MODEL), lambda i: (1, 0)),
            pl.BlockSpec((D_MODEL, D_FF), const),
            pl.BlockSpec((D_FF, D_MODEL), const),
            pl.BlockSpec((1, D_MODEL), const),
        ],
        out_specs=pl.BlockSpec((TOKEN_TILE, D_MODEL), lambda i: (i, 0)),
        out_shape=jax.ShapeDtypeStruct((t, D_MODEL), F32),
        compiler_params=pltpu.CompilerParams(dimension_semantics=("parallel",), vmem_limit_bytes=VMEM_LIMIT),
        name="outproj_mlp",
    )(a, b, x, mods, norm_w.reshape(1, D_MODEL), w_out, w_out, w1, w2, norm_final.reshape(1, D_MODEL))


def _gdn_chunk_operands(q, k, v, beta, g, s, rev):
    c = q.shape[0]
    incl, strict = _tri_masks(c, rev)
    eye = lax.broadcasted_iota(jnp.int32, (c, c), 0) == lax.broadcasted_iota(jnp.int32, (c, c), 1)
    before_col = _tri_masks(c, not rev)[0]
    gc_row = jnp.sum(jnp.where(before_col, jnp.broadcast_to(g, (c, c)), 0.0), axis=0, keepdims=True)
    gc_col = jnp.sum(jnp.where(eye, jnp.broadcast_to(gc_row, (c, c)), 0.0), axis=1, keepdims=True)
    decay = jnp.where(incl, jnp.exp(jnp.where(incl, gc_col - gc_row, 0.0)), 0.0)
    g_tot = jnp.sum(g, axis=0, keepdims=True)
    e_gc = jnp.exp(gc_col)
    return dict(q=q, k=k, s=s, strict=strict, decay=decay, kb=k * beta, vb=v * beta, e_gc=e_gc,
                g_end=jnp.exp(g_tot), k_dec=k * jnp.exp(g_tot - gc_col))


def _gdn_chunks(cs):
    p = GDN_PREC
    a_s = [jnp.where(c["strict"], _mm_nt(c["kb"], c["k"], p["kk"]) * c["decay"], 0.0) for c in cs]
    ts = _unit_tri_inv_many(a_s, p["inv"])
    us = [_mm(t, c["vb"], p["solve"]) for c, t in zip(cs, ts)]
    ws = [_mm(t, c["kb"] * c["e_gc"], p["solve"]) for c, t in zip(cs, ts)]
    intras = [_mm_nt(c["q"], c["k"], p["qk"]) * c["decay"] for c in cs]
    es = [u - _mm(w, c["s"], p["state"]) for c, u, w in zip(cs, us, ws)]
    os_ = [_mm(c["q"] * c["e_gc"], c["s"], p["state"]) + _mm(intra, e, p["out"])
           for c, intra, e in zip(cs, intras, es)]
    s_news = [c["s"] * c["g_end"] + _mm_tn(c["k_dec"], e, p["update"]) for c, e in zip(cs, es)]
    return list(zip(os_, s_news))


def _gdn_kernel(*refs, seq_len, has_init):
    (q_ref, k_ref, v_ref, z_ref, scb_ref, scc_ref, sch_ref, gate_ref,
     cq_ref, ck_ref, cv_ref, csc_ref, gn_ref, alog_ref, dtb_ref) = refs[:15]
    pos = 15
    if has_init:
        s0f_ref, s0b_ref = refs[pos:pos + 2]
        pos += 2
    o_ref, sco_ref, sf_ref, sb_ref = refs[pos:pos + 4]
    qs, ks, vs, osum, betas, gs = refs[pos + 4:]
    head = pl.program_id(1)
    n_chunks = seq_len // GDN_CHUNK

    def conv_silu(x_ref, c_ref):
        return _silu(_conv3(x_ref[...], c_ref))

    def l2norm(x):
        return x * lax.rsqrt(jnp.sum(x * x, axis=-1, keepdims=True) + 1e-6)

    qs[...] = l2norm(conv_silu(q_ref, cq_ref)) * (GDN_D ** -0.5)
    ks[...] = l2norm(conv_silu(k_ref, ck_ref))
    vs[...] = conv_silu(v_ref, cv_ref)
    gates = gate_ref[...]
    betas[...] = jax.nn.sigmoid(gates)
    gs[...] = -jnp.exp(alog_ref[...]) * _softplus(gates + dtb_ref[...])
    sco_ref[...] = scb_ref[...] * _conv3(scc_ref[...] * sch_ref[...], csc_ref)
    osum[...] = jnp.zeros(osum.shape, F32)

    lane = lax.broadcasted_iota(jnp.int32, (GDN_CHUNK, 128), 1)

    def pick(ref, rows, col):
        return jnp.sum(jnp.where(lane == col, ref[rows, :], 0.0), axis=1, keepdims=True)

    def step(i, carry):
        chains, where = [], []
        for d, s in enumerate(carry):
            c = i if d == 0 else n_chunks - 1 - i
            rows = pl.ds(pl.multiple_of(c * GDN_CHUNK, GDN_CHUNK), GDN_CHUNK)
            beta = pick(betas, rows, d * GDN_HEADS + head)
            g = pick(gs, rows, 2 * GDN_HEADS + d * GDN_HEADS + head)
            chains.append(_gdn_chunk_operands(qs[rows, :], ks[rows, :], vs[rows, :], beta, g, s, rev=(d == 1)))
            where.append(rows)
        new = []
        for rows, (o, s) in zip(where, _gdn_chunks(chains)):
            osum[rows, :] += o
            new.append(s)
        return tuple(new)

    if has_init:
        init = (s0f_ref[...], s0b_ref[...])
    else:
        init = (jnp.zeros((GDN_D, GDN_D), F32), jnp.zeros((GDN_D, GDN_D), F32))
    s_f, s_b = lax.fori_loop(0, n_chunks, step, init)
    sf_ref[...] = s_f
    sb_ref[...] = s_b
    o_ref[...] = _rms(osum[...], gn_ref[...]) * _silu(z_ref[...])


def _gdn_mixer(proj, n_seq, seq_len, conv_w, sc_conv_w, gdn_norm, alog_vec, dtb_vec, s0_f, s0_b):
    has_init = s0_f is not None
    hd = GDN_HEADS

    def col(block):
        return pl.BlockSpec((seq_len, 128), lambda s, h, b=block: (s, b * hd + h))

    def wcol(block):
        return pl.BlockSpec((3, 128), lambda s, h, b=block: (0, b * hd + h))

    vec = pl.BlockSpec((1, 128), lambda s, h: (0, 0))
    state = pl.BlockSpec((None, None, None, GDN_D, GDN_D), lambda s, h: (s, 0, h, 0, 0))
    in_specs = [col(0), col(1), col(2), col(3), col(4), col(5), col(6),
                pl.BlockSpec((seq_len, 128), lambda s, h: (s, 7 * hd)),
                wcol(0), wcol(1), wcol(2), wcol(0), vec, vec, vec]
    args = [proj] * 8 + [conv_w, conv_w, conv_w, sc_conv_w, gdn_norm.reshape(1, 128), alog_vec, dtb_vec]
    if has_init:
        in_specs += [state, state]
        args += [s0_f, s0_b]
    t = n_seq * seq_len
    out_tok = pl.BlockSpec((seq_len, 128), lambda s, h: (s, h))
    return pl.pallas_call(
        functools.partial(_gdn_kernel, seq_len=seq_len, has_init=has_init),
        grid=(n_seq, hd),
        in_specs=in_specs,
        out_specs=[out_tok, out_tok, state, state],
        out_shape=[jax.ShapeDtypeStruct((t, hd * GDN_D), F32), jax.ShapeDtypeStruct((t, SC_WIDTH), F32),
                   jax.ShapeDtypeStruct((n_seq, 1, hd, GDN_D, GDN_D), F32),
                   jax.ShapeDtypeStruct((n_seq, 1, hd, GDN_D, GDN_D), F32)],
        scratch_shapes=[pltpu.VMEM((seq_len, 128), F32) for _ in range(6)],
        compiler_params=pltpu.CompilerParams(dimension_semantics=("parallel", "parallel"),
                                             vmem_limit_bytes=VMEM_LIMIT),
        name="gdn_mixer",
    )(*args)


def _softmax_pv(scores, values, sink):
    m = sink
    for s in scores:
        m = jnp.maximum(m, jnp.max(s, axis=-1, keepdims=True))
    den = jnp.exp(sink - m)
    acc = None
    for s, v in zip(scores, values):
        e = jnp.exp(s - m)
        den = den + jnp.sum(e, axis=-1, keepdims=True)
        pv = _mm(e, v, prec="bf")
        acc = pv if acc is None else acc + pv
    return acc / den


def _attn_ctx_kernel(sink_ref, p_ref, o_ref, kc_ref, vc_ref):
    scale = ATT_HD ** -0.5
    for j in range(ATT_KV_HEADS):
        k = p_ref[:, ATT_Q_W + j * ATT_HD:ATT_Q_W + (j + 1) * ATT_HD]
        v = p_ref[:, ATT_Q_W + ATT_KV_W + j * ATT_HD:ATT_Q_W + ATT_KV_W + (j + 1) * ATT_HD]
        kc_ref[j] = k
        vc_ref[j] = v
        for gi in range(ATT_GROUP):
            hh = j * ATT_GROUP + gi
            q = p_ref[:, hh * ATT_HD:(hh + 1) * ATT_HD]
            s = _mm_nt(q, k, prec="bf") * scale
            sink = jnp.full((1, 1), sink_ref[hh], F32)
            o_ref[:, hh * ATT_HD:(hh + 1) * ATT_HD] = _softmax_pv([s], [v], sink)


def _attn_context(proj_att, sink, n_seq, seq_len):
    kv = pl.BlockSpec((None, None, ATT_KV_HEADS, seq_len, ATT_HD), lambda b: (b, 0, 0, 0, 0))
    return pl.pallas_call(
        _attn_ctx_kernel,
        grid=(n_seq,),
        in_specs=[pl.BlockSpec(memory_space=pltpu.SMEM),
                  pl.BlockSpec((seq_len, ATT_W), lambda b: (b, 0))],
        out_specs=[pl.BlockSpec((seq_len, ATT_Q_W), lambda b: (b, 0)), kv, kv],
        out_shape=[jax.ShapeDtypeStruct((n_seq * seq_len, ATT_Q_W), F32),
                   jax.ShapeDtypeStruct((n_seq, 1, ATT_KV_HEADS, seq_len, ATT_HD), F32),
                   jax.ShapeDtypeStruct((n_seq, 1, ATT_KV_HEADS, seq_len, ATT_HD), F32)],
        compiler_params=pltpu.CompilerParams(dimension_semantics=("parallel",)),
        name="attn_context",
    )(sink, proj_att)


def _rope_tables(seq_len):
    pos = np.arange(seq_len)
    half = ATT_HD // 2
    inv = ROPE_BASE ** (-np.arange(0, half, 2, dtype=np.float32) / half)
    ang_r = (pos // GRID_W).astype(np.float32)[:, None] * inv
    ang_c = (pos % GRID_W).astype(np.float32)[:, None] * inv
    cos = np.concatenate([np.cos(ang_r), np.cos(ang_r), np.cos(ang_c), np.cos(ang_c)], axis=1)
    sin = np.concatenate([-np.sin(ang_r), np.sin(ang_r), -np.sin(ang_c), np.sin(ang_c)], axis=1)
    return (jnp.asarray(np.tile(cos, (1, 2)), F32), jnp.asarray(np.tile(sin, (1, 2)), F32))


def _rope(x, cos, sin):
    lane = lax.broadcasted_iota(jnp.int32, x.shape, 1)
    partner = jnp.where((lane & 31) < 16, pltpu.roll(x, 128 - 16, 1), pltpu.roll(x, 16, 1))
    return x * cos + partner * sin


def _attn_lat_kernel(sink_ref, p_ref, ck_ref, cv_ref, cos_ref, sin_ref, o_ref, *, seq_len):
    scale = ATT_HD ** -0.5
    qb = pl.program_id(1)
    span = 3 * ATT_BLOCK
    q0 = pl.multiple_of(qb * ATT_BLOCK, ATT_BLOCK)
    k0 = pl.multiple_of(jnp.clip((qb - 1) * ATT_BLOCK, 0, seq_len - span), ATT_BLOCK)
    qrows = pl.ds(q0, ATT_BLOCK)
    krows = pl.ds(k0, span)
    kwin = _rope(p_ref[krows, ATT_Q_W:ATT_Q_W + ATT_KV_W], cos_ref[krows, :], sin_ref[krows, :])
    vwin = p_ref[krows, ATT_Q_W + ATT_KV_W:ATT_W]
    qpos = q0 + lax.broadcasted_iota(jnp.int32, (ATT_BLOCK, span), 0)
    kpos = k0 + lax.broadcasted_iota(jnp.int32, (ATT_BLOCK, span), 1)
    valid = jnp.abs(qpos - kpos) <= WINDOW
    cos_q = cos_ref[qrows, :]
    sin_q = sin_ref[qrows, :]
    for pair in range(ATT_HEADS // 2):
        qpair = _rope(p_ref[qrows, pair * 128:(pair + 1) * 128], cos_q, sin_q)
        for sub in range(2):
            hh = pair * 2 + sub
            j = hh // ATT_GROUP
            q = qpair[:, sub * ATT_HD:(sub + 1) * ATT_HD]
            s_loc = _mm_nt(q, kwin[:, j * ATT_HD:(j + 1) * ATT_HD], prec="bf") * scale
            s_loc = jnp.where(valid, s_loc, NEG_INF)
            s_ctx = _mm_nt(q, ck_ref[j], prec="bf") * scale
            sink = jnp.full((1, 1), sink_ref[hh], F32)
            o_ref[:, hh * ATT_HD:(hh + 1) * ATT_HD] = _softmax_pv(
                [s_loc, s_ctx], [vwin[:, j * ATT_HD:(j + 1) * ATT_HD], cv_ref[j]], sink)


def _attn_latent(proj_att, sink, cache_k, cache_v, layer, n_seq, seq_len):
    cos, sin = _rope_tables(seq_len)
    past = cache_k.shape[3]
    nqb = seq_len // ATT_BLOCK
    cache = pl.BlockSpec((None, None, ATT_KV_HEADS, past, ATT_HD), lambda b, q: (b, layer, 0, 0, 0))
    table = pl.BlockSpec((seq_len, 128), lambda b, q: (0, 0))
    return pl.pallas_call(
        functools.partial(_attn_lat_kernel, seq_len=seq_len),
        grid=(n_seq, nqb),
        in_specs=[pl.BlockSpec(memory_space=pltpu.SMEM),
                  pl.BlockSpec((seq_len, ATT_W), lambda b, q: (b, 0)),
                  cache, cache, table, table],
        out_specs=pl.BlockSpec((ATT_BLOCK, ATT_Q_W), lambda b, q: (b * nqb + q, 0)),
        out_shape=jax.ShapeDtypeStruct((n_seq * seq_len, ATT_Q_W), F32),
        compiler_params=pltpu.CompilerParams(dimension_semantics=("parallel", "parallel")),
        name="attn_latent",
    )(sink, proj_att, cache_k, cache_v, cos, sin)


def _rwkv_chunk_operands(x_ref, prm, gate, c, d, *, seq_len):
    (mu_ref, w0_ref, wup_ref, a0_ref, aup_ref, gup_ref, kk_ref, ka_ref, rk_ref) = prm
    ch = RWKV_CHUNK
    n_chunks = seq_len // ch
    hd = RWKV_HD
    rev = d == 1
    r0 = pl.multiple_of(c * ch, ch)
    rows = pl.ds(r0, ch)
    x = x_ref[rows, :]
    prev_row = x_ref[pl.ds(jnp.maximum(r0 - 1, 0), 1), :] * jnp.where(c > 0, 1.0, 0.0)
    next_row = x_ref[pl.ds(jnp.minimum(r0 + ch, seq_len - 1), 1), :] * jnp.where(c < n_chunks - 1, 1.0, 0.0)
    xp, xn = _shifted_rows(x, prev_row, next_row)
    xs = x + mu_ref[0:1, :] * (xp - x) + mu_ref[1:2, :] * (xn - x)
    r = xs[:, 0:RWKV_W]
    k = xs[:, RWKV_W:2 * RWKV_W]
    v = xs[:, 2 * RWKV_W:3 * RWKV_W]
    lo = 3 * RWKV_W
    wl = xs[:, lo + d * RWKV_LORA:lo + (d + 1) * RWKV_LORA]
    al = xs[:, lo + 2 * RWKV_LORA + d * RWKV_LORA:lo + 2 * RWKV_LORA + (d + 1) * RWKV_LORA]
    p = RWKV_PREC
    w_log = -_softplus(-(w0_ref[d:d + 1, :] + _mm(jnp.tanh(wl), wup_ref[d], p["lora"]))) - 0.5
    lw = -jnp.exp(w_log)
    a = jax.nn.sigmoid(a0_ref[d:d + 1, :] + _mm(al, aup_ref[d], p["lora"]))
    k2 = k * (1.0 + (a - 1.0) * ka_ref[...])
    kkv = k * kk_ref[...]
    if d == 0:
        gl = xs[:, lo + 4 * RWKV_LORA:lo + 6 * RWKV_LORA]
        gate[rows, :] = _mm(jax.nn.sigmoid(gl), gup_ref[...], p["gate"])
    incl, strict = _tri_masks(ch, rev)
    r2 = lax.broadcasted_iota(jnp.int32, (ch, 2 * ch), 0)
    c2 = lax.broadcasted_iota(jnp.int32, (ch, 2 * ch), 1) & (ch - 1)
    incl2, strict2 = (r2 <= c2, r2 < c2) if rev else (r2 >= c2, r2 > c2)
    g_inc = _mm(jnp.where(incl, 1.0, 0.0), lw, p["cumsum"])
    g_exc = g_inc - lw
    g_tot = jnp.sum(lw, axis=0, keepdims=True)
    r_dec = r * jnp.exp(g_inc)
    e_neg = jnp.exp(-g_inc)
    e_end = jnp.exp(g_tot - g_inc)
    e_exc = jnp.exp(g_exc)
    k_neg = k2 * e_neg
    k_end = k2 * e_end
    dec_tot = jnp.exp(g_tot)
    rkr = r * k2 * rk_ref[...]
    chains = []
    for h in range(RWKV_HEADS):
        hs = slice(h * hd, (h + 1) * hd)
        kk_h = kkv[:, hs]
        kap = kk_h * lax.rsqrt(jnp.sum(kk_h * kk_h, axis=-1, keepdims=True) + 1e-6)
        b_h = kap * a[:, hs]
        v_h = v[:, hs]
        chains.append(dict(
            d=d, h=h, rows=rows, cols=hs, incl2=incl2, strict2=strict2, v=v_h, dec=dec_tot[:, hs],
            left=jnp.concatenate([kap * e_exc[:, hs], r_dec[:, hs]], axis=0),
            right=jnp.concatenate([b_h * e_neg[:, hs], k_neg[:, hs]], axis=0),
            bk_end=jnp.concatenate([b_h * e_end[:, hs], k_end[:, hs]], axis=0),
            bonus=jnp.sum(rkr[:, hs], axis=-1, keepdims=True) * v_h))
    return chains


def _rwkv_step(x_ref, prm, st_ref, ysum, bon, gate, i, *, seq_len):
    ch = RWKV_CHUNK
    p = RWKV_PREC
    n_chunks = seq_len // ch
    cs = (_rwkv_chunk_operands(x_ref, prm, gate, i, 0, seq_len=seq_len)
          + _rwkv_chunk_operands(x_ref, prm, gate, n_chunks - 1 - i, 1, seq_len=seq_len))
    ms = [_mm_nt(c["left"], c["right"], p["amat"]) for c in cs]
    a_ks = [jnp.where(c["strict2"], m[:ch], 0.0) for c, m in zip(cs, ms)]
    a_rs = [jnp.where(c["incl2"], m[ch:], 0.0) for c, m in zip(cs, ms)]
    t_invs = _unit_tri_inv_many([a_k[:, :ch] for a_k in a_ks], p["inv"])
    ss = [st_ref[c["d"], c["h"]] for c in cs]
    lss = [_mm_nt(c["left"], s, p["state"]) for c, s in zip(cs, ss)]
    akvs = [_mm(a_k[:, ch:], c["v"], p["solve"]) for c, a_k in zip(cs, a_ks)]
    us = [-_mm(t, ls[:ch] + akv, p["solve"]) for t, ls, akv in zip(t_invs, lss, akvs)]
    uvs = [jnp.concatenate([u, c["v"]], axis=0) for c, u in zip(cs, us)]
    ys = [ls[ch:] + _mm(a_r, uv, p["out"]) for ls, a_r, uv in zip(lss, a_rs, uvs)]
    s_news = [s * c["dec"] + _mm_tn(uv, c["bk_end"], p["update"]) for c, s, uv in zip(cs, ss, uvs)]
    for c, y, s_new in zip(cs, ys, s_news):
        st_ref[c["d"], c["h"]] = s_new
        ysum[c["rows"], c["cols"]] += y
        bon[c["rows"], c["cols"]] += c["bonus"]


def _rwkv_kernel(*refs, seq_len, has_init):
    x_ref = refs[0]
    prm = refs[1:10]
    lnw_ref, lnb_ref = refs[10:12]
    pos = 12
    if has_init:
        s0f_ref, s0b_ref = refs[pos:pos + 2]
        pos += 2
    o_ref, sf_ref, sb_ref = refs[pos:pos + 3]
    ysum, bon, gate, st_ref = refs[pos + 3:]
    ch = RWKV_CHUNK
    n_chunks = seq_len // ch
    ysum[...] = jnp.zeros(ysum.shape, F32)
    bon[...] = jnp.zeros(bon.shape, F32)
    if has_init:
        st_ref[0] = s0f_ref[...]
        st_ref[1] = s0b_ref[...]
    else:
        st_ref[...] = jnp.zeros(st_ref.shape, F32)

    def step(i, carry):
        _rwkv_step(x_ref, prm, st_ref, ysum, bon, gate, i, seq_len=seq_len)
        return carry

    lax.fori_loop(0, n_chunks, step, 0)
    sf_ref[...] = st_ref[0]
    sb_ref[...] = st_ref[1]

    def finish(i, carry):
        rows = pl.ds(pl.multiple_of(i * ch, ch), ch)
        y = ysum[rows, :]
        parts = []
        for h in range(RWKV_HEADS):
            y_h = y[:, h * RWKV_HD:(h + 1) * RWKV_HD]
            mean = jnp.mean(y_h, axis=-1, keepdims=True)
            cen = y_h - mean
            var = jnp.mean(cen * cen, axis=-1, keepdims=True)
            parts.append(cen * lax.rsqrt(var + GN_EPS))
        yn = jnp.concatenate(parts, axis=1) * lnw_ref[...] + lnb_ref[...]
        o_ref[rows, :] = (yn + bon[rows, :]) * gate[rows, :]
        return carry

    lax.fori_loop(0, n_chunks, finish, 0)


def _rwkv_mixer(x_rw, n_seq, seq_len, params, s0_f, s0_b):
    has_init = s0_f is not None
    (mu, w0, w_up, a0, a_up, g_up, k_k, k_a, r_k, ln_w, ln_b) = params
    row = lambda a: a.reshape(1, RWKV_W)
    args = [x_rw, mu, w0, w_up, a0, a_up, g_up, row(k_k), row(k_a), row(r_k), row(ln_w), row(ln_b)]

    def whole(a):
        nd = a.ndim
        return pl.BlockSpec(a.shape, lambda s, nd=nd: (0,) * nd)

    in_specs = [pl.BlockSpec((seq_len, RWKV_IN), lambda s: (s, 0))] + [whole(a) for a in args[1:]]
    state = pl.BlockSpec((None, None, RWKV_HEADS, RWKV_HD, RWKV_HD), lambda s: (s, 0, 0, 0, 0))
    if has_init:
        in_specs += [state, state]
        args += [s0_f, s0_b]
    st_shape = jax.ShapeDtypeStruct((n_seq, 1, RWKV_HEADS, RWKV_HD, RWKV_HD), F32)
    return pl.pallas_call(
        functools.partial(_rwkv_kernel, seq_len=seq_len, has_init=has_init),
        grid=(n_seq,),
        in_specs=in_specs,
        out_specs=[pl.BlockSpec((seq_len, RWKV_W), lambda s: (s, 0)), state, state],
        out_shape=[jax.ShapeDtypeStruct((n_seq * seq_len, RWKV_W), F32), st_shape, st_shape],
        scratch_shapes=[pltpu.VMEM((seq_len, RWKV_W), F32) for _ in range(3)]
        + [pltpu.VMEM((2, RWKV_HEADS, RWKV_HD, RWKV_HD), F32)],
        compiler_params=pltpu.CompilerParams(dimension_semantics=("parallel",), vmem_limit_bytes=VMEM_LIMIT),
        name="rwkv_mixer",
    )(*args)


def kernel(x_prompt, x_sample, state_gdn_fwd, state_gdn_bwd, cache_attn_k, cache_attn_v, state_rwkv_fwd, state_rwkv_bwd, c, c_ctx, mod_w, mod_b, norm_mix, norm_mlp, mlp_w1, mlp_w2, norm_final, ev_w_in, ev_w_out, gdn_conv, gdn_a_log, gdn_dt_bias, gdn_norm, sc_conv, od_w_in, od_w_out, attn_sink, rwkv_mu, rwkv_w0, rwkv_w_up, rwkv_a0, rwkv_a_up, rwkv_g_up, rwkv_k_k, rwkv_k_a, rwkv_r_k, rwkv_ln_w, rwkv_ln_b):
    bp, lp, _ = x_prompt.shape
    bs, ls, _ = x_sample.shape
    depth = mod_w.shape[0]
    c_rows = jnp.concatenate([c_ctx[None, :], c, jnp.zeros((MOD_ROWS - 1 - bs, D_MODEL), F32)], axis=0)
    mods = _modulation(c_rows, mod_w, mod_b)

    groups = [
        dict(x=x_prompt.reshape(bp * lp, D_MODEL), n=bp, l=lp, latent=False,
             mod=lambda layer: _mod_spec(layer, lp // TOKEN_TILE, 0, 0)),
        dict(x=x_sample.reshape(bs * ls, D_MODEL), n=bs, l=ls, latent=True,
             mod=lambda layer: _mod_spec(layer, ls // TOKEN_TILE, 1, 1)),
    ]
    outs = {}
    for layer in range(depth):
        w1 = mlp_w1[layer].astype(BF16)
        w2 = mlp_w2[layer].astype(BF16)
        final = layer == depth - 1
        if layer % 2 == 0:
            e = layer // 2
            w = ev_w_in[e]
            qkvz = GDN_QKV_W + GDN_HEADS * GDN_D
            n_gate = 4 * GDN_HEADS
            w_in = jnp.concatenate(
                [w[:, :qkvz], w[:, qkvz + n_gate:], w[:, qkvz:qkvz + n_gate],
                 jnp.zeros((D_MODEL, EV_IN_PAD - w.shape[1]), F32)], axis=1).astype(BF16)
            w_out = ev_w_out[e].astype(BF16)
            alog_vec = jnp.zeros((1, 128), F32).at[0, 2 * GDN_HEADS:4 * GDN_HEADS].set(gdn_a_log[e].reshape(-1))
            dtb_vec = jnp.zeros((1, 128), F32).at[0, 2 * GDN_HEADS:4 * GDN_HEADS].set(gdn_dt_bias[e].reshape(-1))
            for grp in groups:
                (proj,) = _inproj(grp["x"], mods, norm_mix[layer], w_in, (EV_IN_PAD,), grp["mod"](layer))
                s0 = (state_gdn_fwd[:, e:e + 1], state_gdn_bwd[:, e:e + 1]) if grp["latent"] else (None, None)
                o, sc, s_f, s_b = _gdn_mixer(proj, grp["n"], grp["l"], gdn_conv[e], sc_conv[e], gdn_norm[e],
                                             alog_vec, dtb_vec, *s0)
                if not grp["latent"]:
                    outs.setdefault("gdn_f", []).append(s_f)
                    outs.setdefault("gdn_b", []).append(s_b)
                grp["x"] = _outproj_mlp(o, sc, grp["x"], mods, norm_mlp[layer], w_out, w1, w2, norm_final,
                                        grp["mod"](layer), final)
        else:
            o_ = layer // 2
            w = od_w_in[o_]
            w_in = jnp.concatenate([w[:, ATT_W:], w[:, :ATT_W]], axis=1).astype(BF16)
            w_out = od_w_out[o_].astype(BF16)
            rw = (rwkv_mu[o_], rwkv_w0[o_], rwkv_w_up[o_], rwkv_a0[o_], rwkv_a_up[o_], rwkv_g_up[o_],
                  rwkv_k_k[o_], rwkv_k_a[o_], rwkv_r_k[o_].reshape(-1), rwkv_ln_w[o_], rwkv_ln_b[o_])
            for grp in groups:
                x_rw, p_att = _inproj(grp["x"], mods, norm_mix[layer], w_in, (RWKV_IN, ATT_W), grp["mod"](layer))
                if grp["latent"]:
                    att = _attn_latent(p_att, attn_sink[o_], cache_attn_k, cache_attn_v, o_, grp["n"], grp["l"])
                    rwo, _, _ = _rwkv_mixer(x_rw, grp["n"], grp["l"], rw,
                                            state_rwkv_fwd[:, o_:o_ + 1], state_rwkv_bwd[:, o_:o_ + 1])
                else:
                    att, kc, vc = _attn_context(p_att, attn_sink[o_], grp["n"], grp["l"])
                    rwo, s_f, s_b = _rwkv_mixer(x_rw, grp["n"], grp["l"], rw, None, None)
                    outs.setdefault("att_k", []).append(kc)
                    outs.setdefault("att_v", []).append(vc)
                    outs.setdefault("rw_f", []).append(s_f)
                    outs.setdefault("rw_b", []).append(s_b)
                grp["x"] = _outproj_mlp(att, rwo, grp["x"], mods, norm_mlp[layer], w_out, w1, w2, norm_final,
                                        grp["mod"](layer), final)
    cat = lambda key: jnp.concatenate(outs[key], axis=1)
    return (groups[0]["x"].reshape(bp, lp, D_MODEL), groups[1]["x"].reshape(bs, ls, D_MODEL),
            cat("gdn_f"), cat("gdn_b"), cat("att_k"), cat("att_v"), cat("rw_f"), cat("rw_b"))
```

```python
import functools

import jax
import jax.numpy as jnp
import numpy as np
from jax import lax
from jax.experimental import pallas as pl
from jax.experimental.pallas import tpu as pltpu

F32 = jnp.float32
BF16 = jnp.bfloat16
HIGHEST = lax.Precision.HIGHEST

D_MODEL = 1024
N_MOD = 6
D_FF = 4 * D_MODEL
NORM_EPS = 1e-6
TOKEN_TILE = 256
MOD_ROWS = 8

GDN_HEADS = 4
GDN_D = 128
GDN_CHUNK = 64
GDN_GROUP = 4
GDN_BLOCK_ROWS = 1024
GDN_QKV_W = 3 * GDN_HEADS * GDN_D
SC_WIDTH = 512
EV_IN_PAD = 3712

ATT_HEADS = 8
ATT_KV_HEADS = 2
ATT_GROUP = ATT_HEADS // ATT_KV_HEADS
ATT_HD = 64
ATT_Q_W = ATT_HEADS * ATT_HD
ATT_KV_W = ATT_KV_HEADS * ATT_HD
ATT_W = ATT_Q_W + 2 * ATT_KV_W
WINDOW = 128
ATT_BLOCK = 128
GRID_W = 64
ROPE_BASE = 10000.0
NEG_INF = -1e30

RWKV_HEADS = 8
RWKV_HD = 64
RWKV_W = RWKV_HEADS * RWKV_HD
RWKV_LORA = 64
RWKV_IN = 3 * RWKV_W + 3 * 2 * RWKV_LORA
RWKV_CHUNK = 64
GN_EPS = 64e-5

VMEM_LIMIT = 56 * 1024 * 1024

RWKV_PREC = dict(lora="bf", gate="bf", cumsum="x3", amat="bf", inv="bf", state="bf", solve="x3", out="bf",
                 update="bf")


def _split_bf16(a):
    hi = a.astype(BF16)
    return hi, (a - hi.astype(F32)).astype(BF16)


def _dot(a, b, dims, prec):
    dn = (dims, ((), ()))
    if prec == "hi":
        return lax.dot_general(a, b, dn, precision=HIGHEST, preferred_element_type=F32)
    if prec == "x3":
        ah, al = _split_bf16(a)
        bh, bl = _split_bf16(b)
        one = lambda x, y: lax.dot_general(x, y, dn, preferred_element_type=F32)
        return one(ah, bh) + one(ah, bl) + one(al, bh)
    assert prec == "bf", prec
    return lax.dot_general(a.astype(BF16), b.astype(BF16), dn, preferred_element_type=F32)


def _mm(a, b, prec="hi"):
    return _dot(a, b, ((1,), (0,)), prec)


def _mm_nt(a, b, prec="hi"):
    return _dot(a, b, ((1,), (1,)), prec)


def _mm_tn(a, b, prec="hi"):
    return _dot(a, b, ((0,), (0,)), prec)


def _silu(x):
    return x * jax.nn.sigmoid(x)


def _softplus(x):
    return jnp.maximum(x, 0.0) + jnp.log1p(jnp.exp(-jnp.abs(x)))


def _rms(x, w):
    return x * lax.rsqrt(jnp.mean(x * x, axis=-1, keepdims=True) + NORM_EPS) * w


def _tri_masks(n, rev):
    r = lax.broadcasted_iota(jnp.int32, (n, n), 0)
    c = lax.broadcasted_iota(jnp.int32, (n, n), 1)
    if rev:
        return r <= c, r < c
    return r >= c, r > c


def _unit_tri_inv_many(mats, prec):
    n = mats[0].shape[0]
    r = lax.broadcasted_iota(jnp.int32, (n, n), 0)
    c = lax.broadcasted_iota(jnp.int32, (n, n), 1)
    eye = jnp.where(r == c, 1.0, 0.0)
    ts = [eye - jnp.where((r >> 1) == (c >> 1), a, 0.0) for a in mats]
    shift = 1
    while (1 << shift) < n:
        join = ((r >> (shift + 1)) == (c >> (shift + 1))) & ((r >> shift) != (c >> shift))
        inner = [_mm(jnp.where(join, a, 0.0), t, prec) for a, t in zip(mats, ts)]
        ts = [t - _mm(t, w, prec) for t, w in zip(ts, inner)]
        shift += 1
    return ts


def _unit_tri_inv(a, prec):
    return _unit_tri_inv_many([a], prec)[0]


def _shifted_rows(x, prev_row, next_row):
    n = x.shape[0]
    row = lax.broadcasted_iota(jnp.int32, x.shape, 0)
    xp = jnp.where(row == 0, prev_row, pltpu.roll(x, 1, 0))
    xn = jnp.where(row == n - 1, next_row, pltpu.roll(x, n - 1, 0))
    return xp, xn


def _conv3(x, w_ref, seq_len):
    n = x.shape[0]
    assert seq_len & (seq_len - 1) == 0 and n % seq_len == 0
    pos = lax.broadcasted_iota(jnp.int32, x.shape, 0) & (seq_len - 1)
    xp = jnp.where(pos == 0, 0.0, pltpu.roll(x, 1, 0))
    xn = jnp.where(pos == seq_len - 1, 0.0, pltpu.roll(x, n - 1, 0))
    return xp * w_ref[0:1, :] + x * w_ref[1:2, :] + xn * w_ref[2:3, :]


def _mod_kernel(c_ref, w_ref, b_ref, o_ref):
    s = _silu(c_ref[...])
    o_ref[...] = _mm(s, w_ref[...], prec="bf") + b_ref[...]


def _modulation(c_rows, mod_w, mod_b):
    depth = mod_w.shape[0]
    nblk = (N_MOD * D_MODEL) // D_MODEL
    out = pl.pallas_call(
        _mod_kernel,
        grid=(depth, nblk),
        in_specs=[
            pl.BlockSpec((MOD_ROWS, D_MODEL), lambda l, j: (0, 0)),
            pl.BlockSpec((None, D_MODEL, D_MODEL), lambda l, j: (l, 0, j)),
            pl.BlockSpec((None, 1, D_MODEL), lambda l, j: (l, 0, j)),
        ],
        out_specs=pl.BlockSpec((None, MOD_ROWS, D_MODEL), lambda l, j: (l, 0, j)),
        out_shape=jax.ShapeDtypeStruct((depth, MOD_ROWS, N_MOD * D_MODEL), F32),
        compiler_params=pltpu.CompilerParams(dimension_semantics=("parallel", "parallel")),
        name="modulation",
    )(c_rows, mod_w, mod_b.reshape(depth, 1, N_MOD * D_MODEL))
    return out.reshape(depth, MOD_ROWS, N_MOD, D_MODEL)


def _mod_spec(layer, tiles_per_seq, row_base, row_step):
    return pl.BlockSpec((None, None, N_MOD, D_MODEL),
                        lambda i: (layer, row_base + (i // tiles_per_seq) * row_step, 0, 0))


def _inproj_kernel(x_ref, mod_ref, nw_ref, w_ref, *o_refs):
    h = _rms(x_ref[...], nw_ref[...])
    h = h * (1.0 + mod_ref[1:2, :]) + mod_ref[0:1, :]
    y = _mm(h, w_ref[...], prec="bf")
    off = 0
    for o_ref in o_refs:
        n = o_ref.shape[-1]
        o_ref[...] = y[:, off:off + n]
        off += n


def _inproj(x, mods, norm_w, w_bf16, splits, mod_spec):
    t = x.shape[0]
    n_in = w_bf16.shape[1]
    return pl.pallas_call(
        _inproj_kernel,
        grid=(t // TOKEN_TILE,),
        in_specs=[
            pl.BlockSpec((TOKEN_TILE, D_MODEL), lambda i: (i, 0)),
            mod_spec,
            pl.BlockSpec((1, D_MODEL), lambda i: (0, 0)),
            pl.BlockSpec((D_MODEL, n_in), lambda i: (0, 0)),
        ],
        out_specs=[pl.BlockSpec((TOKEN_TILE, n), lambda i: (i, 0)) for n in splits],
        out_shape=[jax.ShapeDtypeStruct((t, n), F32) for n in splits],
        compiler_params=pltpu.CompilerParams(dimension_semantics=("parallel",), vmem_limit_bytes=VMEM_LIMIT),
        name="inproj",
    )(x, mods, norm_w.reshape(1, D_MODEL), w_bf16)


def _mlp_kernel(a_ref, b_ref, x_ref, mod_ref, nw_ref, woa_ref, wob_ref, w1_ref, w2_ref, nf_ref, o_ref, *, final):
    y = _mm(a_ref[...], woa_ref[...], prec="bf") + _mm(b_ref[...], wob_ref[...], prec="bf")
    x1 = x_ref[...] + mod_ref[2:3, :] * y
    h = _rms(x1, nw_ref[...])
    h = (h * (1.0 + mod_ref[4:5, :]) + mod_ref[3:4, :]).astype(BF16)
    acc = jnp.zeros(x1.shape, F32)
    for j in range(D_FF // D_MODEL):
        cols = slice(j * D_MODEL, (j + 1) * D_MODEL)
        u = jnp.maximum(_mm(h, w1_ref[:, cols], prec="bf"), 0.0)
        acc = acc + _mm(u * u, w2_ref[cols, :], prec="bf")
    x2 = x1 + mod_ref[5:6, :] * acc
    if final:
        x2 = _rms(x2, nf_ref[...])
    o_ref[...] = x2


def _outproj_mlp(a, b, x, mods, norm_w, w_out, w1, w2, norm_final, mod_spec, final):
    t = x.shape[0]
    half = a.shape[1]
    const = lambda i: (0, 0)
    return pl.pallas_call(
        functools.partial(_mlp_kernel, final=final),
        grid=(t // TOKEN_TILE,),
        in_specs=[
            pl.BlockSpec((TOKEN_TILE, half), lambda i: (i, 0)),
            pl.BlockSpec((TOKEN_TILE, half), lambda i: (i, 0)),
            pl.BlockSpec((TOKEN_TILE, D_MODEL), lambda i: (i, 0)),
            mod_spec,
            pl.BlockSpec((1, D_MODEL), const),
            pl.BlockSpec((half, D_MODEL), const),
            pl.BlockSpec((half, D_MODEL), lambda i: (1, 0)),
            pl.BlockSpec((D_MODEL, D_FF), const),
            pl.BlockSpec((D_FF, D_MODEL), const),
            pl.BlockSpec((1, D_MODEL), const),
        ],
        out_specs=pl.BlockSpec((TOKEN_TILE, D_MODEL), lambda i: (i, 0)),
        out_shape=jax.ShapeDtypeStruct((t, D_MODEL), F32),
        compiler_params=pltpu.CompilerParams(dimension_semantics=("parallel",), vmem_limit_bytes=VMEM_LIMIT),
        name="outproj_mlp",
    )(a, b, x, mods, norm_w.reshape(1, D_MODEL), w_out, w_out, w1, w2, norm_final.reshape(1, D_MODEL))


def _gdn_decay_terms(g, rev):
    c = g.shape[0]
    incl = _tri_masks(c, rev)[0]
    before_col = _tri_masks(c, not rev)[0]
    eye = lax.broadcasted_iota(jnp.int32, (c, c), 0) == lax.broadcasted_iota(jnp.int32, (c, c), 1)
    gc_row = jnp.sum(jnp.where(before_col, jnp.broadcast_to(g, (c, c)), 0.0), axis=0, keepdims=True)
    gc_col = jnp.sum(jnp.where(eye, jnp.broadcast_to(gc_row, (c, c)), 0.0), axis=1, keepdims=True)
    decay = jnp.where(incl, jnp.exp(jnp.where(incl, gc_col - gc_row, 0.0)), 0.0)
    g_tot = jnp.sum(g, axis=0, keepdims=True)
    return decay, jnp.exp(gc_col), jnp.exp(g_tot - gc_col), jnp.exp(g_tot)


def _gdn_kernel(*refs, seq_len, seqs, has_init):
    (q_ref, k_ref, v_ref, z_ref, scb_ref, scc_ref, sch_ref, gate_ref,
     cq_ref, ck_ref, cv_ref, csc_ref, gn_ref, alog_ref, dtb_ref) = refs[:15]
    pos = 15
    if has_init:
        s0f_ref, s0b_ref = refs[pos:pos + 2]
        pos += 2
    o_ref, sco_ref, sf_ref, sb_ref = refs[pos:pos + 4]
    qs, ks, vs, osum, betas, gs, u_s, w_s, qd_s, kd_s, in_s, ge_s, st = refs[pos + 4:]
    head = pl.program_id(1)
    ch = GDN_CHUNK
    n_chunks = seq_len // ch

    def conv_silu(x_ref, c_ref):
        return _silu(_conv3(x_ref[...], c_ref, seq_len))

    def l2norm(x):
        return x * lax.rsqrt(jnp.sum(x * x, axis=-1, keepdims=True) + 1e-6)

    qs[...] = l2norm(conv_silu(q_ref, cq_ref)) * (GDN_D ** -0.5)
    ks[...] = l2norm(conv_silu(k_ref, ck_ref))
    vs[...] = conv_silu(v_ref, cv_ref)
    gates = gate_ref[...]
    betas[...] = jax.nn.sigmoid(gates)
    gs[...] = -jnp.exp(alog_ref[...]) * _softplus(gates + dtb_ref[...])
    sco_ref[...] = scb_ref[...] * _conv3(scc_ref[...] * sch_ref[...], csc_ref, seq_len)
    osum[...] = jnp.zeros(osum.shape, F32)

    lane = lax.broadcasted_iota(jnp.int32, (ch, 128), 1)

    def pick(ref, rows, col):
        return jnp.sum(jnp.where(lane == col, ref[rows, :], 0.0), axis=1, keepdims=True)

    def solve_group(gi, carry):
        items = []
        for j in range(GDN_GROUP):
            c = gi * GDN_GROUP + j
            rows = pl.ds(pl.multiple_of(c * ch, ch), ch)
            items.append(dict(c=c, rows=rows, q=qs[rows, :], k=ks[rows, :], v=vs[rows, :]))
        kks = [_mm_nt(it["k"], it["k"], "bf") for it in items]
        qks = [_mm_nt(it["q"], it["k"], "bf") for it in items]
        subs = []
        for it, kk, qk in zip(items, kks, qks):
            for d in range(2):
                beta = pick(betas, it["rows"], d * GDN_HEADS + head)
                g = pick(gs, it["rows"], 2 * GDN_HEADS + d * GDN_HEADS + head)
                decay, e_gc, e_rest, e_tot = _gdn_decay_terms(g, rev=(d == 1))
                strict = _tri_masks(ch, d == 1)[1]
                subs.append(dict(
                    d=d, c=it["c"], rows=it["rows"],
                    a=jnp.where(strict, kk * beta * decay, 0.0),
                    rhs=jnp.concatenate([it["v"] * beta, it["k"] * (beta * e_gc)], axis=1),
                    intra=qk * decay, qd=it["q"] * e_gc, kd=it["k"] * e_rest, ge=e_tot))
        ts = _unit_tri_inv_many([s["a"] for s in subs], "bf")
        uws = [_mm(t, s["rhs"], "bf") for t, s in zip(ts, subs)]
        for s, uw in zip(subs, uws):
            d, rows = s["d"], s["rows"]
            u_s[d, rows, :] = uw[:, :GDN_D]
            w_s[d, rows, :] = uw[:, GDN_D:].astype(BF16)
            qd_s[d, rows, :] = s["qd"].astype(BF16)
            kd_s[d, rows, :] = s["kd"].astype(BF16)
            in_s[d, rows, :] = s["intra"].astype(BF16)
            ge_s[d, pl.ds(pl.multiple_of(s["c"] * 8, 8), 8), :] = jnp.broadcast_to(s["ge"], (8, 128))
        return carry

    lax.fori_loop(0, seqs * n_chunks // GDN_GROUP, solve_group, 0)

    for j in range(seqs):
        st[0, j] = s0f_ref[j] if has_init else jnp.zeros((GDN_D, GDN_D), F32)
        st[1, j] = s0b_ref[j] if has_init else jnp.zeros((GDN_D, GDN_D), F32)

    def recur(i, carry):
        cs = []
        for j in range(seqs):
            for d in range(2):
                c = j * n_chunks + (i if d == 0 else n_chunks - 1 - i)
                cs.append(dict(d=d, j=j, rows=pl.ds(pl.multiple_of(c * ch, ch), ch),
                               ge=ge_s[d, pl.ds(pl.multiple_of(c * 8, 8), 1), :], s=st[d, j]))
        sbs = [c["s"].astype(BF16) for c in cs]
        wss = [_mm(w_s[c["d"], c["rows"], :], sb, "bf") for c, sb in zip(cs, sbs)]
        qss = [_mm(qd_s[c["d"], c["rows"], :], sb, "bf") for c, sb in zip(cs, sbs)]
        ebs = [(u_s[c["d"], c["rows"], :] - ws).astype(BF16) for c, ws in zip(cs, wss)]
        outs = [qs_ + _mm(in_s[c["d"], c["rows"], :], eb, "bf") for c, qs_, eb in zip(cs, qss, ebs)]
        s_news = [c["s"] * c["ge"] + _mm_tn(kd_s[c["d"], c["rows"], :], eb, "bf") for c, eb in zip(cs, ebs)]
        for c, o, s_new in zip(cs, outs, s_news):
            st[c["d"], c["j"]] = s_new
            osum[c["rows"], :] += o
        return carry

    lax.fori_loop(0, n_chunks, recur, 0)
    sf_ref[...] = st[0]
    sb_ref[...] = st[1]
    o_ref[...] = _rms(osum[...], gn_ref[...]) * _silu(z_ref[...])


def _gdn_mixer(proj, n_seq, seq_len, conv_w, sc_conv_w, gdn_norm, alog_vec, dtb_vec, s0_f, s0_b):
    has_init = s0_f is not None
    hd = GDN_HEADS
    seqs = max(1, GDN_BLOCK_ROWS // seq_len)
    rows = seqs * seq_len
    assert n_seq % seqs == 0 and (rows // GDN_CHUNK) % GDN_GROUP == 0

    def col(block):
        return pl.BlockSpec((rows, 128), lambda s, h, b=block: (s, b * hd + h))

    def wcol(block):
        return pl.BlockSpec((3, 128), lambda s, h, b=block: (0, b * hd + h))

    vec = pl.BlockSpec((1, 128), lambda s, h: (0, 0))
    state = pl.BlockSpec((seqs, None, None, GDN_D, GDN_D), lambda s, h: (s, 0, h, 0, 0))
    in_specs = [col(0), col(1), col(2), col(3), col(4), col(5), col(6),
                pl.BlockSpec((rows, 128), lambda s, h: (s, 7 * hd)),
                wcol(0), wcol(1), wcol(2), wcol(0), vec, vec, vec]
    args = [proj] * 8 + [conv_w, conv_w, conv_w, sc_conv_w, gdn_norm.reshape(1, 128), alog_vec, dtb_vec]
    if has_init:
        in_specs += [state, state]
        args += [s0_f, s0_b]
    t = n_seq * seq_len
    out_tok = pl.BlockSpec((rows, 128), lambda s, h: (s, h))
    scratch = ([pltpu.VMEM((rows, 128), F32) for _ in range(6)]
               + [pltpu.VMEM((2, rows, GDN_D), F32)]
               + [pltpu.VMEM((2, rows, GDN_D), BF16) for _ in range(3)]
               + [pltpu.VMEM((2, rows, GDN_CHUNK), BF16),
                  pltpu.VMEM((2, rows // GDN_CHUNK * 8, 128), F32),
                  pltpu.VMEM((2, seqs, GDN_D, GDN_D), F32)])
    return pl.pallas_call(
        functools.partial(_gdn_kernel, seq_len=seq_len, seqs=seqs, has_init=has_init),
        grid=(n_seq // seqs, hd),
        in_specs=in_specs,
        out_specs=[out_tok, out_tok, state, state],
        out_shape=[jax.ShapeDtypeStruct((t, hd * GDN_D), F32), jax.ShapeDtypeStruct((t, SC_WIDTH), F32),
                   jax.ShapeDtypeStruct((n_seq, 1, hd, GDN_D, GDN_D), F32),
                   jax.ShapeDtypeStruct((n_seq, 1, hd, GDN_D, GDN_D), F32)],
        scratch_shapes=scratch,
        compiler_params=pltpu.CompilerParams(dimension_semantics=("parallel", "parallel"),
                                             vmem_limit_bytes=VMEM_LIMIT),
        name="gdn_mixer",
    )(*args)


def _softmax_pv(scores, values, sink):
    m = sink
    for s in scores:
        m = jnp.maximum(m, jnp.max(s, axis=-1, keepdims=True))
    den = jnp.exp(sink - m)
    acc = None
    for s, v in zip(scores, values):
        e = jnp.exp(s - m)
        den = den + jnp.sum(e, axis=-1, keepdims=True)
        pv = _mm(e, v, prec="bf")
        acc = pv if acc is None else acc + pv
    return acc / den


def _attn_ctx_kernel(sink_ref, p_ref, o_ref, kc_ref, vc_ref):
    scale = ATT_HD ** -0.5
    for j in range(ATT_KV_HEADS):
        k = p_ref[:, ATT_Q_W + j * ATT_HD:ATT_Q_W + (j + 1) * ATT_HD]
        v = p_ref[:, ATT_Q_W + ATT_KV_W + j * ATT_HD:ATT_Q_W + ATT_KV_W + (j + 1) * ATT_HD]
        kc_ref[j] = k
        vc_ref[j] = v
        for gi in range(ATT_GROUP):
            hh = j * ATT_GROUP + gi
            q = p_ref[:, hh * ATT_HD:(hh + 1) * ATT_HD]
            s = _mm_nt(q, k, prec="bf") * scale
            sink = jnp.full((1, 1), sink_ref[hh], F32)
            o_ref[:, hh * ATT_HD:(hh + 1) * ATT_HD] = _softmax_pv([s], [v], sink)


def _attn_context(proj_att, sink, n_seq, seq_len):
    kv = pl.BlockSpec((None, None, ATT_KV_HEADS, seq_len, ATT_HD), lambda b: (b, 0, 0, 0, 0))
    return pl.pallas_call(
        _attn_ctx_kernel,
        grid=(n_seq,),
        in_specs=[pl.BlockSpec(memory_space=pltpu.SMEM),
                  pl.BlockSpec((seq_len, ATT_W), lambda b: (b, 0))],
        out_specs=[pl.BlockSpec((seq_len, ATT_Q_W), lambda b: (b, 0)), kv, kv],
        out_shape=[jax.ShapeDtypeStruct((n_seq * seq_len, ATT_Q_W), F32),
                   jax.ShapeDtypeStruct((n_seq, 1, ATT_KV_HEADS, seq_len, ATT_HD), F32),
                   jax.ShapeDtypeStruct((n_seq, 1, ATT_KV_HEADS, seq_len, ATT_HD), F32)],
        compiler_params=pltpu.CompilerParams(dimension_semantics=("parallel",)),
        name="attn_context",
    )(sink, proj_att)


def _rope_tables(seq_len):
    pos = np.arange(seq_len)
    half = ATT_HD // 2
    inv = ROPE_BASE ** (-np.arange(0, half, 2, dtype=np.float32) / half)
    ang_r = (pos // GRID_W).astype(np.float32)[:, None] * inv
    ang_c = (pos % GRID_W).astype(np.float32)[:, None] * inv
    cos = np.concatenate([np.cos(ang_r), np.cos(ang_r), np.cos(ang_c), np.cos(ang_c)], axis=1)
    sin = np.concatenate([-np.sin(ang_r), np.sin(ang_r), -np.sin(ang_c), np.sin(ang_c)], axis=1)
    return (jnp.asarray(np.tile(cos, (1, 2)), F32), jnp.asarray(np.tile(sin, (1, 2)), F32))


def _rope(x, cos, sin):
    lane = lax.broadcasted_iota(jnp.int32, x.shape, 1)
    partner = jnp.where((lane & 31) < 16, pltpu.roll(x, 128 - 16, 1), pltpu.roll(x, 16, 1))
    return x * cos + partner * sin


def _attn_lat_kernel(sink_ref, p_ref, ck_ref, cv_ref, cos_ref, sin_ref, o_ref, *, seq_len):
    scale = ATT_HD ** -0.5
    qb = pl.program_id(1)
    span = 3 * ATT_BLOCK
    q0 = pl.multiple_of(qb * ATT_BLOCK, ATT_BLOCK)
    k0 = pl.multiple_of(jnp.clip((qb - 1) * ATT_BLOCK, 0, seq_len - span), ATT_BLOCK)
    qrows = pl.ds(q0, ATT_BLOCK)
    krows = pl.ds(k0, span)
    kwin = _rope(p_ref[krows, ATT_Q_W:ATT_Q_W + ATT_KV_W], cos_ref[krows, :], sin_ref[krows, :])
    vwin = p_ref[krows, ATT_Q_W + ATT_KV_W:ATT_W]
    qpos = q0 + lax.broadcasted_iota(jnp.int32, (ATT_BLOCK, span), 0)
    kpos = k0 + lax.broadcasted_iota(jnp.int32, (ATT_BLOCK, span), 1)
    valid = jnp.abs(qpos - kpos) <= WINDOW
    cos_q = cos_ref[qrows, :]
    sin_q = sin_ref[qrows, :]
    for pair in range(ATT_HEADS // 2):
        qpair = _rope(p_ref[qrows, pair * 128:(pair + 1) * 128], cos_q, sin_q)
        for sub in range(2):
            hh = pair * 2 + sub
            j = hh // ATT_GROUP
            q = qpair[:, sub * ATT_HD:(sub + 1) * ATT_HD]
            s_loc = _mm_nt(q, kwin[:, j * ATT_HD:(j + 1) * ATT_HD], prec="bf") * scale
            s_loc = jnp.where(valid, s_loc, NEG_INF)
            s_ctx = _mm_nt(q, ck_ref[j], prec="bf") * scale
            sink = jnp.full((1, 1), sink_ref[hh], F32)
            o_ref[:, hh * ATT_HD:(hh + 1) * ATT_HD] = _softmax_pv(
                [s_loc, s_ctx], [vwin[:, j * ATT_HD:(j + 1) * ATT_HD], cv_ref[j]], sink)


def _attn_latent(proj_att, sink, cache_k, cache_v, layer, n_seq, seq_len):
    cos, sin = _rope_tables(seq_len)
    past = cache_k.shape[3]
    nqb = seq_len // ATT_BLOCK
    cache = pl.BlockSpec((None, None, ATT_KV_HEADS, past, ATT_HD), lambda b, q: (b, layer, 0, 0, 0))
    table = pl.BlockSpec((seq_len, 128), lambda b, q: (0, 0))
    return pl.pallas_call(
        functools.partial(_attn_lat_kernel, seq_len=seq_len),
        grid=(n_seq, nqb),
        in_specs=[pl.BlockSpec(memory_space=pltpu.SMEM),
                  pl.BlockSpec((seq_len, ATT_W), lambda b, q: (b, 0)),
                  cache, cache, table, table],
        out_specs=pl.BlockSpec((ATT_BLOCK, ATT_Q_W), lambda b, q: (b * nqb + q, 0)),
        out_shape=jax.ShapeDtypeStruct((n_seq * seq_len, ATT_Q_W), F32),
        compiler_params=pltpu.CompilerParams(dimension_semantics=("parallel", "parallel")),
        name="attn_latent",
    )(sink, proj_att, cache_k, cache_v, cos, sin)


def _rwkv_chunk_operands(x_ref, prm, gate, c, d, *, seq_len):
    (mu_ref, w0_ref, wup_ref, a0_ref, aup_ref, gup_ref, kk_ref, ka_ref, rk_ref) = prm
    ch = RWKV_CHUNK
    n_chunks = seq_len // ch
    hd = RWKV_HD
    rev = d == 1
    r0 = pl.multiple_of(c * ch, ch)
    rows = pl.ds(r0, ch)
    x = x_ref[rows, :]
    prev_row = x_ref[pl.ds(jnp.maximum(r0 - 1, 0), 1), :] * jnp.where(c > 0, 1.0, 0.0)
    next_row = x_ref[pl.ds(jnp.minimum(r0 + ch, seq_len - 1), 1), :] * jnp.where(c < n_chunks - 1, 1.0, 0.0)
    xp, xn = _shifted_rows(x, prev_row, next_row)
    xs = x + mu_ref[0:1, :] * (xp - x) + mu_ref[1:2, :] * (xn - x)
    r = xs[:, 0:RWKV_W]
    k = xs[:, RWKV_W:2 * RWKV_W]
    v = xs[:, 2 * RWKV_W:3 * RWKV_W]
    lo = 3 * RWKV_W
    wl = xs[:, lo + d * RWKV_LORA:lo + (d + 1) * RWKV_LORA]
    al = xs[:, lo + 2 * RWKV_LORA + d * RWKV_LORA:lo + 2 * RWKV_LORA + (d + 1) * RWKV_LORA]
    p = RWKV_PREC
    w_log = -_softplus(-(w0_ref[d:d + 1, :] + _mm(jnp.tanh(wl), wup_ref[d], p["lora"]))) - 0.5
    lw = -jnp.exp(w_log)
    a = jax.nn.sigmoid(a0_ref[d:d + 1, :] + _mm(al, aup_ref[d], p["lora"]))
    k2 = k * (1.0 + (a - 1.0) * ka_ref[...])
    kkv = k * kk_ref[...]
    if d == 0:
        gl = xs[:, lo + 4 * RWKV_LORA:lo + 6 * RWKV_LORA]
        gate[rows, :] = _mm(jax.nn.sigmoid(gl), gup_ref[...], p["gate"])
    incl, strict = _tri_masks(ch, rev)
    r2 = lax.broadcasted_iota(jnp.int32, (ch, 2 * ch), 0)
    c2 = lax.broadcasted_iota(jnp.int32, (ch, 2 * ch), 1) & (ch - 1)
    incl2, strict2 = (r2 <= c2, r2 < c2) if rev else (r2 >= c2, r2 > c2)
    g_inc = _mm(jnp.where(incl, 1.0, 0.0), lw, p["cumsum"])
    g_exc = g_inc - lw
    g_tot = jnp.sum(lw, axis=0, keepdims=True)
    r_dec = r * jnp.exp(g_inc)
    e_neg = jnp.exp(-g_inc)
    e_end = jnp.exp(g_tot - g_inc)
    e_exc = jnp.exp(g_exc)
    k_neg = k2 * e_neg
    k_end = k2 * e_end
    dec_tot = jnp.exp(g_tot)
    rkr = r * k2 * rk_ref[...]
    chains = []
    for h in range(RWKV_HEADS):
        hs = slice(h * hd, (h + 1) * hd)
        kk_h = kkv[:, hs]
        kap = kk_h * lax.rsqrt(jnp.sum(kk_h * kk_h, axis=-1, keepdims=True) + 1e-6)
        b_h = kap * a[:, hs]
        v_h = v[:, hs]
        chains.append(dict(
            d=d, h=h, rows=rows, cols=hs, incl2=incl2, strict2=strict2, v=v_h, dec=dec_tot[:, hs],
            left=jnp.concatenate([kap * e_exc[:, hs], r_dec[:, hs]], axis=0),
            right=jnp.concatenate([b_h * e_neg[:, hs], k_neg[:, hs]], axis=0),
            bk_end=jnp.concatenate([b_h * e_end[:, hs], k_end[:, hs]], axis=0),
            bonus=jnp.sum(rkr[:, hs], axis=-1, keepdims=True) * v_h))
    return chains


def _rwkv_step(x_ref, prm, st_ref, ysum, bon, gate, i, *, seq_len):
    ch = RWKV_CHUNK
    p = RWKV_PREC
    n_chunks = seq_len // ch
    cs = (_rwkv_chunk_operands(x_ref, prm, gate, i, 0, seq_len=seq_len)
          + _rwkv_chunk_operands(x_ref, prm, gate, n_chunks - 1 - i, 1, seq_len=seq_len))
    ms = [_mm_nt(c["left"], c["right"], p["amat"]) for c in cs]
    a_ks = [jnp.where(c["strict2"], m[:ch], 0.0) for c, m in zip(cs, ms)]
    a_rs = [jnp.where(c["incl2"], m[ch:], 0.0) for c, m in zip(cs, ms)]
    t_invs = _unit_tri_inv_many([a_k[:, :ch] for a_k in a_ks], p["inv"])
    ss = [st_ref[c["d"], c["h"]] for c in cs]
    lss = [_mm_nt(c["left"], s, p["state"]) for c, s in zip(cs, ss)]
    akvs = [_mm(a_k[:, ch:], c["v"], p["solve"]) for c, a_k in zip(cs, a_ks)]
    us = [-_mm(t, ls[:ch] + akv, p["solve"]) for t, ls, akv in zip(t_invs, lss, akvs)]
    uvs = [jnp.concatenate([u, c["v"]], axis=0) for c, u in zip(cs, us)]
    ys = [ls[ch:] + _mm(a_r, uv, p["out"]) for ls, a_r, uv in zip(lss, a_rs, uvs)]
    s_news = [s * c["dec"] + _mm_tn(uv, c["bk_end"], p["update"]) for c, s, uv in zip(cs, ss, uvs)]
    for c, y, s_new in zip(cs, ys, s_news):
        st_ref[c["d"], c["h"]] = s_new
        ysum[c["rows"], c["cols"]] += y
        bon[c["rows"], c["cols"]] += c["bonus"]


def _rwkv_kernel(*refs, seq_len, has_init):
    x_ref = refs[0]
    prm = refs[1:10]
    lnw_ref, lnb_ref = refs[10:12]
    pos = 12
    if has_init:
        s0f_ref, s0b_ref = refs[pos:pos + 2]
        pos += 2
    o_ref, sf_ref, sb_ref = refs[pos:pos + 3]
    ysum, bon, gate, st_ref = refs[pos + 3:]
    ch = RWKV_CHUNK
    n_chunks = seq_len // ch
    ysum[...] = jnp.zeros(ysum.shape, F32)
    bon[...] = jnp.zeros(bon.shape, F32)
    if has_init:
        st_ref[0] = s0f_ref[...]
        st_ref[1] = s0b_ref[...]
    else:
        st_ref[...] = jnp.zeros(st_ref.shape, F32)

    def step(i, carry):
        _rwkv_step(x_ref, prm, st_ref, ysum, bon, gate, i, seq_len=seq_len)
        return carry

    lax.fori_loop(0, n_chunks, step, 0)
    sf_ref[...] = st_ref[0]
    sb_ref[...] = st_ref[1]

    def finish(i, carry):
        rows = pl.ds(pl.multiple_of(i * ch, ch), ch)
        y = ysum[rows, :]
        parts = []
        for h in range(RWKV_HEADS):
            y_h = y[:, h * RWKV_HD:(h + 1) * RWKV_HD]
            mean = jnp.mean(y_h, axis=-1, keepdims=True)
            cen = y_h - mean
            var = jnp.mean(cen * cen, axis=-1, keepdims=True)
            parts.append(cen * lax.rsqrt(var + GN_EPS))
        yn = jnp.concatenate(parts, axis=1) * lnw_ref[...] + lnb_ref[...]
        o_ref[rows, :] = (yn + bon[rows, :]) * gate[rows, :]
        return carry

    lax.fori_loop(0, n_chunks, finish, 0)


def _rwkv_mixer(x_rw, n_seq, seq_len, params, s0_f, s0_b):
    has_init = s0_f is not None
    (mu, w0, w_up, a0, a_up, g_up, k_k, k_a, r_k, ln_w, ln_b) = params
    row = lambda a: a.reshape(1, RWKV_W)
    args = [x_rw, mu, w0, w_up, a0, a_up, g_up, row(k_k), row(k_a), row(r_k), row(ln_w), row(ln_b)]

    def whole(a):
        nd = a.ndim
        return pl.BlockSpec(a.shape, lambda s, nd=nd: (0,) * nd)

    in_specs = [pl.BlockSpec((seq_len, RWKV_IN), lambda s: (s, 0))] + [whole(a) for a in args[1:]]
    state = pl.BlockSpec((None, None, RWKV_HEADS, RWKV_HD, RWKV_HD), lambda s: (s, 0, 0, 0, 0))
    if has_init:
        in_specs += [state, state]
        args += [s0_f, s0_b]
    st_shape = jax.ShapeDtypeStruct((n_seq, 1, RWKV_HEADS, RWKV_HD, RWKV_HD), F32)
    return pl.pallas_call(
        functools.partial(_rwkv_kernel, seq_len=seq_len, has_init=has_init),
        grid=(n_seq,),
        in_specs=in_specs,
        out_specs=[pl.BlockSpec((seq_len, RWKV_W), lambda s: (s, 0)), state, state],
        out_shape=[jax.ShapeDtypeStruct((n_seq * seq_len, RWKV_W), F32), st_shape, st_shape],
        scratch_shapes=[pltpu.VMEM((seq_len, RWKV_W), F32) for _ in range(3)]
        + [pltpu.VMEM((2, RWKV_HEADS, RWKV_HD, RWKV_HD), F32)],
        compiler_params=pltpu.CompilerParams(dimension_semantics=("parallel",), vmem_limit_bytes=VMEM_LIMIT),
        name="rwkv_mixer",
    )(*args)


def kernel(x_prompt, x_sample, state_gdn_fwd, state_gdn_bwd, cache_attn_k, cache_attn_v, state_rwkv_fwd, state_rwkv_bwd, c, c_ctx, mod_w, mod_b, norm_mix, norm_mlp, mlp_w1, mlp_w2, norm_final, ev_w_in, ev_w_out, gdn_conv, gdn_a_log, gdn_dt_bias, gdn_norm, sc_conv, od_w_in, od_w_out, attn_sink, rwkv_mu, rwkv_w0, rwkv_w_up, rwkv_a0, rwkv_a_up, rwkv_g_up, rwkv_k_k, rwkv_k_a, rwkv_r_k, rwkv_ln_w, rwkv_ln_b):
    bp, lp, _ = x_prompt.shape
    bs, ls, _ = x_sample.shape
    depth = mod_w.shape[0]
    c_rows = jnp.concatenate([c_ctx[None, :], c, jnp.zeros((MOD_ROWS - 1 - bs, D_MODEL), F32)], axis=0)
    mods = _modulation(c_rows, mod_w, mod_b)

    groups = [
        dict(x=x_prompt.reshape(bp * lp, D_MODEL), n=bp, l=lp, latent=False,
             mod=lambda layer: _mod_spec(layer, lp // TOKEN_TILE, 0, 0)),
        dict(x=x_sample.reshape(bs * ls, D_MODEL), n=bs, l=ls, latent=True,
             mod=lambda layer: _mod_spec(layer, ls // TOKEN_TILE, 1, 1)),
    ]
    outs = {}
    for layer in range(depth):
        w1 = mlp_w1[layer].astype(BF16)
        w2 = mlp_w2[layer].astype(BF16)
        final = layer == depth - 1
        if layer % 2 == 0:
            e = layer // 2
            w = ev_w_in[e]
            qkvz = GDN_QKV_W + GDN_HEADS * GDN_D
            n_gate = 4 * GDN_HEADS
            w_in = jnp.concatenate(
                [w[:, :qkvz], w[:, qkvz + n_gate:], w[:, qkvz:qkvz + n_gate],
                 jnp.zeros((D_MODEL, EV_IN_PAD - w.shape[1]), F32)], axis=1).astype(BF16)
            w_out = ev_w_out[e].astype(BF16)
            alog_vec = jnp.zeros((1, 128), F32).at[0, 2 * GDN_HEADS:4 * GDN_HEADS].set(gdn_a_log[e].reshape(-1))
            dtb_vec = jnp.zeros((1, 128), F32).at[0, 2 * GDN_HEADS:4 * GDN_HEADS].set(gdn_dt_bias[e].reshape(-1))
            for grp in groups:
                (proj,) = _inproj(grp["x"], mods, norm_mix[layer], w_in, (EV_IN_PAD,), grp["mod"](layer))
                s0 = (state_gdn_fwd[:, e:e + 1], state_gdn_bwd[:, e:e + 1]) if grp["latent"] else (None, None)
                o, sc, s_f, s_b = _gdn_mixer(proj, grp["n"], grp["l"], gdn_conv[e], sc_conv[e], gdn_norm[e],
                                             alog_vec, dtb_vec, *s0)
                if not grp["latent"]:
                    outs.setdefault("gdn_f", []).append(s_f)
                    outs.setdefault("gdn_b", []).append(s_b)
                grp["x"] = _outproj_mlp(o, sc, grp["x"], mods, norm_mlp[layer], w_out, w1, w2, norm_final,
                                        grp["mod"](layer), final)
        else:
            o_ = layer // 2
            w = od_w_in[o_]
            w_in = jnp.concatenate([w[:, ATT_W:], w[:, :ATT_W]], axis=1).astype(BF16)
            w_out = od_w_out[o_].astype(BF16)
            rw = (rwkv_mu[o_], rwkv_w0[o_], rwkv_w_up[o_], rwkv_a0[o_], rwkv_a_up[o_], rwkv_g_up[o_],
                  rwkv_k_k[o_], rwkv_k_a[o_], rwkv_r_k[o_].reshape(-1), rwkv_ln_w[o_], rwkv_ln_b[o_])
            for grp in groups:
                x_rw, p_att = _inproj(grp["x"], mods, norm_mix[layer], w_in, (RWKV_IN, ATT_W), grp["mod"](layer))
                if grp["latent"]:
                    att = _attn_latent(p_att, attn_sink[o_], cache_attn_k, cache_attn_v, o_, grp["n"], grp["l"])
                    rwo, _, _ = _rwkv_mixer(x_rw, grp["n"], grp["l"], rw,
                                            state_rwkv_fwd[:, o_:o_ + 1], state_rwkv_bwd[:, o_:o_ + 1])
                else:
                    att, kc, vc = _attn_context(p_att, attn_sink[o_], grp["n"], grp["l"])
                    rwo, s_f, s_b = _rwkv_mixer(x_rw, grp["n"], grp["l"], rw, None, None)
                    outs.setdefault("att_k", []).append(kc)
                    outs.setdefault("att_v", []).append(vc)
                    outs.setdefault("rw_f", []).append(s_f)
                    outs.setdefault("rw_b", []).append(s_b)
                grp["x"] = _outproj_mlp(att, rwo, grp["x"], mods, norm_mlp[layer], w_out, w1, w2, norm_final,
                                        grp["mod"](layer), final)
    cat = lambda key: jnp.concatenate(outs[key], axis=1)
    return (groups[0]["x"].reshape(bp, lp, D_MODEL), groups[1]["x"].reshape(bs, ls, D_MODEL),
            cat("gdn_f"), cat("gdn_b"), cat("att_k"), cat("att_v"), cat("rw_f"), cat("rw_b"))
```

```python
import functools

import jax
import jax.numpy as jnp
import numpy as np
from jax import lax
from jax.experimental import pallas as pl
from jax.experimental.pallas import tpu as pltpu

F32 = jnp.float32
BF16 = jnp.bfloat16
HIGHEST = lax.Precision.HIGHEST

D_MODEL = 1024
N_MOD = 6
D_FF = 4 * D_MODEL
NORM_EPS = 1e-6
TOKEN_TILE = 256
MOD_ROWS = 8

GDN_HEADS = 4
GDN_D = 128
GDN_CHUNK = 128
GDN_GROUP = 8
GDN_BLOCK_ROWS = 1024
GDN_QKV_W = 3 * GDN_HEADS * GDN_D
SC_WIDTH = 512
EV_IN_PAD = 3712

ATT_HEADS = 8
ATT_KV_HEADS = 2
ATT_GROUP = ATT_HEADS // ATT_KV_HEADS
ATT_HD = 64
ATT_Q_W = ATT_HEADS * ATT_HD
ATT_KV_W = ATT_KV_HEADS * ATT_HD
ATT_W = ATT_Q_W + 2 * ATT_KV_W
WINDOW = 128
ATT_BLOCK = 128
GRID_W = 64
ROPE_BASE = 10000.0
NEG_INF = -1e30

RWKV_HEADS = 8
RWKV_HD = 64
RWKV_W = RWKV_HEADS * RWKV_HD
RWKV_LORA = 64
RWKV_IN = 3 * RWKV_W + 3 * 2 * RWKV_LORA
RWKV_CHUNK = 64
GN_EPS = 64e-5

VMEM_LIMIT = 56 * 1024 * 1024

RWKV_PREC = dict(lora="bf", gate="bf", cumsum="x2r", amat="bf", inv="bf", state="bf", akv="bf", solve="x2r",
                 out="bf", update="bf")


def _split_bf16(a):
    hi = a.astype(BF16)
    return hi, (a - hi.astype(F32)).astype(BF16)


def _dot(a, b, dims, prec):
    dn = (dims, ((), ()))
    if prec == "hi":
        return lax.dot_general(a, b, dn, precision=HIGHEST, preferred_element_type=F32)
    one = lambda x, y: lax.dot_general(x, y, dn, preferred_element_type=F32)
    if prec == "x3":
        ah, al = _split_bf16(a)
        bh, bl = _split_bf16(b)
        return one(ah, bh) + one(ah, bl) + one(al, bh)
    if prec == "x2l":
        ah, al = _split_bf16(a)
        bh = b.astype(BF16)
        return one(ah, bh) + one(al, bh)
    if prec == "x2r":
        ah = a.astype(BF16)
        bh, bl = _split_bf16(b)
        return one(ah, bh) + one(ah, bl)
    assert prec == "bf", prec
    return lax.dot_general(a.astype(BF16), b.astype(BF16), dn, preferred_element_type=F32)


def _mm(a, b, prec="hi"):
    return _dot(a, b, ((1,), (0,)), prec)


def _mm_nt(a, b, prec="hi"):
    return _dot(a, b, ((1,), (1,)), prec)


def _mm_tn(a, b, prec="hi"):
    return _dot(a, b, ((0,), (0,)), prec)


def _silu(x):
    return x * jax.nn.sigmoid(x)


def _softplus(x):
    return jnp.maximum(x, 0.0) + jnp.log1p(jnp.exp(-jnp.abs(x)))


def _rms(x, w):
    return x * lax.rsqrt(jnp.mean(x * x, axis=-1, keepdims=True) + NORM_EPS) * w


def _tri_masks(n, rev):
    r = lax.broadcasted_iota(jnp.int32, (n, n), 0)
    c = lax.broadcasted_iota(jnp.int32, (n, n), 1)
    if rev:
        return r <= c, r < c
    return r >= c, r > c


def _unit_tri_inv_many(mats, prec):
    n = mats[0].shape[0]
    r = lax.broadcasted_iota(jnp.int32, (n, n), 0)
    c = lax.broadcasted_iota(jnp.int32, (n, n), 1)
    eye = jnp.where(r == c, 1.0, 0.0)
    ts = [eye - jnp.where((r >> 1) == (c >> 1), a, 0.0) for a in mats]
    shift = 1
    while (1 << shift) < n:
        join = ((r >> (shift + 1)) == (c >> (shift + 1))) & ((r >> shift) != (c >> shift))
        inner = [_mm(jnp.where(join, a, 0.0), t, prec) for a, t in zip(mats, ts)]
        ts = [t - _mm(t, w, prec) for t, w in zip(ts, inner)]
        shift += 1
    return ts


def _unit_tri_inv(a, prec):
    return _unit_tri_inv_many([a], prec)[0]


def _shifted_rows(x, prev_row, next_row):
    n = x.shape[0]
    row = lax.broadcasted_iota(jnp.int32, x.shape, 0)
    xp = jnp.where(row == 0, prev_row, pltpu.roll(x, 1, 0))
    xn = jnp.where(row == n - 1, next_row, pltpu.roll(x, n - 1, 0))
    return xp, xn


def _conv3(x, w_ref, seq_len):
    n = x.shape[0]
    assert seq_len & (seq_len - 1) == 0 and n % seq_len == 0
    pos = lax.broadcasted_iota(jnp.int32, x.shape, 0) & (seq_len - 1)
    xp = jnp.where(pos == 0, 0.0, pltpu.roll(x, 1, 0))
    xn = jnp.where(pos == seq_len - 1, 0.0, pltpu.roll(x, n - 1, 0))
    return xp * w_ref[0:1, :] + x * w_ref[1:2, :] + xn * w_ref[2:3, :]


def _mod_kernel(c_ref, w_ref, b_ref, o_ref):
    s = _silu(c_ref[...])
    o_ref[...] = _mm(s, w_ref[...], prec="bf") + b_ref[...]


def _modulation(c_rows, mod_w, mod_b):
    depth = mod_w.shape[0]
    nblk = (N_MOD * D_MODEL) // D_MODEL
    out = pl.pallas_call(
        _mod_kernel,
        grid=(depth, nblk),
        in_specs=[
            pl.BlockSpec((MOD_ROWS, D_MODEL), lambda l, j: (0, 0)),
            pl.BlockSpec((None, D_MODEL, D_MODEL), lambda l, j: (l, 0, j)),
            pl.BlockSpec((None, 1, D_MODEL), lambda l, j: (l, 0, j)),
        ],
        out_specs=pl.BlockSpec((None, MOD_ROWS, D_MODEL), lambda l, j: (l, 0, j)),
        out_shape=jax.ShapeDtypeStruct((depth, MOD_ROWS, N_MOD * D_MODEL), F32),
        compiler_params=pltpu.CompilerParams(dimension_semantics=("parallel", "parallel")),
        name="modulation",
    )(c_rows, mod_w, mod_b.reshape(depth, 1, N_MOD * D_MODEL))
    return out.reshape(depth, MOD_ROWS, N_MOD, D_MODEL)


def _mod_spec(layer, tiles_per_seq, row_base, row_step):
    return pl.BlockSpec((None, None, N_MOD, D_MODEL),
                        lambda i: (layer, row_base + (i // tiles_per_seq) * row_step, 0, 0))


def _inproj_kernel(x_ref, mod_ref, nw_ref, w_ref, *o_refs):
    h = _rms(x_ref[...], nw_ref[...])
    h = h * (1.0 + mod_ref[1:2, :]) + mod_ref[0:1, :]
    y = _mm(h, w_ref[...], prec="bf")
    off = 0
    for o_ref in o_refs:
        n = o_ref.shape[-1]
        o_ref[...] = y[:, off:off + n]
        off += n


def _inproj(x, mods, norm_w, w_bf16, splits, mod_spec):
    t = x.shape[0]
    n_in = w_bf16.shape[1]
    return pl.pallas_call(
        _inproj_kernel,
        grid=(t // TOKEN_TILE,),
        in_specs=[
            pl.BlockSpec((TOKEN_TILE, D_MODEL), lambda i: (i, 0)),
            mod_spec,
            pl.BlockSpec((1, D_MODEL), lambda i: (0, 0)),
            pl.BlockSpec((D_MODEL, n_in), lambda i: (0, 0)),
        ],
        out_specs=[pl.BlockSpec((TOKEN_TILE, n), lambda i: (i, 0)) for n in splits],
        out_shape=[jax.ShapeDtypeStruct((t, n), F32) for n in splits],
        compiler_params=pltpu.CompilerParams(dimension_semantics=("parallel",), vmem_limit_bytes=VMEM_LIMIT),
        name="inproj",
    )(x, mods, norm_w.reshape(1, D_MODEL), w_bf16)


def _mlp_kernel(a_ref, b_ref, x_ref, mod_ref, nw_ref, woa_ref, wob_ref, w1_ref, w2_ref, nf_ref, o_ref, *, final):
    y = _mm(a_ref[...], woa_ref[...], prec="bf") + _mm(b_ref[...], wob_ref[...], prec="bf")
    x1 = x_ref[...] + mod_ref[2:3, :] * y
    h = _rms(x1, nw_ref[...])
    h = (h * (1.0 + mod_ref[4:5, :]) + mod_ref[3:4, :]).astype(BF16)
    acc = jnp.zeros(x1.shape, F32)
    for j in range(D_FF // D_MODEL):
        cols = slice(j * D_MODEL, (j + 1) * D_MODEL)
        u = jnp.maximum(_mm(h, w1_ref[:, cols], prec="bf"), 0.0)
        acc = acc + _mm(u * u, w2_ref[cols, :], prec="bf")
    x2 = x1 + mod_ref[5:6, :] * acc
    if final:
        x2 = _rms(x2, nf_ref[...])
    o_ref[...] = x2


def _outproj_mlp(a, b, x, mods, norm_w, w_out, w1, w2, norm_final, mod_spec, final):
    t = x.shape[0]
    half = a.shape[1]
    const = lambda i: (0, 0)
    return pl.pallas_call(
        functools.partial(_mlp_kernel, final=final),
        grid=(t // TOKEN_TILE,),
        in_specs=[
            pl.BlockSpec((TOKEN_TILE, half), lambda i: (i, 0)),
            pl.BlockSpec((TOKEN_TILE, half), lambda i: (i, 0)),
            pl.BlockSpec((TOKEN_TILE, D_MODEL), lambda i: (i, 0)),
            mod_spec,
            pl.BlockSpec((1, D_MODEL), const),
            pl.BlockSpec((half, D_MODEL), const),
            pl.BlockSpec((half, D_MODEL), lambda i: (1, 0)),
            pl.BlockSpec((D_MODEL, D_FF), const),
            pl.BlockSpec((D_FF, D_MODEL), const),
            pl.BlockSpec((1, D_MODEL), const),
        ],
        out_specs=pl.BlockSpec((TOKEN_TILE, D_MODEL), lambda i: (i, 0)),
        out_shape=jax.ShapeDtypeStruct((t, D_MODEL), F32),
        compiler_params=pltpu.CompilerParams(dimension_semantics=("parallel",), vmem_limit_bytes=VMEM_LIMIT),
        name="outproj_mlp",
    )(a, b, x, mods, norm_w.reshape(1, D_MODEL), w_out, w_out, w1, w2, norm_final.reshape(1, D_MODEL))


def _gdn_decay_terms(g, rev):
    c = g.shape[0]
    incl = _tri_masks(c, rev)[0]
    before_col = _tri_masks(c, not rev)[0]
    eye = lax.broadcasted_iota(jnp.int32, (c, c), 0) == lax.broadcasted_iota(jnp.int32, (c, c), 1)
    gc_row = jnp.sum(jnp.where(before_col, jnp.broadcast_to(g, (c, c)), 0.0), axis=0, keepdims=True)
    gc_col = jnp.sum(jnp.where(eye, jnp.broadcast_to(gc_row, (c, c)), 0.0), axis=1, keepdims=True)
    decay = jnp.where(incl, jnp.exp(jnp.where(incl, gc_col - gc_row, 0.0)), 0.0)
    g_tot = jnp.sum(g, axis=0, keepdims=True)
    return decay, jnp.exp(gc_col), jnp.exp(g_tot - gc_col), jnp.exp(g_tot)


def _gdn_kernel(*refs, seq_len, seqs, has_init):
    (q_ref, k_ref, v_ref, z_ref, scb_ref, scc_ref, sch_ref, gate_ref,
     cq_ref, ck_ref, cv_ref, csc_ref, gn_ref, alog_ref, dtb_ref) = refs[:15]
    pos = 15
    if has_init:
        s0f_ref, s0b_ref = refs[pos:pos + 2]
        pos += 2
    o_ref, sco_ref, sf_ref, sb_ref = refs[pos:pos + 4]
    qs, ks, vs, osum, betas, gs, u_s, w_s, qd_s, kd_s, in_s, ge_s, st = refs[pos + 4:]
    head = pl.program_id(1)
    ch = GDN_CHUNK
    n_chunks = seq_len // ch

    def conv_silu(x_ref, c_ref):
        return _silu(_conv3(x_ref[...], c_ref, seq_len))

    def l2norm(x):
        return x * lax.rsqrt(jnp.sum(x * x, axis=-1, keepdims=True) + 1e-6)

    qs[...] = l2norm(conv_silu(q_ref, cq_ref)) * (GDN_D ** -0.5)
    ks[...] = l2norm(conv_silu(k_ref, ck_ref))
    vs[...] = conv_silu(v_ref, cv_ref)
    gates = gate_ref[...]
    betas[...] = jax.nn.sigmoid(gates)
    gs[...] = -jnp.exp(alog_ref[...]) * _softplus(gates + dtb_ref[...])
    sco_ref[...] = scb_ref[...] * _conv3(scc_ref[...] * sch_ref[...], csc_ref, seq_len)
    osum[...] = jnp.zeros(osum.shape, F32)

    lane = lax.broadcasted_iota(jnp.int32, (ch, 128), 1)

    def pick(ref, rows, col):
        return jnp.sum(jnp.where(lane == col, ref[rows, :], 0.0), axis=1, keepdims=True)

    def solve_group(gi, carry):
        items = []
        for j in range(GDN_GROUP):
            c = gi * GDN_GROUP + j
            rows = pl.ds(pl.multiple_of(c * ch, ch), ch)
            items.append(dict(c=c, rows=rows, q=qs[rows, :], k=ks[rows, :], v=vs[rows, :]))
        kks = [_mm_nt(it["k"], it["k"], "bf") for it in items]
        qks = [_mm_nt(it["q"], it["k"], "bf") for it in items]
        subs = []
        for it, kk, qk in zip(items, kks, qks):
            for d in range(2):
                beta = pick(betas, it["rows"], d * GDN_HEADS + head)
                g = pick(gs, it["rows"], 2 * GDN_HEADS + d * GDN_HEADS + head)
                decay, e_gc, e_rest, e_tot = _gdn_decay_terms(g, rev=(d == 1))
                strict = _tri_masks(ch, d == 1)[1]
                subs.append(dict(
                    d=d, c=it["c"], rows=it["rows"],
                    a=jnp.where(strict, kk * beta * decay, 0.0),
                    rhs=jnp.concatenate([it["v"] * beta, it["k"] * (beta * e_gc)], axis=1),
                    intra=qk * decay, qd=it["q"] * e_gc, kd=it["k"] * e_rest, ge=e_tot))
        ts = _unit_tri_inv_many([s["a"] for s in subs], "bf")
        uws = [_mm(t, s["rhs"], "bf") for t, s in zip(ts, subs)]
        for s, uw in zip(subs, uws):
            d, rows = s["d"], s["rows"]
            u_s[d, rows, :] = uw[:, :GDN_D]
            w_s[d, rows, :] = uw[:, GDN_D:].astype(BF16)
            qd_s[d, rows, :] = s["qd"].astype(BF16)
            kd_s[d, rows, :] = s["kd"].astype(BF16)
            in_s[d, rows, :] = s["intra"].astype(BF16)
            ge_s[d, pl.ds(pl.multiple_of(s["c"] * 8, 8), 8), :] = jnp.broadcast_to(s["ge"], (8, 128))
        return carry

    lax.fori_loop(0, seqs * n_chunks // GDN_GROUP, solve_group, 0)

    for j in range(seqs):
        st[0, j] = s0f_ref[j] if has_init else jnp.zeros((GDN_D, GDN_D), F32)
        st[1, j] = s0b_ref[j] if has_init else jnp.zeros((GDN_D, GDN_D), F32)

    def recur(i, carry):
        cs = []
        for j in range(seqs):
            for d in range(2):
                c = j * n_chunks + (i if d == 0 else n_chunks - 1 - i)
                cs.append(dict(d=d, j=j, rows=pl.ds(pl.multiple_of(c * ch, ch), ch),
                               ge=ge_s[d, pl.ds(pl.multiple_of(c * 8, 8), 1), :], s=st[d, j]))
        sbs = [c["s"].astype(BF16) for c in cs]
        wss = [_mm(w_s[c["d"], c["rows"], :], sb, "bf") for c, sb in zip(cs, sbs)]
        qss = [_mm(qd_s[c["d"], c["rows"], :], sb, "bf") for c, sb in zip(cs, sbs)]
        ebs = [(u_s[c["d"], c["rows"], :] - ws).astype(BF16) for c, ws in zip(cs, wss)]
        outs = [qs_ + _mm(in_s[c["d"], c["rows"], :], eb, "bf") for c, qs_, eb in zip(cs, qss, ebs)]
        s_news = [c["s"] * c["ge"] + _mm_tn(kd_s[c["d"], c["rows"], :], eb, "bf") for c, eb in zip(cs, ebs)]
        for c, o, s_new in zip(cs, outs, s_news):
            st[c["d"], c["j"]] = s_new
            osum[c["rows"], :] += o
        return carry

    lax.fori_loop(0, n_chunks, recur, 0)
    sf_ref[...] = st[0]
    sb_ref[...] = st[1]
    o_ref[...] = _rms(osum[...], gn_ref[...]) * _silu(z_ref[...])


def _gdn_mixer(proj, n_seq, seq_len, conv_w, sc_conv_w, gdn_norm, alog_vec, dtb_vec, s0_f, s0_b):
    has_init = s0_f is not None
    hd = GDN_HEADS
    seqs = max(1, GDN_BLOCK_ROWS // seq_len)
    rows = seqs * seq_len
    assert n_seq % seqs == 0 and (rows // GDN_CHUNK) % GDN_GROUP == 0

    def col(block):
        return pl.BlockSpec((rows, 128), lambda s, h, b=block: (s, b * hd + h))

    def wcol(block):
        return pl.BlockSpec((3, 128), lambda s, h, b=block: (0, b * hd + h))

    vec = pl.BlockSpec((1, 128), lambda s, h: (0, 0))
    state = pl.BlockSpec((seqs, None, None, GDN_D, GDN_D), lambda s, h: (s, 0, h, 0, 0))
    in_specs = [col(0), col(1), col(2), col(3), col(4), col(5), col(6),
                pl.BlockSpec((rows, 128), lambda s, h: (s, 7 * hd)),
                wcol(0), wcol(1), wcol(2), wcol(0), vec, vec, vec]
    args = [proj] * 8 + [conv_w, conv_w, conv_w, sc_conv_w, gdn_norm.reshape(1, 128), alog_vec, dtb_vec]
    if has_init:
        in_specs += [state, state]
        args += [s0_f, s0_b]
    t = n_seq * seq_len
    out_tok = pl.BlockSpec((rows, 128), lambda s, h: (s, h))
    scratch = ([pltpu.VMEM((rows, 128), F32) for _ in range(6)]
               + [pltpu.VMEM((2, rows, GDN_D), F32)]
               + [pltpu.VMEM((2, rows, GDN_D), BF16) for _ in range(3)]
               + [pltpu.VMEM((2, rows, GDN_CHUNK), BF16),
                  pltpu.VMEM((2, rows // GDN_CHUNK * 8, 128), F32),
                  pltpu.VMEM((2, seqs, GDN_D, GDN_D), F32)])
    return pl.pallas_call(
        functools.partial(_gdn_kernel, seq_len=seq_len, seqs=seqs, has_init=has_init),
        grid=(n_seq // seqs, hd),
        in_specs=in_specs,
        out_specs=[out_tok, out_tok, state, state],
        out_shape=[jax.ShapeDtypeStruct((t, hd * GDN_D), F32), jax.ShapeDtypeStruct((t, SC_WIDTH), F32),
                   jax.ShapeDtypeStruct((n_seq, 1, hd, GDN_D, GDN_D), F32),
                   jax.ShapeDtypeStruct((n_seq, 1, hd, GDN_D, GDN_D), F32)],
        scratch_shapes=scratch,
        compiler_params=pltpu.CompilerParams(dimension_semantics=("parallel", "parallel"),
                                             vmem_limit_bytes=VMEM_LIMIT),
        name="gdn_mixer",
    )(*args)


def _softmax_pv(scores, values, sink):
    m = sink
    for s in scores:
        m = jnp.maximum(m, jnp.max(s, axis=-1, keepdims=True))
    den = jnp.exp(sink - m)
    acc = None
    for s, v in zip(scores, values):
        e = jnp.exp(s - m)
        den = den + jnp.sum(e, axis=-1, keepdims=True)
        pv = _mm(e, v, prec="bf")
        acc = pv if acc is None else acc + pv
    return acc / den


def _attn_ctx_kernel(sink_ref, p_ref, o_ref, kc_ref, vc_ref):
    scale = ATT_HD ** -0.5
    for j in range(ATT_KV_HEADS):
        k = p_ref[:, ATT_Q_W + j * ATT_HD:ATT_Q_W + (j + 1) * ATT_HD]
        v = p_ref[:, ATT_Q_W + ATT_KV_W + j * ATT_HD:ATT_Q_W + ATT_KV_W + (j + 1) * ATT_HD]
        kc_ref[j] = k
        vc_ref[j] = v
        for gi in range(ATT_GROUP):
            hh = j * ATT_GROUP + gi
            q = p_ref[:, hh * ATT_HD:(hh + 1) * ATT_HD]
            s = _mm_nt(q, k, prec="bf") * scale
            sink = jnp.full((1, 1), sink_ref[hh], F32)
            o_ref[:, hh * ATT_HD:(hh + 1) * ATT_HD] = _softmax_pv([s], [v], sink)


def _attn_context(proj_att, sink, n_seq, seq_len):
    kv = pl.BlockSpec((None, None, ATT_KV_HEADS, seq_len, ATT_HD), lambda b: (b, 0, 0, 0, 0))
    return pl.pallas_call(
        _attn_ctx_kernel,
        grid=(n_seq,),
        in_specs=[pl.BlockSpec(memory_space=pltpu.SMEM),
                  pl.BlockSpec((seq_len, ATT_W), lambda b: (b, 0))],
        out_specs=[pl.BlockSpec((seq_len, ATT_Q_W), lambda b: (b, 0)), kv, kv],
        out_shape=[jax.ShapeDtypeStruct((n_seq * seq_len, ATT_Q_W), F32),
                   jax.ShapeDtypeStruct((n_seq, 1, ATT_KV_HEADS, seq_len, ATT_HD), F32),
                   jax.ShapeDtypeStruct((n_seq, 1, ATT_KV_HEADS, seq_len, ATT_HD), F32)],
        compiler_params=pltpu.CompilerParams(dimension_semantics=("parallel",)),
        name="attn_context",
    )(sink, proj_att)


def _rope_tables(seq_len):
    pos = np.arange(seq_len)
    half = ATT_HD // 2
    inv = ROPE_BASE ** (-np.arange(0, half, 2, dtype=np.float32) / half)
    ang_r = (pos // GRID_W).astype(np.float32)[:, None] * inv
    ang_c = (pos % GRID_W).astype(np.float32)[:, None] * inv
    cos = np.concatenate([np.cos(ang_r), np.cos(ang_r), np.cos(ang_c), np.cos(ang_c)], axis=1)
    sin = np.concatenate([-np.sin(ang_r), np.sin(ang_r), -np.sin(ang_c), np.sin(ang_c)], axis=1)
    return (jnp.asarray(np.tile(cos, (1, 2)), F32), jnp.asarray(np.tile(sin, (1, 2)), F32))


def _rope(x, cos, sin):
    lane = lax.broadcasted_iota(jnp.int32, x.shape, 1)
    partner = jnp.where((lane & 31) < 16, pltpu.roll(x, 128 - 16, 1), pltpu.roll(x, 16, 1))
    return x * cos + partner * sin


def _attn_lat_kernel(sink_ref, p_ref, ck_ref, cv_ref, cos_ref, sin_ref, o_ref, *, seq_len):
    scale = ATT_HD ** -0.5
    qb = pl.program_id(1)
    span = 3 * ATT_BLOCK
    q0 = pl.multiple_of(qb * ATT_BLOCK, ATT_BLOCK)
    k0 = pl.multiple_of(jnp.clip((qb - 1) * ATT_BLOCK, 0, seq_len - span), ATT_BLOCK)
    qrows = pl.ds(q0, ATT_BLOCK)
    krows = pl.ds(k0, span)
    kwin = _rope(p_ref[krows, ATT_Q_W:ATT_Q_W + ATT_KV_W], cos_ref[krows, :], sin_ref[krows, :])
    vwin = p_ref[krows, ATT_Q_W + ATT_KV_W:ATT_W]
    qpos = q0 + lax.broadcasted_iota(jnp.int32, (ATT_BLOCK, span), 0)
    kpos = k0 + lax.broadcasted_iota(jnp.int32, (ATT_BLOCK, span), 1)
    valid = jnp.abs(qpos - kpos) <= WINDOW
    cos_q = cos_ref[qrows, :]
    sin_q = sin_ref[qrows, :]
    for pair in range(ATT_HEADS // 2):
        qpair = _rope(p_ref[qrows, pair * 128:(pair + 1) * 128], cos_q, sin_q)
        for sub in range(2):
            hh = pair * 2 + sub
            j = hh // ATT_GROUP
            q = qpair[:, sub * ATT_HD:(sub + 1) * ATT_HD]
            s_loc = _mm_nt(q, kwin[:, j * ATT_HD:(j + 1) * ATT_HD], prec="bf") * scale
            s_loc = jnp.where(valid, s_loc, NEG_INF)
            s_ctx = _mm_nt(q, ck_ref[j], prec="bf") * scale
            sink = jnp.full((1, 1), sink_ref[hh], F32)
            o_ref[:, hh * ATT_HD:(hh + 1) * ATT_HD] = _softmax_pv(
                [s_loc, s_ctx], [vwin[:, j * ATT_HD:(j + 1) * ATT_HD], cv_ref[j]], sink)


def _attn_latent(proj_att, sink, cache_k, cache_v, layer, n_seq, seq_len):
    cos, sin = _rope_tables(seq_len)
    past = cache_k.shape[3]
    nqb = seq_len // ATT_BLOCK
    cache = pl.BlockSpec((None, None, ATT_KV_HEADS, past, ATT_HD), lambda b, q: (b, layer, 0, 0, 0))
    table = pl.BlockSpec((seq_len, 128), lambda b, q: (0, 0))
    return pl.pallas_call(
        functools.partial(_attn_lat_kernel, seq_len=seq_len),
        grid=(n_seq, nqb),
        in_specs=[pl.BlockSpec(memory_space=pltpu.SMEM),
                  pl.BlockSpec((seq_len, ATT_W), lambda b, q: (b, 0)),
                  cache, cache, table, table],
        out_specs=pl.BlockSpec((ATT_BLOCK, ATT_Q_W), lambda b, q: (b * nqb + q, 0)),
        out_shape=jax.ShapeDtypeStruct((n_seq * seq_len, ATT_Q_W), F32),
        compiler_params=pltpu.CompilerParams(dimension_semantics=("parallel", "parallel")),
        name="attn_latent",
    )(sink, proj_att, cache_k, cache_v, cos, sin)


def _rwkv_chunk_operands(x_ref, prm, gate, c, d, *, seq_len):
    (mu_ref, w0_ref, wup_ref, a0_ref, aup_ref, gup_ref, kk_ref, ka_ref, rk_ref) = prm
    ch = RWKV_CHUNK
    n_chunks = seq_len // ch
    hd = RWKV_HD
    rev = d == 1
    r0 = pl.multiple_of(c * ch, ch)
    rows = pl.ds(r0, ch)
    x = x_ref[rows, :]
    prev_row = x_ref[pl.ds(jnp.maximum(r0 - 1, 0), 1), :] * jnp.where(c > 0, 1.0, 0.0)
    next_row = x_ref[pl.ds(jnp.minimum(r0 + ch, seq_len - 1), 1), :] * jnp.where(c < n_chunks - 1, 1.0, 0.0)
    xp, xn = _shifted_rows(x, prev_row, next_row)
    xs = x + mu_ref[0:1, :] * (xp - x) + mu_ref[1:2, :] * (xn - x)
    r = xs[:, 0:RWKV_W]
    k = xs[:, RWKV_W:2 * RWKV_W]
    v = xs[:, 2 * RWKV_W:3 * RWKV_W]
    lo = 3 * RWKV_W
    wl = xs[:, lo + d * RWKV_LORA:lo + (d + 1) * RWKV_LORA]
    al = xs[:, lo + 2 * RWKV_LORA + d * RWKV_LORA:lo + 2 * RWKV_LORA + (d + 1) * RWKV_LORA]
    p = RWKV_PREC
    w_log = -_softplus(-(w0_ref[d:d + 1, :] + _mm(jnp.tanh(wl), wup_ref[d], p["lora"]))) - 0.5
    lw = -jnp.exp(w_log)
    a = jax.nn.sigmoid(a0_ref[d:d + 1, :] + _mm(al, aup_ref[d], p["lora"]))
    k2 = k * (1.0 + (a - 1.0) * ka_ref[...])
    kkv = k * kk_ref[...]
    if d == 0:
        gl = xs[:, lo + 4 * RWKV_LORA:lo + 6 * RWKV_LORA]
        gate[rows, :] = _mm(jax.nn.sigmoid(gl), gup_ref[...], p["gate"])
    incl, strict = _tri_masks(ch, rev)
    r2 = lax.broadcasted_iota(jnp.int32, (ch, 2 * ch), 0)
    c2 = lax.broadcasted_iota(jnp.int32, (ch, 2 * ch), 1) & (ch - 1)
    incl2, strict2 = (r2 <= c2, r2 < c2) if rev else (r2 >= c2, r2 > c2)
    g_inc = _mm(jnp.where(incl, 1.0, 0.0), lw, p["cumsum"])
    g_exc = g_inc - lw
    g_tot = jnp.sum(lw, axis=0, keepdims=True)
    r_dec = r * jnp.exp(g_inc)
    e_neg = jnp.exp(-g_inc)
    e_end = jnp.exp(g_tot - g_inc)
    e_exc = jnp.exp(g_exc)
    k_neg = k2 * e_neg
    k_end = k2 * e_end
    dec_tot = jnp.exp(g_tot)
    rkr = r * k2 * rk_ref[...]
    chains = []
    for h in range(RWKV_HEADS):
        hs = slice(h * hd, (h + 1) * hd)
        kk_h = kkv[:, hs]
        kap = kk_h * lax.rsqrt(jnp.sum(kk_h * kk_h, axis=-1, keepdims=True) + 1e-6)
        b_h = kap * a[:, hs]
        v_h = v[:, hs]
        chains.append(dict(
            d=d, h=h, rows=rows, cols=hs, incl2=incl2, strict2=strict2, v=v_h, dec=dec_tot[:, hs],
            left=jnp.concatenate([kap * e_exc[:, hs], r_dec[:, hs]], axis=0),
            right=jnp.concatenate([b_h * e_neg[:, hs], k_neg[:, hs]], axis=0),
            bk_end=jnp.concatenate([b_h * e_end[:, hs], k_end[:, hs]], axis=0),
            bonus=jnp.sum(rkr[:, hs], axis=-1, keepdims=True) * v_h))
    return chains


def _rwkv_step(x_ref, prm, st_ref, ysum, bon, gate, i, *, seq_len):
    ch = RWKV_CHUNK
    p = RWKV_PREC
    n_chunks = seq_len // ch
    cs = (_rwkv_chunk_operands(x_ref, prm, gate, i, 0, seq_len=seq_len)
          + _rwkv_chunk_operands(x_ref, prm, gate, n_chunks - 1 - i, 1, seq_len=seq_len))
    ms = [_mm_nt(c["left"], c["right"], p["amat"]) for c in cs]
    a_ks = [jnp.where(c["strict2"], m[:ch], 0.0) for c, m in zip(cs, ms)]
    a_rs = [jnp.where(c["incl2"], m[ch:], 0.0) for c, m in zip(cs, ms)]
    t_invs = _unit_tri_inv_many([a_k[:, :ch] for a_k in a_ks], p["inv"])
    ss = [st_ref[c["d"], c["h"]] for c in cs]
    lss = [_mm_nt(c["left"], s, p["state"]) for c, s in zip(cs, ss)]
    akvs = [_mm(a_k[:, ch:], c["v"], p["akv"]) for c, a_k in zip(cs, a_ks)]
    us = [-_mm(t, ls[:ch] + akv, p["solve"]) for t, ls, akv in zip(t_invs, lss, akvs)]
    uvs = [jnp.concatenate([u, c["v"]], axis=0) for c, u in zip(cs, us)]
    ys = [ls[ch:] + _mm(a_r, uv, p["out"]) for ls, a_r, uv in zip(lss, a_rs, uvs)]
    s_news = [s * c["dec"] + _mm_tn(uv, c["bk_end"], p["update"]) for c, s, uv in zip(cs, ss, uvs)]
    for c, y, s_new in zip(cs, ys, s_news):
        st_ref[c["d"], c["h"]] = s_new
        ysum[c["rows"], c["cols"]] += y
        bon[c["rows"], c["cols"]] += c["bonus"]


def _rwkv_kernel(*refs, seq_len, has_init):
    x_ref = refs[0]
    prm = refs[1:10]
    lnw_ref, lnb_ref = refs[10:12]
    pos = 12
    if has_init:
        s0f_ref, s0b_ref = refs[pos:pos + 2]
        pos += 2
    o_ref, sf_ref, sb_ref = refs[pos:pos + 3]
    ysum, bon, gate, st_ref = refs[pos + 3:]
    ch = RWKV_CHUNK
    n_chunks = seq_len // ch
    ysum[...] = jnp.zeros(ysum.shape, F32)
    bon[...] = jnp.zeros(bon.shape, F32)
    if has_init:
        st_ref[0] = s0f_ref[...]
        st_ref[1] = s0b_ref[...]
    else:
        st_ref[...] = jnp.zeros(st_ref.shape, F32)

    def step(i, carry):
        _rwkv_step(x_ref, prm, st_ref, ysum, bon, gate, i, seq_len=seq_len)
        return carry

    lax.fori_loop(0, n_chunks, step, 0)
    sf_ref[...] = st_ref[0]
    sb_ref[...] = st_ref[1]

    def finish(i, carry):
        rows = pl.ds(pl.multiple_of(i * ch, ch), ch)
        y = ysum[rows, :]
        parts = []
        for h in range(RWKV_HEADS):
            y_h = y[:, h * RWKV_HD:(h + 1) * RWKV_HD]
            mean = jnp.mean(y_h, axis=-1, keepdims=True)
            cen = y_h - mean
            var = jnp.mean(cen * cen, axis=-1, keepdims=True)
            parts.append(cen * lax.rsqrt(var + GN_EPS))
        yn = jnp.concatenate(parts, axis=1) * lnw_ref[...] + lnb_ref[...]
        o_ref[rows, :] = (yn + bon[rows, :]) * gate[rows, :]
        return carry

    lax.fori_loop(0, n_chunks, finish, 0)


def _rwkv_mixer(x_rw, n_seq, seq_len, params, s0_f, s0_b):
    has_init = s0_f is not None
    (mu, w0, w_up, a0, a_up, g_up, k_k, k_a, r_k, ln_w, ln_b) = params
    row = lambda a: a.reshape(1, RWKV_W)
    args = [x_rw, mu, w0, w_up, a0, a_up, g_up, row(k_k), row(k_a), row(r_k), row(ln_w), row(ln_b)]

    def whole(a):
        nd = a.ndim
        return pl.BlockSpec(a.shape, lambda s, nd=nd: (0,) * nd)

    in_specs = [pl.BlockSpec((seq_len, RWKV_IN), lambda s: (s, 0))] + [whole(a) for a in args[1:]]
    state = pl.BlockSpec((None, None, RWKV_HEADS, RWKV_HD, RWKV_HD), lambda s: (s, 0, 0, 0, 0))
    if has_init:
        in_specs += [state, state]
        args += [s0_f, s0_b]
    st_shape = jax.ShapeDtypeStruct((n_seq, 1, RWKV_HEADS, RWKV_HD, RWKV_HD), F32)
    return pl.pallas_call(
        functools.partial(_rwkv_kernel, seq_len=seq_len, has_init=has_init),
        grid=(n_seq,),
        in_specs=in_specs,
        out_specs=[pl.BlockSpec((seq_len, RWKV_W), lambda s: (s, 0)), state, state],
        out_shape=[jax.ShapeDtypeStruct((n_seq * seq_len, RWKV_W), F32), st_shape, st_shape],
        scratch_shapes=[pltpu.VMEM((seq_len, RWKV_W), F32) for _ in range(3)]
        + [pltpu.VMEM((2, RWKV_HEADS, RWKV_HD, RWKV_HD), F32)],
        compiler_params=pltpu.CompilerParams(dimension_semantics=("parallel",), vmem_limit_bytes=VMEM_LIMIT),
        name="rwkv_mixer",
    )(*args)


def kernel(x_prompt, x_sample, state_gdn_fwd, state_gdn_bwd, cache_attn_k, cache_attn_v, state_rwkv_fwd, state_rwkv_bwd, c, c_ctx, mod_w, mod_b, norm_mix, norm_mlp, mlp_w1, mlp_w2, norm_final, ev_w_in, ev_w_out, gdn_conv, gdn_a_log, gdn_dt_bias, gdn_norm, sc_conv, od_w_in, od_w_out, attn_sink, rwkv_mu, rwkv_w0, rwkv_w_up, rwkv_a0, rwkv_a_up, rwkv_g_up, rwkv_k_k, rwkv_k_a, rwkv_r_k, rwkv_ln_w, rwkv_ln_b):
    bp, lp, _ = x_prompt.shape
    bs, ls, _ = x_sample.shape
    depth = mod_w.shape[0]
    c_rows = jnp.concatenate([c_ctx[None, :], c, jnp.zeros((MOD_ROWS - 1 - bs, D_MODEL), F32)], axis=0)
    mods = _modulation(c_rows, mod_w, mod_b)

    groups = [
        dict(x=x_prompt.reshape(bp * lp, D_MODEL), n=bp, l=lp, latent=False,
             mod=lambda layer: _mod_spec(layer, lp // TOKEN_TILE, 0, 0)),
        dict(x=x_sample.reshape(bs * ls, D_MODEL), n=bs, l=ls, latent=True,
             mod=lambda layer: _mod_spec(layer, ls // TOKEN_TILE, 1, 1)),
    ]
    outs = {}
    for layer in range(depth):
        w1 = mlp_w1[layer].astype(BF16)
        w2 = mlp_w2[layer].astype(BF16)
        final = layer == depth - 1
        if layer % 2 == 0:
            e = layer // 2
            w = ev_w_in[e]
            qkvz = GDN_QKV_W + GDN_HEADS * GDN_D
            n_gate = 4 * GDN_HEADS
            w_in = jnp.concatenate(
                [w[:, :qkvz], w[:, qkvz + n_gate:], w[:, qkvz:qkvz + n_gate],
                 jnp.zeros((D_MODEL, EV_IN_PAD - w.shape[1]), F32)], axis=1).astype(BF16)
            w_out = ev_w_out[e].astype(BF16)
            alog_vec = jnp.zeros((1, 128), F32).at[0, 2 * GDN_HEADS:4 * GDN_HEADS].set(gdn_a_log[e].reshape(-1))
            dtb_vec = jnp.zeros((1, 128), F32).at[0, 2 * GDN_HEADS:4 * GDN_HEADS].set(gdn_dt_bias[e].reshape(-1))
            for grp in groups:
                (proj,) = _inproj(grp["x"], mods, norm_mix[layer], w_in, (EV_IN_PAD,), grp["mod"](layer))
                s0 = (state_gdn_fwd[:, e:e + 1], state_gdn_bwd[:, e:e + 1]) if grp["latent"] else (None, None)
                o, sc, s_f, s_b = _gdn_mixer(proj, grp["n"], grp["l"], gdn_conv[e], sc_conv[e], gdn_norm[e],
                                             alog_vec, dtb_vec, *s0)
                if not grp["latent"]:
                    outs.setdefault("gdn_f", []).append(s_f)
                    outs.setdefault("gdn_b", []).append(s_b)
                grp["x"] = _outproj_mlp(o, sc, grp["x"], mods, norm_mlp[layer], w_out, w1, w2, norm_final,
                                        grp["mod"](layer), final)
        else:
            o_ = layer // 2
            w = od_w_in[o_]
            w_in = jnp.concatenate([w[:, ATT_W:], w[:, :ATT_W]], axis=1).astype(BF16)
            w_out = od_w_out[o_].astype(BF16)
            rw = (rwkv_mu[o_], rwkv_w0[o_], rwkv_w_up[o_], rwkv_a0[o_], rwkv_a_up[o_], rwkv_g_up[o_],
                  rwkv_k_k[o_], rwkv_k_a[o_], rwkv_r_k[o_].reshape(-1), rwkv_ln_w[o_], rwkv_ln_b[o_])
            for grp in groups:
                x_rw, p_att = _inproj(grp["x"], mods, norm_mix[layer], w_in, (RWKV_IN, ATT_W), grp["mod"](layer))
                if grp["latent"]:
                    att = _attn_latent(p_att, attn_sink[o_], cache_attn_k, cache_attn_v, o_, grp["n"], grp["l"])
                    rwo, _, _ = _rwkv_mixer(x_rw, grp["n"], grp["l"], rw,
                                            state_rwkv_fwd[:, o_:o_ + 1], state_rwkv_bwd[:, o_:o_ + 1])
                else:
                    att, kc, vc = _attn_context(p_att, attn_sink[o_], grp["n"], grp["l"])
                    rwo, s_f, s_b = _rwkv_mixer(x_rw, grp["n"], grp["l"], rw, None, None)
                    outs.setdefault("att_k", []).append(kc)
                    outs.setdefault("att_v", []).append(vc)
                    outs.setdefault("rw_f", []).append(s_f)
                    outs.setdefault("rw_b", []).append(s_b)
                grp["x"] = _outproj_mlp(att, rwo, grp["x"], mods, norm_mlp[layer], w_out, w1, w2, norm_final,
                                        grp["mod"](layer), final)
    cat = lambda key: jnp.concatenate(outs[key], axis=1)
    return (groups[0]["x"].reshape(bp, lp, D_MODEL), groups[1]["x"].reshape(bs, ls, D_MODEL),
            cat("gdn_f"), cat("gdn_b"), cat("att_k"), cat("att_v"), cat("rw_f"), cat("rw_b"))
```

```python
import functools

import jax
import jax.numpy as jnp
import numpy as np
from jax import lax
from jax.experimental import pallas as pl
from jax.experimental.pallas import tpu as pltpu

F32 = jnp.float32
BF16 = jnp.bfloat16
HIGHEST = lax.Precision.HIGHEST

D_MODEL = 1024
N_MOD = 6
D_FF = 4 * D_MODEL
NORM_EPS = 1e-6
TOKEN_TILE = 256
MOD_ROWS = 8

GDN_HEADS = 4
GDN_D = 128
GDN_CHUNK = 128
GDN_GROUP = 8
GDN_BLOCK_ROWS = 1024
GDN_QKV_W = 3 * GDN_HEADS * GDN_D
SC_WIDTH = 512
EV_IN_PAD = 3712

ATT_HEADS = 8
ATT_KV_HEADS = 2
ATT_GROUP = ATT_HEADS // ATT_KV_HEADS
ATT_HD = 64
ATT_Q_W = ATT_HEADS * ATT_HD
ATT_KV_W = ATT_KV_HEADS * ATT_HD
ATT_W = ATT_Q_W + 2 * ATT_KV_W
WINDOW = 128
ATT_BLOCK = 128
GRID_W = 64
ROPE_BASE = 10000.0
NEG_INF = -1e30

RWKV_HEADS = 8
RWKV_HD = 64
RWKV_W = RWKV_HEADS * RWKV_HD
RWKV_LORA = 64
RWKV_IN = 3 * RWKV_W + 3 * 2 * RWKV_LORA
RWKV_CHUNK = 64
RWKV_GROUP = 4
RWKV_FINISH_ROWS = 256
DECAY_SCALE = float(np.exp(-0.5))
RWKV_BLOCK_ROWS = 1024
PAIR_W = 2 * RWKV_HD
N_PAIRS = RWKV_W // PAIR_W
GN_EPS = 64e-5

VMEM_LIMIT = 56 * 1024 * 1024

RWKV_PREC = dict(lora="bf", gate="bf", cumsum="x2r", amat="bf", inv="bf", state="bf", akv="bf", solve="x2r",
                 out="bf", update="bf")


def _split_bf16(a):
    hi = a.astype(BF16)
    return hi, (a - hi.astype(F32)).astype(BF16)


def _dot(a, b, dims, prec):
    dn = (dims, ((), ()))
    if prec == "hi":
        return lax.dot_general(a, b, dn, precision=HIGHEST, preferred_element_type=F32)
    one = lambda x, y: lax.dot_general(x, y, dn, preferred_element_type=F32)
    if prec == "x3":
        ah, al = _split_bf16(a)
        bh, bl = _split_bf16(b)
        return one(ah, bh) + one(ah, bl) + one(al, bh)
    if prec == "x2l":
        ah, al = _split_bf16(a)
        bh = b.astype(BF16)
        return one(ah, bh) + one(al, bh)
    if prec == "x2r":
        ah = a.astype(BF16)
        bh, bl = _split_bf16(b)
        return one(ah, bh) + one(ah, bl)
    assert prec == "bf", prec
    return lax.dot_general(a.astype(BF16), b.astype(BF16), dn, preferred_element_type=F32)


def _mm(a, b, prec="hi"):
    return _dot(a, b, ((1,), (0,)), prec)


def _mm_nt(a, b, prec="hi"):
    return _dot(a, b, ((1,), (1,)), prec)


def _mm_tn(a, b, prec="hi"):
    return _dot(a, b, ((0,), (0,)), prec)


def _silu(x):
    return x * jax.nn.sigmoid(x)


def _softplus(x):
    return jnp.maximum(x, 0.0) + jnp.log1p(jnp.exp(-jnp.abs(x)))


def _rms(x, w):
    return x * lax.rsqrt(jnp.mean(x * x, axis=-1, keepdims=True) + NORM_EPS) * w


def _tri_masks(n, rev):
    r = lax.broadcasted_iota(jnp.int32, (n, n), 0)
    c = lax.broadcasted_iota(jnp.int32, (n, n), 1)
    if rev:
        return r <= c, r < c
    return r >= c, r > c


def _unit_tri_inv_many(mats, prec):
    n = mats[0].shape[0]
    r = lax.broadcasted_iota(jnp.int32, (n, n), 0)
    c = lax.broadcasted_iota(jnp.int32, (n, n), 1)
    eye = jnp.where(r == c, 1.0, 0.0)
    ts = [eye - jnp.where((r >> 1) == (c >> 1), a, 0.0) for a in mats]
    shift = 1
    while (1 << shift) < n:
        join = ((r >> (shift + 1)) == (c >> (shift + 1))) & ((r >> shift) != (c >> shift))
        inner = [_mm(jnp.where(join, a, 0.0), t, prec) for a, t in zip(mats, ts)]
        ts = [t - _mm(t, w, prec) for t, w in zip(ts, inner)]
        shift += 1
    return ts


def _unit_tri_inv(a, prec):
    return _unit_tri_inv_many([a], prec)[0]


def _shifted_rows(x, prev_row, next_row):
    n = x.shape[0]
    row = lax.broadcasted_iota(jnp.int32, x.shape, 0)
    xp = jnp.where(row == 0, prev_row, pltpu.roll(x, 1, 0))
    xn = jnp.where(row == n - 1, next_row, pltpu.roll(x, n - 1, 0))
    return xp, xn


def _conv3(x, w_ref, seq_len):
    n = x.shape[0]
    assert seq_len & (seq_len - 1) == 0 and n % seq_len == 0
    pos = lax.broadcasted_iota(jnp.int32, x.shape, 0) & (seq_len - 1)
    xp = jnp.where(pos == 0, 0.0, pltpu.roll(x, 1, 0))
    xn = jnp.where(pos == seq_len - 1, 0.0, pltpu.roll(x, n - 1, 0))
    return xp * w_ref[0:1, :] + x * w_ref[1:2, :] + xn * w_ref[2:3, :]


def _mod_kernel(c_ref, w_ref, b_ref, o_ref):
    s = _silu(c_ref[...])
    o_ref[...] = _mm(s, w_ref[...], prec="bf") + b_ref[...]


def _modulation(c_rows, mod_w, mod_b):
    depth = mod_w.shape[0]
    nblk = (N_MOD * D_MODEL) // D_MODEL
    out = pl.pallas_call(
        _mod_kernel,
        grid=(depth, nblk),
        in_specs=[
            pl.BlockSpec((MOD_ROWS, D_MODEL), lambda l, j: (0, 0)),
            pl.BlockSpec((None, D_MODEL, D_MODEL), lambda l, j: (l, 0, j)),
            pl.BlockSpec((None, 1, D_MODEL), lambda l, j: (l, 0, j)),
        ],
        out_specs=pl.BlockSpec((None, MOD_ROWS, D_MODEL), lambda l, j: (l, 0, j)),
        out_shape=jax.ShapeDtypeStruct((depth, MOD_ROWS, N_MOD * D_MODEL), F32),
        compiler_params=pltpu.CompilerParams(dimension_semantics=("parallel", "parallel")),
        name="modulation",
    )(c_rows, mod_w, mod_b.reshape(depth, 1, N_MOD * D_MODEL))
    return out.reshape(depth, MOD_ROWS, N_MOD, D_MODEL)


def _mod_spec(layer, tiles_per_seq, row_base, row_step):
    return pl.BlockSpec((None, None, N_MOD, D_MODEL),
                        lambda i: (layer, row_base + (i // tiles_per_seq) * row_step, 0, 0))


def _inproj_kernel(x_ref, mod_ref, nw_ref, w_ref, *o_refs):
    h = _rms(x_ref[...], nw_ref[...])
    h = h * (1.0 + mod_ref[1:2, :]) + mod_ref[0:1, :]
    y = _mm(h, w_ref[...], prec="bf")
    off = 0
    for o_ref in o_refs:
        n = o_ref.shape[-1]
        o_ref[...] = y[:, off:off + n]
        off += n


def _inproj(x, mods, norm_w, w_bf16, splits, mod_spec):
    t = x.shape[0]
    n_in = w_bf16.shape[1]
    return pl.pallas_call(
        _inproj_kernel,
        grid=(t // TOKEN_TILE,),
        in_specs=[
            pl.BlockSpec((TOKEN_TILE, D_MODEL), lambda i: (i, 0)),
            mod_spec,
            pl.BlockSpec((1, D_MODEL), lambda i: (0, 0)),
            pl.BlockSpec((D_MODEL, n_in), lambda i: (0, 0)),
        ],
        out_specs=[pl.BlockSpec((TOKEN_TILE, n), lambda i: (i, 0)) for n in splits],
        out_shape=[jax.ShapeDtypeStruct((t, n), F32) for n in splits],
        compiler_params=pltpu.CompilerParams(dimension_semantics=("parallel",), vmem_limit_bytes=VMEM_LIMIT),
        name="inproj",
    )(x, mods, norm_w.reshape(1, D_MODEL), w_bf16)


def _mlp_kernel(a_ref, b_ref, x_ref, mod_ref, nw_ref, woa_ref, wob_ref, w1_ref, w2_ref, nf_ref, o_ref, *, final):
    y = _mm(a_ref[...], woa_ref[...], prec="bf") + _mm(b_ref[...], wob_ref[...], prec="bf")
    x1 = x_ref[...] + mod_ref[2:3, :] * y
    h = _rms(x1, nw_ref[...])
    h = (h * (1.0 + mod_ref[4:5, :]) + mod_ref[3:4, :]).astype(BF16)
    acc = jnp.zeros(x1.shape, F32)
    for j in range(D_FF // D_MODEL):
        cols = slice(j * D_MODEL, (j + 1) * D_MODEL)
        u = jnp.maximum(_mm(h, w1_ref[:, cols], prec="bf"), 0.0)
        acc = acc + _mm(u * u, w2_ref[cols, :], prec="bf")
    x2 = x1 + mod_ref[5:6, :] * acc
    if final:
        x2 = _rms(x2, nf_ref[...])
    o_ref[...] = x2


def _outproj_mlp(a, b, x, mods, norm_w, w_out, w1, w2, norm_final, mod_spec, final):
    t = x.shape[0]
    half = a.shape[1]
    const = lambda i: (0, 0)
    return pl.pallas_call(
        functools.partial(_mlp_kernel, final=final),
        grid=(t // TOKEN_TILE,),
        in_specs=[
            pl.BlockSpec((TOKEN_TILE, half), lambda i: (i, 0)),
            pl.BlockSpec((TOKEN_TILE, half), lambda i: (i, 0)),
            pl.BlockSpec((TOKEN_TILE, D_MODEL), lambda i: (i, 0)),
            mod_spec,
            pl.BlockSpec((1, D_MODEL), const),
            pl.BlockSpec((half, D_MODEL), const),
            pl.BlockSpec((half, D_MODEL), lambda i: (1, 0)),
            pl.BlockSpec((D_MODEL, D_FF), const),
            pl.BlockSpec((D_FF, D_MODEL), const),
            pl.BlockSpec((1, D_MODEL), const),
        ],
        out_specs=pl.BlockSpec((TOKEN_TILE, D_MODEL), lambda i: (i, 0)),
        out_shape=jax.ShapeDtypeStruct((t, D_MODEL), F32),
        compiler_params=pltpu.CompilerParams(dimension_semantics=("parallel",), vmem_limit_bytes=VMEM_LIMIT),
        name="outproj_mlp",
    )(a, b, x, mods, norm_w.reshape(1, D_MODEL), w_out, w_out, w1, w2, norm_final.reshape(1, D_MODEL))


def _gdn_decay_terms(g, rev):
    c = g.shape[0]
    incl = _tri_masks(c, rev)[0]
    before_col = _tri_masks(c, not rev)[0]
    eye = lax.broadcasted_iota(jnp.int32, (c, c), 0) == lax.broadcasted_iota(jnp.int32, (c, c), 1)
    gc_row = jnp.sum(jnp.where(before_col, jnp.broadcast_to(g, (c, c)), 0.0), axis=0, keepdims=True)
    gc_col = jnp.sum(jnp.where(eye, jnp.broadcast_to(gc_row, (c, c)), 0.0), axis=1, keepdims=True)
    decay = jnp.where(incl, jnp.exp(jnp.where(incl, gc_col - gc_row, 0.0)), 0.0)
    g_tot = jnp.sum(g, axis=0, keepdims=True)
    return decay, jnp.exp(gc_col), jnp.exp(g_tot - gc_col), jnp.exp(g_tot)


def _gdn_kernel(*refs, seq_len, seqs, has_init):
    (q_ref, k_ref, v_ref, z_ref, scb_ref, scc_ref, sch_ref, gate_ref,
     cq_ref, ck_ref, cv_ref, csc_ref, gn_ref, alog_ref, dtb_ref) = refs[:15]
    pos = 15
    if has_init:
        s0f_ref, s0b_ref = refs[pos:pos + 2]
        pos += 2
    o_ref, sco_ref, sf_ref, sb_ref = refs[pos:pos + 4]
    qs, ks, vs, osum, betas, gs, u_s, w_s, qd_s, kd_s, in_s, ge_s, st = refs[pos + 4:]
    head = pl.program_id(1)
    ch = GDN_CHUNK
    n_chunks = seq_len // ch

    def conv_silu(x_ref, c_ref):
        return _silu(_conv3(x_ref[...], c_ref, seq_len))

    def l2norm(x):
        return x * lax.rsqrt(jnp.sum(x * x, axis=-1, keepdims=True) + 1e-6)

    qs[...] = l2norm(conv_silu(q_ref, cq_ref)) * (GDN_D ** -0.5)
    ks[...] = l2norm(conv_silu(k_ref, ck_ref))
    vs[...] = conv_silu(v_ref, cv_ref)
    gates = gate_ref[...]
    betas[...] = jax.nn.sigmoid(gates)
    gs[...] = -jnp.exp(alog_ref[...]) * _softplus(gates + dtb_ref[...])
    sco_ref[...] = scb_ref[...] * _conv3(scc_ref[...] * sch_ref[...], csc_ref, seq_len)
    osum[...] = jnp.zeros(osum.shape, F32)

    lane = lax.broadcasted_iota(jnp.int32, (ch, 128), 1)

    def pick(ref, rows, col):
        return jnp.sum(jnp.where(lane == col, ref[rows, :], 0.0), axis=1, keepdims=True)

    def solve_group(gi, carry):
        items = []
        for j in range(GDN_GROUP):
            c = gi * GDN_GROUP + j
            rows = pl.ds(pl.multiple_of(c * ch, ch), ch)
            items.append(dict(c=c, rows=rows, q=qs[rows, :], k=ks[rows, :], v=vs[rows, :]))
        kks = [_mm_nt(it["k"], it["k"], "bf") for it in items]
        qks = [_mm_nt(it["q"], it["k"], "bf") for it in items]
        subs = []
        for it, kk, qk in zip(items, kks, qks):
            for d in range(2):
                beta = pick(betas, it["rows"], d * GDN_HEADS + head)
                g = pick(gs, it["rows"], 2 * GDN_HEADS + d * GDN_HEADS + head)
                decay, e_gc, e_rest, e_tot = _gdn_decay_terms(g, rev=(d == 1))
                strict = _tri_masks(ch, d == 1)[1]
                subs.append(dict(
                    d=d, c=it["c"], rows=it["rows"],
                    a=jnp.where(strict, kk * beta * decay, 0.0),
                    rhs=jnp.concatenate([it["v"] * beta, it["k"] * (beta * e_gc)], axis=1),
                    intra=qk * decay, qd=it["q"] * e_gc, kd=it["k"] * e_rest, ge=e_tot))
        ts = _unit_tri_inv_many([s["a"] for s in subs], "bf")
        uws = [_mm(t, s["rhs"], "bf") for t, s in zip(ts, subs)]
        for s, uw in zip(subs, uws):
            d, rows = s["d"], s["rows"]
            u_s[d, rows, :] = uw[:, :GDN_D]
            w_s[d, rows, :] = uw[:, GDN_D:].astype(BF16)
            qd_s[d, rows, :] = s["qd"].astype(BF16)
            kd_s[d, rows, :] = s["kd"].astype(BF16)
            in_s[d, rows, :] = s["intra"].astype(BF16)
            ge_s[d, pl.ds(pl.multiple_of(s["c"] * 8, 8), 8), :] = jnp.broadcast_to(s["ge"], (8, 128))
        return carry

    lax.fori_loop(0, seqs * n_chunks // GDN_GROUP, solve_group, 0)

    for j in range(seqs):
        st[0, j] = s0f_ref[j] if has_init else jnp.zeros((GDN_D, GDN_D), F32)
        st[1, j] = s0b_ref[j] if has_init else jnp.zeros((GDN_D, GDN_D), F32)

    def recur(i, carry):
        cs = []
        for j in range(seqs):
            for d in range(2):
                c = j * n_chunks + (i if d == 0 else n_chunks - 1 - i)
                cs.append(dict(d=d, j=j, rows=pl.ds(pl.multiple_of(c * ch, ch), ch),
                               ge=ge_s[d, pl.ds(pl.multiple_of(c * 8, 8), 1), :], s=st[d, j]))
        sbs = [c["s"].astype(BF16) for c in cs]
        wss = [_mm(w_s[c["d"], c["rows"], :], sb, "bf") for c, sb in zip(cs, sbs)]
        qss = [_mm(qd_s[c["d"], c["rows"], :], sb, "bf") for c, sb in zip(cs, sbs)]
        ebs = [(u_s[c["d"], c["rows"], :] - ws).astype(BF16) for c, ws in zip(cs, wss)]
        outs = [qs_ + _mm(in_s[c["d"], c["rows"], :], eb, "bf") for c, qs_, eb in zip(cs, qss, ebs)]
        s_news = [c["s"] * c["ge"] + _mm_tn(kd_s[c["d"], c["rows"], :], eb, "bf") for c, eb in zip(cs, ebs)]
        for c, o, s_new in zip(cs, outs, s_news):
            st[c["d"], c["j"]] = s_new
            osum[c["rows"], :] += o
        return carry

    lax.fori_loop(0, n_chunks, recur, 0)
    sf_ref[...] = st[0]
    sb_ref[...] = st[1]
    o_ref[...] = _rms(osum[...], gn_ref[...]) * _silu(z_ref[...])


def _gdn_mixer(proj, n_seq, seq_len, conv_w, sc_conv_w, gdn_norm, alog_vec, dtb_vec, s0_f, s0_b):
    has_init = s0_f is not None
    hd = GDN_HEADS
    seqs = max(1, GDN_BLOCK_ROWS // seq_len)
    rows = seqs * seq_len
    assert n_seq % seqs == 0 and (rows // GDN_CHUNK) % GDN_GROUP == 0

    def col(block):
        return pl.BlockSpec((rows, 128), lambda s, h, b=block: (s, b * hd + h))

    def wcol(block):
        return pl.BlockSpec((3, 128), lambda s, h, b=block: (0, b * hd + h))

    vec = pl.BlockSpec((1, 128), lambda s, h: (0, 0))
    state = pl.BlockSpec((seqs, None, None, GDN_D, GDN_D), lambda s, h: (s, 0, h, 0, 0))
    in_specs = [col(0), col(1), col(2), col(3), col(4), col(5), col(6),
                pl.BlockSpec((rows, 128), lambda s, h: (s, 7 * hd)),
                wcol(0), wcol(1), wcol(2), wcol(0), vec, vec, vec]
    args = [proj] * 8 + [conv_w, conv_w, conv_w, sc_conv_w, gdn_norm.reshape(1, 128), alog_vec, dtb_vec]
    if has_init:
        in_specs += [state, state]
        args += [s0_f, s0_b]
    t = n_seq * seq_len
    out_tok = pl.BlockSpec((rows, 128), lambda s, h: (s, h))
    scratch = ([pltpu.VMEM((rows, 128), F32) for _ in range(6)]
               + [pltpu.VMEM((2, rows, GDN_D), F32)]
               + [pltpu.VMEM((2, rows, GDN_D), BF16) for _ in range(3)]
               + [pltpu.VMEM((2, rows, GDN_CHUNK), BF16),
                  pltpu.VMEM((2, rows // GDN_CHUNK * 8, 128), F32),
                  pltpu.VMEM((2, seqs, GDN_D, GDN_D), F32)])
    return pl.pallas_call(
        functools.partial(_gdn_kernel, seq_len=seq_len, seqs=seqs, has_init=has_init),
        grid=(n_seq // seqs, hd),
        in_specs=in_specs,
        out_specs=[out_tok, out_tok, state, state],
        out_shape=[jax.ShapeDtypeStruct((t, hd * GDN_D), F32), jax.ShapeDtypeStruct((t, SC_WIDTH), F32),
                   jax.ShapeDtypeStruct((n_seq, 1, hd, GDN_D, GDN_D), F32),
                   jax.ShapeDtypeStruct((n_seq, 1, hd, GDN_D, GDN_D), F32)],
        scratch_shapes=scratch,
        compiler_params=pltpu.CompilerParams(dimension_semantics=("parallel", "parallel"),
                                             vmem_limit_bytes=VMEM_LIMIT),
        name="gdn_mixer",
    )(*args)


def _softmax_pv(scores, values, sink):
    m = sink
    for s in scores:
        m = jnp.maximum(m, jnp.max(s, axis=-1, keepdims=True))
    den = jnp.exp(sink - m)
    acc = None
    for s, v in zip(scores, values):
        e = jnp.exp(s - m)
        den = den + jnp.sum(e, axis=-1, keepdims=True)
        pv = _mm(e, v, prec="bf")
        acc = pv if acc is None else acc + pv
    return acc / den


def _attn_ctx_kernel(sink_ref, p_ref, o_ref, kc_ref, vc_ref):
    scale = ATT_HD ** -0.5
    for j in range(ATT_KV_HEADS):
        k = p_ref[:, ATT_Q_W + j * ATT_HD:ATT_Q_W + (j + 1) * ATT_HD]
        v = p_ref[:, ATT_Q_W + ATT_KV_W + j * ATT_HD:ATT_Q_W + ATT_KV_W + (j + 1) * ATT_HD]
        kc_ref[j] = k
        vc_ref[j] = v
        for gi in range(ATT_GROUP):
            hh = j * ATT_GROUP + gi
            q = p_ref[:, hh * ATT_HD:(hh + 1) * ATT_HD]
            s = _mm_nt(q, k, prec="bf") * scale
            sink = jnp.full((1, 1), sink_ref[hh], F32)
            o_ref[:, hh * ATT_HD:(hh + 1) * ATT_HD] = _softmax_pv([s], [v], sink)


def _attn_context(proj_att, sink, n_seq, seq_len):
    kv = pl.BlockSpec((None, None, ATT_KV_HEADS, seq_len, ATT_HD), lambda b: (b, 0, 0, 0, 0))
    return pl.pallas_call(
        _attn_ctx_kernel,
        grid=(n_seq,),
        in_specs=[pl.BlockSpec(memory_space=pltpu.SMEM),
                  pl.BlockSpec((seq_len, ATT_W), lambda b: (b, 0))],
        out_specs=[pl.BlockSpec((seq_len, ATT_Q_W), lambda b: (b, 0)), kv, kv],
        out_shape=[jax.ShapeDtypeStruct((n_seq * seq_len, ATT_Q_W), F32),
                   jax.ShapeDtypeStruct((n_seq, 1, ATT_KV_HEADS, seq_len, ATT_HD), F32),
                   jax.ShapeDtypeStruct((n_seq, 1, ATT_KV_HEADS, seq_len, ATT_HD), F32)],
        compiler_params=pltpu.CompilerParams(dimension_semantics=("parallel",)),
        name="attn_context",
    )(sink, proj_att)


def _rope_tables(seq_len):
    pos = np.arange(seq_len)
    half = ATT_HD // 2
    inv = ROPE_BASE ** (-np.arange(0, half, 2, dtype=np.float32) / half)
    ang_r = (pos // GRID_W).astype(np.float32)[:, None] * inv
    ang_c = (pos % GRID_W).astype(np.float32)[:, None] * inv
    cos = np.concatenate([np.cos(ang_r), np.cos(ang_r), np.cos(ang_c), np.cos(ang_c)], axis=1)
    sin = np.concatenate([-np.sin(ang_r), np.sin(ang_r), -np.sin(ang_c), np.sin(ang_c)], axis=1)
    return (jnp.asarray(np.tile(cos, (1, 2)), F32), jnp.asarray(np.tile(sin, (1, 2)), F32))


def _rope(x, cos, sin):
    lane = lax.broadcasted_iota(jnp.int32, x.shape, 1)
    partner = jnp.where((lane & 31) < 16, pltpu.roll(x, 128 - 16, 1), pltpu.roll(x, 16, 1))
    return x * cos + partner * sin


def _attn_lat_kernel(sink_ref, p_ref, ck_ref, cv_ref, cos_ref, sin_ref, o_ref, *, seq_len):
    scale = ATT_HD ** -0.5
    qb = pl.program_id(1)
    span = 3 * ATT_BLOCK
    q0 = pl.multiple_of(qb * ATT_BLOCK, ATT_BLOCK)
    k0 = pl.multiple_of(jnp.clip((qb - 1) * ATT_BLOCK, 0, seq_len - span), ATT_BLOCK)
    qrows = pl.ds(q0, ATT_BLOCK)
    krows = pl.ds(k0, span)
    kwin = _rope(p_ref[krows, ATT_Q_W:ATT_Q_W + ATT_KV_W], cos_ref[krows, :], sin_ref[krows, :])
    vwin = p_ref[krows, ATT_Q_W + ATT_KV_W:ATT_W]
    qpos = q0 + lax.broadcasted_iota(jnp.int32, (ATT_BLOCK, span), 0)
    kpos = k0 + lax.broadcasted_iota(jnp.int32, (ATT_BLOCK, span), 1)
    valid = jnp.abs(qpos - kpos) <= WINDOW
    cos_q = cos_ref[qrows, :]
    sin_q = sin_ref[qrows, :]
    for pair in range(ATT_HEADS // 2):
        qpair = _rope(p_ref[qrows, pair * 128:(pair + 1) * 128], cos_q, sin_q)
        for sub in range(2):
            hh = pair * 2 + sub
            j = hh // ATT_GROUP
            q = qpair[:, sub * ATT_HD:(sub + 1) * ATT_HD]
            s_loc = _mm_nt(q, kwin[:, j * ATT_HD:(j + 1) * ATT_HD], prec="bf") * scale
            s_loc = jnp.where(valid, s_loc, NEG_INF)
            s_ctx = _mm_nt(q, ck_ref[j], prec="bf") * scale
            sink = jnp.full((1, 1), sink_ref[hh], F32)
            o_ref[:, hh * ATT_HD:(hh + 1) * ATT_HD] = _softmax_pv(
                [s_loc, s_ctx], [vwin[:, j * ATT_HD:(j + 1) * ATT_HD], cv_ref[j]], sink)


def _attn_latent(proj_att, sink, cache_k, cache_v, layer, n_seq, seq_len):
    cos, sin = _rope_tables(seq_len)
    past = cache_k.shape[3]
    nqb = seq_len // ATT_BLOCK
    cache = pl.BlockSpec((None, None, ATT_KV_HEADS, past, ATT_HD), lambda b, q: (b, layer, 0, 0, 0))
    table = pl.BlockSpec((seq_len, 128), lambda b, q: (0, 0))
    return pl.pallas_call(
        functools.partial(_attn_lat_kernel, seq_len=seq_len),
        grid=(n_seq, nqb),
        in_specs=[pl.BlockSpec(memory_space=pltpu.SMEM),
                  pl.BlockSpec((seq_len, ATT_W), lambda b, q: (b, 0)),
                  cache, cache, table, table],
        out_specs=pl.BlockSpec((ATT_BLOCK, ATT_Q_W), lambda b, q: (b * nqb + q, 0)),
        out_shape=jax.ShapeDtypeStruct((n_seq * seq_len, ATT_Q_W), F32),
        compiler_params=pltpu.CompilerParams(dimension_semantics=("parallel", "parallel")),
        name="attn_latent",
    )(sink, proj_att, cache_k, cache_v, cos, sin)


def _pair_masks(n, rev):
    r = lax.broadcasted_iota(jnp.int32, (n, 2 * n), 0)
    c = lax.broadcasted_iota(jnp.int32, (n, 2 * n), 1) & (n - 1)
    return (r <= c, r < c) if rev else (r >= c, r > c)


def _bd(x):
    half = x.shape[1] // 2
    lane = lax.broadcasted_iota(jnp.int32, x.shape, 1)
    zero = jnp.zeros_like(x)
    return jnp.concatenate([jnp.where(lane < half, x, zero), jnp.where(lane >= half, x, zero)], axis=0)


def _bd_mask(n):
    r = lax.broadcasted_iota(jnp.int32, (2 * n, 2 * n), 0)
    c = lax.broadcasted_iota(jnp.int32, (2 * n, 2 * n), 1)
    return (r < n) == (c < n)


def _head_sums(x):
    ones = jnp.where(_bd_mask(RWKV_HD), 1.0, 0.0).astype(BF16)
    return _mm(x, ones, "x2l")


def _apply_pairs(tb, x):
    hi, lo = _split_bf16(x)
    return _mm(tb, _bd(hi), "bf") + _mm(tb, _bd(lo), "bf")


def _unit_tri_inv_pairs(mats):
    n = mats[0].shape[0]
    r = lax.broadcasted_iota(jnp.int32, (n, 2 * n), 0)
    c = lax.broadcasted_iota(jnp.int32, (n, 2 * n), 1) & (n - 1)
    eye = jnp.where(r == c, 1.0, 0.0)
    ts = [eye - jnp.where((r >> 1) == (c >> 1), a, 0.0) for a in mats]
    shift = 1
    while (1 << shift) < n:
        join = ((r >> (shift + 1)) == (c >> (shift + 1))) & ((r >> shift) != (c >> shift))
        tbs = [t.astype(BF16) for t in ts]
        inner = [_mm(jnp.where(join, a, 0.0), _bd(tb), "bf") for a, tb in zip(mats, tbs)]
        ts = [t - _mm(tb, _bd(w.astype(BF16)), "bf") for t, tb, w in zip(ts, tbs, inner)]
        shift += 1
    return ts


def _rwkv_chunk_operands(x_ref, prm, sc, c, *, seq_len):
    (mu_ref, w0_ref, wup_ref, a0_ref, aup_ref, gup_ref, kk_ref, ka_ref, rk_ref) = prm
    ch = RWKV_CHUNK
    total = x_ref.shape[0]
    r0 = pl.multiple_of(c * ch, ch)
    rows = pl.ds(r0, ch)
    pos0 = r0 & (seq_len - 1)
    x = x_ref[rows, :]
    prev_row = x_ref[pl.ds(jnp.maximum(r0 - 1, 0), 1), :] * jnp.where(pos0 > 0, 1.0, 0.0)
    next_row = x_ref[pl.ds(jnp.minimum(r0 + ch, total - 1), 1), :] * jnp.where(pos0 + ch < seq_len, 1.0, 0.0)
    xp, xn = _shifted_rows(x, prev_row, next_row)
    xs = x + mu_ref[0:1, :] * (xp - x) + mu_ref[1:2, :] * (xn - x)
    r = xs[:, 0:RWKV_W]
    k = xs[:, RWKV_W:2 * RWKV_W]
    v = xs[:, 2 * RWKV_W:3 * RWKV_W]
    lo = 3 * RWKV_W
    p = RWKV_PREC
    gl = xs[:, lo + 4 * RWKV_LORA:lo + 6 * RWKV_LORA]
    sc["gate"][rows, :] = _mm(jax.nn.sigmoid(gl), gup_ref[...], p["gate"])
    sc["v"][rows, :] = v.astype(BF16)
    pairs = [slice(i * PAIR_W, (i + 1) * PAIR_W) for i in range(N_PAIRS)]
    kkv = k * kk_ref[...]
    kaps = []
    for cols in pairs:
        kk_p = kkv[:, cols]
        kaps.append(kk_p * lax.rsqrt(_head_sums(kk_p * kk_p) + 1e-6))
    items = []
    bonus = None
    for d in range(2):
        rev = d == 1
        wl = xs[:, lo + d * RWKV_LORA:lo + (d + 1) * RWKV_LORA]
        al = xs[:, lo + 2 * RWKV_LORA + d * RWKV_LORA:lo + 2 * RWKV_LORA + (d + 1) * RWKV_LORA]
        lw = -DECAY_SCALE * jax.nn.sigmoid(w0_ref[d:d + 1, :] + _mm(jnp.tanh(wl), wup_ref[d], p["lora"]))
        a = jax.nn.sigmoid(a0_ref[d:d + 1, :] + _mm(al, aup_ref[d], p["lora"]))
        k2 = k * (1.0 + (a - 1.0) * ka_ref[...])
        g_inc = _mm(jnp.where(_tri_masks(ch, rev)[0], 1.0, 0.0), lw, p["cumsum"])
        g_tot = jnp.sum(lw, axis=0, keepdims=True)
        e_neg = jnp.exp(-g_inc)
        e_end = jnp.exp(g_tot - g_inc)
        e_exc = jnp.exp(g_inc - lw)
        r_dec = r * jnp.exp(g_inc)
        k_neg = k2 * e_neg
        k_end = k2 * e_end
        sc["dec"][d, pl.ds(pl.multiple_of(c * 8, 8), 8), :] = jnp.broadcast_to(jnp.exp(g_tot), (8, RWKV_W))
        rkr = r * k2 * rk_ref[...]
        bon_d = jnp.concatenate([_head_sums(rkr[:, cols]) for cols in pairs], axis=1) * v
        bonus = bon_d if bonus is None else bonus + bon_d
        for cols, kap in zip(pairs, kaps):
            b_p = kap * a[:, cols]
            items.append(dict(d=d, rows=rows, cols=cols, kap_dec=kap * e_exc[:, cols], r_dec=r_dec[:, cols],
                              b_neg=b_p * e_neg[:, cols], k_neg=k_neg[:, cols],
                              b_end=b_p * e_end[:, cols], k_end=k_end[:, cols], vb=v[:, cols].astype(BF16)))
    sc["bon"][rows, :] = bonus
    return items


def _rwkv_solve(items, sc):
    ch = RWKV_CHUNK
    masks = [_pair_masks(ch, False), _pair_masks(ch, True)]
    ms = [_mm_nt(jnp.concatenate([it["kap_dec"], it["r_dec"]], axis=0),
                 jnp.concatenate([_bd(it["b_neg"].astype(BF16)), _bd(it["k_neg"].astype(BF16))], axis=0), "bf")
          for it in items]
    a_abs = [jnp.where(masks[it["d"]][1], m[:ch, :2 * ch], 0.0) for it, m in zip(items, ms)]
    a_aks = [jnp.where(masks[it["d"]][1], m[:ch, 2 * ch:], 0.0) for it, m in zip(items, ms)]
    a_rbs = [jnp.where(masks[it["d"]][0], m[ch:, :2 * ch], 0.0) for it, m in zip(items, ms)]
    a_rks = [jnp.where(masks[it["d"]][0], m[ch:, 2 * ch:], 0.0) for it, m in zip(items, ms)]
    tbs = [t.astype(BF16) for t in _unit_tri_inv_pairs(a_abs)]
    akvs = [_mm(a_ak, _bd(it["vb"]), "bf") for it, a_ak in zip(items, a_aks)]
    w2s = [_apply_pairs(tb, it["kap_dec"]) for it, tb in zip(items, tbs)]
    u0s = [_apply_pairs(tb, akv) for tb, akv in zip(tbs, akvs)]
    for it, w2, u0, a_rb, a_rk in zip(items, w2s, u0s, a_rbs, a_rks):
        d, rows, cols = it["d"], it["rows"], it["cols"]
        sc["w2"][d, rows, cols] = w2.astype(BF16)
        sc["rd"][d, rows, cols] = it["r_dec"].astype(BF16)
        sc["u0"][d, rows, cols] = u0
        sc["arb"][d, rows, cols] = a_rb.astype(BF16)
        sc["ark"][d, rows, cols] = a_rk.astype(BF16)
        sc["bh"][d, rows, cols] = it["b_end"].astype(BF16)
        sc["kh"][d, rows, cols] = it["k_end"].astype(BF16)


def _rwkv_recur(sc, st, i, *, seq_len, seqs):
    ch = RWKV_CHUNK
    n_chunks = seq_len // ch
    keep = _bd_mask(RWKV_HD)
    cs = []
    for j in range(seqs):
        for d in range(2):
            c = j * n_chunks + (i if d == 0 else n_chunks - 1 - i)
            rows = pl.ds(pl.multiple_of(c * ch, ch), ch)
            dec = sc["dec"][d, pl.ds(pl.multiple_of(c * 8, 8), 1), :]
            for pi in range(N_PAIRS):
                cols = slice(pi * PAIR_W, (pi + 1) * PAIR_W)
                cs.append(dict(d=d, j=j, p=pi, rows=rows, cols=cols, dec=dec[:, cols], s=st[d, j, pi]))
    sbs = [c["s"].astype(BF16) for c in cs]
    lss = [_mm_nt(jnp.concatenate([sc["w2"][c["d"], c["rows"], c["cols"]],
                                   sc["rd"][c["d"], c["rows"], c["cols"]]], axis=0), sb, "bf")
           for c, sb in zip(cs, sbs)]
    ubs = [(-(ls[:ch] + sc["u0"][c["d"], c["rows"], c["cols"]])).astype(BF16) for c, ls in zip(cs, lss)]
    vbs = [sc["v"][c["rows"], c["cols"]] for c in cs]
    ys = [ls[ch:] + _mm(jnp.concatenate([sc["arb"][c["d"], c["rows"], c["cols"]],
                                         sc["ark"][c["d"], c["rows"], c["cols"]]], axis=1),
                        jnp.concatenate([_bd(ub), _bd(vb)], axis=0), "bf")
          for c, ls, ub, vb in zip(cs, lss, ubs, vbs)]
    ups = [_mm_tn(jnp.concatenate([ub, vb], axis=0),
                  jnp.concatenate([sc["bh"][c["d"], c["rows"], c["cols"]],
                                   sc["kh"][c["d"], c["rows"], c["cols"]]], axis=0), "bf")
           for c, ub, vb in zip(cs, ubs, vbs)]
    for c, y, up in zip(cs, ys, ups):
        st[c["d"], c["j"], c["p"]] = c["s"] * c["dec"] + jnp.where(keep, up, 0.0)
        sc["ysum"][c["rows"], c["cols"]] += y


RWKV_SCRATCH = ("ysum", "bon", "gate", "v", "w2", "rd", "u0", "arb", "ark", "bh", "kh", "dec")


def _rwkv_kernel(*refs, seq_len, seqs, has_init):
    x_ref = refs[0]
    prm = refs[1:10]
    lnw_ref, lnb_ref = refs[10:12]
    pos = 12
    if has_init:
        s0f_ref, s0b_ref = refs[pos:pos + 2]
        pos += 2
    o_ref, sf_ref, sb_ref = refs[pos:pos + 3]
    sc = dict(zip(RWKV_SCRATCH, refs[pos + 3:]))
    st = refs[pos + 3 + len(RWKV_SCRATCH)]
    ch = RWKV_CHUNK
    hd = RWKV_HD
    n_chunks = seq_len // ch
    sc["ysum"][...] = jnp.zeros(sc["ysum"].shape, F32)

    def prepare(gi, carry):
        items = []
        for j in range(RWKV_GROUP):
            items += _rwkv_chunk_operands(x_ref, prm, sc, gi * RWKV_GROUP + j, seq_len=seq_len)
        _rwkv_solve(items, sc)
        return carry

    lax.fori_loop(0, seqs * n_chunks // RWKV_GROUP, prepare, 0)

    zero = jnp.zeros((hd, hd), F32)
    for d, s0_ref in enumerate((s0f_ref, s0b_ref) if has_init else (None, None)):
        for j in range(seqs):
            for pi in range(N_PAIRS):
                s_a = s0_ref[j, 2 * pi] if has_init else zero
                s_b = s0_ref[j, 2 * pi + 1] if has_init else zero
                st[d, j, pi] = jnp.concatenate([jnp.concatenate([s_a, zero], axis=1),
                                                jnp.concatenate([zero, s_b], axis=1)], axis=0)

    def recur(i, carry):
        _rwkv_recur(sc, st, i, seq_len=seq_len, seqs=seqs)
        return carry

    lax.fori_loop(0, n_chunks, recur, 0)
    for d, out_ref in enumerate((sf_ref, sb_ref)):
        for j in range(seqs):
            for pi in range(N_PAIRS):
                s = st[d, j, pi]
                out_ref[j, 2 * pi] = s[:hd, :hd]
                out_ref[j, 2 * pi + 1] = s[hd:, hd:]

    tile = RWKV_FINISH_ROWS

    def finish(i, carry):
        rows = pl.ds(pl.multiple_of(i * tile, tile), tile)
        pairs = [slice(pi * PAIR_W, (pi + 1) * PAIR_W) for pi in range(N_PAIRS)]
        ys = [sc["ysum"][rows, cols] for cols in pairs]
        cens = [y - _head_sums(y) * (1.0 / hd) for y in ys]
        vars_ = [_head_sums(cen * cen) * (1.0 / hd) for cen in cens]
        for cols, cen, var in zip(pairs, cens, vars_):
            yn = cen * lax.rsqrt(var + GN_EPS) * lnw_ref[:, cols] + lnb_ref[:, cols]
            o_ref[rows, cols] = (yn + sc["bon"][rows, cols]) * sc["gate"][rows, cols]
        return carry

    lax.fori_loop(0, seqs * seq_len // tile, finish, 0)


def _rwkv_mixer(x_rw, n_seq, seq_len, params, s0_f, s0_b):
    has_init = s0_f is not None
    (mu, w0, w_up, a0, a_up, g_up, k_k, k_a, r_k, ln_w, ln_b) = params
    row = lambda a: a.reshape(1, RWKV_W)
    args = [x_rw, mu, w0, w_up, a0, a_up, g_up, row(k_k), row(k_a), row(r_k), row(ln_w), row(ln_b)]

    def whole(a):
        nd = a.ndim
        return pl.BlockSpec(a.shape, lambda s, nd=nd: (0,) * nd)

    seqs = max(1, RWKV_BLOCK_ROWS // seq_len)
    rows = seqs * seq_len
    assert n_seq % seqs == 0 and (rows // RWKV_CHUNK) % RWKV_GROUP == 0 and seq_len & (seq_len - 1) == 0
    in_specs = [pl.BlockSpec((rows, RWKV_IN), lambda s: (s, 0))] + [whole(a) for a in args[1:]]
    state = pl.BlockSpec((seqs, None, RWKV_HEADS, RWKV_HD, RWKV_HD), lambda s: (s, 0, 0, 0, 0))
    if has_init:
        in_specs += [state, state]
        args += [s0_f, s0_b]
    st_shape = jax.ShapeDtypeStruct((n_seq, 1, RWKV_HEADS, RWKV_HD, RWKV_HD), F32)
    tok = lambda dt: pltpu.VMEM((rows, RWKV_W), dt)
    per_dir = lambda dt: pltpu.VMEM((2, rows, RWKV_W), dt)
    scratch = dict(ysum=tok(F32), bon=tok(F32), gate=tok(F32), v=tok(BF16), w2=per_dir(BF16), rd=per_dir(BF16),
                   u0=per_dir(F32), arb=per_dir(BF16), ark=per_dir(BF16), bh=per_dir(BF16), kh=per_dir(BF16),
                   dec=pltpu.VMEM((2, rows // RWKV_CHUNK * 8, RWKV_W), F32))
    return pl.pallas_call(
        functools.partial(_rwkv_kernel, seq_len=seq_len, seqs=seqs, has_init=has_init),
        grid=(n_seq // seqs,),
        in_specs=in_specs,
        out_specs=[pl.BlockSpec((rows, RWKV_W), lambda s: (s, 0)), state, state],
        out_shape=[jax.ShapeDtypeStruct((n_seq * seq_len, RWKV_W), F32), st_shape, st_shape],
        scratch_shapes=[scratch[name] for name in RWKV_SCRATCH]
        + [pltpu.VMEM((2, seqs, N_PAIRS, PAIR_W, PAIR_W), F32)],
        compiler_params=pltpu.CompilerParams(dimension_semantics=("parallel",), vmem_limit_bytes=VMEM_LIMIT),
        name="rwkv_mixer",
    )(*args)


def kernel(x_prompt, x_sample, state_gdn_fwd, state_gdn_bwd, cache_attn_k, cache_attn_v, state_rwkv_fwd, state_rwkv_bwd, c, c_ctx, mod_w, mod_b, norm_mix, norm_mlp, mlp_w1, mlp_w2, norm_final, ev_w_in, ev_w_out, gdn_conv, gdn_a_log, gdn_dt_bias, gdn_norm, sc_conv, od_w_in, od_w_out, attn_sink, rwkv_mu, rwkv_w0, rwkv_w_up, rwkv_a0, rwkv_a_up, rwkv_g_up, rwkv_k_k, rwkv_k_a, rwkv_r_k, rwkv_ln_w, rwkv_ln_b):
    bp, lp, _ = x_prompt.shape
    bs, ls, _ = x_sample.shape
    depth = mod_w.shape[0]
    c_rows = jnp.concatenate([c_ctx[None, :], c, jnp.zeros((MOD_ROWS - 1 - bs, D_MODEL), F32)], axis=0)
    mods = _modulation(c_rows, mod_w, mod_b)

    groups = [
        dict(x=x_prompt.reshape(bp * lp, D_MODEL), n=bp, l=lp, latent=False,
             mod=lambda layer: _mod_spec(layer, lp // TOKEN_TILE, 0, 0)),
        dict(x=x_sample.reshape(bs * ls, D_MODEL), n=bs, l=ls, latent=True,
             mod=lambda layer: _mod_spec(layer, ls // TOKEN_TILE, 1, 1)),
    ]
    outs = {}
    for layer in range(depth):
        w1 = mlp_w1[layer].astype(BF16)
        w2 = mlp_w2[layer].astype(BF16)
        final = layer == depth - 1
        if layer % 2 == 0:
            e = layer // 2
            w = ev_w_in[e]
            qkvz = GDN_QKV_W + GDN_HEADS * GDN_D
            n_gate = 4 * GDN_HEADS
            w_in = jnp.concatenate(
                [w[:, :qkvz], w[:, qkvz + n_gate:], w[:, qkvz:qkvz + n_gate],
                 jnp.zeros((D_MODEL, EV_IN_PAD - w.shape[1]), F32)], axis=1).astype(BF16)
            w_out = ev_w_out[e].astype(BF16)
            alog_vec = jnp.zeros((1, 128), F32).at[0, 2 * GDN_HEADS:4 * GDN_HEADS].set(gdn_a_log[e].reshape(-1))
            dtb_vec = jnp.zeros((1, 128), F32).at[0, 2 * GDN_HEADS:4 * GDN_HEADS].set(gdn_dt_bias[e].reshape(-1))
            for grp in groups:
                (proj,) = _inproj(grp["x"], mods, norm_mix[layer], w_in, (EV_IN_PAD,), grp["mod"](layer))
                s0 = (state_gdn_fwd[:, e:e + 1], state_gdn_bwd[:, e:e + 1]) if grp["latent"] else (None, None)
                o, sc, s_f, s_b = _gdn_mixer(proj, grp["n"], grp["l"], gdn_conv[e], sc_conv[e], gdn_norm[e],
                                             alog_vec, dtb_vec, *s0)
                if not grp["latent"]:
                    outs.setdefault("gdn_f", []).append(s_f)
                    outs.setdefault("gdn_b", []).append(s_b)
                grp["x"] = _outproj_mlp(o, sc, grp["x"], mods, norm_mlp[layer], w_out, w1, w2, norm_final,
                                        grp["mod"](layer), final)
        else:
            o_ = layer // 2
            w = od_w_in[o_]
            w_in = jnp.concatenate([w[:, ATT_W:], w[:, :ATT_W]], axis=1).astype(BF16)
            w_out = od_w_out[o_].astype(BF16)
            rw = (rwkv_mu[o_], rwkv_w0[o_], rwkv_w_up[o_], rwkv_a0[o_], rwkv_a_up[o_], rwkv_g_up[o_],
                  rwkv_k_k[o_], rwkv_k_a[o_], rwkv_r_k[o_].reshape(-1), rwkv_ln_w[o_], rwkv_ln_b[o_])
            for grp in groups:
                x_rw, p_att = _inproj(grp["x"], mods, norm_mix[layer], w_in, (RWKV_IN, ATT_W), grp["mod"](layer))
                if grp["latent"]:
                    att = _attn_latent(p_att, attn_sink[o_], cache_attn_k, cache_attn_v, o_, grp["n"], grp["l"])
                    rwo, _, _ = _rwkv_mixer(x_rw, grp["n"], grp["l"], rw,
                                            state_rwkv_fwd[:, o_:o_ + 1], state_rwkv_bwd[:, o_:o_ + 1])
                else:
                    att, kc, vc = _attn_context(p_att, attn_sink[o_], grp["n"], grp["l"])
                    rwo, s_f, s_b = _rwkv_mixer(x_rw, grp["n"], grp["l"], rw, None, None)
                    outs.setdefault("att_k", []).append(kc)
                    outs.setdefault("att_v", []).append(vc)
                    outs.setdefault("rw_f", []).append(s_f)
                    outs.setdefault("rw_b", []).append(s_b)
                grp["x"] = _outproj_mlp(att, rwo, grp["x"], mods, norm_mlp[layer], w_out, w1, w2, norm_final,
                                        grp["mod"](layer), final)
    cat = lambda key: jnp.concatenate(outs[key], axis=1)
    return (groups[0]["x"].reshape(bp, lp, D_MODEL), groups[1]["x"].reshape(bs, ls, D_MODEL),
            cat("gdn_f"), cat("gdn_b"), cat("att_k"), cat("att_v"), cat("rw_f"), cat("rw_b"))
```

```python
import functools

import jax
import jax.numpy as jnp
import numpy as np
from jax import lax
from jax.experimental import pallas as pl
from jax.experimental.pallas import tpu as pltpu

F32 = jnp.float32
BF16 = jnp.bfloat16
HIGHEST = lax.Precision.HIGHEST

D_MODEL = 1024
N_MOD = 6
D_FF = 4 * D_MODEL
NORM_EPS = 1e-6
TOKEN_TILE = 256
MOD_ROWS = 8

GDN_HEADS = 4
GDN_D = 128
GDN_CHUNK = 128
GDN_GROUP = 8
GDN_BLOCK_ROWS = 1024
GDN_QKV_W = 3 * GDN_HEADS * GDN_D
SC_WIDTH = 512
EV_IN_PAD = 3712

ATT_HEADS = 8
ATT_KV_HEADS = 2
ATT_GROUP = ATT_HEADS // ATT_KV_HEADS
ATT_HD = 64
ATT_Q_W = ATT_HEADS * ATT_HD
ATT_KV_W = ATT_KV_HEADS * ATT_HD
ATT_W = ATT_Q_W + 2 * ATT_KV_W
WINDOW = 128
ATT_BLOCK = 128
GRID_W = 64
ROPE_BASE = 10000.0
NEG_INF = -1e30

RWKV_HEADS = 8
RWKV_HD = 64
RWKV_W = RWKV_HEADS * RWKV_HD
RWKV_LORA = 64
RWKV_IN = 3 * RWKV_W + 3 * 2 * RWKV_LORA
RWKV_CHUNK = 64
RWKV_GROUP = 4
RWKV_FINISH_ROWS = 256
DECAY_SCALE = float(np.exp(-0.5))
RWKV_BLOCK_ROWS = 1024
PAIR_W = 2 * RWKV_HD
N_PAIRS = RWKV_W // PAIR_W
GN_EPS = 64e-5

VMEM_LIMIT = 56 * 1024 * 1024

RWKV_PREC = dict(lora="bf", gate="bf", cumsum="x2r", amat="bf", inv="bf", state="bf", akv="bf", solve="x2r",
                 out="bf", update="bf")


def _split_bf16(a):
    hi = a.astype(BF16)
    return hi, (a - hi.astype(F32)).astype(BF16)


def _dot(a, b, dims, prec):
    dn = (dims, ((), ()))
    if prec == "hi":
        return lax.dot_general(a, b, dn, precision=HIGHEST, preferred_element_type=F32)
    one = lambda x, y: lax.dot_general(x, y, dn, preferred_element_type=F32)
    if prec == "x3":
        ah, al = _split_bf16(a)
        bh, bl = _split_bf16(b)
        return one(ah, bh) + one(ah, bl) + one(al, bh)
    if prec == "x2l":
        ah, al = _split_bf16(a)
        bh = b.astype(BF16)
        return one(ah, bh) + one(al, bh)
    if prec == "x2r":
        ah = a.astype(BF16)
        bh, bl = _split_bf16(b)
        return one(ah, bh) + one(ah, bl)
    assert prec == "bf", prec
    return lax.dot_general(a.astype(BF16), b.astype(BF16), dn, preferred_element_type=F32)


def _mm(a, b, prec="hi"):
    return _dot(a, b, ((1,), (0,)), prec)


def _mm_nt(a, b, prec="hi"):
    return _dot(a, b, ((1,), (1,)), prec)


def _mm_tn(a, b, prec="hi"):
    return _dot(a, b, ((0,), (0,)), prec)


def _silu(x):
    return x * jax.nn.sigmoid(x)


def _softplus(x):
    return jnp.maximum(x, 0.0) + jnp.log1p(jnp.exp(-jnp.abs(x)))


def _rms(x, w):
    return x * lax.rsqrt(jnp.mean(x * x, axis=-1, keepdims=True) + NORM_EPS) * w


def _tri_masks(n, rev):
    r = lax.broadcasted_iota(jnp.int32, (n, n), 0)
    c = lax.broadcasted_iota(jnp.int32, (n, n), 1)
    if rev:
        return r <= c, r < c
    return r >= c, r > c


def _unit_tri_inv_many(mats, prec):
    n = mats[0].shape[0]
    r = lax.broadcasted_iota(jnp.int32, (n, n), 0)
    c = lax.broadcasted_iota(jnp.int32, (n, n), 1)
    eye = jnp.where(r == c, 1.0, 0.0)
    ts = [eye - jnp.where((r >> 1) == (c >> 1), a, 0.0) for a in mats]
    shift = 1
    while (1 << shift) < n:
        join = ((r >> (shift + 1)) == (c >> (shift + 1))) & ((r >> shift) != (c >> shift))
        inner = [_mm(jnp.where(join, a, 0.0), t, prec) for a, t in zip(mats, ts)]
        ts = [t - _mm(t, w, prec) for t, w in zip(ts, inner)]
        shift += 1
    return ts


def _unit_tri_inv(a, prec):
    return _unit_tri_inv_many([a], prec)[0]


def _shifted_rows(x, prev_row, next_row):
    n = x.shape[0]
    row = lax.broadcasted_iota(jnp.int32, x.shape, 0)
    xp = jnp.where(row == 0, prev_row, pltpu.roll(x, 1, 0))
    xn = jnp.where(row == n - 1, next_row, pltpu.roll(x, n - 1, 0))
    return xp, xn


def _conv3(x, w_ref, seq_len):
    n = x.shape[0]
    assert seq_len & (seq_len - 1) == 0 and n % seq_len == 0
    pos = lax.broadcasted_iota(jnp.int32, x.shape, 0) & (seq_len - 1)
    xp = jnp.where(pos == 0, 0.0, pltpu.roll(x, 1, 0))
    xn = jnp.where(pos == seq_len - 1, 0.0, pltpu.roll(x, n - 1, 0))
    return xp * w_ref[0:1, :] + x * w_ref[1:2, :] + xn * w_ref[2:3, :]


def _mod_kernel(c_ref, w_ref, b_ref, o_ref):
    s = _silu(c_ref[...])
    o_ref[...] = _mm(s, w_ref[...], prec="bf") + b_ref[...]


def _modulation(c_rows, mod_w, mod_b):
    depth = mod_w.shape[0]
    nblk = (N_MOD * D_MODEL) // D_MODEL
    out = pl.pallas_call(
        _mod_kernel,
        grid=(depth, nblk),
        in_specs=[
            pl.BlockSpec((MOD_ROWS, D_MODEL), lambda l, j: (0, 0)),
            pl.BlockSpec((None, D_MODEL, D_MODEL), lambda l, j: (l, 0, j)),
            pl.BlockSpec((None, 1, D_MODEL), lambda l, j: (l, 0, j)),
        ],
        out_specs=pl.BlockSpec((None, MOD_ROWS, D_MODEL), lambda l, j: (l, 0, j)),
        out_shape=jax.ShapeDtypeStruct((depth, MOD_ROWS, N_MOD * D_MODEL), F32),
        compiler_params=pltpu.CompilerParams(dimension_semantics=("parallel", "parallel")),
        name="modulation",
    )(c_rows, mod_w, mod_b.reshape(depth, 1, N_MOD * D_MODEL))
    return out.reshape(depth, MOD_ROWS, N_MOD, D_MODEL)


def _mod_spec(layer, tiles_per_seq, row_base, row_step):
    return pl.BlockSpec((None, None, N_MOD, D_MODEL),
                        lambda i: (layer, row_base + (i // tiles_per_seq) * row_step, 0, 0))


def _inproj_kernel(x_ref, mod_ref, nw_ref, w_ref, *o_refs):
    h = _rms(x_ref[...], nw_ref[...])
    h = h * (1.0 + mod_ref[1:2, :]) + mod_ref[0:1, :]
    y = _mm(h, w_ref[...], prec="bf")
    off = 0
    for o_ref in o_refs:
        n = o_ref.shape[-1]
        o_ref[...] = y[:, off:off + n]
        off += n


def _inproj(x, mods, norm_w, w_bf16, splits, mod_spec):
    t = x.shape[0]
    n_in = w_bf16.shape[1]
    return pl.pallas_call(
        _inproj_kernel,
        grid=(t // TOKEN_TILE,),
        in_specs=[
            pl.BlockSpec((TOKEN_TILE, D_MODEL), lambda i: (i, 0)),
            mod_spec,
            pl.BlockSpec((1, D_MODEL), lambda i: (0, 0)),
            pl.BlockSpec((D_MODEL, n_in), lambda i: (0, 0)),
        ],
        out_specs=[pl.BlockSpec((TOKEN_TILE, n), lambda i: (i, 0)) for n in splits],
        out_shape=[jax.ShapeDtypeStruct((t, n), F32) for n in splits],
        compiler_params=pltpu.CompilerParams(dimension_semantics=("parallel",), vmem_limit_bytes=VMEM_LIMIT),
        name="inproj",
    )(x, mods, norm_w.reshape(1, D_MODEL), w_bf16)


def _mlp_kernel(a_ref, b_ref, x_ref, mod_ref, nw_ref, woa_ref, wob_ref, w1_ref, w2_ref, nf_ref, o_ref, *, final):
    y = _mm(a_ref[...], woa_ref[...], prec="bf") + _mm(b_ref[...], wob_ref[...], prec="bf")
    x1 = x_ref[...] + mod_ref[2:3, :] * y
    h = _rms(x1, nw_ref[...])
    h = (h * (1.0 + mod_ref[4:5, :]) + mod_ref[3:4, :]).astype(BF16)
    acc = jnp.zeros(x1.shape, F32)
    for j in range(D_FF // D_MODEL):
        cols = slice(j * D_MODEL, (j + 1) * D_MODEL)
        u = jnp.maximum(_mm(h, w1_ref[:, cols], prec="bf"), 0.0)
        acc = acc + _mm(u * u, w2_ref[cols, :], prec="bf")
    x2 = x1 + mod_ref[5:6, :] * acc
    if final:
        x2 = _rms(x2, nf_ref[...])
    o_ref[...] = x2


def _outproj_mlp(a, b, x, mods, norm_w, w_out, w1, w2, norm_final, mod_spec, final):
    t = x.shape[0]
    half = a.shape[1]
    const = lambda i: (0, 0)
    return pl.pallas_call(
        functools.partial(_mlp_kernel, final=final),
        grid=(t // TOKEN_TILE,),
        in_specs=[
            pl.BlockSpec((TOKEN_TILE, half), lambda i: (i, 0)),
            pl.BlockSpec((TOKEN_TILE, half), lambda i: (i, 0)),
            pl.BlockSpec((TOKEN_TILE, D_MODEL), lambda i: (i, 0)),
            mod_spec,
            pl.BlockSpec((1, D_MODEL), const),
            pl.BlockSpec((half, D_MODEL), const),
            pl.BlockSpec((half, D_MODEL), lambda i: (1, 0)),
            pl.BlockSpec((D_MODEL, D_FF), const),
            pl.BlockSpec((D_FF, D_MODEL), const),
            pl.BlockSpec((1, D_MODEL), const),
        ],
        out_specs=pl.BlockSpec((TOKEN_TILE, D_MODEL), lambda i: (i, 0)),
        out_shape=jax.ShapeDtypeStruct((t, D_MODEL), F32),
        compiler_params=pltpu.CompilerParams(dimension_semantics=("parallel",), vmem_limit_bytes=VMEM_LIMIT),
        name="outproj_mlp",
    )(a, b, x, mods, norm_w.reshape(1, D_MODEL), w_out, w_out, w1, w2, norm_final.reshape(1, D_MODEL))


def _gdn_decay_terms(g, rev):
    c = g.shape[0]
    incl = _tri_masks(c, rev)[0]
    before_col = _tri_masks(c, not rev)[0]
    eye = lax.broadcasted_iota(jnp.int32, (c, c), 0) == lax.broadcasted_iota(jnp.int32, (c, c), 1)
    gc_row = jnp.sum(jnp.where(before_col, jnp.broadcast_to(g, (c, c)), 0.0), axis=0, keepdims=True)
    gc_col = jnp.sum(jnp.where(eye, jnp.broadcast_to(gc_row, (c, c)), 0.0), axis=1, keepdims=True)
    decay = jnp.where(incl, jnp.exp(jnp.where(incl, gc_col - gc_row, 0.0)), 0.0)
    g_tot = jnp.sum(g, axis=0, keepdims=True)
    return decay, jnp.exp(gc_col), jnp.exp(g_tot - gc_col), jnp.exp(g_tot)


def _gdn_kernel(*refs, seq_len, seqs, has_init):
    (q_ref, k_ref, v_ref, z_ref, scb_ref, scc_ref, sch_ref, gate_ref,
     cq_ref, ck_ref, cv_ref, csc_ref, gn_ref, alog_ref, dtb_ref) = refs[:15]
    pos = 15
    if has_init:
        s0f_ref, s0b_ref = refs[pos:pos + 2]
        pos += 2
    o_ref, sco_ref, sf_ref, sb_ref = refs[pos:pos + 4]
    qs, ks, vs, osum, betas, gs, u_s, w_s, qd_s, kd_s, in_s, ge_s, st = refs[pos + 4:]
    head = pl.program_id(1)
    ch = GDN_CHUNK
    n_chunks = seq_len // ch

    def conv_silu(x_ref, c_ref):
        return _silu(_conv3(x_ref[...], c_ref, seq_len))

    def l2norm(x):
        return x * lax.rsqrt(jnp.sum(x * x, axis=-1, keepdims=True) + 1e-6)

    qs[...] = l2norm(conv_silu(q_ref, cq_ref)) * (GDN_D ** -0.5)
    ks[...] = l2norm(conv_silu(k_ref, ck_ref))
    vs[...] = conv_silu(v_ref, cv_ref)
    gates = gate_ref[...]
    betas[...] = jax.nn.sigmoid(gates)
    gs[...] = -jnp.exp(alog_ref[...]) * _softplus(gates + dtb_ref[...])
    sco_ref[...] = scb_ref[...] * _conv3(scc_ref[...] * sch_ref[...], csc_ref, seq_len)
    osum[...] = jnp.zeros(osum.shape, F32)

    lane = lax.broadcasted_iota(jnp.int32, (ch, 128), 1)

    def pick(ref, rows, col):
        return jnp.sum(jnp.where(lane == col, ref[rows, :], 0.0), axis=1, keepdims=True)

    def solve_group(gi, carry):
        items = []
        for j in range(GDN_GROUP):
            c = gi * GDN_GROUP + j
            rows = pl.ds(pl.multiple_of(c * ch, ch), ch)
            items.append(dict(c=c, rows=rows, q=qs[rows, :], k=ks[rows, :], v=vs[rows, :]))
        kks = [_mm_nt(it["k"], it["k"], "bf") for it in items]
        qks = [_mm_nt(it["q"], it["k"], "bf") for it in items]
        subs = []
        for it, kk, qk in zip(items, kks, qks):
            for d in range(2):
                beta = pick(betas, it["rows"], d * GDN_HEADS + head)
                g = pick(gs, it["rows"], 2 * GDN_HEADS + d * GDN_HEADS + head)
                decay, e_gc, e_rest, e_tot = _gdn_decay_terms(g, rev=(d == 1))
                strict = _tri_masks(ch, d == 1)[1]
                subs.append(dict(
                    d=d, c=it["c"], rows=it["rows"],
                    a=jnp.where(strict, kk * beta * decay, 0.0),
                    rhs=jnp.concatenate([it["v"] * beta, it["k"] * (beta * e_gc)], axis=1),
                    intra=qk * decay, qd=it["q"] * e_gc, kd=it["k"] * e_rest, ge=e_tot))
        ts = _unit_tri_inv_many([s["a"] for s in subs], "bf")
        uws = [_mm(t, s["rhs"], "bf") for t, s in zip(ts, subs)]
        for s, uw in zip(subs, uws):
            d, rows = s["d"], s["rows"]
            u_s[d, rows, :] = uw[:, :GDN_D]
            w_s[d, rows, :] = uw[:, GDN_D:].astype(BF16)
            qd_s[d, rows, :] = s["qd"].astype(BF16)
            kd_s[d, rows, :] = s["kd"].astype(BF16)
            in_s[d, rows, :] = s["intra"].astype(BF16)
            ge_s[d, pl.ds(pl.multiple_of(s["c"] * 8, 8), 8), :] = jnp.broadcast_to(s["ge"], (8, 128))
        return carry

    lax.fori_loop(0, seqs * n_chunks // GDN_GROUP, solve_group, 0)

    for j in range(seqs):
        st[0, j] = s0f_ref[j] if has_init else jnp.zeros((GDN_D, GDN_D), F32)
        st[1, j] = s0b_ref[j] if has_init else jnp.zeros((GDN_D, GDN_D), F32)

    def recur(i, carry):
        cs = []
        for j in range(seqs):
            for d in range(2):
                c = j * n_chunks + (i if d == 0 else n_chunks - 1 - i)
                cs.append(dict(d=d, j=j, rows=pl.ds(pl.multiple_of(c * ch, ch), ch),
                               ge=ge_s[d, pl.ds(pl.multiple_of(c * 8, 8), 1), :], s=st[d, j]))
        sbs = [c["s"].astype(BF16) for c in cs]
        wss = [_mm(w_s[c["d"], c["rows"], :], sb, "bf") for c, sb in zip(cs, sbs)]
        qss = [_mm(qd_s[c["d"], c["rows"], :], sb, "bf") for c, sb in zip(cs, sbs)]
        ebs = [(u_s[c["d"], c["rows"], :] - ws).astype(BF16) for c, ws in zip(cs, wss)]
        outs = [qs_ + _mm(in_s[c["d"], c["rows"], :], eb, "bf") for c, qs_, eb in zip(cs, qss, ebs)]
        s_news = [c["s"] * c["ge"] + _mm_tn(kd_s[c["d"], c["rows"], :], eb, "bf") for c, eb in zip(cs, ebs)]
        for c, o, s_new in zip(cs, outs, s_news):
            st[c["d"], c["j"]] = s_new
            osum[c["rows"], :] += o
        return carry

    lax.fori_loop(0, n_chunks, recur, 0)
    sf_ref[...] = st[0]
    sb_ref[...] = st[1]
    o_ref[...] = _rms(osum[...], gn_ref[...]) * _silu(z_ref[...])


def _gdn_mixer(proj, n_seq, seq_len, conv_w, sc_conv_w, gdn_norm, alog_vec, dtb_vec, s0_f, s0_b):
    has_init = s0_f is not None
    hd = GDN_HEADS
    seqs = max(1, GDN_BLOCK_ROWS // seq_len)
    rows = seqs * seq_len
    assert n_seq % seqs == 0 and (rows // GDN_CHUNK) % GDN_GROUP == 0

    def col(block):
        return pl.BlockSpec((rows, 128), lambda s, h, b=block: (s, b * hd + h))

    def wcol(block):
        return pl.BlockSpec((3, 128), lambda s, h, b=block: (0, b * hd + h))

    vec = pl.BlockSpec((1, 128), lambda s, h: (0, 0))
    state = pl.BlockSpec((seqs, None, None, GDN_D, GDN_D), lambda s, h: (s, 0, h, 0, 0))
    in_specs = [col(0), col(1), col(2), col(3), col(4), col(5), col(6),
                pl.BlockSpec((rows, 128), lambda s, h: (s, 7 * hd)),
                wcol(0), wcol(1), wcol(2), wcol(0), vec, vec, vec]
    args = [proj] * 8 + [conv_w, conv_w, conv_w, sc_conv_w, gdn_norm.reshape(1, 128), alog_vec, dtb_vec]
    if has_init:
        in_specs += [state, state]
        args += [s0_f, s0_b]
    t = n_seq * seq_len
    out_tok = pl.BlockSpec((rows, 128), lambda s, h: (s, h))
    scratch = ([pltpu.VMEM((rows, 128), F32) for _ in range(6)]
               + [pltpu.VMEM((2, rows, GDN_D), F32)]
               + [pltpu.VMEM((2, rows, GDN_D), BF16) for _ in range(3)]
               + [pltpu.VMEM((2, rows, GDN_CHUNK), BF16),
                  pltpu.VMEM((2, rows // GDN_CHUNK * 8, 128), F32),
                  pltpu.VMEM((2, seqs, GDN_D, GDN_D), F32)])
    return pl.pallas_call(
        functools.partial(_gdn_kernel, seq_len=seq_len, seqs=seqs, has_init=has_init),
        grid=(n_seq // seqs, hd),
        in_specs=in_specs,
        out_specs=[out_tok, out_tok, state, state],
        out_shape=[jax.ShapeDtypeStruct((t, hd * GDN_D), F32), jax.ShapeDtypeStruct((t, SC_WIDTH), F32),
                   jax.ShapeDtypeStruct((n_seq, 1, hd, GDN_D, GDN_D), F32),
                   jax.ShapeDtypeStruct((n_seq, 1, hd, GDN_D, GDN_D), F32)],
        scratch_shapes=scratch,
        compiler_params=pltpu.CompilerParams(dimension_semantics=("parallel", "parallel"),
                                             vmem_limit_bytes=VMEM_LIMIT),
        name="gdn_mixer",
    )(*args)


def _softmax_pv(scores, values, sink):
    m = sink
    for s in scores:
        m = jnp.maximum(m, jnp.max(s, axis=-1, keepdims=True))
    den = jnp.exp(sink - m)
    acc = None
    for s, v in zip(scores, values):
        e = jnp.exp(s - m)
        den = den + jnp.sum(e, axis=-1, keepdims=True)
        pv = _mm(e, v, prec="bf")
        acc = pv if acc is None else acc + pv
    return acc / den


def _group_sinks(sink_ref, j, rows):
    assert rows & (rows - 1) == 0
    grp = lax.broadcasted_iota(jnp.int32, (ATT_GROUP * rows, 1), 0) >> (rows.bit_length() - 1)
    col = jnp.full(grp.shape, sink_ref[j * ATT_GROUP], F32)
    for gi in range(1, ATT_GROUP):
        col = jnp.where(grp == gi, sink_ref[j * ATT_GROUP + gi], col)
    return col


def _store_group(o_ref, j, o, rows):
    for gi in range(ATT_GROUP):
        hh = j * ATT_GROUP + gi
        o_ref[:, hh * ATT_HD:(hh + 1) * ATT_HD] = o[gi * rows:(gi + 1) * rows]


def _attn_ctx_kernel(sink_ref, p_ref, o_ref, kc_ref, vc_ref):
    scale = ATT_HD ** -0.5
    rows = p_ref.shape[0]
    for j in range(ATT_KV_HEADS):
        k = p_ref[:, ATT_Q_W + j * ATT_HD:ATT_Q_W + (j + 1) * ATT_HD]
        v = p_ref[:, ATT_Q_W + ATT_KV_W + j * ATT_HD:ATT_Q_W + ATT_KV_W + (j + 1) * ATT_HD]
        kc_ref[j] = k
        vc_ref[j] = v
        q = jnp.concatenate([p_ref[:, hh * ATT_HD:(hh + 1) * ATT_HD]
                             for hh in range(j * ATT_GROUP, (j + 1) * ATT_GROUP)], axis=0)
        s = _mm_nt(q, k, prec="bf") * scale
        _store_group(o_ref, j, _softmax_pv([s], [v], _group_sinks(sink_ref, j, rows)), rows)


def _attn_context(proj_att, sink, n_seq, seq_len):
    kv = pl.BlockSpec((None, None, ATT_KV_HEADS, seq_len, ATT_HD), lambda b: (b, 0, 0, 0, 0))
    return pl.pallas_call(
        _attn_ctx_kernel,
        grid=(n_seq,),
        in_specs=[pl.BlockSpec(memory_space=pltpu.SMEM),
                  pl.BlockSpec((seq_len, ATT_W), lambda b: (b, 0))],
        out_specs=[pl.BlockSpec((seq_len, ATT_Q_W), lambda b: (b, 0)), kv, kv],
        out_shape=[jax.ShapeDtypeStruct((n_seq * seq_len, ATT_Q_W), F32),
                   jax.ShapeDtypeStruct((n_seq, 1, ATT_KV_HEADS, seq_len, ATT_HD), F32),
                   jax.ShapeDtypeStruct((n_seq, 1, ATT_KV_HEADS, seq_len, ATT_HD), F32)],
        compiler_params=pltpu.CompilerParams(dimension_semantics=("parallel",)),
        name="attn_context",
    )(sink, proj_att)


def _rope_tables(seq_len):
    pos = np.arange(seq_len)
    half = ATT_HD // 2
    inv = ROPE_BASE ** (-np.arange(0, half, 2, dtype=np.float32) / half)
    ang_r = (pos // GRID_W).astype(np.float32)[:, None] * inv
    ang_c = (pos % GRID_W).astype(np.float32)[:, None] * inv
    cos = np.concatenate([np.cos(ang_r), np.cos(ang_r), np.cos(ang_c), np.cos(ang_c)], axis=1)
    sin = np.concatenate([-np.sin(ang_r), np.sin(ang_r), -np.sin(ang_c), np.sin(ang_c)], axis=1)
    return (jnp.asarray(np.tile(cos, (1, 2)), F32), jnp.asarray(np.tile(sin, (1, 2)), F32))


def _rope(x, cos, sin):
    lane = lax.broadcasted_iota(jnp.int32, x.shape, 1)
    partner = jnp.where((lane & 31) < 16, pltpu.roll(x, 128 - 16, 1), pltpu.roll(x, 16, 1))
    return x * cos + partner * sin


def _attn_lat_kernel(sink_ref, p_ref, ck_ref, cv_ref, cos_ref, sin_ref, o_ref, *, seq_len):
    scale = ATT_HD ** -0.5
    qb = pl.program_id(1)
    span = 3 * ATT_BLOCK
    q0 = pl.multiple_of(qb * ATT_BLOCK, ATT_BLOCK)
    k0 = pl.multiple_of(jnp.clip((qb - 1) * ATT_BLOCK, 0, seq_len - span), ATT_BLOCK)
    qrows = pl.ds(q0, ATT_BLOCK)
    krows = pl.ds(k0, span)
    kwin = _rope(p_ref[krows, ATT_Q_W:ATT_Q_W + ATT_KV_W], cos_ref[krows, :], sin_ref[krows, :])
    vwin = p_ref[krows, ATT_Q_W + ATT_KV_W:ATT_W]
    stacked = ATT_GROUP * ATT_BLOCK
    qpos = q0 + (lax.broadcasted_iota(jnp.int32, (stacked, span), 0) & (ATT_BLOCK - 1))
    kpos = k0 + lax.broadcasted_iota(jnp.int32, (stacked, span), 1)
    valid = jnp.abs(qpos - kpos) <= WINDOW
    cos_q = cos_ref[qrows, :]
    sin_q = sin_ref[qrows, :]
    heads = []
    for pair in range(ATT_HEADS // 2):
        qpair = _rope(p_ref[qrows, pair * 128:(pair + 1) * 128], cos_q, sin_q)
        heads += [qpair[:, :ATT_HD], qpair[:, ATT_HD:]]
    kv = range(ATT_KV_HEADS)
    qs = [jnp.concatenate(heads[j * ATT_GROUP:(j + 1) * ATT_GROUP], axis=0) for j in kv]
    s_locs = [jnp.where(valid, _mm_nt(qs[j], kwin[:, j * ATT_HD:(j + 1) * ATT_HD], prec="bf") * scale, NEG_INF)
              for j in kv]
    s_ctxs = [_mm_nt(qs[j], ck_ref[j], prec="bf") * scale for j in kv]
    outs = [_softmax_pv([s_locs[j], s_ctxs[j]], [vwin[:, j * ATT_HD:(j + 1) * ATT_HD], cv_ref[j]],
                        _group_sinks(sink_ref, j, ATT_BLOCK)) for j in kv]
    for j in kv:
        _store_group(o_ref, j, outs[j], ATT_BLOCK)


def _attn_latent(proj_att, sink, cache_k, cache_v, layer, n_seq, seq_len):
    cos, sin = _rope_tables(seq_len)
    past = cache_k.shape[3]
    nqb = seq_len // ATT_BLOCK
    cache = pl.BlockSpec((None, None, ATT_KV_HEADS, past, ATT_HD), lambda b, q: (b, layer, 0, 0, 0))
    table = pl.BlockSpec((seq_len, 128), lambda b, q: (0, 0))
    return pl.pallas_call(
        functools.partial(_attn_lat_kernel, seq_len=seq_len),
        grid=(n_seq, nqb),
        in_specs=[pl.BlockSpec(memory_space=pltpu.SMEM),
                  pl.BlockSpec((seq_len, ATT_W), lambda b, q: (b, 0)),
                  cache, cache, table, table],
        out_specs=pl.BlockSpec((ATT_BLOCK, ATT_Q_W), lambda b, q: (b * nqb + q, 0)),
        out_shape=jax.ShapeDtypeStruct((n_seq * seq_len, ATT_Q_W), F32),
        compiler_params=pltpu.CompilerParams(dimension_semantics=("parallel", "parallel")),
        name="attn_latent",
    )(sink, proj_att, cache_k, cache_v, cos, sin)


def _pair_masks(n, rev):
    r = lax.broadcasted_iota(jnp.int32, (n, 2 * n), 0)
    c = lax.broadcasted_iota(jnp.int32, (n, 2 * n), 1) & (n - 1)
    return (r <= c, r < c) if rev else (r >= c, r > c)


def _bd(x):
    half = x.shape[1] // 2
    lane = lax.broadcasted_iota(jnp.int32, x.shape, 1)
    zero = jnp.zeros_like(x)
    return jnp.concatenate([jnp.where(lane < half, x, zero), jnp.where(lane >= half, x, zero)], axis=0)


def _bd_mask(n):
    r = lax.broadcasted_iota(jnp.int32, (2 * n, 2 * n), 0)
    c = lax.broadcasted_iota(jnp.int32, (2 * n, 2 * n), 1)
    return (r < n) == (c < n)


def _head_sums(x):
    ones = jnp.where(_bd_mask(RWKV_HD), 1.0, 0.0).astype(BF16)
    return _mm(x, ones, "x2l")


def _apply_pairs(tb, x):
    hi, lo = _split_bf16(x)
    return _mm(tb, _bd(hi), "bf") + _mm(tb, _bd(lo), "bf")


def _unit_tri_inv_pairs(mats):
    n = mats[0].shape[0]
    r = lax.broadcasted_iota(jnp.int32, (n, 2 * n), 0)
    c = lax.broadcasted_iota(jnp.int32, (n, 2 * n), 1) & (n - 1)
    eye = jnp.where(r == c, 1.0, 0.0)
    ts = [eye - jnp.where((r >> 1) == (c >> 1), a, 0.0) for a in mats]
    shift = 1
    while (1 << shift) < n:
        join = ((r >> (shift + 1)) == (c >> (shift + 1))) & ((r >> shift) != (c >> shift))
        tbs = [t.astype(BF16) for t in ts]
        inner = [_mm(jnp.where(join, a, 0.0), _bd(tb), "bf") for a, tb in zip(mats, tbs)]
        ts = [t - _mm(tb, _bd(w.astype(BF16)), "bf") for t, tb, w in zip(ts, tbs, inner)]
        shift += 1
    return ts


def _rwkv_chunk_operands(x_ref, prm, sc, c, *, seq_len):
    (mu_ref, w0_ref, wup_ref, a0_ref, aup_ref, gup_ref, kk_ref, ka_ref, rk_ref) = prm
    ch = RWKV_CHUNK
    total = x_ref.shape[0]
    r0 = pl.multiple_of(c * ch, ch)
    rows = pl.ds(r0, ch)
    pos0 = r0 & (seq_len - 1)
    x = x_ref[rows, :]
    prev_row = x_ref[pl.ds(jnp.maximum(r0 - 1, 0), 1), :] * jnp.where(pos0 > 0, 1.0, 0.0)
    next_row = x_ref[pl.ds(jnp.minimum(r0 + ch, total - 1), 1), :] * jnp.where(pos0 + ch < seq_len, 1.0, 0.0)
    xp, xn = _shifted_rows(x, prev_row, next_row)
    xs = x + mu_ref[0:1, :] * (xp - x) + mu_ref[1:2, :] * (xn - x)
    r = xs[:, 0:RWKV_W]
    k = xs[:, RWKV_W:2 * RWKV_W]
    v = xs[:, 2 * RWKV_W:3 * RWKV_W]
    lo = 3 * RWKV_W
    p = RWKV_PREC
    gl = xs[:, lo + 4 * RWKV_LORA:lo + 6 * RWKV_LORA]
    sc["gate"][rows, :] = _mm(jax.nn.sigmoid(gl), gup_ref[...], p["gate"])
    sc["v"][rows, :] = v.astype(BF16)
    pairs = [slice(i * PAIR_W, (i + 1) * PAIR_W) for i in range(N_PAIRS)]
    kkv = k * kk_ref[...]
    kaps = []
    for cols in pairs:
        kk_p = kkv[:, cols]
        kaps.append(kk_p * lax.rsqrt(_head_sums(kk_p * kk_p) + 1e-6))
    items = []
    bonus = None
    for d in range(2):
        rev = d == 1
        wl = xs[:, lo + d * RWKV_LORA:lo + (d + 1) * RWKV_LORA]
        al = xs[:, lo + 2 * RWKV_LORA + d * RWKV_LORA:lo + 2 * RWKV_LORA + (d + 1) * RWKV_LORA]
        lw = -DECAY_SCALE * jax.nn.sigmoid(w0_ref[d:d + 1, :] + _mm(jnp.tanh(wl), wup_ref[d], p["lora"]))
        a = jax.nn.sigmoid(a0_ref[d:d + 1, :] + _mm(al, aup_ref[d], p["lora"]))
        k2 = k * (1.0 + (a - 1.0) * ka_ref[...])
        g_inc = _mm(jnp.where(_tri_masks(ch, rev)[0], 1.0, 0.0), lw, p["cumsum"])
        g_tot = jnp.sum(lw, axis=0, keepdims=True)
        e_neg = jnp.exp(-g_inc)
        e_end = jnp.exp(g_tot - g_inc)
        e_exc = jnp.exp(g_inc - lw)
        r_dec = r * jnp.exp(g_inc)
        k_neg = k2 * e_neg
        k_end = k2 * e_end
        sc["dec"][d, pl.ds(pl.multiple_of(c * 8, 8), 8), :] = jnp.broadcast_to(jnp.exp(g_tot), (8, RWKV_W))
        rkr = r * k2 * rk_ref[...]
        bon_d = jnp.concatenate([_head_sums(rkr[:, cols]) for cols in pairs], axis=1) * v
        bonus = bon_d if bonus is None else bonus + bon_d
        for cols, kap in zip(pairs, kaps):
            b_p = kap * a[:, cols]
            items.append(dict(d=d, rows=rows, cols=cols, kap_dec=kap * e_exc[:, cols], r_dec=r_dec[:, cols],
                              b_neg=b_p * e_neg[:, cols], k_neg=k_neg[:, cols],
                              b_end=b_p * e_end[:, cols], k_end=k_end[:, cols], vb=v[:, cols].astype(BF16)))
    sc["bon"][rows, :] = bonus
    return items


def _rwkv_solve(items, sc):
    ch = RWKV_CHUNK
    masks = [_pair_masks(ch, False), _pair_masks(ch, True)]
    ms = [_mm_nt(jnp.concatenate([it["kap_dec"], it["r_dec"]], axis=0),
                 jnp.concatenate([_bd(it["b_neg"].astype(BF16)), _bd(it["k_neg"].astype(BF16))], axis=0), "bf")
          for it in items]
    a_abs = [jnp.where(masks[it["d"]][1], m[:ch, :2 * ch], 0.0) for it, m in zip(items, ms)]
    a_aks = [jnp.where(masks[it["d"]][1], m[:ch, 2 * ch:], 0.0) for it, m in zip(items, ms)]
    a_rbs = [jnp.where(masks[it["d"]][0], m[ch:, :2 * ch], 0.0) for it, m in zip(items, ms)]
    a_rks = [jnp.where(masks[it["d"]][0], m[ch:, 2 * ch:], 0.0) for it, m in zip(items, ms)]
    tbs = [t.astype(BF16) for t in _unit_tri_inv_pairs(a_abs)]
    akvs = [_mm(a_ak, _bd(it["vb"]), "bf") for it, a_ak in zip(items, a_aks)]
    w2s = [_apply_pairs(tb, it["kap_dec"]) for it, tb in zip(items, tbs)]
    u0s = [_apply_pairs(tb, akv) for tb, akv in zip(tbs, akvs)]
    for it, w2, u0, a_rb, a_rk in zip(items, w2s, u0s, a_rbs, a_rks):
        d, rows, cols = it["d"], it["rows"], it["cols"]
        sc["w2"][d, rows, cols] = w2.astype(BF16)
        sc["rd"][d, rows, cols] = it["r_dec"].astype(BF16)
        sc["u0"][d, rows, cols] = u0
        sc["arb"][d, rows, cols] = a_rb.astype(BF16)
        sc["ark"][d, rows, cols] = a_rk.astype(BF16)
        sc["bh"][d, rows, cols] = it["b_end"].astype(BF16)
        sc["kh"][d, rows, cols] = it["k_end"].astype(BF16)


def _rwkv_recur(sc, st, i, *, seq_len, seqs):
    ch = RWKV_CHUNK
    n_chunks = seq_len // ch
    keep = _bd_mask(RWKV_HD)
    cs = []
    for j in range(seqs):
        for d in range(2):
            c = j * n_chunks + (i if d == 0 else n_chunks - 1 - i)
            rows = pl.ds(pl.multiple_of(c * ch, ch), ch)
            dec = sc["dec"][d, pl.ds(pl.multiple_of(c * 8, 8), 1), :]
            for pi in range(N_PAIRS):
                cols = slice(pi * PAIR_W, (pi + 1) * PAIR_W)
                cs.append(dict(d=d, j=j, p=pi, rows=rows, cols=cols, dec=dec[:, cols], s=st[d, j, pi]))
    sbs = [c["s"].astype(BF16) for c in cs]
    lss = [_mm_nt(jnp.concatenate([sc["w2"][c["d"], c["rows"], c["cols"]],
                                   sc["rd"][c["d"], c["rows"], c["cols"]]], axis=0), sb, "bf")
           for c, sb in zip(cs, sbs)]
    ubs = [(-(ls[:ch] + sc["u0"][c["d"], c["rows"], c["cols"]])).astype(BF16) for c, ls in zip(cs, lss)]
    vbs = [sc["v"][c["rows"], c["cols"]] for c in cs]
    ys = [ls[ch:] + _mm(jnp.concatenate([sc["arb"][c["d"], c["rows"], c["cols"]],
                                         sc["ark"][c["d"], c["rows"], c["cols"]]], axis=1),
                        jnp.concatenate([_bd(ub), _bd(vb)], axis=0), "bf")
          for c, ls, ub, vb in zip(cs, lss, ubs, vbs)]
    ups = [_mm_tn(jnp.concatenate([ub, vb], axis=0),
                  jnp.concatenate([sc["bh"][c["d"], c["rows"], c["cols"]],
                                   sc["kh"][c["d"], c["rows"], c["cols"]]], axis=0), "bf")
           for c, ub, vb in zip(cs, ubs, vbs)]
    for c, y, up in zip(cs, ys, ups):
        st[c["d"], c["j"], c["p"]] = c["s"] * c["dec"] + jnp.where(keep, up, 0.0)
        sc["ysum"][c["rows"], c["cols"]] += y


RWKV_SCRATCH = ("ysum", "bon", "gate", "v", "w2", "rd", "u0", "arb", "ark", "bh", "kh", "dec")


def _rwkv_kernel(*refs, seq_len, seqs, has_init):
    x_ref = refs[0]
    prm = refs[1:10]
    lnw_ref, lnb_ref = refs[10:12]
    pos = 12
    if has_init:
        s0f_ref, s0b_ref = refs[pos:pos + 2]
        pos += 2
    o_ref, sf_ref, sb_ref = refs[pos:pos + 3]
    sc = dict(zip(RWKV_SCRATCH, refs[pos + 3:]))
    st = refs[pos + 3 + len(RWKV_SCRATCH)]
    ch = RWKV_CHUNK
    hd = RWKV_HD
    n_chunks = seq_len // ch
    sc["ysum"][...] = jnp.zeros(sc["ysum"].shape, F32)

    def prepare(gi, carry):
        items = []
        for j in range(RWKV_GROUP):
            items += _rwkv_chunk_operands(x_ref, prm, sc, gi * RWKV_GROUP + j, seq_len=seq_len)
        _rwkv_solve(items, sc)
        return carry

    lax.fori_loop(0, seqs * n_chunks // RWKV_GROUP, prepare, 0)

    zero = jnp.zeros((hd, hd), F32)
    for d, s0_ref in enumerate((s0f_ref, s0b_ref) if has_init else (None, None)):
        for j in range(seqs):
            for pi in range(N_PAIRS):
                s_a = s0_ref[j, 2 * pi] if has_init else zero
                s_b = s0_ref[j, 2 * pi + 1] if has_init else zero
                st[d, j, pi] = jnp.concatenate([jnp.concatenate([s_a, zero], axis=1),
                                                jnp.concatenate([zero, s_b], axis=1)], axis=0)

    def recur(i, carry):
        _rwkv_recur(sc, st, i, seq_len=seq_len, seqs=seqs)
        return carry

    lax.fori_loop(0, n_chunks, recur, 0)
    for d, out_ref in enumerate((sf_ref, sb_ref)):
        for j in range(seqs):
            for pi in range(N_PAIRS):
                s = st[d, j, pi]
                out_ref[j, 2 * pi] = s[:hd, :hd]
                out_ref[j, 2 * pi + 1] = s[hd:, hd:]

    tile = RWKV_FINISH_ROWS

    def finish(i, carry):
        rows = pl.ds(pl.multiple_of(i * tile, tile), tile)
        pairs = [slice(pi * PAIR_W, (pi + 1) * PAIR_W) for pi in range(N_PAIRS)]
        ys = [sc["ysum"][rows, cols] for cols in pairs]
        cens = [y - _head_sums(y) * (1.0 / hd) for y in ys]
        vars_ = [_head_sums(cen * cen) * (1.0 / hd) for cen in cens]
        for cols, cen, var in zip(pairs, cens, vars_):
            yn = cen * lax.rsqrt(var + GN_EPS) * lnw_ref[:, cols] + lnb_ref[:, cols]
            o_ref[rows, cols] = (yn + sc["bon"][rows, cols]) * sc["gate"][rows, cols]
        return carry

    lax.fori_loop(0, seqs * seq_len // tile, finish, 0)


def _rwkv_mixer(x_rw, n_seq, seq_len, params, s0_f, s0_b):
    has_init = s0_f is not None
    (mu, w0, w_up, a0, a_up, g_up, k_k, k_a, r_k, ln_w, ln_b) = params
    row = lambda a: a.reshape(1, RWKV_W)
    args = [x_rw, mu, w0, w_up, a0, a_up, g_up, row(k_k), row(k_a), row(r_k), row(ln_w), row(ln_b)]

    def whole(a):
        nd = a.ndim
        return pl.BlockSpec(a.shape, lambda s, nd=nd: (0,) * nd)

    seqs = max(1, RWKV_BLOCK_ROWS // seq_len)
    rows = seqs * seq_len
    assert n_seq % seqs == 0 and (rows // RWKV_CHUNK) % RWKV_GROUP == 0 and seq_len & (seq_len - 1) == 0
    in_specs = [pl.BlockSpec((rows, RWKV_IN), lambda s: (s, 0))] + [whole(a) for a in args[1:]]
    state = pl.BlockSpec((seqs, None, RWKV_HEADS, RWKV_HD, RWKV_HD), lambda s: (s, 0, 0, 0, 0))
    if has_init:
        in_specs += [state, state]
        args += [s0_f, s0_b]
    st_shape = jax.ShapeDtypeStruct((n_seq, 1, RWKV_HEADS, RWKV_HD, RWKV_HD), F32)
    tok = lambda dt: pltpu.VMEM((rows, RWKV_W), dt)
    per_dir = lambda dt: pltpu.VMEM((2, rows, RWKV_W), dt)
    scratch = dict(ysum=tok(F32), bon=tok(F32), gate=tok(F32), v=tok(BF16), w2=per_dir(BF16), rd=per_dir(BF16),
                   u0=per_dir(F32), arb=per_dir(BF16), ark=per_dir(BF16), bh=per_dir(BF16), kh=per_dir(BF16),
                   dec=pltpu.VMEM((2, rows // RWKV_CHUNK * 8, RWKV_W), F32))
    return pl.pallas_call(
        functools.partial(_rwkv_kernel, seq_len=seq_len, seqs=seqs, has_init=has_init),
        grid=(n_seq // seqs,),
        in_specs=in_specs,
        out_specs=[pl.BlockSpec((rows, RWKV_W), lambda s: (s, 0)), state, state],
        out_shape=[jax.ShapeDtypeStruct((n_seq * seq_len, RWKV_W), F32), st_shape, st_shape],
        scratch_shapes=[scratch[name] for name in RWKV_SCRATCH]
        + [pltpu.VMEM((2, seqs, N_PAIRS, PAIR_W, PAIR_W), F32)],
        compiler_params=pltpu.CompilerParams(dimension_semantics=("parallel",), vmem_limit_bytes=VMEM_LIMIT),
        name="rwkv_mixer",
    )(*args)


def kernel(x_prompt, x_sample, state_gdn_fwd, state_gdn_bwd, cache_attn_k, cache_attn_v, state_rwkv_fwd, state_rwkv_bwd, c, c_ctx, mod_w, mod_b, norm_mix, norm_mlp, mlp_w1, mlp_w2, norm_final, ev_w_in, ev_w_out, gdn_conv, gdn_a_log, gdn_dt_bias, gdn_norm, sc_conv, od_w_in, od_w_out, attn_sink, rwkv_mu, rwkv_w0, rwkv_w_up, rwkv_a0, rwkv_a_up, rwkv_g_up, rwkv_k_k, rwkv_k_a, rwkv_r_k, rwkv_ln_w, rwkv_ln_b):
    bp, lp, _ = x_prompt.shape
    bs, ls, _ = x_sample.shape
    depth = mod_w.shape[0]
    c_rows = jnp.concatenate([c_ctx[None, :], c, jnp.zeros((MOD_ROWS - 1 - bs, D_MODEL), F32)], axis=0)
    mods = _modulation(c_rows, mod_w, mod_b)

    groups = [
        dict(x=x_prompt.reshape(bp * lp, D_MODEL), n=bp, l=lp, latent=False,
             mod=lambda layer: _mod_spec(layer, lp // TOKEN_TILE, 0, 0)),
        dict(x=x_sample.reshape(bs * ls, D_MODEL), n=bs, l=ls, latent=True,
             mod=lambda layer: _mod_spec(layer, ls // TOKEN_TILE, 1, 1)),
    ]
    outs = {}
    for layer in range(depth):
        w1 = mlp_w1[layer].astype(BF16)
        w2 = mlp_w2[layer].astype(BF16)
        final = layer == depth - 1
        if layer % 2 == 0:
            e = layer // 2
            w = ev_w_in[e]
            qkvz = GDN_QKV_W + GDN_HEADS * GDN_D
            n_gate = 4 * GDN_HEADS
            w_in = jnp.concatenate(
                [w[:, :qkvz], w[:, qkvz + n_gate:], w[:, qkvz:qkvz + n_gate],
                 jnp.zeros((D_MODEL, EV_IN_PAD - w.shape[1]), F32)], axis=1).astype(BF16)
            w_out = ev_w_out[e].astype(BF16)
            alog_vec = jnp.zeros((1, 128), F32).at[0, 2 * GDN_HEADS:4 * GDN_HEADS].set(gdn_a_log[e].reshape(-1))
            dtb_vec = jnp.zeros((1, 128), F32).at[0, 2 * GDN_HEADS:4 * GDN_HEADS].set(gdn_dt_bias[e].reshape(-1))
            for grp in groups:
                (proj,) = _inproj(grp["x"], mods, norm_mix[layer], w_in, (EV_IN_PAD,), grp["mod"](layer))
                s0 = (state_gdn_fwd[:, e:e + 1], state_gdn_bwd[:, e:e + 1]) if grp["latent"] else (None, None)
                o, sc, s_f, s_b = _gdn_mixer(proj, grp["n"], grp["l"], gdn_conv[e], sc_conv[e], gdn_norm[e],
                                             alog_vec, dtb_vec, *s0)
                if not grp["latent"]:
                    outs.setdefault("gdn_f", []).append(s_f)
                    outs.setdefault("gdn_b", []).append(s_b)
                grp["x"] = _outproj_mlp(o, sc, grp["x"], mods, norm_mlp[layer], w_out, w1, w2, norm_final,
                                        grp["mod"](layer), final)
        else:
            o_ = layer // 2
            w = od_w_in[o_]
            w_in = jnp.concatenate([w[:, ATT_W:], w[:, :ATT_W]], axis=1).astype(BF16)
            w_out = od_w_out[o_].astype(BF16)
            rw = (rwkv_mu[o_], rwkv_w0[o_], rwkv_w_up[o_], rwkv_a0[o_], rwkv_a_up[o_], rwkv_g_up[o_],
                  rwkv_k_k[o_], rwkv_k_a[o_], rwkv_r_k[o_].reshape(-1), rwkv_ln_w[o_], rwkv_ln_b[o_])
            for grp in groups:
                x_rw, p_att = _inproj(grp["x"], mods, norm_mix[layer], w_in, (RWKV_IN, ATT_W), grp["mod"](layer))
                if grp["latent"]:
                    att = _attn_latent(p_att, attn_sink[o_], cache_attn_k, cache_attn_v, o_, grp["n"], grp["l"])
                    rwo, _, _ = _rwkv_mixer(x_rw, grp["n"], grp["l"], rw,
                                            state_rwkv_fwd[:, o_:o_ + 1], state_rwkv_bwd[:, o_:o_ + 1])
                else:
                    att, kc, vc = _attn_context(p_att, attn_sink[o_], grp["n"], grp["l"])
                    rwo, s_f, s_b = _rwkv_mixer(x_rw, grp["n"], grp["l"], rw, None, None)
                    outs.setdefault("att_k", []).append(kc)
                    outs.setdefault("att_v", []).append(vc)
                    outs.setdefault("rw_f", []).append(s_f)
                    outs.setdefault("rw_b", []).append(s_b)
                grp["x"] = _outproj_mlp(att, rwo, grp["x"], mods, norm_mlp[layer], w_out, w1, w2, norm_final,
                                        grp["mod"](layer), final)
    cat = lambda key: jnp.concatenate(outs[key], axis=1)
    return (groups[0]["x"].reshape(bp, lp, D_MODEL), groups[1]["x"].reshape(bs, ls, D_MODEL),
            cat("gdn_f"), cat("gdn_b"), cat("att_k"), cat("att_v"), cat("rw_f"), cat("rw_b"))
```

```python
import functools

import jax
import jax.numpy as jnp
import numpy as np
from jax import lax
from jax.experimental import pallas as pl
from jax.experimental.pallas import tpu as pltpu

F32 = jnp.float32
BF16 = jnp.bfloat16
HIGHEST = lax.Precision.HIGHEST

D_MODEL = 1024
N_MOD = 6
D_FF = 4 * D_MODEL
NORM_EPS = 1e-6
TOKEN_TILE = 512
MOD_ROWS = 8

GDN_HEADS = 4
GDN_D = 128
GDN_CHUNK = 128
GDN_GROUP = 8
GDN_BLOCK_ROWS = 1024
GDN_QKV_W = 3 * GDN_HEADS * GDN_D
SC_WIDTH = 512
EV_IN_PAD = 3712

ATT_HEADS = 8
ATT_KV_HEADS = 2
ATT_GROUP = ATT_HEADS // ATT_KV_HEADS
ATT_HD = 64
ATT_Q_W = ATT_HEADS * ATT_HD
ATT_KV_W = ATT_KV_HEADS * ATT_HD
ATT_W = ATT_Q_W + 2 * ATT_KV_W
WINDOW = 128
ATT_BLOCK = 128
GRID_W = 64
ROPE_BASE = 10000.0
NEG_INF = -1e30

RWKV_HEADS = 8
RWKV_HD = 64
RWKV_W = RWKV_HEADS * RWKV_HD
RWKV_LORA = 64
RWKV_IN = 3 * RWKV_W + 3 * 2 * RWKV_LORA
RWKV_CHUNK = 64
RWKV_GROUP = 4
RWKV_FINISH_ROWS = 256
DECAY_SCALE = float(np.exp(-0.5))
RWKV_BLOCK_ROWS = 1024
PAIR_W = 2 * RWKV_HD
N_PAIRS = RWKV_W // PAIR_W
GN_EPS = 64e-5

VMEM_LIMIT = 56 * 1024 * 1024

RWKV_PREC = dict(lora="bf", gate="bf", cumsum="x2r", amat="bf", inv="bf", state="bf", akv="bf", solve="x2r",
                 out="bf", update="bf")


def _split_bf16(a):
    hi = a.astype(BF16)
    return hi, (a - hi.astype(F32)).astype(BF16)


def _dot(a, b, dims, prec):
    dn = (dims, ((), ()))
    if prec == "hi":
        return lax.dot_general(a, b, dn, precision=HIGHEST, preferred_element_type=F32)
    one = lambda x, y: lax.dot_general(x, y, dn, preferred_element_type=F32)
    if prec == "x3":
        ah, al = _split_bf16(a)
        bh, bl = _split_bf16(b)
        return one(ah, bh) + one(ah, bl) + one(al, bh)
    if prec == "x2l":
        ah, al = _split_bf16(a)
        bh = b.astype(BF16)
        return one(ah, bh) + one(al, bh)
    if prec == "x2r":
        ah = a.astype(BF16)
        bh, bl = _split_bf16(b)
        return one(ah, bh) + one(ah, bl)
    assert prec == "bf", prec
    return lax.dot_general(a.astype(BF16), b.astype(BF16), dn, preferred_element_type=F32)


def _mm(a, b, prec="hi"):
    return _dot(a, b, ((1,), (0,)), prec)


def _mm_nt(a, b, prec="hi"):
    return _dot(a, b, ((1,), (1,)), prec)


def _mm_tn(a, b, prec="hi"):
    return _dot(a, b, ((0,), (0,)), prec)


def _silu(x):
    return x * jax.nn.sigmoid(x)


def _softplus(x):
    return jnp.maximum(x, 0.0) + jnp.log1p(jnp.exp(-jnp.abs(x)))


def _rms(x, w):
    return x * lax.rsqrt(jnp.mean(x * x, axis=-1, keepdims=True) + NORM_EPS) * w


def _tri_masks(n, rev):
    r = lax.broadcasted_iota(jnp.int32, (n, n), 0)
    c = lax.broadcasted_iota(jnp.int32, (n, n), 1)
    if rev:
        return r <= c, r < c
    return r >= c, r > c


def _unit_tri_inv_many(mats, prec):
    assert prec == "bf"
    n = mats[0].shape[0]
    r = lax.broadcasted_iota(jnp.int32, (n, n), 0)
    c = lax.broadcasted_iota(jnp.int32, (n, n), 1)
    eye = jnp.where(r == c, 1.0, 0.0)
    ts = [eye - a * jnp.where((r >> 1) == (c >> 1), 1.0, 0.0) for a in mats]
    abs_ = [a.astype(BF16) for a in mats]
    shift = 1
    while (1 << shift) < n:
        join = jnp.where(((r >> (shift + 1)) == (c >> (shift + 1))) & ((r >> shift) != (c >> shift)),
                         1.0, 0.0).astype(BF16)
        tbs = [t.astype(BF16) for t in ts]
        inner = [_mm(ab * join, tb, prec) for ab, tb in zip(abs_, tbs)]
        ts = [t - _mm(tb, w, prec) for t, tb, w in zip(ts, tbs, inner)]
        shift += 1
    return ts


def _shifted_rows(x, prev_row, next_row):
    n = x.shape[0]
    row = lax.broadcasted_iota(jnp.int32, x.shape, 0)
    xp = jnp.where(row == 0, prev_row, pltpu.roll(x, 1, 0))
    xn = jnp.where(row == n - 1, next_row, pltpu.roll(x, n - 1, 0))
    return xp, xn


def _conv3(x, w_ref, seq_len):
    n = x.shape[0]
    assert seq_len & (seq_len - 1) == 0 and n % seq_len == 0
    pos = lax.broadcasted_iota(jnp.int32, x.shape, 0) & (seq_len - 1)
    xp = jnp.where(pos == 0, 0.0, pltpu.roll(x, 1, 0))
    xn = jnp.where(pos == seq_len - 1, 0.0, pltpu.roll(x, n - 1, 0))
    return xp * w_ref[0:1, :] + x * w_ref[1:2, :] + xn * w_ref[2:3, :]


def _mod_kernel(c_ref, w_ref, b_ref, o_ref):
    s = _silu(c_ref[...])
    o_ref[...] = _mm(s, w_ref[...], prec="bf") + b_ref[...]


def _modulation(c_rows, mod_w, mod_b):
    depth = mod_w.shape[0]
    nblk = (N_MOD * D_MODEL) // D_MODEL
    out = pl.pallas_call(
        _mod_kernel,
        grid=(depth, nblk),
        in_specs=[
            pl.BlockSpec((MOD_ROWS, D_MODEL), lambda l, j: (0, 0)),
            pl.BlockSpec((None, D_MODEL, D_MODEL), lambda l, j: (l, 0, j)),
            pl.BlockSpec((None, 1, D_MODEL), lambda l, j: (l, 0, j)),
        ],
        out_specs=pl.BlockSpec((None, MOD_ROWS, D_MODEL), lambda l, j: (l, 0, j)),
        out_shape=jax.ShapeDtypeStruct((depth, MOD_ROWS, N_MOD * D_MODEL), F32),
        compiler_params=pltpu.CompilerParams(dimension_semantics=("parallel", "parallel")),
        name="modulation",
    )(c_rows, mod_w, mod_b.reshape(depth, 1, N_MOD * D_MODEL))
    return out.reshape(depth, MOD_ROWS, N_MOD, D_MODEL)


def _mod_spec(layer, tiles_per_seq, row_base, row_step):
    return pl.BlockSpec((None, None, N_MOD, D_MODEL),
                        lambda i: (layer, row_base + (i // tiles_per_seq) * row_step, 0, 0))


def _inproj_kernel(x_ref, mod_ref, nw_ref, w_ref, *o_refs):
    h = _rms(x_ref[...], nw_ref[...])
    h = h * (1.0 + mod_ref[1:2, :]) + mod_ref[0:1, :]
    y = _mm(h, w_ref[...], prec="bf")
    off = 0
    for o_ref in o_refs:
        n = o_ref.shape[-1]
        o_ref[...] = y[:, off:off + n]
        off += n


def _inproj(x, mods, norm_w, w_bf16, splits, mod_spec):
    t = x.shape[0]
    n_in = w_bf16.shape[1]
    return pl.pallas_call(
        _inproj_kernel,
        grid=(t // TOKEN_TILE,),
        in_specs=[
            pl.BlockSpec((TOKEN_TILE, D_MODEL), lambda i: (i, 0)),
            mod_spec,
            pl.BlockSpec((1, D_MODEL), lambda i: (0, 0)),
            pl.BlockSpec((D_MODEL, n_in), lambda i: (0, 0)),
        ],
        out_specs=[pl.BlockSpec((TOKEN_TILE, n), lambda i: (i, 0)) for n in splits],
        out_shape=[jax.ShapeDtypeStruct((t, n), F32) for n in splits],
        compiler_params=pltpu.CompilerParams(dimension_semantics=("parallel",), vmem_limit_bytes=VMEM_LIMIT),
        name="inproj",
    )(x, mods, norm_w.reshape(1, D_MODEL), w_bf16)


def _mlp_kernel(a_ref, b_ref, x_ref, mod_ref, nw_ref, woa_ref, wob_ref, w1_ref, w2_ref, nf_ref, o_ref, *, final):
    y = _mm(a_ref[...], woa_ref[...], prec="bf") + _mm(b_ref[...], wob_ref[...], prec="bf")
    x1 = x_ref[...] + mod_ref[2:3, :] * y
    h = _rms(x1, nw_ref[...])
    h = (h * (1.0 + mod_ref[4:5, :]) + mod_ref[3:4, :]).astype(BF16)
    acc = jnp.zeros(x1.shape, F32)
    for j in range(D_FF // D_MODEL):
        cols = slice(j * D_MODEL, (j + 1) * D_MODEL)
        u = jnp.maximum(_mm(h, w1_ref[:, cols], prec="bf"), 0.0)
        acc = acc + _mm(u * u, w2_ref[cols, :], prec="bf")
    x2 = x1 + mod_ref[5:6, :] * acc
    if final:
        x2 = _rms(x2, nf_ref[...])
    o_ref[...] = x2


def _outproj_mlp(a, b, x, mods, norm_w, w_out, w1, w2, norm_final, mod_spec, final):
    t = x.shape[0]
    half = a.shape[1]
    const = lambda i: (0, 0)
    return pl.pallas_call(
        functools.partial(_mlp_kernel, final=final),
        grid=(t // TOKEN_TILE,),
        in_specs=[
            pl.BlockSpec((TOKEN_TILE, half), lambda i: (i, 0)),
            pl.BlockSpec((TOKEN_TILE, half), lambda i: (i, 0)),
            pl.BlockSpec((TOKEN_TILE, D_MODEL), lambda i: (i, 0)),
            mod_spec,
            pl.BlockSpec((1, D_MODEL), const),
            pl.BlockSpec((half, D_MODEL), const),
            pl.BlockSpec((half, D_MODEL), lambda i: (1, 0)),
            pl.BlockSpec((D_MODEL, D_FF), const),
            pl.BlockSpec((D_FF, D_MODEL), const),
            pl.BlockSpec((1, D_MODEL), const),
        ],
        out_specs=pl.BlockSpec((TOKEN_TILE, D_MODEL), lambda i: (i, 0)),
        out_shape=jax.ShapeDtypeStruct((t, D_MODEL), F32),
        compiler_params=pltpu.CompilerParams(dimension_semantics=("parallel",), vmem_limit_bytes=VMEM_LIMIT),
        name="outproj_mlp",
    )(a, b, x, mods, norm_w.reshape(1, D_MODEL), w_out, w_out, w1, w2, norm_final.reshape(1, D_MODEL))


def _gdn_decay_terms(g, rev):
    c = g.shape[0]
    incl = _tri_masks(c, rev)[0]
    before_col = _tri_masks(c, not rev)[0]
    eye = lax.broadcasted_iota(jnp.int32, (c, c), 0) == lax.broadcasted_iota(jnp.int32, (c, c), 1)
    gc_row = jnp.sum(jnp.where(before_col, jnp.broadcast_to(g, (c, c)), 0.0), axis=0, keepdims=True)
    gc_col = jnp.sum(jnp.where(eye, jnp.broadcast_to(gc_row, (c, c)), 0.0), axis=1, keepdims=True)
    decay = jnp.where(incl, jnp.exp(jnp.where(incl, gc_col - gc_row, 0.0)), 0.0)
    g_tot = jnp.sum(g, axis=0, keepdims=True)
    return decay, jnp.exp(gc_col), jnp.exp(g_tot - gc_col), jnp.exp(g_tot)


def _gdn_kernel(*refs, seq_len, seqs, has_init):
    (q_ref, k_ref, v_ref, z_ref, scb_ref, scc_ref, sch_ref, gate_ref,
     cq_ref, ck_ref, cv_ref, csc_ref, gn_ref, alog_ref, dtb_ref) = refs[:15]
    pos = 15
    if has_init:
        s0f_ref, s0b_ref = refs[pos:pos + 2]
        pos += 2
    o_ref, sco_ref, sf_ref, sb_ref = refs[pos:pos + 4]
    qs, ks, vs, osum, betas, gs, u_s, w_s, qd_s, kd_s, in_s, ge_s, st = refs[pos + 4:]
    head = pl.program_id(1)
    ch = GDN_CHUNK
    n_chunks = seq_len // ch

    def conv_silu(x_ref, c_ref):
        return _silu(_conv3(x_ref[...], c_ref, seq_len))

    def l2norm(x):
        return x * lax.rsqrt(jnp.sum(x * x, axis=-1, keepdims=True) + 1e-6)

    qs[...] = l2norm(conv_silu(q_ref, cq_ref)) * (GDN_D ** -0.5)
    ks[...] = l2norm(conv_silu(k_ref, ck_ref))
    vs[...] = conv_silu(v_ref, cv_ref)
    gates = gate_ref[...]
    betas[...] = jax.nn.sigmoid(gates)
    gs[...] = -jnp.exp(alog_ref[...]) * _softplus(gates + dtb_ref[...])
    sco_ref[...] = scb_ref[...] * _conv3(scc_ref[...] * sch_ref[...], csc_ref, seq_len)
    osum[...] = jnp.zeros(osum.shape, F32)

    lane = lax.broadcasted_iota(jnp.int32, (ch, 128), 1)

    def pick(ref, rows, col):
        return jnp.sum(jnp.where(lane == col, ref[rows, :], 0.0), axis=1, keepdims=True)

    def solve_group(gi, carry):
        items = []
        for j in range(GDN_GROUP):
            c = gi * GDN_GROUP + j
            rows = pl.ds(pl.multiple_of(c * ch, ch), ch)
            items.append(dict(c=c, rows=rows, q=qs[rows, :], k=ks[rows, :], v=vs[rows, :]))
        kks = [_mm_nt(it["k"], it["k"], "bf") for it in items]
        qks = [_mm_nt(it["q"], it["k"], "bf") for it in items]
        subs = []
        for it, kk, qk in zip(items, kks, qks):
            for d in range(2):
                beta = pick(betas, it["rows"], d * GDN_HEADS + head)
                g = pick(gs, it["rows"], 2 * GDN_HEADS + d * GDN_HEADS + head)
                decay, e_gc, e_rest, e_tot = _gdn_decay_terms(g, rev=(d == 1))
                strict = _tri_masks(ch, d == 1)[1]
                subs.append(dict(
                    d=d, c=it["c"], rows=it["rows"],
                    a=jnp.where(strict, kk * beta * decay, 0.0),
                    rhs=jnp.concatenate([it["v"] * beta, it["k"] * (beta * e_gc)], axis=1),
                    intra=qk * decay, qd=it["q"] * e_gc, kd=it["k"] * e_rest, ge=e_tot))
        ts = _unit_tri_inv_many([s["a"] for s in subs], "bf")
        uws = [_mm(t, s["rhs"], "bf") for t, s in zip(ts, subs)]
        for s, uw in zip(subs, uws):
            d, rows = s["d"], s["rows"]
            u_s[d, rows, :] = uw[:, :GDN_D]
            w_s[d, rows, :] = uw[:, GDN_D:].astype(BF16)
            qd_s[d, rows, :] = s["qd"].astype(BF16)
            kd_s[d, rows, :] = s["kd"].astype(BF16)
            in_s[d, rows, :] = s["intra"].astype(BF16)
            ge_s[d, pl.ds(pl.multiple_of(s["c"] * 8, 8), 8), :] = jnp.broadcast_to(s["ge"], (8, 128))
        return carry

    lax.fori_loop(0, seqs * n_chunks // GDN_GROUP, solve_group, 0)

    for j in range(seqs):
        st[0, j] = s0f_ref[j] if has_init else jnp.zeros((GDN_D, GDN_D), F32)
        st[1, j] = s0b_ref[j] if has_init else jnp.zeros((GDN_D, GDN_D), F32)

    def recur(i, carry):
        cs = []
        for j in range(seqs):
            for d in range(2):
                c = j * n_chunks + (i if d == 0 else n_chunks - 1 - i)
                cs.append(dict(d=d, j=j, rows=pl.ds(pl.multiple_of(c * ch, ch), ch),
                               ge=ge_s[d, pl.ds(pl.multiple_of(c * 8, 8), 1), :], s=st[d, j]))
        sbs = [c["s"].astype(BF16) for c in cs]
        wss = [_mm(w_s[c["d"], c["rows"], :], sb, "bf") for c, sb in zip(cs, sbs)]
        qss = [_mm(qd_s[c["d"], c["rows"], :], sb, "bf") for c, sb in zip(cs, sbs)]
        ebs = [(u_s[c["d"], c["rows"], :] - ws).astype(BF16) for c, ws in zip(cs, wss)]
        outs = [qs_ + _mm(in_s[c["d"], c["rows"], :], eb, "bf") for c, qs_, eb in zip(cs, qss, ebs)]
        s_news = [c["s"] * c["ge"] + _mm_tn(kd_s[c["d"], c["rows"], :], eb, "bf") for c, eb in zip(cs, ebs)]
        for c, o, s_new in zip(cs, outs, s_news):
            st[c["d"], c["j"]] = s_new
            osum[c["rows"], :] += o
        return carry

    lax.fori_loop(0, n_chunks, recur, 0)
    sf_ref[...] = st[0]
    sb_ref[...] = st[1]
    o_ref[...] = _rms(osum[...], gn_ref[...]) * _silu(z_ref[...])


def _gdn_mixer(proj, n_seq, seq_len, conv_w, sc_conv_w, gdn_norm, alog_vec, dtb_vec, s0_f, s0_b):
    has_init = s0_f is not None
    hd = GDN_HEADS
    seqs = max(1, GDN_BLOCK_ROWS // seq_len)
    rows = seqs * seq_len
    assert n_seq % seqs == 0 and (rows // GDN_CHUNK) % GDN_GROUP == 0

    def col(block):
        return pl.BlockSpec((rows, 128), lambda s, h, b=block: (s, b * hd + h))

    def wcol(block):
        return pl.BlockSpec((3, 128), lambda s, h, b=block: (0, b * hd + h))

    vec = pl.BlockSpec((1, 128), lambda s, h: (0, 0))
    state = pl.BlockSpec((seqs, None, None, GDN_D, GDN_D), lambda s, h: (s, 0, h, 0, 0))
    in_specs = [col(0), col(1), col(2), col(3), col(4), col(5), col(6),
                pl.BlockSpec((rows, 128), lambda s, h: (s, 7 * hd)),
                wcol(0), wcol(1), wcol(2), wcol(0), vec, vec, vec]
    args = [proj] * 8 + [conv_w, conv_w, conv_w, sc_conv_w, gdn_norm.reshape(1, 128), alog_vec, dtb_vec]
    if has_init:
        in_specs += [state, state]
        args += [s0_f, s0_b]
    t = n_seq * seq_len
    out_tok = pl.BlockSpec((rows, 128), lambda s, h: (s, h))
    scratch = ([pltpu.VMEM((rows, 128), F32) for _ in range(6)]
               + [pltpu.VMEM((2, rows, GDN_D), F32)]
               + [pltpu.VMEM((2, rows, GDN_D), BF16) for _ in range(3)]
               + [pltpu.VMEM((2, rows, GDN_CHUNK), BF16),
                  pltpu.VMEM((2, rows // GDN_CHUNK * 8, 128), F32),
                  pltpu.VMEM((2, seqs, GDN_D, GDN_D), F32)])
    return pl.pallas_call(
        functools.partial(_gdn_kernel, seq_len=seq_len, seqs=seqs, has_init=has_init),
        grid=(n_seq // seqs, hd),
        in_specs=in_specs,
        out_specs=[out_tok, out_tok, state, state],
        out_shape=[jax.ShapeDtypeStruct((t, hd * GDN_D), F32), jax.ShapeDtypeStruct((t, SC_WIDTH), F32),
                   jax.ShapeDtypeStruct((n_seq, 1, hd, GDN_D, GDN_D), F32),
                   jax.ShapeDtypeStruct((n_seq, 1, hd, GDN_D, GDN_D), F32)],
        scratch_shapes=scratch,
        compiler_params=pltpu.CompilerParams(dimension_semantics=("parallel", "parallel"),
                                             vmem_limit_bytes=VMEM_LIMIT),
        name="gdn_mixer",
    )(*args)


def _softmax_pv(scores, values, sink):
    m = sink
    for s in scores:
        m = jnp.maximum(m, jnp.max(s, axis=-1, keepdims=True))
    den = jnp.exp(sink - m)
    acc = None
    for s, v in zip(scores, values):
        e = jnp.exp(s - m)
        den = den + jnp.sum(e, axis=-1, keepdims=True)
        pv = _mm(e, v, prec="bf")
        acc = pv if acc is None else acc + pv
    return acc / den


def _group_sinks(sink_ref, j, rows):
    assert rows & (rows - 1) == 0
    grp = lax.broadcasted_iota(jnp.int32, (ATT_GROUP * rows, 1), 0) >> (rows.bit_length() - 1)
    col = jnp.full(grp.shape, sink_ref[j * ATT_GROUP], F32)
    for gi in range(1, ATT_GROUP):
        col = jnp.where(grp == gi, sink_ref[j * ATT_GROUP + gi], col)
    return col


def _store_group(o_ref, j, o, rows):
    for gi in range(ATT_GROUP):
        hh = j * ATT_GROUP + gi
        o_ref[:, hh * ATT_HD:(hh + 1) * ATT_HD] = o[gi * rows:(gi + 1) * rows]


def _attn_ctx_kernel(sink_ref, p_ref, o_ref, kc_ref, vc_ref):
    scale = ATT_HD ** -0.5
    rows = p_ref.shape[0]
    for j in range(ATT_KV_HEADS):
        k = p_ref[:, ATT_Q_W + j * ATT_HD:ATT_Q_W + (j + 1) * ATT_HD]
        v = p_ref[:, ATT_Q_W + ATT_KV_W + j * ATT_HD:ATT_Q_W + ATT_KV_W + (j + 1) * ATT_HD]
        kc_ref[j] = k
        vc_ref[j] = v
        q = jnp.concatenate([p_ref[:, hh * ATT_HD:(hh + 1) * ATT_HD]
                             for hh in range(j * ATT_GROUP, (j + 1) * ATT_GROUP)], axis=0)
        s = _mm_nt(q, k, prec="bf") * scale
        _store_group(o_ref, j, _softmax_pv([s], [v], _group_sinks(sink_ref, j, rows)), rows)


def _attn_context(proj_att, sink, n_seq, seq_len):
    kv = pl.BlockSpec((None, None, ATT_KV_HEADS, seq_len, ATT_HD), lambda b: (b, 0, 0, 0, 0))
    return pl.pallas_call(
        _attn_ctx_kernel,
        grid=(n_seq,),
        in_specs=[pl.BlockSpec(memory_space=pltpu.SMEM),
                  pl.BlockSpec((seq_len, ATT_W), lambda b: (b, 0))],
        out_specs=[pl.BlockSpec((seq_len, ATT_Q_W), lambda b: (b, 0)), kv, kv],
        out_shape=[jax.ShapeDtypeStruct((n_seq * seq_len, ATT_Q_W), F32),
                   jax.ShapeDtypeStruct((n_seq, 1, ATT_KV_HEADS, seq_len, ATT_HD), F32),
                   jax.ShapeDtypeStruct((n_seq, 1, ATT_KV_HEADS, seq_len, ATT_HD), F32)],
        compiler_params=pltpu.CompilerParams(dimension_semantics=("parallel",)),
        name="attn_context",
    )(sink, proj_att)


def _rope_tables(seq_len):
    pos = np.arange(seq_len)
    half = ATT_HD // 2
    inv = ROPE_BASE ** (-np.arange(0, half, 2, dtype=np.float32) / half)
    ang_r = (pos // GRID_W).astype(np.float32)[:, None] * inv
    ang_c = (pos % GRID_W).astype(np.float32)[:, None] * inv
    cos = np.concatenate([np.cos(ang_r), np.cos(ang_r), np.cos(ang_c), np.cos(ang_c)], axis=1)
    sin = np.concatenate([-np.sin(ang_r), np.sin(ang_r), -np.sin(ang_c), np.sin(ang_c)], axis=1)
    return (jnp.asarray(np.tile(cos, (1, 2)), F32), jnp.asarray(np.tile(sin, (1, 2)), F32))


def _rope(x, cos, sin):
    lane = lax.broadcasted_iota(jnp.int32, x.shape, 1)
    partner = jnp.where((lane & 31) < 16, pltpu.roll(x, 128 - 16, 1), pltpu.roll(x, 16, 1))
    return x * cos + partner * sin


def _attn_lat_kernel(sink_ref, p_ref, ck_ref, cv_ref, cos_ref, sin_ref, o_ref, *, seq_len):
    scale = ATT_HD ** -0.5
    qb = pl.program_id(1)
    span = 3 * ATT_BLOCK
    q0 = pl.multiple_of(qb * ATT_BLOCK, ATT_BLOCK)
    k0 = pl.multiple_of(jnp.clip((qb - 1) * ATT_BLOCK, 0, seq_len - span), ATT_BLOCK)
    qrows = pl.ds(q0, ATT_BLOCK)
    krows = pl.ds(k0, span)
    kwin = _rope(p_ref[krows, ATT_Q_W:ATT_Q_W + ATT_KV_W], cos_ref[krows, :], sin_ref[krows, :])
    vwin = p_ref[krows, ATT_Q_W + ATT_KV_W:ATT_W]
    stacked = ATT_GROUP * ATT_BLOCK
    qpos = q0 + (lax.broadcasted_iota(jnp.int32, (stacked, span), 0) & (ATT_BLOCK - 1))
    kpos = k0 + lax.broadcasted_iota(jnp.int32, (stacked, span), 1)
    valid = jnp.abs(qpos - kpos) <= WINDOW
    cos_q = cos_ref[qrows, :]
    sin_q = sin_ref[qrows, :]
    heads = []
    for pair in range(ATT_HEADS // 2):
        qpair = _rope(p_ref[qrows, pair * 128:(pair + 1) * 128], cos_q, sin_q)
        heads += [qpair[:, :ATT_HD], qpair[:, ATT_HD:]]
    kv = range(ATT_KV_HEADS)
    qs = [jnp.concatenate(heads[j * ATT_GROUP:(j + 1) * ATT_GROUP], axis=0) for j in kv]
    s_locs = [jnp.where(valid, _mm_nt(qs[j], kwin[:, j * ATT_HD:(j + 1) * ATT_HD], prec="bf") * scale, NEG_INF)
              for j in kv]
    s_ctxs = [_mm_nt(qs[j], ck_ref[j], prec="bf") * scale for j in kv]
    outs = [_softmax_pv([s_locs[j], s_ctxs[j]], [vwin[:, j * ATT_HD:(j + 1) * ATT_HD], cv_ref[j]],
                        _group_sinks(sink_ref, j, ATT_BLOCK)) for j in kv]
    for j in kv:
        _store_group(o_ref, j, outs[j], ATT_BLOCK)


def _attn_latent(proj_att, sink, cache_k, cache_v, layer, n_seq, seq_len):
    cos, sin = _rope_tables(seq_len)
    past = cache_k.shape[3]
    nqb = seq_len // ATT_BLOCK
    cache = pl.BlockSpec((None, None, ATT_KV_HEADS, past, ATT_HD), lambda b, q: (b, layer, 0, 0, 0))
    table = pl.BlockSpec((seq_len, 128), lambda b, q: (0, 0))
    return pl.pallas_call(
        functools.partial(_attn_lat_kernel, seq_len=seq_len),
        grid=(n_seq, nqb),
        in_specs=[pl.BlockSpec(memory_space=pltpu.SMEM),
                  pl.BlockSpec((seq_len, ATT_W), lambda b, q: (b, 0)),
                  cache, cache, table, table],
        out_specs=pl.BlockSpec((ATT_BLOCK, ATT_Q_W), lambda b, q: (b * nqb + q, 0)),
        out_shape=jax.ShapeDtypeStruct((n_seq * seq_len, ATT_Q_W), F32),
        compiler_params=pltpu.CompilerParams(dimension_semantics=("parallel", "parallel")),
        name="attn_latent",
    )(sink, proj_att, cache_k, cache_v, cos, sin)


def _pair_masks(n, rev):
    r = lax.broadcasted_iota(jnp.int32, (n, 2 * n), 0)
    c = lax.broadcasted_iota(jnp.int32, (n, 2 * n), 1) & (n - 1)
    return (r <= c, r < c) if rev else (r >= c, r > c)


def _bd(x):
    half = x.shape[1] // 2
    lane = lax.broadcasted_iota(jnp.int32, x.shape, 1)
    zero = jnp.zeros_like(x)
    return jnp.concatenate([jnp.where(lane < half, x, zero), jnp.where(lane >= half, x, zero)], axis=0)


def _bd_mask(n):
    r = lax.broadcasted_iota(jnp.int32, (2 * n, 2 * n), 0)
    c = lax.broadcasted_iota(jnp.int32, (2 * n, 2 * n), 1)
    return (r < n) == (c < n)


def _head_sums(x):
    ones = jnp.where(_bd_mask(RWKV_HD), 1.0, 0.0).astype(BF16)
    return _mm(x, ones, "x2l")


def _apply_pairs(tb, x):
    hi, lo = _split_bf16(x)
    return _mm(tb, _bd(hi), "bf") + _mm(tb, _bd(lo), "bf")


def _unit_tri_inv_pairs(mats):
    n = mats[0].shape[0]
    r = lax.broadcasted_iota(jnp.int32, (n, 2 * n), 0)
    c = lax.broadcasted_iota(jnp.int32, (n, 2 * n), 1) & (n - 1)
    eye = jnp.where(r == c, 1.0, 0.0)
    ts = [eye - a * jnp.where((r >> 1) == (c >> 1), 1.0, 0.0) for a in mats]
    abs_ = [a.astype(BF16) for a in mats]
    shift = 1
    while (1 << shift) < n:
        join = jnp.where(((r >> (shift + 1)) == (c >> (shift + 1))) & ((r >> shift) != (c >> shift)),
                         1.0, 0.0).astype(BF16)
        tbs = [t.astype(BF16) for t in ts]
        inner = [_mm(ab * join, _bd(tb), "bf") for ab, tb in zip(abs_, tbs)]
        ts = [t - _mm(tb, _bd(w.astype(BF16)), "bf") for t, tb, w in zip(ts, tbs, inner)]
        shift += 1
    return ts


def _rwkv_chunk_operands(x_ref, prm, sc, c, *, seq_len):
    (mu_ref, w0_ref, wup_ref, a0_ref, aup_ref, gup_ref, kk_ref, ka_ref, rk_ref) = prm
    ch = RWKV_CHUNK
    total = x_ref.shape[0]
    r0 = pl.multiple_of(c * ch, ch)
    rows = pl.ds(r0, ch)
    pos0 = r0 & (seq_len - 1)
    x = x_ref[rows, :]
    prev_row = x_ref[pl.ds(jnp.maximum(r0 - 1, 0), 1), :] * jnp.where(pos0 > 0, 1.0, 0.0)
    next_row = x_ref[pl.ds(jnp.minimum(r0 + ch, total - 1), 1), :] * jnp.where(pos0 + ch < seq_len, 1.0, 0.0)
    xp, xn = _shifted_rows(x, prev_row, next_row)
    xs = x + mu_ref[0:1, :] * (xp - x) + mu_ref[1:2, :] * (xn - x)
    r = xs[:, 0:RWKV_W]
    k = xs[:, RWKV_W:2 * RWKV_W]
    v = xs[:, 2 * RWKV_W:3 * RWKV_W]
    lo = 3 * RWKV_W
    p = RWKV_PREC
    gl = xs[:, lo + 4 * RWKV_LORA:lo + 6 * RWKV_LORA]
    sc["gate"][rows, :] = _mm(jax.nn.sigmoid(gl), gup_ref[...], p["gate"])
    sc["v"][rows, :] = v.astype(BF16)
    pairs = [slice(i * PAIR_W, (i + 1) * PAIR_W) for i in range(N_PAIRS)]
    kkv = k * kk_ref[...]
    kaps = []
    for cols in pairs:
        kk_p = kkv[:, cols]
        kaps.append(kk_p * lax.rsqrt(_head_sums(kk_p * kk_p) + 1e-6))
    items = []
    bonus = None
    for d in range(2):
        rev = d == 1
        wl = xs[:, lo + d * RWKV_LORA:lo + (d + 1) * RWKV_LORA]
        al = xs[:, lo + 2 * RWKV_LORA + d * RWKV_LORA:lo + 2 * RWKV_LORA + (d + 1) * RWKV_LORA]
        lw = -DECAY_SCALE * jax.nn.sigmoid(w0_ref[d:d + 1, :] + _mm(jnp.tanh(wl), wup_ref[d], p["lora"]))
        a = jax.nn.sigmoid(a0_ref[d:d + 1, :] + _mm(al, aup_ref[d], p["lora"]))
        k2 = k * (1.0 + (a - 1.0) * ka_ref[...])
        g_inc = _mm(jnp.where(_tri_masks(ch, rev)[0], 1.0, 0.0), lw, p["cumsum"])
        g_tot = jnp.sum(lw, axis=0, keepdims=True)
        e_neg = jnp.exp(-g_inc)
        e_end = jnp.exp(g_tot - g_inc)
        e_exc = jnp.exp(g_inc - lw)
        r_dec = r * jnp.exp(g_inc)
        k_neg = k2 * e_neg
        k_end = k2 * e_end
        sc["dec"][d, pl.ds(pl.multiple_of(c * 8, 8), 8), :] = jnp.broadcast_to(jnp.exp(g_tot), (8, RWKV_W))
        rkr = r * k2 * rk_ref[...]
        bon_d = jnp.concatenate([_head_sums(rkr[:, cols]) for cols in pairs], axis=1) * v
        bonus = bon_d if bonus is None else bonus + bon_d
        for cols, kap in zip(pairs, kaps):
            b_p = kap * a[:, cols]
            items.append(dict(d=d, rows=rows, cols=cols, kap_dec=kap * e_exc[:, cols], r_dec=r_dec[:, cols],
                              b_neg=b_p * e_neg[:, cols], k_neg=k_neg[:, cols],
                              b_end=b_p * e_end[:, cols], k_end=k_end[:, cols], vb=v[:, cols].astype(BF16)))
    sc["bon"][rows, :] = bonus
    return items


def _rwkv_solve(items, sc):
    ch = RWKV_CHUNK
    masks = [_pair_masks(ch, False), _pair_masks(ch, True)]
    ms = [_mm_nt(jnp.concatenate([it["kap_dec"], it["r_dec"]], axis=0),
                 jnp.concatenate([_bd(it["b_neg"].astype(BF16)), _bd(it["k_neg"].astype(BF16))], axis=0), "bf")
          for it in items]
    a_abs = [jnp.where(masks[it["d"]][1], m[:ch, :2 * ch], 0.0) for it, m in zip(items, ms)]
    a_aks = [jnp.where(masks[it["d"]][1], m[:ch, 2 * ch:], 0.0) for it, m in zip(items, ms)]
    a_rbs = [jnp.where(masks[it["d"]][0], m[ch:, :2 * ch], 0.0) for it, m in zip(items, ms)]
    a_rks = [jnp.where(masks[it["d"]][0], m[ch:, 2 * ch:], 0.0) for it, m in zip(items, ms)]
    tbs = [t.astype(BF16) for t in _unit_tri_inv_pairs(a_abs)]
    akvs = [_mm(a_ak, _bd(it["vb"]), "bf") for it, a_ak in zip(items, a_aks)]
    w2s = [_apply_pairs(tb, it["kap_dec"]) for it, tb in zip(items, tbs)]
    u0s = [_apply_pairs(tb, akv) for tb, akv in zip(tbs, akvs)]
    for it, w2, u0, a_rb, a_rk in zip(items, w2s, u0s, a_rbs, a_rks):
        d, rows, cols = it["d"], it["rows"], it["cols"]
        sc["w2"][d, rows, cols] = w2.astype(BF16)
        sc["rd"][d, rows, cols] = it["r_dec"].astype(BF16)
        sc["u0"][d, rows, cols] = u0
        sc["arb"][d, rows, cols] = a_rb.astype(BF16)
        sc["ark"][d, rows, cols] = a_rk.astype(BF16)
        sc["bh"][d, rows, cols] = it["b_end"].astype(BF16)
        sc["kh"][d, rows, cols] = it["k_end"].astype(BF16)


def _rwkv_recur(sc, st, i, *, seq_len, seqs):
    ch = RWKV_CHUNK
    n_chunks = seq_len // ch
    keep = _bd_mask(RWKV_HD)
    cs = []
    for j in range(seqs):
        for d in range(2):
            c = j * n_chunks + (i if d == 0 else n_chunks - 1 - i)
            rows = pl.ds(pl.multiple_of(c * ch, ch), ch)
            dec = sc["dec"][d, pl.ds(pl.multiple_of(c * 8, 8), 1), :]
            for pi in range(N_PAIRS):
                cols = slice(pi * PAIR_W, (pi + 1) * PAIR_W)
                cs.append(dict(d=d, j=j, p=pi, rows=rows, cols=cols, dec=dec[:, cols], s=st[d, j, pi]))
    sbs = [c["s"].astype(BF16) for c in cs]
    lss = [_mm_nt(jnp.concatenate([sc["w2"][c["d"], c["rows"], c["cols"]],
                                   sc["rd"][c["d"], c["rows"], c["cols"]]], axis=0), sb, "bf")
           for c, sb in zip(cs, sbs)]
    ubs = [(-(ls[:ch] + sc["u0"][c["d"], c["rows"], c["cols"]])).astype(BF16) for c, ls in zip(cs, lss)]
    vbs = [sc["v"][c["rows"], c["cols"]] for c in cs]
    ys = [ls[ch:] + _mm(jnp.concatenate([sc["arb"][c["d"], c["rows"], c["cols"]],
                                         sc["ark"][c["d"], c["rows"], c["cols"]]], axis=1),
                        jnp.concatenate([_bd(ub), _bd(vb)], axis=0), "bf")
          for c, ls, ub, vb in zip(cs, lss, ubs, vbs)]
    ups = [_mm_tn(jnp.concatenate([ub, vb], axis=0),
                  jnp.concatenate([sc["bh"][c["d"], c["rows"], c["cols"]],
                                   sc["kh"][c["d"], c["rows"], c["cols"]]], axis=0), "bf")
           for c, ub, vb in zip(cs, ubs, vbs)]
    for c, y, up in zip(cs, ys, ups):
        st[c["d"], c["j"], c["p"]] = c["s"] * c["dec"] + jnp.where(keep, up, 0.0)
        sc["ysum"][c["rows"], c["cols"]] += y


RWKV_SCRATCH = ("ysum", "bon", "gate", "v", "w2", "rd", "u0", "arb", "ark", "bh", "kh", "dec")


def _rwkv_kernel(*refs, seq_len, seqs, has_init):
    x_ref = refs[0]
    prm = refs[1:10]
    lnw_ref, lnb_ref = refs[10:12]
    pos = 12
    if has_init:
        s0f_ref, s0b_ref = refs[pos:pos + 2]
        pos += 2
    o_ref, sf_ref, sb_ref = refs[pos:pos + 3]
    sc = dict(zip(RWKV_SCRATCH, refs[pos + 3:]))
    st = refs[pos + 3 + len(RWKV_SCRATCH)]
    ch = RWKV_CHUNK
    hd = RWKV_HD
    n_chunks = seq_len // ch
    sc["ysum"][...] = jnp.zeros(sc["ysum"].shape, F32)

    def prepare(gi, carry):
        items = []
        for j in range(RWKV_GROUP):
            items += _rwkv_chunk_operands(x_ref, prm, sc, gi * RWKV_GROUP + j, seq_len=seq_len)
        _rwkv_solve(items, sc)
        return carry

    lax.fori_loop(0, seqs * n_chunks // RWKV_GROUP, prepare, 0)

    zero = jnp.zeros((hd, hd), F32)
    for d, s0_ref in enumerate((s0f_ref, s0b_ref) if has_init else (None, None)):
        for j in range(seqs):
            for pi in range(N_PAIRS):
                s_a = s0_ref[j, 2 * pi] if has_init else zero
                s_b = s0_ref[j, 2 * pi + 1] if has_init else zero
                st[d, j, pi] = jnp.concatenate([jnp.concatenate([s_a, zero], axis=1),
                                                jnp.concatenate([zero, s_b], axis=1)], axis=0)

    def recur(i, carry):
        _rwkv_recur(sc, st, i, seq_len=seq_len, seqs=seqs)
        return carry

    lax.fori_loop(0, n_chunks, recur, 0)
    for d, out_ref in enumerate((sf_ref, sb_ref)):
        for j in range(seqs):
            for pi in range(N_PAIRS):
                s = st[d, j, pi]
                out_ref[j, 2 * pi] = s[:hd, :hd]
                out_ref[j, 2 * pi + 1] = s[hd:, hd:]

    tile = RWKV_FINISH_ROWS

    def finish(i, carry):
        rows = pl.ds(pl.multiple_of(i * tile, tile), tile)
        pairs = [slice(pi * PAIR_W, (pi + 1) * PAIR_W) for pi in range(N_PAIRS)]
        ys = [sc["ysum"][rows, cols] for cols in pairs]
        cens = [y - _head_sums(y) * (1.0 / hd) for y in ys]
        vars_ = [_head_sums(cen * cen) * (1.0 / hd) for cen in cens]
        for cols, cen, var in zip(pairs, cens, vars_):
            yn = cen * lax.rsqrt(var + GN_EPS) * lnw_ref[:, cols] + lnb_ref[:, cols]
            o_ref[rows, cols] = (yn + sc["bon"][rows, cols]) * sc["gate"][rows, cols]
        return carry

    lax.fori_loop(0, seqs * seq_len // tile, finish, 0)


def _rwkv_mixer(x_rw, n_seq, seq_len, params, s0_f, s0_b):
    has_init = s0_f is not None
    (mu, w0, w_up, a0, a_up, g_up, k_k, k_a, r_k, ln_w, ln_b) = params
    row = lambda a: a.reshape(1, RWKV_W)
    args = [x_rw, mu, w0, w_up, a0, a_up, g_up, row(k_k), row(k_a), row(r_k), row(ln_w), row(ln_b)]

    def whole(a):
        nd = a.ndim
        return pl.BlockSpec(a.shape, lambda s, nd=nd: (0,) * nd)

    seqs = max(1, RWKV_BLOCK_ROWS // seq_len)
    rows = seqs * seq_len
    assert n_seq % seqs == 0 and (rows // RWKV_CHUNK) % RWKV_GROUP == 0 and seq_len & (seq_len - 1) == 0
    in_specs = [pl.BlockSpec((rows, RWKV_IN), lambda s: (s, 0))] + [whole(a) for a in args[1:]]
    state = pl.BlockSpec((seqs, None, RWKV_HEADS, RWKV_HD, RWKV_HD), lambda s: (s, 0, 0, 0, 0))
    if has_init:
        in_specs += [state, state]
        args += [s0_f, s0_b]
    st_shape = jax.ShapeDtypeStruct((n_seq, 1, RWKV_HEADS, RWKV_HD, RWKV_HD), F32)
    tok = lambda dt: pltpu.VMEM((rows, RWKV_W), dt)
    per_dir = lambda dt: pltpu.VMEM((2, rows, RWKV_W), dt)
    scratch = dict(ysum=tok(F32), bon=tok(F32), gate=tok(F32), v=tok(BF16), w2=per_dir(BF16), rd=per_dir(BF16),
                   u0=per_dir(F32), arb=per_dir(BF16), ark=per_dir(BF16), bh=per_dir(BF16), kh=per_dir(BF16),
                   dec=pltpu.VMEM((2, rows // RWKV_CHUNK * 8, RWKV_W), F32))
    return pl.pallas_call(
        functools.partial(_rwkv_kernel, seq_len=seq_len, seqs=seqs, has_init=has_init),
        grid=(n_seq // seqs,),
        in_specs=in_specs,
        out_specs=[pl.BlockSpec((rows, RWKV_W), lambda s: (s, 0)), state, state],
        out_shape=[jax.ShapeDtypeStruct((n_seq * seq_len, RWKV_W), F32), st_shape, st_shape],
        scratch_shapes=[scratch[name] for name in RWKV_SCRATCH]
        + [pltpu.VMEM((2, seqs, N_PAIRS, PAIR_W, PAIR_W), F32)],
        compiler_params=pltpu.CompilerParams(dimension_semantics=("parallel",), vmem_limit_bytes=VMEM_LIMIT),
        name="rwkv_mixer",
    )(*args)


def kernel(x_prompt, x_sample, state_gdn_fwd, state_gdn_bwd, cache_attn_k, cache_attn_v, state_rwkv_fwd, state_rwkv_bwd, c, c_ctx, mod_w, mod_b, norm_mix, norm_mlp, mlp_w1, mlp_w2, norm_final, ev_w_in, ev_w_out, gdn_conv, gdn_a_log, gdn_dt_bias, gdn_norm, sc_conv, od_w_in, od_w_out, attn_sink, rwkv_mu, rwkv_w0, rwkv_w_up, rwkv_a0, rwkv_a_up, rwkv_g_up, rwkv_k_k, rwkv_k_a, rwkv_r_k, rwkv_ln_w, rwkv_ln_b):
    bp, lp, _ = x_prompt.shape
    bs, ls, _ = x_sample.shape
    depth = mod_w.shape[0]
    c_rows = jnp.concatenate([c_ctx[None, :], c, jnp.zeros((MOD_ROWS - 1 - bs, D_MODEL), F32)], axis=0)
    mods = _modulation(c_rows, mod_w, mod_b)

    assert ls % TOKEN_TILE == 0 and (bp * lp) % TOKEN_TILE == 0
    groups = [
        dict(x=x_prompt.reshape(bp * lp, D_MODEL), n=bp, l=lp, latent=False,
             mod=lambda layer: _mod_spec(layer, 1, 0, 0)),
        dict(x=x_sample.reshape(bs * ls, D_MODEL), n=bs, l=ls, latent=True,
             mod=lambda layer: _mod_spec(layer, ls // TOKEN_TILE, 1, 1)),
    ]
    outs = {}
    for layer in range(depth):
        w1 = mlp_w1[layer].astype(BF16)
        w2 = mlp_w2[layer].astype(BF16)
        final = layer == depth - 1
        if layer % 2 == 0:
            e = layer // 2
            w = ev_w_in[e]
            qkvz = GDN_QKV_W + GDN_HEADS * GDN_D
            n_gate = 4 * GDN_HEADS
            w_in = jnp.concatenate(
                [w[:, :qkvz], w[:, qkvz + n_gate:], w[:, qkvz:qkvz + n_gate],
                 jnp.zeros((D_MODEL, EV_IN_PAD - w.shape[1]), F32)], axis=1).astype(BF16)
            w_out = ev_w_out[e].astype(BF16)
            alog_vec = jnp.zeros((1, 128), F32).at[0, 2 * GDN_HEADS:4 * GDN_HEADS].set(gdn_a_log[e].reshape(-1))
            dtb_vec = jnp.zeros((1, 128), F32).at[0, 2 * GDN_HEADS:4 * GDN_HEADS].set(gdn_dt_bias[e].reshape(-1))
            for grp in groups:
                (proj,) = _inproj(grp["x"], mods, norm_mix[layer], w_in, (EV_IN_PAD,), grp["mod"](layer))
                s0 = (state_gdn_fwd[:, e:e + 1], state_gdn_bwd[:, e:e + 1]) if grp["latent"] else (None, None)
                o, sc, s_f, s_b = _gdn_mixer(proj, grp["n"], grp["l"], gdn_conv[e], sc_conv[e], gdn_norm[e],
                                             alog_vec, dtb_vec, *s0)
                if not grp["latent"]:
                    outs.setdefault("gdn_f", []).append(s_f)
                    outs.setdefault("gdn_b", []).append(s_b)
                grp["x"] = _outproj_mlp(o, sc, grp["x"], mods, norm_mlp[layer], w_out, w1, w2, norm_final,
                                        grp["mod"](layer), final)
        else:
            o_ = layer // 2
            w = od_w_in[o_]
            w_in = w.astype(BF16)
            w_out = od_w_out[o_].astype(BF16)
            rw = (rwkv_mu[o_], rwkv_w0[o_], rwkv_w_up[o_], rwkv_a0[o_], rwkv_a_up[o_], rwkv_g_up[o_],
                  rwkv_k_k[o_], rwkv_k_a[o_], rwkv_r_k[o_].reshape(-1), rwkv_ln_w[o_], rwkv_ln_b[o_])
            for grp in groups:
                p_att, x_rw = _inproj(grp["x"], mods, norm_mix[layer], w_in, (ATT_W, RWKV_IN), grp["mod"](layer))
                if grp["latent"]:
                    att = _attn_latent(p_att, attn_sink[o_], cache_attn_k, cache_attn_v, o_, grp["n"], grp["l"])
                    rwo, _, _ = _rwkv_mixer(x_rw, grp["n"], grp["l"], rw,
                                            state_rwkv_fwd[:, o_:o_ + 1], state_rwkv_bwd[:, o_:o_ + 1])
                else:
                    att, kc, vc = _attn_context(p_att, attn_sink[o_], grp["n"], grp["l"])
                    rwo, s_f, s_b = _rwkv_mixer(x_rw, grp["n"], grp["l"], rw, None, None)
                    outs.setdefault("att_k", []).append(kc)
                    outs.setdefault("att_v", []).append(vc)
                    outs.setdefault("rw_f", []).append(s_f)
                    outs.setdefault("rw_b", []).append(s_b)
                grp["x"] = _outproj_mlp(att, rwo, grp["x"], mods, norm_mlp[layer], w_out, w1, w2, norm_final,
                                        grp["mod"](layer), final)
    cat = lambda key: jnp.concatenate(outs[key], axis=1)
    return (groups[0]["x"].reshape(bp, lp, D_MODEL), groups[1]["x"].reshape(bs, ls, D_MODEL),
            cat("gdn_f"), cat("gdn_b"), cat("att_k"), cat("att_v"), cat("rw_f"), cat("rw_b"))
```

```python
import functools

import jax
import jax.numpy as jnp
import numpy as np
from jax import lax
from jax.experimental import pallas as pl
from jax.experimental.pallas import tpu as pltpu

F32 = jnp.float32
BF16 = jnp.bfloat16
HIGHEST = lax.Precision.HIGHEST

D_MODEL = 1024
N_MOD = 6
D_FF = 4 * D_MODEL
NORM_EPS = 1e-6
TOKEN_TILE = 512
MOD_ROWS = 8

GDN_HEADS = 4
GDN_D = 128
GDN_CHUNK = 128
GDN_GROUP = 8
GDN_BLOCK_ROWS = 1024
GDN_QKV_W = 3 * GDN_HEADS * GDN_D
SC_WIDTH = 512
EV_IN_PAD = 3712

ATT_HEADS = 8
ATT_KV_HEADS = 2
ATT_GROUP = ATT_HEADS // ATT_KV_HEADS
ATT_HD = 64
ATT_Q_W = ATT_HEADS * ATT_HD
ATT_KV_W = ATT_KV_HEADS * ATT_HD
ATT_W = ATT_Q_W + 2 * ATT_KV_W
WINDOW = 128
ATT_BLOCK = 128
GRID_W = 64
ROPE_BASE = 10000.0
NEG_INF = -1e30

RWKV_HEADS = 8
RWKV_HD = 64
RWKV_W = RWKV_HEADS * RWKV_HD
RWKV_LORA = 64
RWKV_IN = 3 * RWKV_W + 3 * 2 * RWKV_LORA
RWKV_CHUNK = 64
RWKV_GROUP = 4
RWKV_FINISH_ROWS = 256
DECAY_SCALE = float(np.exp(-0.5))
RWKV_BLOCK_ROWS = 1024
PAIR_W = 2 * RWKV_HD
N_PAIRS = RWKV_W // PAIR_W
GN_EPS = 64e-5

VMEM_LIMIT = 56 * 1024 * 1024

RWKV_PREC = dict(lora="bf", gate="bf", cumsum="x2r", amat="bf", inv="bf", state="bf", akv="bf", solve="x2r",
                 out="bf", update="bf")


def _split_bf16(a):
    hi = a.astype(BF16)
    return hi, (a - hi.astype(F32)).astype(BF16)


def _dot(a, b, dims, prec):
    dn = (dims, ((), ()))
    if prec == "hi":
        return lax.dot_general(a, b, dn, precision=HIGHEST, preferred_element_type=F32)
    one = lambda x, y: lax.dot_general(x, y, dn, preferred_element_type=F32)
    if prec == "x3":
        ah, al = _split_bf16(a)
        bh, bl = _split_bf16(b)
        return one(ah, bh) + one(ah, bl) + one(al, bh)
    if prec == "x2l":
        ah, al = _split_bf16(a)
        bh = b.astype(BF16)
        return one(ah, bh) + one(al, bh)
    if prec == "x2r":
        ah = a.astype(BF16)
        bh, bl = _split_bf16(b)
        return one(ah, bh) + one(ah, bl)
    assert prec == "bf", prec
    return lax.dot_general(a.astype(BF16), b.astype(BF16), dn, preferred_element_type=F32)


def _mm(a, b, prec="hi"):
    return _dot(a, b, ((1,), (0,)), prec)


def _mm_nt(a, b, prec="hi"):
    return _dot(a, b, ((1,), (1,)), prec)


def _mm_tn(a, b, prec="hi"):
    return _dot(a, b, ((0,), (0,)), prec)


def _silu(x):
    return x * jax.nn.sigmoid(x)


def _softplus(x):
    return jnp.maximum(x, 0.0) + jnp.log1p(jnp.exp(-jnp.abs(x)))


def _rms(x, w):
    return x * lax.rsqrt(jnp.mean(x * x, axis=-1, keepdims=True) + NORM_EPS) * w


def _tri_masks(n, rev):
    r = lax.broadcasted_iota(jnp.int32, (n, n), 0)
    c = lax.broadcasted_iota(jnp.int32, (n, n), 1)
    if rev:
        return r <= c, r < c
    return r >= c, r > c


def _unit_tri_inv_many(mats, prec):
    assert prec == "bf"
    n = mats[0].shape[0]
    r = lax.broadcasted_iota(jnp.int32, (n, n), 0)
    c = lax.broadcasted_iota(jnp.int32, (n, n), 1)
    eye = jnp.where(r == c, 1.0, 0.0)
    ts = [eye - a * jnp.where((r >> 1) == (c >> 1), 1.0, 0.0) for a in mats]
    abs_ = [a.astype(BF16) for a in mats]
    shift = 1
    while (1 << shift) < n:
        join = jnp.where(((r >> (shift + 1)) == (c >> (shift + 1))) & ((r >> shift) != (c >> shift)),
                         1.0, 0.0).astype(BF16)
        tbs = [t.astype(BF16) for t in ts]
        inner = [_mm(ab * join, tb, prec) for ab, tb in zip(abs_, tbs)]
        ts = [t - _mm(tb, w, prec) for t, tb, w in zip(ts, tbs, inner)]
        shift += 1
    return ts


def _shifted_rows(x, prev_row, next_row):
    n = x.shape[0]
    row = lax.broadcasted_iota(jnp.int32, x.shape, 0)
    xp = jnp.where(row == 0, prev_row, pltpu.roll(x, 1, 0))
    xn = jnp.where(row == n - 1, next_row, pltpu.roll(x, n - 1, 0))
    return xp, xn


def _conv3(x, w_ref, seq_len):
    n = x.shape[0]
    assert seq_len & (seq_len - 1) == 0 and n % seq_len == 0
    pos = lax.broadcasted_iota(jnp.int32, x.shape, 0) & (seq_len - 1)
    xp = jnp.where(pos == 0, 0.0, pltpu.roll(x, 1, 0))
    xn = jnp.where(pos == seq_len - 1, 0.0, pltpu.roll(x, n - 1, 0))
    return xp * w_ref[0:1, :] + x * w_ref[1:2, :] + xn * w_ref[2:3, :]


def _mod_kernel(c_ref, w_ref, b_ref, o_ref):
    s = _silu(c_ref[...])
    o_ref[...] = _mm(s, w_ref[...], prec="bf") + b_ref[...]


def _modulation(c_rows, mod_w, mod_b):
    depth = mod_w.shape[0]
    nblk = (N_MOD * D_MODEL) // D_MODEL
    out = pl.pallas_call(
        _mod_kernel,
        grid=(depth, nblk),
        in_specs=[
            pl.BlockSpec((MOD_ROWS, D_MODEL), lambda l, j: (0, 0)),
            pl.BlockSpec((None, D_MODEL, D_MODEL), lambda l, j: (l, 0, j)),
            pl.BlockSpec((None, 1, D_MODEL), lambda l, j: (l, 0, j)),
        ],
        out_specs=pl.BlockSpec((None, MOD_ROWS, D_MODEL), lambda l, j: (l, 0, j)),
        out_shape=jax.ShapeDtypeStruct((depth, MOD_ROWS, N_MOD * D_MODEL), F32),
        compiler_params=pltpu.CompilerParams(dimension_semantics=("parallel", "parallel")),
        name="modulation",
    )(c_rows, mod_w, mod_b.reshape(depth, 1, N_MOD * D_MODEL))
    return out.reshape(depth, MOD_ROWS, N_MOD, D_MODEL)


def _mod_spec(layer, tiles_per_seq, row_base, row_step):
    return pl.BlockSpec((None, None, N_MOD, D_MODEL),
                        lambda i: (layer, row_base + (i // tiles_per_seq) * row_step, 0, 0))


def _inproj_kernel(*refs, n_w):
    x_ref, mod_ref, nw_ref = refs[:3]
    w_refs = refs[3:3 + n_w]
    o_refs = refs[3 + n_w:]
    h = _rms(x_ref[...], nw_ref[...])
    h = (h * (1.0 + mod_ref[1:2, :]) + mod_ref[0:1, :]).astype(BF16)
    pieces = [_mm(h, w_ref[...], prec="bf") for w_ref in w_refs]
    y = pieces[0] if n_w == 1 else jnp.concatenate(pieces, axis=1)
    off = 0
    for o_ref in o_refs:
        n = o_ref.shape[-1]
        o_ref[...] = y[:, off:off + n]
        off += n


def _inproj(x, mods, norm_w, ws_bf16, splits, mod_spec):
    t = x.shape[0]
    assert sum(w.shape[1] for w in ws_bf16) == sum(splits) and all(w.shape[1] % 128 == 0 for w in ws_bf16)
    return pl.pallas_call(
        functools.partial(_inproj_kernel, n_w=len(ws_bf16)),
        grid=(t // TOKEN_TILE,),
        in_specs=[
            pl.BlockSpec((TOKEN_TILE, D_MODEL), lambda i: (i, 0)),
            mod_spec,
            pl.BlockSpec((1, D_MODEL), lambda i: (0, 0)),
        ] + [pl.BlockSpec(w.shape, lambda i: (0, 0)) for w in ws_bf16],
        out_specs=[pl.BlockSpec((TOKEN_TILE, n), lambda i: (i, 0)) for n in splits],
        out_shape=[jax.ShapeDtypeStruct((t, n), F32) for n in splits],
        compiler_params=pltpu.CompilerParams(dimension_semantics=("parallel",), vmem_limit_bytes=VMEM_LIMIT),
        name="inproj",
    )(x, mods, norm_w.reshape(1, D_MODEL), *ws_bf16)


def _mlp_kernel(a_ref, b_ref, x_ref, mod_ref, nw_ref, woa_ref, wob_ref, w1_ref, w2_ref, nf_ref, o_ref, *, final):
    y = _mm(a_ref[...], woa_ref[...], prec="bf") + _mm(b_ref[...], wob_ref[...], prec="bf")
    x1 = x_ref[...] + mod_ref[2:3, :] * y
    h = _rms(x1, nw_ref[...])
    h = (h * (1.0 + mod_ref[4:5, :]) + mod_ref[3:4, :]).astype(BF16)
    acc = jnp.zeros(x1.shape, F32)
    for j in range(D_FF // D_MODEL):
        cols = slice(j * D_MODEL, (j + 1) * D_MODEL)
        u = jnp.maximum(_mm(h, w1_ref[:, cols], prec="bf"), 0.0)
        acc = acc + _mm(u * u, w2_ref[cols, :], prec="bf")
    x2 = x1 + mod_ref[5:6, :] * acc
    if final:
        x2 = _rms(x2, nf_ref[...])
    o_ref[...] = x2


def _outproj_mlp(a, b, x, mods, norm_w, w_out, w1_all, w2_all, layer, norm_final, mod_spec, final):
    t = x.shape[0]
    half = a.shape[1]
    const = lambda i: (0, 0)
    return pl.pallas_call(
        functools.partial(_mlp_kernel, final=final),
        grid=(t // TOKEN_TILE,),
        in_specs=[
            pl.BlockSpec((TOKEN_TILE, half), lambda i: (i, 0)),
            pl.BlockSpec((TOKEN_TILE, half), lambda i: (i, 0)),
            pl.BlockSpec((TOKEN_TILE, D_MODEL), lambda i: (i, 0)),
            mod_spec,
            pl.BlockSpec((1, D_MODEL), const),
            pl.BlockSpec((half, D_MODEL), const),
            pl.BlockSpec((half, D_MODEL), lambda i: (1, 0)),
            pl.BlockSpec((None, D_MODEL, D_FF), lambda i: (layer, 0, 0)),
            pl.BlockSpec((None, D_FF, D_MODEL), lambda i: (layer, 0, 0)),
            pl.BlockSpec((1, D_MODEL), const),
        ],
        out_specs=pl.BlockSpec((TOKEN_TILE, D_MODEL), lambda i: (i, 0)),
        out_shape=jax.ShapeDtypeStruct((t, D_MODEL), F32),
        compiler_params=pltpu.CompilerParams(dimension_semantics=("parallel",), vmem_limit_bytes=VMEM_LIMIT),
        name="outproj_mlp",
    )(a, b, x, mods, norm_w.reshape(1, D_MODEL), w_out, w_out, w1_all, w2_all, norm_final.reshape(1, D_MODEL))


def _gdn_decay_terms(g, rev):
    c = g.shape[0]
    incl = _tri_masks(c, rev)[0]
    before_col = _tri_masks(c, not rev)[0]
    eye = lax.broadcasted_iota(jnp.int32, (c, c), 0) == lax.broadcasted_iota(jnp.int32, (c, c), 1)
    gc_row = jnp.sum(jnp.where(before_col, jnp.broadcast_to(g, (c, c)), 0.0), axis=0, keepdims=True)
    gc_col = jnp.sum(jnp.where(eye, jnp.broadcast_to(gc_row, (c, c)), 0.0), axis=1, keepdims=True)
    decay = jnp.where(incl, jnp.exp(jnp.where(incl, gc_col - gc_row, 0.0)), 0.0)
    g_tot = jnp.sum(g, axis=0, keepdims=True)
    return decay, jnp.exp(gc_col), jnp.exp(g_tot - gc_col), jnp.exp(g_tot)


def _gdn_kernel(*refs, seq_len, seqs, has_init):
    (q_ref, k_ref, v_ref, z_ref, scb_ref, scc_ref, sch_ref, gate_ref,
     cq_ref, ck_ref, cv_ref, csc_ref, gn_ref, alog_ref, dtb_ref) = refs[:15]
    pos = 15
    if has_init:
        s0f_ref, s0b_ref = refs[pos:pos + 2]
        pos += 2
    o_ref, sco_ref, sf_ref, sb_ref = refs[pos:pos + 4]
    qs, ks, vs, osum, betas, gs, u_s, w_s, qd_s, kd_s, in_s, ge_s, st = refs[pos + 4:]
    head = pl.program_id(1)
    ch = GDN_CHUNK
    n_chunks = seq_len // ch

    def conv_silu(x_ref, c_ref):
        return _silu(_conv3(x_ref[...], c_ref, seq_len))

    def l2norm(x):
        return x * lax.rsqrt(jnp.sum(x * x, axis=-1, keepdims=True) + 1e-6)

    qs[...] = l2norm(conv_silu(q_ref, cq_ref)) * (GDN_D ** -0.5)
    ks[...] = l2norm(conv_silu(k_ref, ck_ref))
    vs[...] = conv_silu(v_ref, cv_ref)
    gates = gate_ref[...]
    betas[...] = jax.nn.sigmoid(gates)
    gs[...] = -jnp.exp(alog_ref[...]) * _softplus(gates + dtb_ref[...])
    sco_ref[...] = scb_ref[...] * _conv3(scc_ref[...] * sch_ref[...], csc_ref, seq_len)
    osum[...] = jnp.zeros(osum.shape, F32)

    lane = lax.broadcasted_iota(jnp.int32, (ch, 128), 1)

    def pick(ref, rows, col):
        return jnp.sum(jnp.where(lane == col, ref[rows, :], 0.0), axis=1, keepdims=True)

    def solve_group(gi, carry):
        items = []
        for j in range(GDN_GROUP):
            c = gi * GDN_GROUP + j
            rows = pl.ds(pl.multiple_of(c * ch, ch), ch)
            items.append(dict(c=c, rows=rows, q=qs[rows, :], k=ks[rows, :], v=vs[rows, :]))
        kks = [_mm_nt(it["k"], it["k"], "bf") for it in items]
        qks = [_mm_nt(it["q"], it["k"], "bf") for it in items]
        subs = []
        for it, kk, qk in zip(items, kks, qks):
            for d in range(2):
                beta = pick(betas, it["rows"], d * GDN_HEADS + head)
                g = pick(gs, it["rows"], 2 * GDN_HEADS + d * GDN_HEADS + head)
                decay, e_gc, e_rest, e_tot = _gdn_decay_terms(g, rev=(d == 1))
                strict = _tri_masks(ch, d == 1)[1]
                subs.append(dict(
                    d=d, c=it["c"], rows=it["rows"],
                    a=jnp.where(strict, kk * beta * decay, 0.0),
                    rhs=jnp.concatenate([it["v"] * beta, it["k"] * (beta * e_gc)], axis=1),
                    intra=qk * decay, qd=it["q"] * e_gc, kd=it["k"] * e_rest, ge=e_tot))
        ts = _unit_tri_inv_many([s["a"] for s in subs], "bf")
        uws = [_mm(t, s["rhs"], "bf") for t, s in zip(ts, subs)]
        for s, uw in zip(subs, uws):
            d, rows = s["d"], s["rows"]
            u_s[d, rows, :] = uw[:, :GDN_D]
            w_s[d, rows, :] = uw[:, GDN_D:].astype(BF16)
            qd_s[d, rows, :] = s["qd"].astype(BF16)
            kd_s[d, rows, :] = s["kd"].astype(BF16)
            in_s[d, rows, :] = s["intra"].astype(BF16)
            ge_s[d, pl.ds(pl.multiple_of(s["c"] * 8, 8), 8), :] = jnp.broadcast_to(s["ge"], (8, 128))
        return carry

    lax.fori_loop(0, seqs * n_chunks // GDN_GROUP, solve_group, 0)

    for j in range(seqs):
        st[0, j] = s0f_ref[j] if has_init else jnp.zeros((GDN_D, GDN_D), F32)
        st[1, j] = s0b_ref[j] if has_init else jnp.zeros((GDN_D, GDN_D), F32)

    def recur(i, carry):
        cs = []
        for j in range(seqs):
            for d in range(2):
                c = j * n_chunks + (i if d == 0 else n_chunks - 1 - i)
                cs.append(dict(d=d, j=j, rows=pl.ds(pl.multiple_of(c * ch, ch), ch),
                               ge=ge_s[d, pl.ds(pl.multiple_of(c * 8, 8), 1), :], s=st[d, j]))
        sbs = [c["s"].astype(BF16) for c in cs]
        wss = [_mm(w_s[c["d"], c["rows"], :], sb, "bf") for c, sb in zip(cs, sbs)]
        qss = [_mm(qd_s[c["d"], c["rows"], :], sb, "bf") for c, sb in zip(cs, sbs)]
        ebs = [(u_s[c["d"], c["rows"], :] - ws).astype(BF16) for c, ws in zip(cs, wss)]
        outs = [qs_ + _mm(in_s[c["d"], c["rows"], :], eb, "bf") for c, qs_, eb in zip(cs, qss, ebs)]
        s_news = [c["s"] * c["ge"] + _mm_tn(kd_s[c["d"], c["rows"], :], eb, "bf") for c, eb in zip(cs, ebs)]
        for c, o, s_new in zip(cs, outs, s_news):
            st[c["d"], c["j"]] = s_new
            osum[c["rows"], :] += o
        return carry

    lax.fori_loop(0, n_chunks, recur, 0)
    sf_ref[...] = st[0]
    sb_ref[...] = st[1]
    o_ref[...] = _rms(osum[...], gn_ref[...]) * _silu(z_ref[...])


def _gdn_mixer(proj, n_seq, seq_len, conv_w, sc_conv_w, gdn_norm, alog_vec, dtb_vec, s0_f, s0_b):
    has_init = s0_f is not None
    hd = GDN_HEADS
    seqs = max(1, GDN_BLOCK_ROWS // seq_len)
    rows = seqs * seq_len
    assert n_seq % seqs == 0 and (rows // GDN_CHUNK) % GDN_GROUP == 0

    def col(block):
        return pl.BlockSpec((rows, 128), lambda s, h, b=block: (s, b * hd + h))

    def wcol(block):
        return pl.BlockSpec((3, 128), lambda s, h, b=block: (0, b * hd + h))

    vec = pl.BlockSpec((1, 128), lambda s, h: (0, 0))
    state = pl.BlockSpec((seqs, None, None, GDN_D, GDN_D), lambda s, h: (s, 0, h, 0, 0))
    in_specs = [col(0), col(1), col(2), col(3), col(4), col(5), col(6),
                pl.BlockSpec((rows, 128), lambda s, h: (s, 7 * hd)),
                wcol(0), wcol(1), wcol(2), wcol(0), vec, vec, vec]
    args = [proj] * 8 + [conv_w, conv_w, conv_w, sc_conv_w, gdn_norm.reshape(1, 128), alog_vec, dtb_vec]
    if has_init:
        in_specs += [state, state]
        args += [s0_f, s0_b]
    t = n_seq * seq_len
    out_tok = pl.BlockSpec((rows, 128), lambda s, h: (s, h))
    scratch = ([pltpu.VMEM((rows, 128), F32) for _ in range(6)]
               + [pltpu.VMEM((2, rows, GDN_D), F32)]
               + [pltpu.VMEM((2, rows, GDN_D), BF16) for _ in range(3)]
               + [pltpu.VMEM((2, rows, GDN_CHUNK), BF16),
                  pltpu.VMEM((2, rows // GDN_CHUNK * 8, 128), F32),
                  pltpu.VMEM((2, seqs, GDN_D, GDN_D), F32)])
    return pl.pallas_call(
        functools.partial(_gdn_kernel, seq_len=seq_len, seqs=seqs, has_init=has_init),
        grid=(n_seq // seqs, hd),
        in_specs=in_specs,
        out_specs=[out_tok, out_tok, state, state],
        out_shape=[jax.ShapeDtypeStruct((t, hd * GDN_D), F32), jax.ShapeDtypeStruct((t, SC_WIDTH), F32),
                   jax.ShapeDtypeStruct((n_seq, 1, hd, GDN_D, GDN_D), F32),
                   jax.ShapeDtypeStruct((n_seq, 1, hd, GDN_D, GDN_D), F32)],
        scratch_shapes=scratch,
        compiler_params=pltpu.CompilerParams(dimension_semantics=("parallel", "parallel"),
                                             vmem_limit_bytes=VMEM_LIMIT),
        name="gdn_mixer",
    )(*args)


def _softmax_pv(scores, values, sink):
    m = sink
    for s in scores:
        m = jnp.maximum(m, jnp.max(s, axis=-1, keepdims=True))
    den = jnp.exp(sink - m)
    acc = None
    for s, v in zip(scores, values):
        e = jnp.exp(s - m)
        den = den + jnp.sum(e, axis=-1, keepdims=True)
        pv = _mm(e, v, prec="bf")
        acc = pv if acc is None else acc + pv
    return acc / den


def _group_sinks(sink_ref, j, rows):
    assert rows & (rows - 1) == 0
    grp = lax.broadcasted_iota(jnp.int32, (ATT_GROUP * rows, 1), 0) >> (rows.bit_length() - 1)
    col = jnp.full(grp.shape, sink_ref[j * ATT_GROUP], F32)
    for gi in range(1, ATT_GROUP):
        col = jnp.where(grp == gi, sink_ref[j * ATT_GROUP + gi], col)
    return col


def _store_group(o_ref, j, o, rows):
    for gi in range(ATT_GROUP):
        hh = j * ATT_GROUP + gi
        o_ref[:, hh * ATT_HD:(hh + 1) * ATT_HD] = o[gi * rows:(gi + 1) * rows]


def _attn_ctx_kernel(sink_ref, p_ref, o_ref, kc_ref, vc_ref):
    scale = ATT_HD ** -0.5
    rows = p_ref.shape[0]
    for j in range(ATT_KV_HEADS):
        k = p_ref[:, ATT_Q_W + j * ATT_HD:ATT_Q_W + (j + 1) * ATT_HD]
        v = p_ref[:, ATT_Q_W + ATT_KV_W + j * ATT_HD:ATT_Q_W + ATT_KV_W + (j + 1) * ATT_HD]
        kc_ref[j] = k
        vc_ref[j] = v
        q = jnp.concatenate([p_ref[:, hh * ATT_HD:(hh + 1) * ATT_HD]
                             for hh in range(j * ATT_GROUP, (j + 1) * ATT_GROUP)], axis=0)
        s = _mm_nt(q, k, prec="bf") * scale
        _store_group(o_ref, j, _softmax_pv([s], [v], _group_sinks(sink_ref, j, rows)), rows)


def _attn_context(proj_att, sink, n_seq, seq_len):
    kv = pl.BlockSpec((None, None, ATT_KV_HEADS, seq_len, ATT_HD), lambda b: (b, 0, 0, 0, 0))
    return pl.pallas_call(
        _attn_ctx_kernel,
        grid=(n_seq,),
        in_specs=[pl.BlockSpec(memory_space=pltpu.SMEM),
                  pl.BlockSpec((seq_len, ATT_W), lambda b: (b, 0))],
        out_specs=[pl.BlockSpec((seq_len, ATT_Q_W), lambda b: (b, 0)), kv, kv],
        out_shape=[jax.ShapeDtypeStruct((n_seq * seq_len, ATT_Q_W), F32),
                   jax.ShapeDtypeStruct((n_seq, 1, ATT_KV_HEADS, seq_len, ATT_HD), F32),
                   jax.ShapeDtypeStruct((n_seq, 1, ATT_KV_HEADS, seq_len, ATT_HD), F32)],
        compiler_params=pltpu.CompilerParams(dimension_semantics=("parallel",)),
        name="attn_context",
    )(sink, proj_att)


def _rope_tables(seq_len):
    pos = np.arange(seq_len)
    half = ATT_HD // 2
    inv = ROPE_BASE ** (-np.arange(0, half, 2, dtype=np.float32) / half)
    ang_r = (pos // GRID_W).astype(np.float32)[:, None] * inv
    ang_c = (pos % GRID_W).astype(np.float32)[:, None] * inv
    cos = np.concatenate([np.cos(ang_r), np.cos(ang_r), np.cos(ang_c), np.cos(ang_c)], axis=1)
    sin = np.concatenate([-np.sin(ang_r), np.sin(ang_r), -np.sin(ang_c), np.sin(ang_c)], axis=1)
    return (jnp.asarray(np.tile(cos, (1, 2)), F32), jnp.asarray(np.tile(sin, (1, 2)), F32))


def _rope(x, cos, sin):
    lane = lax.broadcasted_iota(jnp.int32, x.shape, 1)
    partner = jnp.where((lane & 31) < 16, pltpu.roll(x, 128 - 16, 1), pltpu.roll(x, 16, 1))
    return x * cos + partner * sin


def _attn_lat_kernel(sink_ref, p_ref, ck_ref, cv_ref, cos_ref, sin_ref, o_ref, *, seq_len):
    scale = ATT_HD ** -0.5
    qb = pl.program_id(1)
    span = 3 * ATT_BLOCK
    q0 = pl.multiple_of(qb * ATT_BLOCK, ATT_BLOCK)
    k0 = pl.multiple_of(jnp.clip((qb - 1) * ATT_BLOCK, 0, seq_len - span), ATT_BLOCK)
    qrows = pl.ds(q0, ATT_BLOCK)
    krows = pl.ds(k0, span)
    kwin = _rope(p_ref[krows, ATT_Q_W:ATT_Q_W + ATT_KV_W], cos_ref[krows, :], sin_ref[krows, :])
    vwin = p_ref[krows, ATT_Q_W + ATT_KV_W:ATT_W]
    stacked = ATT_GROUP * ATT_BLOCK
    qpos = q0 + (lax.broadcasted_iota(jnp.int32, (stacked, span), 0) & (ATT_BLOCK - 1))
    kpos = k0 + lax.broadcasted_iota(jnp.int32, (stacked, span), 1)
    valid = jnp.abs(qpos - kpos) <= WINDOW
    cos_q = cos_ref[qrows, :]
    sin_q = sin_ref[qrows, :]
    heads = []
    for pair in range(ATT_HEADS // 2):
        qpair = _rope(p_ref[qrows, pair * 128:(pair + 1) * 128], cos_q, sin_q)
        heads += [qpair[:, :ATT_HD], qpair[:, ATT_HD:]]
    kv = range(ATT_KV_HEADS)
    qs = [jnp.concatenate(heads[j * ATT_GROUP:(j + 1) * ATT_GROUP], axis=0) for j in kv]
    s_locs = [jnp.where(valid, _mm_nt(qs[j], kwin[:, j * ATT_HD:(j + 1) * ATT_HD], prec="bf") * scale, NEG_INF)
              for j in kv]
    s_ctxs = [_mm_nt(qs[j], ck_ref[j], prec="bf") * scale for j in kv]
    outs = [_softmax_pv([s_locs[j], s_ctxs[j]], [vwin[:, j * ATT_HD:(j + 1) * ATT_HD], cv_ref[j]],
                        _group_sinks(sink_ref, j, ATT_BLOCK)) for j in kv]
    for j in kv:
        _store_group(o_ref, j, outs[j], ATT_BLOCK)


def _attn_latent(proj_att, sink, cache_k, cache_v, layer, n_seq, seq_len):
    cos, sin = _rope_tables(seq_len)
    past = cache_k.shape[3]
    nqb = seq_len // ATT_BLOCK
    cache = pl.BlockSpec((None, None, ATT_KV_HEADS, past, ATT_HD), lambda b, q: (b, layer, 0, 0, 0))
    table = pl.BlockSpec((seq_len, 128), lambda b, q: (0, 0))
    return pl.pallas_call(
        functools.partial(_attn_lat_kernel, seq_len=seq_len),
        grid=(n_seq, nqb),
        in_specs=[pl.BlockSpec(memory_space=pltpu.SMEM),
                  pl.BlockSpec((seq_len, ATT_W), lambda b, q: (b, 0)),
                  cache, cache, table, table],
        out_specs=pl.BlockSpec((ATT_BLOCK, ATT_Q_W), lambda b, q: (b * nqb + q, 0)),
        out_shape=jax.ShapeDtypeStruct((n_seq * seq_len, ATT_Q_W), F32),
        compiler_params=pltpu.CompilerParams(dimension_semantics=("parallel", "parallel")),
        name="attn_latent",
    )(sink, proj_att, cache_k, cache_v, cos, sin)


def _pair_masks(n, rev):
    r = lax.broadcasted_iota(jnp.int32, (n, 2 * n), 0)
    c = lax.broadcasted_iota(jnp.int32, (n, 2 * n), 1) & (n - 1)
    return (r <= c, r < c) if rev else (r >= c, r > c)


def _bd(x):
    half = x.shape[1] // 2
    lane = lax.broadcasted_iota(jnp.int32, x.shape, 1)
    zero = jnp.zeros_like(x)
    return jnp.concatenate([jnp.where(lane < half, x, zero), jnp.where(lane >= half, x, zero)], axis=0)


def _bd_mask(n):
    r = lax.broadcasted_iota(jnp.int32, (2 * n, 2 * n), 0)
    c = lax.broadcasted_iota(jnp.int32, (2 * n, 2 * n), 1)
    return (r < n) == (c < n)


def _head_sums(x):
    ones = jnp.where(_bd_mask(RWKV_HD), 1.0, 0.0).astype(BF16)
    return _mm(x, ones, "x2l")


def _apply_pairs(tb, x):
    hi, lo = _split_bf16(x)
    return _mm(tb, _bd(hi), "bf") + _mm(tb, _bd(lo), "bf")


def _unit_tri_inv_pairs(mats):
    n = mats[0].shape[0]
    r = lax.broadcasted_iota(jnp.int32, (n, 2 * n), 0)
    c = lax.broadcasted_iota(jnp.int32, (n, 2 * n), 1) & (n - 1)
    eye = jnp.where(r == c, 1.0, 0.0)
    ts = [eye - a * jnp.where((r >> 1) == (c >> 1), 1.0, 0.0) for a in mats]
    abs_ = [a.astype(BF16) for a in mats]
    shift = 1
    while (1 << shift) < n:
        join = jnp.where(((r >> (shift + 1)) == (c >> (shift + 1))) & ((r >> shift) != (c >> shift)),
                         1.0, 0.0).astype(BF16)
        tbs = [t.astype(BF16) for t in ts]
        inner = [_mm(ab * join, _bd(tb), "bf") for ab, tb in zip(abs_, tbs)]
        ts = [t - _mm(tb, _bd(w.astype(BF16)), "bf") for t, tb, w in zip(ts, tbs, inner)]
        shift += 1
    return ts


def _rwkv_chunk_operands(x_ref, prm, sc, c, *, seq_len):
    (mu_ref, w0_ref, wup_ref, a0_ref, aup_ref, gup_ref, kk_ref, ka_ref, rk_ref) = prm
    ch = RWKV_CHUNK
    total = x_ref.shape[0]
    r0 = pl.multiple_of(c * ch, ch)
    rows = pl.ds(r0, ch)
    pos0 = r0 & (seq_len - 1)
    x = x_ref[rows, :]
    prev_row = x_ref[pl.ds(jnp.maximum(r0 - 1, 0), 1), :] * jnp.where(pos0 > 0, 1.0, 0.0)
    next_row = x_ref[pl.ds(jnp.minimum(r0 + ch, total - 1), 1), :] * jnp.where(pos0 + ch < seq_len, 1.0, 0.0)
    xp, xn = _shifted_rows(x, prev_row, next_row)
    xs = x + mu_ref[0:1, :] * (xp - x) + mu_ref[1:2, :] * (xn - x)
    r = xs[:, 0:RWKV_W]
    k = xs[:, RWKV_W:2 * RWKV_W]
    v = xs[:, 2 * RWKV_W:3 * RWKV_W]
    lo = 3 * RWKV_W
    p = RWKV_PREC
    gl = xs[:, lo + 4 * RWKV_LORA:lo + 6 * RWKV_LORA]
    sc["gate"][rows, :] = _mm(jax.nn.sigmoid(gl), gup_ref[...], p["gate"])
    sc["v"][rows, :] = v.astype(BF16)
    pairs = [slice(i * PAIR_W, (i + 1) * PAIR_W) for i in range(N_PAIRS)]
    kkv = k * kk_ref[...]
    kaps = []
    for cols in pairs:
        kk_p = kkv[:, cols]
        kaps.append(kk_p * lax.rsqrt(_head_sums(kk_p * kk_p) + 1e-6))
    items = []
    bonus = None
    for d in range(2):
        rev = d == 1
        wl = xs[:, lo + d * RWKV_LORA:lo + (d + 1) * RWKV_LORA]
        al = xs[:, lo + 2 * RWKV_LORA + d * RWKV_LORA:lo + 2 * RWKV_LORA + (d + 1) * RWKV_LORA]
        lw = -DECAY_SCALE * jax.nn.sigmoid(w0_ref[d:d + 1, :] + _mm(jnp.tanh(wl), wup_ref[d], p["lora"]))
        a = jax.nn.sigmoid(a0_ref[d:d + 1, :] + _mm(al, aup_ref[d], p["lora"]))
        k2 = k * (1.0 + (a - 1.0) * ka_ref[...])
        g_inc = _mm(jnp.where(_tri_masks(ch, rev)[0], 1.0, 0.0), lw, p["cumsum"])
        g_tot = jnp.sum(lw, axis=0, keepdims=True)
        e_neg = jnp.exp(-g_inc)
        e_end = jnp.exp(g_tot - g_inc)
        e_exc = jnp.exp(g_inc - lw)
        r_dec = r * jnp.exp(g_inc)
        k_neg = k2 * e_neg
        k_end = k2 * e_end
        sc["dec"][d, pl.ds(pl.multiple_of(c * 8, 8), 8), :] = jnp.broadcast_to(jnp.exp(g_tot), (8, RWKV_W))
        rkr = r * k2 * rk_ref[...]
        bon_d = jnp.concatenate([_head_sums(rkr[:, cols]) for cols in pairs], axis=1) * v
        bonus = bon_d if bonus is None else bonus + bon_d
        for cols, kap in zip(pairs, kaps):
            b_p = kap * a[:, cols]
            items.append(dict(d=d, rows=rows, cols=cols, kap_dec=kap * e_exc[:, cols], r_dec=r_dec[:, cols],
                              b_neg=b_p * e_neg[:, cols], k_neg=k_neg[:, cols],
                              b_end=b_p * e_end[:, cols], k_end=k_end[:, cols], vb=v[:, cols].astype(BF16)))
    sc["bon"][rows, :] = bonus
    return items


def _rwkv_solve(items, sc):
    ch = RWKV_CHUNK
    masks = [_pair_masks(ch, False), _pair_masks(ch, True)]
    ms = [_mm_nt(jnp.concatenate([it["kap_dec"], it["r_dec"]], axis=0),
                 jnp.concatenate([_bd(it["b_neg"].astype(BF16)), _bd(it["k_neg"].astype(BF16))], axis=0), "bf")
          for it in items]
    a_abs = [jnp.where(masks[it["d"]][1], m[:ch, :2 * ch], 0.0) for it, m in zip(items, ms)]
    a_aks = [jnp.where(masks[it["d"]][1], m[:ch, 2 * ch:], 0.0) for it, m in zip(items, ms)]
    a_rbs = [jnp.where(masks[it["d"]][0], m[ch:, :2 * ch], 0.0) for it, m in zip(items, ms)]
    a_rks = [jnp.where(masks[it["d"]][0], m[ch:, 2 * ch:], 0.0) for it, m in zip(items, ms)]
    tbs = [t.astype(BF16) for t in _unit_tri_inv_pairs(a_abs)]
    akvs = [_mm(a_ak, _bd(it["vb"]), "bf") for it, a_ak in zip(items, a_aks)]
    w2s = [_apply_pairs(tb, it["kap_dec"]) for it, tb in zip(items, tbs)]
    u0s = [_apply_pairs(tb, akv) for tb, akv in zip(tbs, akvs)]
    for it, w2, u0, a_rb, a_rk in zip(items, w2s, u0s, a_rbs, a_rks):
        d, rows, cols = it["d"], it["rows"], it["cols"]
        sc["w2"][d, rows, cols] = w2.astype(BF16)
        sc["rd"][d, rows, cols] = it["r_dec"].astype(BF16)
        sc["u0"][d, rows, cols] = u0
        sc["arb"][d, rows, cols] = a_rb.astype(BF16)
        sc["ark"][d, rows, cols] = a_rk.astype(BF16)
        sc["bh"][d, rows, cols] = it["b_end"].astype(BF16)
        sc["kh"][d, rows, cols] = it["k_end"].astype(BF16)


def _rwkv_recur(sc, st, i, *, seq_len, seqs):
    ch = RWKV_CHUNK
    n_chunks = seq_len // ch
    keep = _bd_mask(RWKV_HD)
    cs = []
    for j in range(seqs):
        for d in range(2):
            c = j * n_chunks + (i if d == 0 else n_chunks - 1 - i)
            rows = pl.ds(pl.multiple_of(c * ch, ch), ch)
            dec = sc["dec"][d, pl.ds(pl.multiple_of(c * 8, 8), 1), :]
            for pi in range(N_PAIRS):
                cols = slice(pi * PAIR_W, (pi + 1) * PAIR_W)
                cs.append(dict(d=d, j=j, p=pi, rows=rows, cols=cols, dec=dec[:, cols], s=st[d, j, pi]))
    sbs = [c["s"].astype(BF16) for c in cs]
    lss = [_mm_nt(jnp.concatenate([sc["w2"][c["d"], c["rows"], c["cols"]],
                                   sc["rd"][c["d"], c["rows"], c["cols"]]], axis=0), sb, "bf")
           for c, sb in zip(cs, sbs)]
    ubs = [(-(ls[:ch] + sc["u0"][c["d"], c["rows"], c["cols"]])).astype(BF16) for c, ls in zip(cs, lss)]
    vbs = [sc["v"][c["rows"], c["cols"]] for c in cs]
    ys = [ls[ch:] + _mm(jnp.concatenate([sc["arb"][c["d"], c["rows"], c["cols"]],
                                         sc["ark"][c["d"], c["rows"], c["cols"]]], axis=1),
                        jnp.concatenate([_bd(ub), _bd(vb)], axis=0), "bf")
          for c, ls, ub, vb in zip(cs, lss, ubs, vbs)]
    ups = [_mm_tn(jnp.concatenate([ub, vb], axis=0),
                  jnp.concatenate([sc["bh"][c["d"], c["rows"], c["cols"]],
                                   sc["kh"][c["d"], c["rows"], c["cols"]]], axis=0), "bf")
           for c, ub, vb in zip(cs, ubs, vbs)]
    for c, y, up in zip(cs, ys, ups):
        st[c["d"], c["j"], c["p"]] = c["s"] * c["dec"] + jnp.where(keep, up, 0.0)
        sc["ysum"][c["rows"], c["cols"]] += y


RWKV_SCRATCH = ("ysum", "bon", "gate", "v", "w2", "rd", "u0", "arb", "ark", "bh", "kh", "dec")


def _rwkv_kernel(*refs, seq_len, seqs, has_init):
    x_ref = refs[0]
    prm = refs[1:10]
    lnw_ref, lnb_ref = refs[10:12]
    pos = 12
    if has_init:
        s0f_ref, s0b_ref = refs[pos:pos + 2]
        pos += 2
    o_ref, sf_ref, sb_ref = refs[pos:pos + 3]
    sc = dict(zip(RWKV_SCRATCH, refs[pos + 3:]))
    st = refs[pos + 3 + len(RWKV_SCRATCH)]
    ch = RWKV_CHUNK
    hd = RWKV_HD
    n_chunks = seq_len // ch
    sc["ysum"][...] = jnp.zeros(sc["ysum"].shape, F32)

    def prepare(gi, carry):
        items = []
        for j in range(RWKV_GROUP):
            items += _rwkv_chunk_operands(x_ref, prm, sc, gi * RWKV_GROUP + j, seq_len=seq_len)
        _rwkv_solve(items, sc)
        return carry

    lax.fori_loop(0, seqs * n_chunks // RWKV_GROUP, prepare, 0)

    zero = jnp.zeros((hd, hd), F32)
    for d, s0_ref in enumerate((s0f_ref, s0b_ref) if has_init else (None, None)):
        for j in range(seqs):
            for pi in range(N_PAIRS):
                s_a = s0_ref[j, 2 * pi] if has_init else zero
                s_b = s0_ref[j, 2 * pi + 1] if has_init else zero
                st[d, j, pi] = jnp.concatenate([jnp.concatenate([s_a, zero], axis=1),
                                                jnp.concatenate([zero, s_b], axis=1)], axis=0)

    def recur(i, carry):
        _rwkv_recur(sc, st, i, seq_len=seq_len, seqs=seqs)
        return carry

    lax.fori_loop(0, n_chunks, recur, 0)
    for d, out_ref in enumerate((sf_ref, sb_ref)):
        for j in range(seqs):
            for pi in range(N_PAIRS):
                s = st[d, j, pi]
                out_ref[j, 2 * pi] = s[:hd, :hd]
                out_ref[j, 2 * pi + 1] = s[hd:, hd:]

    tile = RWKV_FINISH_ROWS

    def finish(i, carry):
        rows = pl.ds(pl.multiple_of(i * tile, tile), tile)
        pairs = [slice(pi * PAIR_W, (pi + 1) * PAIR_W) for pi in range(N_PAIRS)]
        ys = [sc["ysum"][rows, cols] for cols in pairs]
        cens = [y - _head_sums(y) * (1.0 / hd) for y in ys]
        vars_ = [_head_sums(cen * cen) * (1.0 / hd) for cen in cens]
        for cols, cen, var in zip(pairs, cens, vars_):
            yn = cen * lax.rsqrt(var + GN_EPS) * lnw_ref[:, cols] + lnb_ref[:, cols]
            o_ref[rows, cols] = (yn + sc["bon"][rows, cols]) * sc["gate"][rows, cols]
        return carry

    lax.fori_loop(0, seqs * seq_len // tile, finish, 0)


def _rwkv_mixer(x_rw, n_seq, seq_len, params, s0_f, s0_b):
    has_init = s0_f is not None
    (mu, w0, w_up, a0, a_up, g_up, k_k, k_a, r_k, ln_w, ln_b) = params
    row = lambda a: a.reshape(1, RWKV_W)
    args = [x_rw, mu, w0, w_up, a0, a_up, g_up, row(k_k), row(k_a), row(r_k), row(ln_w), row(ln_b)]

    def whole(a):
        nd = a.ndim
        return pl.BlockSpec(a.shape, lambda s, nd=nd: (0,) * nd)

    seqs = max(1, RWKV_BLOCK_ROWS // seq_len)
    rows = seqs * seq_len
    assert n_seq % seqs == 0 and (rows // RWKV_CHUNK) % RWKV_GROUP == 0 and seq_len & (seq_len - 1) == 0
    in_specs = [pl.BlockSpec((rows, RWKV_IN), lambda s: (s, 0))] + [whole(a) for a in args[1:]]
    state = pl.BlockSpec((seqs, None, RWKV_HEADS, RWKV_HD, RWKV_HD), lambda s: (s, 0, 0, 0, 0))
    if has_init:
        in_specs += [state, state]
        args += [s0_f, s0_b]
    st_shape = jax.ShapeDtypeStruct((n_seq, 1, RWKV_HEADS, RWKV_HD, RWKV_HD), F32)
    tok = lambda dt: pltpu.VMEM((rows, RWKV_W), dt)
    per_dir = lambda dt: pltpu.VMEM((2, rows, RWKV_W), dt)
    scratch = dict(ysum=tok(F32), bon=tok(F32), gate=tok(F32), v=tok(BF16), w2=per_dir(BF16), rd=per_dir(BF16),
                   u0=per_dir(F32), arb=per_dir(BF16), ark=per_dir(BF16), bh=per_dir(BF16), kh=per_dir(BF16),
                   dec=pltpu.VMEM((2, rows // RWKV_CHUNK * 8, RWKV_W), F32))
    return pl.pallas_call(
        functools.partial(_rwkv_kernel, seq_len=seq_len, seqs=seqs, has_init=has_init),
        grid=(n_seq // seqs,),
        in_specs=in_specs,
        out_specs=[pl.BlockSpec((rows, RWKV_W), lambda s: (s, 0)), state, state],
        out_shape=[jax.ShapeDtypeStruct((n_seq * seq_len, RWKV_W), F32), st_shape, st_shape],
        scratch_shapes=[scratch[name] for name in RWKV_SCRATCH]
        + [pltpu.VMEM((2, seqs, N_PAIRS, PAIR_W, PAIR_W), F32)],
        compiler_params=pltpu.CompilerParams(dimension_semantics=("parallel",), vmem_limit_bytes=VMEM_LIMIT),
        name="rwkv_mixer",
    )(*args)


def kernel(x_prompt, x_sample, state_gdn_fwd, state_gdn_bwd, cache_attn_k, cache_attn_v, state_rwkv_fwd, state_rwkv_bwd, c, c_ctx, mod_w, mod_b, norm_mix, norm_mlp, mlp_w1, mlp_w2, norm_final, ev_w_in, ev_w_out, gdn_conv, gdn_a_log, gdn_dt_bias, gdn_norm, sc_conv, od_w_in, od_w_out, attn_sink, rwkv_mu, rwkv_w0, rwkv_w_up, rwkv_a0, rwkv_a_up, rwkv_g_up, rwkv_k_k, rwkv_k_a, rwkv_r_k, rwkv_ln_w, rwkv_ln_b):
    bp, lp, _ = x_prompt.shape
    bs, ls, _ = x_sample.shape
    depth = mod_w.shape[0]
    c_rows = jnp.concatenate([c_ctx[None, :], c, jnp.zeros((MOD_ROWS - 1 - bs, D_MODEL), F32)], axis=0)
    mods = _modulation(c_rows, mod_w, mod_b)

    assert ls % TOKEN_TILE == 0 and (bp * lp) % TOKEN_TILE == 0
    groups = [
        dict(x=x_prompt.reshape(bp * lp, D_MODEL), n=bp, l=lp, latent=False,
             mod=lambda layer: _mod_spec(layer, 1, 0, 0)),
        dict(x=x_sample.reshape(bs * ls, D_MODEL), n=bs, l=ls, latent=True,
             mod=lambda layer: _mod_spec(layer, ls // TOKEN_TILE, 1, 1)),
    ]
    outs = {}
    w1 = mlp_w1.astype(BF16)
    w2 = mlp_w2.astype(BF16)
    for layer in range(depth):
        final = layer == depth - 1
        if layer % 2 == 0:
            e = layer // 2
            w = ev_w_in[e]
            qkvz = GDN_QKV_W + GDN_HEADS * GDN_D
            n_gate = 4 * GDN_HEADS
            gate_cols = jnp.concatenate([w[:, qkvz:qkvz + n_gate],
                                         jnp.zeros((D_MODEL, EV_IN_PAD - w.shape[1]), F32)], axis=1)
            w_in = [w[:, :qkvz].astype(BF16), w[:, qkvz + n_gate:].astype(BF16), gate_cols.astype(BF16)]
            w_out = ev_w_out[e].astype(BF16)
            alog_vec = jnp.zeros((1, 128), F32).at[0, 2 * GDN_HEADS:4 * GDN_HEADS].set(gdn_a_log[e].reshape(-1))
            dtb_vec = jnp.zeros((1, 128), F32).at[0, 2 * GDN_HEADS:4 * GDN_HEADS].set(gdn_dt_bias[e].reshape(-1))
            for grp in groups:
                (proj,) = _inproj(grp["x"], mods, norm_mix[layer], w_in, (EV_IN_PAD,), grp["mod"](layer))
                s0 = (state_gdn_fwd[:, e:e + 1], state_gdn_bwd[:, e:e + 1]) if grp["latent"] else (None, None)
                o, sc, s_f, s_b = _gdn_mixer(proj, grp["n"], grp["l"], gdn_conv[e], sc_conv[e], gdn_norm[e],
                                             alog_vec, dtb_vec, *s0)
                if not grp["latent"]:
                    outs.setdefault("gdn_f", []).append(s_f)
                    outs.setdefault("gdn_b", []).append(s_b)
                grp["x"] = _outproj_mlp(o, sc, grp["x"], mods, norm_mlp[layer], w_out, w1, w2, layer, norm_final,
                                        grp["mod"](layer), final)
        else:
            o_ = layer // 2
            w = od_w_in[o_]
            w_in = [w.astype(BF16)]
            w_out = od_w_out[o_].astype(BF16)
            rw = (rwkv_mu[o_], rwkv_w0[o_], rwkv_w_up[o_], rwkv_a0[o_], rwkv_a_up[o_], rwkv_g_up[o_],
                  rwkv_k_k[o_], rwkv_k_a[o_], rwkv_r_k[o_].reshape(-1), rwkv_ln_w[o_], rwkv_ln_b[o_])
            for grp in groups:
                p_att, x_rw = _inproj(grp["x"], mods, norm_mix[layer], w_in, (ATT_W, RWKV_IN), grp["mod"](layer))
                if grp["latent"]:
                    att = _attn_latent(p_att, attn_sink[o_], cache_attn_k, cache_attn_v, o_, grp["n"], grp["l"])
                    rwo, _, _ = _rwkv_mixer(x_rw, grp["n"], grp["l"], rw,
                                            state_rwkv_fwd[:, o_:o_ + 1], state_rwkv_bwd[:, o_:o_ + 1])
                else:
                    att, kc, vc = _attn_context(p_att, attn_sink[o_], grp["n"], grp["l"])
                    rwo, s_f, s_b = _rwkv_mixer(x_rw, grp["n"], grp["l"], rw, None, None)
                    outs.setdefault("att_k", []).append(kc)
                    outs.setdefault("att_v", []).append(vc)
                    outs.setdefault("rw_f", []).append(s_f)
                    outs.setdefault("rw_b", []).append(s_b)
                grp["x"] = _outproj_mlp(att, rwo, grp["x"], mods, norm_mlp[layer], w_out, w1, w2, layer, norm_final,
                                        grp["mod"](layer), final)
    cat = lambda key: jnp.concatenate(outs[key], axis=1)
    return (groups[0]["x"].reshape(bp, lp, D_MODEL), groups[1]["x"].reshape(bs, ls, D_MODEL),
            cat("gdn_f"), cat("gdn_b"), cat("att_k"), cat("att_v"), cat("rw_f"), cat("rw_b"))
```

```python
import functools

import jax
import jax.numpy as jnp
import numpy as np
from jax import lax
from jax.experimental import pallas as pl
from jax.experimental.pallas import tpu as pltpu

F32 = jnp.float32
BF16 = jnp.bfloat16
HIGHEST = lax.Precision.HIGHEST

D_MODEL = 1024
N_MOD = 6
D_FF = 4 * D_MODEL
NORM_EPS = 1e-6
TOKEN_TILE = 512
HALO = 8
MOD_ROWS = 8

GDN_HEADS = 4
GDN_D = 128
GDN_CHUNK = 128
GDN_GROUP = 8
GDN_BLOCK_ROWS = 1024
GDN_QKV_W = 3 * GDN_HEADS * GDN_D
SC_WIDTH = 512
EV_IN_PAD = 3712

ATT_HEADS = 8
ATT_KV_HEADS = 2
ATT_GROUP = ATT_HEADS // ATT_KV_HEADS
ATT_HD = 64
ATT_Q_W = ATT_HEADS * ATT_HD
ATT_KV_W = ATT_KV_HEADS * ATT_HD
ATT_W = ATT_Q_W + 2 * ATT_KV_W
WINDOW = 128
ATT_BLOCK = 128
GRID_W = 64
ROPE_BASE = 10000.0
NEG_INF = -1e30

RWKV_HEADS = 8
RWKV_HD = 64
RWKV_W = RWKV_HEADS * RWKV_HD
RWKV_LORA = 64
RWKV_IN = 3 * RWKV_W + 3 * 2 * RWKV_LORA
RWKV_CHUNK = 64
RWKV_GROUP = 4
RWKV_FINISH_ROWS = 256
DECAY_SCALE = float(np.exp(-0.5))
RWKV_BLOCK_ROWS = 1024
PAIR_W = 2 * RWKV_HD
N_PAIRS = RWKV_W // PAIR_W
GN_EPS = 64e-5

VMEM_LIMIT = 56 * 1024 * 1024

RWKV_PREC = dict(lora="bf", gate="bf", cumsum="x2r", amat="bf", inv="bf", state="bf", akv="bf", solve="x2r",
                 out="bf", update="bf")


def _split_bf16(a):
    hi = a.astype(BF16)
    return hi, (a - hi.astype(F32)).astype(BF16)


def _dot(a, b, dims, prec):
    dn = (dims, ((), ()))
    if prec == "hi":
        return lax.dot_general(a, b, dn, precision=HIGHEST, preferred_element_type=F32)
    one = lambda x, y: lax.dot_general(x, y, dn, preferred_element_type=F32)
    if prec == "x3":
        ah, al = _split_bf16(a)
        bh, bl = _split_bf16(b)
        return one(ah, bh) + one(ah, bl) + one(al, bh)
    if prec == "x2l":
        ah, al = _split_bf16(a)
        bh = b.astype(BF16)
        return one(ah, bh) + one(al, bh)
    if prec == "x2r":
        ah = a.astype(BF16)
        bh, bl = _split_bf16(b)
        return one(ah, bh) + one(ah, bl)
    assert prec == "bf", prec
    return lax.dot_general(a.astype(BF16), b.astype(BF16), dn, preferred_element_type=F32)


def _mm(a, b, prec="hi"):
    return _dot(a, b, ((1,), (0,)), prec)


def _mm_nt(a, b, prec="hi"):
    return _dot(a, b, ((1,), (1,)), prec)


def _mm_tn(a, b, prec="hi"):
    return _dot(a, b, ((0,), (0,)), prec)


def _silu(x):
    return x * jax.nn.sigmoid(x)


def _softplus(x):
    return jnp.maximum(x, 0.0) + jnp.log1p(jnp.exp(-jnp.abs(x)))


def _rms(x, w):
    return x * lax.rsqrt(jnp.mean(x * x, axis=-1, keepdims=True) + NORM_EPS) * w


def _tri_masks(n, rev):
    r = lax.broadcasted_iota(jnp.int32, (n, n), 0)
    c = lax.broadcasted_iota(jnp.int32, (n, n), 1)
    if rev:
        return r <= c, r < c
    return r >= c, r > c


def _unit_tri_inv_many(mats, prec):
    assert prec == "bf"
    n = mats[0].shape[0]
    r = lax.broadcasted_iota(jnp.int32, (n, n), 0)
    c = lax.broadcasted_iota(jnp.int32, (n, n), 1)
    eye = jnp.where(r == c, 1.0, 0.0)
    ts = [eye - a * jnp.where((r >> 1) == (c >> 1), 1.0, 0.0) for a in mats]
    abs_ = [a.astype(BF16) for a in mats]
    shift = 1
    while (1 << shift) < n:
        join = jnp.where(((r >> (shift + 1)) == (c >> (shift + 1))) & ((r >> shift) != (c >> shift)),
                         1.0, 0.0).astype(BF16)
        tbs = [t.astype(BF16) for t in ts]
        inner = [_mm(ab * join, tb, prec) for ab, tb in zip(abs_, tbs)]
        ts = [t - _mm(tb, w, prec) for t, tb, w in zip(ts, tbs, inner)]
        shift += 1
    return ts


def _shifted_rows(x, prev_row, next_row):
    n = x.shape[0]
    row = lax.broadcasted_iota(jnp.int32, x.shape, 0)
    xp = jnp.where(row == 0, prev_row, pltpu.roll(x, 1, 0))
    xn = jnp.where(row == n - 1, next_row, pltpu.roll(x, n - 1, 0))
    return xp, xn


def _conv3(x, w_ref, seq_len):
    n = x.shape[0]
    assert seq_len & (seq_len - 1) == 0 and n % seq_len == 0
    pos = lax.broadcasted_iota(jnp.int32, x.shape, 0) & (seq_len - 1)
    xp = jnp.where(pos == 0, 0.0, pltpu.roll(x, 1, 0))
    xn = jnp.where(pos == seq_len - 1, 0.0, pltpu.roll(x, n - 1, 0))
    return xp * w_ref[0:1, :] + x * w_ref[1:2, :] + xn * w_ref[2:3, :]


def _mod_kernel(c_ref, w_ref, b_ref, o_ref):
    s = _silu(c_ref[...])
    o_ref[...] = _mm(s, w_ref[...], prec="bf") + b_ref[...]


def _modulation(c_rows, mod_w, mod_b):
    depth = mod_w.shape[0]
    nblk = (N_MOD * D_MODEL) // D_MODEL
    out = pl.pallas_call(
        _mod_kernel,
        grid=(depth, nblk),
        in_specs=[
            pl.BlockSpec((MOD_ROWS, D_MODEL), lambda l, j: (0, 0)),
            pl.BlockSpec((None, D_MODEL, D_MODEL), lambda l, j: (l, 0, j)),
            pl.BlockSpec((None, 1, D_MODEL), lambda l, j: (l, 0, j)),
        ],
        out_specs=pl.BlockSpec((None, MOD_ROWS, D_MODEL), lambda l, j: (l, 0, j)),
        out_shape=jax.ShapeDtypeStruct((depth, MOD_ROWS, N_MOD * D_MODEL), F32),
        compiler_params=pltpu.CompilerParams(dimension_semantics=("parallel", "parallel")),
        name="modulation",
    )(c_rows, mod_w, mod_b.reshape(depth, 1, N_MOD * D_MODEL))
    return out.reshape(depth, MOD_ROWS, N_MOD, D_MODEL)


def _mod_spec(layer, tiles_per_seq, row_base, row_step):
    return pl.BlockSpec((None, None, N_MOD, D_MODEL),
                        lambda i: (layer, row_base + (i // tiles_per_seq) * row_step, 0, 0))


def _inproj_kernel(*refs, n_w):
    x_ref, mod_ref, nw_ref = refs[:3]
    w_refs = refs[3:3 + n_w]
    o_refs = refs[3 + n_w:]
    h = _rms(x_ref[...], nw_ref[...])
    h = (h * (1.0 + mod_ref[1:2, :]) + mod_ref[0:1, :]).astype(BF16)
    pieces = [_mm(h, w_ref[...], prec="bf") for w_ref in w_refs]
    y = pieces[0] if n_w == 1 else jnp.concatenate(pieces, axis=1)
    off = 0
    for o_ref in o_refs:
        n = o_ref.shape[-1]
        o_ref[...] = y[:, off:off + n]
        off += n


def _inproj(x, mods, norm_w, ws_bf16, splits, mod_spec):
    t = x.shape[0]
    assert sum(w.shape[1] for w in ws_bf16) == sum(splits) and all(w.shape[1] % 128 == 0 for w in ws_bf16)
    return pl.pallas_call(
        functools.partial(_inproj_kernel, n_w=len(ws_bf16)),
        grid=(t // TOKEN_TILE,),
        in_specs=[
            pl.BlockSpec((TOKEN_TILE, D_MODEL), lambda i: (i, 0)),
            mod_spec,
            pl.BlockSpec((1, D_MODEL), lambda i: (0, 0)),
        ] + [pl.BlockSpec(w.shape, lambda i: (0, 0)) for w in ws_bf16],
        out_specs=[pl.BlockSpec((TOKEN_TILE, n), lambda i: (i, 0)) for n in splits],
        out_shape=[jax.ShapeDtypeStruct((t, n), F32) for n in splits],
        compiler_params=pltpu.CompilerParams(dimension_semantics=("parallel",), vmem_limit_bytes=VMEM_LIMIT),
        name="inproj",
    )(x, mods, norm_w.reshape(1, D_MODEL), *ws_bf16)


def _inproj_even_kernel(xp_ref, x_ref, xn_ref, mod_ref, nw_ref, wqkvz_ref, wsc_ref, wgate_ref, cqkv_ref, csc_ref,
                        alog_ref, dtb_ref, qkv_ref, gz_ref, sc_ref, gate_ref, *, seq_len):
    tile = x_ref.shape[0]
    x = jnp.concatenate([xp_ref[...], x_ref[...], xn_ref[...]], axis=0)
    h = _rms(x, nw_ref[...])
    h = (h * (1.0 + mod_ref[1:2, :]) + mod_ref[0:1, :]).astype(BF16)
    n = tile + 2 * HALO
    first = pl.program_id(0) * tile - HALO
    pos = (first + lax.broadcasted_iota(jnp.int32, (n, 1), 0)) & (seq_len - 1)
    at_start = pos == 0
    at_end = pos == seq_len - 1

    def conv3(v, c_ref):
        vp = jnp.where(at_start, 0.0, pltpu.roll(v, 1, 0))
        vn = jnp.where(at_end, 0.0, pltpu.roll(v, n - 1, 0))
        return vp * c_ref[0:1, :] + v * c_ref[1:2, :] + vn * c_ref[2:3, :]

    keep = slice(HALO, HALO + tile)
    wide = 2 * GDN_D
    for j in range(GDN_QKV_W // wide):
        cols = slice(j * wide, (j + 1) * wide)
        act = _silu(conv3(_mm(h, wqkvz_ref[:, cols], "bf"), cqkv_ref.at[:, cols]))[keep]
        for i in range(2):
            part = act[:, i * GDN_D:(i + 1) * GDN_D]
            if j < 2 * GDN_HEADS // 2:
                part = part * lax.rsqrt(jnp.sum(part * part, axis=-1, keepdims=True) + 1e-6)
                if j < GDN_HEADS // 2:
                    part = part * (GDN_D ** -0.5)
            qkv_ref[:, j * wide + i * GDN_D:j * wide + (i + 1) * GDN_D] = part.astype(BF16)
    for j in range(GDN_HEADS * GDN_D // wide):
        cols = slice(j * wide, (j + 1) * wide)
        zcols = slice(GDN_QKV_W + j * wide, GDN_QKV_W + (j + 1) * wide)
        gz_ref[:, cols] = _silu(_mm(h, wqkvz_ref[:, zcols], "bf")[keep]).astype(BF16)
    for j in range(SC_WIDTH // wide):
        sc_b, sc_c, sc_h = [_mm(h, wsc_ref[:, i * SC_WIDTH + j * wide:i * SC_WIDTH + (j + 1) * wide], "bf")
                            for i in range(3)]
        cols = slice(j * wide, (j + 1) * wide)
        sc_ref[:, cols] = (sc_b * conv3(sc_c * sc_h, csc_ref.at[:, cols]))[keep].astype(BF16)
    g = _mm(h, wgate_ref[...], "bf")[keep]
    lane = lax.broadcasted_iota(jnp.int32, g.shape, 1)
    gate_ref[...] = jnp.where(lane < 2 * GDN_HEADS, jax.nn.sigmoid(g),
                              -jnp.exp(alog_ref[...]) * _softplus(g + dtb_ref[...]))


def _inproj_even(x, mods, norm_w, w_qkvz, w_sc, w_gate, conv_w, sc_conv_w, alog_vec, dtb_vec, seq_len, mod_spec):
    t = x.shape[0]
    tile = TOKEN_TILE
    per = tile // HALO
    last = t // HALO - 1
    const = lambda i: (0, 0)
    tok = lambda n: pl.BlockSpec((tile, n), lambda i: (i, 0))
    return pl.pallas_call(
        functools.partial(_inproj_even_kernel, seq_len=seq_len),
        grid=(t // tile,),
        in_specs=[
            pl.BlockSpec((HALO, D_MODEL), lambda i: (jnp.maximum(i * per - 1, 0), 0)),
            pl.BlockSpec((tile, D_MODEL), lambda i: (i, 0)),
            pl.BlockSpec((HALO, D_MODEL), lambda i: (jnp.minimum((i + 1) * per, last), 0)),
            mod_spec,
            pl.BlockSpec((1, D_MODEL), const),
            pl.BlockSpec(w_qkvz.shape, const), pl.BlockSpec(w_sc.shape, const), pl.BlockSpec(w_gate.shape, const),
            pl.BlockSpec(conv_w.shape, const), pl.BlockSpec(sc_conv_w.shape, const),
            pl.BlockSpec((1, 128), const), pl.BlockSpec((1, 128), const),
        ],
        out_specs=[tok(GDN_QKV_W), tok(GDN_HEADS * GDN_D), tok(SC_WIDTH), tok(128)],
        out_shape=[jax.ShapeDtypeStruct((t, GDN_QKV_W), BF16), jax.ShapeDtypeStruct((t, GDN_HEADS * GDN_D), BF16),
                   jax.ShapeDtypeStruct((t, SC_WIDTH), BF16), jax.ShapeDtypeStruct((t, 128), F32)],
        compiler_params=pltpu.CompilerParams(dimension_semantics=("parallel",), vmem_limit_bytes=VMEM_LIMIT),
        name="inproj_even",
    )(x, x, x, mods, norm_w.reshape(1, D_MODEL), w_qkvz, w_sc, w_gate, conv_w, sc_conv_w, alog_vec, dtb_vec)


def _mlp_kernel(a_ref, b_ref, x_ref, mod_ref, nw_ref, woa_ref, wob_ref, w1_ref, w2_ref, nf_ref, o_ref, *, final):
    y = _mm(a_ref[...], woa_ref[...], prec="bf") + _mm(b_ref[...], wob_ref[...], prec="bf")
    x1 = x_ref[...] + mod_ref[2:3, :] * y
    h = _rms(x1, nw_ref[...])
    h = (h * (1.0 + mod_ref[4:5, :]) + mod_ref[3:4, :]).astype(BF16)
    acc = jnp.zeros(x1.shape, F32)
    for j in range(D_FF // D_MODEL):
        cols = slice(j * D_MODEL, (j + 1) * D_MODEL)
        u = jnp.maximum(_mm(h, w1_ref[:, cols], prec="bf"), 0.0)
        acc = acc + _mm(u * u, w2_ref[cols, :], prec="bf")
    x2 = x1 + mod_ref[5:6, :] * acc
    if final:
        x2 = _rms(x2, nf_ref[...])
    o_ref[...] = x2


def _outproj_mlp(a, b, x, mods, norm_w, w_out, w1_all, w2_all, layer, norm_final, mod_spec, final):
    t = x.shape[0]
    half = a.shape[1]
    const = lambda i: (0, 0)
    return pl.pallas_call(
        functools.partial(_mlp_kernel, final=final),
        grid=(t // TOKEN_TILE,),
        in_specs=[
            pl.BlockSpec((TOKEN_TILE, half), lambda i: (i, 0)),
            pl.BlockSpec((TOKEN_TILE, half), lambda i: (i, 0)),
            pl.BlockSpec((TOKEN_TILE, D_MODEL), lambda i: (i, 0)),
            mod_spec,
            pl.BlockSpec((1, D_MODEL), const),
            pl.BlockSpec((half, D_MODEL), const),
            pl.BlockSpec((half, D_MODEL), lambda i: (1, 0)),
            pl.BlockSpec((None, D_MODEL, D_FF), lambda i: (layer, 0, 0)),
            pl.BlockSpec((None, D_FF, D_MODEL), lambda i: (layer, 0, 0)),
            pl.BlockSpec((1, D_MODEL), const),
        ],
        out_specs=pl.BlockSpec((TOKEN_TILE, D_MODEL), lambda i: (i, 0)),
        out_shape=jax.ShapeDtypeStruct((t, D_MODEL), F32),
        compiler_params=pltpu.CompilerParams(dimension_semantics=("parallel",), vmem_limit_bytes=VMEM_LIMIT),
        name="outproj_mlp",
    )(a, b, x, mods, norm_w.reshape(1, D_MODEL), w_out, w_out, w1_all, w2_all, norm_final.reshape(1, D_MODEL))


def _gdn_decay_terms(g, rev):
    c = g.shape[0]
    incl = _tri_masks(c, rev)[0]
    before_col = _tri_masks(c, not rev)[0]
    eye = lax.broadcasted_iota(jnp.int32, (c, c), 0) == lax.broadcasted_iota(jnp.int32, (c, c), 1)
    gc_row = jnp.sum(jnp.where(before_col, jnp.broadcast_to(g, (c, c)), 0.0), axis=0, keepdims=True)
    gc_col = jnp.sum(jnp.where(eye, jnp.broadcast_to(gc_row, (c, c)), 0.0), axis=1, keepdims=True)
    decay = jnp.where(incl, jnp.exp(jnp.where(incl, gc_col - gc_row, 0.0)), 0.0)
    g_tot = jnp.sum(g, axis=0, keepdims=True)
    return decay, jnp.exp(gc_col), jnp.exp(g_tot - gc_col), jnp.exp(g_tot)


def _gdn_kernel(*refs, seq_len, seqs, has_init):
    q_ref, k_ref, v_ref, gz_ref, gate_ref, gn_ref = refs[:6]
    pos = 6
    if has_init:
        s0f_ref, s0b_ref = refs[pos:pos + 2]
        pos += 2
    o_ref, sf_ref, sb_ref = refs[pos:pos + 3]
    osum, u_s, w_s, qd_s, kd_s, in_s, ge_s, st = refs[pos + 3:]
    head = pl.program_id(1)
    ch = GDN_CHUNK
    n_chunks = seq_len // ch
    osum[...] = jnp.zeros(osum.shape, F32)

    lane = lax.broadcasted_iota(jnp.int32, (ch, 128), 1)

    def pick(rows, col):
        return jnp.sum(jnp.where(lane == col, gate_ref[rows, :], 0.0), axis=1, keepdims=True)

    def solve_group(gi, carry):
        items = []
        for j in range(GDN_GROUP):
            c = gi * GDN_GROUP + j
            rows = pl.ds(pl.multiple_of(c * ch, ch), ch)
            items.append(dict(c=c, rows=rows, q=q_ref[rows, :].astype(F32), k=k_ref[rows, :].astype(F32),
                              v=v_ref[rows, :].astype(F32)))
        kks = [_mm_nt(it["k"], it["k"], "bf") for it in items]
        qks = [_mm_nt(it["q"], it["k"], "bf") for it in items]
        subs = []
        for it, kk, qk in zip(items, kks, qks):
            for d in range(2):
                beta = pick(it["rows"], d * GDN_HEADS + head)
                g = pick(it["rows"], 2 * GDN_HEADS + d * GDN_HEADS + head)
                decay, e_gc, e_rest, e_tot = _gdn_decay_terms(g, rev=(d == 1))
                strict = _tri_masks(ch, d == 1)[1]
                subs.append(dict(
                    d=d, c=it["c"], rows=it["rows"],
                    a=jnp.where(strict, kk * beta * decay, 0.0),
                    rhs=jnp.concatenate([it["v"] * beta, it["k"] * (beta * e_gc)], axis=1),
                    intra=qk * decay, qd=it["q"] * e_gc, kd=it["k"] * e_rest, ge=e_tot))
        ts = _unit_tri_inv_many([s["a"] for s in subs], "bf")
        uws = [_mm(t, s["rhs"], "bf") for t, s in zip(ts, subs)]
        for s, uw in zip(subs, uws):
            d, rows = s["d"], s["rows"]
            u_s[d, rows, :] = uw[:, :GDN_D]
            w_s[d, rows, :] = uw[:, GDN_D:].astype(BF16)
            qd_s[d, rows, :] = s["qd"].astype(BF16)
            kd_s[d, rows, :] = s["kd"].astype(BF16)
            in_s[d, rows, :] = s["intra"].astype(BF16)
            ge_s[d, pl.ds(pl.multiple_of(s["c"] * 8, 8), 8), :] = jnp.broadcast_to(s["ge"], (8, 128))
        return carry

    lax.fori_loop(0, seqs * n_chunks // GDN_GROUP, solve_group, 0)

    for j in range(seqs):
        st[0, j] = s0f_ref[j] if has_init else jnp.zeros((GDN_D, GDN_D), F32)
        st[1, j] = s0b_ref[j] if has_init else jnp.zeros((GDN_D, GDN_D), F32)

    def recur(i, carry):
        cs = []
        for j in range(seqs):
            for d in range(2):
                c = j * n_chunks + (i if d == 0 else n_chunks - 1 - i)
                cs.append(dict(d=d, j=j, rows=pl.ds(pl.multiple_of(c * ch, ch), ch),
                               ge=ge_s[d, pl.ds(pl.multiple_of(c * 8, 8), 1), :], s=st[d, j]))
        sbs = [c["s"].astype(BF16) for c in cs]
        wss = [_mm(w_s[c["d"], c["rows"], :], sb, "bf") for c, sb in zip(cs, sbs)]
        qss = [_mm(qd_s[c["d"], c["rows"], :], sb, "bf") for c, sb in zip(cs, sbs)]
        ebs = [(u_s[c["d"], c["rows"], :] - ws).astype(BF16) for c, ws in zip(cs, wss)]
        outs = [qs_ + _mm(in_s[c["d"], c["rows"], :], eb, "bf") for c, qs_, eb in zip(cs, qss, ebs)]
        s_news = [c["s"] * c["ge"] + _mm_tn(kd_s[c["d"], c["rows"], :], eb, "bf") for c, eb in zip(cs, ebs)]
        for c, o, s_new in zip(cs, outs, s_news):
            st[c["d"], c["j"]] = s_new
            osum[c["rows"], :] += o
        return carry

    lax.fori_loop(0, n_chunks, recur, 0)
    sf_ref[...] = st[0]
    sb_ref[...] = st[1]
    o_ref[...] = _rms(osum[...], gn_ref[...]) * gz_ref[...].astype(F32)


def _gdn_mixer(qkv, gz, gates, n_seq, seq_len, gdn_norm, s0_f, s0_b):
    has_init = s0_f is not None
    hd = GDN_HEADS
    seqs = max(1, GDN_BLOCK_ROWS // seq_len)
    rows = seqs * seq_len
    assert n_seq % seqs == 0 and (rows // GDN_CHUNK) % GDN_GROUP == 0

    def col(block):
        return pl.BlockSpec((rows, 128), lambda s, h, b=block: (s, b * hd + h))

    state = pl.BlockSpec((seqs, None, None, GDN_D, GDN_D), lambda s, h: (s, 0, h, 0, 0))
    in_specs = [col(0), col(1), col(2), col(0), pl.BlockSpec((rows, 128), lambda s, h: (s, 0)),
                pl.BlockSpec((1, 128), lambda s, h: (0, 0))]
    args = [qkv, qkv, qkv, gz, gates, gdn_norm.reshape(1, 128)]
    if has_init:
        in_specs += [state, state]
        args += [s0_f, s0_b]
    t = n_seq * seq_len
    scratch = ([pltpu.VMEM((rows, 128), F32)]
               + [pltpu.VMEM((2, rows, GDN_D), F32)]
               + [pltpu.VMEM((2, rows, GDN_D), BF16) for _ in range(3)]
               + [pltpu.VMEM((2, rows, GDN_CHUNK), BF16),
                  pltpu.VMEM((2, rows // GDN_CHUNK * 8, 128), F32),
                  pltpu.VMEM((2, seqs, GDN_D, GDN_D), F32)])
    return pl.pallas_call(
        functools.partial(_gdn_kernel, seq_len=seq_len, seqs=seqs, has_init=has_init),
        grid=(n_seq // seqs, hd),
        in_specs=in_specs,
        out_specs=[pl.BlockSpec((rows, 128), lambda s, h: (s, h)), state, state],
        out_shape=[jax.ShapeDtypeStruct((t, hd * GDN_D), F32),
                   jax.ShapeDtypeStruct((n_seq, 1, hd, GDN_D, GDN_D), F32),
                   jax.ShapeDtypeStruct((n_seq, 1, hd, GDN_D, GDN_D), F32)],
        scratch_shapes=scratch,
        compiler_params=pltpu.CompilerParams(dimension_semantics=("parallel", "parallel"),
                                             vmem_limit_bytes=VMEM_LIMIT),
        name="gdn_mixer",
    )(*args)


def _softmax_pv(scores, values, sink):
    m = sink
    for s in scores:
        m = jnp.maximum(m, jnp.max(s, axis=-1, keepdims=True))
    den = jnp.exp(sink - m)
    acc = None
    for s, v in zip(scores, values):
        e = jnp.exp(s - m)
        den = den + jnp.sum(e, axis=-1, keepdims=True)
        pv = _mm(e, v, prec="bf")
        acc = pv if acc is None else acc + pv
    return acc / den


def _group_sinks(sink_ref, j, rows):
    assert rows & (rows - 1) == 0
    grp = lax.broadcasted_iota(jnp.int32, (ATT_GROUP * rows, 1), 0) >> (rows.bit_length() - 1)
    col = jnp.full(grp.shape, sink_ref[j * ATT_GROUP], F32)
    for gi in range(1, ATT_GROUP):
        col = jnp.where(grp == gi, sink_ref[j * ATT_GROUP + gi], col)
    return col


def _store_group(o_ref, j, o, rows):
    for gi in range(ATT_GROUP):
        hh = j * ATT_GROUP + gi
        o_ref[:, hh * ATT_HD:(hh + 1) * ATT_HD] = o[gi * rows:(gi + 1) * rows]


def _attn_ctx_kernel(sink_ref, p_ref, o_ref, kc_ref, vc_ref):
    scale = ATT_HD ** -0.5
    rows = p_ref.shape[0]
    for j in range(ATT_KV_HEADS):
        k = p_ref[:, ATT_Q_W + j * ATT_HD:ATT_Q_W + (j + 1) * ATT_HD]
        v = p_ref[:, ATT_Q_W + ATT_KV_W + j * ATT_HD:ATT_Q_W + ATT_KV_W + (j + 1) * ATT_HD]
        kc_ref[j] = k
        vc_ref[j] = v
        q = jnp.concatenate([p_ref[:, hh * ATT_HD:(hh + 1) * ATT_HD]
                             for hh in range(j * ATT_GROUP, (j + 1) * ATT_GROUP)], axis=0)
        s = _mm_nt(q, k, prec="bf") * scale
        _store_group(o_ref, j, _softmax_pv([s], [v], _group_sinks(sink_ref, j, rows)), rows)


def _attn_context(proj_att, sink, n_seq, seq_len):
    kv = pl.BlockSpec((None, None, ATT_KV_HEADS, seq_len, ATT_HD), lambda b: (b, 0, 0, 0, 0))
    return pl.pallas_call(
        _attn_ctx_kernel,
        grid=(n_seq,),
        in_specs=[pl.BlockSpec(memory_space=pltpu.SMEM),
                  pl.BlockSpec((seq_len, ATT_W), lambda b: (b, 0))],
        out_specs=[pl.BlockSpec((seq_len, ATT_Q_W), lambda b: (b, 0)), kv, kv],
        out_shape=[jax.ShapeDtypeStruct((n_seq * seq_len, ATT_Q_W), F32),
                   jax.ShapeDtypeStruct((n_seq, 1, ATT_KV_HEADS, seq_len, ATT_HD), F32),
                   jax.ShapeDtypeStruct((n_seq, 1, ATT_KV_HEADS, seq_len, ATT_HD), F32)],
        compiler_params=pltpu.CompilerParams(dimension_semantics=("parallel",)),
        name="attn_context",
    )(sink, proj_att)


def _rope_tables(seq_len):
    pos = np.arange(seq_len)
    half = ATT_HD // 2
    inv = ROPE_BASE ** (-np.arange(0, half, 2, dtype=np.float32) / half)
    ang_r = (pos // GRID_W).astype(np.float32)[:, None] * inv
    ang_c = (pos % GRID_W).astype(np.float32)[:, None] * inv
    cos = np.concatenate([np.cos(ang_r), np.cos(ang_r), np.cos(ang_c), np.cos(ang_c)], axis=1)
    sin = np.concatenate([-np.sin(ang_r), np.sin(ang_r), -np.sin(ang_c), np.sin(ang_c)], axis=1)
    return (jnp.asarray(np.tile(cos, (1, 2)), F32), jnp.asarray(np.tile(sin, (1, 2)), F32))


def _rope(x, cos, sin):
    lane = lax.broadcasted_iota(jnp.int32, x.shape, 1)
    partner = jnp.where((lane & 31) < 16, pltpu.roll(x, 128 - 16, 1), pltpu.roll(x, 16, 1))
    return x * cos + partner * sin


def _attn_lat_kernel(sink_ref, p_ref, ck_ref, cv_ref, cos_ref, sin_ref, o_ref, *, seq_len):
    scale = ATT_HD ** -0.5
    qb = pl.program_id(1)
    span = 3 * ATT_BLOCK
    q0 = pl.multiple_of(qb * ATT_BLOCK, ATT_BLOCK)
    k0 = pl.multiple_of(jnp.clip((qb - 1) * ATT_BLOCK, 0, seq_len - span), ATT_BLOCK)
    qrows = pl.ds(q0, ATT_BLOCK)
    krows = pl.ds(k0, span)
    kwin = _rope(p_ref[krows, ATT_Q_W:ATT_Q_W + ATT_KV_W], cos_ref[krows, :], sin_ref[krows, :])
    vwin = p_ref[krows, ATT_Q_W + ATT_KV_W:ATT_W]
    stacked = ATT_GROUP * ATT_BLOCK
    qpos = q0 + (lax.broadcasted_iota(jnp.int32, (stacked, span), 0) & (ATT_BLOCK - 1))
    kpos = k0 + lax.broadcasted_iota(jnp.int32, (stacked, span), 1)
    valid = jnp.abs(qpos - kpos) <= WINDOW
    cos_q = cos_ref[qrows, :]
    sin_q = sin_ref[qrows, :]
    heads = []
    for pair in range(ATT_HEADS // 2):
        qpair = _rope(p_ref[qrows, pair * 128:(pair + 1) * 128], cos_q, sin_q)
        heads += [qpair[:, :ATT_HD], qpair[:, ATT_HD:]]
    kv = range(ATT_KV_HEADS)
    qs = [jnp.concatenate(heads[j * ATT_GROUP:(j + 1) * ATT_GROUP], axis=0) for j in kv]
    s_locs = [jnp.where(valid, _mm_nt(qs[j], kwin[:, j * ATT_HD:(j + 1) * ATT_HD], prec="bf") * scale, NEG_INF)
              for j in kv]
    s_ctxs = [_mm_nt(qs[j], ck_ref[j], prec="bf") * scale for j in kv]
    outs = [_softmax_pv([s_locs[j], s_ctxs[j]], [vwin[:, j * ATT_HD:(j + 1) * ATT_HD], cv_ref[j]],
                        _group_sinks(sink_ref, j, ATT_BLOCK)) for j in kv]
    for j in kv:
        _store_group(o_ref, j, outs[j], ATT_BLOCK)


def _attn_latent(proj_att, sink, cache_k, cache_v, layer, n_seq, seq_len):
    cos, sin = _rope_tables(seq_len)
    past = cache_k.shape[3]
    nqb = seq_len // ATT_BLOCK
    cache = pl.BlockSpec((None, None, ATT_KV_HEADS, past, ATT_HD), lambda b, q: (b, layer, 0, 0, 0))
    table = pl.BlockSpec((seq_len, 128), lambda b, q: (0, 0))
    return pl.pallas_call(
        functools.partial(_attn_lat_kernel, seq_len=seq_len),
        grid=(n_seq, nqb),
        in_specs=[pl.BlockSpec(memory_space=pltpu.SMEM),
                  pl.BlockSpec((seq_len, ATT_W), lambda b, q: (b, 0)),
                  cache, cache, table, table],
        out_specs=pl.BlockSpec((ATT_BLOCK, ATT_Q_W), lambda b, q: (b * nqb + q, 0)),
        out_shape=jax.ShapeDtypeStruct((n_seq * seq_len, ATT_Q_W), F32),
        compiler_params=pltpu.CompilerParams(dimension_semantics=("parallel", "parallel")),
        name="attn_latent",
    )(sink, proj_att, cache_k, cache_v, cos, sin)


def _pair_masks(n, rev):
    r = lax.broadcasted_iota(jnp.int32, (n, 2 * n), 0)
    c = lax.broadcasted_iota(jnp.int32, (n, 2 * n), 1) & (n - 1)
    return (r <= c, r < c) if rev else (r >= c, r > c)


def _bd(x):
    half = x.shape[1] // 2
    lane = lax.broadcasted_iota(jnp.int32, x.shape, 1)
    zero = jnp.zeros_like(x)
    return jnp.concatenate([jnp.where(lane < half, x, zero), jnp.where(lane >= half, x, zero)], axis=0)


def _bd_mask(n):
    r = lax.broadcasted_iota(jnp.int32, (2 * n, 2 * n), 0)
    c = lax.broadcasted_iota(jnp.int32, (2 * n, 2 * n), 1)
    return (r < n) == (c < n)


def _head_sums(x):
    ones = jnp.where(_bd_mask(RWKV_HD), 1.0, 0.0).astype(BF16)
    return _mm(x, ones, "x2l")


def _apply_pairs(tb, x):
    hi, lo = _split_bf16(x)
    return _mm(tb, _bd(hi), "bf") + _mm(tb, _bd(lo), "bf")


def _unit_tri_inv_pairs(mats):
    n = mats[0].shape[0]
    r = lax.broadcasted_iota(jnp.int32, (n, 2 * n), 0)
    c = lax.broadcasted_iota(jnp.int32, (n, 2 * n), 1) & (n - 1)
    eye = jnp.where(r == c, 1.0, 0.0)
    ts = [eye - a * jnp.where((r >> 1) == (c >> 1), 1.0, 0.0) for a in mats]
    abs_ = [a.astype(BF16) for a in mats]
    shift = 1
    while (1 << shift) < n:
        join = jnp.where(((r >> (shift + 1)) == (c >> (shift + 1))) & ((r >> shift) != (c >> shift)),
                         1.0, 0.0).astype(BF16)
        tbs = [t.astype(BF16) for t in ts]
        inner = [_mm(ab * join, _bd(tb), "bf") for ab, tb in zip(abs_, tbs)]
        ts = [t - _mm(tb, _bd(w.astype(BF16)), "bf") for t, tb, w in zip(ts, tbs, inner)]
        shift += 1
    return ts


def _rwkv_chunk_operands(x_ref, prm, sc, c, *, seq_len):
    (mu_ref, w0_ref, wup_ref, a0_ref, aup_ref, gup_ref, kk_ref, ka_ref, rk_ref) = prm
    ch = RWKV_CHUNK
    total = x_ref.shape[0]
    r0 = pl.multiple_of(c * ch, ch)
    rows = pl.ds(r0, ch)
    pos0 = r0 & (seq_len - 1)
    x = x_ref[rows, :]
    prev_row = x_ref[pl.ds(jnp.maximum(r0 - 1, 0), 1), :] * jnp.where(pos0 > 0, 1.0, 0.0)
    next_row = x_ref[pl.ds(jnp.minimum(r0 + ch, total - 1), 1), :] * jnp.where(pos0 + ch < seq_len, 1.0, 0.0)
    xp, xn = _shifted_rows(x, prev_row, next_row)
    xs = x + mu_ref[0:1, :] * (xp - x) + mu_ref[1:2, :] * (xn - x)
    r = xs[:, 0:RWKV_W]
    k = xs[:, RWKV_W:2 * RWKV_W]
    v = xs[:, 2 * RWKV_W:3 * RWKV_W]
    lo = 3 * RWKV_W
    p = RWKV_PREC
    gl = xs[:, lo + 4 * RWKV_LORA:lo + 6 * RWKV_LORA]
    sc["gate"][rows, :] = _mm(jax.nn.sigmoid(gl), gup_ref[...], p["gate"])
    sc["v"][rows, :] = v.astype(BF16)
    pairs = [slice(i * PAIR_W, (i + 1) * PAIR_W) for i in range(N_PAIRS)]
    kkv = k * kk_ref[...]
    kaps = []
    for cols in pairs:
        kk_p = kkv[:, cols]
        kaps.append(kk_p * lax.rsqrt(_head_sums(kk_p * kk_p) + 1e-6))
    items = []
    bonus = None
    for d in range(2):
        rev = d == 1
        wl = xs[:, lo + d * RWKV_LORA:lo + (d + 1) * RWKV_LORA]
        al = xs[:, lo + 2 * RWKV_LORA + d * RWKV_LORA:lo + 2 * RWKV_LORA + (d + 1) * RWKV_LORA]
        lw = -DECAY_SCALE * jax.nn.sigmoid(w0_ref[d:d + 1, :] + _mm(jnp.tanh(wl), wup_ref[d], p["lora"]))
        a = jax.nn.sigmoid(a0_ref[d:d + 1, :] + _mm(al, aup_ref[d], p["lora"]))
        k2 = k * (1.0 + (a - 1.0) * ka_ref[...])
        g_inc = _mm(jnp.where(_tri_masks(ch, rev)[0], 1.0, 0.0), lw, p["cumsum"])
        g_tot = jnp.sum(lw, axis=0, keepdims=True)
        e_neg = jnp.exp(-g_inc)
        e_end = jnp.exp(g_tot - g_inc)
        e_exc = jnp.exp(g_inc - lw)
        r_dec = r * jnp.exp(g_inc)
        k_neg = k2 * e_neg
        k_end = k2 * e_end
        sc["dec"][d, pl.ds(pl.multiple_of(c * 8, 8), 8), :] = jnp.broadcast_to(jnp.exp(g_tot), (8, RWKV_W))
        rkr = r * k2 * rk_ref[...]
        bon_d = jnp.concatenate([_head_sums(rkr[:, cols]) for cols in pairs], axis=1) * v
        bonus = bon_d if bonus is None else bonus + bon_d
        for cols, kap in zip(pairs, kaps):
            b_p = kap * a[:, cols]
            items.append(dict(d=d, rows=rows, cols=cols, kap_dec=kap * e_exc[:, cols], r_dec=r_dec[:, cols],
                              b_neg=b_p * e_neg[:, cols], k_neg=k_neg[:, cols],
                              b_end=b_p * e_end[:, cols], k_end=k_end[:, cols], vb=v[:, cols].astype(BF16)))
    sc["bon"][rows, :] = bonus
    return items


def _rwkv_solve(items, sc):
    ch = RWKV_CHUNK
    masks = [_pair_masks(ch, False), _pair_masks(ch, True)]
    ms = [_mm_nt(jnp.concatenate([it["kap_dec"], it["r_dec"]], axis=0),
                 jnp.concatenate([_bd(it["b_neg"].astype(BF16)), _bd(it["k_neg"].astype(BF16))], axis=0), "bf")
          for it in items]
    a_abs = [jnp.where(masks[it["d"]][1], m[:ch, :2 * ch], 0.0) for it, m in zip(items, ms)]
    a_aks = [jnp.where(masks[it["d"]][1], m[:ch, 2 * ch:], 0.0) for it, m in zip(items, ms)]
    a_rbs = [jnp.where(masks[it["d"]][0], m[ch:, :2 * ch], 0.0) for it, m in zip(items, ms)]
    a_rks = [jnp.where(masks[it["d"]][0], m[ch:, 2 * ch:], 0.0) for it, m in zip(items, ms)]
    tbs = [t.astype(BF16) for t in _unit_tri_inv_pairs(a_abs)]
    akvs = [_mm(a_ak, _bd(it["vb"]), "bf") for it, a_ak in zip(items, a_aks)]
    w2s = [_apply_pairs(tb, it["kap_dec"]) for it, tb in zip(items, tbs)]
    u0s = [_apply_pairs(tb, akv) for tb, akv in zip(tbs, akvs)]
    for it, w2, u0, a_rb, a_rk in zip(items, w2s, u0s, a_rbs, a_rks):
        d, rows, cols = it["d"], it["rows"], it["cols"]
        sc["w2"][d, rows, cols] = w2.astype(BF16)
        sc["rd"][d, rows, cols] = it["r_dec"].astype(BF16)
        sc["u0"][d, rows, cols] = u0
        sc["arb"][d, rows, cols] = a_rb.astype(BF16)
        sc["ark"][d, rows, cols] = a_rk.astype(BF16)
        sc["bh"][d, rows, cols] = it["b_end"].astype(BF16)
        sc["kh"][d, rows, cols] = it["k_end"].astype(BF16)


def _rwkv_recur(sc, st, i, *, seq_len, seqs):
    ch = RWKV_CHUNK
    n_chunks = seq_len // ch
    keep = _bd_mask(RWKV_HD)
    cs = []
    for j in range(seqs):
        for d in range(2):
            c = j * n_chunks + (i if d == 0 else n_chunks - 1 - i)
            rows = pl.ds(pl.multiple_of(c * ch, ch), ch)
            dec = sc["dec"][d, pl.ds(pl.multiple_of(c * 8, 8), 1), :]
            for pi in range(N_PAIRS):
                cols = slice(pi * PAIR_W, (pi + 1) * PAIR_W)
                cs.append(dict(d=d, j=j, p=pi, rows=rows, cols=cols, dec=dec[:, cols], s=st[d, j, pi]))
    sbs = [c["s"].astype(BF16) for c in cs]
    lss = [_mm_nt(jnp.concatenate([sc["w2"][c["d"], c["rows"], c["cols"]],
                                   sc["rd"][c["d"], c["rows"], c["cols"]]], axis=0), sb, "bf")
           for c, sb in zip(cs, sbs)]
    ubs = [(-(ls[:ch] + sc["u0"][c["d"], c["rows"], c["cols"]])).astype(BF16) for c, ls in zip(cs, lss)]
    vbs = [sc["v"][c["rows"], c["cols"]] for c in cs]
    ys = [ls[ch:] + _mm(jnp.concatenate([sc["arb"][c["d"], c["rows"], c["cols"]],
                                         sc["ark"][c["d"], c["rows"], c["cols"]]], axis=1),
                        jnp.concatenate([_bd(ub), _bd(vb)], axis=0), "bf")
          for c, ls, ub, vb in zip(cs, lss, ubs, vbs)]
    ups = [_mm_tn(jnp.concatenate([ub, vb], axis=0),
                  jnp.concatenate([sc["bh"][c["d"], c["rows"], c["cols"]],
                                   sc["kh"][c["d"], c["rows"], c["cols"]]], axis=0), "bf")
           for c, ub, vb in zip(cs, ubs, vbs)]
    for c, y, up in zip(cs, ys, ups):
        st[c["d"], c["j"], c["p"]] = c["s"] * c["dec"] + jnp.where(keep, up, 0.0)
        sc["ysum"][c["rows"], c["cols"]] += y


RWKV_SCRATCH = ("ysum", "bon", "gate", "v", "w2", "rd", "u0", "arb", "ark", "bh", "kh", "dec")


def _rwkv_kernel(*refs, seq_len, seqs, has_init):
    x_ref = refs[0]
    prm = refs[1:10]
    lnw_ref, lnb_ref = refs[10:12]
    pos = 12
    if has_init:
        s0f_ref, s0b_ref = refs[pos:pos + 2]
        pos += 2
    o_ref, sf_ref, sb_ref = refs[pos:pos + 3]
    sc = dict(zip(RWKV_SCRATCH, refs[pos + 3:]))
    st = refs[pos + 3 + len(RWKV_SCRATCH)]
    ch = RWKV_CHUNK
    hd = RWKV_HD
    n_chunks = seq_len // ch
    sc["ysum"][...] = jnp.zeros(sc["ysum"].shape, F32)

    def prepare(gi, carry):
        items = []
        for j in range(RWKV_GROUP):
            items += _rwkv_chunk_operands(x_ref, prm, sc, gi * RWKV_GROUP + j, seq_len=seq_len)
        _rwkv_solve(items, sc)
        return carry

    lax.fori_loop(0, seqs * n_chunks // RWKV_GROUP, prepare, 0)

    zero = jnp.zeros((hd, hd), F32)
    for d, s0_ref in enumerate((s0f_ref, s0b_ref) if has_init else (None, None)):
        for j in range(seqs):
            for pi in range(N_PAIRS):
                s_a = s0_ref[j, 2 * pi] if has_init else zero
                s_b = s0_ref[j, 2 * pi + 1] if has_init else zero
                st[d, j, pi] = jnp.concatenate([jnp.concatenate([s_a, zero], axis=1),
                                                jnp.concatenate([zero, s_b], axis=1)], axis=0)

    def recur(i, carry):
        _rwkv_recur(sc, st, i, seq_len=seq_len, seqs=seqs)
        return carry

    lax.fori_loop(0, n_chunks, recur, 0)
    for d, out_ref in enumerate((sf_ref, sb_ref)):
        for j in range(seqs):
            for pi in range(N_PAIRS):
                s = st[d, j, pi]
                out_ref[j, 2 * pi] = s[:hd, :hd]
                out_ref[j, 2 * pi + 1] = s[hd:, hd:]

    tile = RWKV_FINISH_ROWS

    def finish(i, carry):
        rows = pl.ds(pl.multiple_of(i * tile, tile), tile)
        pairs = [slice(pi * PAIR_W, (pi + 1) * PAIR_W) for pi in range(N_PAIRS)]
        ys = [sc["ysum"][rows, cols] for cols in pairs]
        cens = [y - _head_sums(y) * (1.0 / hd) for y in ys]
        vars_ = [_head_sums(cen * cen) * (1.0 / hd) for cen in cens]
        for cols, cen, var in zip(pairs, cens, vars_):
            yn = cen * lax.rsqrt(var + GN_EPS) * lnw_ref[:, cols] + lnb_ref[:, cols]
            o_ref[rows, cols] = (yn + sc["bon"][rows, cols]) * sc["gate"][rows, cols]
        return carry

    lax.fori_loop(0, seqs * seq_len // tile, finish, 0)


def _rwkv_mixer(x_rw, n_seq, seq_len, params, s0_f, s0_b):
    has_init = s0_f is not None
    (mu, w0, w_up, a0, a_up, g_up, k_k, k_a, r_k, ln_w, ln_b) = params
    row = lambda a: a.reshape(1, RWKV_W)
    args = [x_rw, mu, w0, w_up, a0, a_up, g_up, row(k_k), row(k_a), row(r_k), row(ln_w), row(ln_b)]

    def whole(a):
        nd = a.ndim
        return pl.BlockSpec(a.shape, lambda s, nd=nd: (0,) * nd)

    seqs = max(1, RWKV_BLOCK_ROWS // seq_len)
    rows = seqs * seq_len
    assert n_seq % seqs == 0 and (rows // RWKV_CHUNK) % RWKV_GROUP == 0 and seq_len & (seq_len - 1) == 0
    in_specs = [pl.BlockSpec((rows, RWKV_IN), lambda s: (s, 0))] + [whole(a) for a in args[1:]]
    state = pl.BlockSpec((seqs, None, RWKV_HEADS, RWKV_HD, RWKV_HD), lambda s: (s, 0, 0, 0, 0))
    if has_init:
        in_specs += [state, state]
        args += [s0_f, s0_b]
    st_shape = jax.ShapeDtypeStruct((n_seq, 1, RWKV_HEADS, RWKV_HD, RWKV_HD), F32)
    tok = lambda dt: pltpu.VMEM((rows, RWKV_W), dt)
    per_dir = lambda dt: pltpu.VMEM((2, rows, RWKV_W), dt)
    scratch = dict(ysum=tok(F32), bon=tok(F32), gate=tok(F32), v=tok(BF16), w2=per_dir(BF16), rd=per_dir(BF16),
                   u0=per_dir(F32), arb=per_dir(BF16), ark=per_dir(BF16), bh=per_dir(BF16), kh=per_dir(BF16),
                   dec=pltpu.VMEM((2, rows // RWKV_CHUNK * 8, RWKV_W), F32))
    return pl.pallas_call(
        functools.partial(_rwkv_kernel, seq_len=seq_len, seqs=seqs, has_init=has_init),
        grid=(n_seq // seqs,),
        in_specs=in_specs,
        out_specs=[pl.BlockSpec((rows, RWKV_W), lambda s: (s, 0)), state, state],
        out_shape=[jax.ShapeDtypeStruct((n_seq * seq_len, RWKV_W), F32), st_shape, st_shape],
        scratch_shapes=[scratch[name] for name in RWKV_SCRATCH]
        + [pltpu.VMEM((2, seqs, N_PAIRS, PAIR_W, PAIR_W), F32)],
        compiler_params=pltpu.CompilerParams(dimension_semantics=("parallel",), vmem_limit_bytes=VMEM_LIMIT),
        name="rwkv_mixer",
    )(*args)


def kernel(x_prompt, x_sample, state_gdn_fwd, state_gdn_bwd, cache_attn_k, cache_attn_v, state_rwkv_fwd, state_rwkv_bwd, c, c_ctx, mod_w, mod_b, norm_mix, norm_mlp, mlp_w1, mlp_w2, norm_final, ev_w_in, ev_w_out, gdn_conv, gdn_a_log, gdn_dt_bias, gdn_norm, sc_conv, od_w_in, od_w_out, attn_sink, rwkv_mu, rwkv_w0, rwkv_w_up, rwkv_a0, rwkv_a_up, rwkv_g_up, rwkv_k_k, rwkv_k_a, rwkv_r_k, rwkv_ln_w, rwkv_ln_b):
    bp, lp, _ = x_prompt.shape
    bs, ls, _ = x_sample.shape
    depth = mod_w.shape[0]
    c_rows = jnp.concatenate([c_ctx[None, :], c, jnp.zeros((MOD_ROWS - 1 - bs, D_MODEL), F32)], axis=0)
    mods = _modulation(c_rows, mod_w, mod_b)

    assert ls % TOKEN_TILE == 0 and (bp * lp) % TOKEN_TILE == 0
    groups = [
        dict(x=x_prompt.reshape(bp * lp, D_MODEL), n=bp, l=lp, latent=False,
             mod=lambda layer: _mod_spec(layer, 1, 0, 0)),
        dict(x=x_sample.reshape(bs * ls, D_MODEL), n=bs, l=ls, latent=True,
             mod=lambda layer: _mod_spec(layer, ls // TOKEN_TILE, 1, 1)),
    ]
    outs = {}
    w1 = mlp_w1.astype(BF16)
    w2 = mlp_w2.astype(BF16)
    for layer in range(depth):
        final = layer == depth - 1
        if layer % 2 == 0:
            e = layer // 2
            w = ev_w_in[e]
            qkvz = GDN_QKV_W + GDN_HEADS * GDN_D
            n_gate = 4 * GDN_HEADS
            gate_cols = jnp.concatenate([w[:, qkvz:qkvz + n_gate],
                                         jnp.zeros((D_MODEL, EV_IN_PAD - w.shape[1]), F32)], axis=1)
            w_qkvz, w_sc, w_gate = w[:, :qkvz].astype(BF16), w[:, qkvz + n_gate:].astype(BF16), gate_cols.astype(BF16)
            w_out = ev_w_out[e].astype(BF16)
            alog_vec = jnp.zeros((1, 128), F32).at[0, 2 * GDN_HEADS:4 * GDN_HEADS].set(gdn_a_log[e].reshape(-1))
            dtb_vec = jnp.zeros((1, 128), F32).at[0, 2 * GDN_HEADS:4 * GDN_HEADS].set(gdn_dt_bias[e].reshape(-1))
            for grp in groups:
                qkv, gz, sc, gates = _inproj_even(grp["x"], mods, norm_mix[layer], w_qkvz, w_sc, w_gate, gdn_conv[e],
                                                  sc_conv[e], alog_vec, dtb_vec, grp["l"], grp["mod"](layer))
                s0 = (state_gdn_fwd[:, e:e + 1], state_gdn_bwd[:, e:e + 1]) if grp["latent"] else (None, None)
                o, s_f, s_b = _gdn_mixer(qkv, gz, gates, grp["n"], grp["l"], gdn_norm[e], *s0)
                if not grp["latent"]:
                    outs.setdefault("gdn_f", []).append(s_f)
                    outs.setdefault("gdn_b", []).append(s_b)
                grp["x"] = _outproj_mlp(o, sc, grp["x"], mods, norm_mlp[layer], w_out, w1, w2, layer, norm_final,
                                        grp["mod"](layer), final)
        else:
            o_ = layer // 2
            w = od_w_in[o_]
            w_in = [w.astype(BF16)]
            w_out = od_w_out[o_].astype(BF16)
            rw = (rwkv_mu[o_], rwkv_w0[o_], rwkv_w_up[o_], rwkv_a0[o_], rwkv_a_up[o_], rwkv_g_up[o_],
                  rwkv_k_k[o_], rwkv_k_a[o_], rwkv_r_k[o_].reshape(-1), rwkv_ln_w[o_], rwkv_ln_b[o_])
            for grp in groups:
                p_att, x_rw = _inproj(grp["x"], mods, norm_mix[layer], w_in, (ATT_W, RWKV_IN), grp["mod"](layer))
                if grp["latent"]:
                    att = _attn_latent(p_att, attn_sink[o_], cache_attn_k, cache_attn_v, o_, grp["n"], grp["l"])
                    rwo, _, _ = _rwkv_mixer(x_rw, grp["n"], grp["l"], rw,
                                            state_rwkv_fwd[:, o_:o_ + 1], state_rwkv_bwd[:, o_:o_ + 1])
                else:
                    att, kc, vc = _attn_context(p_att, attn_sink[o_], grp["n"], grp["l"])
                    rwo, s_f, s_b = _rwkv_mixer(x_rw, grp["n"], grp["l"], rw, None, None)
                    outs.setdefault("att_k", []).append(kc)
                    outs.setdefault("att_v", []).append(vc)
                    outs.setdefault("rw_f", []).append(s_f)
                    outs.setdefault("rw_b", []).append(s_b)
                grp["x"] = _outproj_mlp(att, rwo, grp["x"], mods, norm_mlp[layer], w_out, w1, w2, layer, norm_final,
                                        grp["mod"](layer), final)
    cat = lambda key: jnp.concatenate(outs[key], axis=1)
    return (groups[0]["x"].reshape(bp, lp, D_MODEL), groups[1]["x"].reshape(bs, ls, D_MODEL),
            cat("gdn_f"), cat("gdn_b"), cat("att_k"), cat("att_v"), cat("rw_f"), cat("rw_b"))
```

```python
import functools

import jax
import jax.numpy as jnp
import numpy as np
from jax import lax
from jax.experimental import pallas as pl
from jax.experimental.pallas import tpu as pltpu

F32 = jnp.float32
BF16 = jnp.bfloat16
HIGHEST = lax.Precision.HIGHEST

D_MODEL = 1024
N_MOD = 6
D_FF = 4 * D_MODEL
NORM_EPS = 1e-6
TOKEN_TILE = 512
HALO = 8
MOD_ROWS = 8

GDN_HEADS = 4
GDN_D = 128
GDN_CHUNK = 128
GDN_GROUP = 8
GDN_HEADS_PER_STEP = 2
GDN_BLOCK_ROWS = 1024
GDN_QKV_W = 3 * GDN_HEADS * GDN_D
SC_WIDTH = 512
EV_IN_PAD = 3712

ATT_HEADS = 8
ATT_KV_HEADS = 2
ATT_GROUP = ATT_HEADS // ATT_KV_HEADS
ATT_HD = 64
ATT_Q_W = ATT_HEADS * ATT_HD
ATT_KV_W = ATT_KV_HEADS * ATT_HD
ATT_W = ATT_Q_W + 2 * ATT_KV_W
WINDOW = 128
ATT_BLOCK = 128
GRID_W = 64
ROPE_BASE = 10000.0
NEG_INF = -1e30

RWKV_HEADS = 8
RWKV_HD = 64
RWKV_W = RWKV_HEADS * RWKV_HD
RWKV_LORA = 64
RWKV_IN = 3 * RWKV_W + 3 * 2 * RWKV_LORA
RWKV_CHUNK = 64
RWKV_GROUP = 4
RWKV_FINISH_ROWS = 256
DECAY_SCALE = float(np.exp(-0.5))
RWKV_BLOCK_ROWS = 1024
PAIR_W = 2 * RWKV_HD
N_PAIRS = RWKV_W // PAIR_W
GN_EPS = 64e-5

VMEM_LIMIT = 56 * 1024 * 1024

RWKV_PREC = dict(lora="bf", gate="bf", cumsum="x2r", amat="bf", inv="bf", state="bf", akv="bf", solve="x2r",
                 out="bf", update="bf")


def _split_bf16(a):
    hi = a.astype(BF16)
    return hi, (a - hi.astype(F32)).astype(BF16)


def _dot(a, b, dims, prec):
    dn = (dims, ((), ()))
    if prec == "hi":
        return lax.dot_general(a, b, dn, precision=HIGHEST, preferred_element_type=F32)
    one = lambda x, y: lax.dot_general(x, y, dn, preferred_element_type=F32)
    if prec == "x3":
        ah, al = _split_bf16(a)
        bh, bl = _split_bf16(b)
        return one(ah, bh) + one(ah, bl) + one(al, bh)
    if prec == "x2l":
        ah, al = _split_bf16(a)
        bh = b.astype(BF16)
        return one(ah, bh) + one(al, bh)
    if prec == "x2r":
        ah = a.astype(BF16)
        bh, bl = _split_bf16(b)
        return one(ah, bh) + one(ah, bl)
    assert prec == "bf", prec
    return lax.dot_general(a.astype(BF16), b.astype(BF16), dn, preferred_element_type=F32)


def _mm(a, b, prec="hi"):
    return _dot(a, b, ((1,), (0,)), prec)


def _mm_nt(a, b, prec="hi"):
    return _dot(a, b, ((1,), (1,)), prec)


def _mm_tn(a, b, prec="hi"):
    return _dot(a, b, ((0,), (0,)), prec)


def _silu(x):
    return x * jax.nn.sigmoid(x)


def _softplus(x):
    return jnp.maximum(x, 0.0) + jnp.log1p(jnp.exp(-jnp.abs(x)))


def _rms(x, w):
    return x * lax.rsqrt(jnp.mean(x * x, axis=-1, keepdims=True) + NORM_EPS) * w


def _tri_masks(n, rev):
    r = lax.broadcasted_iota(jnp.int32, (n, n), 0)
    c = lax.broadcasted_iota(jnp.int32, (n, n), 1)
    if rev:
        return r <= c, r < c
    return r >= c, r > c


def _unit_tri_inv_many(mats, prec):
    assert prec == "bf"
    n = mats[0].shape[0]
    r = lax.broadcasted_iota(jnp.int32, (n, n), 0)
    c = lax.broadcasted_iota(jnp.int32, (n, n), 1)
    eye = jnp.where(r == c, 1.0, 0.0)
    ts = [eye - a * jnp.where((r >> 1) == (c >> 1), 1.0, 0.0) for a in mats]
    abs_ = [a.astype(BF16) for a in mats]
    shift = 1
    while (1 << shift) < n:
        join = jnp.where(((r >> (shift + 1)) == (c >> (shift + 1))) & ((r >> shift) != (c >> shift)),
                         1.0, 0.0).astype(BF16)
        tbs = [t.astype(BF16) for t in ts]
        inner = [_mm(ab * join, tb, prec) for ab, tb in zip(abs_, tbs)]
        ts = [t - _mm(tb, w, prec) for t, tb, w in zip(ts, tbs, inner)]
        shift += 1
    return ts


def _shifted_rows(x, prev_row, next_row):
    n = x.shape[0]
    row = lax.broadcasted_iota(jnp.int32, x.shape, 0)
    xp = jnp.where(row == 0, prev_row, pltpu.roll(x, 1, 0))
    xn = jnp.where(row == n - 1, next_row, pltpu.roll(x, n - 1, 0))
    return xp, xn


def _conv3(x, w_ref, seq_len):
    n = x.shape[0]
    assert seq_len & (seq_len - 1) == 0 and n % seq_len == 0
    pos = lax.broadcasted_iota(jnp.int32, x.shape, 0) & (seq_len - 1)
    xp = jnp.where(pos == 0, 0.0, pltpu.roll(x, 1, 0))
    xn = jnp.where(pos == seq_len - 1, 0.0, pltpu.roll(x, n - 1, 0))
    return xp * w_ref[0:1, :] + x * w_ref[1:2, :] + xn * w_ref[2:3, :]


def _mod_kernel(c_ref, w_ref, b_ref, o_ref):
    s = _silu(c_ref[...])
    o_ref[...] = _mm(s, w_ref[...], prec="bf") + b_ref[...]


def _modulation(c_rows, mod_w, mod_b):
    depth = mod_w.shape[0]
    nblk = (N_MOD * D_MODEL) // D_MODEL
    out = pl.pallas_call(
        _mod_kernel,
        grid=(depth, nblk),
        in_specs=[
            pl.BlockSpec((MOD_ROWS, D_MODEL), lambda l, j: (0, 0)),
            pl.BlockSpec((None, D_MODEL, D_MODEL), lambda l, j: (l, 0, j)),
            pl.BlockSpec((None, 1, D_MODEL), lambda l, j: (l, 0, j)),
        ],
        out_specs=pl.BlockSpec((None, MOD_ROWS, D_MODEL), lambda l, j: (l, 0, j)),
        out_shape=jax.ShapeDtypeStruct((depth, MOD_ROWS, N_MOD * D_MODEL), F32),
        compiler_params=pltpu.CompilerParams(dimension_semantics=("parallel", "parallel")),
        name="modulation",
    )(c_rows, mod_w, mod_b.reshape(depth, 1, N_MOD * D_MODEL))
    return out.reshape(depth, MOD_ROWS, N_MOD, D_MODEL)


def _mod_spec(layer, tiles_per_seq, row_base, row_step):
    return pl.BlockSpec((None, None, N_MOD, D_MODEL),
                        lambda i: (layer, row_base + (i // tiles_per_seq) * row_step, 0, 0))


def _inproj_kernel(*refs, n_w):
    x_ref, mod_ref, nw_ref = refs[:3]
    w_refs = refs[3:3 + n_w]
    o_refs = refs[3 + n_w:]
    h = _rms(x_ref[...], nw_ref[...])
    h = (h * (1.0 + mod_ref[1:2, :]) + mod_ref[0:1, :]).astype(BF16)
    pieces = [_mm(h, w_ref[...], prec="bf") for w_ref in w_refs]
    y = pieces[0] if n_w == 1 else jnp.concatenate(pieces, axis=1)
    off = 0
    for o_ref in o_refs:
        n = o_ref.shape[-1]
        o_ref[...] = y[:, off:off + n]
        off += n


def _inproj(x, mods, norm_w, ws_bf16, splits, mod_spec):
    t = x.shape[0]
    assert sum(w.shape[1] for w in ws_bf16) == sum(splits) and all(w.shape[1] % 128 == 0 for w in ws_bf16)
    return pl.pallas_call(
        functools.partial(_inproj_kernel, n_w=len(ws_bf16)),
        grid=(t // TOKEN_TILE,),
        in_specs=[
            pl.BlockSpec((TOKEN_TILE, D_MODEL), lambda i: (i, 0)),
            mod_spec,
            pl.BlockSpec((1, D_MODEL), lambda i: (0, 0)),
        ] + [pl.BlockSpec(w.shape, lambda i: (0, 0)) for w in ws_bf16],
        out_specs=[pl.BlockSpec((TOKEN_TILE, n), lambda i: (i, 0)) for n in splits],
        out_shape=[jax.ShapeDtypeStruct((t, n), F32) for n in splits],
        compiler_params=pltpu.CompilerParams(dimension_semantics=("parallel",), vmem_limit_bytes=VMEM_LIMIT),
        name="inproj",
    )(x, mods, norm_w.reshape(1, D_MODEL), *ws_bf16)


def _inproj_even_kernel(xp_ref, x_ref, xn_ref, mod_ref, nw_ref, wqkvz_ref, wsc_ref, wgate_ref, cqkv_ref, csc_ref,
                        alog_ref, dtb_ref, qkv_ref, gz_ref, sc_ref, gate_ref, *, seq_len):
    tile = x_ref.shape[0]
    x = jnp.concatenate([xp_ref[...], x_ref[...], xn_ref[...]], axis=0)
    h = _rms(x, nw_ref[...])
    h = (h * (1.0 + mod_ref[1:2, :]) + mod_ref[0:1, :]).astype(BF16)
    n = tile + 2 * HALO
    first = pl.program_id(0) * tile - HALO
    pos = (first + lax.broadcasted_iota(jnp.int32, (n, 1), 0)) & (seq_len - 1)
    at_start = pos == 0
    at_end = pos == seq_len - 1

    def conv3(v, c_ref):
        vp = jnp.where(at_start, 0.0, pltpu.roll(v, 1, 0))
        vn = jnp.where(at_end, 0.0, pltpu.roll(v, n - 1, 0))
        return vp * c_ref[0:1, :] + v * c_ref[1:2, :] + vn * c_ref[2:3, :]

    keep = slice(HALO, HALO + tile)
    wide = 2 * GDN_D
    for j in range(GDN_QKV_W // wide):
        cols = slice(j * wide, (j + 1) * wide)
        act = _silu(conv3(_mm(h, wqkvz_ref[:, cols], "bf"), cqkv_ref.at[:, cols]))[keep]
        for i in range(2):
            part = act[:, i * GDN_D:(i + 1) * GDN_D]
            if j < 2 * GDN_HEADS // 2:
                part = part * lax.rsqrt(jnp.sum(part * part, axis=-1, keepdims=True) + 1e-6)
                if j < GDN_HEADS // 2:
                    part = part * (GDN_D ** -0.5)
            qkv_ref[:, j * wide + i * GDN_D:j * wide + (i + 1) * GDN_D] = part.astype(BF16)
    for j in range(GDN_HEADS * GDN_D // wide):
        cols = slice(j * wide, (j + 1) * wide)
        zcols = slice(GDN_QKV_W + j * wide, GDN_QKV_W + (j + 1) * wide)
        gz_ref[:, cols] = _silu(_mm(h, wqkvz_ref[:, zcols], "bf")[keep]).astype(BF16)
    for j in range(SC_WIDTH // wide):
        sc_b, sc_c, sc_h = [_mm(h, wsc_ref[:, i * SC_WIDTH + j * wide:i * SC_WIDTH + (j + 1) * wide], "bf")
                            for i in range(3)]
        cols = slice(j * wide, (j + 1) * wide)
        sc_ref[:, cols] = (sc_b * conv3(sc_c * sc_h, csc_ref.at[:, cols]))[keep].astype(BF16)
    g = _mm(h, wgate_ref[...], "bf")[keep]
    lane = lax.broadcasted_iota(jnp.int32, g.shape, 1)
    gate_ref[...] = jnp.where(lane < 2 * GDN_HEADS, jax.nn.sigmoid(g),
                              -jnp.exp(alog_ref[...]) * _softplus(g + dtb_ref[...]))


def _inproj_even(x, mods, norm_w, w_qkvz, w_sc, w_gate, conv_w, sc_conv_w, alog_vec, dtb_vec, seq_len, mod_spec):
    t = x.shape[0]
    tile = TOKEN_TILE
    per = tile // HALO
    last = t // HALO - 1
    const = lambda i: (0, 0)
    tok = lambda n: pl.BlockSpec((tile, n), lambda i: (i, 0))
    return pl.pallas_call(
        functools.partial(_inproj_even_kernel, seq_len=seq_len),
        grid=(t // tile,),
        in_specs=[
            pl.BlockSpec((HALO, D_MODEL), lambda i: (jnp.maximum(i * per - 1, 0), 0)),
            pl.BlockSpec((tile, D_MODEL), lambda i: (i, 0)),
            pl.BlockSpec((HALO, D_MODEL), lambda i: (jnp.minimum((i + 1) * per, last), 0)),
            mod_spec,
            pl.BlockSpec((1, D_MODEL), const),
            pl.BlockSpec(w_qkvz.shape, const), pl.BlockSpec(w_sc.shape, const), pl.BlockSpec(w_gate.shape, const),
            pl.BlockSpec(conv_w.shape, const), pl.BlockSpec(sc_conv_w.shape, const),
            pl.BlockSpec((1, 128), const), pl.BlockSpec((1, 128), const),
        ],
        out_specs=[tok(GDN_QKV_W), tok(GDN_HEADS * GDN_D), tok(SC_WIDTH), tok(128)],
        out_shape=[jax.ShapeDtypeStruct((t, GDN_QKV_W), BF16), jax.ShapeDtypeStruct((t, GDN_HEADS * GDN_D), BF16),
                   jax.ShapeDtypeStruct((t, SC_WIDTH), BF16), jax.ShapeDtypeStruct((t, 128), F32)],
        compiler_params=pltpu.CompilerParams(dimension_semantics=("parallel",), vmem_limit_bytes=VMEM_LIMIT),
        name="inproj_even",
    )(x, x, x, mods, norm_w.reshape(1, D_MODEL), w_qkvz, w_sc, w_gate, conv_w, sc_conv_w, alog_vec, dtb_vec)


def _mlp_kernel(a_ref, b_ref, x_ref, mod_ref, nw_ref, woa_ref, wob_ref, w1_ref, w2_ref, nf_ref, o_ref, *, final):
    y = _mm(a_ref[...], woa_ref[...], prec="bf") + _mm(b_ref[...], wob_ref[...], prec="bf")
    x1 = x_ref[...] + mod_ref[2:3, :] * y
    h = _rms(x1, nw_ref[...])
    h = (h * (1.0 + mod_ref[4:5, :]) + mod_ref[3:4, :]).astype(BF16)
    acc = jnp.zeros(x1.shape, F32)
    for j in range(D_FF // D_MODEL):
        cols = slice(j * D_MODEL, (j + 1) * D_MODEL)
        u = jnp.maximum(_mm(h, w1_ref[:, cols], prec="bf"), 0.0)
        acc = acc + _mm(u * u, w2_ref[cols, :], prec="bf")
    x2 = x1 + mod_ref[5:6, :] * acc
    if final:
        x2 = _rms(x2, nf_ref[...])
    o_ref[...] = x2


def _outproj_mlp(a, b, x, mods, norm_w, w_out, w1_all, w2_all, layer, norm_final, mod_spec, final):
    t = x.shape[0]
    half = a.shape[1]
    const = lambda i: (0, 0)
    return pl.pallas_call(
        functools.partial(_mlp_kernel, final=final),
        grid=(t // TOKEN_TILE,),
        in_specs=[
            pl.BlockSpec((TOKEN_TILE, half), lambda i: (i, 0)),
            pl.BlockSpec((TOKEN_TILE, half), lambda i: (i, 0)),
            pl.BlockSpec((TOKEN_TILE, D_MODEL), lambda i: (i, 0)),
            mod_spec,
            pl.BlockSpec((1, D_MODEL), const),
            pl.BlockSpec((half, D_MODEL), const),
            pl.BlockSpec((half, D_MODEL), lambda i: (1, 0)),
            pl.BlockSpec((None, D_MODEL, D_FF), lambda i: (layer, 0, 0)),
            pl.BlockSpec((None, D_FF, D_MODEL), lambda i: (layer, 0, 0)),
            pl.BlockSpec((1, D_MODEL), const),
        ],
        out_specs=pl.BlockSpec((TOKEN_TILE, D_MODEL), lambda i: (i, 0)),
        out_shape=jax.ShapeDtypeStruct((t, D_MODEL), F32),
        compiler_params=pltpu.CompilerParams(dimension_semantics=("parallel",), vmem_limit_bytes=VMEM_LIMIT),
        name="outproj_mlp",
    )(a, b, x, mods, norm_w.reshape(1, D_MODEL), w_out, w_out, w1_all, w2_all, norm_final.reshape(1, D_MODEL))


def _gdn_decay_terms(g, rev):
    c = g.shape[0]
    incl = _tri_masks(c, rev)[0]
    before_col = _tri_masks(c, not rev)[0]
    eye = lax.broadcasted_iota(jnp.int32, (c, c), 0) == lax.broadcasted_iota(jnp.int32, (c, c), 1)
    gc_row = jnp.sum(jnp.where(before_col, jnp.broadcast_to(g, (c, c)), 0.0), axis=0, keepdims=True)
    gc_col = jnp.sum(jnp.where(eye, jnp.broadcast_to(gc_row, (c, c)), 0.0), axis=1, keepdims=True)
    decay = jnp.where(incl, jnp.exp(jnp.where(incl, gc_col - gc_row, 0.0)), 0.0)
    g_tot = jnp.sum(g, axis=0, keepdims=True)
    return decay, jnp.exp(gc_col), jnp.exp(g_tot - gc_col), jnp.exp(g_tot)


def _gdn_kernel(*refs, seq_len, seqs, has_init):
    q_ref, k_ref, v_ref, gz_ref, gate_ref, gn_ref = refs[:6]
    pos = 6
    if has_init:
        s0f_ref, s0b_ref = refs[pos:pos + 2]
        pos += 2
    o_ref, sf_ref, sb_ref = refs[pos:pos + 3]
    osum, u_s, w_s, qd_s, kd_s, in_s, ge_s, st = refs[pos + 3:]
    hps = GDN_HEADS_PER_STEP
    head0 = pl.program_id(1) * hps
    ch = GDN_CHUNK
    n_chunks = seq_len // ch
    osum[...] = jnp.zeros(osum.shape, F32)
    hcols = [slice(hh * GDN_D, (hh + 1) * GDN_D) for hh in range(hps)]

    lane = lax.broadcasted_iota(jnp.int32, (ch, 128), 1)

    def pick(rows, col):
        return jnp.sum(jnp.where(lane == col, gate_ref[rows, :], 0.0), axis=1, keepdims=True)

    def solve_group(gi, carry):
        items = []
        for j in range(GDN_GROUP):
            c = gi * GDN_GROUP + j
            rows = pl.ds(pl.multiple_of(c * ch, ch), ch)
            for hh in range(hps):
                items.append(dict(c=c, hh=hh, rows=rows, q=q_ref[rows, hcols[hh]].astype(F32),
                                  k=k_ref[rows, hcols[hh]].astype(F32), v=v_ref[rows, hcols[hh]].astype(F32)))
        kks = [_mm_nt(it["k"], it["k"], "bf") for it in items]
        qks = [_mm_nt(it["q"], it["k"], "bf") for it in items]
        subs = []
        for it, kk, qk in zip(items, kks, qks):
            for d in range(2):
                beta = pick(it["rows"], d * GDN_HEADS + head0 + it["hh"])
                g = pick(it["rows"], 2 * GDN_HEADS + d * GDN_HEADS + head0 + it["hh"])
                decay, e_gc, e_rest, e_tot = _gdn_decay_terms(g, rev=(d == 1))
                strict = _tri_masks(ch, d == 1)[1]
                subs.append(dict(
                    d=d, c=it["c"], hh=it["hh"], rows=it["rows"],
                    a=jnp.where(strict, kk * beta * decay, 0.0),
                    rhs=jnp.concatenate([it["v"] * beta, it["k"] * (beta * e_gc)], axis=1),
                    intra=qk * decay, qd=it["q"] * e_gc, kd=it["k"] * e_rest, ge=e_tot))
        ts = _unit_tri_inv_many([s["a"] for s in subs], "bf")
        uws = [_mm(t, s["rhs"], "bf") for t, s in zip(ts, subs)]
        for s, uw in zip(subs, uws):
            d, hh, rows = s["d"], s["hh"], s["rows"]
            u_s[d, hh, rows, :] = uw[:, :GDN_D]
            w_s[d, hh, rows, :] = uw[:, GDN_D:].astype(BF16)
            qd_s[d, hh, rows, :] = s["qd"].astype(BF16)
            kd_s[d, hh, rows, :] = s["kd"].astype(BF16)
            in_s[d, hh, rows, :] = s["intra"].astype(BF16)
            ge_s[d, hh, pl.ds(pl.multiple_of(s["c"] * 8, 8), 8), :] = jnp.broadcast_to(s["ge"], (8, 128))
        return carry

    lax.fori_loop(0, seqs * n_chunks // GDN_GROUP, solve_group, 0)

    for hh in range(hps):
        for j in range(seqs):
            st[0, hh, j] = s0f_ref[j, hh] if has_init else jnp.zeros((GDN_D, GDN_D), F32)
            st[1, hh, j] = s0b_ref[j, hh] if has_init else jnp.zeros((GDN_D, GDN_D), F32)

    def recur(i, carry):
        cs = []
        for hh in range(hps):
            for j in range(seqs):
                for d in range(2):
                    c = j * n_chunks + (i if d == 0 else n_chunks - 1 - i)
                    cs.append(dict(d=d, hh=hh, j=j, rows=pl.ds(pl.multiple_of(c * ch, ch), ch),
                                   ge=ge_s[d, hh, pl.ds(pl.multiple_of(c * 8, 8), 1), :], s=st[d, hh, j]))
        sbs = [c["s"].astype(BF16) for c in cs]
        wss = [_mm(w_s[c["d"], c["hh"], c["rows"], :], sb, "bf") for c, sb in zip(cs, sbs)]
        qss = [_mm(qd_s[c["d"], c["hh"], c["rows"], :], sb, "bf") for c, sb in zip(cs, sbs)]
        ebs = [(u_s[c["d"], c["hh"], c["rows"], :] - ws).astype(BF16) for c, ws in zip(cs, wss)]
        outs = [qs_ + _mm(in_s[c["d"], c["hh"], c["rows"], :], eb, "bf") for c, qs_, eb in zip(cs, qss, ebs)]
        s_news = [c["s"] * c["ge"] + _mm_tn(kd_s[c["d"], c["hh"], c["rows"], :], eb, "bf")
                  for c, eb in zip(cs, ebs)]
        for c, o, s_new in zip(cs, outs, s_news):
            st[c["d"], c["hh"], c["j"]] = s_new
            osum[c["rows"], hcols[c["hh"]]] += o
        return carry

    lax.fori_loop(0, n_chunks, recur, 0)
    for hh in range(hps):
        for j in range(seqs):
            sf_ref[j, hh] = st[0, hh, j]
            sb_ref[j, hh] = st[1, hh, j]
        o_ref[:, hcols[hh]] = (_rms(osum[:, hcols[hh]], gn_ref[...]) * gz_ref[:, hcols[hh]].astype(F32))


def _gdn_mixer(qkv, gz, gates, n_seq, seq_len, gdn_norm, s0_f, s0_b):
    has_init = s0_f is not None
    hd = GDN_HEADS
    hps = GDN_HEADS_PER_STEP
    hgroups = hd // hps
    seqs = max(1, GDN_BLOCK_ROWS // seq_len)
    rows = seqs * seq_len
    assert n_seq % seqs == 0 and (rows // GDN_CHUNK) % GDN_GROUP == 0 and hd % hps == 0

    def col(block):
        return pl.BlockSpec((rows, hps * GDN_D), lambda s, h, b=block: (s, b * hgroups + h))

    state = pl.BlockSpec((seqs, None, hps, GDN_D, GDN_D), lambda s, h: (s, 0, h, 0, 0))
    in_specs = [col(0), col(1), col(2), col(0), pl.BlockSpec((rows, 128), lambda s, h: (s, 0)),
                pl.BlockSpec((1, 128), lambda s, h: (0, 0))]
    args = [qkv, qkv, qkv, gz, gates, gdn_norm.reshape(1, 128)]
    if has_init:
        in_specs += [state, state]
        args += [s0_f, s0_b]
    t = n_seq * seq_len
    scratch = ([pltpu.VMEM((rows, hps * GDN_D), F32)]
               + [pltpu.VMEM((2, hps, rows, GDN_D), F32)]
               + [pltpu.VMEM((2, hps, rows, GDN_D), BF16) for _ in range(3)]
               + [pltpu.VMEM((2, hps, rows, GDN_CHUNK), BF16),
                  pltpu.VMEM((2, hps, rows // GDN_CHUNK * 8, 128), F32),
                  pltpu.VMEM((2, hps, seqs, GDN_D, GDN_D), F32)])
    return pl.pallas_call(
        functools.partial(_gdn_kernel, seq_len=seq_len, seqs=seqs, has_init=has_init),
        grid=(n_seq // seqs, hgroups),
        in_specs=in_specs,
        out_specs=[pl.BlockSpec((rows, hps * GDN_D), lambda s, h: (s, h)), state, state],
        out_shape=[jax.ShapeDtypeStruct((t, hd * GDN_D), F32),
                   jax.ShapeDtypeStruct((n_seq, 1, hd, GDN_D, GDN_D), F32),
                   jax.ShapeDtypeStruct((n_seq, 1, hd, GDN_D, GDN_D), F32)],
        scratch_shapes=scratch,
        compiler_params=pltpu.CompilerParams(dimension_semantics=("parallel", "parallel"),
                                             vmem_limit_bytes=VMEM_LIMIT),
        name="gdn_mixer",
    )(*args)


def _softmax_pv(scores, values, sink):
    m = sink
    for s in scores:
        m = jnp.maximum(m, jnp.max(s, axis=-1, keepdims=True))
    den = jnp.exp(sink - m)
    acc = None
    for s, v in zip(scores, values):
        e = jnp.exp(s - m)
        den = den + jnp.sum(e, axis=-1, keepdims=True)
        pv = _mm(e, v, prec="bf")
        acc = pv if acc is None else acc + pv
    return acc / den


def _group_sinks(sink_ref, j, rows):
    assert rows & (rows - 1) == 0
    grp = lax.broadcasted_iota(jnp.int32, (ATT_GROUP * rows, 1), 0) >> (rows.bit_length() - 1)
    col = jnp.full(grp.shape, sink_ref[j * ATT_GROUP], F32)
    for gi in range(1, ATT_GROUP):
        col = jnp.where(grp == gi, sink_ref[j * ATT_GROUP + gi], col)
    return col


def _store_group(o_ref, j, o, rows):
    for gi in range(ATT_GROUP):
        hh = j * ATT_GROUP + gi
        o_ref[:, hh * ATT_HD:(hh + 1) * ATT_HD] = o[gi * rows:(gi + 1) * rows]


def _attn_ctx_kernel(sink_ref, p_ref, o_ref, kc_ref, vc_ref):
    scale = ATT_HD ** -0.5
    rows = p_ref.shape[0]
    for j in range(ATT_KV_HEADS):
        k = p_ref[:, ATT_Q_W + j * ATT_HD:ATT_Q_W + (j + 1) * ATT_HD]
        v = p_ref[:, ATT_Q_W + ATT_KV_W + j * ATT_HD:ATT_Q_W + ATT_KV_W + (j + 1) * ATT_HD]
        kc_ref[j] = k
        vc_ref[j] = v
        q = jnp.concatenate([p_ref[:, hh * ATT_HD:(hh + 1) * ATT_HD]
                             for hh in range(j * ATT_GROUP, (j + 1) * ATT_GROUP)], axis=0)
        s = _mm_nt(q, k, prec="bf") * scale
        _store_group(o_ref, j, _softmax_pv([s], [v], _group_sinks(sink_ref, j, rows)), rows)


def _attn_context(proj_att, sink, n_seq, seq_len):
    kv = pl.BlockSpec((None, None, ATT_KV_HEADS, seq_len, ATT_HD), lambda b: (b, 0, 0, 0, 0))
    return pl.pallas_call(
        _attn_ctx_kernel,
        grid=(n_seq,),
        in_specs=[pl.BlockSpec(memory_space=pltpu.SMEM),
                  pl.BlockSpec((seq_len, ATT_W), lambda b: (b, 0))],
        out_specs=[pl.BlockSpec((seq_len, ATT_Q_W), lambda b: (b, 0)), kv, kv],
        out_shape=[jax.ShapeDtypeStruct((n_seq * seq_len, ATT_Q_W), F32),
                   jax.ShapeDtypeStruct((n_seq, 1, ATT_KV_HEADS, seq_len, ATT_HD), F32),
                   jax.ShapeDtypeStruct((n_seq, 1, ATT_KV_HEADS, seq_len, ATT_HD), F32)],
        compiler_params=pltpu.CompilerParams(dimension_semantics=("parallel",)),
        name="attn_context",
    )(sink, proj_att)


def _rope_tables(seq_len):
    pos = np.arange(seq_len)
    half = ATT_HD // 2
    inv = ROPE_BASE ** (-np.arange(0, half, 2, dtype=np.float32) / half)
    ang_r = (pos // GRID_W).astype(np.float32)[:, None] * inv
    ang_c = (pos % GRID_W).astype(np.float32)[:, None] * inv
    cos = np.concatenate([np.cos(ang_r), np.cos(ang_r), np.cos(ang_c), np.cos(ang_c)], axis=1)
    sin = np.concatenate([-np.sin(ang_r), np.sin(ang_r), -np.sin(ang_c), np.sin(ang_c)], axis=1)
    return (jnp.asarray(np.tile(cos, (1, 2)), F32), jnp.asarray(np.tile(sin, (1, 2)), F32))


def _rope(x, cos, sin):
    lane = lax.broadcasted_iota(jnp.int32, x.shape, 1)
    partner = jnp.where((lane & 31) < 16, pltpu.roll(x, 128 - 16, 1), pltpu.roll(x, 16, 1))
    return x * cos + partner * sin


def _attn_lat_kernel(sink_ref, p_ref, ck_ref, cv_ref, cos_ref, sin_ref, o_ref, *, seq_len):
    scale = ATT_HD ** -0.5
    qb = pl.program_id(1)
    span = 3 * ATT_BLOCK
    q0 = pl.multiple_of(qb * ATT_BLOCK, ATT_BLOCK)
    k0 = pl.multiple_of(jnp.clip((qb - 1) * ATT_BLOCK, 0, seq_len - span), ATT_BLOCK)
    qrows = pl.ds(q0, ATT_BLOCK)
    krows = pl.ds(k0, span)
    kwin = _rope(p_ref[krows, ATT_Q_W:ATT_Q_W + ATT_KV_W], cos_ref[krows, :], sin_ref[krows, :])
    vwin = p_ref[krows, ATT_Q_W + ATT_KV_W:ATT_W]
    stacked = ATT_GROUP * ATT_BLOCK
    qpos = q0 + (lax.broadcasted_iota(jnp.int32, (stacked, span), 0) & (ATT_BLOCK - 1))
    kpos = k0 + lax.broadcasted_iota(jnp.int32, (stacked, span), 1)
    valid = jnp.abs(qpos - kpos) <= WINDOW
    cos_q = cos_ref[qrows, :]
    sin_q = sin_ref[qrows, :]
    heads = []
    for pair in range(ATT_HEADS // 2):
        qpair = _rope(p_ref[qrows, pair * 128:(pair + 1) * 128], cos_q, sin_q)
        heads += [qpair[:, :ATT_HD], qpair[:, ATT_HD:]]
    kv = range(ATT_KV_HEADS)
    qs = [jnp.concatenate(heads[j * ATT_GROUP:(j + 1) * ATT_GROUP], axis=0) for j in kv]
    s_locs = [jnp.where(valid, _mm_nt(qs[j], kwin[:, j * ATT_HD:(j + 1) * ATT_HD], prec="bf") * scale, NEG_INF)
              for j in kv]
    s_ctxs = [_mm_nt(qs[j], ck_ref[j], prec="bf") * scale for j in kv]
    outs = [_softmax_pv([s_locs[j], s_ctxs[j]], [vwin[:, j * ATT_HD:(j + 1) * ATT_HD], cv_ref[j]],
                        _group_sinks(sink_ref, j, ATT_BLOCK)) for j in kv]
    for j in kv:
        _store_group(o_ref, j, outs[j], ATT_BLOCK)


def _attn_latent(proj_att, sink, cache_k, cache_v, layer, n_seq, seq_len):
    cos, sin = _rope_tables(seq_len)
    past = cache_k.shape[3]
    nqb = seq_len // ATT_BLOCK
    cache = pl.BlockSpec((None, None, ATT_KV_HEADS, past, ATT_HD), lambda b, q: (b, layer, 0, 0, 0))
    table = pl.BlockSpec((seq_len, 128), lambda b, q: (0, 0))
    return pl.pallas_call(
        functools.partial(_attn_lat_kernel, seq_len=seq_len),
        grid=(n_seq, nqb),
        in_specs=[pl.BlockSpec(memory_space=pltpu.SMEM),
                  pl.BlockSpec((seq_len, ATT_W), lambda b, q: (b, 0)),
                  cache, cache, table, table],
        out_specs=pl.BlockSpec((ATT_BLOCK, ATT_Q_W), lambda b, q: (b * nqb + q, 0)),
        out_shape=jax.ShapeDtypeStruct((n_seq * seq_len, ATT_Q_W), F32),
        compiler_params=pltpu.CompilerParams(dimension_semantics=("parallel", "parallel")),
        name="attn_latent",
    )(sink, proj_att, cache_k, cache_v, cos, sin)


def _pair_masks(n, rev):
    r = lax.broadcasted_iota(jnp.int32, (n, 2 * n), 0)
    c = lax.broadcasted_iota(jnp.int32, (n, 2 * n), 1) & (n - 1)
    return (r <= c, r < c) if rev else (r >= c, r > c)


def _bd(x):
    half = x.shape[1] // 2
    lane = lax.broadcasted_iota(jnp.int32, x.shape, 1)
    zero = jnp.zeros_like(x)
    return jnp.concatenate([jnp.where(lane < half, x, zero), jnp.where(lane >= half, x, zero)], axis=0)


def _bd_mask(n):
    r = lax.broadcasted_iota(jnp.int32, (2 * n, 2 * n), 0)
    c = lax.broadcasted_iota(jnp.int32, (2 * n, 2 * n), 1)
    return (r < n) == (c < n)


def _head_sums(x):
    ones = jnp.where(_bd_mask(RWKV_HD), 1.0, 0.0).astype(BF16)
    return _mm(x, ones, "x2l")


def _apply_pairs(tb, x):
    hi, lo = _split_bf16(x)
    return _mm(tb, _bd(hi), "bf") + _mm(tb, _bd(lo), "bf")


def _unit_tri_inv_pairs(mats):
    n = mats[0].shape[0]
    r = lax.broadcasted_iota(jnp.int32, (n, 2 * n), 0)
    c = lax.broadcasted_iota(jnp.int32, (n, 2 * n), 1) & (n - 1)
    eye = jnp.where(r == c, 1.0, 0.0)
    ts = [eye - a * jnp.where((r >> 1) == (c >> 1), 1.0, 0.0) for a in mats]
    abs_ = [a.astype(BF16) for a in mats]
    shift = 1
    while (1 << shift) < n:
        join = jnp.where(((r >> (shift + 1)) == (c >> (shift + 1))) & ((r >> shift) != (c >> shift)),
                         1.0, 0.0).astype(BF16)
        tbs = [t.astype(BF16) for t in ts]
        inner = [_mm(ab * join, _bd(tb), "bf") for ab, tb in zip(abs_, tbs)]
        ts = [t - _mm(tb, _bd(w.astype(BF16)), "bf") for t, tb, w in zip(ts, tbs, inner)]
        shift += 1
    return ts


def _rwkv_chunk_operands(x_ref, prm, sc, c, *, seq_len):
    (mu_ref, w0_ref, wup_ref, a0_ref, aup_ref, gup_ref, kk_ref, ka_ref, rk_ref) = prm
    ch = RWKV_CHUNK
    total = x_ref.shape[0]
    r0 = pl.multiple_of(c * ch, ch)
    rows = pl.ds(r0, ch)
    pos0 = r0 & (seq_len - 1)
    x = x_ref[rows, :]
    prev_row = x_ref[pl.ds(jnp.maximum(r0 - 1, 0), 1), :] * jnp.where(pos0 > 0, 1.0, 0.0)
    next_row = x_ref[pl.ds(jnp.minimum(r0 + ch, total - 1), 1), :] * jnp.where(pos0 + ch < seq_len, 1.0, 0.0)
    xp, xn = _shifted_rows(x, prev_row, next_row)
    xs = x + mu_ref[0:1, :] * (xp - x) + mu_ref[1:2, :] * (xn - x)
    r = xs[:, 0:RWKV_W]
    k = xs[:, RWKV_W:2 * RWKV_W]
    v = xs[:, 2 * RWKV_W:3 * RWKV_W]
    lo = 3 * RWKV_W
    p = RWKV_PREC
    gl = xs[:, lo + 4 * RWKV_LORA:lo + 6 * RWKV_LORA]
    sc["gate"][rows, :] = _mm(jax.nn.sigmoid(gl), gup_ref[...], p["gate"])
    sc["v"][rows, :] = v.astype(BF16)
    pairs = [slice(i * PAIR_W, (i + 1) * PAIR_W) for i in range(N_PAIRS)]
    kkv = k * kk_ref[...]
    kaps = []
    for cols in pairs:
        kk_p = kkv[:, cols]
        kaps.append(kk_p * lax.rsqrt(_head_sums(kk_p * kk_p) + 1e-6))
    items = []
    bonus = None
    for d in range(2):
        rev = d == 1
        wl = xs[:, lo + d * RWKV_LORA:lo + (d + 1) * RWKV_LORA]
        al = xs[:, lo + 2 * RWKV_LORA + d * RWKV_LORA:lo + 2 * RWKV_LORA + (d + 1) * RWKV_LORA]
        lw = -DECAY_SCALE * jax.nn.sigmoid(w0_ref[d:d + 1, :] + _mm(jnp.tanh(wl), wup_ref[d], p["lora"]))
        a = jax.nn.sigmoid(a0_ref[d:d + 1, :] + _mm(al, aup_ref[d], p["lora"]))
        k2 = k * (1.0 + (a - 1.0) * ka_ref[...])
        g_inc = _mm(jnp.where(_tri_masks(ch, rev)[0], 1.0, 0.0), lw, p["cumsum"])
        g_tot = jnp.sum(lw, axis=0, keepdims=True)
        e_neg = jnp.exp(-g_inc)
        e_end = jnp.exp(g_tot - g_inc)
        e_exc = jnp.exp(g_inc - lw)
        r_dec = r * jnp.exp(g_inc)
        k_neg = k2 * e_neg
        k_end = k2 * e_end
        sc["dec"][d, pl.ds(pl.multiple_of(c * 8, 8), 8), :] = jnp.broadcast_to(jnp.exp(g_tot), (8, RWKV_W))
        rkr = r * k2 * rk_ref[...]
        bon_d = jnp.concatenate([_head_sums(rkr[:, cols]) for cols in pairs], axis=1) * v
        bonus = bon_d if bonus is None else bonus + bon_d
        for cols, kap in zip(pairs, kaps):
            b_p = kap * a[:, cols]
            items.append(dict(d=d, rows=rows, cols=cols, kap_dec=kap * e_exc[:, cols], r_dec=r_dec[:, cols],
                              b_neg=b_p * e_neg[:, cols], k_neg=k_neg[:, cols],
                              b_end=b_p * e_end[:, cols], k_end=k_end[:, cols], vb=v[:, cols].astype(BF16)))
    sc["bon"][rows, :] = bonus
    return items


def _rwkv_solve(items, sc):
    ch = RWKV_CHUNK
    masks = [_pair_masks(ch, False), _pair_masks(ch, True)]
    ms = [_mm_nt(jnp.concatenate([it["kap_dec"], it["r_dec"]], axis=0),
                 jnp.concatenate([_bd(it["b_neg"].astype(BF16)), _bd(it["k_neg"].astype(BF16))], axis=0), "bf")
          for it in items]
    a_abs = [jnp.where(masks[it["d"]][1], m[:ch, :2 * ch], 0.0) for it, m in zip(items, ms)]
    a_aks = [jnp.where(masks[it["d"]][1], m[:ch, 2 * ch:], 0.0) for it, m in zip(items, ms)]
    a_rbs = [jnp.where(masks[it["d"]][0], m[ch:, :2 * ch], 0.0) for it, m in zip(items, ms)]
    a_rks = [jnp.where(masks[it["d"]][0], m[ch:, 2 * ch:], 0.0) for it, m in zip(items, ms)]
    tbs = [t.astype(BF16) for t in _unit_tri_inv_pairs(a_abs)]
    akvs = [_mm(a_ak, _bd(it["vb"]), "bf") for it, a_ak in zip(items, a_aks)]
    w2s = [_apply_pairs(tb, it["kap_dec"]) for it, tb in zip(items, tbs)]
    u0s = [_apply_pairs(tb, akv) for tb, akv in zip(tbs, akvs)]
    for it, w2, u0, a_rb, a_rk in zip(items, w2s, u0s, a_rbs, a_rks):
        d, rows, cols = it["d"], it["rows"], it["cols"]
        sc["w2"][d, rows, cols] = w2.astype(BF16)
        sc["rd"][d, rows, cols] = it["r_dec"].astype(BF16)
        sc["u0"][d, rows, cols] = u0
        sc["arb"][d, rows, cols] = a_rb.astype(BF16)
        sc["ark"][d, rows, cols] = a_rk.astype(BF16)
        sc["bh"][d, rows, cols] = it["b_end"].astype(BF16)
        sc["kh"][d, rows, cols] = it["k_end"].astype(BF16)


def _rwkv_recur(sc, st, i, *, seq_len, seqs):
    ch = RWKV_CHUNK
    n_chunks = seq_len // ch
    keep = _bd_mask(RWKV_HD)
    cs = []
    for j in range(seqs):
        for d in range(2):
            c = j * n_chunks + (i if d == 0 else n_chunks - 1 - i)
            rows = pl.ds(pl.multiple_of(c * ch, ch), ch)
            dec = sc["dec"][d, pl.ds(pl.multiple_of(c * 8, 8), 1), :]
            for pi in range(N_PAIRS):
                cols = slice(pi * PAIR_W, (pi + 1) * PAIR_W)
                cs.append(dict(d=d, j=j, p=pi, rows=rows, cols=cols, dec=dec[:, cols], s=st[d, j, pi]))
    sbs = [c["s"].astype(BF16) for c in cs]
    lss = [_mm_nt(jnp.concatenate([sc["w2"][c["d"], c["rows"], c["cols"]],
                                   sc["rd"][c["d"], c["rows"], c["cols"]]], axis=0), sb, "bf")
           for c, sb in zip(cs, sbs)]
    ubs = [(-(ls[:ch] + sc["u0"][c["d"], c["rows"], c["cols"]])).astype(BF16) for c, ls in zip(cs, lss)]
    vbs = [sc["v"][c["rows"], c["cols"]] for c in cs]
    ys = [ls[ch:] + _mm(jnp.concatenate([sc["arb"][c["d"], c["rows"], c["cols"]],
                                         sc["ark"][c["d"], c["rows"], c["cols"]]], axis=1),
                        jnp.concatenate([_bd(ub), _bd(vb)], axis=0), "bf")
          for c, ls, ub, vb in zip(cs, lss, ubs, vbs)]
    ups = [_mm_tn(jnp.concatenate([ub, vb], axis=0),
                  jnp.concatenate([sc["bh"][c["d"], c["rows"], c["cols"]],
                                   sc["kh"][c["d"], c["rows"], c["cols"]]], axis=0), "bf")
           for c, ub, vb in zip(cs, ubs, vbs)]
    for c, y, up in zip(cs, ys, ups):
        st[c["d"], c["j"], c["p"]] = c["s"] * c["dec"] + jnp.where(keep, up, 0.0)
        sc["ysum"][c["rows"], c["cols"]] += y


RWKV_SCRATCH = ("ysum", "bon", "gate", "v", "w2", "rd", "u0", "arb", "ark", "bh", "kh", "dec")


def _rwkv_kernel(*refs, seq_len, seqs, has_init):
    x_ref = refs[0]
    prm = refs[1:10]
    lnw_ref, lnb_ref = refs[10:12]
    pos = 12
    if has_init:
        s0f_ref, s0b_ref = refs[pos:pos + 2]
        pos += 2
    o_ref, sf_ref, sb_ref = refs[pos:pos + 3]
    sc = dict(zip(RWKV_SCRATCH, refs[pos + 3:]))
    st = refs[pos + 3 + len(RWKV_SCRATCH)]
    ch = RWKV_CHUNK
    hd = RWKV_HD
    n_chunks = seq_len // ch
    sc["ysum"][...] = jnp.zeros(sc["ysum"].shape, F32)

    def prepare(gi, carry):
        items = []
        for j in range(RWKV_GROUP):
            items += _rwkv_chunk_operands(x_ref, prm, sc, gi * RWKV_GROUP + j, seq_len=seq_len)
        _rwkv_solve(items, sc)
        return carry

    lax.fori_loop(0, seqs * n_chunks // RWKV_GROUP, prepare, 0)

    zero = jnp.zeros((hd, hd), F32)
    for d, s0_ref in enumerate((s0f_ref, s0b_ref) if has_init else (None, None)):
        for j in range(seqs):
            for pi in range(N_PAIRS):
                s_a = s0_ref[j, 2 * pi] if has_init else zero
                s_b = s0_ref[j, 2 * pi + 1] if has_init else zero
                st[d, j, pi] = jnp.concatenate([jnp.concatenate([s_a, zero], axis=1),
                                                jnp.concatenate([zero, s_b], axis=1)], axis=0)

    def recur(i, carry):
        _rwkv_recur(sc, st, i, seq_len=seq_len, seqs=seqs)
        return carry

    lax.fori_loop(0, n_chunks, recur, 0)
    for d, out_ref in enumerate((sf_ref, sb_ref)):
        for j in range(seqs):
            for pi in range(N_PAIRS):
                s = st[d, j, pi]
                out_ref[j, 2 * pi] = s[:hd, :hd]
                out_ref[j, 2 * pi + 1] = s[hd:, hd:]

    tile = RWKV_FINISH_ROWS

    def finish(i, carry):
        rows = pl.ds(pl.multiple_of(i * tile, tile), tile)
        pairs = [slice(pi * PAIR_W, (pi + 1) * PAIR_W) for pi in range(N_PAIRS)]
        ys = [sc["ysum"][rows, cols] for cols in pairs]
        cens = [y - _head_sums(y) * (1.0 / hd) for y in ys]
        vars_ = [_head_sums(cen * cen) * (1.0 / hd) for cen in cens]
        for cols, cen, var in zip(pairs, cens, vars_):
            yn = cen * lax.rsqrt(var + GN_EPS) * lnw_ref[:, cols] + lnb_ref[:, cols]
            o_ref[rows, cols] = (yn + sc["bon"][rows, cols]) * sc["gate"][rows, cols]
        return carry

    lax.fori_loop(0, seqs * seq_len // tile, finish, 0)


def _rwkv_mixer(x_rw, n_seq, seq_len, params, s0_f, s0_b):
    has_init = s0_f is not None
    (mu, w0, w_up, a0, a_up, g_up, k_k, k_a, r_k, ln_w, ln_b) = params
    row = lambda a: a.reshape(1, RWKV_W)
    args = [x_rw, mu, w0, w_up, a0, a_up, g_up, row(k_k), row(k_a), row(r_k), row(ln_w), row(ln_b)]

    def whole(a):
        nd = a.ndim
        return pl.BlockSpec(a.shape, lambda s, nd=nd: (0,) * nd)

    seqs = max(1, RWKV_BLOCK_ROWS // seq_len)
    rows = seqs * seq_len
    assert n_seq % seqs == 0 and (rows // RWKV_CHUNK) % RWKV_GROUP == 0 and seq_len & (seq_len - 1) == 0
    in_specs = [pl.BlockSpec((rows, RWKV_IN), lambda s: (s, 0))] + [whole(a) for a in args[1:]]
    state = pl.BlockSpec((seqs, None, RWKV_HEADS, RWKV_HD, RWKV_HD), lambda s: (s, 0, 0, 0, 0))
    if has_init:
        in_specs += [state, state]
        args += [s0_f, s0_b]
    st_shape = jax.ShapeDtypeStruct((n_seq, 1, RWKV_HEADS, RWKV_HD, RWKV_HD), F32)
    tok = lambda dt: pltpu.VMEM((rows, RWKV_W), dt)
    per_dir = lambda dt: pltpu.VMEM((2, rows, RWKV_W), dt)
    scratch = dict(ysum=tok(F32), bon=tok(F32), gate=tok(F32), v=tok(BF16), w2=per_dir(BF16), rd=per_dir(BF16),
                   u0=per_dir(F32), arb=per_dir(BF16), ark=per_dir(BF16), bh=per_dir(BF16), kh=per_dir(BF16),
                   dec=pltpu.VMEM((2, rows // RWKV_CHUNK * 8, RWKV_W), F32))
    return pl.pallas_call(
        functools.partial(_rwkv_kernel, seq_len=seq_len, seqs=seqs, has_init=has_init),
        grid=(n_seq // seqs,),
        in_specs=in_specs,
        out_specs=[pl.BlockSpec((rows, RWKV_W), lambda s: (s, 0)), state, state],
        out_shape=[jax.ShapeDtypeStruct((n_seq * seq_len, RWKV_W), F32), st_shape, st_shape],
        scratch_shapes=[scratch[name] for name in RWKV_SCRATCH]
        + [pltpu.VMEM((2, seqs, N_PAIRS, PAIR_W, PAIR_W), F32)],
        compiler_params=pltpu.CompilerParams(dimension_semantics=("parallel",), vmem_limit_bytes=VMEM_LIMIT),
        name="rwkv_mixer",
    )(*args)


def kernel(x_prompt, x_sample, state_gdn_fwd, state_gdn_bwd, cache_attn_k, cache_attn_v, state_rwkv_fwd, state_rwkv_bwd, c, c_ctx, mod_w, mod_b, norm_mix, norm_mlp, mlp_w1, mlp_w2, norm_final, ev_w_in, ev_w_out, gdn_conv, gdn_a_log, gdn_dt_bias, gdn_norm, sc_conv, od_w_in, od_w_out, attn_sink, rwkv_mu, rwkv_w0, rwkv_w_up, rwkv_a0, rwkv_a_up, rwkv_g_up, rwkv_k_k, rwkv_k_a, rwkv_r_k, rwkv_ln_w, rwkv_ln_b):
    bp, lp, _ = x_prompt.shape
    bs, ls, _ = x_sample.shape
    depth = mod_w.shape[0]
    c_rows = jnp.concatenate([c_ctx[None, :], c, jnp.zeros((MOD_ROWS - 1 - bs, D_MODEL), F32)], axis=0)
    mods = _modulation(c_rows, mod_w, mod_b)

    assert ls % TOKEN_TILE == 0 and (bp * lp) % TOKEN_TILE == 0
    groups = [
        dict(x=x_prompt.reshape(bp * lp, D_MODEL), n=bp, l=lp, latent=False,
             mod=lambda layer: _mod_spec(layer, 1, 0, 0)),
        dict(x=x_sample.reshape(bs * ls, D_MODEL), n=bs, l=ls, latent=True,
             mod=lambda layer: _mod_spec(layer, ls // TOKEN_TILE, 1, 1)),
    ]
    outs = {}
    w1 = mlp_w1.astype(BF16)
    w2 = mlp_w2.astype(BF16)
    for layer in range(depth):
        final = layer == depth - 1
        if layer % 2 == 0:
            e = layer // 2
            w = ev_w_in[e]
            qkvz = GDN_QKV_W + GDN_HEADS * GDN_D
            n_gate = 4 * GDN_HEADS
            gate_cols = jnp.concatenate([w[:, qkvz:qkvz + n_gate],
                                         jnp.zeros((D_MODEL, EV_IN_PAD - w.shape[1]), F32)], axis=1)
            w_qkvz, w_sc, w_gate = w[:, :qkvz].astype(BF16), w[:, qkvz + n_gate:].astype(BF16), gate_cols.astype(BF16)
            w_out = ev_w_out[e].astype(BF16)
            alog_vec = jnp.zeros((1, 128), F32).at[0, 2 * GDN_HEADS:4 * GDN_HEADS].set(gdn_a_log[e].reshape(-1))
            dtb_vec = jnp.zeros((1, 128), F32).at[0, 2 * GDN_HEADS:4 * GDN_HEADS].set(gdn_dt_bias[e].reshape(-1))
            for grp in groups:
                qkv, gz, sc, gates = _inproj_even(grp["x"], mods, norm_mix[layer], w_qkvz, w_sc, w_gate, gdn_conv[e],
                                                  sc_conv[e], alog_vec, dtb_vec, grp["l"], grp["mod"](layer))
                s0 = (state_gdn_fwd[:, e:e + 1], state_gdn_bwd[:, e:e + 1]) if grp["latent"] else (None, None)
                o, s_f, s_b = _gdn_mixer(qkv, gz, gates, grp["n"], grp["l"], gdn_norm[e], *s0)
                if not grp["latent"]:
                    outs.setdefault("gdn_f", []).append(s_f)
                    outs.setdefault("gdn_b", []).append(s_b)
                grp["x"] = _outproj_mlp(o, sc, grp["x"], mods, norm_mlp[layer], w_out, w1, w2, layer, norm_final,
                                        grp["mod"](layer), final)
        else:
            o_ = layer // 2
            w = od_w_in[o_]
            w_in = [w.astype(BF16)]
            w_out = od_w_out[o_].astype(BF16)
            rw = (rwkv_mu[o_], rwkv_w0[o_], rwkv_w_up[o_], rwkv_a0[o_], rwkv_a_up[o_], rwkv_g_up[o_],
                  rwkv_k_k[o_], rwkv_k_a[o_], rwkv_r_k[o_].reshape(-1), rwkv_ln_w[o_], rwkv_ln_b[o_])
            for grp in groups:
                p_att, x_rw = _inproj(grp["x"], mods, norm_mix[layer], w_in, (ATT_W, RWKV_IN), grp["mod"](layer))
                if grp["latent"]:
                    att = _attn_latent(p_att, attn_sink[o_], cache_attn_k, cache_attn_v, o_, grp["n"], grp["l"])
                    rwo, _, _ = _rwkv_mixer(x_rw, grp["n"], grp["l"], rw,
                                            state_rwkv_fwd[:, o_:o_ + 1], state_rwkv_bwd[:, o_:o_ + 1])
                else:
                    att, kc, vc = _attn_context(p_att, attn_sink[o_], grp["n"], grp["l"])
                    rwo, s_f, s_b = _rwkv_mixer(x_rw, grp["n"], grp["l"], rw, None, None)
                    outs.setdefault("att_k", []).append(kc)
                    outs.setdefault("att_v", []).append(vc)
                    outs.setdefault("rw_f", []).append(s_f)
                    outs.setdefault("rw_b", []).append(s_b)
                grp["x"] = _outproj_mlp(att, rwo, grp["x"], mods, norm_mlp[layer], w_out, w1, w2, layer, norm_final,
                                        grp["mod"](layer), final)
    cat = lambda key: jnp.concatenate(outs[key], axis=1)
    return (groups[0]["x"].reshape(bp, lp, D_MODEL), groups[1]["x"].reshape(bs, ls, D_MODEL),
            cat("gdn_f"), cat("gdn_b"), cat("att_k"), cat("att_v"), cat("rw_f"), cat("rw_b"))
```

```python
import functools

import jax
import jax.numpy as jnp
import numpy as np
from jax import lax
from jax.experimental import pallas as pl
from jax.experimental.pallas import tpu as pltpu

F32 = jnp.float32
BF16 = jnp.bfloat16

D_MODEL = 1024
N_MOD = 6
D_FF = 4 * D_MODEL
NORM_EPS = 1e-6
TOKEN_TILE = 512
HALO = 8
MOD_ROWS = 8

GDN_HEADS = 4
GDN_D = 128
GDN_CHUNK = 128
GDN_GROUP = 8
GDN_HEADS_PER_STEP = 2
GDN_BLOCK_ROWS = 1024
GDN_QKV_W = 3 * GDN_HEADS * GDN_D
SC_WIDTH = 512
EV_IN_PAD = 3712

ATT_HEADS = 8
ATT_KV_HEADS = 2
ATT_GROUP = ATT_HEADS // ATT_KV_HEADS
ATT_HD = 64
ATT_Q_W = ATT_HEADS * ATT_HD
ATT_KV_W = ATT_KV_HEADS * ATT_HD
ATT_W = ATT_Q_W + 2 * ATT_KV_W
WINDOW = 128
ATT_BLOCK = 128
GRID_W = 64
ROPE_BASE = 10000.0
NEG_INF = -1e30

RWKV_HEADS = 8
RWKV_HD = 64
RWKV_W = RWKV_HEADS * RWKV_HD
RWKV_LORA = 64
RWKV_IN = 3 * RWKV_W + 3 * 2 * RWKV_LORA
RWKV_CHUNK = 64
RWKV_GROUP = 4
RWKV_FINISH_ROWS = 256
DECAY_SCALE = float(np.exp(-0.5))
RWKV_BLOCK_ROWS = 1024
PAIR_W = 2 * RWKV_HD
N_PAIRS = RWKV_W // PAIR_W
GN_EPS = 64e-5

VMEM_LIMIT = 56 * 1024 * 1024

RWKV_PREC = dict(lora="bf", gate="bf", cumsum="x2r")


def _split_bf16(a):
    hi = a.astype(BF16)
    return hi, (a - hi.astype(F32)).astype(BF16)


def _dot(a, b, dims, prec):
    dn = (dims, ((), ()))
    one = lambda x, y: lax.dot_general(x, y, dn, preferred_element_type=F32)
    if prec == "x2r":
        ah = a.astype(BF16)
        bh, bl = _split_bf16(b)
        return one(ah, bh) + one(ah, bl)
    assert prec == "bf", prec
    return one(a.astype(BF16), b.astype(BF16))


def _mm(a, b, prec):
    return _dot(a, b, ((1,), (0,)), prec)


def _mm_nt(a, b, prec):
    return _dot(a, b, ((1,), (1,)), prec)


def _mm_tn(a, b, prec):
    return _dot(a, b, ((0,), (0,)), prec)


def _silu(x):
    return x * jax.nn.sigmoid(x)


def _softplus(x):
    return jnp.maximum(x, 0.0) + jnp.log1p(jnp.exp(-jnp.abs(x)))


def _rms(x, w):
    return x * lax.rsqrt(jnp.mean(x * x, axis=-1, keepdims=True) + NORM_EPS) * w


def _tri_masks(n, rev):
    r = lax.broadcasted_iota(jnp.int32, (n, n), 0)
    c = lax.broadcasted_iota(jnp.int32, (n, n), 1)
    if rev:
        return r <= c, r < c
    return r >= c, r > c


def _unit_tri_inv_many(mats, prec):
    assert prec == "bf"
    n = mats[0].shape[0]
    r = lax.broadcasted_iota(jnp.int32, (n, n), 0)
    c = lax.broadcasted_iota(jnp.int32, (n, n), 1)
    eye = jnp.where(r == c, 1.0, 0.0)
    ts = [eye - a * jnp.where((r >> 1) == (c >> 1), 1.0, 0.0) for a in mats]
    abs_ = [a.astype(BF16) for a in mats]
    shift = 1
    while (1 << shift) < n:
        join = jnp.where(((r >> (shift + 1)) == (c >> (shift + 1))) & ((r >> shift) != (c >> shift)),
                         1.0, 0.0).astype(BF16)
        tbs = [t.astype(BF16) for t in ts]
        inner = [_mm(ab * join, tb, prec) for ab, tb in zip(abs_, tbs)]
        ts = [t - _mm(tb, w, prec) for t, tb, w in zip(ts, tbs, inner)]
        shift += 1
    return ts


def _shifted_rows(x, prev_row, next_row):
    n = x.shape[0]
    row = lax.broadcasted_iota(jnp.int32, x.shape, 0)
    xp = jnp.where(row == 0, prev_row, pltpu.roll(x, 1, 0))
    xn = jnp.where(row == n - 1, next_row, pltpu.roll(x, n - 1, 0))
    return xp, xn


def _mod_kernel(c_ref, w_ref, b_ref, o_ref):
    s = _silu(c_ref[...])
    o_ref[...] = _mm(s, w_ref[...], prec="bf") + b_ref[...]


def _modulation(c_rows, mod_w, mod_b):
    depth = mod_w.shape[0]
    nblk = (N_MOD * D_MODEL) // D_MODEL
    out = pl.pallas_call(
        _mod_kernel,
        grid=(depth, nblk),
        in_specs=[
            pl.BlockSpec((MOD_ROWS, D_MODEL), lambda l, j: (0, 0)),
            pl.BlockSpec((None, D_MODEL, D_MODEL), lambda l, j: (l, 0, j)),
            pl.BlockSpec((None, 1, D_MODEL), lambda l, j: (l, 0, j)),
        ],
        out_specs=pl.BlockSpec((None, MOD_ROWS, D_MODEL), lambda l, j: (l, 0, j)),
        out_shape=jax.ShapeDtypeStruct((depth, MOD_ROWS, N_MOD * D_MODEL), F32),
        compiler_params=pltpu.CompilerParams(dimension_semantics=("parallel", "parallel")),
        name="modulation",
    )(c_rows, mod_w, mod_b.reshape(depth, 1, N_MOD * D_MODEL))
    return out.reshape(depth, MOD_ROWS, N_MOD, D_MODEL)


def _mod_spec(layer, tiles_per_seq, row_base, row_step):
    return pl.BlockSpec((None, None, N_MOD, D_MODEL),
                        lambda i: (layer, row_base + (i // tiles_per_seq) * row_step, 0, 0))


def _inproj_kernel(*refs, n_w):
    x_ref, mod_ref, nw_ref = refs[:3]
    w_refs = refs[3:3 + n_w]
    o_refs = refs[3 + n_w:]
    h = _rms(x_ref[...], nw_ref[...])
    h = (h * (1.0 + mod_ref[1:2, :]) + mod_ref[0:1, :]).astype(BF16)
    pieces = [_mm(h, w_ref[...], prec="bf") for w_ref in w_refs]
    y = pieces[0] if n_w == 1 else jnp.concatenate(pieces, axis=1)
    off = 0
    for o_ref in o_refs:
        n = o_ref.shape[-1]
        o_ref[...] = y[:, off:off + n]
        off += n


def _inproj(x, mods, norm_w, ws_bf16, splits, mod_spec):
    t = x.shape[0]
    assert sum(w.shape[1] for w in ws_bf16) == sum(splits) and all(w.shape[1] % 128 == 0 for w in ws_bf16)
    return pl.pallas_call(
        functools.partial(_inproj_kernel, n_w=len(ws_bf16)),
        grid=(t // TOKEN_TILE,),
        in_specs=[
            pl.BlockSpec((TOKEN_TILE, D_MODEL), lambda i: (i, 0)),
            mod_spec,
            pl.BlockSpec((1, D_MODEL), lambda i: (0, 0)),
        ] + [pl.BlockSpec(w.shape, lambda i: (0, 0)) for w in ws_bf16],
        out_specs=[pl.BlockSpec((TOKEN_TILE, n), lambda i: (i, 0)) for n in splits],
        out_shape=[jax.ShapeDtypeStruct((t, n), F32) for n in splits],
        compiler_params=pltpu.CompilerParams(dimension_semantics=("parallel",), vmem_limit_bytes=VMEM_LIMIT),
        name="inproj",
    )(x, mods, norm_w.reshape(1, D_MODEL), *ws_bf16)


def _inproj_even_kernel(xp_ref, x_ref, xn_ref, mod_ref, nw_ref, wqkvz_ref, wsc_ref, wgate_ref, cqkv_ref, csc_ref,
                        alog_ref, dtb_ref, qkv_ref, gz_ref, sc_ref, gate_ref, *, seq_len):
    tile = x_ref.shape[0]
    x = jnp.concatenate([xp_ref[...], x_ref[...], xn_ref[...]], axis=0)
    h = _rms(x, nw_ref[...])
    h = (h * (1.0 + mod_ref[1:2, :]) + mod_ref[0:1, :]).astype(BF16)
    n = tile + 2 * HALO
    first = pl.program_id(0) * tile - HALO
    pos = (first + lax.broadcasted_iota(jnp.int32, (n, 1), 0)) & (seq_len - 1)
    at_start = pos == 0
    at_end = pos == seq_len - 1

    def conv3(v, c_ref):
        vp = jnp.where(at_start, 0.0, pltpu.roll(v, 1, 0))
        vn = jnp.where(at_end, 0.0, pltpu.roll(v, n - 1, 0))
        return vp * c_ref[0:1, :] + v * c_ref[1:2, :] + vn * c_ref[2:3, :]

    keep = slice(HALO, HALO + tile)
    wide = 2 * GDN_D
    for j in range(GDN_QKV_W // wide):
        cols = slice(j * wide, (j + 1) * wide)
        act = _silu(conv3(_mm(h, wqkvz_ref[:, cols], "bf"), cqkv_ref.at[:, cols]))[keep]
        for i in range(2):
            part = act[:, i * GDN_D:(i + 1) * GDN_D]
            if j < 2 * GDN_HEADS // 2:
                part = part * lax.rsqrt(jnp.sum(part * part, axis=-1, keepdims=True) + 1e-6)
                if j < GDN_HEADS // 2:
                    part = part * (GDN_D ** -0.5)
            qkv_ref[:, j * wide + i * GDN_D:j * wide + (i + 1) * GDN_D] = part.astype(BF16)
    for j in range(GDN_HEADS * GDN_D // wide):
        cols = slice(j * wide, (j + 1) * wide)
        zcols = slice(GDN_QKV_W + j * wide, GDN_QKV_W + (j + 1) * wide)
        gz_ref[:, cols] = _silu(_mm(h, wqkvz_ref[:, zcols], "bf")[keep]).astype(BF16)
    for j in range(SC_WIDTH // wide):
        sc_b, sc_c, sc_h = [_mm(h, wsc_ref[:, i * SC_WIDTH + j * wide:i * SC_WIDTH + (j + 1) * wide], "bf")
                            for i in range(3)]
        cols = slice(j * wide, (j + 1) * wide)
        sc_ref[:, cols] = (sc_b * conv3(sc_c * sc_h, csc_ref.at[:, cols]))[keep].astype(BF16)
    g = _mm(h, wgate_ref[...], "bf")[keep]
    lane = lax.broadcasted_iota(jnp.int32, g.shape, 1)
    gate_ref[...] = jnp.where(lane < 2 * GDN_HEADS, jax.nn.sigmoid(g),
                              -jnp.exp(alog_ref[...]) * _softplus(g + dtb_ref[...]))


def _inproj_even(x, mods, norm_w, w_qkvz, w_sc, w_gate, conv_w, sc_conv_w, alog_vec, dtb_vec, seq_len, mod_spec):
    t = x.shape[0]
    tile = TOKEN_TILE
    assert seq_len & (seq_len - 1) == 0 and t % tile == 0
    per = tile // HALO
    last = t // HALO - 1
    const = lambda i: (0, 0)
    tok = lambda n: pl.BlockSpec((tile, n), lambda i: (i, 0))
    return pl.pallas_call(
        functools.partial(_inproj_even_kernel, seq_len=seq_len),
        grid=(t // tile,),
        in_specs=[
            pl.BlockSpec((HALO, D_MODEL), lambda i: (jnp.maximum(i * per - 1, 0), 0)),
            pl.BlockSpec((tile, D_MODEL), lambda i: (i, 0)),
            pl.BlockSpec((HALO, D_MODEL), lambda i: (jnp.minimum((i + 1) * per, last), 0)),
            mod_spec,
            pl.BlockSpec((1, D_MODEL), const),
            pl.BlockSpec(w_qkvz.shape, const), pl.BlockSpec(w_sc.shape, const), pl.BlockSpec(w_gate.shape, const),
            pl.BlockSpec(conv_w.shape, const), pl.BlockSpec(sc_conv_w.shape, const),
            pl.BlockSpec((1, 128), const), pl.BlockSpec((1, 128), const),
        ],
        out_specs=[tok(GDN_QKV_W), tok(GDN_HEADS * GDN_D), tok(SC_WIDTH), tok(128)],
        out_shape=[jax.ShapeDtypeStruct((t, GDN_QKV_W), BF16), jax.ShapeDtypeStruct((t, GDN_HEADS * GDN_D), BF16),
                   jax.ShapeDtypeStruct((t, SC_WIDTH), BF16), jax.ShapeDtypeStruct((t, 128), F32)],
        compiler_params=pltpu.CompilerParams(dimension_semantics=("parallel",), vmem_limit_bytes=VMEM_LIMIT),
        name="inproj_even",
    )(x, x, x, mods, norm_w.reshape(1, D_MODEL), w_qkvz, w_sc, w_gate, conv_w, sc_conv_w, alog_vec, dtb_vec)


def _mlp_kernel(a_ref, b_ref, x_ref, mod_ref, nw_ref, woa_ref, wob_ref, w1_ref, w2_ref, nf_ref, o_ref, *, final):
    y = _mm(a_ref[...], woa_ref[...], prec="bf") + _mm(b_ref[...], wob_ref[...], prec="bf")
    x1 = x_ref[...] + mod_ref[2:3, :] * y
    h = _rms(x1, nw_ref[...])
    h = (h * (1.0 + mod_ref[4:5, :]) + mod_ref[3:4, :]).astype(BF16)
    acc = jnp.zeros(x1.shape, F32)
    for j in range(D_FF // D_MODEL):
        cols = slice(j * D_MODEL, (j + 1) * D_MODEL)
        u = jnp.maximum(_mm(h, w1_ref[:, cols], prec="bf"), 0.0)
        acc = acc + _mm(u * u, w2_ref[cols, :], prec="bf")
    x2 = x1 + mod_ref[5:6, :] * acc
    if final:
        x2 = _rms(x2, nf_ref[...])
    o_ref[...] = x2


def _outproj_mlp(a, b, x, mods, norm_w, w_out, w1_all, w2_all, layer, norm_final, mod_spec, final):
    t = x.shape[0]
    half = a.shape[1]
    const = lambda i: (0, 0)
    return pl.pallas_call(
        functools.partial(_mlp_kernel, final=final),
        grid=(t // TOKEN_TILE,),
        in_specs=[
            pl.BlockSpec((TOKEN_TILE, half), lambda i: (i, 0)),
            pl.BlockSpec((TOKEN_TILE, half), lambda i: (i, 0)),
            pl.BlockSpec((TOKEN_TILE, D_MODEL), lambda i: (i, 0)),
            mod_spec,
            pl.BlockSpec((1, D_MODEL), const),
            pl.BlockSpec((half, D_MODEL), const),
            pl.BlockSpec((half, D_MODEL), lambda i: (1, 0)),
            pl.BlockSpec((None, D_MODEL, D_FF), lambda i: (layer, 0, 0)),
            pl.BlockSpec((None, D_FF, D_MODEL), lambda i: (layer, 0, 0)),
            pl.BlockSpec((1, D_MODEL), const),
        ],
        out_specs=pl.BlockSpec((TOKEN_TILE, D_MODEL), lambda i: (i, 0)),
        out_shape=jax.ShapeDtypeStruct((t, D_MODEL), F32),
        compiler_params=pltpu.CompilerParams(dimension_semantics=("parallel",), vmem_limit_bytes=VMEM_LIMIT),
        name="outproj_mlp",
    )(a, b, x, mods, norm_w.reshape(1, D_MODEL), w_out, w_out, w1_all, w2_all, norm_final.reshape(1, D_MODEL))


def _gdn_decay_terms(g, rev):
    c = g.shape[0]
    incl = _tri_masks(c, rev)[0]
    before_col = _tri_masks(c, not rev)[0]
    eye = lax.broadcasted_iota(jnp.int32, (c, c), 0) == lax.broadcasted_iota(jnp.int32, (c, c), 1)
    gc_row = jnp.sum(jnp.where(before_col, jnp.broadcast_to(g, (c, c)), 0.0), axis=0, keepdims=True)
    gc_col = jnp.sum(jnp.where(eye, jnp.broadcast_to(gc_row, (c, c)), 0.0), axis=1, keepdims=True)
    decay = jnp.where(incl, jnp.exp(jnp.where(incl, gc_col - gc_row, 0.0)), 0.0)
    g_tot = jnp.sum(g, axis=0, keepdims=True)
    return decay, jnp.exp(gc_col), jnp.exp(g_tot - gc_col), jnp.exp(g_tot)


def _gdn_kernel(*refs, seq_len, seqs, has_init):
    q_ref, k_ref, v_ref, gz_ref, gate_ref, gn_ref = refs[:6]
    pos = 6
    if has_init:
        s0f_ref, s0b_ref = refs[pos:pos + 2]
        pos += 2
    o_ref, sf_ref, sb_ref = refs[pos:pos + 3]
    osum, u_s, w_s, qd_s, kd_s, in_s, ge_s, st = refs[pos + 3:]
    hps = GDN_HEADS_PER_STEP
    head0 = pl.program_id(1) * hps
    ch = GDN_CHUNK
    n_chunks = seq_len // ch
    osum[...] = jnp.zeros(osum.shape, F32)
    hcols = [slice(hh * GDN_D, (hh + 1) * GDN_D) for hh in range(hps)]

    lane = lax.broadcasted_iota(jnp.int32, (ch, 128), 1)

    def pick(rows, col):
        return jnp.sum(jnp.where(lane == col, gate_ref[rows, :], 0.0), axis=1, keepdims=True)

    def solve_group(gi, carry):
        items = []
        for j in range(GDN_GROUP):
            c = gi * GDN_GROUP + j
            rows = pl.ds(pl.multiple_of(c * ch, ch), ch)
            for hh in range(hps):
                items.append(dict(c=c, hh=hh, rows=rows, q=q_ref[rows, hcols[hh]].astype(F32),
                                  k=k_ref[rows, hcols[hh]].astype(F32), v=v_ref[rows, hcols[hh]].astype(F32)))
        kks = [_mm_nt(it["k"], it["k"], "bf") for it in items]
        qks = [_mm_nt(it["q"], it["k"], "bf") for it in items]
        subs = []
        for it, kk, qk in zip(items, kks, qks):
            for d in range(2):
                beta = pick(it["rows"], d * GDN_HEADS + head0 + it["hh"])
                g = pick(it["rows"], 2 * GDN_HEADS + d * GDN_HEADS + head0 + it["hh"])
                decay, e_gc, e_rest, e_tot = _gdn_decay_terms(g, rev=(d == 1))
                strict = _tri_masks(ch, d == 1)[1]
                subs.append(dict(
                    d=d, c=it["c"], hh=it["hh"], rows=it["rows"],
                    a=jnp.where(strict, kk * beta * decay, 0.0),
                    rhs=jnp.concatenate([it["v"] * beta, it["k"] * (beta * e_gc)], axis=1),
                    intra=qk * decay, qd=it["q"] * e_gc, kd=it["k"] * e_rest, ge=e_tot))
        ts = _unit_tri_inv_many([s["a"] for s in subs], "bf")
        uws = [_mm(t, s["rhs"], "bf") for t, s in zip(ts, subs)]
        for s, uw in zip(subs, uws):
            d, hh, rows = s["d"], s["hh"], s["rows"]
            u_s[d, hh, rows, :] = uw[:, :GDN_D]
            w_s[d, hh, rows, :] = uw[:, GDN_D:].astype(BF16)
            qd_s[d, hh, rows, :] = s["qd"].astype(BF16)
            kd_s[d, hh, rows, :] = s["kd"].astype(BF16)
            in_s[d, hh, rows, :] = s["intra"].astype(BF16)
            ge_s[d, hh, pl.ds(pl.multiple_of(s["c"] * 8, 8), 8), :] = jnp.broadcast_to(s["ge"], (8, 128))
        return carry

    lax.fori_loop(0, seqs * n_chunks // GDN_GROUP, solve_group, 0)

    for hh in range(hps):
        for j in range(seqs):
            st[0, hh, j] = s0f_ref[j, hh] if has_init else jnp.zeros((GDN_D, GDN_D), F32)
            st[1, hh, j] = s0b_ref[j, hh] if has_init else jnp.zeros((GDN_D, GDN_D), F32)

    def recur(i, carry):
        cs = []
        for hh in range(hps):
            for j in range(seqs):
                for d in range(2):
                    c = j * n_chunks + (i if d == 0 else n_chunks - 1 - i)
                    cs.append(dict(d=d, hh=hh, j=j, rows=pl.ds(pl.multiple_of(c * ch, ch), ch),
                                   ge=ge_s[d, hh, pl.ds(pl.multiple_of(c * 8, 8), 1), :], s=st[d, hh, j]))
        sbs = [c["s"].astype(BF16) for c in cs]
        wss = [_mm(w_s[c["d"], c["hh"], c["rows"], :], sb, "bf") for c, sb in zip(cs, sbs)]
        qss = [_mm(qd_s[c["d"], c["hh"], c["rows"], :], sb, "bf") for c, sb in zip(cs, sbs)]
        ebs = [(u_s[c["d"], c["hh"], c["rows"], :] - ws).astype(BF16) for c, ws in zip(cs, wss)]
        outs = [qs_ + _mm(in_s[c["d"], c["hh"], c["rows"], :], eb, "bf") for c, qs_, eb in zip(cs, qss, ebs)]
        s_news = [c["s"] * c["ge"] + _mm_tn(kd_s[c["d"], c["hh"], c["rows"], :], eb, "bf")
                  for c, eb in zip(cs, ebs)]
        for c, o, s_new in zip(cs, outs, s_news):
            st[c["d"], c["hh"], c["j"]] = s_new
            osum[c["rows"], hcols[c["hh"]]] += o
        return carry

    lax.fori_loop(0, n_chunks, recur, 0)
    for hh in range(hps):
        for j in range(seqs):
            sf_ref[j, hh] = st[0, hh, j]
            sb_ref[j, hh] = st[1, hh, j]
        o_ref[:, hcols[hh]] = (_rms(osum[:, hcols[hh]], gn_ref[...]) * gz_ref[:, hcols[hh]].astype(F32))


def _gdn_mixer(qkv, gz, gates, n_seq, seq_len, gdn_norm, s0_f, s0_b):
    has_init = s0_f is not None
    hd = GDN_HEADS
    hps = GDN_HEADS_PER_STEP
    hgroups = hd // hps
    seqs = max(1, GDN_BLOCK_ROWS // seq_len)
    rows = seqs * seq_len
    assert n_seq % seqs == 0 and (rows // GDN_CHUNK) % GDN_GROUP == 0 and hd % hps == 0

    def col(block):
        return pl.BlockSpec((rows, hps * GDN_D), lambda s, h, b=block: (s, b * hgroups + h))

    state = pl.BlockSpec((seqs, None, hps, GDN_D, GDN_D), lambda s, h: (s, 0, h, 0, 0))
    in_specs = [col(0), col(1), col(2), col(0), pl.BlockSpec((rows, 128), lambda s, h: (s, 0)),
                pl.BlockSpec((1, 128), lambda s, h: (0, 0))]
    args = [qkv, qkv, qkv, gz, gates, gdn_norm.reshape(1, 128)]
    if has_init:
        in_specs += [state, state]
        args += [s0_f, s0_b]
    t = n_seq * seq_len
    scratch = ([pltpu.VMEM((rows, hps * GDN_D), F32)]
               + [pltpu.VMEM((2, hps, rows, GDN_D), F32)]
               + [pltpu.VMEM((2, hps, rows, GDN_D), BF16) for _ in range(3)]
               + [pltpu.VMEM((2, hps, rows, GDN_CHUNK), BF16),
                  pltpu.VMEM((2, hps, rows // GDN_CHUNK * 8, 128), F32),
                  pltpu.VMEM((2, hps, seqs, GDN_D, GDN_D), F32)])
    return pl.pallas_call(
        functools.partial(_gdn_kernel, seq_len=seq_len, seqs=seqs, has_init=has_init),
        grid=(n_seq // seqs, hgroups),
        in_specs=in_specs,
        out_specs=[pl.BlockSpec((rows, hps * GDN_D), lambda s, h: (s, h)), state, state],
        out_shape=[jax.ShapeDtypeStruct((t, hd * GDN_D), F32),
                   jax.ShapeDtypeStruct((n_seq, 1, hd, GDN_D, GDN_D), F32),
                   jax.ShapeDtypeStruct((n_seq, 1, hd, GDN_D, GDN_D), F32)],
        scratch_shapes=scratch,
        compiler_params=pltpu.CompilerParams(dimension_semantics=("parallel", "parallel"),
                                             vmem_limit_bytes=VMEM_LIMIT),
        name="gdn_mixer",
    )(*args)


def _softmax_pv(scores, values, sink):
    m = sink
    for s in scores:
        m = jnp.maximum(m, jnp.max(s, axis=-1, keepdims=True))
    den = jnp.exp(sink - m)
    acc = None
    for s, v in zip(scores, values):
        e = jnp.exp(s - m)
        den = den + jnp.sum(e, axis=-1, keepdims=True)
        pv = _mm(e, v, prec="bf")
        acc = pv if acc is None else acc + pv
    return acc / den


def _group_sinks(sink_ref, j, rows):
    assert rows & (rows - 1) == 0
    grp = lax.broadcasted_iota(jnp.int32, (ATT_GROUP * rows, 1), 0) >> (rows.bit_length() - 1)
    col = jnp.full(grp.shape, sink_ref[j * ATT_GROUP], F32)
    for gi in range(1, ATT_GROUP):
        col = jnp.where(grp == gi, sink_ref[j * ATT_GROUP + gi], col)
    return col


def _store_group(o_ref, j, o, rows):
    for gi in range(ATT_GROUP):
        hh = j * ATT_GROUP + gi
        o_ref[:, hh * ATT_HD:(hh + 1) * ATT_HD] = o[gi * rows:(gi + 1) * rows]


def _attn_ctx_kernel(sink_ref, p_ref, o_ref, kc_ref, vc_ref):
    scale = ATT_HD ** -0.5
    rows = p_ref.shape[0]
    for j in range(ATT_KV_HEADS):
        k = p_ref[:, ATT_Q_W + j * ATT_HD:ATT_Q_W + (j + 1) * ATT_HD]
        v = p_ref[:, ATT_Q_W + ATT_KV_W + j * ATT_HD:ATT_Q_W + ATT_KV_W + (j + 1) * ATT_HD]
        kc_ref[j] = k
        vc_ref[j] = v
        q = jnp.concatenate([p_ref[:, hh * ATT_HD:(hh + 1) * ATT_HD]
                             for hh in range(j * ATT_GROUP, (j + 1) * ATT_GROUP)], axis=0)
        s = _mm_nt(q, k, prec="bf") * scale
        _store_group(o_ref, j, _softmax_pv([s], [v], _group_sinks(sink_ref, j, rows)), rows)


def _attn_context(proj_att, sink, n_seq, seq_len):
    kv = pl.BlockSpec((None, None, ATT_KV_HEADS, seq_len, ATT_HD), lambda b: (b, 0, 0, 0, 0))
    return pl.pallas_call(
        _attn_ctx_kernel,
        grid=(n_seq,),
        in_specs=[pl.BlockSpec(memory_space=pltpu.SMEM),
                  pl.BlockSpec((seq_len, ATT_W), lambda b: (b, 0))],
        out_specs=[pl.BlockSpec((seq_len, ATT_Q_W), lambda b: (b, 0)), kv, kv],
        out_shape=[jax.ShapeDtypeStruct((n_seq * seq_len, ATT_Q_W), F32),
                   jax.ShapeDtypeStruct((n_seq, 1, ATT_KV_HEADS, seq_len, ATT_HD), F32),
                   jax.ShapeDtypeStruct((n_seq, 1, ATT_KV_HEADS, seq_len, ATT_HD), F32)],
        compiler_params=pltpu.CompilerParams(dimension_semantics=("parallel",)),
        name="attn_context",
    )(sink, proj_att)


def _rope_tables(seq_len):
    pos = np.arange(seq_len)
    half = ATT_HD // 2
    inv = ROPE_BASE ** (-np.arange(0, half, 2, dtype=np.float32) / half)
    ang_r = (pos // GRID_W).astype(np.float32)[:, None] * inv
    ang_c = (pos % GRID_W).astype(np.float32)[:, None] * inv
    cos = np.concatenate([np.cos(ang_r), np.cos(ang_r), np.cos(ang_c), np.cos(ang_c)], axis=1)
    sin = np.concatenate([-np.sin(ang_r), np.sin(ang_r), -np.sin(ang_c), np.sin(ang_c)], axis=1)
    return (jnp.asarray(np.tile(cos, (1, 2)), F32), jnp.asarray(np.tile(sin, (1, 2)), F32))


def _rope(x, cos, sin):
    lane = lax.broadcasted_iota(jnp.int32, x.shape, 1)
    partner = jnp.where((lane & 31) < 16, pltpu.roll(x, 128 - 16, 1), pltpu.roll(x, 16, 1))
    return x * cos + partner * sin


def _attn_lat_kernel(sink_ref, p_ref, ck_ref, cv_ref, cos_ref, sin_ref, o_ref, *, seq_len):
    scale = ATT_HD ** -0.5
    qb = pl.program_id(1)
    span = 3 * ATT_BLOCK
    q0 = pl.multiple_of(qb * ATT_BLOCK, ATT_BLOCK)
    k0 = pl.multiple_of(jnp.clip((qb - 1) * ATT_BLOCK, 0, seq_len - span), ATT_BLOCK)
    qrows = pl.ds(q0, ATT_BLOCK)
    krows = pl.ds(k0, span)
    kwin = _rope(p_ref[krows, ATT_Q_W:ATT_Q_W + ATT_KV_W], cos_ref[krows, :], sin_ref[krows, :])
    vwin = p_ref[krows, ATT_Q_W + ATT_KV_W:ATT_W]
    stacked = ATT_GROUP * ATT_BLOCK
    qpos = q0 + (lax.broadcasted_iota(jnp.int32, (stacked, span), 0) & (ATT_BLOCK - 1))
    kpos = k0 + lax.broadcasted_iota(jnp.int32, (stacked, span), 1)
    valid = jnp.abs(qpos - kpos) <= WINDOW
    cos_q = cos_ref[qrows, :]
    sin_q = sin_ref[qrows, :]
    heads = []
    for pair in range(ATT_HEADS // 2):
        qpair = _rope(p_ref[qrows, pair * 128:(pair + 1) * 128], cos_q, sin_q)
        heads += [qpair[:, :ATT_HD], qpair[:, ATT_HD:]]
    kv = range(ATT_KV_HEADS)
    qs = [jnp.concatenate(heads[j * ATT_GROUP:(j + 1) * ATT_GROUP], axis=0) for j in kv]
    s_locs = [jnp.where(valid, _mm_nt(qs[j], kwin[:, j * ATT_HD:(j + 1) * ATT_HD], prec="bf") * scale, NEG_INF)
              for j in kv]
    s_ctxs = [_mm_nt(qs[j], ck_ref[j], prec="bf") * scale for j in kv]
    outs = [_softmax_pv([s_locs[j], s_ctxs[j]], [vwin[:, j * ATT_HD:(j + 1) * ATT_HD], cv_ref[j]],
                        _group_sinks(sink_ref, j, ATT_BLOCK)) for j in kv]
    for j in kv:
        _store_group(o_ref, j, outs[j], ATT_BLOCK)


def _attn_latent(proj_att, sink, cache_k, cache_v, layer, n_seq, seq_len):
    cos, sin = _rope_tables(seq_len)
    past = cache_k.shape[3]
    nqb = seq_len // ATT_BLOCK
    cache = pl.BlockSpec((None, None, ATT_KV_HEADS, past, ATT_HD), lambda b, q: (b, layer, 0, 0, 0))
    table = pl.BlockSpec((seq_len, 128), lambda b, q: (0, 0))
    return pl.pallas_call(
        functools.partial(_attn_lat_kernel, seq_len=seq_len),
        grid=(n_seq, nqb),
        in_specs=[pl.BlockSpec(memory_space=pltpu.SMEM),
                  pl.BlockSpec((seq_len, ATT_W), lambda b, q: (b, 0)),
                  cache, cache, table, table],
        out_specs=pl.BlockSpec((ATT_BLOCK, ATT_Q_W), lambda b, q: (b * nqb + q, 0)),
        out_shape=jax.ShapeDtypeStruct((n_seq * seq_len, ATT_Q_W), F32),
        compiler_params=pltpu.CompilerParams(dimension_semantics=("parallel", "parallel")),
        name="attn_latent",
    )(sink, proj_att, cache_k, cache_v, cos, sin)


def _pair_masks(n, rev):
    r = lax.broadcasted_iota(jnp.int32, (n, 2 * n), 0)
    c = lax.broadcasted_iota(jnp.int32, (n, 2 * n), 1) & (n - 1)
    return (r <= c, r < c) if rev else (r >= c, r > c)


def _bd(x):
    half = x.shape[1] // 2
    lane = lax.broadcasted_iota(jnp.int32, x.shape, 1)
    zero = jnp.zeros_like(x)
    return jnp.concatenate([jnp.where(lane < half, x, zero), jnp.where(lane >= half, x, zero)], axis=0)


def _bd_mask(n):
    r = lax.broadcasted_iota(jnp.int32, (2 * n, 2 * n), 0)
    c = lax.broadcasted_iota(jnp.int32, (2 * n, 2 * n), 1)
    return (r < n) == (c < n)


def _head_sums(x):
    ones = jnp.where(_bd_mask(RWKV_HD), 1.0, 0.0).astype(BF16)
    return _mm(x, ones, "bf")


def _apply_pairs(tb, x):
    hi, lo = _split_bf16(x)
    return _mm(tb, _bd(hi), "bf") + _mm(tb, _bd(lo), "bf")


def _unit_tri_inv_pairs(mats):
    n = mats[0].shape[0]
    r = lax.broadcasted_iota(jnp.int32, (n, 2 * n), 0)
    c = lax.broadcasted_iota(jnp.int32, (n, 2 * n), 1) & (n - 1)
    eye = jnp.where(r == c, 1.0, 0.0)
    ts = [eye - a * jnp.where((r >> 1) == (c >> 1), 1.0, 0.0) for a in mats]
    abs_ = [a.astype(BF16) for a in mats]
    shift = 1
    while (1 << shift) < n:
        join = jnp.where(((r >> (shift + 1)) == (c >> (shift + 1))) & ((r >> shift) != (c >> shift)),
                         1.0, 0.0).astype(BF16)
        tbs = [t.astype(BF16) for t in ts]
        inner = [_mm(ab * join, _bd(tb), "bf") for ab, tb in zip(abs_, tbs)]
        ts = [t - _mm(tb, _bd(w.astype(BF16)), "bf") for t, tb, w in zip(ts, tbs, inner)]
        shift += 1
    return ts


def _rwkv_chunk_operands(x_ref, prm, sc, c, *, seq_len):
    (mu_ref, w0_ref, wup_ref, a0_ref, aup_ref, gup_ref, kk_ref, ka_ref, rk_ref) = prm
    ch = RWKV_CHUNK
    total = x_ref.shape[0]
    r0 = pl.multiple_of(c * ch, ch)
    rows = pl.ds(r0, ch)
    pos0 = r0 & (seq_len - 1)
    x = x_ref[rows, :]
    prev_row = x_ref[pl.ds(jnp.maximum(r0 - 1, 0), 1), :] * jnp.where(pos0 > 0, 1.0, 0.0)
    next_row = x_ref[pl.ds(jnp.minimum(r0 + ch, total - 1), 1), :] * jnp.where(pos0 + ch < seq_len, 1.0, 0.0)
    xp, xn = _shifted_rows(x, prev_row, next_row)
    mu0, mu1 = mu_ref[0:1, :], mu_ref[1:2, :]
    xs = x * (1.0 - mu0 - mu1) + xp * mu0 + xn * mu1
    r = xs[:, 0:RWKV_W]
    k = xs[:, RWKV_W:2 * RWKV_W]
    v = xs[:, 2 * RWKV_W:3 * RWKV_W]
    lo = 3 * RWKV_W
    p = RWKV_PREC
    gl = xs[:, lo + 4 * RWKV_LORA:lo + 6 * RWKV_LORA]
    sc["gate"][rows, :] = _mm(jax.nn.sigmoid(gl), gup_ref[...], p["gate"])
    sc["v"][rows, :] = v.astype(BF16)
    pairs = [slice(i * PAIR_W, (i + 1) * PAIR_W) for i in range(N_PAIRS)]
    kkv = k * kk_ref[...]
    kaps = []
    for cols in pairs:
        kk_p = kkv[:, cols]
        kaps.append(kk_p * lax.rsqrt(_head_sums(kk_p * kk_p) + 1e-6))
    items = []
    bonus = None
    for d in range(2):
        rev = d == 1
        wl = xs[:, lo + d * RWKV_LORA:lo + (d + 1) * RWKV_LORA]
        al = xs[:, lo + 2 * RWKV_LORA + d * RWKV_LORA:lo + 2 * RWKV_LORA + (d + 1) * RWKV_LORA]
        lw = -DECAY_SCALE * jax.nn.sigmoid(w0_ref[d:d + 1, :] + _mm(jnp.tanh(wl), wup_ref[d], p["lora"]))
        a = jax.nn.sigmoid(a0_ref[d:d + 1, :] + _mm(al, aup_ref[d], p["lora"]))
        k2 = k * (1.0 + (a - 1.0) * ka_ref[...])
        g_inc = _mm(jnp.where(_tri_masks(ch, rev)[0], 1.0, 0.0), lw, p["cumsum"])
        g_tot = jnp.sum(lw, axis=0, keepdims=True)
        e_neg = jnp.exp(-g_inc)
        e_end = jnp.exp(g_tot - g_inc)
        e_exc = jnp.exp(g_inc - lw)
        r_dec = r * jnp.exp(g_inc)
        k_neg = k2 * e_neg
        k_end = k2 * e_end
        sc["dec"][d, pl.ds(pl.multiple_of(c * 8, 8), 8), :] = jnp.broadcast_to(jnp.exp(g_tot), (8, RWKV_W))
        rkr = r * k2 * rk_ref[...]
        bon_d = jnp.concatenate([_head_sums(rkr[:, cols]) for cols in pairs], axis=1) * v
        bonus = bon_d if bonus is None else bonus + bon_d
        for cols, kap in zip(pairs, kaps):
            b_p = kap * a[:, cols]
            items.append(dict(d=d, rows=rows, cols=cols, kap_dec=kap * e_exc[:, cols], r_dec=r_dec[:, cols],
                              b_neg=b_p * e_neg[:, cols], k_neg=k_neg[:, cols],
                              b_end=b_p * e_end[:, cols], k_end=k_end[:, cols], vb=v[:, cols].astype(BF16)))
    sc["bon"][rows, :] = bonus
    return items


def _rwkv_solve(items, sc):
    ch = RWKV_CHUNK
    masks = [_pair_masks(ch, False), _pair_masks(ch, True)]
    ms = [_mm_nt(jnp.concatenate([it["kap_dec"], it["r_dec"]], axis=0),
                 jnp.concatenate([_bd(it["b_neg"].astype(BF16)), _bd(it["k_neg"].astype(BF16))], axis=0), "bf")
          for it in items]
    a_abs = [jnp.where(masks[it["d"]][1], m[:ch, :2 * ch], 0.0) for it, m in zip(items, ms)]
    a_aks = [jnp.where(masks[it["d"]][1], m[:ch, 2 * ch:], 0.0) for it, m in zip(items, ms)]
    a_rbs = [jnp.where(masks[it["d"]][0], m[ch:, :2 * ch], 0.0) for it, m in zip(items, ms)]
    a_rks = [jnp.where(masks[it["d"]][0], m[ch:, 2 * ch:], 0.0) for it, m in zip(items, ms)]
    tbs = [t.astype(BF16) for t in _unit_tri_inv_pairs(a_abs)]
    akvs = [_mm(a_ak, _bd(it["vb"]), "bf") for it, a_ak in zip(items, a_aks)]
    w2s = [_apply_pairs(tb, it["kap_dec"]) for it, tb in zip(items, tbs)]
    u0s = [_apply_pairs(tb, akv) for tb, akv in zip(tbs, akvs)]
    for it, w2, u0, a_rb, a_rk in zip(items, w2s, u0s, a_rbs, a_rks):
        d, rows, cols = it["d"], it["rows"], it["cols"]
        sc["w2"][d, rows, cols] = w2.astype(BF16)
        sc["rd"][d, rows, cols] = it["r_dec"].astype(BF16)
        sc["u0"][d, rows, cols] = u0
        sc["arb"][d, rows, cols] = a_rb.astype(BF16)
        sc["ark"][d, rows, cols] = a_rk.astype(BF16)
        sc["bh"][d, rows, cols] = it["b_end"].astype(BF16)
        sc["kh"][d, rows, cols] = it["k_end"].astype(BF16)


def _rwkv_recur(sc, st, i, *, seq_len, seqs):
    ch = RWKV_CHUNK
    n_chunks = seq_len // ch
    keep = _bd_mask(RWKV_HD)
    cs = []
    for j in range(seqs):
        for d in range(2):
            c = j * n_chunks + (i if d == 0 else n_chunks - 1 - i)
            rows = pl.ds(pl.multiple_of(c * ch, ch), ch)
            dec = sc["dec"][d, pl.ds(pl.multiple_of(c * 8, 8), 1), :]
            for pi in range(N_PAIRS):
                cols = slice(pi * PAIR_W, (pi + 1) * PAIR_W)
                cs.append(dict(d=d, j=j, p=pi, rows=rows, cols=cols, dec=dec[:, cols], s=st[d, j, pi]))
    sbs = [c["s"].astype(BF16) for c in cs]
    lss = [_mm_nt(jnp.concatenate([sc["w2"][c["d"], c["rows"], c["cols"]],
                                   sc["rd"][c["d"], c["rows"], c["cols"]]], axis=0), sb, "bf")
           for c, sb in zip(cs, sbs)]
    ubs = [(-(ls[:ch] + sc["u0"][c["d"], c["rows"], c["cols"]])).astype(BF16) for c, ls in zip(cs, lss)]
    vbs = [sc["v"][c["rows"], c["cols"]] for c in cs]
    ys = [ls[ch:] + _mm(jnp.concatenate([sc["arb"][c["d"], c["rows"], c["cols"]],
                                         sc["ark"][c["d"], c["rows"], c["cols"]]], axis=1),
                        jnp.concatenate([_bd(ub), _bd(vb)], axis=0), "bf")
          for c, ls, ub, vb in zip(cs, lss, ubs, vbs)]
    ups = [_mm_tn(jnp.concatenate([ub, vb], axis=0),
                  jnp.concatenate([sc["bh"][c["d"], c["rows"], c["cols"]],
                                   sc["kh"][c["d"], c["rows"], c["cols"]]], axis=0), "bf")
           for c, ub, vb in zip(cs, ubs, vbs)]
    for c, y, up in zip(cs, ys, ups):
        st[c["d"], c["j"], c["p"]] = c["s"] * c["dec"] + jnp.where(keep, up, 0.0)
        sc["ysum"][c["rows"], c["cols"]] += y


RWKV_SCRATCH = ("ysum", "bon", "gate", "v", "w2", "rd", "u0", "arb", "ark", "bh", "kh", "dec")


def _rwkv_kernel(*refs, seq_len, seqs, has_init):
    x_ref = refs[0]
    prm = refs[1:10]
    lnw_ref, lnb_ref = refs[10:12]
    pos = 12
    if has_init:
        s0f_ref, s0b_ref = refs[pos:pos + 2]
        pos += 2
    o_ref, sf_ref, sb_ref = refs[pos:pos + 3]
    sc = dict(zip(RWKV_SCRATCH, refs[pos + 3:]))
    st = refs[pos + 3 + len(RWKV_SCRATCH)]
    ch = RWKV_CHUNK
    hd = RWKV_HD
    n_chunks = seq_len // ch
    sc["ysum"][...] = jnp.zeros(sc["ysum"].shape, F32)

    def prepare(gi, carry):
        items = []
        for j in range(RWKV_GROUP):
            items += _rwkv_chunk_operands(x_ref, prm, sc, gi * RWKV_GROUP + j, seq_len=seq_len)
        _rwkv_solve(items, sc)
        return carry

    lax.fori_loop(0, seqs * n_chunks // RWKV_GROUP, prepare, 0)

    zero = jnp.zeros((hd, hd), F32)
    for d, s0_ref in enumerate((s0f_ref, s0b_ref) if has_init else (None, None)):
        for j in range(seqs):
            for pi in range(N_PAIRS):
                s_a = s0_ref[j, 2 * pi] if has_init else zero
                s_b = s0_ref[j, 2 * pi + 1] if has_init else zero
                st[d, j, pi] = jnp.concatenate([jnp.concatenate([s_a, zero], axis=1),
                                                jnp.concatenate([zero, s_b], axis=1)], axis=0)

    def recur(i, carry):
        _rwkv_recur(sc, st, i, seq_len=seq_len, seqs=seqs)
        return carry

    lax.fori_loop(0, n_chunks, recur, 0)
    for d, out_ref in enumerate((sf_ref, sb_ref)):
        for j in range(seqs):
            for pi in range(N_PAIRS):
                s = st[d, j, pi]
                out_ref[j, 2 * pi] = s[:hd, :hd]
                out_ref[j, 2 * pi + 1] = s[hd:, hd:]

    tile = RWKV_FINISH_ROWS

    def finish(i, carry):
        rows = pl.ds(pl.multiple_of(i * tile, tile), tile)
        pairs = [slice(pi * PAIR_W, (pi + 1) * PAIR_W) for pi in range(N_PAIRS)]
        ys = [sc["ysum"][rows, cols] for cols in pairs]
        cens = [y - _head_sums(y) * (1.0 / hd) for y in ys]
        vars_ = [_head_sums(cen * cen) * (1.0 / hd) for cen in cens]
        for cols, cen, var in zip(pairs, cens, vars_):
            yn = cen * lax.rsqrt(var + GN_EPS) * lnw_ref[:, cols] + lnb_ref[:, cols]
            o_ref[rows, cols] = (yn + sc["bon"][rows, cols]) * sc["gate"][rows, cols]
        return carry

    lax.fori_loop(0, seqs * seq_len // tile, finish, 0)


def _rwkv_mixer(x_rw, n_seq, seq_len, params, s0_f, s0_b):
    has_init = s0_f is not None
    (mu, w0, w_up, a0, a_up, g_up, k_k, k_a, r_k, ln_w, ln_b) = params
    row = lambda a: a.reshape(1, RWKV_W)
    args = [x_rw, mu, w0, w_up, a0, a_up, g_up, row(k_k), row(k_a), row(r_k), row(ln_w), row(ln_b)]

    def whole(a):
        nd = a.ndim
        return pl.BlockSpec(a.shape, lambda s, nd=nd: (0,) * nd)

    seqs = max(1, RWKV_BLOCK_ROWS // seq_len)
    rows = seqs * seq_len
    assert n_seq % seqs == 0 and (rows // RWKV_CHUNK) % RWKV_GROUP == 0 and seq_len & (seq_len - 1) == 0
    in_specs = [pl.BlockSpec((rows, RWKV_IN), lambda s: (s, 0))] + [whole(a) for a in args[1:]]
    state = pl.BlockSpec((seqs, None, RWKV_HEADS, RWKV_HD, RWKV_HD), lambda s: (s, 0, 0, 0, 0))
    if has_init:
        in_specs += [state, state]
        args += [s0_f, s0_b]
    st_shape = jax.ShapeDtypeStruct((n_seq, 1, RWKV_HEADS, RWKV_HD, RWKV_HD), F32)
    tok = lambda dt: pltpu.VMEM((rows, RWKV_W), dt)
    per_dir = lambda dt: pltpu.VMEM((2, rows, RWKV_W), dt)
    scratch = dict(ysum=tok(F32), bon=tok(F32), gate=tok(F32), v=tok(BF16), w2=per_dir(BF16), rd=per_dir(BF16),
                   u0=per_dir(F32), arb=per_dir(BF16), ark=per_dir(BF16), bh=per_dir(BF16), kh=per_dir(BF16),
                   dec=pltpu.VMEM((2, rows // RWKV_CHUNK * 8, RWKV_W), F32))
    return pl.pallas_call(
        functools.partial(_rwkv_kernel, seq_len=seq_len, seqs=seqs, has_init=has_init),
        grid=(n_seq // seqs,),
        in_specs=in_specs,
        out_specs=[pl.BlockSpec((rows, RWKV_W), lambda s: (s, 0)), state, state],
        out_shape=[jax.ShapeDtypeStruct((n_seq * seq_len, RWKV_W), F32), st_shape, st_shape],
        scratch_shapes=[scratch[name] for name in RWKV_SCRATCH]
        + [pltpu.VMEM((2, seqs, N_PAIRS, PAIR_W, PAIR_W), F32)],
        compiler_params=pltpu.CompilerParams(dimension_semantics=("parallel",), vmem_limit_bytes=VMEM_LIMIT),
        name="rwkv_mixer",
    )(*args)


def kernel(x_prompt, x_sample, state_gdn_fwd, state_gdn_bwd, cache_attn_k, cache_attn_v, state_rwkv_fwd, state_rwkv_bwd, c, c_ctx, mod_w, mod_b, norm_mix, norm_mlp, mlp_w1, mlp_w2, norm_final, ev_w_in, ev_w_out, gdn_conv, gdn_a_log, gdn_dt_bias, gdn_norm, sc_conv, od_w_in, od_w_out, attn_sink, rwkv_mu, rwkv_w0, rwkv_w_up, rwkv_a0, rwkv_a_up, rwkv_g_up, rwkv_k_k, rwkv_k_a, rwkv_r_k, rwkv_ln_w, rwkv_ln_b):
    bp, lp, _ = x_prompt.shape
    bs, ls, _ = x_sample.shape
    depth = mod_w.shape[0]
    c_rows = jnp.concatenate([c_ctx[None, :], c, jnp.zeros((MOD_ROWS - 1 - bs, D_MODEL), F32)], axis=0)
    mods = _modulation(c_rows, mod_w, mod_b)

    assert ls % TOKEN_TILE == 0 and (bp * lp) % TOKEN_TILE == 0
    groups = [
        dict(x=x_prompt.reshape(bp * lp, D_MODEL), n=bp, l=lp, latent=False,
             mod=lambda layer: _mod_spec(layer, 1, 0, 0)),
        dict(x=x_sample.reshape(bs * ls, D_MODEL), n=bs, l=ls, latent=True,
             mod=lambda layer: _mod_spec(layer, ls // TOKEN_TILE, 1, 1)),
    ]
    outs = {}
    w1 = mlp_w1.astype(BF16)
    w2 = mlp_w2.astype(BF16)
    for layer in range(depth):
        final = layer == depth - 1
        if layer % 2 == 0:
            e = layer // 2
            w = ev_w_in[e]
            qkvz = GDN_QKV_W + GDN_HEADS * GDN_D
            n_gate = 4 * GDN_HEADS
            gate_cols = jnp.concatenate([w[:, qkvz:qkvz + n_gate],
                                         jnp.zeros((D_MODEL, EV_IN_PAD - w.shape[1]), F32)], axis=1)
            w_qkvz, w_sc, w_gate = w[:, :qkvz].astype(BF16), w[:, qkvz + n_gate:].astype(BF16), gate_cols.astype(BF16)
            w_out = ev_w_out[e].astype(BF16)
            alog_vec = jnp.zeros((1, 128), F32).at[0, 2 * GDN_HEADS:4 * GDN_HEADS].set(gdn_a_log[e].reshape(-1))
            dtb_vec = jnp.zeros((1, 128), F32).at[0, 2 * GDN_HEADS:4 * GDN_HEADS].set(gdn_dt_bias[e].reshape(-1))
            for grp in groups:
                qkv, gz, sc, gates = _inproj_even(grp["x"], mods, norm_mix[layer], w_qkvz, w_sc, w_gate, gdn_conv[e],
                                                  sc_conv[e], alog_vec, dtb_vec, grp["l"], grp["mod"](layer))
                s0 = (state_gdn_fwd[:, e:e + 1], state_gdn_bwd[:, e:e + 1]) if grp["latent"] else (None, None)
                o, s_f, s_b = _gdn_mixer(qkv, gz, gates, grp["n"], grp["l"], gdn_norm[e], *s0)
                if not grp["latent"]:
                    outs.setdefault("gdn_f", []).append(s_f)
                    outs.setdefault("gdn_b", []).append(s_b)
                grp["x"] = _outproj_mlp(o, sc, grp["x"], mods, norm_mlp[layer], w_out, w1, w2, layer, norm_final,
                                        grp["mod"](layer), final)
        else:
            o_ = layer // 2
            w = od_w_in[o_]
            w_in = [w.astype(BF16)]
            w_out = od_w_out[o_].astype(BF16)
            rw = (rwkv_mu[o_], rwkv_w0[o_], rwkv_w_up[o_], rwkv_a0[o_], rwkv_a_up[o_], rwkv_g_up[o_],
                  rwkv_k_k[o_], rwkv_k_a[o_], rwkv_r_k[o_].reshape(-1), rwkv_ln_w[o_], rwkv_ln_b[o_])
            for grp in groups:
                p_att, x_rw = _inproj(grp["x"], mods, norm_mix[layer], w_in, (ATT_W, RWKV_IN), grp["mod"](layer))
                if grp["latent"]:
                    att = _attn_latent(p_att, attn_sink[o_], cache_attn_k, cache_attn_v, o_, grp["n"], grp["l"])
                    rwo, _, _ = _rwkv_mixer(x_rw, grp["n"], grp["l"], rw,
                                            state_rwkv_fwd[:, o_:o_ + 1], state_rwkv_bwd[:, o_:o_ + 1])
                else:
                    att, kc, vc = _attn_context(p_att, attn_sink[o_], grp["n"], grp["l"])
                    rwo, s_f, s_b = _rwkv_mixer(x_rw, grp["n"], grp["l"], rw, None, None)
                    outs.setdefault("att_k", []).append(kc)
                    outs.setdefault("att_v", []).append(vc)
                    outs.setdefault("rw_f", []).append(s_f)
                    outs.setdefault("rw_b", []).append(s_b)
                grp["x"] = _outproj_mlp(att, rwo, grp["x"], mods, norm_mlp[layer], w_out, w1, w2, layer, norm_final,
                                        grp["mod"](layer), final)
    cat = lambda key: jnp.concatenate(outs[key], axis=1)
    return (groups[0]["x"].reshape(bp, lp, D_MODEL), groups[1]["x"].reshape(bs, ls, D_MODEL),
            cat("gdn_f"), cat("gdn_b"), cat("att_k"), cat("att_v"), cat("rw_f"), cat("rw_b"))
```

```python
import functools

import jax
import jax.numpy as jnp
import numpy as np
from jax import lax
from jax.experimental import pallas as pl
from jax.experimental.pallas import tpu as pltpu

F32 = jnp.float32
BF16 = jnp.bfloat16

D_MODEL = 1024
N_MOD = 6
D_FF = 4 * D_MODEL
NORM_EPS = 1e-6
TOKEN_TILE = 512
HALO = 8
MOD_ROWS = 8

GDN_HEADS = 4
GDN_D = 128
GDN_CHUNK = 128
GDN_GROUP = 8
GDN_HEADS_PER_STEP = 2
GDN_BLOCK_ROWS = 1024
GDN_QKV_W = 3 * GDN_HEADS * GDN_D
SC_WIDTH = 512
EV_IN_PAD = 3712

ATT_HEADS = 8
ATT_KV_HEADS = 2
ATT_GROUP = ATT_HEADS // ATT_KV_HEADS
ATT_HD = 64
ATT_Q_W = ATT_HEADS * ATT_HD
ATT_KV_W = ATT_KV_HEADS * ATT_HD
ATT_W = ATT_Q_W + 2 * ATT_KV_W
WINDOW = 128
ATT_BLOCK = 128
GRID_W = 64
ROPE_BASE = 10000.0
NEG_INF = -1e30

RWKV_HEADS = 8
RWKV_HD = 64
RWKV_W = RWKV_HEADS * RWKV_HD
RWKV_LORA = 64
RWKV_IN = 3 * RWKV_W + 3 * 2 * RWKV_LORA
RWKV_CHUNK = 64
RWKV_GROUP = 4
RWKV_FINISH_ROWS = 256
DECAY_SCALE = float(np.exp(-0.5))
RWKV_BLOCK_ROWS = 1024
PAIR_W = 2 * RWKV_HD
N_PAIRS = RWKV_W // PAIR_W
GN_EPS = 64e-5

VMEM_LIMIT = 56 * 1024 * 1024

RWKV_PREC = dict(lora="bf", gate="bf", cumsum="x2r")


def _split_bf16(a):
    hi = a.astype(BF16)
    return hi, (a - hi.astype(F32)).astype(BF16)


def _dot(a, b, dims, prec):
    dn = (dims, ((), ()))
    one = lambda x, y: lax.dot_general(x, y, dn, preferred_element_type=F32)
    if prec == "x2r":
        ah = a.astype(BF16)
        bh, bl = _split_bf16(b)
        return one(ah, bh) + one(ah, bl)
    assert prec == "bf", prec
    return one(a.astype(BF16), b.astype(BF16))


def _mm(a, b, prec):
    return _dot(a, b, ((1,), (0,)), prec)


def _mm_nt(a, b, prec):
    return _dot(a, b, ((1,), (1,)), prec)


def _mm_tn(a, b, prec):
    return _dot(a, b, ((0,), (0,)), prec)


def _silu(x):
    return x * jax.nn.sigmoid(x)


def _softplus(x):
    return jnp.maximum(x, 0.0) + jnp.log1p(jnp.exp(-jnp.abs(x)))


def _rms(x, w):
    return x * lax.rsqrt(jnp.mean(x * x, axis=-1, keepdims=True) + NORM_EPS) * w


def _tri_masks(n, rev):
    r = lax.broadcasted_iota(jnp.int32, (n, n), 0)
    c = lax.broadcasted_iota(jnp.int32, (n, n), 1)
    if rev:
        return r <= c, r < c
    return r >= c, r > c


def _unit_tri_inv_many(mats, prec):
    assert prec == "bf"
    n = mats[0].shape[0]
    r = lax.broadcasted_iota(jnp.int32, (n, n), 0)
    c = lax.broadcasted_iota(jnp.int32, (n, n), 1)
    eye = jnp.where(r == c, 1.0, 0.0)
    within2 = jnp.where((r >> 1) == (c >> 1), 1.0, 0.0)
    ts = [eye - a * within2 for a in mats]
    abs_ = [a.astype(BF16) for a in mats]
    shift = 1
    while (1 << shift) < n:
        join = jnp.where(((r >> (shift + 1)) == (c >> (shift + 1))) & ((r >> shift) != (c >> shift)),
                         1.0, 0.0).astype(BF16)
        tbs = [t.astype(BF16) for t in ts]
        inner = [_mm(ab * join, tb, prec) for ab, tb in zip(abs_, tbs)]
        ts = [t - _mm(tb, w, prec) for t, tb, w in zip(ts, tbs, inner)]
        shift += 1
    return ts


def _shifted_rows(x, prev_row, next_row):
    n = x.shape[0]
    row = lax.broadcasted_iota(jnp.int32, x.shape, 0)
    xp = jnp.where(row == 0, prev_row, pltpu.roll(x, 1, 0))
    xn = jnp.where(row == n - 1, next_row, pltpu.roll(x, n - 1, 0))
    return xp, xn


def _mod_kernel(c_ref, w_ref, b_ref, o_ref):
    s = _silu(c_ref[...])
    o_ref[...] = _mm(s, w_ref[...], prec="bf") + b_ref[...]


def _modulation(c_rows, mod_w, mod_b):
    depth = mod_w.shape[0]
    nblk = (N_MOD * D_MODEL) // D_MODEL
    out = pl.pallas_call(
        _mod_kernel,
        grid=(depth, nblk),
        in_specs=[
            pl.BlockSpec((MOD_ROWS, D_MODEL), lambda l, j: (0, 0)),
            pl.BlockSpec((None, D_MODEL, D_MODEL), lambda l, j: (l, 0, j)),
            pl.BlockSpec((None, 1, D_MODEL), lambda l, j: (l, 0, j)),
        ],
        out_specs=pl.BlockSpec((None, MOD_ROWS, D_MODEL), lambda l, j: (l, 0, j)),
        out_shape=jax.ShapeDtypeStruct((depth, MOD_ROWS, N_MOD * D_MODEL), F32),
        compiler_params=pltpu.CompilerParams(dimension_semantics=("parallel", "parallel")),
        name="modulation",
    )(c_rows, mod_w, mod_b.reshape(depth, 1, N_MOD * D_MODEL))
    return out.reshape(depth, MOD_ROWS, N_MOD, D_MODEL)


def _mod_spec(layer, tiles_per_seq, row_base, row_step):
    return pl.BlockSpec((None, None, N_MOD, D_MODEL),
                        lambda i: (layer, row_base + (i // tiles_per_seq) * row_step, 0, 0))


def _inproj_kernel(*refs, n_w):
    x_ref, mod_ref, nw_ref = refs[:3]
    w_refs = refs[3:3 + n_w]
    o_refs = refs[3 + n_w:]
    h = _rms(x_ref[...], nw_ref[...])
    h = (h * (1.0 + mod_ref[1:2, :]) + mod_ref[0:1, :]).astype(BF16)
    pieces = [_mm(h, w_ref[...], prec="bf") for w_ref in w_refs]
    y = pieces[0] if n_w == 1 else jnp.concatenate(pieces, axis=1)
    off = 0
    for o_ref in o_refs:
        n = o_ref.shape[-1]
        o_ref[...] = y[:, off:off + n]
        off += n


def _inproj(x, mods, norm_w, ws_bf16, splits, mod_spec):
    t = x.shape[0]
    assert sum(w.shape[1] for w in ws_bf16) == sum(splits) and all(w.shape[1] % 128 == 0 for w in ws_bf16)
    return pl.pallas_call(
        functools.partial(_inproj_kernel, n_w=len(ws_bf16)),
        grid=(t // TOKEN_TILE,),
        in_specs=[
            pl.BlockSpec((TOKEN_TILE, D_MODEL), lambda i: (i, 0)),
            mod_spec,
            pl.BlockSpec((1, D_MODEL), lambda i: (0, 0)),
        ] + [pl.BlockSpec(w.shape, lambda i: (0, 0)) for w in ws_bf16],
        out_specs=[pl.BlockSpec((TOKEN_TILE, n), lambda i: (i, 0)) for n in splits],
        out_shape=[jax.ShapeDtypeStruct((t, n), F32) for n in splits],
        compiler_params=pltpu.CompilerParams(dimension_semantics=("parallel",), vmem_limit_bytes=VMEM_LIMIT),
        name="inproj",
    )(x, mods, norm_w.reshape(1, D_MODEL), *ws_bf16)


def _inproj_even_kernel(xp_ref, x_ref, xn_ref, mod_ref, nw_ref, wqkvz_ref, wsc_ref, wgate_ref, cqkv_ref, csc_ref,
                        alog_ref, dtb_ref, qkv_ref, gz_ref, sc_ref, gate_ref, *, seq_len):
    tile = x_ref.shape[0]
    x = jnp.concatenate([xp_ref[...], x_ref[...], xn_ref[...]], axis=0)
    h = _rms(x, nw_ref[...])
    h = (h * (1.0 + mod_ref[1:2, :]) + mod_ref[0:1, :]).astype(BF16)
    n = tile + 2 * HALO
    first = pl.program_id(0) * tile - HALO
    pos = (first + lax.broadcasted_iota(jnp.int32, (n, 1), 0)) & (seq_len - 1)
    at_start = pos == 0
    at_end = pos == seq_len - 1

    def conv3(v, c_ref):
        vp = jnp.where(at_start, 0.0, pltpu.roll(v, 1, 0))
        vn = jnp.where(at_end, 0.0, pltpu.roll(v, n - 1, 0))
        return vp * c_ref[0:1, :] + v * c_ref[1:2, :] + vn * c_ref[2:3, :]

    keep = slice(HALO, HALO + tile)
    wide = 2 * GDN_D
    for j in range(GDN_QKV_W // wide):
        cols = slice(j * wide, (j + 1) * wide)
        act = _silu(conv3(_mm(h, wqkvz_ref[:, cols], "bf"), cqkv_ref.at[:, cols]))[keep]
        for i in range(2):
            part = act[:, i * GDN_D:(i + 1) * GDN_D]
            if j < 2 * GDN_HEADS // 2:
                part = part * lax.rsqrt(jnp.sum(part * part, axis=-1, keepdims=True) + 1e-6)
                if j < GDN_HEADS // 2:
                    part = part * (GDN_D ** -0.5)
            qkv_ref[:, j * wide + i * GDN_D:j * wide + (i + 1) * GDN_D] = part.astype(BF16)
    for j in range(GDN_HEADS * GDN_D // wide):
        cols = slice(j * wide, (j + 1) * wide)
        zcols = slice(GDN_QKV_W + j * wide, GDN_QKV_W + (j + 1) * wide)
        gz_ref[:, cols] = _silu(_mm(h, wqkvz_ref[:, zcols], "bf")[keep]).astype(BF16)
    for j in range(SC_WIDTH // wide):
        sc_b, sc_c, sc_h = [_mm(h, wsc_ref[:, i * SC_WIDTH + j * wide:i * SC_WIDTH + (j + 1) * wide], "bf")
                            for i in range(3)]
        cols = slice(j * wide, (j + 1) * wide)
        sc_ref[:, cols] = (sc_b * conv3(sc_c * sc_h, csc_ref.at[:, cols]))[keep].astype(BF16)
    g = _mm(h, wgate_ref[...], "bf")[keep]
    lane = lax.broadcasted_iota(jnp.int32, g.shape, 1)
    gate_ref[...] = jnp.where(lane < 2 * GDN_HEADS, jax.nn.sigmoid(g),
                              -jnp.exp(alog_ref[...]) * _softplus(g + dtb_ref[...]))


def _inproj_even(x, mods, norm_w, w_qkvz, w_sc, w_gate, conv_w, sc_conv_w, alog_vec, dtb_vec, seq_len, mod_spec):
    t = x.shape[0]
    tile = TOKEN_TILE
    assert seq_len & (seq_len - 1) == 0 and t % tile == 0
    per = tile // HALO
    last = t // HALO - 1
    const = lambda i: (0, 0)
    tok = lambda n: pl.BlockSpec((tile, n), lambda i: (i, 0))
    return pl.pallas_call(
        functools.partial(_inproj_even_kernel, seq_len=seq_len),
        grid=(t // tile,),
        in_specs=[
            pl.BlockSpec((HALO, D_MODEL), lambda i: (jnp.maximum(i * per - 1, 0), 0)),
            pl.BlockSpec((tile, D_MODEL), lambda i: (i, 0)),
            pl.BlockSpec((HALO, D_MODEL), lambda i: (jnp.minimum((i + 1) * per, last), 0)),
            mod_spec,
            pl.BlockSpec((1, D_MODEL), const),
            pl.BlockSpec(w_qkvz.shape, const), pl.BlockSpec(w_sc.shape, const), pl.BlockSpec(w_gate.shape, const),
            pl.BlockSpec(conv_w.shape, const), pl.BlockSpec(sc_conv_w.shape, const),
            pl.BlockSpec((1, 128), const), pl.BlockSpec((1, 128), const),
        ],
        out_specs=[tok(GDN_QKV_W), tok(GDN_HEADS * GDN_D), tok(SC_WIDTH), tok(128)],
        out_shape=[jax.ShapeDtypeStruct((t, GDN_QKV_W), BF16), jax.ShapeDtypeStruct((t, GDN_HEADS * GDN_D), BF16),
                   jax.ShapeDtypeStruct((t, SC_WIDTH), BF16), jax.ShapeDtypeStruct((t, 128), F32)],
        compiler_params=pltpu.CompilerParams(dimension_semantics=("parallel",), vmem_limit_bytes=VMEM_LIMIT),
        name="inproj_even",
    )(x, x, x, mods, norm_w.reshape(1, D_MODEL), w_qkvz, w_sc, w_gate, conv_w, sc_conv_w, alog_vec, dtb_vec)


def _mlp_kernel(a_ref, b_ref, x_ref, mod_ref, nw_ref, woa_ref, wob_ref, w1_ref, w2_ref, nf_ref, o_ref, *, final):
    y = _mm(a_ref[...], woa_ref[...], prec="bf") + _mm(b_ref[...], wob_ref[...], prec="bf")
    x1 = x_ref[...] + mod_ref[2:3, :] * y
    h = _rms(x1, nw_ref[...])
    h = (h * (1.0 + mod_ref[4:5, :]) + mod_ref[3:4, :]).astype(BF16)
    acc = jnp.zeros(x1.shape, F32)
    for j in range(D_FF // D_MODEL):
        cols = slice(j * D_MODEL, (j + 1) * D_MODEL)
        u = jnp.maximum(_mm(h, w1_ref[:, cols], prec="bf"), 0.0)
        acc = acc + _mm(u * u, w2_ref[cols, :], prec="bf")
    x2 = x1 + mod_ref[5:6, :] * acc
    if final:
        x2 = _rms(x2, nf_ref[...])
    o_ref[...] = x2


def _outproj_mlp(a, b, x, mods, norm_w, w_out, w1, w2, norm_final, mod_spec, final):
    t = x.shape[0]
    half = a.shape[1]
    const = lambda i: (0, 0)
    return pl.pallas_call(
        functools.partial(_mlp_kernel, final=final),
        grid=(t // TOKEN_TILE,),
        in_specs=[
            pl.BlockSpec((TOKEN_TILE, half), lambda i: (i, 0)),
            pl.BlockSpec((TOKEN_TILE, half), lambda i: (i, 0)),
            pl.BlockSpec((TOKEN_TILE, D_MODEL), lambda i: (i, 0)),
            mod_spec,
            pl.BlockSpec((1, D_MODEL), const),
            pl.BlockSpec((half, D_MODEL), const),
            pl.BlockSpec((half, D_MODEL), lambda i: (1, 0)),
            pl.BlockSpec((D_MODEL, D_FF), const),
            pl.BlockSpec((D_FF, D_MODEL), const),
            pl.BlockSpec((1, D_MODEL), const),
        ],
        out_specs=pl.BlockSpec((TOKEN_TILE, D_MODEL), lambda i: (i, 0)),
        out_shape=jax.ShapeDtypeStruct((t, D_MODEL), F32),
        compiler_params=pltpu.CompilerParams(dimension_semantics=("parallel",), vmem_limit_bytes=VMEM_LIMIT),
        name="outproj_mlp",
    )(a, b, x, mods, norm_w.reshape(1, D_MODEL), w_out, w_out, w1, w2, norm_final.reshape(1, D_MODEL))


def _gdn_decay_terms(g, rev):
    c = g.shape[0]
    incl = _tri_masks(c, rev)[0]
    before_col = _tri_masks(c, not rev)[0]
    eye = lax.broadcasted_iota(jnp.int32, (c, c), 0) == lax.broadcasted_iota(jnp.int32, (c, c), 1)
    gc_row = jnp.sum(jnp.where(before_col, jnp.broadcast_to(g, (c, c)), 0.0), axis=0, keepdims=True)
    gc_col = jnp.sum(jnp.where(eye, jnp.broadcast_to(gc_row, (c, c)), 0.0), axis=1, keepdims=True)
    decay = jnp.where(incl, jnp.exp(jnp.where(incl, gc_col - gc_row, 0.0)), 0.0)
    g_tot = jnp.sum(g, axis=0, keepdims=True)
    return decay, jnp.exp(gc_col), jnp.exp(g_tot - gc_col), jnp.exp(g_tot)


def _cast_side_job(arrays_layers, n_steps, step_index):
    in_specs, out_specs, out_shapes = [], [], []
    for arr, layer in arrays_layers:
        _, r, c = arr.shape
        blk = r // n_steps
        assert r % n_steps == 0 and blk % 16 == 0
        in_specs.append(pl.BlockSpec((None, blk, c), lambda *g, layer=layer: (layer, step_index(*g), 0)))
        out_specs.append(pl.BlockSpec((blk, c), lambda *g: (step_index(*g), 0)))
        out_shapes.append(jax.ShapeDtypeStruct((r, c), BF16))
    return in_specs, out_specs, out_shapes


def _gdn_kernel(*refs, seq_len, seqs, has_init, n_cast):
    q_ref, k_ref, v_ref, gz_ref, gate_ref, gn_ref = refs[:6]
    pos = 6
    if has_init:
        s0f_ref, s0b_ref = refs[pos:pos + 2]
        pos += 2
    cast_in = refs[pos:pos + n_cast]
    pos += n_cast
    o_ref, sf_ref, sb_ref = refs[pos:pos + 3]
    cast_out = refs[pos + 3:pos + 3 + n_cast]
    pos += n_cast
    for src, dst in zip(cast_in, cast_out):
        dst[...] = src[...].astype(BF16)
    osum, u_s, w_s, qd_s, kd_s, in_s, ge_s, st = refs[pos + 3:]
    hps = GDN_HEADS_PER_STEP
    head0 = pl.program_id(1) * hps
    ch = GDN_CHUNK
    n_chunks = seq_len // ch
    osum[...] = jnp.zeros(osum.shape, F32)
    hcols = [slice(hh * GDN_D, (hh + 1) * GDN_D) for hh in range(hps)]

    lane = lax.broadcasted_iota(jnp.int32, (ch, 128), 1)

    def pick(rows, col):
        return jnp.sum(jnp.where(lane == col, gate_ref[rows, :], 0.0), axis=1, keepdims=True)

    def solve_group(gi, carry):
        items = []
        for j in range(GDN_GROUP):
            c = gi * GDN_GROUP + j
            rows = pl.ds(pl.multiple_of(c * ch, ch), ch)
            for hh in range(hps):
                items.append(dict(c=c, hh=hh, rows=rows, q=q_ref[rows, hcols[hh]].astype(F32),
                                  k=k_ref[rows, hcols[hh]].astype(F32), v=v_ref[rows, hcols[hh]].astype(F32)))
        kks = [_mm_nt(it["k"], it["k"], "bf") for it in items]
        qks = [_mm_nt(it["q"], it["k"], "bf") for it in items]
        subs = []
        for it, kk, qk in zip(items, kks, qks):
            for d in range(2):
                beta = pick(it["rows"], d * GDN_HEADS + head0 + it["hh"])
                g = pick(it["rows"], 2 * GDN_HEADS + d * GDN_HEADS + head0 + it["hh"])
                decay, e_gc, e_rest, e_tot = _gdn_decay_terms(g, rev=(d == 1))
                strict = _tri_masks(ch, d == 1)[1]
                subs.append(dict(
                    d=d, c=it["c"], hh=it["hh"], rows=it["rows"],
                    a=jnp.where(strict, kk * beta * decay, 0.0),
                    rhs=jnp.concatenate([it["v"] * beta, it["k"] * (beta * e_gc)], axis=1),
                    intra=qk * decay, qd=it["q"] * e_gc, kd=it["k"] * e_rest, ge=e_tot))
        ts = _unit_tri_inv_many([s["a"] for s in subs], "bf")
        uws = [_mm(t, s["rhs"], "bf") for t, s in zip(ts, subs)]
        for s, uw in zip(subs, uws):
            d, hh, rows = s["d"], s["hh"], s["rows"]
            u_s[d, hh, rows, :] = uw[:, :GDN_D]
            w_s[d, hh, rows, :] = uw[:, GDN_D:].astype(BF16)
            qd_s[d, hh, rows, :] = s["qd"].astype(BF16)
            kd_s[d, hh, rows, :] = s["kd"].astype(BF16)
            in_s[d, hh, rows, :] = s["intra"].astype(BF16)
            ge_s[d, hh, pl.ds(pl.multiple_of(s["c"] * 8, 8), 8), :] = jnp.broadcast_to(s["ge"], (8, 128))
        return carry

    lax.fori_loop(0, seqs * n_chunks // GDN_GROUP, solve_group, 0)

    for hh in range(hps):
        for j in range(seqs):
            st[0, hh, j] = s0f_ref[j, hh] if has_init else jnp.zeros((GDN_D, GDN_D), F32)
            st[1, hh, j] = s0b_ref[j, hh] if has_init else jnp.zeros((GDN_D, GDN_D), F32)

    def recur(i, carry):
        cs = []
        for hh in range(hps):
            for j in range(seqs):
                for d in range(2):
                    c = j * n_chunks + (i if d == 0 else n_chunks - 1 - i)
                    cs.append(dict(d=d, hh=hh, j=j, rows=pl.ds(pl.multiple_of(c * ch, ch), ch),
                                   ge=ge_s[d, hh, pl.ds(pl.multiple_of(c * 8, 8), 1), :], s=st[d, hh, j]))
        sbs = [c["s"].astype(BF16) for c in cs]
        wss = [_mm(w_s[c["d"], c["hh"], c["rows"], :], sb, "bf") for c, sb in zip(cs, sbs)]
        qss = [_mm(qd_s[c["d"], c["hh"], c["rows"], :], sb, "bf") for c, sb in zip(cs, sbs)]
        ebs = [(u_s[c["d"], c["hh"], c["rows"], :] - ws).astype(BF16) for c, ws in zip(cs, wss)]
        outs = [qs_ + _mm(in_s[c["d"], c["hh"], c["rows"], :], eb, "bf") for c, qs_, eb in zip(cs, qss, ebs)]
        s_news = [c["s"] * c["ge"] + _mm_tn(kd_s[c["d"], c["hh"], c["rows"], :], eb, "bf")
                  for c, eb in zip(cs, ebs)]
        for c, o, s_new in zip(cs, outs, s_news):
            st[c["d"], c["hh"], c["j"]] = s_new
            osum[c["rows"], hcols[c["hh"]]] += o
        return carry

    lax.fori_loop(0, n_chunks, recur, 0)
    for hh in range(hps):
        for j in range(seqs):
            sf_ref[j, hh] = st[0, hh, j]
            sb_ref[j, hh] = st[1, hh, j]
        o_ref[:, hcols[hh]] = (_rms(osum[:, hcols[hh]], gn_ref[...]) * gz_ref[:, hcols[hh]].astype(F32))


def _gdn_mixer(qkv, gz, gates, n_seq, seq_len, gdn_norm, s0_f, s0_b, cast=()):
    has_init = s0_f is not None
    hd = GDN_HEADS
    hps = GDN_HEADS_PER_STEP
    hgroups = hd // hps
    seqs = max(1, GDN_BLOCK_ROWS // seq_len)
    rows = seqs * seq_len
    assert n_seq % seqs == 0 and (rows // GDN_CHUNK) % GDN_GROUP == 0 and hd % hps == 0

    def col(block):
        return pl.BlockSpec((rows, hps * GDN_D), lambda s, h, b=block: (s, b * hgroups + h))

    state = pl.BlockSpec((seqs, None, hps, GDN_D, GDN_D), lambda s, h: (s, 0, h, 0, 0))
    in_specs = [col(0), col(1), col(2), col(0), pl.BlockSpec((rows, 128), lambda s, h: (s, 0)),
                pl.BlockSpec((1, 128), lambda s, h: (0, 0))]
    args = [qkv, qkv, qkv, gz, gates, gdn_norm.reshape(1, 128)]
    if has_init:
        in_specs += [state, state]
        args += [s0_f, s0_b]
    n_steps = (n_seq // seqs) * hgroups
    cast_in, cast_out, cast_shapes = _cast_side_job(cast, n_steps, lambda s, h: s * hgroups + h)
    in_specs += cast_in
    args += [arr for arr, _ in cast]
    t = n_seq * seq_len
    scratch = ([pltpu.VMEM((rows, hps * GDN_D), F32)]
               + [pltpu.VMEM((2, hps, rows, GDN_D), F32)]
               + [pltpu.VMEM((2, hps, rows, GDN_D), BF16) for _ in range(3)]
               + [pltpu.VMEM((2, hps, rows, GDN_CHUNK), BF16),
                  pltpu.VMEM((2, hps, rows // GDN_CHUNK * 8, 128), F32),
                  pltpu.VMEM((2, hps, seqs, GDN_D, GDN_D), F32)])
    return pl.pallas_call(
        functools.partial(_gdn_kernel, seq_len=seq_len, seqs=seqs, has_init=has_init, n_cast=len(cast)),
        grid=(n_seq // seqs, hgroups),
        in_specs=in_specs,
        out_specs=[pl.BlockSpec((rows, hps * GDN_D), lambda s, h: (s, h)), state, state] + cast_out,
        out_shape=[jax.ShapeDtypeStruct((t, hd * GDN_D), F32),
                   jax.ShapeDtypeStruct((n_seq, 1, hd, GDN_D, GDN_D), F32),
                   jax.ShapeDtypeStruct((n_seq, 1, hd, GDN_D, GDN_D), F32)] + cast_shapes,
        scratch_shapes=scratch,
        compiler_params=pltpu.CompilerParams(dimension_semantics=("parallel", "parallel"),
                                             vmem_limit_bytes=VMEM_LIMIT),
        name="gdn_mixer",
    )(*args)


def _softmax_pv(scores, values, sink):
    m = sink
    for s in scores:
        m = jnp.maximum(m, jnp.max(s, axis=-1, keepdims=True))
    den = jnp.exp(sink - m)
    acc = None
    for s, v in zip(scores, values):
        e = jnp.exp(s - m)
        den = den + jnp.sum(e, axis=-1, keepdims=True)
        pv = _mm(e, v, prec="bf")
        acc = pv if acc is None else acc + pv
    return acc / den


def _group_sinks(sink_ref, j, rows):
    assert rows & (rows - 1) == 0
    grp = lax.broadcasted_iota(jnp.int32, (ATT_GROUP * rows, 1), 0) >> (rows.bit_length() - 1)
    col = jnp.full(grp.shape, sink_ref[j * ATT_GROUP], F32)
    for gi in range(1, ATT_GROUP):
        col = jnp.where(grp == gi, sink_ref[j * ATT_GROUP + gi], col)
    return col


def _store_group(o_ref, j, o, rows):
    for gi in range(ATT_GROUP):
        hh = j * ATT_GROUP + gi
        o_ref[:, hh * ATT_HD:(hh + 1) * ATT_HD] = o[gi * rows:(gi + 1) * rows]


def _attn_ctx_kernel(*refs, n_cast):
    sink_ref, p_ref = refs[:2]
    cast_in = refs[2:2 + n_cast]
    o_ref, kc_ref, vc_ref = refs[2 + n_cast:5 + n_cast]
    for src, dst in zip(cast_in, refs[5 + n_cast:]):
        dst[...] = src[...].astype(BF16)
    scale = ATT_HD ** -0.5
    rows = p_ref.shape[0]
    for j in range(ATT_KV_HEADS):
        k = p_ref[:, ATT_Q_W + j * ATT_HD:ATT_Q_W + (j + 1) * ATT_HD]
        v = p_ref[:, ATT_Q_W + ATT_KV_W + j * ATT_HD:ATT_Q_W + ATT_KV_W + (j + 1) * ATT_HD]
        kc_ref[j] = k
        vc_ref[j] = v
        q = jnp.concatenate([p_ref[:, hh * ATT_HD:(hh + 1) * ATT_HD]
                             for hh in range(j * ATT_GROUP, (j + 1) * ATT_GROUP)], axis=0)
        s = _mm_nt(q, k, prec="bf") * scale
        _store_group(o_ref, j, _softmax_pv([s], [v], _group_sinks(sink_ref, j, rows)), rows)


def _attn_context(proj_att, sink, n_seq, seq_len, cast=()):
    kv = pl.BlockSpec((None, None, ATT_KV_HEADS, seq_len, ATT_HD), lambda b: (b, 0, 0, 0, 0))
    cast_in, cast_out, cast_shapes = _cast_side_job(cast, n_seq, lambda b: b)
    return pl.pallas_call(
        functools.partial(_attn_ctx_kernel, n_cast=len(cast)),
        grid=(n_seq,),
        in_specs=[pl.BlockSpec(memory_space=pltpu.SMEM),
                  pl.BlockSpec((seq_len, ATT_W), lambda b: (b, 0))] + cast_in,
        out_specs=[pl.BlockSpec((seq_len, ATT_Q_W), lambda b: (b, 0)), kv, kv] + cast_out,
        out_shape=[jax.ShapeDtypeStruct((n_seq * seq_len, ATT_Q_W), F32),
                   jax.ShapeDtypeStruct((n_seq, 1, ATT_KV_HEADS, seq_len, ATT_HD), F32),
                   jax.ShapeDtypeStruct((n_seq, 1, ATT_KV_HEADS, seq_len, ATT_HD), F32)] + cast_shapes,
        compiler_params=pltpu.CompilerParams(dimension_semantics=("parallel",), vmem_limit_bytes=VMEM_LIMIT),
        name="attn_context",
    )(sink, proj_att, *[arr for arr, _ in cast])


def _rope_tables(seq_len):
    pos = np.arange(seq_len)
    half = ATT_HD // 2
    inv = ROPE_BASE ** (-np.arange(0, half, 2, dtype=np.float32) / half)
    ang_r = (pos // GRID_W).astype(np.float32)[:, None] * inv
    ang_c = (pos % GRID_W).astype(np.float32)[:, None] * inv
    cos = np.concatenate([np.cos(ang_r), np.cos(ang_r), np.cos(ang_c), np.cos(ang_c)], axis=1)
    sin = np.concatenate([-np.sin(ang_r), np.sin(ang_r), -np.sin(ang_c), np.sin(ang_c)], axis=1)
    return (jnp.asarray(np.tile(cos, (1, 2)), F32), jnp.asarray(np.tile(sin, (1, 2)), F32))


def _rope(x, cos, sin):
    lane = lax.broadcasted_iota(jnp.int32, x.shape, 1)
    partner = jnp.where((lane & 31) < 16, pltpu.roll(x, 128 - 16, 1), pltpu.roll(x, 16, 1))
    return x * cos + partner * sin


def _attn_lat_kernel(sink_ref, p_ref, ck_ref, cv_ref, cos_ref, sin_ref, o_ref, *, seq_len):
    scale = ATT_HD ** -0.5
    qb = pl.program_id(1)
    span = 3 * ATT_BLOCK
    q0 = pl.multiple_of(qb * ATT_BLOCK, ATT_BLOCK)
    k0 = pl.multiple_of(jnp.clip((qb - 1) * ATT_BLOCK, 0, seq_len - span), ATT_BLOCK)
    qrows = pl.ds(q0, ATT_BLOCK)
    krows = pl.ds(k0, span)
    kwin = _rope(p_ref[krows, ATT_Q_W:ATT_Q_W + ATT_KV_W], cos_ref[krows, :], sin_ref[krows, :])
    vwin = p_ref[krows, ATT_Q_W + ATT_KV_W:ATT_W]
    stacked = ATT_GROUP * ATT_BLOCK
    qpos = q0 + (lax.broadcasted_iota(jnp.int32, (stacked, span), 0) & (ATT_BLOCK - 1))
    kpos = k0 + lax.broadcasted_iota(jnp.int32, (stacked, span), 1)
    valid = jnp.abs(qpos - kpos) <= WINDOW
    cos_q = cos_ref[qrows, :]
    sin_q = sin_ref[qrows, :]
    heads = []
    for pair in range(ATT_HEADS // 2):
        qpair = _rope(p_ref[qrows, pair * 128:(pair + 1) * 128], cos_q, sin_q)
        heads += [qpair[:, :ATT_HD], qpair[:, ATT_HD:]]
    kv = range(ATT_KV_HEADS)
    qs = [jnp.concatenate(heads[j * ATT_GROUP:(j + 1) * ATT_GROUP], axis=0) for j in kv]
    s_locs = [jnp.where(valid, _mm_nt(qs[j], kwin[:, j * ATT_HD:(j + 1) * ATT_HD], prec="bf") * scale, NEG_INF)
              for j in kv]
    s_ctxs = [_mm_nt(qs[j], ck_ref[j], prec="bf") * scale for j in kv]
    outs = [_softmax_pv([s_locs[j], s_ctxs[j]], [vwin[:, j * ATT_HD:(j + 1) * ATT_HD], cv_ref[j]],
                        _group_sinks(sink_ref, j, ATT_BLOCK)) for j in kv]
    for j in kv:
        _store_group(o_ref, j, outs[j], ATT_BLOCK)


def _attn_latent(proj_att, sink, cache_k, cache_v, layer, n_seq, seq_len):
    cos, sin = _rope_tables(seq_len)
    past = cache_k.shape[3]
    nqb = seq_len // ATT_BLOCK
    cache = pl.BlockSpec((None, None, ATT_KV_HEADS, past, ATT_HD), lambda b, q: (b, layer, 0, 0, 0))
    table = pl.BlockSpec((seq_len, 128), lambda b, q: (0, 0))
    return pl.pallas_call(
        functools.partial(_attn_lat_kernel, seq_len=seq_len),
        grid=(n_seq, nqb),
        in_specs=[pl.BlockSpec(memory_space=pltpu.SMEM),
                  pl.BlockSpec((seq_len, ATT_W), lambda b, q: (b, 0)),
                  cache, cache, table, table],
        out_specs=pl.BlockSpec((ATT_BLOCK, ATT_Q_W), lambda b, q: (b * nqb + q, 0)),
        out_shape=jax.ShapeDtypeStruct((n_seq * seq_len, ATT_Q_W), F32),
        compiler_params=pltpu.CompilerParams(dimension_semantics=("parallel", "parallel")),
        name="attn_latent",
    )(sink, proj_att, cache_k, cache_v, cos, sin)


def _pair_masks(n, rev):
    r = lax.broadcasted_iota(jnp.int32, (n, 2 * n), 0)
    c = lax.broadcasted_iota(jnp.int32, (n, 2 * n), 1) & (n - 1)
    return (r <= c, r < c) if rev else (r >= c, r > c)


def _bd(x):
    half = x.shape[1] // 2
    lane = lax.broadcasted_iota(jnp.int32, x.shape, 1)
    zero = jnp.zeros_like(x)
    return jnp.concatenate([jnp.where(lane < half, x, zero), jnp.where(lane >= half, x, zero)], axis=0)


def _bd_mask(n):
    r = lax.broadcasted_iota(jnp.int32, (2 * n, 2 * n), 0)
    c = lax.broadcasted_iota(jnp.int32, (2 * n, 2 * n), 1)
    return (r < n) == (c < n)


def _head_sums(x):
    ones = jnp.where(_bd_mask(RWKV_HD), 1.0, 0.0).astype(BF16)
    return _mm(x, ones, "bf")


def _apply_pairs(tb, x):
    hi, lo = _split_bf16(x)
    return _mm(tb, _bd(hi), "bf") + _mm(tb, _bd(lo), "bf")


def _unit_tri_inv_pairs(mats):
    n = mats[0].shape[0]
    r = lax.broadcasted_iota(jnp.int32, (n, 2 * n), 0)
    c = lax.broadcasted_iota(jnp.int32, (n, 2 * n), 1) & (n - 1)
    eye = jnp.where(r == c, 1.0, 0.0)
    within2 = jnp.where((r >> 1) == (c >> 1), 1.0, 0.0)
    ts = [eye - a * within2 for a in mats]
    abs_ = [a.astype(BF16) for a in mats]
    shift = 1
    while (1 << shift) < n:
        join = jnp.where(((r >> (shift + 1)) == (c >> (shift + 1))) & ((r >> shift) != (c >> shift)),
                         1.0, 0.0).astype(BF16)
        tbs = [t.astype(BF16) for t in ts]
        inner = [_mm(ab * join, _bd(tb), "bf") for ab, tb in zip(abs_, tbs)]
        ts = [t - _mm(tb, _bd(w.astype(BF16)), "bf") for t, tb, w in zip(ts, tbs, inner)]
        shift += 1
    return ts


def _rwkv_chunk_operands(x_ref, prm, sc, c, *, seq_len):
    (mu_ref, w0_ref, wup_ref, a0_ref, aup_ref, gup_ref, kk_ref, ka_ref, rk_ref) = prm
    ch = RWKV_CHUNK
    total = x_ref.shape[0]
    r0 = pl.multiple_of(c * ch, ch)
    rows = pl.ds(r0, ch)
    pos0 = r0 & (seq_len - 1)
    x = x_ref[rows, :]
    prev_row = x_ref[pl.ds(jnp.maximum(r0 - 1, 0), 1), :] * jnp.where(pos0 > 0, 1.0, 0.0)
    next_row = x_ref[pl.ds(jnp.minimum(r0 + ch, total - 1), 1), :] * jnp.where(pos0 + ch < seq_len, 1.0, 0.0)
    xp, xn = _shifted_rows(x, prev_row, next_row)
    mu0, mu1 = mu_ref[0:1, :], mu_ref[1:2, :]
    xs = x * (1.0 - mu0 - mu1) + xp * mu0 + xn * mu1
    r = xs[:, 0:RWKV_W]
    k = xs[:, RWKV_W:2 * RWKV_W]
    v = xs[:, 2 * RWKV_W:3 * RWKV_W]
    lo = 3 * RWKV_W
    p = RWKV_PREC
    gl = xs[:, lo + 4 * RWKV_LORA:lo + 6 * RWKV_LORA]
    sc["gate"][rows, :] = _mm(jax.nn.sigmoid(gl), gup_ref[...], p["gate"])
    sc["v"][rows, :] = v.astype(BF16)
    pairs = [slice(i * PAIR_W, (i + 1) * PAIR_W) for i in range(N_PAIRS)]
    kkv = k * kk_ref[...]
    kaps = []
    for cols in pairs:
        kk_p = kkv[:, cols]
        kaps.append(kk_p * lax.rsqrt(_head_sums(kk_p * kk_p) + 1e-6))
    items = []
    bonus = None
    for d in range(2):
        rev = d == 1
        wl = xs[:, lo + d * RWKV_LORA:lo + (d + 1) * RWKV_LORA]
        al = xs[:, lo + 2 * RWKV_LORA + d * RWKV_LORA:lo + 2 * RWKV_LORA + (d + 1) * RWKV_LORA]
        lw = -DECAY_SCALE * jax.nn.sigmoid(w0_ref[d:d + 1, :] + _mm(jnp.tanh(wl), wup_ref[d], p["lora"]))
        a = jax.nn.sigmoid(a0_ref[d:d + 1, :] + _mm(al, aup_ref[d], p["lora"]))
        k2 = k * (1.0 + (a - 1.0) * ka_ref[...])
        g_inc = _mm(jnp.where(_tri_masks(ch, rev)[0], 1.0, 0.0), lw, p["cumsum"])
        g_tot = jnp.sum(lw, axis=0, keepdims=True)
        e_neg = jnp.exp(-g_inc)
        e_end = jnp.exp(g_tot - g_inc)
        e_exc = jnp.exp(g_inc - lw)
        r_dec = r * jnp.exp(g_inc)
        k_neg = k2 * e_neg
        k_end = k2 * e_end
        sc["dec"][d, pl.ds(pl.multiple_of(c * 8, 8), 8), :] = jnp.broadcast_to(jnp.exp(g_tot), (8, RWKV_W))
        rkr = r * k2 * rk_ref[...]
        bon_d = jnp.concatenate([_head_sums(rkr[:, cols]) for cols in pairs], axis=1) * v
        bonus = bon_d if bonus is None else bonus + bon_d
        for cols, kap in zip(pairs, kaps):
            b_p = kap * a[:, cols]
            items.append(dict(d=d, rows=rows, cols=cols, kap_dec=kap * e_exc[:, cols], r_dec=r_dec[:, cols],
                              b_neg=b_p * e_neg[:, cols], k_neg=k_neg[:, cols],
                              b_end=b_p * e_end[:, cols], k_end=k_end[:, cols], vb=v[:, cols].astype(BF16)))
    sc["bon"][rows, :] = bonus
    return items


def _rwkv_solve(items, sc):
    ch = RWKV_CHUNK
    masks = [_pair_masks(ch, False), _pair_masks(ch, True)]
    ms = [_mm_nt(jnp.concatenate([it["kap_dec"], it["r_dec"]], axis=0),
                 jnp.concatenate([_bd(it["b_neg"].astype(BF16)), _bd(it["k_neg"].astype(BF16))], axis=0), "bf")
          for it in items]
    a_abs = [jnp.where(masks[it["d"]][1], m[:ch, :2 * ch], 0.0) for it, m in zip(items, ms)]
    a_aks = [jnp.where(masks[it["d"]][1], m[:ch, 2 * ch:], 0.0) for it, m in zip(items, ms)]
    a_rbs = [jnp.where(masks[it["d"]][0], m[ch:, :2 * ch], 0.0) for it, m in zip(items, ms)]
    a_rks = [jnp.where(masks[it["d"]][0], m[ch:, 2 * ch:], 0.0) for it, m in zip(items, ms)]
    tbs = [t.astype(BF16) for t in _unit_tri_inv_pairs(a_abs)]
    akvs = [_mm(a_ak, _bd(it["vb"]), "bf") for it, a_ak in zip(items, a_aks)]
    w2s = [_apply_pairs(tb, it["kap_dec"]) for it, tb in zip(items, tbs)]
    u0s = [_apply_pairs(tb, akv) for tb, akv in zip(tbs, akvs)]
    for it, w2, u0, a_rb, a_rk in zip(items, w2s, u0s, a_rbs, a_rks):
        d, rows, cols = it["d"], it["rows"], it["cols"]
        sc["w2"][d, rows, cols] = w2.astype(BF16)
        sc["rd"][d, rows, cols] = it["r_dec"].astype(BF16)
        sc["u0"][d, rows, cols] = u0
        sc["arb"][d, rows, cols] = a_rb.astype(BF16)
        sc["ark"][d, rows, cols] = a_rk.astype(BF16)
        sc["bh"][d, rows, cols] = it["b_end"].astype(BF16)
        sc["kh"][d, rows, cols] = it["k_end"].astype(BF16)


def _rwkv_recur(sc, st, i, *, seq_len, seqs):
    ch = RWKV_CHUNK
    n_chunks = seq_len // ch
    keep = _bd_mask(RWKV_HD)
    cs = []
    for j in range(seqs):
        for d in range(2):
            c = j * n_chunks + (i if d == 0 else n_chunks - 1 - i)
            rows = pl.ds(pl.multiple_of(c * ch, ch), ch)
            dec = sc["dec"][d, pl.ds(pl.multiple_of(c * 8, 8), 1), :]
            for pi in range(N_PAIRS):
                cols = slice(pi * PAIR_W, (pi + 1) * PAIR_W)
                cs.append(dict(d=d, j=j, p=pi, rows=rows, cols=cols, dec=dec[:, cols], s=st[d, j, pi]))
    sbs = [c["s"].astype(BF16) for c in cs]
    lss = [_mm_nt(jnp.concatenate([sc["w2"][c["d"], c["rows"], c["cols"]],
                                   sc["rd"][c["d"], c["rows"], c["cols"]]], axis=0), sb, "bf")
           for c, sb in zip(cs, sbs)]
    ubs = [(-(ls[:ch] + sc["u0"][c["d"], c["rows"], c["cols"]])).astype(BF16) for c, ls in zip(cs, lss)]
    vbs = [sc["v"][c["rows"], c["cols"]] for c in cs]
    ys = [ls[ch:] + _mm(jnp.concatenate([sc["arb"][c["d"], c["rows"], c["cols"]],
                                         sc["ark"][c["d"], c["rows"], c["cols"]]], axis=1),
                        jnp.concatenate([_bd(ub), _bd(vb)], axis=0), "bf")
          for c, ls, ub, vb in zip(cs, lss, ubs, vbs)]
    ups = [_mm_tn(jnp.concatenate([ub, vb], axis=0),
                  jnp.concatenate([sc["bh"][c["d"], c["rows"], c["cols"]],
                                   sc["kh"][c["d"], c["rows"], c["cols"]]], axis=0), "bf")
           for c, ub, vb in zip(cs, ubs, vbs)]
    for c, y, up in zip(cs, ys, ups):
        st[c["d"], c["j"], c["p"]] = c["s"] * c["dec"] + jnp.where(keep, up, 0.0)
        sc["ysum"][c["rows"], c["cols"]] += y


RWKV_SCRATCH = ("ysum", "bon", "gate", "v", "w2", "rd", "u0", "arb", "ark", "bh", "kh", "dec")


def _rwkv_kernel(*refs, seq_len, seqs, has_init):
    x_ref = refs[0]
    prm = refs[1:10]
    lnw_ref, lnb_ref = refs[10:12]
    pos = 12
    if has_init:
        s0f_ref, s0b_ref = refs[pos:pos + 2]
        pos += 2
    o_ref, sf_ref, sb_ref = refs[pos:pos + 3]
    sc = dict(zip(RWKV_SCRATCH, refs[pos + 3:]))
    st = refs[pos + 3 + len(RWKV_SCRATCH)]
    ch = RWKV_CHUNK
    hd = RWKV_HD
    n_chunks = seq_len // ch
    sc["ysum"][...] = jnp.zeros(sc["ysum"].shape, F32)

    def prepare(gi, carry):
        items = []
        for j in range(RWKV_GROUP):
            items += _rwkv_chunk_operands(x_ref, prm, sc, gi * RWKV_GROUP + j, seq_len=seq_len)
        _rwkv_solve(items, sc)
        return carry

    lax.fori_loop(0, seqs * n_chunks // RWKV_GROUP, prepare, 0)

    zero = jnp.zeros((hd, hd), F32)
    for d, s0_ref in enumerate((s0f_ref, s0b_ref) if has_init else (None, None)):
        for j in range(seqs):
            for pi in range(N_PAIRS):
                s_a = s0_ref[j, 2 * pi] if has_init else zero
                s_b = s0_ref[j, 2 * pi + 1] if has_init else zero
                st[d, j, pi] = jnp.concatenate([jnp.concatenate([s_a, zero], axis=1),
                                                jnp.concatenate([zero, s_b], axis=1)], axis=0)

    def recur(i, carry):
        _rwkv_recur(sc, st, i, seq_len=seq_len, seqs=seqs)
        return carry

    lax.fori_loop(0, n_chunks, recur, 0)
    for d, out_ref in enumerate((sf_ref, sb_ref)):
        for j in range(seqs):
            for pi in range(N_PAIRS):
                s = st[d, j, pi]
                out_ref[j, 2 * pi] = s[:hd, :hd]
                out_ref[j, 2 * pi + 1] = s[hd:, hd:]

    tile = RWKV_FINISH_ROWS

    def finish(i, carry):
        rows = pl.ds(pl.multiple_of(i * tile, tile), tile)
        pairs = [slice(pi * PAIR_W, (pi + 1) * PAIR_W) for pi in range(N_PAIRS)]
        ys = [sc["ysum"][rows, cols] for cols in pairs]
        cens = [y - _head_sums(y) * (1.0 / hd) for y in ys]
        vars_ = [_head_sums(cen * cen) * (1.0 / hd) for cen in cens]
        for cols, cen, var in zip(pairs, cens, vars_):
            yn = cen * lax.rsqrt(var + GN_EPS) * lnw_ref[:, cols] + lnb_ref[:, cols]
            o_ref[rows, cols] = (yn + sc["bon"][rows, cols]) * sc["gate"][rows, cols]
        return carry

    lax.fori_loop(0, seqs * seq_len // tile, finish, 0)


def _rwkv_mixer(x_rw, n_seq, seq_len, params, s0_f, s0_b):
    has_init = s0_f is not None
    (mu, w0, w_up, a0, a_up, g_up, k_k, k_a, r_k, ln_w, ln_b) = params
    row = lambda a: a.reshape(1, RWKV_W)
    args = [x_rw, mu, w0, w_up, a0, a_up, g_up, row(k_k), row(k_a), row(r_k), row(ln_w), row(ln_b)]

    def whole(a):
        nd = a.ndim
        return pl.BlockSpec(a.shape, lambda s, nd=nd: (0,) * nd)

    seqs = max(1, RWKV_BLOCK_ROWS // seq_len)
    rows = seqs * seq_len
    assert n_seq % seqs == 0 and (rows // RWKV_CHUNK) % RWKV_GROUP == 0 and seq_len & (seq_len - 1) == 0
    in_specs = [pl.BlockSpec((rows, RWKV_IN), lambda s: (s, 0))] + [whole(a) for a in args[1:]]
    state = pl.BlockSpec((seqs, None, RWKV_HEADS, RWKV_HD, RWKV_HD), lambda s: (s, 0, 0, 0, 0))
    if has_init:
        in_specs += [state, state]
        args += [s0_f, s0_b]
    st_shape = jax.ShapeDtypeStruct((n_seq, 1, RWKV_HEADS, RWKV_HD, RWKV_HD), F32)
    tok = lambda dt: pltpu.VMEM((rows, RWKV_W), dt)
    per_dir = lambda dt: pltpu.VMEM((2, rows, RWKV_W), dt)
    scratch = dict(ysum=tok(F32), bon=tok(F32), gate=tok(F32), v=tok(BF16), w2=per_dir(BF16), rd=per_dir(BF16),
                   u0=per_dir(F32), arb=per_dir(BF16), ark=per_dir(BF16), bh=per_dir(BF16), kh=per_dir(BF16),
                   dec=pltpu.VMEM((2, rows // RWKV_CHUNK * 8, RWKV_W), F32))
    return pl.pallas_call(
        functools.partial(_rwkv_kernel, seq_len=seq_len, seqs=seqs, has_init=has_init),
        grid=(n_seq // seqs,),
        in_specs=in_specs,
        out_specs=[pl.BlockSpec((rows, RWKV_W), lambda s: (s, 0)), state, state],
        out_shape=[jax.ShapeDtypeStruct((n_seq * seq_len, RWKV_W), F32), st_shape, st_shape],
        scratch_shapes=[scratch[name] for name in RWKV_SCRATCH]
        + [pltpu.VMEM((2, seqs, N_PAIRS, PAIR_W, PAIR_W), F32)],
        compiler_params=pltpu.CompilerParams(dimension_semantics=("parallel",), vmem_limit_bytes=VMEM_LIMIT),
        name="rwkv_mixer",
    )(*args)


def kernel(x_prompt, x_sample, state_gdn_fwd, state_gdn_bwd, cache_attn_k, cache_attn_v, state_rwkv_fwd, state_rwkv_bwd, c, c_ctx, mod_w, mod_b, norm_mix, norm_mlp, mlp_w1, mlp_w2, norm_final, ev_w_in, ev_w_out, gdn_conv, gdn_a_log, gdn_dt_bias, gdn_norm, sc_conv, od_w_in, od_w_out, attn_sink, rwkv_mu, rwkv_w0, rwkv_w_up, rwkv_a0, rwkv_a_up, rwkv_g_up, rwkv_k_k, rwkv_k_a, rwkv_r_k, rwkv_ln_w, rwkv_ln_b):
    bp, lp, _ = x_prompt.shape
    bs, ls, _ = x_sample.shape
    depth = mod_w.shape[0]
    c_rows = jnp.concatenate([c_ctx[None, :], c, jnp.zeros((MOD_ROWS - 1 - bs, D_MODEL), F32)], axis=0)
    mods = _modulation(c_rows, mod_w, mod_b)

    assert ls % TOKEN_TILE == 0 and (bp * lp) % TOKEN_TILE == 0
    groups = [
        dict(x=x_prompt.reshape(bp * lp, D_MODEL), n=bp, l=lp, latent=False,
             mod=lambda layer: _mod_spec(layer, 1, 0, 0)),
        dict(x=x_sample.reshape(bs * ls, D_MODEL), n=bs, l=ls, latent=True,
             mod=lambda layer: _mod_spec(layer, ls // TOKEN_TILE, 1, 1)),
    ]
    outs = {}
    for layer in range(depth):
        final = layer == depth - 1
        mlp_cast = ((mlp_w1, layer), (mlp_w2, layer))
        if layer % 2 == 0:
            e = layer // 2
            w = ev_w_in[e]
            qkvz = GDN_QKV_W + GDN_HEADS * GDN_D
            n_gate = 4 * GDN_HEADS
            gate_cols = jnp.concatenate([w[:, qkvz:qkvz + n_gate],
                                         jnp.zeros((D_MODEL, EV_IN_PAD - w.shape[1]), F32)], axis=1)
            w_qkvz, w_sc, w_gate = w[:, :qkvz].astype(BF16), w[:, qkvz + n_gate:].astype(BF16), gate_cols.astype(BF16)
            w_out = ev_w_out[e].astype(BF16)
            alog_vec = jnp.zeros((1, 128), F32).at[0, 2 * GDN_HEADS:4 * GDN_HEADS].set(gdn_a_log[e].reshape(-1))
            dtb_vec = jnp.zeros((1, 128), F32).at[0, 2 * GDN_HEADS:4 * GDN_HEADS].set(gdn_dt_bias[e].reshape(-1))
            for grp in groups:
                qkv, gz, sc, gates = _inproj_even(grp["x"], mods, norm_mix[layer], w_qkvz, w_sc, w_gate, gdn_conv[e],
                                                  sc_conv[e], alog_vec, dtb_vec, grp["l"], grp["mod"](layer))
                s0 = (state_gdn_fwd[:, e:e + 1], state_gdn_bwd[:, e:e + 1]) if grp["latent"] else (None, None)
                if grp["latent"]:
                    o, _, _ = _gdn_mixer(qkv, gz, gates, grp["n"], grp["l"], gdn_norm[e], *s0)
                else:
                    o, s_f, s_b, w1, w2 = _gdn_mixer(qkv, gz, gates, grp["n"], grp["l"], gdn_norm[e], *s0,
                                                     cast=mlp_cast)
                    outs.setdefault("gdn_f", []).append(s_f)
                    outs.setdefault("gdn_b", []).append(s_b)
                grp["x"] = _outproj_mlp(o, sc, grp["x"], mods, norm_mlp[layer], w_out, w1, w2, norm_final,
                                        grp["mod"](layer), final)
        else:
            o_ = layer // 2
            w = od_w_in[o_]
            w_in = [w.astype(BF16)]
            w_out = od_w_out[o_].astype(BF16)
            rw = (rwkv_mu[o_], rwkv_w0[o_], rwkv_w_up[o_], rwkv_a0[o_], rwkv_a_up[o_], rwkv_g_up[o_],
                  rwkv_k_k[o_], rwkv_k_a[o_], rwkv_r_k[o_].reshape(-1), rwkv_ln_w[o_], rwkv_ln_b[o_])
            for grp in groups:
                p_att, x_rw = _inproj(grp["x"], mods, norm_mix[layer], w_in, (ATT_W, RWKV_IN), grp["mod"](layer))
                if grp["latent"]:
                    att = _attn_latent(p_att, attn_sink[o_], cache_attn_k, cache_attn_v, o_, grp["n"], grp["l"])
                    rwo, _, _ = _rwkv_mixer(x_rw, grp["n"], grp["l"], rw,
                                            state_rwkv_fwd[:, o_:o_ + 1], state_rwkv_bwd[:, o_:o_ + 1])
                else:
                    att, kc, vc, w1, w2 = _attn_context(p_att, attn_sink[o_], grp["n"], grp["l"], cast=mlp_cast)
                    rwo, s_f, s_b = _rwkv_mixer(x_rw, grp["n"], grp["l"], rw, None, None)
                    outs.setdefault("att_k", []).append(kc)
                    outs.setdefault("att_v", []).append(vc)
                    outs.setdefault("rw_f", []).append(s_f)
                    outs.setdefault("rw_b", []).append(s_b)
                grp["x"] = _outproj_mlp(att, rwo, grp["x"], mods, norm_mlp[layer], w_out, w1, w2, norm_final,
                                        grp["mod"](layer), final)
    cat = lambda key: jnp.concatenate(outs[key], axis=1)
    return (groups[0]["x"].reshape(bp, lp, D_MODEL), groups[1]["x"].reshape(bs, ls, D_MODEL),
            cat("gdn_f"), cat("gdn_b"), cat("att_k"), cat("att_v"), cat("rw_f"), cat("rw_b"))
```

```python
import functools

import jax
import jax.numpy as jnp
import numpy as np
from jax import lax
from jax.experimental import pallas as pl
from jax.experimental.pallas import tpu as pltpu

F32 = jnp.float32
BF16 = jnp.bfloat16

D_MODEL = 1024
N_MOD = 6
D_FF = 4 * D_MODEL
NORM_EPS = 1e-6
TOKEN_TILE = 512
HALO = 8
MOD_ROWS = 8

GDN_HEADS = 4
GDN_D = 128
GDN_CHUNK = 128
GDN_GROUP = 8
GDN_HEADS_PER_STEP = 2
GDN_BLOCK_ROWS = 1024
GDN_QKV_W = 3 * GDN_HEADS * GDN_D
SC_WIDTH = 512
MOD_COL_BLOCKS = 4

ATT_HEADS = 8
ATT_KV_HEADS = 2
ATT_GROUP = ATT_HEADS // ATT_KV_HEADS
ATT_HD = 64
ATT_Q_W = ATT_HEADS * ATT_HD
ATT_KV_W = ATT_KV_HEADS * ATT_HD
ATT_W = ATT_Q_W + 2 * ATT_KV_W
WINDOW = 128
ATT_BLOCK = 128
GRID_W = 64
ROPE_BASE = 10000.0
NEG_INF = -1e30

RWKV_HEADS = 8
RWKV_HD = 64
RWKV_W = RWKV_HEADS * RWKV_HD
RWKV_LORA = 64
RWKV_IN = 3 * RWKV_W + 3 * 2 * RWKV_LORA
RWKV_CHUNK = 64
RWKV_GROUP = 4
RWKV_FINISH_ROWS = 256
DECAY_SCALE = float(np.exp(-0.5))
RWKV_BLOCK_ROWS = 1024
PAIR_W = 2 * RWKV_HD
N_PAIRS = RWKV_W // PAIR_W
GN_EPS = 64e-5

VMEM_LIMIT = 56 * 1024 * 1024

RWKV_PREC = dict(lora="bf", gate="bf", cumsum="x2r")


def _split_bf16(a):
    hi = a.astype(BF16)
    return hi, (a - hi.astype(F32)).astype(BF16)


def _dot(a, b, dims, prec):
    dn = (dims, ((), ()))
    one = lambda x, y: lax.dot_general(x, y, dn, preferred_element_type=F32)
    if prec == "x2r":
        ah = a.astype(BF16)
        bh, bl = _split_bf16(b)
        return one(ah, bh) + one(ah, bl)
    assert prec == "bf", prec
    return one(a.astype(BF16), b.astype(BF16))


def _mm(a, b, prec):
    return _dot(a, b, ((1,), (0,)), prec)


def _mm_nt(a, b, prec):
    return _dot(a, b, ((1,), (1,)), prec)


def _mm_tn(a, b, prec):
    return _dot(a, b, ((0,), (0,)), prec)


def _silu(x):
    return x * jax.nn.sigmoid(x)


def _softplus(x):
    return jnp.maximum(x, 0.0) + jnp.log1p(jnp.exp(-jnp.abs(x)))


def _rms(x, w):
    return x * lax.rsqrt(jnp.mean(x * x, axis=-1, keepdims=True) + NORM_EPS) * w


def _tri_masks(n, rev):
    r = lax.broadcasted_iota(jnp.int32, (n, n), 0)
    c = lax.broadcasted_iota(jnp.int32, (n, n), 1)
    if rev:
        return r <= c, r < c
    return r >= c, r > c


def _unit_tri_inv_many(mats, prec):
    assert prec == "bf"
    n = mats[0].shape[0]
    r = lax.broadcasted_iota(jnp.int32, (n, n), 0)
    c = lax.broadcasted_iota(jnp.int32, (n, n), 1)
    eye = jnp.where(r == c, 1.0, 0.0)
    within2 = jnp.where((r >> 1) == (c >> 1), 1.0, 0.0)
    ts = [eye - a * within2 for a in mats]
    abs_ = [a.astype(BF16) for a in mats]
    shift = 1
    while (1 << shift) < n:
        join = jnp.where(((r >> (shift + 1)) == (c >> (shift + 1))) & ((r >> shift) != (c >> shift)),
                         1.0, 0.0).astype(BF16)
        tbs = [t.astype(BF16) for t in ts]
        inner = [_mm(ab * join, tb, prec) for ab, tb in zip(abs_, tbs)]
        ts = [t - _mm(tb, w, prec) for t, tb, w in zip(ts, tbs, inner)]
        shift += 1
    return ts


def _shifted_rows(x, prev_row, next_row):
    n = x.shape[0]
    row = lax.broadcasted_iota(jnp.int32, x.shape, 0)
    xp = jnp.where(row == 0, prev_row, pltpu.roll(x, 1, 0))
    xn = jnp.where(row == n - 1, next_row, pltpu.roll(x, n - 1, 0))
    return xp, xn


def _mod_kernel(c_ref, w_ref, b_ref, ev_ref, o_ref, wqkvz_ref, wsc_ref, wgate_ref):
    s = _silu(c_ref[...])
    o_ref[...] = _mm(s, w_ref[...], prec="bf") + b_ref[...]
    ev = ev_ref[...]
    qkvz = GDN_QKV_W + GDN_HEADS * GDN_D
    n_gate = 4 * GDN_HEADS
    wqkvz_ref[...] = ev[:, :qkvz].astype(BF16)
    wsc_ref[...] = ev[:, qkvz + n_gate:].astype(BF16)
    gate = ev[:, qkvz:qkvz + 128]
    lane = lax.broadcasted_iota(jnp.int32, gate.shape, 1)
    wgate_ref[...] = jnp.where(lane < n_gate, gate, 0.0).astype(BF16)


def _modulation(c_rows, mod_w, mod_b, ev_w_in):
    depth = mod_w.shape[0]
    n_even, _, ev_cols = ev_w_in.shape
    nblk = MOD_COL_BLOCKS
    width = N_MOD * D_MODEL // nblk
    steps = depth * nblk
    ev_rows = n_even * D_MODEL
    blk = ev_rows // steps
    assert ev_rows % steps == 0 and blk % 16 == 0 and width % 128 == 0
    qkvz = GDN_QKV_W + GDN_HEADS * GDN_D
    side = lambda n: pl.BlockSpec((blk, n), lambda l, j: (l * nblk + j, 0))
    out, w_qkvz, w_sc, w_gate = pl.pallas_call(
        _mod_kernel,
        grid=(depth, nblk),
        in_specs=[
            pl.BlockSpec((MOD_ROWS, D_MODEL), lambda l, j: (0, 0)),
            pl.BlockSpec((None, D_MODEL, width), lambda l, j: (l, 0, j)),
            pl.BlockSpec((None, 1, width), lambda l, j: (l, 0, j)),
            side(ev_cols),
        ],
        out_specs=[pl.BlockSpec((None, MOD_ROWS, width), lambda l, j: (l, 0, j)),
                   side(qkvz), side(3 * SC_WIDTH), side(128)],
        out_shape=[jax.ShapeDtypeStruct((depth, MOD_ROWS, N_MOD * D_MODEL), F32),
                   jax.ShapeDtypeStruct((ev_rows, qkvz), BF16),
                   jax.ShapeDtypeStruct((ev_rows, 3 * SC_WIDTH), BF16),
                   jax.ShapeDtypeStruct((ev_rows, 128), BF16)],
        compiler_params=pltpu.CompilerParams(dimension_semantics=("parallel", "parallel"),
                                             vmem_limit_bytes=VMEM_LIMIT),
        name="modulation",
    )(c_rows, mod_w, mod_b.reshape(depth, 1, N_MOD * D_MODEL), ev_w_in.reshape(ev_rows, ev_cols))
    per_layer = lambda a: a.reshape(n_even, D_MODEL, a.shape[1])
    return out.reshape(depth, MOD_ROWS, N_MOD, D_MODEL), per_layer(w_qkvz), per_layer(w_sc), per_layer(w_gate)


def _mod_spec(layer, tiles_per_seq, row_base, row_step):
    return pl.BlockSpec((None, None, N_MOD, D_MODEL),
                        lambda i: (layer, row_base + (i // tiles_per_seq) * row_step, 0, 0))


def _inproj_kernel(*refs, n_w):
    x_ref, mod_ref, nw_ref = refs[:3]
    w_refs = refs[3:3 + n_w]
    o_refs = refs[3 + n_w:]
    h = _rms(x_ref[...], nw_ref[...])
    h = (h * (1.0 + mod_ref[1:2, :]) + mod_ref[0:1, :]).astype(BF16)
    pieces = [_mm(h, w_ref[...], prec="bf") for w_ref in w_refs]
    y = pieces[0] if n_w == 1 else jnp.concatenate(pieces, axis=1)
    off = 0
    for o_ref in o_refs:
        n = o_ref.shape[-1]
        o_ref[...] = y[:, off:off + n]
        off += n


def _inproj(x, mods, norm_w, ws_bf16, splits, mod_spec):
    t = x.shape[0]
    assert sum(w.shape[1] for w in ws_bf16) == sum(splits) and all(w.shape[1] % 128 == 0 for w in ws_bf16)
    return pl.pallas_call(
        functools.partial(_inproj_kernel, n_w=len(ws_bf16)),
        grid=(t // TOKEN_TILE,),
        in_specs=[
            pl.BlockSpec((TOKEN_TILE, D_MODEL), lambda i: (i, 0)),
            mod_spec,
            pl.BlockSpec((1, D_MODEL), lambda i: (0, 0)),
        ] + [pl.BlockSpec(w.shape, lambda i: (0, 0)) for w in ws_bf16],
        out_specs=[pl.BlockSpec((TOKEN_TILE, n), lambda i: (i, 0)) for n in splits],
        out_shape=[jax.ShapeDtypeStruct((t, n), F32) for n in splits],
        compiler_params=pltpu.CompilerParams(dimension_semantics=("parallel",), vmem_limit_bytes=VMEM_LIMIT),
        name="inproj",
    )(x, mods, norm_w.reshape(1, D_MODEL), *ws_bf16)


def _inproj_even_kernel(xp_ref, x_ref, xn_ref, mod_ref, nw_ref, wqkvz_ref, wsc_ref, wgate_ref, cqkv_ref, csc_ref,
                        alog_ref, dtb_ref, qkv_ref, gz_ref, sc_ref, gate_ref, *, seq_len):
    tile = x_ref.shape[0]
    x = jnp.concatenate([xp_ref[...], x_ref[...], xn_ref[...]], axis=0)
    h = _rms(x, nw_ref[...])
    h = (h * (1.0 + mod_ref[1:2, :]) + mod_ref[0:1, :]).astype(BF16)
    n = tile + 2 * HALO
    first = pl.program_id(0) * tile - HALO
    pos = (first + lax.broadcasted_iota(jnp.int32, (n, 1), 0)) & (seq_len - 1)
    at_start = pos == 0
    at_end = pos == seq_len - 1

    def conv3(v, c_ref):
        vp = jnp.where(at_start, 0.0, pltpu.roll(v, 1, 0))
        vn = jnp.where(at_end, 0.0, pltpu.roll(v, n - 1, 0))
        return vp * c_ref[0:1, :] + v * c_ref[1:2, :] + vn * c_ref[2:3, :]

    keep = slice(HALO, HALO + tile)
    wide = 2 * GDN_D
    for j in range(GDN_QKV_W // wide):
        cols = slice(j * wide, (j + 1) * wide)
        act = _silu(conv3(_mm(h, wqkvz_ref[:, cols], "bf"), cqkv_ref.at[:, cols]))[keep]
        for i in range(2):
            part = act[:, i * GDN_D:(i + 1) * GDN_D]
            if j < 2 * GDN_HEADS // 2:
                part = part * lax.rsqrt(jnp.sum(part * part, axis=-1, keepdims=True) + 1e-6)
                if j < GDN_HEADS // 2:
                    part = part * (GDN_D ** -0.5)
            qkv_ref[:, j * wide + i * GDN_D:j * wide + (i + 1) * GDN_D] = part.astype(BF16)
    for j in range(GDN_HEADS * GDN_D // wide):
        cols = slice(j * wide, (j + 1) * wide)
        zcols = slice(GDN_QKV_W + j * wide, GDN_QKV_W + (j + 1) * wide)
        gz_ref[:, cols] = _silu(_mm(h, wqkvz_ref[:, zcols], "bf")[keep]).astype(BF16)
    for j in range(SC_WIDTH // wide):
        sc_b, sc_c, sc_h = [_mm(h, wsc_ref[:, i * SC_WIDTH + j * wide:i * SC_WIDTH + (j + 1) * wide], "bf")
                            for i in range(3)]
        cols = slice(j * wide, (j + 1) * wide)
        sc_ref[:, cols] = (sc_b * conv3(sc_c * sc_h, csc_ref.at[:, cols]))[keep].astype(BF16)
    g = _mm(h, wgate_ref[...], "bf")[keep]
    lane = lax.broadcasted_iota(jnp.int32, g.shape, 1)
    gate_ref[...] = jnp.where(lane < 2 * GDN_HEADS, jax.nn.sigmoid(g),
                              -jnp.exp(alog_ref[...]) * _softplus(g + dtb_ref[...]))


def _inproj_even(x, mods, norm_w, w_qkvz, w_sc, w_gate, conv_w, sc_conv_w, alog_vec, dtb_vec, seq_len, mod_spec):
    t = x.shape[0]
    tile = TOKEN_TILE
    assert seq_len & (seq_len - 1) == 0 and t % tile == 0
    per = tile // HALO
    last = t // HALO - 1
    const = lambda i: (0, 0)
    tok = lambda n: pl.BlockSpec((tile, n), lambda i: (i, 0))
    return pl.pallas_call(
        functools.partial(_inproj_even_kernel, seq_len=seq_len),
        grid=(t // tile,),
        in_specs=[
            pl.BlockSpec((HALO, D_MODEL), lambda i: (jnp.maximum(i * per - 1, 0), 0)),
            pl.BlockSpec((tile, D_MODEL), lambda i: (i, 0)),
            pl.BlockSpec((HALO, D_MODEL), lambda i: (jnp.minimum((i + 1) * per, last), 0)),
            mod_spec,
            pl.BlockSpec((1, D_MODEL), const),
            pl.BlockSpec(w_qkvz.shape, const), pl.BlockSpec(w_sc.shape, const), pl.BlockSpec(w_gate.shape, const),
            pl.BlockSpec(conv_w.shape, const), pl.BlockSpec(sc_conv_w.shape, const),
            pl.BlockSpec((1, 128), const), pl.BlockSpec((1, 128), const),
        ],
        out_specs=[tok(GDN_QKV_W), tok(GDN_HEADS * GDN_D), tok(SC_WIDTH), tok(128)],
        out_shape=[jax.ShapeDtypeStruct((t, GDN_QKV_W), BF16), jax.ShapeDtypeStruct((t, GDN_HEADS * GDN_D), BF16),
                   jax.ShapeDtypeStruct((t, SC_WIDTH), BF16), jax.ShapeDtypeStruct((t, 128), F32)],
        compiler_params=pltpu.CompilerParams(dimension_semantics=("parallel",), vmem_limit_bytes=VMEM_LIMIT),
        name="inproj_even",
    )(x, x, x, mods, norm_w.reshape(1, D_MODEL), w_qkvz, w_sc, w_gate, conv_w, sc_conv_w, alog_vec, dtb_vec)


def _mlp_kernel(a_ref, b_ref, x_ref, mod_ref, nw_ref, woa_ref, wob_ref, w1_ref, w2_ref, nf_ref, o_ref, *, final):
    y = _mm(a_ref[...], woa_ref[...], prec="bf") + _mm(b_ref[...], wob_ref[...], prec="bf")
    x1 = x_ref[...] + mod_ref[2:3, :] * y
    h = _rms(x1, nw_ref[...])
    h = (h * (1.0 + mod_ref[4:5, :]) + mod_ref[3:4, :]).astype(BF16)
    acc = jnp.zeros(x1.shape, F32)
    for j in range(D_FF // D_MODEL):
        cols = slice(j * D_MODEL, (j + 1) * D_MODEL)
        u = jnp.maximum(_mm(h, w1_ref[:, cols], prec="bf"), 0.0)
        acc = acc + _mm(u * u, w2_ref[cols, :], prec="bf")
    x2 = x1 + mod_ref[5:6, :] * acc
    if final:
        x2 = _rms(x2, nf_ref[...])
    o_ref[...] = x2


def _outproj_mlp(a, b, x, mods, norm_w, w_out, w1, w2, norm_final, mod_spec, final):
    t = x.shape[0]
    half = a.shape[1]
    const = lambda i: (0, 0)
    return pl.pallas_call(
        functools.partial(_mlp_kernel, final=final),
        grid=(t // TOKEN_TILE,),
        in_specs=[
            pl.BlockSpec((TOKEN_TILE, half), lambda i: (i, 0)),
            pl.BlockSpec((TOKEN_TILE, half), lambda i: (i, 0)),
            pl.BlockSpec((TOKEN_TILE, D_MODEL), lambda i: (i, 0)),
            mod_spec,
            pl.BlockSpec((1, D_MODEL), const),
            pl.BlockSpec((half, D_MODEL), const),
            pl.BlockSpec((half, D_MODEL), lambda i: (1, 0)),
            pl.BlockSpec((D_MODEL, D_FF), const),
            pl.BlockSpec((D_FF, D_MODEL), const),
            pl.BlockSpec((1, D_MODEL), const),
        ],
        out_specs=pl.BlockSpec((TOKEN_TILE, D_MODEL), lambda i: (i, 0)),
        out_shape=jax.ShapeDtypeStruct((t, D_MODEL), F32),
        compiler_params=pltpu.CompilerParams(dimension_semantics=("parallel",), vmem_limit_bytes=VMEM_LIMIT),
        name="outproj_mlp",
    )(a, b, x, mods, norm_w.reshape(1, D_MODEL), w_out, w_out, w1, w2, norm_final.reshape(1, D_MODEL))


def _gdn_decay_terms(g, rev):
    c = g.shape[0]
    incl = _tri_masks(c, rev)[0]
    before_col = _tri_masks(c, not rev)[0]
    eye = lax.broadcasted_iota(jnp.int32, (c, c), 0) == lax.broadcasted_iota(jnp.int32, (c, c), 1)
    gc_row = jnp.sum(jnp.where(before_col, jnp.broadcast_to(g, (c, c)), 0.0), axis=0, keepdims=True)
    gc_col = jnp.sum(jnp.where(eye, jnp.broadcast_to(gc_row, (c, c)), 0.0), axis=1, keepdims=True)
    decay = jnp.where(incl, jnp.exp(jnp.where(incl, gc_col - gc_row, 0.0)), 0.0)
    g_tot = jnp.sum(g, axis=0, keepdims=True)
    return decay, jnp.exp(gc_col), jnp.exp(g_tot - gc_col), jnp.exp(g_tot)


def _cast_side_job(arrays_layers, n_steps, step_index):
    in_specs, out_specs, out_shapes = [], [], []
    for arr, layer in arrays_layers:
        _, r, c = arr.shape
        blk = r // n_steps
        assert r % n_steps == 0 and blk % 16 == 0
        in_specs.append(pl.BlockSpec((None, blk, c), lambda *g, layer=layer: (layer, step_index(*g), 0)))
        out_specs.append(pl.BlockSpec((blk, c), lambda *g: (step_index(*g), 0)))
        out_shapes.append(jax.ShapeDtypeStruct((r, c), BF16))
    return in_specs, out_specs, out_shapes


def _gdn_kernel(*refs, seq_len, seqs, has_init, n_cast):
    q_ref, k_ref, v_ref, gz_ref, gate_ref, gn_ref = refs[:6]
    pos = 6
    if has_init:
        s0f_ref, s0b_ref = refs[pos:pos + 2]
        pos += 2
    cast_in = refs[pos:pos + n_cast]
    pos += n_cast
    o_ref, sf_ref, sb_ref = refs[pos:pos + 3]
    cast_out = refs[pos + 3:pos + 3 + n_cast]
    pos += n_cast
    for src, dst in zip(cast_in, cast_out):
        dst[...] = src[...].astype(BF16)
    osum, u_s, w_s, qd_s, kd_s, in_s, ge_s, st = refs[pos + 3:]
    hps = GDN_HEADS_PER_STEP
    head0 = pl.program_id(1) * hps
    ch = GDN_CHUNK
    n_chunks = seq_len // ch
    osum[...] = jnp.zeros(osum.shape, F32)
    hcols = [slice(hh * GDN_D, (hh + 1) * GDN_D) for hh in range(hps)]

    lane = lax.broadcasted_iota(jnp.int32, (ch, 128), 1)

    def pick(rows, col):
        return jnp.sum(jnp.where(lane == col, gate_ref[rows, :], 0.0), axis=1, keepdims=True)

    def solve_group(gi, carry):
        items = []
        for j in range(GDN_GROUP):
            c = gi * GDN_GROUP + j
            rows = pl.ds(pl.multiple_of(c * ch, ch), ch)
            for hh in range(hps):
                items.append(dict(c=c, hh=hh, rows=rows, q=q_ref[rows, hcols[hh]].astype(F32),
                                  k=k_ref[rows, hcols[hh]].astype(F32), v=v_ref[rows, hcols[hh]].astype(F32)))
        kks = [_mm_nt(it["k"], it["k"], "bf") for it in items]
        qks = [_mm_nt(it["q"], it["k"], "bf") for it in items]
        subs = []
        for it, kk, qk in zip(items, kks, qks):
            for d in range(2):
                beta = pick(it["rows"], d * GDN_HEADS + head0 + it["hh"])
                g = pick(it["rows"], 2 * GDN_HEADS + d * GDN_HEADS + head0 + it["hh"])
                decay, e_gc, e_rest, e_tot = _gdn_decay_terms(g, rev=(d == 1))
                strict = _tri_masks(ch, d == 1)[1]
                subs.append(dict(
                    d=d, c=it["c"], hh=it["hh"], rows=it["rows"],
                    a=jnp.where(strict, kk * beta * decay, 0.0),
                    rhs=jnp.concatenate([it["v"] * beta, it["k"] * (beta * e_gc)], axis=1),
                    intra=qk * decay, qd=it["q"] * e_gc, kd=it["k"] * e_rest, ge=e_tot))
        ts = _unit_tri_inv_many([s["a"] for s in subs], "bf")
        uws = [_mm(t, s["rhs"], "bf") for t, s in zip(ts, subs)]
        for s, uw in zip(subs, uws):
            d, hh, rows = s["d"], s["hh"], s["rows"]
            u_s[d, hh, rows, :] = uw[:, :GDN_D]
            w_s[d, hh, rows, :] = uw[:, GDN_D:].astype(BF16)
            qd_s[d, hh, rows, :] = s["qd"].astype(BF16)
            kd_s[d, hh, rows, :] = s["kd"].astype(BF16)
            in_s[d, hh, rows, :] = s["intra"].astype(BF16)
            ge_s[d, hh, pl.ds(pl.multiple_of(s["c"] * 8, 8), 8), :] = jnp.broadcast_to(s["ge"], (8, 128))
        return carry

    lax.fori_loop(0, seqs * n_chunks // GDN_GROUP, solve_group, 0)

    for hh in range(hps):
        for j in range(seqs):
            st[0, hh, j] = s0f_ref[j, hh] if has_init else jnp.zeros((GDN_D, GDN_D), F32)
            st[1, hh, j] = s0b_ref[j, hh] if has_init else jnp.zeros((GDN_D, GDN_D), F32)

    def recur(i, carry):
        cs = []
        for hh in range(hps):
            for j in range(seqs):
                for d in range(2):
                    c = j * n_chunks + (i if d == 0 else n_chunks - 1 - i)
                    cs.append(dict(d=d, hh=hh, j=j, rows=pl.ds(pl.multiple_of(c * ch, ch), ch),
                                   ge=ge_s[d, hh, pl.ds(pl.multiple_of(c * 8, 8), 1), :], s=st[d, hh, j]))
        sbs = [c["s"].astype(BF16) for c in cs]
        wss = [_mm(w_s[c["d"], c["hh"], c["rows"], :], sb, "bf") for c, sb in zip(cs, sbs)]
        qss = [_mm(qd_s[c["d"], c["hh"], c["rows"], :], sb, "bf") for c, sb in zip(cs, sbs)]
        ebs = [(u_s[c["d"], c["hh"], c["rows"], :] - ws).astype(BF16) for c, ws in zip(cs, wss)]
        outs = [qs_ + _mm(in_s[c["d"], c["hh"], c["rows"], :], eb, "bf") for c, qs_, eb in zip(cs, qss, ebs)]
        s_news = [c["s"] * c["ge"] + _mm_tn(kd_s[c["d"], c["hh"], c["rows"], :], eb, "bf")
                  for c, eb in zip(cs, ebs)]
        for c, o, s_new in zip(cs, outs, s_news):
            st[c["d"], c["hh"], c["j"]] = s_new
            osum[c["rows"], hcols[c["hh"]]] += o
        return carry

    lax.fori_loop(0, n_chunks, recur, 0)
    for hh in range(hps):
        for j in range(seqs):
            sf_ref[j, hh] = st[0, hh, j]
            sb_ref[j, hh] = st[1, hh, j]
        o_ref[:, hcols[hh]] = (_rms(osum[:, hcols[hh]], gn_ref[...]) * gz_ref[:, hcols[hh]].astype(F32))


def _gdn_mixer(qkv, gz, gates, n_seq, seq_len, gdn_norm, s0_f, s0_b, cast=()):
    has_init = s0_f is not None
    hd = GDN_HEADS
    hps = GDN_HEADS_PER_STEP
    hgroups = hd // hps
    seqs = max(1, GDN_BLOCK_ROWS // seq_len)
    rows = seqs * seq_len
    assert n_seq % seqs == 0 and (rows // GDN_CHUNK) % GDN_GROUP == 0 and hd % hps == 0

    def col(block):
        return pl.BlockSpec((rows, hps * GDN_D), lambda s, h, b=block: (s, b * hgroups + h))

    state = pl.BlockSpec((seqs, None, hps, GDN_D, GDN_D), lambda s, h: (s, 0, h, 0, 0))
    in_specs = [col(0), col(1), col(2), col(0), pl.BlockSpec((rows, 128), lambda s, h: (s, 0)),
                pl.BlockSpec((1, 128), lambda s, h: (0, 0))]
    args = [qkv, qkv, qkv, gz, gates, gdn_norm.reshape(1, 128)]
    if has_init:
        in_specs += [state, state]
        args += [s0_f, s0_b]
    n_steps = (n_seq // seqs) * hgroups
    cast_in, cast_out, cast_shapes = _cast_side_job(cast, n_steps, lambda s, h: s * hgroups + h)
    in_specs += cast_in
    args += [arr for arr, _ in cast]
    t = n_seq * seq_len
    scratch = ([pltpu.VMEM((rows, hps * GDN_D), F32)]
               + [pltpu.VMEM((2, hps, rows, GDN_D), F32)]
               + [pltpu.VMEM((2, hps, rows, GDN_D), BF16) for _ in range(3)]
               + [pltpu.VMEM((2, hps, rows, GDN_CHUNK), BF16),
                  pltpu.VMEM((2, hps, rows // GDN_CHUNK * 8, 128), F32),
                  pltpu.VMEM((2, hps, seqs, GDN_D, GDN_D), F32)])
    return pl.pallas_call(
        functools.partial(_gdn_kernel, seq_len=seq_len, seqs=seqs, has_init=has_init, n_cast=len(cast)),
        grid=(n_seq // seqs, hgroups),
        in_specs=in_specs,
        out_specs=[pl.BlockSpec((rows, hps * GDN_D), lambda s, h: (s, h)), state, state] + cast_out,
        out_shape=[jax.ShapeDtypeStruct((t, hd * GDN_D), F32),
                   jax.ShapeDtypeStruct((n_seq, 1, hd, GDN_D, GDN_D), F32),
                   jax.ShapeDtypeStruct((n_seq, 1, hd, GDN_D, GDN_D), F32)] + cast_shapes,
        scratch_shapes=scratch,
        compiler_params=pltpu.CompilerParams(dimension_semantics=("parallel", "parallel"),
                                             vmem_limit_bytes=VMEM_LIMIT),
        name="gdn_mixer",
    )(*args)


def _softmax_pv(scores, values, sink):
    m = sink
    for s in scores:
        m = jnp.maximum(m, jnp.max(s, axis=-1, keepdims=True))
    den = jnp.exp(sink - m)
    acc = None
    for s, v in zip(scores, values):
        e = jnp.exp(s - m)
        den = den + jnp.sum(e, axis=-1, keepdims=True)
        pv = _mm(e, v, prec="bf")
        acc = pv if acc is None else acc + pv
    return acc / den


def _group_sinks(sink_ref, j, rows):
    assert rows & (rows - 1) == 0
    grp = lax.broadcasted_iota(jnp.int32, (ATT_GROUP * rows, 1), 0) >> (rows.bit_length() - 1)
    col = jnp.full(grp.shape, sink_ref[j * ATT_GROUP], F32)
    for gi in range(1, ATT_GROUP):
        col = jnp.where(grp == gi, sink_ref[j * ATT_GROUP + gi], col)
    return col


def _store_group(o_ref, j, o, rows):
    for gi in range(ATT_GROUP):
        hh = j * ATT_GROUP + gi
        o_ref[:, hh * ATT_HD:(hh + 1) * ATT_HD] = o[gi * rows:(gi + 1) * rows]


def _attn_ctx_kernel(*refs, n_cast):
    sink_ref, p_ref = refs[:2]
    cast_in = refs[2:2 + n_cast]
    o_ref, kc_ref, vc_ref = refs[2 + n_cast:5 + n_cast]
    for src, dst in zip(cast_in, refs[5 + n_cast:]):
        dst[...] = src[...].astype(BF16)
    scale = ATT_HD ** -0.5
    rows = p_ref.shape[0]
    for j in range(ATT_KV_HEADS):
        k = p_ref[:, ATT_Q_W + j * ATT_HD:ATT_Q_W + (j + 1) * ATT_HD]
        v = p_ref[:, ATT_Q_W + ATT_KV_W + j * ATT_HD:ATT_Q_W + ATT_KV_W + (j + 1) * ATT_HD]
        kc_ref[j] = k
        vc_ref[j] = v
        q = jnp.concatenate([p_ref[:, hh * ATT_HD:(hh + 1) * ATT_HD]
                             for hh in range(j * ATT_GROUP, (j + 1) * ATT_GROUP)], axis=0)
        s = _mm_nt(q, k, prec="bf") * scale
        _store_group(o_ref, j, _softmax_pv([s], [v], _group_sinks(sink_ref, j, rows)), rows)


def _attn_context(proj_att, sink, n_seq, seq_len, cast=()):
    kv = pl.BlockSpec((None, None, ATT_KV_HEADS, seq_len, ATT_HD), lambda b: (b, 0, 0, 0, 0))
    cast_in, cast_out, cast_shapes = _cast_side_job(cast, n_seq, lambda b: b)
    return pl.pallas_call(
        functools.partial(_attn_ctx_kernel, n_cast=len(cast)),
        grid=(n_seq,),
        in_specs=[pl.BlockSpec(memory_space=pltpu.SMEM),
                  pl.BlockSpec((seq_len, ATT_W), lambda b: (b, 0))] + cast_in,
        out_specs=[pl.BlockSpec((seq_len, ATT_Q_W), lambda b: (b, 0)), kv, kv] + cast_out,
        out_shape=[jax.ShapeDtypeStruct((n_seq * seq_len, ATT_Q_W), F32),
                   jax.ShapeDtypeStruct((n_seq, 1, ATT_KV_HEADS, seq_len, ATT_HD), F32),
                   jax.ShapeDtypeStruct((n_seq, 1, ATT_KV_HEADS, seq_len, ATT_HD), F32)] + cast_shapes,
        compiler_params=pltpu.CompilerParams(dimension_semantics=("parallel",), vmem_limit_bytes=VMEM_LIMIT),
        name="attn_context",
    )(sink, proj_att, *[arr for arr, _ in cast])


def _rope_tables(seq_len):
    pos = np.arange(seq_len)
    half = ATT_HD // 2
    inv = ROPE_BASE ** (-np.arange(0, half, 2, dtype=np.float32) / half)
    ang_r = (pos // GRID_W).astype(np.float32)[:, None] * inv
    ang_c = (pos % GRID_W).astype(np.float32)[:, None] * inv
    cos = np.concatenate([np.cos(ang_r), np.cos(ang_r), np.cos(ang_c), np.cos(ang_c)], axis=1)
    sin = np.concatenate([-np.sin(ang_r), np.sin(ang_r), -np.sin(ang_c), np.sin(ang_c)], axis=1)
    return (jnp.asarray(np.tile(cos, (1, 2)), F32), jnp.asarray(np.tile(sin, (1, 2)), F32))


def _rope(x, cos, sin):
    lane = lax.broadcasted_iota(jnp.int32, x.shape, 1)
    partner = jnp.where((lane & 31) < 16, pltpu.roll(x, 128 - 16, 1), pltpu.roll(x, 16, 1))
    return x * cos + partner * sin


def _attn_lat_kernel(sink_ref, p_ref, ck_ref, cv_ref, cos_ref, sin_ref, o_ref, *, seq_len):
    scale = ATT_HD ** -0.5
    qb = pl.program_id(1)
    span = 3 * ATT_BLOCK
    q0 = pl.multiple_of(qb * ATT_BLOCK, ATT_BLOCK)
    k0 = pl.multiple_of(jnp.clip((qb - 1) * ATT_BLOCK, 0, seq_len - span), ATT_BLOCK)
    qrows = pl.ds(q0, ATT_BLOCK)
    krows = pl.ds(k0, span)
    kwin = _rope(p_ref[krows, ATT_Q_W:ATT_Q_W + ATT_KV_W], cos_ref[krows, :], sin_ref[krows, :])
    vwin = p_ref[krows, ATT_Q_W + ATT_KV_W:ATT_W]
    stacked = ATT_GROUP * ATT_BLOCK
    qpos = q0 + (lax.broadcasted_iota(jnp.int32, (stacked, span), 0) & (ATT_BLOCK - 1))
    kpos = k0 + lax.broadcasted_iota(jnp.int32, (stacked, span), 1)
    valid = jnp.abs(qpos - kpos) <= WINDOW
    cos_q = cos_ref[qrows, :]
    sin_q = sin_ref[qrows, :]
    heads = []
    for pair in range(ATT_HEADS // 2):
        qpair = _rope(p_ref[qrows, pair * 128:(pair + 1) * 128], cos_q, sin_q)
        heads += [qpair[:, :ATT_HD], qpair[:, ATT_HD:]]
    kv = range(ATT_KV_HEADS)
    qs = [jnp.concatenate(heads[j * ATT_GROUP:(j + 1) * ATT_GROUP], axis=0) for j in kv]
    s_locs = [jnp.where(valid, _mm_nt(qs[j], kwin[:, j * ATT_HD:(j + 1) * ATT_HD], prec="bf") * scale, NEG_INF)
              for j in kv]
    s_ctxs = [_mm_nt(qs[j], ck_ref[j], prec="bf") * scale for j in kv]
    outs = [_softmax_pv([s_locs[j], s_ctxs[j]], [vwin[:, j * ATT_HD:(j + 1) * ATT_HD], cv_ref[j]],
                        _group_sinks(sink_ref, j, ATT_BLOCK)) for j in kv]
    for j in kv:
        _store_group(o_ref, j, outs[j], ATT_BLOCK)


def _attn_latent(proj_att, sink, cache_k, cache_v, layer, n_seq, seq_len):
    cos, sin = _rope_tables(seq_len)
    past = cache_k.shape[3]
    nqb = seq_len // ATT_BLOCK
    cache = pl.BlockSpec((None, None, ATT_KV_HEADS, past, ATT_HD), lambda b, q: (b, layer, 0, 0, 0))
    table = pl.BlockSpec((seq_len, 128), lambda b, q: (0, 0))
    return pl.pallas_call(
        functools.partial(_attn_lat_kernel, seq_len=seq_len),
        grid=(n_seq, nqb),
        in_specs=[pl.BlockSpec(memory_space=pltpu.SMEM),
                  pl.BlockSpec((seq_len, ATT_W), lambda b, q: (b, 0)),
                  cache, cache, table, table],
        out_specs=pl.BlockSpec((ATT_BLOCK, ATT_Q_W), lambda b, q: (b * nqb + q, 0)),
        out_shape=jax.ShapeDtypeStruct((n_seq * seq_len, ATT_Q_W), F32),
        compiler_params=pltpu.CompilerParams(dimension_semantics=("parallel", "parallel")),
        name="attn_latent",
    )(sink, proj_att, cache_k, cache_v, cos, sin)


def _pair_masks(n, rev):
    r = lax.broadcasted_iota(jnp.int32, (n, 2 * n), 0)
    c = lax.broadcasted_iota(jnp.int32, (n, 2 * n), 1) & (n - 1)
    return (r <= c, r < c) if rev else (r >= c, r > c)


def _bd(x):
    half = x.shape[1] // 2
    lane = lax.broadcasted_iota(jnp.int32, x.shape, 1)
    zero = jnp.zeros_like(x)
    return jnp.concatenate([jnp.where(lane < half, x, zero), jnp.where(lane >= half, x, zero)], axis=0)


def _bd_mask(n):
    r = lax.broadcasted_iota(jnp.int32, (2 * n, 2 * n), 0)
    c = lax.broadcasted_iota(jnp.int32, (2 * n, 2 * n), 1)
    return (r < n) == (c < n)


def _head_sums(x):
    ones = jnp.where(_bd_mask(RWKV_HD), 1.0, 0.0).astype(BF16)
    return _mm(x, ones, "bf")


def _apply_pairs(tb, x):
    hi, lo = _split_bf16(x)
    return _mm(tb, _bd(hi), "bf") + _mm(tb, _bd(lo), "bf")


def _unit_tri_inv_pairs(mats):
    n = mats[0].shape[0]
    r = lax.broadcasted_iota(jnp.int32, (n, 2 * n), 0)
    c = lax.broadcasted_iota(jnp.int32, (n, 2 * n), 1) & (n - 1)
    eye = jnp.where(r == c, 1.0, 0.0)
    within2 = jnp.where((r >> 1) == (c >> 1), 1.0, 0.0)
    ts = [eye - a * within2 for a in mats]
    abs_ = [a.astype(BF16) for a in mats]
    shift = 1
    while (1 << shift) < n:
        join = jnp.where(((r >> (shift + 1)) == (c >> (shift + 1))) & ((r >> shift) != (c >> shift)),
                         1.0, 0.0).astype(BF16)
        tbs = [t.astype(BF16) for t in ts]
        inner = [_mm(ab * join, _bd(tb), "bf") for ab, tb in zip(abs_, tbs)]
        ts = [t - _mm(tb, _bd(w.astype(BF16)), "bf") for t, tb, w in zip(ts, tbs, inner)]
        shift += 1
    return ts


def _rwkv_chunk_operands(x_ref, prm, sc, c, *, seq_len):
    (mu_ref, w0_ref, wup_ref, a0_ref, aup_ref, gup_ref, kk_ref, ka_ref, rk_ref) = prm
    ch = RWKV_CHUNK
    total = x_ref.shape[0]
    r0 = pl.multiple_of(c * ch, ch)
    rows = pl.ds(r0, ch)
    pos0 = r0 & (seq_len - 1)
    x = x_ref[rows, :]
    prev_row = x_ref[pl.ds(jnp.maximum(r0 - 1, 0), 1), :] * jnp.where(pos0 > 0, 1.0, 0.0)
    next_row = x_ref[pl.ds(jnp.minimum(r0 + ch, total - 1), 1), :] * jnp.where(pos0 + ch < seq_len, 1.0, 0.0)
    xp, xn = _shifted_rows(x, prev_row, next_row)
    mu0, mu1 = mu_ref[0:1, :], mu_ref[1:2, :]
    xs = x * (1.0 - mu0 - mu1) + xp * mu0 + xn * mu1
    r = xs[:, 0:RWKV_W]
    k = xs[:, RWKV_W:2 * RWKV_W]
    v = xs[:, 2 * RWKV_W:3 * RWKV_W]
    lo = 3 * RWKV_W
    p = RWKV_PREC
    gl = xs[:, lo + 4 * RWKV_LORA:lo + 6 * RWKV_LORA]
    sc["gate"][rows, :] = _mm(jax.nn.sigmoid(gl), gup_ref[...], p["gate"])
    sc["v"][rows, :] = v.astype(BF16)
    pairs = [slice(i * PAIR_W, (i + 1) * PAIR_W) for i in range(N_PAIRS)]
    kkv = k * kk_ref[...]
    kaps = []
    for cols in pairs:
        kk_p = kkv[:, cols]
        kaps.append(kk_p * lax.rsqrt(_head_sums(kk_p * kk_p) + 1e-6))
    items = []
    bonus = None
    for d in range(2):
        rev = d == 1
        wl = xs[:, lo + d * RWKV_LORA:lo + (d + 1) * RWKV_LORA]
        al = xs[:, lo + 2 * RWKV_LORA + d * RWKV_LORA:lo + 2 * RWKV_LORA + (d + 1) * RWKV_LORA]
        lw = -DECAY_SCALE * jax.nn.sigmoid(w0_ref[d:d + 1, :] + _mm(jnp.tanh(wl), wup_ref[d], p["lora"]))
        a = jax.nn.sigmoid(a0_ref[d:d + 1, :] + _mm(al, aup_ref[d], p["lora"]))
        k2 = k * (1.0 + (a - 1.0) * ka_ref[...])
        g_inc = _mm(jnp.where(_tri_masks(ch, rev)[0], 1.0, 0.0), lw, p["cumsum"])
        g_tot = jnp.sum(lw, axis=0, keepdims=True)
        e_neg = jnp.exp(-g_inc)
        e_end = jnp.exp(g_tot - g_inc)
        e_exc = jnp.exp(g_inc - lw)
        r_dec = r * jnp.exp(g_inc)
        k_neg = k2 * e_neg
        k_end = k2 * e_end
        sc["dec"][d, pl.ds(pl.multiple_of(c * 8, 8), 8), :] = jnp.broadcast_to(jnp.exp(g_tot), (8, RWKV_W))
        rkr = r * k2 * rk_ref[...]
        bon_d = jnp.concatenate([_head_sums(rkr[:, cols]) for cols in pairs], axis=1) * v
        bonus = bon_d if bonus is None else bonus + bon_d
        for cols, kap in zip(pairs, kaps):
            b_p = kap * a[:, cols]
            items.append(dict(d=d, rows=rows, cols=cols, kap_dec=kap * e_exc[:, cols], r_dec=r_dec[:, cols],
                              b_neg=b_p * e_neg[:, cols], k_neg=k_neg[:, cols],
                              b_end=b_p * e_end[:, cols], k_end=k_end[:, cols], vb=v[:, cols].astype(BF16)))
    sc["bon"][rows, :] = bonus
    return items


def _rwkv_solve(items, sc):
    ch = RWKV_CHUNK
    masks = [_pair_masks(ch, False), _pair_masks(ch, True)]
    ms = [_mm_nt(jnp.concatenate([it["kap_dec"], it["r_dec"]], axis=0),
                 jnp.concatenate([_bd(it["b_neg"].astype(BF16)), _bd(it["k_neg"].astype(BF16))], axis=0), "bf")
          for it in items]
    a_abs = [jnp.where(masks[it["d"]][1], m[:ch, :2 * ch], 0.0) for it, m in zip(items, ms)]
    a_aks = [jnp.where(masks[it["d"]][1], m[:ch, 2 * ch:], 0.0) for it, m in zip(items, ms)]
    a_rbs = [jnp.where(masks[it["d"]][0], m[ch:, :2 * ch], 0.0) for it, m in zip(items, ms)]
    a_rks = [jnp.where(masks[it["d"]][0], m[ch:, 2 * ch:], 0.0) for it, m in zip(items, ms)]
    tbs = [t.astype(BF16) for t in _unit_tri_inv_pairs(a_abs)]
    akvs = [_mm(a_ak, _bd(it["vb"]), "bf") for it, a_ak in zip(items, a_aks)]
    w2s = [_apply_pairs(tb, it["kap_dec"]) for it, tb in zip(items, tbs)]
    u0s = [_apply_pairs(tb, akv) for tb, akv in zip(tbs, akvs)]
    for it, w2, u0, a_rb, a_rk in zip(items, w2s, u0s, a_rbs, a_rks):
        d, rows, cols = it["d"], it["rows"], it["cols"]
        sc["w2"][d, rows, cols] = w2.astype(BF16)
        sc["rd"][d, rows, cols] = it["r_dec"].astype(BF16)
        sc["u0"][d, rows, cols] = u0
        sc["arb"][d, rows, cols] = a_rb.astype(BF16)
        sc["ark"][d, rows, cols] = a_rk.astype(BF16)
        sc["bh"][d, rows, cols] = it["b_end"].astype(BF16)
        sc["kh"][d, rows, cols] = it["k_end"].astype(BF16)


def _rwkv_recur(sc, st, i, *, seq_len, seqs):
    ch = RWKV_CHUNK
    n_chunks = seq_len // ch
    keep = _bd_mask(RWKV_HD)
    cs = []
    for j in range(seqs):
        for d in range(2):
            c = j * n_chunks + (i if d == 0 else n_chunks - 1 - i)
            rows = pl.ds(pl.multiple_of(c * ch, ch), ch)
            dec = sc["dec"][d, pl.ds(pl.multiple_of(c * 8, 8), 1), :]
            for pi in range(N_PAIRS):
                cols = slice(pi * PAIR_W, (pi + 1) * PAIR_W)
                cs.append(dict(d=d, j=j, p=pi, rows=rows, cols=cols, dec=dec[:, cols], s=st[d, j, pi]))
    sbs = [c["s"].astype(BF16) for c in cs]
    lss = [_mm_nt(jnp.concatenate([sc["w2"][c["d"], c["rows"], c["cols"]],
                                   sc["rd"][c["d"], c["rows"], c["cols"]]], axis=0), sb, "bf")
           for c, sb in zip(cs, sbs)]
    ubs = [(-(ls[:ch] + sc["u0"][c["d"], c["rows"], c["cols"]])).astype(BF16) for c, ls in zip(cs, lss)]
    vbs = [sc["v"][c["rows"], c["cols"]] for c in cs]
    ys = [ls[ch:] + _mm(jnp.concatenate([sc["arb"][c["d"], c["rows"], c["cols"]],
                                         sc["ark"][c["d"], c["rows"], c["cols"]]], axis=1),
                        jnp.concatenate([_bd(ub), _bd(vb)], axis=0), "bf")
          for c, ls, ub, vb in zip(cs, lss, ubs, vbs)]
    ups = [_mm_tn(jnp.concatenate([ub, vb], axis=0),
                  jnp.concatenate([sc["bh"][c["d"], c["rows"], c["cols"]],
                                   sc["kh"][c["d"], c["rows"], c["cols"]]], axis=0), "bf")
           for c, ub, vb in zip(cs, ubs, vbs)]
    for c, y, up in zip(cs, ys, ups):
        st[c["d"], c["j"], c["p"]] = c["s"] * c["dec"] + jnp.where(keep, up, 0.0)
        sc["ysum"][c["rows"], c["cols"]] += y


RWKV_SCRATCH = ("ysum", "bon", "gate", "v", "w2", "rd", "u0", "arb", "ark", "bh", "kh", "dec")


def _rwkv_kernel(*refs, seq_len, seqs, has_init):
    x_ref = refs[0]
    prm = refs[1:10]
    lnw_ref, lnb_ref = refs[10:12]
    pos = 12
    if has_init:
        s0f_ref, s0b_ref = refs[pos:pos + 2]
        pos += 2
    o_ref, sf_ref, sb_ref = refs[pos:pos + 3]
    sc = dict(zip(RWKV_SCRATCH, refs[pos + 3:]))
    st = refs[pos + 3 + len(RWKV_SCRATCH)]
    ch = RWKV_CHUNK
    hd = RWKV_HD
    n_chunks = seq_len // ch
    sc["ysum"][...] = jnp.zeros(sc["ysum"].shape, F32)

    def prepare(gi, carry):
        items = []
        for j in range(RWKV_GROUP):
            items += _rwkv_chunk_operands(x_ref, prm, sc, gi * RWKV_GROUP + j, seq_len=seq_len)
        _rwkv_solve(items, sc)
        return carry

    lax.fori_loop(0, seqs * n_chunks // RWKV_GROUP, prepare, 0)

    zero = jnp.zeros((hd, hd), F32)
    for d, s0_ref in enumerate((s0f_ref, s0b_ref) if has_init else (None, None)):
        for j in range(seqs):
            for pi in range(N_PAIRS):
                s_a = s0_ref[j, 2 * pi] if has_init else zero
                s_b = s0_ref[j, 2 * pi + 1] if has_init else zero
                st[d, j, pi] = jnp.concatenate([jnp.concatenate([s_a, zero], axis=1),
                                                jnp.concatenate([zero, s_b], axis=1)], axis=0)

    def recur(i, carry):
        _rwkv_recur(sc, st, i, seq_len=seq_len, seqs=seqs)
        return carry

    lax.fori_loop(0, n_chunks, recur, 0)
    for d, out_ref in enumerate((sf_ref, sb_ref)):
        for j in range(seqs):
            for pi in range(N_PAIRS):
                s = st[d, j, pi]
                out_ref[j, 2 * pi] = s[:hd, :hd]
                out_ref[j, 2 * pi + 1] = s[hd:, hd:]

    tile = RWKV_FINISH_ROWS

    def finish(i, carry):
        rows = pl.ds(pl.multiple_of(i * tile, tile), tile)
        pairs = [slice(pi * PAIR_W, (pi + 1) * PAIR_W) for pi in range(N_PAIRS)]
        ys = [sc["ysum"][rows, cols] for cols in pairs]
        cens = [y - _head_sums(y) * (1.0 / hd) for y in ys]
        vars_ = [_head_sums(cen * cen) * (1.0 / hd) for cen in cens]
        for cols, cen, var in zip(pairs, cens, vars_):
            yn = cen * lax.rsqrt(var + GN_EPS) * lnw_ref[:, cols] + lnb_ref[:, cols]
            o_ref[rows, cols] = (yn + sc["bon"][rows, cols]) * sc["gate"][rows, cols]
        return carry

    lax.fori_loop(0, seqs * seq_len // tile, finish, 0)


def _rwkv_mixer(x_rw, n_seq, seq_len, params, s0_f, s0_b):
    has_init = s0_f is not None
    (mu, w0, w_up, a0, a_up, g_up, k_k, k_a, r_k, ln_w, ln_b) = params
    row = lambda a: a.reshape(1, RWKV_W)
    args = [x_rw, mu, w0, w_up, a0, a_up, g_up, row(k_k), row(k_a), row(r_k), row(ln_w), row(ln_b)]

    def whole(a):
        nd = a.ndim
        return pl.BlockSpec(a.shape, lambda s, nd=nd: (0,) * nd)

    seqs = max(1, RWKV_BLOCK_ROWS // seq_len)
    rows = seqs * seq_len
    assert n_seq % seqs == 0 and (rows // RWKV_CHUNK) % RWKV_GROUP == 0 and seq_len & (seq_len - 1) == 0
    in_specs = [pl.BlockSpec((rows, RWKV_IN), lambda s: (s, 0))] + [whole(a) for a in args[1:]]
    state = pl.BlockSpec((seqs, None, RWKV_HEADS, RWKV_HD, RWKV_HD), lambda s: (s, 0, 0, 0, 0))
    if has_init:
        in_specs += [state, state]
        args += [s0_f, s0_b]
    st_shape = jax.ShapeDtypeStruct((n_seq, 1, RWKV_HEADS, RWKV_HD, RWKV_HD), F32)
    tok = lambda dt: pltpu.VMEM((rows, RWKV_W), dt)
    per_dir = lambda dt: pltpu.VMEM((2, rows, RWKV_W), dt)
    scratch = dict(ysum=tok(F32), bon=tok(F32), gate=tok(F32), v=tok(BF16), w2=per_dir(BF16), rd=per_dir(BF16),
                   u0=per_dir(F32), arb=per_dir(BF16), ark=per_dir(BF16), bh=per_dir(BF16), kh=per_dir(BF16),
                   dec=pltpu.VMEM((2, rows // RWKV_CHUNK * 8, RWKV_W), F32))
    return pl.pallas_call(
        functools.partial(_rwkv_kernel, seq_len=seq_len, seqs=seqs, has_init=has_init),
        grid=(n_seq // seqs,),
        in_specs=in_specs,
        out_specs=[pl.BlockSpec((rows, RWKV_W), lambda s: (s, 0)), state, state],
        out_shape=[jax.ShapeDtypeStruct((n_seq * seq_len, RWKV_W), F32), st_shape, st_shape],
        scratch_shapes=[scratch[name] for name in RWKV_SCRATCH]
        + [pltpu.VMEM((2, seqs, N_PAIRS, PAIR_W, PAIR_W), F32)],
        compiler_params=pltpu.CompilerParams(dimension_semantics=("parallel",), vmem_limit_bytes=VMEM_LIMIT),
        name="rwkv_mixer",
    )(*args)


def kernel(x_prompt, x_sample, state_gdn_fwd, state_gdn_bwd, cache_attn_k, cache_attn_v, state_rwkv_fwd, state_rwkv_bwd, c, c_ctx, mod_w, mod_b, norm_mix, norm_mlp, mlp_w1, mlp_w2, norm_final, ev_w_in, ev_w_out, gdn_conv, gdn_a_log, gdn_dt_bias, gdn_norm, sc_conv, od_w_in, od_w_out, attn_sink, rwkv_mu, rwkv_w0, rwkv_w_up, rwkv_a0, rwkv_a_up, rwkv_g_up, rwkv_k_k, rwkv_k_a, rwkv_r_k, rwkv_ln_w, rwkv_ln_b):
    bp, lp, _ = x_prompt.shape
    bs, ls, _ = x_sample.shape
    depth = mod_w.shape[0]
    c_rows = jnp.concatenate([c_ctx[None, :], c, jnp.zeros((MOD_ROWS - 1 - bs, D_MODEL), F32)], axis=0)
    mods, ev_qkvz, ev_sc, ev_gate = _modulation(c_rows, mod_w, mod_b, ev_w_in)

    assert ls % TOKEN_TILE == 0 and (bp * lp) % TOKEN_TILE == 0
    groups = [
        dict(x=x_prompt.reshape(bp * lp, D_MODEL), n=bp, l=lp, latent=False,
             mod=lambda layer: _mod_spec(layer, 1, 0, 0)),
        dict(x=x_sample.reshape(bs * ls, D_MODEL), n=bs, l=ls, latent=True,
             mod=lambda layer: _mod_spec(layer, ls // TOKEN_TILE, 1, 1)),
    ]
    outs = {}
    for layer in range(depth):
        final = layer == depth - 1
        mlp_cast = ((mlp_w1, layer), (mlp_w2, layer))
        if layer % 2 == 0:
            e = layer // 2
            w_qkvz, w_sc, w_gate = ev_qkvz[e], ev_sc[e], ev_gate[e]
            alog_vec = jnp.zeros((1, 128), F32).at[0, 2 * GDN_HEADS:4 * GDN_HEADS].set(gdn_a_log[e].reshape(-1))
            dtb_vec = jnp.zeros((1, 128), F32).at[0, 2 * GDN_HEADS:4 * GDN_HEADS].set(gdn_dt_bias[e].reshape(-1))
            for grp in groups:
                qkv, gz, sc, gates = _inproj_even(grp["x"], mods, norm_mix[layer], w_qkvz, w_sc, w_gate, gdn_conv[e],
                                                  sc_conv[e], alog_vec, dtb_vec, grp["l"], grp["mod"](layer))
                s0 = (state_gdn_fwd[:, e:e + 1], state_gdn_bwd[:, e:e + 1]) if grp["latent"] else (None, None)
                if grp["latent"]:
                    cast = ((od_w_in, layer // 2), (od_w_out, layer // 2)) if layer + 1 < depth else ()
                    o, _, _, *next_w = _gdn_mixer(qkv, gz, gates, grp["n"], grp["l"], gdn_norm[e], *s0, cast=cast)
                else:
                    o, s_f, s_b, w_out, w1, w2 = _gdn_mixer(qkv, gz, gates, grp["n"], grp["l"], gdn_norm[e], *s0,
                                                            cast=((ev_w_out, e),) + mlp_cast)
                    outs.setdefault("gdn_f", []).append(s_f)
                    outs.setdefault("gdn_b", []).append(s_b)
                grp["x"] = _outproj_mlp(o, sc, grp["x"], mods, norm_mlp[layer], w_out, w1, w2, norm_final,
                                        grp["mod"](layer), final)
        else:
            o_ = layer // 2
            w_in, w_out = [next_w[0]], next_w[1]
            rw = (rwkv_mu[o_], rwkv_w0[o_], rwkv_w_up[o_], rwkv_a0[o_], rwkv_a_up[o_], rwkv_g_up[o_],
                  rwkv_k_k[o_], rwkv_k_a[o_], rwkv_r_k[o_].reshape(-1), rwkv_ln_w[o_], rwkv_ln_b[o_])
            for grp in groups:
                p_att, x_rw = _inproj(grp["x"], mods, norm_mix[layer], w_in, (ATT_W, RWKV_IN), grp["mod"](layer))
                if grp["latent"]:
                    att = _attn_latent(p_att, attn_sink[o_], cache_attn_k, cache_attn_v, o_, grp["n"], grp["l"])
                    rwo, _, _ = _rwkv_mixer(x_rw, grp["n"], grp["l"], rw,
                                            state_rwkv_fwd[:, o_:o_ + 1], state_rwkv_bwd[:, o_:o_ + 1])
                else:
                    att, kc, vc, w1, w2 = _attn_context(p_att, attn_sink[o_], grp["n"], grp["l"], cast=mlp_cast)
                    rwo, s_f, s_b = _rwkv_mixer(x_rw, grp["n"], grp["l"], rw, None, None)
                    outs.setdefault("att_k", []).append(kc)
                    outs.setdefault("att_v", []).append(vc)
                    outs.setdefault("rw_f", []).append(s_f)
                    outs.setdefault("rw_b", []).append(s_b)
                grp["x"] = _outproj_mlp(att, rwo, grp["x"], mods, norm_mlp[layer], w_out, w1, w2, norm_final,
                                        grp["mod"](layer), final)
    cat = lambda key: jnp.concatenate(outs[key], axis=1)
    return (groups[0]["x"].reshape(bp, lp, D_MODEL), groups[1]["x"].reshape(bs, ls, D_MODEL),
            cat("gdn_f"), cat("gdn_b"), cat("att_k"), cat("att_v"), cat("rw_f"), cat("rw_b"))
```

```python
import functools

import jax
import jax.numpy as jnp
import numpy as np
from jax import lax
from jax.experimental import pallas as pl
from jax.experimental.pallas import tpu as pltpu

F32 = jnp.float32
BF16 = jnp.bfloat16

D_MODEL = 1024
N_MOD = 6
D_FF = 4 * D_MODEL
NORM_EPS = 1e-6
TOKEN_TILE = 512
HALO = 8
MOD_ROWS = 8

GDN_HEADS = 4
GDN_D = 128
GDN_CHUNK = 128
GDN_GROUP = 8
GDN_HEADS_PER_STEP = 2
GDN_BLOCK_ROWS = 1024
GDN_QKV_W = 3 * GDN_HEADS * GDN_D
SC_WIDTH = 512
MOD_COL_BLOCKS = 4
EV_CAST_ROWS = 720

ATT_HEADS = 8
ATT_KV_HEADS = 2
ATT_GROUP = ATT_HEADS // ATT_KV_HEADS
ATT_HD = 64
ATT_Q_W = ATT_HEADS * ATT_HD
ATT_KV_W = ATT_KV_HEADS * ATT_HD
ATT_W = ATT_Q_W + 2 * ATT_KV_W
WINDOW = 128
ATT_BLOCK = 128
GRID_W = 64
ROPE_BASE = 10000.0
NEG_INF = -1e30

RWKV_HEADS = 8
RWKV_HD = 64
RWKV_W = RWKV_HEADS * RWKV_HD
RWKV_LORA = 64
RWKV_IN = 3 * RWKV_W + 3 * 2 * RWKV_LORA
RWKV_CHUNK = 64
RWKV_GROUP = 4
RWKV_FINISH_ROWS = 256
DECAY_SCALE = float(np.exp(-0.5))
RWKV_BLOCK_ROWS = 1024
PAIR_W = 2 * RWKV_HD
N_PAIRS = RWKV_W // PAIR_W
GN_EPS = 64e-5

VMEM_LIMIT = 56 * 1024 * 1024

RWKV_PREC = dict(lora="bf", gate="bf", cumsum="x2r")


def _split_bf16(a):
    hi = a.astype(BF16)
    return hi, (a - hi.astype(F32)).astype(BF16)


def _dot(a, b, dims, prec):
    dn = (dims, ((), ()))
    one = lambda x, y: lax.dot_general(x, y, dn, preferred_element_type=F32)
    if prec == "x2r":
        ah = a.astype(BF16)
        bh, bl = _split_bf16(b)
        return one(ah, bh) + one(ah, bl)
    assert prec == "bf", prec
    return one(a.astype(BF16), b.astype(BF16))


def _mm(a, b, prec):
    return _dot(a, b, ((1,), (0,)), prec)


def _mm_nt(a, b, prec):
    return _dot(a, b, ((1,), (1,)), prec)


def _mm_tn(a, b, prec):
    return _dot(a, b, ((0,), (0,)), prec)


def _silu(x):
    return x * jax.nn.sigmoid(x)


def _softplus(x):
    return jnp.maximum(x, 0.0) + jnp.log1p(jnp.exp(-jnp.abs(x)))


def _rms(x, w):
    return x * lax.rsqrt(jnp.mean(x * x, axis=-1, keepdims=True) + NORM_EPS) * w


def _tri_masks(n, rev):
    r = lax.broadcasted_iota(jnp.int32, (n, n), 0)
    c = lax.broadcasted_iota(jnp.int32, (n, n), 1)
    if rev:
        return r <= c, r < c
    return r >= c, r > c


def _unit_tri_inv_many(mats, prec):
    assert prec == "bf"
    n = mats[0].shape[0]
    r = lax.broadcasted_iota(jnp.int32, (n, n), 0)
    c = lax.broadcasted_iota(jnp.int32, (n, n), 1)
    eye = jnp.where(r == c, 1.0, 0.0)
    within2 = jnp.where((r >> 1) == (c >> 1), 1.0, 0.0)
    ts = [eye - a * within2 for a in mats]
    abs_ = [a.astype(BF16) for a in mats]
    shift = 1
    while (1 << shift) < n:
        join = jnp.where(((r >> (shift + 1)) == (c >> (shift + 1))) & ((r >> shift) != (c >> shift)),
                         1.0, 0.0).astype(BF16)
        tbs = [t.astype(BF16) for t in ts]
        inner = [_mm(ab * join, tb, prec) for ab, tb in zip(abs_, tbs)]
        ts = [t - _mm(tb, w, prec) for t, tb, w in zip(ts, tbs, inner)]
        shift += 1
    return ts


def _shifted_rows(x, prev_row, next_row):
    n = x.shape[0]
    row = lax.broadcasted_iota(jnp.int32, x.shape, 0)
    xp = jnp.where(row == 0, prev_row, pltpu.roll(x, 1, 0))
    xn = jnp.where(row == n - 1, next_row, pltpu.roll(x, n - 1, 0))
    return xp, xn


def _mod_kernel(c_ref, w_ref, b_ref, evt_ref, o_ref, wt_ref):
    s = _silu(c_ref[...])
    o_ref[...] = _mm(s, w_ref[...], prec="bf") + b_ref[...]
    wt_ref[...] = evt_ref[...].astype(BF16)


def _modulation(c_rows, mod_w, mod_b, ev_w_in):
    depth = mod_w.shape[0]
    n_even, _, ev_cols = ev_w_in.shape
    nblk = MOD_COL_BLOCKS
    width = N_MOD * D_MODEL // nblk
    steps = depth * nblk
    rows = n_even * ev_cols
    n_side = rows // EV_CAST_ROWS
    assert rows % EV_CAST_ROWS == 0 and n_side <= steps and EV_CAST_ROWS % 16 == 0 and width % 128 == 0
    side = pl.BlockSpec((EV_CAST_ROWS, D_MODEL), lambda l, j: (jnp.minimum(l * nblk + j, n_side - 1), 0))
    out, w_t = pl.pallas_call(
        _mod_kernel,
        grid=(depth, nblk),
        in_specs=[
            pl.BlockSpec((MOD_ROWS, D_MODEL), lambda l, j: (0, 0)),
            pl.BlockSpec((None, D_MODEL, width), lambda l, j: (l, 0, j)),
            pl.BlockSpec((None, 1, width), lambda l, j: (l, 0, j)),
            side,
        ],
        out_specs=[pl.BlockSpec((None, MOD_ROWS, width), lambda l, j: (l, 0, j)), side],
        out_shape=[jax.ShapeDtypeStruct((depth, MOD_ROWS, N_MOD * D_MODEL), F32),
                   jax.ShapeDtypeStruct((rows, D_MODEL), BF16)],
        compiler_params=pltpu.CompilerParams(dimension_semantics=("arbitrary", "arbitrary"),
                                             vmem_limit_bytes=VMEM_LIMIT),
        name="modulation",
    )(c_rows, mod_w, mod_b.reshape(depth, 1, N_MOD * D_MODEL), jnp.swapaxes(ev_w_in, 1, 2).reshape(rows, D_MODEL))
    return out.reshape(depth, MOD_ROWS, N_MOD, D_MODEL), w_t.reshape(n_even, ev_cols, D_MODEL)


def _mod_spec(layer, tiles_per_seq, row_base, row_step):
    return pl.BlockSpec((None, None, N_MOD, D_MODEL),
                        lambda i: (layer, row_base + (i // tiles_per_seq) * row_step, 0, 0))


def _inproj_kernel(*refs, n_w):
    x_ref, mod_ref, nw_ref = refs[:3]
    w_refs = refs[3:3 + n_w]
    o_refs = refs[3 + n_w:]
    h = _rms(x_ref[...], nw_ref[...])
    h = (h * (1.0 + mod_ref[1:2, :]) + mod_ref[0:1, :]).astype(BF16)
    pieces = [_mm(h, w_ref[...], prec="bf") for w_ref in w_refs]
    y = pieces[0] if n_w == 1 else jnp.concatenate(pieces, axis=1)
    off = 0
    for o_ref in o_refs:
        n = o_ref.shape[-1]
        o_ref[...] = y[:, off:off + n]
        off += n


def _inproj(x, mods, norm_w, ws_bf16, splits, mod_spec):
    t = x.shape[0]
    assert sum(w.shape[1] for w in ws_bf16) == sum(splits) and all(w.shape[1] % 128 == 0 for w in ws_bf16)
    return pl.pallas_call(
        functools.partial(_inproj_kernel, n_w=len(ws_bf16)),
        grid=(t // TOKEN_TILE,),
        in_specs=[
            pl.BlockSpec((TOKEN_TILE, D_MODEL), lambda i: (i, 0)),
            mod_spec,
            pl.BlockSpec((1, D_MODEL), lambda i: (0, 0)),
        ] + [pl.BlockSpec(w.shape, lambda i: (0, 0)) for w in ws_bf16],
        out_specs=[pl.BlockSpec((TOKEN_TILE, n), lambda i: (i, 0)) for n in splits],
        out_shape=[jax.ShapeDtypeStruct((t, n), F32) for n in splits],
        compiler_params=pltpu.CompilerParams(dimension_semantics=("parallel",), vmem_limit_bytes=VMEM_LIMIT),
        name="inproj",
    )(x, mods, norm_w.reshape(1, D_MODEL), *ws_bf16)


def _inproj_even_kernel(xp_ref, x_ref, xn_ref, mod_ref, nw_ref, wt_ref, cqkv_ref, csc_ref,
                        alog_ref, dtb_ref, qkv_ref, gz_ref, sc_ref, gate_ref, *, seq_len):
    tile = x_ref.shape[0]
    proj = lambda row0, width: _mm_nt(h, wt_ref[row0:row0 + width, :], "bf")
    x = jnp.concatenate([xp_ref[...], x_ref[...], xn_ref[...]], axis=0)
    h = _rms(x, nw_ref[...])
    h = (h * (1.0 + mod_ref[1:2, :]) + mod_ref[0:1, :]).astype(BF16)
    n = tile + 2 * HALO
    first = pl.program_id(0) * tile - HALO
    pos = (first + lax.broadcasted_iota(jnp.int32, (n, 1), 0)) & (seq_len - 1)
    at_start = pos == 0
    at_end = pos == seq_len - 1

    def conv3(v, c_ref):
        vp = jnp.where(at_start, 0.0, pltpu.roll(v, 1, 0))
        vn = jnp.where(at_end, 0.0, pltpu.roll(v, n - 1, 0))
        return vp * c_ref[0:1, :] + v * c_ref[1:2, :] + vn * c_ref[2:3, :]

    keep = slice(HALO, HALO + tile)
    wide = 2 * GDN_D
    gate_row = GDN_QKV_W + GDN_HEADS * GDN_D
    sc_row = gate_row + 4 * GDN_HEADS
    for j in range(GDN_QKV_W // wide):
        cols = slice(j * wide, (j + 1) * wide)
        act = _silu(conv3(proj(j * wide, wide), cqkv_ref.at[:, cols]))[keep]
        for i in range(2):
            part = act[:, i * GDN_D:(i + 1) * GDN_D]
            if j < 2 * GDN_HEADS // 2:
                part = part * lax.rsqrt(jnp.sum(part * part, axis=-1, keepdims=True) + 1e-6)
                if j < GDN_HEADS // 2:
                    part = part * (GDN_D ** -0.5)
            qkv_ref[:, j * wide + i * GDN_D:j * wide + (i + 1) * GDN_D] = part.astype(BF16)
    for j in range(GDN_HEADS * GDN_D // wide):
        cols = slice(j * wide, (j + 1) * wide)
        gz_ref[:, cols] = _silu(proj(GDN_QKV_W + j * wide, wide)[keep]).astype(BF16)
    for j in range(SC_WIDTH // wide):
        sc_b, sc_c, sc_h = [proj(sc_row + i * SC_WIDTH + j * wide, wide) for i in range(3)]
        cols = slice(j * wide, (j + 1) * wide)
        sc_ref[:, cols] = (sc_b * conv3(sc_c * sc_h, csc_ref.at[:, cols]))[keep].astype(BF16)
    g = proj(gate_row, 128)[keep]
    lane = lax.broadcasted_iota(jnp.int32, g.shape, 1)
    gate_ref[...] = jnp.where(lane < 2 * GDN_HEADS, jax.nn.sigmoid(g),
                              -jnp.exp(alog_ref[...]) * _softplus(g + dtb_ref[...]))


def _inproj_even(x, mods, norm_w, w_t, conv_w, sc_conv_w, alog_vec, dtb_vec, seq_len, mod_spec):
    t = x.shape[0]
    tile = TOKEN_TILE
    assert seq_len & (seq_len - 1) == 0 and t % tile == 0
    per = tile // HALO
    last = t // HALO - 1
    const = lambda i: (0, 0)
    tok = lambda n: pl.BlockSpec((tile, n), lambda i: (i, 0))
    return pl.pallas_call(
        functools.partial(_inproj_even_kernel, seq_len=seq_len),
        grid=(t // tile,),
        in_specs=[
            pl.BlockSpec((HALO, D_MODEL), lambda i: (jnp.maximum(i * per - 1, 0), 0)),
            pl.BlockSpec((tile, D_MODEL), lambda i: (i, 0)),
            pl.BlockSpec((HALO, D_MODEL), lambda i: (jnp.minimum((i + 1) * per, last), 0)),
            mod_spec,
            pl.BlockSpec((1, D_MODEL), const),
            pl.BlockSpec(w_t.shape, const),
            pl.BlockSpec(conv_w.shape, const), pl.BlockSpec(sc_conv_w.shape, const),
            pl.BlockSpec((1, 128), const), pl.BlockSpec((1, 128), const),
        ],
        out_specs=[tok(GDN_QKV_W), tok(GDN_HEADS * GDN_D), tok(SC_WIDTH), tok(128)],
        out_shape=[jax.ShapeDtypeStruct((t, GDN_QKV_W), BF16), jax.ShapeDtypeStruct((t, GDN_HEADS * GDN_D), BF16),
                   jax.ShapeDtypeStruct((t, SC_WIDTH), BF16), jax.ShapeDtypeStruct((t, 128), F32)],
        compiler_params=pltpu.CompilerParams(dimension_semantics=("parallel",), vmem_limit_bytes=VMEM_LIMIT),
        name="inproj_even",
    )(x, x, x, mods, norm_w.reshape(1, D_MODEL), w_t, conv_w, sc_conv_w, alog_vec, dtb_vec)


def _mlp_kernel(a_ref, b_ref, x_ref, mod_ref, nw_ref, woa_ref, wob_ref, w1_ref, w2_ref, nf_ref, o_ref, *, final):
    y = _mm(a_ref[...], woa_ref[...], prec="bf") + _mm(b_ref[...], wob_ref[...], prec="bf")
    x1 = x_ref[...] + mod_ref[2:3, :] * y
    h = _rms(x1, nw_ref[...])
    h = (h * (1.0 + mod_ref[4:5, :]) + mod_ref[3:4, :]).astype(BF16)
    acc = jnp.zeros(x1.shape, F32)
    for j in range(D_FF // D_MODEL):
        cols = slice(j * D_MODEL, (j + 1) * D_MODEL)
        u = jnp.maximum(_mm(h, w1_ref[:, cols], prec="bf"), 0.0)
        acc = acc + _mm(u * u, w2_ref[cols, :], prec="bf")
    x2 = x1 + mod_ref[5:6, :] * acc
    if final:
        x2 = _rms(x2, nf_ref[...])
    o_ref[...] = x2


def _outproj_mlp(a, b, x, mods, norm_w, w_out, w1, w2, norm_final, mod_spec, final):
    t = x.shape[0]
    half = a.shape[1]
    const = lambda i: (0, 0)
    return pl.pallas_call(
        functools.partial(_mlp_kernel, final=final),
        grid=(t // TOKEN_TILE,),
        in_specs=[
            pl.BlockSpec((TOKEN_TILE, half), lambda i: (i, 0)),
            pl.BlockSpec((TOKEN_TILE, half), lambda i: (i, 0)),
            pl.BlockSpec((TOKEN_TILE, D_MODEL), lambda i: (i, 0)),
            mod_spec,
            pl.BlockSpec((1, D_MODEL), const),
            pl.BlockSpec((half, D_MODEL), const),
            pl.BlockSpec((half, D_MODEL), lambda i: (1, 0)),
            pl.BlockSpec((D_MODEL, D_FF), const),
            pl.BlockSpec((D_FF, D_MODEL), const),
            pl.BlockSpec((1, D_MODEL), const),
        ],
        out_specs=pl.BlockSpec((TOKEN_TILE, D_MODEL), lambda i: (i, 0)),
        out_shape=jax.ShapeDtypeStruct((t, D_MODEL), F32),
        compiler_params=pltpu.CompilerParams(dimension_semantics=("parallel",), vmem_limit_bytes=VMEM_LIMIT),
        name="outproj_mlp",
    )(a, b, x, mods, norm_w.reshape(1, D_MODEL), w_out, w_out, w1, w2, norm_final.reshape(1, D_MODEL))


def _gdn_decay_terms(g, rev):
    c = g.shape[0]
    incl = _tri_masks(c, rev)[0]
    before_col = _tri_masks(c, not rev)[0]
    eye = lax.broadcasted_iota(jnp.int32, (c, c), 0) == lax.broadcasted_iota(jnp.int32, (c, c), 1)
    gc_row = jnp.sum(jnp.where(before_col, jnp.broadcast_to(g, (c, c)), 0.0), axis=0, keepdims=True)
    gc_col = jnp.sum(jnp.where(eye, jnp.broadcast_to(gc_row, (c, c)), 0.0), axis=1, keepdims=True)
    decay = jnp.where(incl, jnp.exp(jnp.where(incl, gc_col - gc_row, 0.0)), 0.0)
    g_tot = jnp.sum(g, axis=0, keepdims=True)
    return decay, jnp.exp(gc_col), jnp.exp(g_tot - gc_col), jnp.exp(g_tot)


def _cast_side_job(arrays_layers, n_steps, step_index):
    in_specs, out_specs, out_shapes = [], [], []
    for arr, layer in arrays_layers:
        _, r, c = arr.shape
        blk = r // n_steps
        assert r % n_steps == 0 and blk % 16 == 0
        in_specs.append(pl.BlockSpec((None, blk, c), lambda *g, layer=layer: (layer, step_index(*g), 0)))
        out_specs.append(pl.BlockSpec((blk, c), lambda *g: (step_index(*g), 0)))
        out_shapes.append(jax.ShapeDtypeStruct((r, c), BF16))
    return in_specs, out_specs, out_shapes


def _gdn_kernel(*refs, seq_len, seqs, has_init, n_cast):
    q_ref, k_ref, v_ref, gz_ref, gate_ref, gn_ref = refs[:6]
    pos = 6
    if has_init:
        s0f_ref, s0b_ref = refs[pos:pos + 2]
        pos += 2
    cast_in = refs[pos:pos + n_cast]
    pos += n_cast
    o_ref, sf_ref, sb_ref = refs[pos:pos + 3]
    cast_out = refs[pos + 3:pos + 3 + n_cast]
    pos += n_cast
    for src, dst in zip(cast_in, cast_out):
        dst[...] = src[...].astype(BF16)
    osum, u_s, w_s, qd_s, kd_s, in_s, ge_s, st = refs[pos + 3:]
    hps = GDN_HEADS_PER_STEP
    head0 = pl.program_id(1) * hps
    ch = GDN_CHUNK
    n_chunks = seq_len // ch
    osum[...] = jnp.zeros(osum.shape, F32)
    hcols = [slice(hh * GDN_D, (hh + 1) * GDN_D) for hh in range(hps)]

    lane = lax.broadcasted_iota(jnp.int32, (ch, 128), 1)

    def pick(rows, col):
        return jnp.sum(jnp.where(lane == col, gate_ref[rows, :], 0.0), axis=1, keepdims=True)

    def solve_group(gi, carry):
        items = []
        for j in range(GDN_GROUP):
            c = gi * GDN_GROUP + j
            rows = pl.ds(pl.multiple_of(c * ch, ch), ch)
            for hh in range(hps):
                items.append(dict(c=c, hh=hh, rows=rows, q=q_ref[rows, hcols[hh]].astype(F32),
                                  k=k_ref[rows, hcols[hh]].astype(F32), v=v_ref[rows, hcols[hh]].astype(F32)))
        kks = [_mm_nt(it["k"], it["k"], "bf") for it in items]
        qks = [_mm_nt(it["q"], it["k"], "bf") for it in items]
        subs = []
        for it, kk, qk in zip(items, kks, qks):
            for d in range(2):
                beta = pick(it["rows"], d * GDN_HEADS + head0 + it["hh"])
                g = pick(it["rows"], 2 * GDN_HEADS + d * GDN_HEADS + head0 + it["hh"])
                decay, e_gc, e_rest, e_tot = _gdn_decay_terms(g, rev=(d == 1))
                strict = _tri_masks(ch, d == 1)[1]
                subs.append(dict(
                    d=d, c=it["c"], hh=it["hh"], rows=it["rows"],
                    a=jnp.where(strict, kk * beta * decay, 0.0),
                    rhs=jnp.concatenate([it["v"] * beta, it["k"] * (beta * e_gc)], axis=1),
                    intra=qk * decay, qd=it["q"] * e_gc, kd=it["k"] * e_rest, ge=e_tot))
        ts = _unit_tri_inv_many([s["a"] for s in subs], "bf")
        uws = [_mm(t, s["rhs"], "bf") for t, s in zip(ts, subs)]
        for s, uw in zip(subs, uws):
            d, hh, rows = s["d"], s["hh"], s["rows"]
            u_s[d, hh, rows, :] = uw[:, :GDN_D]
            w_s[d, hh, rows, :] = uw[:, GDN_D:].astype(BF16)
            qd_s[d, hh, rows, :] = s["qd"].astype(BF16)
            kd_s[d, hh, rows, :] = s["kd"].astype(BF16)
            in_s[d, hh, rows, :] = s["intra"].astype(BF16)
            ge_s[d, hh, pl.ds(pl.multiple_of(s["c"] * 8, 8), 8), :] = jnp.broadcast_to(s["ge"], (8, 128))
        return carry

    lax.fori_loop(0, seqs * n_chunks // GDN_GROUP, solve_group, 0)

    for hh in range(hps):
        for j in range(seqs):
            st[0, hh, j] = s0f_ref[j, hh] if has_init else jnp.zeros((GDN_D, GDN_D), F32)
            st[1, hh, j] = s0b_ref[j, hh] if has_init else jnp.zeros((GDN_D, GDN_D), F32)

    def recur(i, carry):
        cs = []
        for hh in range(hps):
            for j in range(seqs):
                for d in range(2):
                    c = j * n_chunks + (i if d == 0 else n_chunks - 1 - i)
                    cs.append(dict(d=d, hh=hh, j=j, rows=pl.ds(pl.multiple_of(c * ch, ch), ch),
                                   ge=ge_s[d, hh, pl.ds(pl.multiple_of(c * 8, 8), 1), :], s=st[d, hh, j]))
        sbs = [c["s"].astype(BF16) for c in cs]
        wss = [_mm(w_s[c["d"], c["hh"], c["rows"], :], sb, "bf") for c, sb in zip(cs, sbs)]
        qss = [_mm(qd_s[c["d"], c["hh"], c["rows"], :], sb, "bf") for c, sb in zip(cs, sbs)]
        ebs = [(u_s[c["d"], c["hh"], c["rows"], :] - ws).astype(BF16) for c, ws in zip(cs, wss)]
        outs = [qs_ + _mm(in_s[c["d"], c["hh"], c["rows"], :], eb, "bf") for c, qs_, eb in zip(cs, qss, ebs)]
        s_news = [c["s"] * c["ge"] + _mm_tn(kd_s[c["d"], c["hh"], c["rows"], :], eb, "bf")
                  for c, eb in zip(cs, ebs)]
        for c, o, s_new in zip(cs, outs, s_news):
            st[c["d"], c["hh"], c["j"]] = s_new
            osum[c["rows"], hcols[c["hh"]]] += o
        return carry

    lax.fori_loop(0, n_chunks, recur, 0)
    for hh in range(hps):
        for j in range(seqs):
            sf_ref[j, hh] = st[0, hh, j]
            sb_ref[j, hh] = st[1, hh, j]
        o_ref[:, hcols[hh]] = (_rms(osum[:, hcols[hh]], gn_ref[...]) * gz_ref[:, hcols[hh]].astype(F32))


def _gdn_mixer(qkv, gz, gates, n_seq, seq_len, gdn_norm, s0_f, s0_b, cast=()):
    has_init = s0_f is not None
    hd = GDN_HEADS
    hps = GDN_HEADS_PER_STEP
    hgroups = hd // hps
    seqs = max(1, GDN_BLOCK_ROWS // seq_len)
    rows = seqs * seq_len
    assert n_seq % seqs == 0 and (rows // GDN_CHUNK) % GDN_GROUP == 0 and hd % hps == 0

    def col(block):
        return pl.BlockSpec((rows, hps * GDN_D), lambda s, h, b=block: (s, b * hgroups + h))

    state = pl.BlockSpec((seqs, None, hps, GDN_D, GDN_D), lambda s, h: (s, 0, h, 0, 0))
    in_specs = [col(0), col(1), col(2), col(0), pl.BlockSpec((rows, 128), lambda s, h: (s, 0)),
                pl.BlockSpec((1, 128), lambda s, h: (0, 0))]
    args = [qkv, qkv, qkv, gz, gates, gdn_norm.reshape(1, 128)]
    if has_init:
        in_specs += [state, state]
        args += [s0_f, s0_b]
    n_steps = (n_seq // seqs) * hgroups
    cast_in, cast_out, cast_shapes = _cast_side_job(cast, n_steps, lambda s, h: s * hgroups + h)
    in_specs += cast_in
    args += [arr for arr, _ in cast]
    t = n_seq * seq_len
    scratch = ([pltpu.VMEM((rows, hps * GDN_D), F32)]
               + [pltpu.VMEM((2, hps, rows, GDN_D), F32)]
               + [pltpu.VMEM((2, hps, rows, GDN_D), BF16) for _ in range(3)]
               + [pltpu.VMEM((2, hps, rows, GDN_CHUNK), BF16),
                  pltpu.VMEM((2, hps, rows // GDN_CHUNK * 8, 128), F32),
                  pltpu.VMEM((2, hps, seqs, GDN_D, GDN_D), F32)])
    return pl.pallas_call(
        functools.partial(_gdn_kernel, seq_len=seq_len, seqs=seqs, has_init=has_init, n_cast=len(cast)),
        grid=(n_seq // seqs, hgroups),
        in_specs=in_specs,
        out_specs=[pl.BlockSpec((rows, hps * GDN_D), lambda s, h: (s, h)), state, state] + cast_out,
        out_shape=[jax.ShapeDtypeStruct((t, hd * GDN_D), F32),
                   jax.ShapeDtypeStruct((n_seq, 1, hd, GDN_D, GDN_D), F32),
                   jax.ShapeDtypeStruct((n_seq, 1, hd, GDN_D, GDN_D), F32)] + cast_shapes,
        scratch_shapes=scratch,
        compiler_params=pltpu.CompilerParams(dimension_semantics=("parallel", "parallel"),
                                             vmem_limit_bytes=VMEM_LIMIT),
        name="gdn_mixer",
    )(*args)


def _softmax_pv(scores, values, sink):
    m = sink
    for s in scores:
        m = jnp.maximum(m, jnp.max(s, axis=-1, keepdims=True))
    den = jnp.exp(sink - m)
    acc = None
    for s, v in zip(scores, values):
        e = jnp.exp(s - m)
        den = den + jnp.sum(e, axis=-1, keepdims=True)
        pv = _mm(e, v, prec="bf")
        acc = pv if acc is None else acc + pv
    return acc / den


def _group_sinks(sink_ref, j, rows):
    assert rows & (rows - 1) == 0
    grp = lax.broadcasted_iota(jnp.int32, (ATT_GROUP * rows, 1), 0) >> (rows.bit_length() - 1)
    col = jnp.full(grp.shape, sink_ref[j * ATT_GROUP], F32)
    for gi in range(1, ATT_GROUP):
        col = jnp.where(grp == gi, sink_ref[j * ATT_GROUP + gi], col)
    return col


def _store_group(o_ref, j, o, rows):
    for gi in range(ATT_GROUP):
        hh = j * ATT_GROUP + gi
        o_ref[:, hh * ATT_HD:(hh + 1) * ATT_HD] = o[gi * rows:(gi + 1) * rows]


def _attn_ctx_kernel(*refs, n_cast):
    sink_ref, p_ref = refs[:2]
    cast_in = refs[2:2 + n_cast]
    o_ref, kc_ref, vc_ref = refs[2 + n_cast:5 + n_cast]
    for src, dst in zip(cast_in, refs[5 + n_cast:]):
        dst[...] = src[...].astype(BF16)
    scale = ATT_HD ** -0.5
    rows = p_ref.shape[0]
    for j in range(ATT_KV_HEADS):
        k = p_ref[:, ATT_Q_W + j * ATT_HD:ATT_Q_W + (j + 1) * ATT_HD]
        v = p_ref[:, ATT_Q_W + ATT_KV_W + j * ATT_HD:ATT_Q_W + ATT_KV_W + (j + 1) * ATT_HD]
        kc_ref[j] = k
        vc_ref[j] = v
        q = jnp.concatenate([p_ref[:, hh * ATT_HD:(hh + 1) * ATT_HD]
                             for hh in range(j * ATT_GROUP, (j + 1) * ATT_GROUP)], axis=0)
        s = _mm_nt(q, k, prec="bf") * scale
        _store_group(o_ref, j, _softmax_pv([s], [v], _group_sinks(sink_ref, j, rows)), rows)


def _attn_context(proj_att, sink, n_seq, seq_len, cast=()):
    kv = pl.BlockSpec((None, None, ATT_KV_HEADS, seq_len, ATT_HD), lambda b: (b, 0, 0, 0, 0))
    cast_in, cast_out, cast_shapes = _cast_side_job(cast, n_seq, lambda b: b)
    return pl.pallas_call(
        functools.partial(_attn_ctx_kernel, n_cast=len(cast)),
        grid=(n_seq,),
        in_specs=[pl.BlockSpec(memory_space=pltpu.SMEM),
                  pl.BlockSpec((seq_len, ATT_W), lambda b: (b, 0))] + cast_in,
        out_specs=[pl.BlockSpec((seq_len, ATT_Q_W), lambda b: (b, 0)), kv, kv] + cast_out,
        out_shape=[jax.ShapeDtypeStruct((n_seq * seq_len, ATT_Q_W), F32),
                   jax.ShapeDtypeStruct((n_seq, 1, ATT_KV_HEADS, seq_len, ATT_HD), F32),
                   jax.ShapeDtypeStruct((n_seq, 1, ATT_KV_HEADS, seq_len, ATT_HD), F32)] + cast_shapes,
        compiler_params=pltpu.CompilerParams(dimension_semantics=("parallel",), vmem_limit_bytes=VMEM_LIMIT),
        name="attn_context",
    )(sink, proj_att, *[arr for arr, _ in cast])


def _rope_tables(seq_len):
    pos = np.arange(seq_len)
    half = ATT_HD // 2
    inv = ROPE_BASE ** (-np.arange(0, half, 2, dtype=np.float32) / half)
    ang_r = (pos // GRID_W).astype(np.float32)[:, None] * inv
    ang_c = (pos % GRID_W).astype(np.float32)[:, None] * inv
    cos = np.concatenate([np.cos(ang_r), np.cos(ang_r), np.cos(ang_c), np.cos(ang_c)], axis=1)
    sin = np.concatenate([-np.sin(ang_r), np.sin(ang_r), -np.sin(ang_c), np.sin(ang_c)], axis=1)
    return (jnp.asarray(np.tile(cos, (1, 2)), F32), jnp.asarray(np.tile(sin, (1, 2)), F32))


def _rope(x, cos, sin):
    lane = lax.broadcasted_iota(jnp.int32, x.shape, 1)
    partner = jnp.where((lane & 31) < 16, pltpu.roll(x, 128 - 16, 1), pltpu.roll(x, 16, 1))
    return x * cos + partner * sin


def _attn_lat_kernel(sink_ref, p_ref, ck_ref, cv_ref, cos_ref, sin_ref, o_ref, *, seq_len):
    scale = ATT_HD ** -0.5
    qb = pl.program_id(1)
    span = 3 * ATT_BLOCK
    q0 = pl.multiple_of(qb * ATT_BLOCK, ATT_BLOCK)
    k0 = pl.multiple_of(jnp.clip((qb - 1) * ATT_BLOCK, 0, seq_len - span), ATT_BLOCK)
    qrows = pl.ds(q0, ATT_BLOCK)
    krows = pl.ds(k0, span)
    kwin = _rope(p_ref[krows, ATT_Q_W:ATT_Q_W + ATT_KV_W], cos_ref[krows, :], sin_ref[krows, :])
    vwin = p_ref[krows, ATT_Q_W + ATT_KV_W:ATT_W]
    stacked = ATT_GROUP * ATT_BLOCK
    qpos = q0 + (lax.broadcasted_iota(jnp.int32, (stacked, span), 0) & (ATT_BLOCK - 1))
    kpos = k0 + lax.broadcasted_iota(jnp.int32, (stacked, span), 1)
    valid = jnp.abs(qpos - kpos) <= WINDOW
    cos_q = cos_ref[qrows, :]
    sin_q = sin_ref[qrows, :]
    heads = []
    for pair in range(ATT_HEADS // 2):
        qpair = _rope(p_ref[qrows, pair * 128:(pair + 1) * 128], cos_q, sin_q)
        heads += [qpair[:, :ATT_HD], qpair[:, ATT_HD:]]
    kv = range(ATT_KV_HEADS)
    qs = [jnp.concatenate(heads[j * ATT_GROUP:(j + 1) * ATT_GROUP], axis=0) for j in kv]
    s_locs = [jnp.where(valid, _mm_nt(qs[j], kwin[:, j * ATT_HD:(j + 1) * ATT_HD], prec="bf") * scale, NEG_INF)
              for j in kv]
    s_ctxs = [_mm_nt(qs[j], ck_ref[j], prec="bf") * scale for j in kv]
    outs = [_softmax_pv([s_locs[j], s_ctxs[j]], [vwin[:, j * ATT_HD:(j + 1) * ATT_HD], cv_ref[j]],
                        _group_sinks(sink_ref, j, ATT_BLOCK)) for j in kv]
    for j in kv:
        _store_group(o_ref, j, outs[j], ATT_BLOCK)


def _attn_latent(proj_att, sink, cache_k, cache_v, layer, n_seq, seq_len):
    cos, sin = _rope_tables(seq_len)
    past = cache_k.shape[3]
    nqb = seq_len // ATT_BLOCK
    cache = pl.BlockSpec((None, None, ATT_KV_HEADS, past, ATT_HD), lambda b, q: (b, layer, 0, 0, 0))
    table = pl.BlockSpec((seq_len, 128), lambda b, q: (0, 0))
    return pl.pallas_call(
        functools.partial(_attn_lat_kernel, seq_len=seq_len),
        grid=(n_seq, nqb),
        in_specs=[pl.BlockSpec(memory_space=pltpu.SMEM),
                  pl.BlockSpec((seq_len, ATT_W), lambda b, q: (b, 0)),
                  cache, cache, table, table],
        out_specs=pl.BlockSpec((ATT_BLOCK, ATT_Q_W), lambda b, q: (b * nqb + q, 0)),
        out_shape=jax.ShapeDtypeStruct((n_seq * seq_len, ATT_Q_W), F32),
        compiler_params=pltpu.CompilerParams(dimension_semantics=("parallel", "parallel")),
        name="attn_latent",
    )(sink, proj_att, cache_k, cache_v, cos, sin)


def _pair_masks(n, rev):
    r = lax.broadcasted_iota(jnp.int32, (n, 2 * n), 0)
    c = lax.broadcasted_iota(jnp.int32, (n, 2 * n), 1) & (n - 1)
    return (r <= c, r < c) if rev else (r >= c, r > c)


def _bd(x):
    half = x.shape[1] // 2
    lane = lax.broadcasted_iota(jnp.int32, x.shape, 1)
    zero = jnp.zeros_like(x)
    return jnp.concatenate([jnp.where(lane < half, x, zero), jnp.where(lane >= half, x, zero)], axis=0)


def _bd_mask(n):
    r = lax.broadcasted_iota(jnp.int32, (2 * n, 2 * n), 0)
    c = lax.broadcasted_iota(jnp.int32, (2 * n, 2 * n), 1)
    return (r < n) == (c < n)


def _head_sums(x):
    ones = jnp.where(_bd_mask(RWKV_HD), 1.0, 0.0).astype(BF16)
    return _mm(x, ones, "bf")


def _apply_pairs(tb, x):
    hi, lo = _split_bf16(x)
    return _mm(tb, _bd(hi), "bf") + _mm(tb, _bd(lo), "bf")


def _unit_tri_inv_pairs(mats):
    n = mats[0].shape[0]
    r = lax.broadcasted_iota(jnp.int32, (n, 2 * n), 0)
    c = lax.broadcasted_iota(jnp.int32, (n, 2 * n), 1) & (n - 1)
    eye = jnp.where(r == c, 1.0, 0.0)
    within2 = jnp.where((r >> 1) == (c >> 1), 1.0, 0.0)
    ts = [eye - a * within2 for a in mats]
    abs_ = [a.astype(BF16) for a in mats]
    shift = 1
    while (1 << shift) < n:
        join = jnp.where(((r >> (shift + 1)) == (c >> (shift + 1))) & ((r >> shift) != (c >> shift)),
                         1.0, 0.0).astype(BF16)
        tbs = [t.astype(BF16) for t in ts]
        inner = [_mm(ab * join, _bd(tb), "bf") for ab, tb in zip(abs_, tbs)]
        ts = [t - _mm(tb, _bd(w.astype(BF16)), "bf") for t, tb, w in zip(ts, tbs, inner)]
        shift += 1
    return ts


def _rwkv_chunk_operands(x_ref, prm, sc, c, *, seq_len):
    (mu_ref, w0_ref, wup_ref, a0_ref, aup_ref, gup_ref, kk_ref, ka_ref, rk_ref) = prm
    ch = RWKV_CHUNK
    total = x_ref.shape[0]
    r0 = pl.multiple_of(c * ch, ch)
    rows = pl.ds(r0, ch)
    pos0 = r0 & (seq_len - 1)
    x = x_ref[rows, :]
    prev_row = x_ref[pl.ds(jnp.maximum(r0 - 1, 0), 1), :] * jnp.where(pos0 > 0, 1.0, 0.0)
    next_row = x_ref[pl.ds(jnp.minimum(r0 + ch, total - 1), 1), :] * jnp.where(pos0 + ch < seq_len, 1.0, 0.0)
    xp, xn = _shifted_rows(x, prev_row, next_row)
    mu0, mu1 = mu_ref[0:1, :], mu_ref[1:2, :]
    xs = x * (1.0 - mu0 - mu1) + xp * mu0 + xn * mu1
    r = xs[:, 0:RWKV_W]
    k = xs[:, RWKV_W:2 * RWKV_W]
    v = xs[:, 2 * RWKV_W:3 * RWKV_W]
    lo = 3 * RWKV_W
    p = RWKV_PREC
    gl = xs[:, lo + 4 * RWKV_LORA:lo + 6 * RWKV_LORA]
    sc["gate"][rows, :] = _mm(jax.nn.sigmoid(gl), gup_ref[...], p["gate"])
    sc["v"][rows, :] = v.astype(BF16)
    pairs = [slice(i * PAIR_W, (i + 1) * PAIR_W) for i in range(N_PAIRS)]
    kkv = k * kk_ref[...]
    kaps = []
    for cols in pairs:
        kk_p = kkv[:, cols]
        kaps.append(kk_p * lax.rsqrt(_head_sums(kk_p * kk_p) + 1e-6))
    items = []
    bonus = None
    for d in range(2):
        rev = d == 1
        wl = xs[:, lo + d * RWKV_LORA:lo + (d + 1) * RWKV_LORA]
        al = xs[:, lo + 2 * RWKV_LORA + d * RWKV_LORA:lo + 2 * RWKV_LORA + (d + 1) * RWKV_LORA]
        lw = -DECAY_SCALE * jax.nn.sigmoid(w0_ref[d:d + 1, :] + _mm(jnp.tanh(wl), wup_ref[d], p["lora"]))
        a = jax.nn.sigmoid(a0_ref[d:d + 1, :] + _mm(al, aup_ref[d], p["lora"]))
        k2 = k * (1.0 + (a - 1.0) * ka_ref[...])
        g_inc = _mm(jnp.where(_tri_masks(ch, rev)[0], 1.0, 0.0), lw, p["cumsum"])
        g_tot = jnp.sum(lw, axis=0, keepdims=True)
        e_neg = jnp.exp(-g_inc)
        e_end = jnp.exp(g_tot - g_inc)
        e_exc = jnp.exp(g_inc - lw)
        r_dec = r * jnp.exp(g_inc)
        k_neg = k2 * e_neg
        k_end = k2 * e_end
        sc["dec"][d, pl.ds(pl.multiple_of(c * 8, 8), 8), :] = jnp.broadcast_to(jnp.exp(g_tot), (8, RWKV_W))
        rkr = r * k2 * rk_ref[...]
        bon_d = jnp.concatenate([_head_sums(rkr[:, cols]) for cols in pairs], axis=1) * v
        bonus = bon_d if bonus is None else bonus + bon_d
        for cols, kap in zip(pairs, kaps):
            b_p = kap * a[:, cols]
            items.append(dict(d=d, rows=rows, cols=cols, kap_dec=kap * e_exc[:, cols], r_dec=r_dec[:, cols],
                              b_neg=b_p * e_neg[:, cols], k_neg=k_neg[:, cols],
                              b_end=b_p * e_end[:, cols], k_end=k_end[:, cols], vb=v[:, cols].astype(BF16)))
    sc["bon"][rows, :] = bonus
    return items


def _rwkv_solve(items, sc):
    ch = RWKV_CHUNK
    masks = [_pair_masks(ch, False), _pair_masks(ch, True)]
    ms = [_mm_nt(jnp.concatenate([it["kap_dec"], it["r_dec"]], axis=0),
                 jnp.concatenate([_bd(it["b_neg"].astype(BF16)), _bd(it["k_neg"].astype(BF16))], axis=0), "bf")
          for it in items]
    a_abs = [jnp.where(masks[it["d"]][1], m[:ch, :2 * ch], 0.0) for it, m in zip(items, ms)]
    a_aks = [jnp.where(masks[it["d"]][1], m[:ch, 2 * ch:], 0.0) for it, m in zip(items, ms)]
    a_rbs = [jnp.where(masks[it["d"]][0], m[ch:, :2 * ch], 0.0) for it, m in zip(items, ms)]
    a_rks = [jnp.where(masks[it["d"]][0], m[ch:, 2 * ch:], 0.0) for it, m in zip(items, ms)]
    tbs = [t.astype(BF16) for t in _unit_tri_inv_pairs(a_abs)]
    akvs = [_mm(a_ak, _bd(it["vb"]), "bf") for it, a_ak in zip(items, a_aks)]
    w2s = [_apply_pairs(tb, it["kap_dec"]) for it, tb in zip(items, tbs)]
    u0s = [_apply_pairs(tb, akv) for tb, akv in zip(tbs, akvs)]
    for it, w2, u0, a_rb, a_rk in zip(items, w2s, u0s, a_rbs, a_rks):
        d, rows, cols = it["d"], it["rows"], it["cols"]
        sc["w2"][d, rows, cols] = w2.astype(BF16)
        sc["rd"][d, rows, cols] = it["r_dec"].astype(BF16)
        sc["u0"][d, rows, cols] = u0
        sc["arb"][d, rows, cols] = a_rb.astype(BF16)
        sc["ark"][d, rows, cols] = a_rk.astype(BF16)
        sc["bh"][d, rows, cols] = it["b_end"].astype(BF16)
        sc["kh"][d, rows, cols] = it["k_end"].astype(BF16)


def _rwkv_recur(sc, st, i, *, seq_len, seqs):
    ch = RWKV_CHUNK
    n_chunks = seq_len // ch
    keep = _bd_mask(RWKV_HD)
    cs = []
    for j in range(seqs):
        for d in range(2):
            c = j * n_chunks + (i if d == 0 else n_chunks - 1 - i)
            rows = pl.ds(pl.multiple_of(c * ch, ch), ch)
            dec = sc["dec"][d, pl.ds(pl.multiple_of(c * 8, 8), 1), :]
            for pi in range(N_PAIRS):
                cols = slice(pi * PAIR_W, (pi + 1) * PAIR_W)
                cs.append(dict(d=d, j=j, p=pi, rows=rows, cols=cols, dec=dec[:, cols], s=st[d, j, pi]))
    sbs = [c["s"].astype(BF16) for c in cs]
    lss = [_mm_nt(jnp.concatenate([sc["w2"][c["d"], c["rows"], c["cols"]],
                                   sc["rd"][c["d"], c["rows"], c["cols"]]], axis=0), sb, "bf")
           for c, sb in zip(cs, sbs)]
    ubs = [(-(ls[:ch] + sc["u0"][c["d"], c["rows"], c["cols"]])).astype(BF16) for c, ls in zip(cs, lss)]
    vbs = [sc["v"][c["rows"], c["cols"]] for c in cs]
    ys = [ls[ch:] + _mm(jnp.concatenate([sc["arb"][c["d"], c["rows"], c["cols"]],
                                         sc["ark"][c["d"], c["rows"], c["cols"]]], axis=1),
                        jnp.concatenate([_bd(ub), _bd(vb)], axis=0), "bf")
          for c, ls, ub, vb in zip(cs, lss, ubs, vbs)]
    ups = [_mm_tn(jnp.concatenate([ub, vb], axis=0),
                  jnp.concatenate([sc["bh"][c["d"], c["rows"], c["cols"]],
                                   sc["kh"][c["d"], c["rows"], c["cols"]]], axis=0), "bf")
           for c, ub, vb in zip(cs, ubs, vbs)]
    for c, y, up in zip(cs, ys, ups):
        st[c["d"], c["j"], c["p"]] = c["s"] * c["dec"] + jnp.where(keep, up, 0.0)
        sc["ysum"][c["rows"], c["cols"]] += y


RWKV_SCRATCH = ("ysum", "bon", "gate", "v", "w2", "rd", "u0", "arb", "ark", "bh", "kh", "dec")


def _rwkv_kernel(*refs, seq_len, seqs, has_init):
    x_ref = refs[0]
    prm = refs[1:10]
    lnw_ref, lnb_ref = refs[10:12]
    pos = 12
    if has_init:
        s0f_ref, s0b_ref = refs[pos:pos + 2]
        pos += 2
    o_ref, sf_ref, sb_ref = refs[pos:pos + 3]
    sc = dict(zip(RWKV_SCRATCH, refs[pos + 3:]))
    st = refs[pos + 3 + len(RWKV_SCRATCH)]
    ch = RWKV_CHUNK
    hd = RWKV_HD
    n_chunks = seq_len // ch
    sc["ysum"][...] = jnp.zeros(sc["ysum"].shape, F32)

    def prepare(gi, carry):
        items = []
        for j in range(RWKV_GROUP):
            items += _rwkv_chunk_operands(x_ref, prm, sc, gi * RWKV_GROUP + j, seq_len=seq_len)
        _rwkv_solve(items, sc)
        return carry

    lax.fori_loop(0, seqs * n_chunks // RWKV_GROUP, prepare, 0)

    zero = jnp.zeros((hd, hd), F32)
    for d, s0_ref in enumerate((s0f_ref, s0b_ref) if has_init else (None, None)):
        for j in range(seqs):
            for pi in range(N_PAIRS):
                s_a = s0_ref[j, 2 * pi] if has_init else zero
                s_b = s0_ref[j, 2 * pi + 1] if has_init else zero
                st[d, j, pi] = jnp.concatenate([jnp.concatenate([s_a, zero], axis=1),
                                                jnp.concatenate([zero, s_b], axis=1)], axis=0)

    def recur(i, carry):
        _rwkv_recur(sc, st, i, seq_len=seq_len, seqs=seqs)
        return carry

    lax.fori_loop(0, n_chunks, recur, 0)
    for d, out_ref in enumerate((sf_ref, sb_ref)):
        for j in range(seqs):
            for pi in range(N_PAIRS):
                s = st[d, j, pi]
                out_ref[j, 2 * pi] = s[:hd, :hd]
                out_ref[j, 2 * pi + 1] = s[hd:, hd:]

    tile = RWKV_FINISH_ROWS

    def finish(i, carry):
        rows = pl.ds(pl.multiple_of(i * tile, tile), tile)
        pairs = [slice(pi * PAIR_W, (pi + 1) * PAIR_W) for pi in range(N_PAIRS)]
        ys = [sc["ysum"][rows, cols] for cols in pairs]
        cens = [y - _head_sums(y) * (1.0 / hd) for y in ys]
        vars_ = [_head_sums(cen * cen) * (1.0 / hd) for cen in cens]
        for cols, cen, var in zip(pairs, cens, vars_):
            yn = cen * lax.rsqrt(var + GN_EPS) * lnw_ref[:, cols] + lnb_ref[:, cols]
            o_ref[rows, cols] = (yn + sc["bon"][rows, cols]) * sc["gate"][rows, cols]
        return carry

    lax.fori_loop(0, seqs * seq_len // tile, finish, 0)


def _rwkv_mixer(x_rw, n_seq, seq_len, params, s0_f, s0_b):
    has_init = s0_f is not None
    (mu, w0, w_up, a0, a_up, g_up, k_k, k_a, r_k, ln_w, ln_b) = params
    row = lambda a: a.reshape(1, RWKV_W)
    args = [x_rw, mu, w0, w_up, a0, a_up, g_up, row(k_k), row(k_a), row(r_k), row(ln_w), row(ln_b)]

    def whole(a):
        nd = a.ndim
        return pl.BlockSpec(a.shape, lambda s, nd=nd: (0,) * nd)

    seqs = max(1, RWKV_BLOCK_ROWS // seq_len)
    rows = seqs * seq_len
    assert n_seq % seqs == 0 and (rows // RWKV_CHUNK) % RWKV_GROUP == 0 and seq_len & (seq_len - 1) == 0
    in_specs = [pl.BlockSpec((rows, RWKV_IN), lambda s: (s, 0))] + [whole(a) for a in args[1:]]
    state = pl.BlockSpec((seqs, None, RWKV_HEADS, RWKV_HD, RWKV_HD), lambda s: (s, 0, 0, 0, 0))
    if has_init:
        in_specs += [state, state]
        args += [s0_f, s0_b]
    st_shape = jax.ShapeDtypeStruct((n_seq, 1, RWKV_HEADS, RWKV_HD, RWKV_HD), F32)
    tok = lambda dt: pltpu.VMEM((rows, RWKV_W), dt)
    per_dir = lambda dt: pltpu.VMEM((2, rows, RWKV_W), dt)
    scratch = dict(ysum=tok(F32), bon=tok(F32), gate=tok(F32), v=tok(BF16), w2=per_dir(BF16), rd=per_dir(BF16),
                   u0=per_dir(F32), arb=per_dir(BF16), ark=per_dir(BF16), bh=per_dir(BF16), kh=per_dir(BF16),
                   dec=pltpu.VMEM((2, rows // RWKV_CHUNK * 8, RWKV_W), F32))
    return pl.pallas_call(
        functools.partial(_rwkv_kernel, seq_len=seq_len, seqs=seqs, has_init=has_init),
        grid=(n_seq // seqs,),
        in_specs=in_specs,
        out_specs=[pl.BlockSpec((rows, RWKV_W), lambda s: (s, 0)), state, state],
        out_shape=[jax.ShapeDtypeStruct((n_seq * seq_len, RWKV_W), F32), st_shape, st_shape],
        scratch_shapes=[scratch[name] for name in RWKV_SCRATCH]
        + [pltpu.VMEM((2, seqs, N_PAIRS, PAIR_W, PAIR_W), F32)],
        compiler_params=pltpu.CompilerParams(dimension_semantics=("parallel",), vmem_limit_bytes=VMEM_LIMIT),
        name="rwkv_mixer",
    )(*args)


def kernel(x_prompt, x_sample, state_gdn_fwd, state_gdn_bwd, cache_attn_k, cache_attn_v, state_rwkv_fwd, state_rwkv_bwd, c, c_ctx, mod_w, mod_b, norm_mix, norm_mlp, mlp_w1, mlp_w2, norm_final, ev_w_in, ev_w_out, gdn_conv, gdn_a_log, gdn_dt_bias, gdn_norm, sc_conv, od_w_in, od_w_out, attn_sink, rwkv_mu, rwkv_w0, rwkv_w_up, rwkv_a0, rwkv_a_up, rwkv_g_up, rwkv_k_k, rwkv_k_a, rwkv_r_k, rwkv_ln_w, rwkv_ln_b):
    bp, lp, _ = x_prompt.shape
    bs, ls, _ = x_sample.shape
    depth = mod_w.shape[0]
    c_rows = jnp.concatenate([c_ctx[None, :], c, jnp.zeros((MOD_ROWS - 1 - bs, D_MODEL), F32)], axis=0)
    mods, ev_w_t = _modulation(c_rows, mod_w, mod_b, ev_w_in)

    assert ls % TOKEN_TILE == 0 and (bp * lp) % TOKEN_TILE == 0
    groups = [
        dict(x=x_prompt.reshape(bp * lp, D_MODEL), n=bp, l=lp, latent=False,
             mod=lambda layer: _mod_spec(layer, 1, 0, 0)),
        dict(x=x_sample.reshape(bs * ls, D_MODEL), n=bs, l=ls, latent=True,
             mod=lambda layer: _mod_spec(layer, ls // TOKEN_TILE, 1, 1)),
    ]
    outs = {}
    for layer in range(depth):
        final = layer == depth - 1
        mlp_cast = ((mlp_w1, layer), (mlp_w2, layer))
        if layer % 2 == 0:
            e = layer // 2
            alog_vec = jnp.zeros((1, 128), F32).at[0, 2 * GDN_HEADS:4 * GDN_HEADS].set(gdn_a_log[e].reshape(-1))
            dtb_vec = jnp.zeros((1, 128), F32).at[0, 2 * GDN_HEADS:4 * GDN_HEADS].set(gdn_dt_bias[e].reshape(-1))
            for grp in groups:
                qkv, gz, sc, gates = _inproj_even(grp["x"], mods, norm_mix[layer], ev_w_t[e], gdn_conv[e],
                                                  sc_conv[e], alog_vec, dtb_vec, grp["l"], grp["mod"](layer))
                s0 = (state_gdn_fwd[:, e:e + 1], state_gdn_bwd[:, e:e + 1]) if grp["latent"] else (None, None)
                if grp["latent"]:
                    cast = ((od_w_in, layer // 2), (od_w_out, layer // 2)) if layer + 1 < depth else ()
                    o, _, _, *next_w = _gdn_mixer(qkv, gz, gates, grp["n"], grp["l"], gdn_norm[e], *s0, cast=cast)
                else:
                    o, s_f, s_b, w_out, w1, w2 = _gdn_mixer(qkv, gz, gates, grp["n"], grp["l"], gdn_norm[e], *s0,
                                                            cast=((ev_w_out, e),) + mlp_cast)
                    outs.setdefault("gdn_f", []).append(s_f)
                    outs.setdefault("gdn_b", []).append(s_b)
                grp["x"] = _outproj_mlp(o, sc, grp["x"], mods, norm_mlp[layer], w_out, w1, w2, norm_final,
                                        grp["mod"](layer), final)
        else:
            o_ = layer // 2
            w_in, w_out = [next_w[0]], next_w[1]
            rw = (rwkv_mu[o_], rwkv_w0[o_], rwkv_w_up[o_], rwkv_a0[o_], rwkv_a_up[o_], rwkv_g_up[o_],
                  rwkv_k_k[o_], rwkv_k_a[o_], rwkv_r_k[o_].reshape(-1), rwkv_ln_w[o_], rwkv_ln_b[o_])
            for grp in groups:
                p_att, x_rw = _inproj(grp["x"], mods, norm_mix[layer], w_in, (ATT_W, RWKV_IN), grp["mod"](layer))
                if grp["latent"]:
                    att = _attn_latent(p_att, attn_sink[o_], cache_attn_k, cache_attn_v, o_, grp["n"], grp["l"])
                    rwo, _, _ = _rwkv_mixer(x_rw, grp["n"], grp["l"], rw,
                                            state_rwkv_fwd[:, o_:o_ + 1], state_rwkv_bwd[:, o_:o_ + 1])
                else:
                    att, kc, vc, w1, w2 = _attn_context(p_att, attn_sink[o_], grp["n"], grp["l"], cast=mlp_cast)
                    rwo, s_f, s_b = _rwkv_mixer(x_rw, grp["n"], grp["l"], rw, None, None)
                    outs.setdefault("att_k", []).append(kc)
                    outs.setdefault("att_v", []).append(vc)
                    outs.setdefault("rw_f", []).append(s_f)
                    outs.setdefault("rw_b", []).append(s_b)
                grp["x"] = _outproj_mlp(att, rwo, grp["x"], mods, norm_mlp[layer], w_out, w1, w2, norm_final,
                                        grp["mod"](layer), final)
    cat = lambda key: jnp.concatenate(outs[key], axis=1)
    return (groups[0]["x"].reshape(bp, lp, D_MODEL), groups[1]["x"].reshape(bs, ls, D_MODEL),
            cat("gdn_f"), cat("gdn_b"), cat("att_k"), cat("att_v"), cat("rw_f"), cat("rw_b"))
```

```python
import functools

import jax
import jax.numpy as jnp
import numpy as np
from jax import lax
from jax.experimental import pallas as pl
from jax.experimental.pallas import tpu as pltpu

F32 = jnp.float32
BF16 = jnp.bfloat16

D_MODEL = 1024
N_MOD = 6
D_FF = 4 * D_MODEL
NORM_EPS = 1e-6
TOKEN_TILE = 512
HALO = 8
MOD_ROWS = 8

GDN_HEADS = 4
GDN_D = 128
GDN_CHUNK = 128
GDN_GROUP = 8
GDN_HEADS_PER_STEP = 2
GDN_BLOCK_ROWS = 1024
GDN_QKV_W = 3 * GDN_HEADS * GDN_D
SC_WIDTH = 512
MOD_COL_BLOCKS = 4
EV_CAST_ROWS = 720

ATT_HEADS = 8
ATT_KV_HEADS = 2
ATT_GROUP = ATT_HEADS // ATT_KV_HEADS
ATT_HD = 64
ATT_Q_W = ATT_HEADS * ATT_HD
ATT_KV_W = ATT_KV_HEADS * ATT_HD
ATT_W = ATT_Q_W + 2 * ATT_KV_W
WINDOW = 128
ATT_BLOCK = 128
GRID_W = 64
ROPE_BASE = 10000.0
NEG_INF = -1e30

RWKV_HEADS = 8
RWKV_HD = 64
RWKV_W = RWKV_HEADS * RWKV_HD
RWKV_LORA = 64
RWKV_IN = 3 * RWKV_W + 3 * 2 * RWKV_LORA
RWKV_CHUNK = 64
RWKV_GROUP = 4
RWKV_FINISH_ROWS = 256
DECAY_SCALE = float(np.exp(-0.5))
LOG2E = float(np.log2(np.e))
RWKV_BLOCK_ROWS = 1024
PAIR_W = 2 * RWKV_HD
N_PAIRS = RWKV_W // PAIR_W
GN_EPS = 64e-5

VMEM_LIMIT = 56 * 1024 * 1024

RWKV_PREC = dict(lora="bf", gate="bf", cumsum="x2r")


def _split_bf16(a):
    hi = a.astype(BF16)
    return hi, (a - hi.astype(F32)).astype(BF16)


def _dot(a, b, dims, prec):
    dn = (dims, ((), ()))
    one = lambda x, y: lax.dot_general(x, y, dn, preferred_element_type=F32)
    if prec == "x2r":
        ah = a.astype(BF16)
        bh, bl = _split_bf16(b)
        return one(ah, bh) + one(ah, bl)
    assert prec == "bf", prec
    return one(a.astype(BF16), b.astype(BF16))


def _mm(a, b, prec):
    return _dot(a, b, ((1,), (0,)), prec)


def _mm_nt(a, b, prec):
    return _dot(a, b, ((1,), (1,)), prec)


def _mm_tn(a, b, prec):
    return _dot(a, b, ((0,), (0,)), prec)


def _silu(x):
    return x * jax.nn.sigmoid(x)


def _softplus(x):
    return jnp.maximum(x, 0.0) + jnp.log1p(jnp.exp(-jnp.abs(x)))


def _rms(x, w):
    return x * lax.rsqrt(jnp.mean(x * x, axis=-1, keepdims=True) + NORM_EPS) * w


def _tri_masks(n, rev):
    r = lax.broadcasted_iota(jnp.int32, (n, n), 0)
    c = lax.broadcasted_iota(jnp.int32, (n, n), 1)
    if rev:
        return r <= c, r < c
    return r >= c, r > c


def _unit_tri_inv_many(mats, prec):
    assert prec == "bf"
    n = mats[0].shape[0]
    r = lax.broadcasted_iota(jnp.int32, (n, n), 0)
    c = lax.broadcasted_iota(jnp.int32, (n, n), 1)
    eye = jnp.where(r == c, 1.0, 0.0)
    within2 = jnp.where((r >> 1) == (c >> 1), 1.0, 0.0)
    ts = [eye - a * within2 for a in mats]
    abs_ = [a.astype(BF16) for a in mats]
    shift = 1
    while (1 << shift) < n:
        join = jnp.where(((r >> (shift + 1)) == (c >> (shift + 1))) & ((r >> shift) != (c >> shift)),
                         1.0, 0.0).astype(BF16)
        tbs = [t.astype(BF16) for t in ts]
        inner = [_mm(ab * join, tb, prec) for ab, tb in zip(abs_, tbs)]
        ts = [t - _mm(tb, w, prec) for t, tb, w in zip(ts, tbs, inner)]
        shift += 1
    return ts


def _shifted_rows(x, prev_row, next_row):
    n = x.shape[0]
    row = lax.broadcasted_iota(jnp.int32, x.shape, 0)
    xp = jnp.where(row == 0, prev_row, pltpu.roll(x, 1, 0))
    xn = jnp.where(row == n - 1, next_row, pltpu.roll(x, n - 1, 0))
    return xp, xn


def _mod_kernel(c_ref, w_ref, b_ref, evt_ref, o_ref, wt_ref):
    s = _silu(c_ref[...])
    o_ref[...] = _mm(s, w_ref[...], prec="bf") + b_ref[...]
    wt_ref[...] = evt_ref[...].astype(BF16)


def _modulation(c_rows, mod_w, mod_b, ev_w_in):
    depth = mod_w.shape[0]
    n_even, _, ev_cols = ev_w_in.shape
    nblk = MOD_COL_BLOCKS
    width = N_MOD * D_MODEL // nblk
    steps = depth * nblk
    rows = n_even * ev_cols
    n_side = rows // EV_CAST_ROWS
    assert rows % EV_CAST_ROWS == 0 and n_side <= steps and EV_CAST_ROWS % 16 == 0 and width % 128 == 0
    side = pl.BlockSpec((EV_CAST_ROWS, D_MODEL), lambda l, j: (jnp.minimum(l * nblk + j, n_side - 1), 0))
    out, w_t = pl.pallas_call(
        _mod_kernel,
        grid=(depth, nblk),
        in_specs=[
            pl.BlockSpec((MOD_ROWS, D_MODEL), lambda l, j: (0, 0)),
            pl.BlockSpec((None, D_MODEL, width), lambda l, j: (l, 0, j)),
            pl.BlockSpec((None, 1, width), lambda l, j: (l, 0, j)),
            side,
        ],
        out_specs=[pl.BlockSpec((None, MOD_ROWS, width), lambda l, j: (l, 0, j)), side],
        out_shape=[jax.ShapeDtypeStruct((depth, MOD_ROWS, N_MOD * D_MODEL), F32),
                   jax.ShapeDtypeStruct((rows, D_MODEL), BF16)],
        compiler_params=pltpu.CompilerParams(dimension_semantics=("arbitrary", "arbitrary"),
                                             vmem_limit_bytes=VMEM_LIMIT),
        name="modulation",
    )(c_rows, mod_w, mod_b.reshape(depth, 1, N_MOD * D_MODEL), jnp.swapaxes(ev_w_in, 1, 2).reshape(rows, D_MODEL))
    return out.reshape(depth, MOD_ROWS, N_MOD, D_MODEL), w_t.reshape(n_even, ev_cols, D_MODEL)


def _mod_spec(layer, tiles_per_seq, row_base, row_step):
    return pl.BlockSpec((None, None, N_MOD, D_MODEL),
                        lambda i: (layer, row_base + (i // tiles_per_seq) * row_step, 0, 0))


def _inproj_kernel(*refs, n_w):
    x_ref, mod_ref, nw_ref = refs[:3]
    w_refs = refs[3:3 + n_w]
    o_refs = refs[3 + n_w:]
    h = _rms(x_ref[...], nw_ref[...])
    h = (h * (1.0 + mod_ref[1:2, :]) + mod_ref[0:1, :]).astype(BF16)
    pieces = [_mm(h, w_ref[...], prec="bf") for w_ref in w_refs]
    y = pieces[0] if n_w == 1 else jnp.concatenate(pieces, axis=1)
    off = 0
    for o_ref in o_refs:
        n = o_ref.shape[-1]
        o_ref[...] = y[:, off:off + n]
        off += n


def _inproj(x, mods, norm_w, ws_bf16, splits, mod_spec):
    t = x.shape[0]
    assert sum(w.shape[1] for w in ws_bf16) == sum(splits) and all(w.shape[1] % 128 == 0 for w in ws_bf16)
    return pl.pallas_call(
        functools.partial(_inproj_kernel, n_w=len(ws_bf16)),
        grid=(t // TOKEN_TILE,),
        in_specs=[
            pl.BlockSpec((TOKEN_TILE, D_MODEL), lambda i: (i, 0)),
            mod_spec,
            pl.BlockSpec((1, D_MODEL), lambda i: (0, 0)),
        ] + [pl.BlockSpec(w.shape, lambda i: (0, 0)) for w in ws_bf16],
        out_specs=[pl.BlockSpec((TOKEN_TILE, n), lambda i: (i, 0)) for n in splits],
        out_shape=[jax.ShapeDtypeStruct((t, n), F32) for n in splits],
        compiler_params=pltpu.CompilerParams(dimension_semantics=("parallel",), vmem_limit_bytes=VMEM_LIMIT),
        name="inproj",
    )(x, mods, norm_w.reshape(1, D_MODEL), *ws_bf16)


def _inproj_even_kernel(xp_ref, x_ref, xn_ref, mod_ref, nw_ref, wt_ref, cqkv_ref, csc_ref,
                        alog_ref, dtb_ref, qkv_ref, gz_ref, sc_ref, gate_ref, *, seq_len):
    tile = x_ref.shape[0]
    proj = lambda row0, width: _mm_nt(h, wt_ref[row0:row0 + width, :], "bf")
    x = jnp.concatenate([xp_ref[...], x_ref[...], xn_ref[...]], axis=0)
    h = _rms(x, nw_ref[...])
    h = (h * (1.0 + mod_ref[1:2, :]) + mod_ref[0:1, :]).astype(BF16)
    n = tile + 2 * HALO
    first = pl.program_id(0) * tile - HALO
    pos = (first + lax.broadcasted_iota(jnp.int32, (n, 1), 0)) & (seq_len - 1)
    at_start = pos == 0
    at_end = pos == seq_len - 1

    def conv3(v, c_ref):
        vp = jnp.where(at_start, 0.0, pltpu.roll(v, 1, 0))
        vn = jnp.where(at_end, 0.0, pltpu.roll(v, n - 1, 0))
        return vp * c_ref[0:1, :] + v * c_ref[1:2, :] + vn * c_ref[2:3, :]

    keep = slice(HALO, HALO + tile)
    wide = 2 * GDN_D
    gate_row = GDN_QKV_W + GDN_HEADS * GDN_D
    sc_row = gate_row + 4 * GDN_HEADS
    for j in range(GDN_QKV_W // wide):
        cols = slice(j * wide, (j + 1) * wide)
        act = _silu(conv3(proj(j * wide, wide), cqkv_ref.at[:, cols]))[keep]
        for i in range(2):
            part = act[:, i * GDN_D:(i + 1) * GDN_D]
            if j < 2 * GDN_HEADS // 2:
                part = part * lax.rsqrt(jnp.sum(part * part, axis=-1, keepdims=True) + 1e-6)
                if j < GDN_HEADS // 2:
                    part = part * (GDN_D ** -0.5)
            qkv_ref[:, j * wide + i * GDN_D:j * wide + (i + 1) * GDN_D] = part.astype(BF16)
    for j in range(GDN_HEADS * GDN_D // wide):
        cols = slice(j * wide, (j + 1) * wide)
        gz_ref[:, cols] = _silu(proj(GDN_QKV_W + j * wide, wide)[keep]).astype(BF16)
    for j in range(SC_WIDTH // wide):
        sc_b, sc_c, sc_h = [proj(sc_row + i * SC_WIDTH + j * wide, wide) for i in range(3)]
        cols = slice(j * wide, (j + 1) * wide)
        sc_ref[:, cols] = (sc_b * conv3(sc_c * sc_h, csc_ref.at[:, cols]))[keep].astype(BF16)
    g = proj(gate_row, 128)[keep]
    lane = lax.broadcasted_iota(jnp.int32, g.shape, 1)
    gate_ref[...] = jnp.where(lane < 2 * GDN_HEADS, jax.nn.sigmoid(g),
                              (-LOG2E * jnp.exp(alog_ref[...])) * _softplus(g + dtb_ref[...]))


def _inproj_even(x, mods, norm_w, w_t, conv_w, sc_conv_w, alog_vec, dtb_vec, seq_len, mod_spec):
    t = x.shape[0]
    tile = TOKEN_TILE
    assert seq_len & (seq_len - 1) == 0 and t % tile == 0
    per = tile // HALO
    last = t // HALO - 1
    const = lambda i: (0, 0)
    tok = lambda n: pl.BlockSpec((tile, n), lambda i: (i, 0))
    return pl.pallas_call(
        functools.partial(_inproj_even_kernel, seq_len=seq_len),
        grid=(t // tile,),
        in_specs=[
            pl.BlockSpec((HALO, D_MODEL), lambda i: (jnp.maximum(i * per - 1, 0), 0)),
            pl.BlockSpec((tile, D_MODEL), lambda i: (i, 0)),
            pl.BlockSpec((HALO, D_MODEL), lambda i: (jnp.minimum((i + 1) * per, last), 0)),
            mod_spec,
            pl.BlockSpec((1, D_MODEL), const),
            pl.BlockSpec(w_t.shape, const),
            pl.BlockSpec(conv_w.shape, const), pl.BlockSpec(sc_conv_w.shape, const),
            pl.BlockSpec((1, 128), const), pl.BlockSpec((1, 128), const),
        ],
        out_specs=[tok(GDN_QKV_W), tok(GDN_HEADS * GDN_D), tok(SC_WIDTH), tok(128)],
        out_shape=[jax.ShapeDtypeStruct((t, GDN_QKV_W), BF16), jax.ShapeDtypeStruct((t, GDN_HEADS * GDN_D), BF16),
                   jax.ShapeDtypeStruct((t, SC_WIDTH), BF16), jax.ShapeDtypeStruct((t, 128), F32)],
        compiler_params=pltpu.CompilerParams(dimension_semantics=("parallel",), vmem_limit_bytes=VMEM_LIMIT),
        name="inproj_even",
    )(x, x, x, mods, norm_w.reshape(1, D_MODEL), w_t, conv_w, sc_conv_w, alog_vec, dtb_vec)


def _mlp_kernel(a_ref, b_ref, x_ref, mod_ref, nw_ref, woa_ref, wob_ref, w1_ref, w2_ref, nf_ref, o_ref, *, final):
    y = _mm(a_ref[...], woa_ref[...], prec="bf") + _mm(b_ref[...], wob_ref[...], prec="bf")
    x1 = x_ref[...] + mod_ref[2:3, :] * y
    h = _rms(x1, nw_ref[...])
    h = (h * (1.0 + mod_ref[4:5, :]) + mod_ref[3:4, :]).astype(BF16)
    acc = jnp.zeros(x1.shape, F32)
    for j in range(D_FF // D_MODEL):
        cols = slice(j * D_MODEL, (j + 1) * D_MODEL)
        u = jnp.maximum(_mm(h, w1_ref[:, cols], prec="bf"), 0.0)
        acc = acc + _mm(u * u, w2_ref[cols, :], prec="bf")
    x2 = x1 + mod_ref[5:6, :] * acc
    if final:
        x2 = _rms(x2, nf_ref[...])
    o_ref[...] = x2


def _outproj_mlp(a, b, x, mods, norm_w, w_out, w1, w2, norm_final, mod_spec, final):
    t = x.shape[0]
    half = a.shape[1]
    const = lambda i: (0, 0)
    return pl.pallas_call(
        functools.partial(_mlp_kernel, final=final),
        grid=(t // TOKEN_TILE,),
        in_specs=[
            pl.BlockSpec((TOKEN_TILE, half), lambda i: (i, 0)),
            pl.BlockSpec((TOKEN_TILE, half), lambda i: (i, 0)),
            pl.BlockSpec((TOKEN_TILE, D_MODEL), lambda i: (i, 0)),
            mod_spec,
            pl.BlockSpec((1, D_MODEL), const),
            pl.BlockSpec((half, D_MODEL), const),
            pl.BlockSpec((half, D_MODEL), lambda i: (1, 0)),
            pl.BlockSpec((D_MODEL, D_FF), const),
            pl.BlockSpec((D_FF, D_MODEL), const),
            pl.BlockSpec((1, D_MODEL), const),
        ],
        out_specs=pl.BlockSpec((TOKEN_TILE, D_MODEL), lambda i: (i, 0)),
        out_shape=jax.ShapeDtypeStruct((t, D_MODEL), F32),
        compiler_params=pltpu.CompilerParams(dimension_semantics=("parallel",), vmem_limit_bytes=VMEM_LIMIT),
        name="outproj_mlp",
    )(a, b, x, mods, norm_w.reshape(1, D_MODEL), w_out, w_out, w1, w2, norm_final.reshape(1, D_MODEL))


def _gdn_decay_terms(g, rev):
    c = g.shape[0]
    incl = _tri_masks(c, rev)[0]
    before_col = _tri_masks(c, not rev)[0]
    eye = lax.broadcasted_iota(jnp.int32, (c, c), 0) == lax.broadcasted_iota(jnp.int32, (c, c), 1)
    gc_row = jnp.sum(jnp.where(before_col, jnp.broadcast_to(g, (c, c)), 0.0), axis=0, keepdims=True)
    gc_col = jnp.sum(jnp.where(eye, jnp.broadcast_to(gc_row, (c, c)), 0.0), axis=1, keepdims=True)
    decay = jnp.where(incl, jnp.exp2(jnp.where(incl, gc_col - gc_row, 0.0)), 0.0)
    g_tot = jnp.sum(g, axis=0, keepdims=True)
    return decay, jnp.exp2(gc_col), jnp.exp2(g_tot - gc_col), jnp.exp2(g_tot)


def _cast_side_job(arrays_layers, n_steps, step_index):
    in_specs, out_specs, out_shapes = [], [], []
    for arr, layer in arrays_layers:
        _, r, c = arr.shape
        blk = r // n_steps
        assert r % n_steps == 0 and blk % 16 == 0
        in_specs.append(pl.BlockSpec((None, blk, c), lambda *g, layer=layer: (layer, step_index(*g), 0)))
        out_specs.append(pl.BlockSpec((blk, c), lambda *g: (step_index(*g), 0)))
        out_shapes.append(jax.ShapeDtypeStruct((r, c), BF16))
    return in_specs, out_specs, out_shapes


def _gdn_kernel(*refs, seq_len, seqs, has_init, n_cast):
    q_ref, k_ref, v_ref, gz_ref, gate_ref, gn_ref = refs[:6]
    pos = 6
    if has_init:
        s0f_ref, s0b_ref = refs[pos:pos + 2]
        pos += 2
    cast_in = refs[pos:pos + n_cast]
    pos += n_cast
    o_ref, sf_ref, sb_ref = refs[pos:pos + 3]
    cast_out = refs[pos + 3:pos + 3 + n_cast]
    pos += n_cast
    for src, dst in zip(cast_in, cast_out):
        dst[...] = src[...].astype(BF16)
    osum, u_s, w_s, qd_s, kd_s, in_s, ge_s, st = refs[pos + 3:]
    hps = GDN_HEADS_PER_STEP
    head0 = pl.program_id(1) * hps
    ch = GDN_CHUNK
    n_chunks = seq_len // ch
    osum[...] = jnp.zeros(osum.shape, F32)
    hcols = [slice(hh * GDN_D, (hh + 1) * GDN_D) for hh in range(hps)]

    lane = lax.broadcasted_iota(jnp.int32, (ch, 128), 1)

    def pick(rows, col):
        return jnp.sum(jnp.where(lane == col, gate_ref[rows, :], 0.0), axis=1, keepdims=True)

    def solve_group(gi, carry):
        items = []
        for j in range(GDN_GROUP):
            c = gi * GDN_GROUP + j
            rows = pl.ds(pl.multiple_of(c * ch, ch), ch)
            for hh in range(hps):
                items.append(dict(c=c, hh=hh, rows=rows, q=q_ref[rows, hcols[hh]].astype(F32),
                                  k=k_ref[rows, hcols[hh]].astype(F32), v=v_ref[rows, hcols[hh]].astype(F32)))
        kks = [_mm_nt(it["k"], it["k"], "bf") for it in items]
        qks = [_mm_nt(it["q"], it["k"], "bf") for it in items]
        subs = []
        for it, kk, qk in zip(items, kks, qks):
            for d in range(2):
                beta = pick(it["rows"], d * GDN_HEADS + head0 + it["hh"])
                g = pick(it["rows"], 2 * GDN_HEADS + d * GDN_HEADS + head0 + it["hh"])
                decay, e_gc, e_rest, e_tot = _gdn_decay_terms(g, rev=(d == 1))
                strict = _tri_masks(ch, d == 1)[1]
                subs.append(dict(
                    d=d, c=it["c"], hh=it["hh"], rows=it["rows"],
                    a=jnp.where(strict, kk * beta * decay, 0.0),
                    rhs=jnp.concatenate([it["v"] * beta, it["k"] * (beta * e_gc)], axis=1),
                    intra=qk * decay, qd=it["q"] * e_gc, kd=it["k"] * e_rest, ge=e_tot))
        ts = _unit_tri_inv_many([s["a"] for s in subs], "bf")
        uws = [_mm(t, s["rhs"], "bf") for t, s in zip(ts, subs)]
        for s, uw in zip(subs, uws):
            d, hh, rows = s["d"], s["hh"], s["rows"]
            u_s[d, hh, rows, :] = uw[:, :GDN_D]
            w_s[d, hh, rows, :] = uw[:, GDN_D:].astype(BF16)
            qd_s[d, hh, rows, :] = s["qd"].astype(BF16)
            kd_s[d, hh, rows, :] = s["kd"].astype(BF16)
            in_s[d, hh, rows, :] = s["intra"].astype(BF16)
            ge_s[d, hh, pl.ds(pl.multiple_of(s["c"] * 8, 8), 8), :] = jnp.broadcast_to(s["ge"], (8, 128))
        return carry

    lax.fori_loop(0, seqs * n_chunks // GDN_GROUP, solve_group, 0)

    for hh in range(hps):
        for j in range(seqs):
            st[0, hh, j] = s0f_ref[j, hh] if has_init else jnp.zeros((GDN_D, GDN_D), F32)
            st[1, hh, j] = s0b_ref[j, hh] if has_init else jnp.zeros((GDN_D, GDN_D), F32)

    def recur(i, carry):
        cs = []
        for hh in range(hps):
            for j in range(seqs):
                for d in range(2):
                    c = j * n_chunks + (i if d == 0 else n_chunks - 1 - i)
                    cs.append(dict(d=d, hh=hh, j=j, rows=pl.ds(pl.multiple_of(c * ch, ch), ch),
                                   ge=ge_s[d, hh, pl.ds(pl.multiple_of(c * 8, 8), 1), :], s=st[d, hh, j]))
        sbs = [c["s"].astype(BF16) for c in cs]
        wss = [_mm(w_s[c["d"], c["hh"], c["rows"], :], sb, "bf") for c, sb in zip(cs, sbs)]
        qss = [_mm(qd_s[c["d"], c["hh"], c["rows"], :], sb, "bf") for c, sb in zip(cs, sbs)]
        ebs = [(u_s[c["d"], c["hh"], c["rows"], :] - ws).astype(BF16) for c, ws in zip(cs, wss)]
        outs = [qs_ + _mm(in_s[c["d"], c["hh"], c["rows"], :], eb, "bf") for c, qs_, eb in zip(cs, qss, ebs)]
        s_news = [c["s"] * c["ge"] + _mm_tn(kd_s[c["d"], c["hh"], c["rows"], :], eb, "bf")
                  for c, eb in zip(cs, ebs)]
        for c, o, s_new in zip(cs, outs, s_news):
            st[c["d"], c["hh"], c["j"]] = s_new
            osum[c["rows"], hcols[c["hh"]]] += o
        return carry

    lax.fori_loop(0, n_chunks, recur, 0)
    for hh in range(hps):
        for j in range(seqs):
            sf_ref[j, hh] = st[0, hh, j]
            sb_ref[j, hh] = st[1, hh, j]
        o_ref[:, hcols[hh]] = (_rms(osum[:, hcols[hh]], gn_ref[...]) * gz_ref[:, hcols[hh]].astype(F32))


def _gdn_mixer(qkv, gz, gates, n_seq, seq_len, gdn_norm, s0_f, s0_b, cast=()):
    has_init = s0_f is not None
    hd = GDN_HEADS
    hps = GDN_HEADS_PER_STEP
    hgroups = hd // hps
    seqs = max(1, GDN_BLOCK_ROWS // seq_len)
    rows = seqs * seq_len
    assert n_seq % seqs == 0 and (rows // GDN_CHUNK) % GDN_GROUP == 0 and hd % hps == 0

    def col(block):
        return pl.BlockSpec((rows, hps * GDN_D), lambda s, h, b=block: (s, b * hgroups + h))

    state = pl.BlockSpec((seqs, None, hps, GDN_D, GDN_D), lambda s, h: (s, 0, h, 0, 0))
    in_specs = [col(0), col(1), col(2), col(0), pl.BlockSpec((rows, 128), lambda s, h: (s, 0)),
                pl.BlockSpec((1, 128), lambda s, h: (0, 0))]
    args = [qkv, qkv, qkv, gz, gates, gdn_norm.reshape(1, 128)]
    if has_init:
        in_specs += [state, state]
        args += [s0_f, s0_b]
    n_steps = (n_seq // seqs) * hgroups
    cast_in, cast_out, cast_shapes = _cast_side_job(cast, n_steps, lambda s, h: s * hgroups + h)
    in_specs += cast_in
    args += [arr for arr, _ in cast]
    t = n_seq * seq_len
    scratch = ([pltpu.VMEM((rows, hps * GDN_D), F32)]
               + [pltpu.VMEM((2, hps, rows, GDN_D), F32)]
               + [pltpu.VMEM((2, hps, rows, GDN_D), BF16) for _ in range(3)]
               + [pltpu.VMEM((2, hps, rows, GDN_CHUNK), BF16),
                  pltpu.VMEM((2, hps, rows // GDN_CHUNK * 8, 128), F32),
                  pltpu.VMEM((2, hps, seqs, GDN_D, GDN_D), F32)])
    return pl.pallas_call(
        functools.partial(_gdn_kernel, seq_len=seq_len, seqs=seqs, has_init=has_init, n_cast=len(cast)),
        grid=(n_seq // seqs, hgroups),
        in_specs=in_specs,
        out_specs=[pl.BlockSpec((rows, hps * GDN_D), lambda s, h: (s, h)), state, state] + cast_out,
        out_shape=[jax.ShapeDtypeStruct((t, hd * GDN_D), F32),
                   jax.ShapeDtypeStruct((n_seq, 1, hd, GDN_D, GDN_D), F32),
                   jax.ShapeDtypeStruct((n_seq, 1, hd, GDN_D, GDN_D), F32)] + cast_shapes,
        scratch_shapes=scratch,
        compiler_params=pltpu.CompilerParams(dimension_semantics=("parallel", "parallel"),
                                             vmem_limit_bytes=VMEM_LIMIT),
        name="gdn_mixer",
    )(*args)


def _softmax_pv(scores, values, sink):
    m = sink
    for s in scores:
        m = jnp.maximum(m, jnp.max(s, axis=-1, keepdims=True))
    den = jnp.exp2(sink - m)
    acc = None
    for s, v in zip(scores, values):
        e = jnp.exp2(s - m)
        den = den + jnp.sum(e, axis=-1, keepdims=True)
        pv = _mm(e, v, prec="bf")
        acc = pv if acc is None else acc + pv
    return acc / den


def _group_sinks(sink_ref, j, rows):
    assert rows & (rows - 1) == 0
    grp = lax.broadcasted_iota(jnp.int32, (ATT_GROUP * rows, 1), 0) >> (rows.bit_length() - 1)
    col = jnp.full(grp.shape, sink_ref[j * ATT_GROUP], F32)
    for gi in range(1, ATT_GROUP):
        col = jnp.where(grp == gi, sink_ref[j * ATT_GROUP + gi], col)
    return col * LOG2E


def _store_group(o_ref, j, o, rows):
    for gi in range(ATT_GROUP):
        hh = j * ATT_GROUP + gi
        o_ref[:, hh * ATT_HD:(hh + 1) * ATT_HD] = o[gi * rows:(gi + 1) * rows]


def _attn_ctx_kernel(*refs, n_cast):
    sink_ref, p_ref = refs[:2]
    cast_in = refs[2:2 + n_cast]
    o_ref, kc_ref, vc_ref = refs[2 + n_cast:5 + n_cast]
    for src, dst in zip(cast_in, refs[5 + n_cast:]):
        dst[...] = src[...].astype(BF16)
    scale = ATT_HD ** -0.5 * LOG2E
    rows = p_ref.shape[0]
    for j in range(ATT_KV_HEADS):
        k = p_ref[:, ATT_Q_W + j * ATT_HD:ATT_Q_W + (j + 1) * ATT_HD]
        v = p_ref[:, ATT_Q_W + ATT_KV_W + j * ATT_HD:ATT_Q_W + ATT_KV_W + (j + 1) * ATT_HD]
        kc_ref[j] = k
        vc_ref[j] = v
        q = jnp.concatenate([p_ref[:, hh * ATT_HD:(hh + 1) * ATT_HD]
                             for hh in range(j * ATT_GROUP, (j + 1) * ATT_GROUP)], axis=0) * scale
        s = _mm_nt(q, k, prec="bf")
        _store_group(o_ref, j, _softmax_pv([s], [v], _group_sinks(sink_ref, j, rows)), rows)


def _attn_context(proj_att, sink, n_seq, seq_len, cast=()):
    kv = pl.BlockSpec((None, None, ATT_KV_HEADS, seq_len, ATT_HD), lambda b: (b, 0, 0, 0, 0))
    cast_in, cast_out, cast_shapes = _cast_side_job(cast, n_seq, lambda b: b)
    return pl.pallas_call(
        functools.partial(_attn_ctx_kernel, n_cast=len(cast)),
        grid=(n_seq,),
        in_specs=[pl.BlockSpec(memory_space=pltpu.SMEM),
                  pl.BlockSpec((seq_len, ATT_W), lambda b: (b, 0))] + cast_in,
        out_specs=[pl.BlockSpec((seq_len, ATT_Q_W), lambda b: (b, 0)), kv, kv] + cast_out,
        out_shape=[jax.ShapeDtypeStruct((n_seq * seq_len, ATT_Q_W), F32),
                   jax.ShapeDtypeStruct((n_seq, 1, ATT_KV_HEADS, seq_len, ATT_HD), F32),
                   jax.ShapeDtypeStruct((n_seq, 1, ATT_KV_HEADS, seq_len, ATT_HD), F32)] + cast_shapes,
        compiler_params=pltpu.CompilerParams(dimension_semantics=("parallel",), vmem_limit_bytes=VMEM_LIMIT),
        name="attn_context",
    )(sink, proj_att, *[arr for arr, _ in cast])


def _rope_tables(seq_len):
    pos = np.arange(seq_len)
    half = ATT_HD // 2
    inv = ROPE_BASE ** (-np.arange(0, half, 2, dtype=np.float32) / half)
    ang_r = (pos // GRID_W).astype(np.float32)[:, None] * inv
    ang_c = (pos % GRID_W).astype(np.float32)[:, None] * inv
    cos = np.concatenate([np.cos(ang_r), np.cos(ang_r), np.cos(ang_c), np.cos(ang_c)], axis=1)
    sin = np.concatenate([-np.sin(ang_r), np.sin(ang_r), -np.sin(ang_c), np.sin(ang_c)], axis=1)
    return (jnp.asarray(np.tile(cos, (1, 2)), F32), jnp.asarray(np.tile(sin, (1, 2)), F32))


def _rope(x, cos, sin):
    lane = lax.broadcasted_iota(jnp.int32, x.shape, 1)
    partner = jnp.where((lane & 31) < 16, pltpu.roll(x, 128 - 16, 1), pltpu.roll(x, 16, 1))
    return x * cos + partner * sin


def _attn_lat_kernel(sink_ref, p_ref, ck_ref, cv_ref, cos_ref, sin_ref, o_ref, *, seq_len):
    scale = ATT_HD ** -0.5 * LOG2E
    qb = pl.program_id(1)
    span = 3 * ATT_BLOCK
    q0 = pl.multiple_of(qb * ATT_BLOCK, ATT_BLOCK)
    k0 = pl.multiple_of(jnp.clip((qb - 1) * ATT_BLOCK, 0, seq_len - span), ATT_BLOCK)
    qrows = pl.ds(q0, ATT_BLOCK)
    krows = pl.ds(k0, span)
    kwin = _rope(p_ref[krows, ATT_Q_W:ATT_Q_W + ATT_KV_W], cos_ref[krows, :], sin_ref[krows, :])
    vwin = p_ref[krows, ATT_Q_W + ATT_KV_W:ATT_W]
    stacked = ATT_GROUP * ATT_BLOCK
    qpos = q0 + (lax.broadcasted_iota(jnp.int32, (stacked, span), 0) & (ATT_BLOCK - 1))
    kpos = k0 + lax.broadcasted_iota(jnp.int32, (stacked, span), 1)
    valid = jnp.abs(qpos - kpos) <= WINDOW
    cos_q = cos_ref[qrows, :]
    sin_q = sin_ref[qrows, :]
    heads = []
    for pair in range(ATT_HEADS // 2):
        qpair = _rope(p_ref[qrows, pair * 128:(pair + 1) * 128], cos_q, sin_q) * scale
        heads += [qpair[:, :ATT_HD], qpair[:, ATT_HD:]]
    kv = range(ATT_KV_HEADS)
    qs = [jnp.concatenate(heads[j * ATT_GROUP:(j + 1) * ATT_GROUP], axis=0) for j in kv]
    s_locs = [jnp.where(valid, _mm_nt(qs[j], kwin[:, j * ATT_HD:(j + 1) * ATT_HD], prec="bf"), NEG_INF)
              for j in kv]
    s_ctxs = [_mm_nt(qs[j], ck_ref[j], prec="bf") for j in kv]
    outs = [_softmax_pv([s_locs[j], s_ctxs[j]], [vwin[:, j * ATT_HD:(j + 1) * ATT_HD], cv_ref[j]],
                        _group_sinks(sink_ref, j, ATT_BLOCK)) for j in kv]
    for j in kv:
        _store_group(o_ref, j, outs[j], ATT_BLOCK)


def _attn_latent(proj_att, sink, cache_k, cache_v, layer, n_seq, seq_len):
    cos, sin = _rope_tables(seq_len)
    past = cache_k.shape[3]
    nqb = seq_len // ATT_BLOCK
    cache = pl.BlockSpec((None, None, ATT_KV_HEADS, past, ATT_HD), lambda b, q: (b, layer, 0, 0, 0))
    table = pl.BlockSpec((seq_len, 128), lambda b, q: (0, 0))
    return pl.pallas_call(
        functools.partial(_attn_lat_kernel, seq_len=seq_len),
        grid=(n_seq, nqb),
        in_specs=[pl.BlockSpec(memory_space=pltpu.SMEM),
                  pl.BlockSpec((seq_len, ATT_W), lambda b, q: (b, 0)),
                  cache, cache, table, table],
        out_specs=pl.BlockSpec((ATT_BLOCK, ATT_Q_W), lambda b, q: (b * nqb + q, 0)),
        out_shape=jax.ShapeDtypeStruct((n_seq * seq_len, ATT_Q_W), F32),
        compiler_params=pltpu.CompilerParams(dimension_semantics=("parallel", "parallel")),
        name="attn_latent",
    )(sink, proj_att, cache_k, cache_v, cos, sin)


def _pair_masks(n, rev):
    r = lax.broadcasted_iota(jnp.int32, (n, 2 * n), 0)
    c = lax.broadcasted_iota(jnp.int32, (n, 2 * n), 1) & (n - 1)
    return (r <= c, r < c) if rev else (r >= c, r > c)


def _bd(x):
    half = x.shape[1] // 2
    lane = lax.broadcasted_iota(jnp.int32, x.shape, 1)
    zero = jnp.zeros_like(x)
    return jnp.concatenate([jnp.where(lane < half, x, zero), jnp.where(lane >= half, x, zero)], axis=0)


def _bd_mask(n):
    r = lax.broadcasted_iota(jnp.int32, (2 * n, 2 * n), 0)
    c = lax.broadcasted_iota(jnp.int32, (2 * n, 2 * n), 1)
    return (r < n) == (c < n)


def _head_sums(x):
    ones = jnp.where(_bd_mask(RWKV_HD), 1.0, 0.0).astype(BF16)
    return _mm(x, ones, "bf")


def _apply_pairs(tb, x):
    hi, lo = _split_bf16(x)
    return _mm(tb, _bd(hi), "bf") + _mm(tb, _bd(lo), "bf")


def _unit_tri_inv_pairs(mats):
    n = mats[0].shape[0]
    r = lax.broadcasted_iota(jnp.int32, (n, 2 * n), 0)
    c = lax.broadcasted_iota(jnp.int32, (n, 2 * n), 1) & (n - 1)
    eye = jnp.where(r == c, 1.0, 0.0)
    within2 = jnp.where((r >> 1) == (c >> 1), 1.0, 0.0)
    ts = [eye - a * within2 for a in mats]
    abs_ = [a.astype(BF16) for a in mats]
    shift = 1
    while (1 << shift) < n:
        join = jnp.where(((r >> (shift + 1)) == (c >> (shift + 1))) & ((r >> shift) != (c >> shift)),
                         1.0, 0.0).astype(BF16)
        tbs = [t.astype(BF16) for t in ts]
        inner = [_mm(ab * join, _bd(tb), "bf") for ab, tb in zip(abs_, tbs)]
        ts = [t - _mm(tb, _bd(w.astype(BF16)), "bf") for t, tb, w in zip(ts, tbs, inner)]
        shift += 1
    return ts


def _rwkv_chunk_operands(x_ref, prm, sc, c, *, seq_len):
    (mu_ref, w0_ref, wup_ref, a0_ref, aup_ref, gup_ref, kk_ref, ka_ref, rk_ref) = prm
    ch = RWKV_CHUNK
    total = x_ref.shape[0]
    r0 = pl.multiple_of(c * ch, ch)
    rows = pl.ds(r0, ch)
    pos0 = r0 & (seq_len - 1)
    x = x_ref[rows, :]
    prev_row = x_ref[pl.ds(jnp.maximum(r0 - 1, 0), 1), :] * jnp.where(pos0 > 0, 1.0, 0.0)
    next_row = x_ref[pl.ds(jnp.minimum(r0 + ch, total - 1), 1), :] * jnp.where(pos0 + ch < seq_len, 1.0, 0.0)
    xp, xn = _shifted_rows(x, prev_row, next_row)
    mu0, mu1 = mu_ref[0:1, :], mu_ref[1:2, :]
    xs = x * (1.0 - mu0 - mu1) + xp * mu0 + xn * mu1
    r = xs[:, 0:RWKV_W]
    k = xs[:, RWKV_W:2 * RWKV_W]
    v = xs[:, 2 * RWKV_W:3 * RWKV_W]
    lo = 3 * RWKV_W
    p = RWKV_PREC
    gl = xs[:, lo + 4 * RWKV_LORA:lo + 6 * RWKV_LORA]
    sc["gate"][rows, :] = _mm(jax.nn.sigmoid(gl), gup_ref[...], p["gate"])
    sc["v"][rows, :] = v.astype(BF16)
    pairs = [slice(i * PAIR_W, (i + 1) * PAIR_W) for i in range(N_PAIRS)]
    kkv = k * kk_ref[...]
    kaps = []
    for cols in pairs:
        kk_p = kkv[:, cols]
        kaps.append(kk_p * lax.rsqrt(_head_sums(kk_p * kk_p) + 1e-6))
    items = []
    bonus = None
    for d in range(2):
        rev = d == 1
        wl = xs[:, lo + d * RWKV_LORA:lo + (d + 1) * RWKV_LORA]
        al = xs[:, lo + 2 * RWKV_LORA + d * RWKV_LORA:lo + 2 * RWKV_LORA + (d + 1) * RWKV_LORA]
        lw = (-DECAY_SCALE * LOG2E) * jax.nn.sigmoid(w0_ref[d:d + 1, :] + _mm(jnp.tanh(wl), wup_ref[d], p["lora"]))
        a = jax.nn.sigmoid(a0_ref[d:d + 1, :] + _mm(al, aup_ref[d], p["lora"]))
        k2 = k * (1.0 + (a - 1.0) * ka_ref[...])
        g_inc = _mm(jnp.where(_tri_masks(ch, rev)[0], 1.0, 0.0), lw, p["cumsum"])
        g_tot = jnp.sum(lw, axis=0, keepdims=True)
        e_neg = jnp.exp2(-g_inc)
        e_end = jnp.exp2(g_tot - g_inc)
        e_exc = jnp.exp2(g_inc - lw)
        r_dec = r * jnp.exp2(g_inc)
        k_neg = k2 * e_neg
        k_end = k2 * e_end
        sc["dec"][d, pl.ds(pl.multiple_of(c * 8, 8), 8), :] = jnp.broadcast_to(jnp.exp2(g_tot), (8, RWKV_W))
        rkr = r * k2 * rk_ref[...]
        bon_d = jnp.concatenate([_head_sums(rkr[:, cols]) for cols in pairs], axis=1) * v
        bonus = bon_d if bonus is None else bonus + bon_d
        for cols, kap in zip(pairs, kaps):
            b_p = kap * a[:, cols]
            items.append(dict(d=d, rows=rows, cols=cols, kap_dec=kap * e_exc[:, cols], r_dec=r_dec[:, cols],
                              b_neg=b_p * e_neg[:, cols], k_neg=k_neg[:, cols],
                              b_end=b_p * e_end[:, cols], k_end=k_end[:, cols], vb=v[:, cols].astype(BF16)))
    sc["bon"][rows, :] = bonus
    return items


def _rwkv_solve(items, sc):
    ch = RWKV_CHUNK
    masks = [_pair_masks(ch, False), _pair_masks(ch, True)]
    ms = [_mm_nt(jnp.concatenate([it["kap_dec"], it["r_dec"]], axis=0),
                 jnp.concatenate([_bd(it["b_neg"].astype(BF16)), _bd(it["k_neg"].astype(BF16))], axis=0), "bf")
          for it in items]
    a_abs = [jnp.where(masks[it["d"]][1], m[:ch, :2 * ch], 0.0) for it, m in zip(items, ms)]
    a_aks = [jnp.where(masks[it["d"]][1], m[:ch, 2 * ch:], 0.0) for it, m in zip(items, ms)]
    a_rbs = [jnp.where(masks[it["d"]][0], m[ch:, :2 * ch], 0.0) for it, m in zip(items, ms)]
    a_rks = [jnp.where(masks[it["d"]][0], m[ch:, 2 * ch:], 0.0) for it, m in zip(items, ms)]
    tbs = [t.astype(BF16) for t in _unit_tri_inv_pairs(a_abs)]
    akvs = [_mm(a_ak, _bd(it["vb"]), "bf") for it, a_ak in zip(items, a_aks)]
    w2s = [_apply_pairs(tb, it["kap_dec"]) for it, tb in zip(items, tbs)]
    u0s = [_apply_pairs(tb, akv) for tb, akv in zip(tbs, akvs)]
    for it, w2, u0, a_rb, a_rk in zip(items, w2s, u0s, a_rbs, a_rks):
        d, rows, cols = it["d"], it["rows"], it["cols"]
        sc["w2"][d, rows, cols] = w2.astype(BF16)
        sc["rd"][d, rows, cols] = it["r_dec"].astype(BF16)
        sc["u0"][d, rows, cols] = u0
        sc["arb"][d, rows, cols] = a_rb.astype(BF16)
        sc["ark"][d, rows, cols] = a_rk.astype(BF16)
        sc["bh"][d, rows, cols] = it["b_end"].astype(BF16)
        sc["kh"][d, rows, cols] = it["k_end"].astype(BF16)


def _rwkv_recur(sc, st, i, *, seq_len, seqs):
    ch = RWKV_CHUNK
    n_chunks = seq_len // ch
    keep = _bd_mask(RWKV_HD)
    cs = []
    for j in range(seqs):
        for d in range(2):
            c = j * n_chunks + (i if d == 0 else n_chunks - 1 - i)
            rows = pl.ds(pl.multiple_of(c * ch, ch), ch)
            dec = sc["dec"][d, pl.ds(pl.multiple_of(c * 8, 8), 1), :]
            for pi in range(N_PAIRS):
                cols = slice(pi * PAIR_W, (pi + 1) * PAIR_W)
                cs.append(dict(d=d, j=j, p=pi, rows=rows, cols=cols, dec=dec[:, cols], s=st[d, j, pi]))
    sbs = [c["s"].astype(BF16) for c in cs]
    lss = [_mm_nt(jnp.concatenate([sc["w2"][c["d"], c["rows"], c["cols"]],
                                   sc["rd"][c["d"], c["rows"], c["cols"]]], axis=0), sb, "bf")
           for c, sb in zip(cs, sbs)]
    ubs = [(-(ls[:ch] + sc["u0"][c["d"], c["rows"], c["cols"]])).astype(BF16) for c, ls in zip(cs, lss)]
    vbs = [sc["v"][c["rows"], c["cols"]] for c in cs]
    ys = [ls[ch:] + _mm(jnp.concatenate([sc["arb"][c["d"], c["rows"], c["cols"]],
                                         sc["ark"][c["d"], c["rows"], c["cols"]]], axis=1),
                        jnp.concatenate([_bd(ub), _bd(vb)], axis=0), "bf")
          for c, ls, ub, vb in zip(cs, lss, ubs, vbs)]
    ups = [_mm_tn(jnp.concatenate([ub, vb], axis=0),
                  jnp.concatenate([sc["bh"][c["d"], c["rows"], c["cols"]],
                                   sc["kh"][c["d"], c["rows"], c["cols"]]], axis=0), "bf")
           for c, ub, vb in zip(cs, ubs, vbs)]
    for c, y, up in zip(cs, ys, ups):
        st[c["d"], c["j"], c["p"]] = c["s"] * c["dec"] + jnp.where(keep, up, 0.0)
        sc["ysum"][c["rows"], c["cols"]] += y


RWKV_SCRATCH = ("ysum", "bon", "gate", "v", "w2", "rd", "u0", "arb", "ark", "bh", "kh", "dec")


def _rwkv_kernel(*refs, seq_len, seqs, has_init):
    x_ref = refs[0]
    prm = refs[1:10]
    lnw_ref, lnb_ref = refs[10:12]
    pos = 12
    if has_init:
        s0f_ref, s0b_ref = refs[pos:pos + 2]
        pos += 2
    o_ref, sf_ref, sb_ref = refs[pos:pos + 3]
    sc = dict(zip(RWKV_SCRATCH, refs[pos + 3:]))
    st = refs[pos + 3 + len(RWKV_SCRATCH)]
    ch = RWKV_CHUNK
    hd = RWKV_HD
    n_chunks = seq_len // ch
    sc["ysum"][...] = jnp.zeros(sc["ysum"].shape, F32)

    def prepare(gi, carry):
        items = []
        for j in range(RWKV_GROUP):
            items += _rwkv_chunk_operands(x_ref, prm, sc, gi * RWKV_GROUP + j, seq_len=seq_len)
        _rwkv_solve(items, sc)
        return carry

    lax.fori_loop(0, seqs * n_chunks // RWKV_GROUP, prepare, 0)

    zero = jnp.zeros((hd, hd), F32)
    for d, s0_ref in enumerate((s0f_ref, s0b_ref) if has_init else (None, None)):
        for j in range(seqs):
            for pi in range(N_PAIRS):
                s_a = s0_ref[j, 2 * pi] if has_init else zero
                s_b = s0_ref[j, 2 * pi + 1] if has_init else zero
                st[d, j, pi] = jnp.concatenate([jnp.concatenate([s_a, zero], axis=1),
                                                jnp.concatenate([zero, s_b], axis=1)], axis=0)

    def recur(i, carry):
        _rwkv_recur(sc, st, i, seq_len=seq_len, seqs=seqs)
        return carry

    lax.fori_loop(0, n_chunks, recur, 0)
    for d, out_ref in enumerate((sf_ref, sb_ref)):
        for j in range(seqs):
            for pi in range(N_PAIRS):
                s = st[d, j, pi]
                out_ref[j, 2 * pi] = s[:hd, :hd]
                out_ref[j, 2 * pi + 1] = s[hd:, hd:]

    tile = RWKV_FINISH_ROWS

    def finish(i, carry):
        rows = pl.ds(pl.multiple_of(i * tile, tile), tile)
        pairs = [slice(pi * PAIR_W, (pi + 1) * PAIR_W) for pi in range(N_PAIRS)]
        ys = [sc["ysum"][rows, cols] for cols in pairs]
        cens = [y - _head_sums(y) * (1.0 / hd) for y in ys]
        vars_ = [_head_sums(cen * cen) * (1.0 / hd) for cen in cens]
        for cols, cen, var in zip(pairs, cens, vars_):
            yn = cen * lax.rsqrt(var + GN_EPS) * lnw_ref[:, cols] + lnb_ref[:, cols]
            o_ref[rows, cols] = (yn + sc["bon"][rows, cols]) * sc["gate"][rows, cols]
        return carry

    lax.fori_loop(0, seqs * seq_len // tile, finish, 0)


def _rwkv_mixer(x_rw, n_seq, seq_len, params, s0_f, s0_b):
    has_init = s0_f is not None
    (mu, w0, w_up, a0, a_up, g_up, k_k, k_a, r_k, ln_w, ln_b) = params
    row = lambda a: a.reshape(1, RWKV_W)
    args = [x_rw, mu, w0, w_up, a0, a_up, g_up, row(k_k), row(k_a), row(r_k), row(ln_w), row(ln_b)]

    def whole(a):
        nd = a.ndim
        return pl.BlockSpec(a.shape, lambda s, nd=nd: (0,) * nd)

    seqs = max(1, RWKV_BLOCK_ROWS // seq_len)
    rows = seqs * seq_len
    assert n_seq % seqs == 0 and (rows // RWKV_CHUNK) % RWKV_GROUP == 0 and seq_len & (seq_len - 1) == 0
    in_specs = [pl.BlockSpec((rows, RWKV_IN), lambda s: (s, 0))] + [whole(a) for a in args[1:]]
    state = pl.BlockSpec((seqs, None, RWKV_HEADS, RWKV_HD, RWKV_HD), lambda s: (s, 0, 0, 0, 0))
    if has_init:
        in_specs += [state, state]
        args += [s0_f, s0_b]
    st_shape = jax.ShapeDtypeStruct((n_seq, 1, RWKV_HEADS, RWKV_HD, RWKV_HD), F32)
    tok = lambda dt: pltpu.VMEM((rows, RWKV_W), dt)
    per_dir = lambda dt: pltpu.VMEM((2, rows, RWKV_W), dt)
    scratch = dict(ysum=tok(F32), bon=tok(F32), gate=tok(F32), v=tok(BF16), w2=per_dir(BF16), rd=per_dir(BF16),
                   u0=per_dir(F32), arb=per_dir(BF16), ark=per_dir(BF16), bh=per_dir(BF16), kh=per_dir(BF16),
                   dec=pltpu.VMEM((2, rows // RWKV_CHUNK * 8, RWKV_W), F32))
    return pl.pallas_call(
        functools.partial(_rwkv_kernel, seq_len=seq_len, seqs=seqs, has_init=has_init),
        grid=(n_seq // seqs,),
        in_specs=in_specs,
        out_specs=[pl.BlockSpec((rows, RWKV_W), lambda s: (s, 0)), state, state],
        out_shape=[jax.ShapeDtypeStruct((n_seq * seq_len, RWKV_W), F32), st_shape, st_shape],
        scratch_shapes=[scratch[name] for name in RWKV_SCRATCH]
        + [pltpu.VMEM((2, seqs, N_PAIRS, PAIR_W, PAIR_W), F32)],
        compiler_params=pltpu.CompilerParams(dimension_semantics=("parallel",), vmem_limit_bytes=VMEM_LIMIT),
        name="rwkv_mixer",
    )(*args)


def kernel(x_prompt, x_sample, state_gdn_fwd, state_gdn_bwd, cache_attn_k, cache_attn_v, state_rwkv_fwd, state_rwkv_bwd, c, c_ctx, mod_w, mod_b, norm_mix, norm_mlp, mlp_w1, mlp_w2, norm_final, ev_w_in, ev_w_out, gdn_conv, gdn_a_log, gdn_dt_bias, gdn_norm, sc_conv, od_w_in, od_w_out, attn_sink, rwkv_mu, rwkv_w0, rwkv_w_up, rwkv_a0, rwkv_a_up, rwkv_g_up, rwkv_k_k, rwkv_k_a, rwkv_r_k, rwkv_ln_w, rwkv_ln_b):
    bp, lp, _ = x_prompt.shape
    bs, ls, _ = x_sample.shape
    depth = mod_w.shape[0]
    c_rows = jnp.concatenate([c_ctx[None, :], c, jnp.zeros((MOD_ROWS - 1 - bs, D_MODEL), F32)], axis=0)
    mods, ev_w_t = _modulation(c_rows, mod_w, mod_b, ev_w_in)

    assert ls % TOKEN_TILE == 0 and (bp * lp) % TOKEN_TILE == 0
    groups = [
        dict(x=x_prompt.reshape(bp * lp, D_MODEL), n=bp, l=lp, latent=False,
             mod=lambda layer: _mod_spec(layer, 1, 0, 0)),
        dict(x=x_sample.reshape(bs * ls, D_MODEL), n=bs, l=ls, latent=True,
             mod=lambda layer: _mod_spec(layer, ls // TOKEN_TILE, 1, 1)),
    ]
    outs = {}
    for layer in range(depth):
        final = layer == depth - 1
        mlp_cast = ((mlp_w1, layer), (mlp_w2, layer))
        if layer % 2 == 0:
            e = layer // 2
            alog_vec = jnp.zeros((1, 128), F32).at[0, 2 * GDN_HEADS:4 * GDN_HEADS].set(gdn_a_log[e].reshape(-1))
            dtb_vec = jnp.zeros((1, 128), F32).at[0, 2 * GDN_HEADS:4 * GDN_HEADS].set(gdn_dt_bias[e].reshape(-1))
            for grp in groups:
                qkv, gz, sc, gates = _inproj_even(grp["x"], mods, norm_mix[layer], ev_w_t[e], gdn_conv[e],
                                                  sc_conv[e], alog_vec, dtb_vec, grp["l"], grp["mod"](layer))
                s0 = (state_gdn_fwd[:, e:e + 1], state_gdn_bwd[:, e:e + 1]) if grp["latent"] else (None, None)
                if grp["latent"]:
                    cast = ((od_w_in, layer // 2), (od_w_out, layer // 2)) if layer + 1 < depth else ()
                    o, _, _, *next_w = _gdn_mixer(qkv, gz, gates, grp["n"], grp["l"], gdn_norm[e], *s0, cast=cast)
                else:
                    o, s_f, s_b, w_out, w1, w2 = _gdn_mixer(qkv, gz, gates, grp["n"], grp["l"], gdn_norm[e], *s0,
                                                            cast=((ev_w_out, e),) + mlp_cast)
                    outs.setdefault("gdn_f", []).append(s_f)
                    outs.setdefault("gdn_b", []).append(s_b)
                grp["x"] = _outproj_mlp(o, sc, grp["x"], mods, norm_mlp[layer], w_out, w1, w2, norm_final,
                                        grp["mod"](layer), final)
        else:
            o_ = layer // 2
            w_in, w_out = [next_w[0]], next_w[1]
            rw = (rwkv_mu[o_], rwkv_w0[o_], rwkv_w_up[o_], rwkv_a0[o_], rwkv_a_up[o_], rwkv_g_up[o_],
                  rwkv_k_k[o_], rwkv_k_a[o_], rwkv_r_k[o_].reshape(-1), rwkv_ln_w[o_], rwkv_ln_b[o_])
            for grp in groups:
                p_att, x_rw = _inproj(grp["x"], mods, norm_mix[layer], w_in, (ATT_W, RWKV_IN), grp["mod"](layer))
                if grp["latent"]:
                    att = _attn_latent(p_att, attn_sink[o_], cache_attn_k, cache_attn_v, o_, grp["n"], grp["l"])
                    rwo, _, _ = _rwkv_mixer(x_rw, grp["n"], grp["l"], rw,
                                            state_rwkv_fwd[:, o_:o_ + 1], state_rwkv_bwd[:, o_:o_ + 1])
                else:
                    att, kc, vc, w1, w2 = _attn_context(p_att, attn_sink[o_], grp["n"], grp["l"], cast=mlp_cast)
                    rwo, s_f, s_b = _rwkv_mixer(x_rw, grp["n"], grp["l"], rw, None, None)
                    outs.setdefault("att_k", []).append(kc)
                    outs.setdefault("att_v", []).append(vc)
                    outs.setdefault("rw_f", []).append(s_f)
                    outs.setdefault("rw_b", []).append(s_b)
                grp["x"] = _outproj_mlp(att, rwo, grp["x"], mods, norm_mlp[layer], w_out, w1, w2, norm_final,
                                        grp["mod"](layer), final)
    cat = lambda key: jnp.concatenate(outs[key], axis=1)
    return (groups[0]["x"].reshape(bp, lp, D_MODEL), groups[1]["x"].reshape(bs, ls, D_MODEL),
            cat("gdn_f"), cat("gdn_b"), cat("att_k"), cat("att_v"), cat("rw_f"), cat("rw_b"))
```

```python
import functools

import jax
import jax.numpy as jnp
import numpy as np
from jax import lax
from jax.experimental import pallas as pl
from jax.experimental.pallas import tpu as pltpu

F32 = jnp.float32
BF16 = jnp.bfloat16

D_MODEL = 1024
N_MOD = 6
D_FF = 4 * D_MODEL
NORM_EPS = 1e-6
TOKEN_TILE = 512
HALO = 8
MOD_ROWS = 8

GDN_HEADS = 4
GDN_D = 128
GDN_CHUNK = 128
GDN_GROUP = 8
GDN_HEADS_PER_STEP = 2
GDN_BLOCK_ROWS = 1024
GDN_QKV_W = 3 * GDN_HEADS * GDN_D
SC_WIDTH = 512
MOD_COL_BLOCKS = 4
EV_CAST_ROWS = 720

ATT_HEADS = 8
ATT_KV_HEADS = 2
ATT_GROUP = ATT_HEADS // ATT_KV_HEADS
ATT_HD = 64
ATT_Q_W = ATT_HEADS * ATT_HD
ATT_KV_W = ATT_KV_HEADS * ATT_HD
ATT_W = ATT_Q_W + 2 * ATT_KV_W
ATT_CTX_SEQS = 2
WINDOW = 128
ATT_BLOCK = 128
GRID_W = 64
ROPE_BASE = 10000.0
NEG_INF = -1e30

RWKV_HEADS = 8
RWKV_HD = 64
RWKV_W = RWKV_HEADS * RWKV_HD
RWKV_LORA = 64
RWKV_IN = 3 * RWKV_W + 3 * 2 * RWKV_LORA
RWKV_CHUNK = 64
RWKV_GROUP = 4
RWKV_FINISH_ROWS = 256
DECAY_SCALE = float(np.exp(-0.5))
LOG2E = float(np.log2(np.e))
RWKV_BLOCK_ROWS = 1024
PAIR_W = 2 * RWKV_HD
N_PAIRS = RWKV_W // PAIR_W
GN_EPS = 64e-5

VMEM_LIMIT = 56 * 1024 * 1024

RWKV_PREC = dict(lora="bf", gate="bf", cumsum="x2r")


def _split_bf16(a):
    hi = a.astype(BF16)
    return hi, (a - hi.astype(F32)).astype(BF16)


def _dot(a, b, dims, prec):
    dn = (dims, ((), ()))
    one = lambda x, y: lax.dot_general(x, y, dn, preferred_element_type=F32)
    if prec == "x2r":
        ah = a.astype(BF16)
        bh, bl = _split_bf16(b)
        return one(ah, bh) + one(ah, bl)
    assert prec == "bf", prec
    return one(a.astype(BF16), b.astype(BF16))


def _mm(a, b, prec):
    return _dot(a, b, ((1,), (0,)), prec)


def _mm_nt(a, b, prec):
    return _dot(a, b, ((1,), (1,)), prec)


def _mm_tn(a, b, prec):
    return _dot(a, b, ((0,), (0,)), prec)


def _silu(x):
    h = 0.5 * x
    return h + h * jnp.tanh(h)


def _softplus(x):
    return jnp.maximum(x, 0.0) + jnp.log1p(jnp.exp(-jnp.abs(x)))


def _rms(x, w):
    return x * lax.rsqrt(jnp.mean(x * x, axis=-1, keepdims=True) + NORM_EPS) * w


def _tri_masks(n, rev):
    r = lax.broadcasted_iota(jnp.int32, (n, n), 0)
    c = lax.broadcasted_iota(jnp.int32, (n, n), 1)
    if rev:
        return r <= c, r < c
    return r >= c, r > c


def _unit_tri_inv_many(mats, prec):
    assert prec == "bf"
    n = mats[0].shape[0]
    r = lax.broadcasted_iota(jnp.int32, (n, n), 0)
    c = lax.broadcasted_iota(jnp.int32, (n, n), 1)
    eye = jnp.where(r == c, 1.0, 0.0)
    within2 = jnp.where((r >> 1) == (c >> 1), 1.0, 0.0)
    ts = [eye - a * within2 for a in mats]
    abs_ = [a.astype(BF16) for a in mats]
    shift = 1
    while (1 << shift) < n:
        join = jnp.where(((r >> (shift + 1)) == (c >> (shift + 1))) & ((r >> shift) != (c >> shift)),
                         1.0, 0.0).astype(BF16)
        tbs = [t.astype(BF16) for t in ts]
        inner = [_mm(ab * join, tb, prec) for ab, tb in zip(abs_, tbs)]
        ts = [t - _mm(tb, w, prec) for t, tb, w in zip(ts, tbs, inner)]
        shift += 1
    return ts


def _shifted_rows(x, prev_row, next_row):
    n = x.shape[0]
    row = lax.broadcasted_iota(jnp.int32, x.shape, 0)
    xp = jnp.where(row == 0, prev_row, pltpu.roll(x, 1, 0))
    xn = jnp.where(row == n - 1, next_row, pltpu.roll(x, n - 1, 0))
    return xp, xn


def _mod_kernel(c_ref, w_ref, b_ref, evt_ref, o_ref, wt_ref):
    s = _silu(c_ref[...])
    o_ref[...] = _mm(s, w_ref[...], prec="bf") + b_ref[...]
    wt_ref[...] = evt_ref[...].astype(BF16)


def _modulation(c_rows, mod_w, mod_b, ev_w_in):
    depth = mod_w.shape[0]
    n_even, _, ev_cols = ev_w_in.shape
    nblk = MOD_COL_BLOCKS
    width = N_MOD * D_MODEL // nblk
    steps = depth * nblk
    rows = n_even * ev_cols
    n_side = rows // EV_CAST_ROWS
    assert rows % EV_CAST_ROWS == 0 and n_side <= steps and EV_CAST_ROWS % 16 == 0 and width % 128 == 0
    side = pl.BlockSpec((EV_CAST_ROWS, D_MODEL), lambda l, j: (jnp.minimum(l * nblk + j, n_side - 1), 0))
    out, w_t = pl.pallas_call(
        _mod_kernel,
        grid=(depth, nblk),
        in_specs=[
            pl.BlockSpec((MOD_ROWS, D_MODEL), lambda l, j: (0, 0)),
            pl.BlockSpec((None, D_MODEL, width), lambda l, j: (l, 0, j)),
            pl.BlockSpec((None, 1, width), lambda l, j: (l, 0, j)),
            side,
        ],
        out_specs=[pl.BlockSpec((None, MOD_ROWS, width), lambda l, j: (l, 0, j)), side],
        out_shape=[jax.ShapeDtypeStruct((depth, MOD_ROWS, N_MOD * D_MODEL), F32),
                   jax.ShapeDtypeStruct((rows, D_MODEL), BF16)],
        compiler_params=pltpu.CompilerParams(dimension_semantics=("arbitrary", "arbitrary"),
                                             vmem_limit_bytes=VMEM_LIMIT),
        name="modulation",
    )(c_rows, mod_w, mod_b.reshape(depth, 1, N_MOD * D_MODEL), jnp.swapaxes(ev_w_in, 1, 2).reshape(rows, D_MODEL))
    return out.reshape(depth, MOD_ROWS, N_MOD, D_MODEL), w_t.reshape(n_even, ev_cols, D_MODEL)


def _mod_spec(layer, tiles_per_seq, row_base, row_step):
    return pl.BlockSpec((None, None, N_MOD, D_MODEL),
                        lambda i: (layer, row_base + (i // tiles_per_seq) * row_step, 0, 0))


def _inproj_kernel(*refs, n_w):
    x_ref, mod_ref, nw_ref = refs[:3]
    w_refs = refs[3:3 + n_w]
    o_refs = refs[3 + n_w:]
    h = _rms(x_ref[...], nw_ref[...])
    h = (h * (1.0 + mod_ref[1:2, :]) + mod_ref[0:1, :]).astype(BF16)
    pieces = [_mm(h, w_ref[...], prec="bf") for w_ref in w_refs]
    y = pieces[0] if n_w == 1 else jnp.concatenate(pieces, axis=1)
    off = 0
    for o_ref in o_refs:
        n = o_ref.shape[-1]
        o_ref[...] = y[:, off:off + n]
        off += n


def _inproj(x, mods, norm_w, ws_bf16, splits, mod_spec):
    t = x.shape[0]
    assert sum(w.shape[1] for w in ws_bf16) == sum(splits) and all(w.shape[1] % 128 == 0 for w in ws_bf16)
    return pl.pallas_call(
        functools.partial(_inproj_kernel, n_w=len(ws_bf16)),
        grid=(t // TOKEN_TILE,),
        in_specs=[
            pl.BlockSpec((TOKEN_TILE, D_MODEL), lambda i: (i, 0)),
            mod_spec,
            pl.BlockSpec((1, D_MODEL), lambda i: (0, 0)),
        ] + [pl.BlockSpec(w.shape, lambda i: (0, 0)) for w in ws_bf16],
        out_specs=[pl.BlockSpec((TOKEN_TILE, n), lambda i: (i, 0)) for n in splits],
        out_shape=[jax.ShapeDtypeStruct((t, n), F32) for n in splits],
        compiler_params=pltpu.CompilerParams(dimension_semantics=("parallel",), vmem_limit_bytes=VMEM_LIMIT),
        name="inproj",
    )(x, mods, norm_w.reshape(1, D_MODEL), *ws_bf16)


def _inproj_even_kernel(xp_ref, x_ref, xn_ref, mod_ref, nw_ref, wt_ref, cqkv_ref, csc_ref,
                        alog_ref, dtb_ref, qkv_ref, gz_ref, sc_ref, gate_ref, *, seq_len):
    tile = x_ref.shape[0]
    proj = lambda row0, width: _mm_nt(h, wt_ref[row0:row0 + width, :], "bf")
    x = jnp.concatenate([xp_ref[...], x_ref[...], xn_ref[...]], axis=0)
    h = _rms(x, nw_ref[...])
    h = (h * (1.0 + mod_ref[1:2, :]) + mod_ref[0:1, :]).astype(BF16)
    n = tile + 2 * HALO
    first = pl.program_id(0) * tile - HALO
    pos = (first + lax.broadcasted_iota(jnp.int32, (n, 1), 0)) & (seq_len - 1)
    at_start = pos == 0
    at_end = pos == seq_len - 1

    def conv3(v, c_ref):
        vp = jnp.where(at_start, 0.0, pltpu.roll(v, 1, 0))
        vn = jnp.where(at_end, 0.0, pltpu.roll(v, n - 1, 0))
        return vp * c_ref[0:1, :] + v * c_ref[1:2, :] + vn * c_ref[2:3, :]

    keep = slice(HALO, HALO + tile)
    wide = 2 * GDN_D
    gate_row = GDN_QKV_W + GDN_HEADS * GDN_D
    sc_row = gate_row + 4 * GDN_HEADS
    for j in range(GDN_QKV_W // wide):
        cols = slice(j * wide, (j + 1) * wide)
        act = _silu(conv3(proj(j * wide, wide), cqkv_ref.at[:, cols]))[keep]
        for i in range(2):
            part = act[:, i * GDN_D:(i + 1) * GDN_D]
            if j < 2 * GDN_HEADS // 2:
                part = part * lax.rsqrt(jnp.sum(part * part, axis=-1, keepdims=True) + 1e-6)
                if j < GDN_HEADS // 2:
                    part = part * (GDN_D ** -0.5)
            qkv_ref[:, j * wide + i * GDN_D:j * wide + (i + 1) * GDN_D] = part.astype(BF16)
    for j in range(GDN_HEADS * GDN_D // wide):
        cols = slice(j * wide, (j + 1) * wide)
        gz_ref[:, cols] = _silu(proj(GDN_QKV_W + j * wide, wide)[keep]).astype(BF16)
    for j in range(SC_WIDTH // wide):
        sc_b, sc_c, sc_h = [proj(sc_row + i * SC_WIDTH + j * wide, wide) for i in range(3)]
        cols = slice(j * wide, (j + 1) * wide)
        sc_ref[:, cols] = (sc_b * conv3(sc_c * sc_h, csc_ref.at[:, cols]))[keep].astype(BF16)
    g = proj(gate_row, 128)[keep]
    lane = lax.broadcasted_iota(jnp.int32, g.shape, 1)
    gate_ref[...] = jnp.where(lane < 2 * GDN_HEADS, jax.nn.sigmoid(g),
                              (-LOG2E * jnp.exp(alog_ref[...])) * _softplus(g + dtb_ref[...]))


def _inproj_even(x, mods, norm_w, w_t, conv_w, sc_conv_w, alog_vec, dtb_vec, seq_len, mod_spec):
    t = x.shape[0]
    tile = TOKEN_TILE
    assert seq_len & (seq_len - 1) == 0 and t % tile == 0
    per = tile // HALO
    last = t // HALO - 1
    const = lambda i: (0, 0)
    tok = lambda n: pl.BlockSpec((tile, n), lambda i: (i, 0))
    return pl.pallas_call(
        functools.partial(_inproj_even_kernel, seq_len=seq_len),
        grid=(t // tile,),
        in_specs=[
            pl.BlockSpec((HALO, D_MODEL), lambda i: (jnp.maximum(i * per - 1, 0), 0)),
            pl.BlockSpec((tile, D_MODEL), lambda i: (i, 0)),
            pl.BlockSpec((HALO, D_MODEL), lambda i: (jnp.minimum((i + 1) * per, last), 0)),
            mod_spec,
            pl.BlockSpec((1, D_MODEL), const),
            pl.BlockSpec(w_t.shape, const),
            pl.BlockSpec(conv_w.shape, const), pl.BlockSpec(sc_conv_w.shape, const),
            pl.BlockSpec((1, 128), const), pl.BlockSpec((1, 128), const),
        ],
        out_specs=[tok(GDN_QKV_W), tok(GDN_HEADS * GDN_D), tok(SC_WIDTH), tok(128)],
        out_shape=[jax.ShapeDtypeStruct((t, GDN_QKV_W), BF16), jax.ShapeDtypeStruct((t, GDN_HEADS * GDN_D), BF16),
                   jax.ShapeDtypeStruct((t, SC_WIDTH), BF16), jax.ShapeDtypeStruct((t, 128), F32)],
        compiler_params=pltpu.CompilerParams(dimension_semantics=("parallel",), vmem_limit_bytes=VMEM_LIMIT),
        name="inproj_even",
    )(x, x, x, mods, norm_w.reshape(1, D_MODEL), w_t, conv_w, sc_conv_w, alog_vec, dtb_vec)


def _mlp_kernel(a_ref, b_ref, x_ref, mod_ref, nw_ref, woa_ref, wob_ref, w1_ref, w2_ref, nf_ref, o_ref, *, final):
    y = _mm(a_ref[...], woa_ref[...], prec="bf") + _mm(b_ref[...], wob_ref[...], prec="bf")
    x1 = x_ref[...] + mod_ref[2:3, :] * y
    h = _rms(x1, nw_ref[...])
    h = (h * (1.0 + mod_ref[4:5, :]) + mod_ref[3:4, :]).astype(BF16)
    acc = jnp.zeros(x1.shape, F32)
    for j in range(D_FF // D_MODEL):
        cols = slice(j * D_MODEL, (j + 1) * D_MODEL)
        u = jnp.maximum(_mm(h, w1_ref[:, cols], prec="bf"), 0.0)
        acc = acc + _mm(u * u, w2_ref[cols, :], prec="bf")
    x2 = x1 + mod_ref[5:6, :] * acc
    if final:
        x2 = _rms(x2, nf_ref[...])
    o_ref[...] = x2


def _outproj_mlp(a, b, x, mods, norm_w, w_out, w1, w2, norm_final, mod_spec, final):
    t = x.shape[0]
    half = a.shape[1]
    const = lambda i: (0, 0)
    return pl.pallas_call(
        functools.partial(_mlp_kernel, final=final),
        grid=(t // TOKEN_TILE,),
        in_specs=[
            pl.BlockSpec((TOKEN_TILE, half), lambda i: (i, 0)),
            pl.BlockSpec((TOKEN_TILE, half), lambda i: (i, 0)),
            pl.BlockSpec((TOKEN_TILE, D_MODEL), lambda i: (i, 0)),
            mod_spec,
            pl.BlockSpec((1, D_MODEL), const),
            pl.BlockSpec((half, D_MODEL), const),
            pl.BlockSpec((half, D_MODEL), lambda i: (1, 0)),
            pl.BlockSpec((D_MODEL, D_FF), const),
            pl.BlockSpec((D_FF, D_MODEL), const),
            pl.BlockSpec((1, D_MODEL), const),
        ],
        out_specs=pl.BlockSpec((TOKEN_TILE, D_MODEL), lambda i: (i, 0)),
        out_shape=jax.ShapeDtypeStruct((t, D_MODEL), F32),
        compiler_params=pltpu.CompilerParams(dimension_semantics=("parallel",), vmem_limit_bytes=VMEM_LIMIT),
        name="outproj_mlp",
    )(a, b, x, mods, norm_w.reshape(1, D_MODEL), w_out, w_out, w1, w2, norm_final.reshape(1, D_MODEL))


def _gdn_decay_terms(g, rev):
    c = g.shape[0]
    incl = _tri_masks(c, rev)[0]
    before_col = _tri_masks(c, not rev)[0]
    eye = lax.broadcasted_iota(jnp.int32, (c, c), 0) == lax.broadcasted_iota(jnp.int32, (c, c), 1)
    gc_row = jnp.sum(jnp.where(before_col, jnp.broadcast_to(g, (c, c)), 0.0), axis=0, keepdims=True)
    gc_col = jnp.sum(jnp.where(eye, jnp.broadcast_to(gc_row, (c, c)), 0.0), axis=1, keepdims=True)
    decay = jnp.where(incl, jnp.exp2(jnp.where(incl, gc_col - gc_row, 0.0)), 0.0)
    g_tot = jnp.sum(g, axis=0, keepdims=True)
    return decay, jnp.exp2(gc_col), jnp.exp2(g_tot - gc_col), jnp.exp2(g_tot)


def _cast_side_job(arrays_layers, n_steps, step_index):
    in_specs, out_specs, out_shapes = [], [], []
    for arr, layer in arrays_layers:
        _, r, c = arr.shape
        blk = r // n_steps
        assert r % n_steps == 0 and blk % 16 == 0
        in_specs.append(pl.BlockSpec((None, blk, c), lambda *g, layer=layer: (layer, step_index(*g), 0)))
        out_specs.append(pl.BlockSpec((blk, c), lambda *g: (step_index(*g), 0)))
        out_shapes.append(jax.ShapeDtypeStruct((r, c), BF16))
    return in_specs, out_specs, out_shapes


def _gdn_kernel(*refs, seq_len, seqs, has_init, n_cast):
    q_ref, k_ref, v_ref, gz_ref, gate_ref, gn_ref = refs[:6]
    pos = 6
    if has_init:
        s0f_ref, s0b_ref = refs[pos:pos + 2]
        pos += 2
    cast_in = refs[pos:pos + n_cast]
    pos += n_cast
    o_ref, sf_ref, sb_ref = refs[pos:pos + 3]
    cast_out = refs[pos + 3:pos + 3 + n_cast]
    pos += n_cast
    for src, dst in zip(cast_in, cast_out):
        dst[...] = src[...].astype(BF16)
    osum, u_s, w_s, qd_s, kd_s, in_s, ge_s, st = refs[pos + 3:]
    hps = GDN_HEADS_PER_STEP
    head0 = pl.program_id(1) * hps
    ch = GDN_CHUNK
    n_chunks = seq_len // ch
    osum[...] = jnp.zeros(osum.shape, F32)
    hcols = [slice(hh * GDN_D, (hh + 1) * GDN_D) for hh in range(hps)]

    lane = lax.broadcasted_iota(jnp.int32, (ch, 128), 1)

    def pick(rows, col):
        return jnp.sum(jnp.where(lane == col, gate_ref[rows, :], 0.0), axis=1, keepdims=True)

    def solve_group(gi, carry):
        items = []
        for j in range(GDN_GROUP):
            c = gi * GDN_GROUP + j
            rows = pl.ds(pl.multiple_of(c * ch, ch), ch)
            for hh in range(hps):
                items.append(dict(c=c, hh=hh, rows=rows, q=q_ref[rows, hcols[hh]].astype(F32),
                                  k=k_ref[rows, hcols[hh]].astype(F32), v=v_ref[rows, hcols[hh]].astype(F32)))
        kks = [_mm_nt(it["k"], it["k"], "bf") for it in items]
        qks = [_mm_nt(it["q"], it["k"], "bf") for it in items]
        subs = []
        for it, kk, qk in zip(items, kks, qks):
            for d in range(2):
                beta = pick(it["rows"], d * GDN_HEADS + head0 + it["hh"])
                g = pick(it["rows"], 2 * GDN_HEADS + d * GDN_HEADS + head0 + it["hh"])
                decay, e_gc, e_rest, e_tot = _gdn_decay_terms(g, rev=(d == 1))
                strict = _tri_masks(ch, d == 1)[1]
                subs.append(dict(
                    d=d, c=it["c"], hh=it["hh"], rows=it["rows"],
                    a=jnp.where(strict, kk * beta * decay, 0.0),
                    rhs=jnp.concatenate([it["v"] * beta, it["k"] * (beta * e_gc)], axis=1),
                    intra=qk * decay, qd=it["q"] * e_gc, kd=it["k"] * e_rest, ge=e_tot))
        ts = _unit_tri_inv_many([s["a"] for s in subs], "bf")
        uws = [_mm(t, s["rhs"], "bf") for t, s in zip(ts, subs)]
        for s, uw in zip(subs, uws):
            d, hh, rows = s["d"], s["hh"], s["rows"]
            u_s[d, hh, rows, :] = uw[:, :GDN_D]
            w_s[d, hh, rows, :] = uw[:, GDN_D:].astype(BF16)
            qd_s[d, hh, rows, :] = s["qd"].astype(BF16)
            kd_s[d, hh, rows, :] = s["kd"].astype(BF16)
            in_s[d, hh, rows, :] = s["intra"].astype(BF16)
            ge_s[d, hh, pl.ds(pl.multiple_of(s["c"] * 8, 8), 8), :] = jnp.broadcast_to(s["ge"], (8, 128))
        return carry

    lax.fori_loop(0, seqs * n_chunks // GDN_GROUP, solve_group, 0)

    for hh in range(hps):
        for j in range(seqs):
            st[0, hh, j] = s0f_ref[j, hh] if has_init else jnp.zeros((GDN_D, GDN_D), F32)
            st[1, hh, j] = s0b_ref[j, hh] if has_init else jnp.zeros((GDN_D, GDN_D), F32)

    def recur(i, carry):
        cs = []
        for hh in range(hps):
            for j in range(seqs):
                for d in range(2):
                    c = j * n_chunks + (i if d == 0 else n_chunks - 1 - i)
                    cs.append(dict(d=d, hh=hh, j=j, rows=pl.ds(pl.multiple_of(c * ch, ch), ch),
                                   ge=ge_s[d, hh, pl.ds(pl.multiple_of(c * 8, 8), 1), :], s=st[d, hh, j]))
        sbs = [c["s"].astype(BF16) for c in cs]
        wss = [_mm(w_s[c["d"], c["hh"], c["rows"], :], sb, "bf") for c, sb in zip(cs, sbs)]
        qss = [_mm(qd_s[c["d"], c["hh"], c["rows"], :], sb, "bf") for c, sb in zip(cs, sbs)]
        ebs = [(u_s[c["d"], c["hh"], c["rows"], :] - ws).astype(BF16) for c, ws in zip(cs, wss)]
        outs = [qs_ + _mm(in_s[c["d"], c["hh"], c["rows"], :], eb, "bf") for c, qs_, eb in zip(cs, qss, ebs)]
        s_news = [c["s"] * c["ge"] + _mm_tn(kd_s[c["d"], c["hh"], c["rows"], :], eb, "bf")
                  for c, eb in zip(cs, ebs)]
        for c, o, s_new in zip(cs, outs, s_news):
            st[c["d"], c["hh"], c["j"]] = s_new
            osum[c["rows"], hcols[c["hh"]]] += o
        return carry

    lax.fori_loop(0, n_chunks, recur, 0)
    for hh in range(hps):
        for j in range(seqs):
            sf_ref[j, hh] = st[0, hh, j]
            sb_ref[j, hh] = st[1, hh, j]
        o_ref[:, hcols[hh]] = (_rms(osum[:, hcols[hh]], gn_ref[...]) * gz_ref[:, hcols[hh]].astype(F32))


def _gdn_mixer(qkv, gz, gates, n_seq, seq_len, gdn_norm, s0_f, s0_b, cast=()):
    has_init = s0_f is not None
    hd = GDN_HEADS
    hps = GDN_HEADS_PER_STEP
    hgroups = hd // hps
    seqs = max(1, GDN_BLOCK_ROWS // seq_len)
    rows = seqs * seq_len
    assert n_seq % seqs == 0 and (rows // GDN_CHUNK) % GDN_GROUP == 0 and hd % hps == 0

    def col(block):
        return pl.BlockSpec((rows, hps * GDN_D), lambda s, h, b=block: (s, b * hgroups + h))

    state = pl.BlockSpec((seqs, None, hps, GDN_D, GDN_D), lambda s, h: (s, 0, h, 0, 0))
    in_specs = [col(0), col(1), col(2), col(0), pl.BlockSpec((rows, 128), lambda s, h: (s, 0)),
                pl.BlockSpec((1, 128), lambda s, h: (0, 0))]
    args = [qkv, qkv, qkv, gz, gates, gdn_norm.reshape(1, 128)]
    if has_init:
        in_specs += [state, state]
        args += [s0_f, s0_b]
    n_steps = (n_seq // seqs) * hgroups
    cast_in, cast_out, cast_shapes = _cast_side_job(cast, n_steps, lambda s, h: s * hgroups + h)
    in_specs += cast_in
    args += [arr for arr, _ in cast]
    t = n_seq * seq_len
    scratch = ([pltpu.VMEM((rows, hps * GDN_D), F32)]
               + [pltpu.VMEM((2, hps, rows, GDN_D), F32)]
               + [pltpu.VMEM((2, hps, rows, GDN_D), BF16) for _ in range(3)]
               + [pltpu.VMEM((2, hps, rows, GDN_CHUNK), BF16),
                  pltpu.VMEM((2, hps, rows // GDN_CHUNK * 8, 128), F32),
                  pltpu.VMEM((2, hps, seqs, GDN_D, GDN_D), F32)])
    return pl.pallas_call(
        functools.partial(_gdn_kernel, seq_len=seq_len, seqs=seqs, has_init=has_init, n_cast=len(cast)),
        grid=(n_seq // seqs, hgroups),
        in_specs=in_specs,
        out_specs=[pl.BlockSpec((rows, hps * GDN_D), lambda s, h: (s, h)), state, state] + cast_out,
        out_shape=[jax.ShapeDtypeStruct((t, hd * GDN_D), F32),
                   jax.ShapeDtypeStruct((n_seq, 1, hd, GDN_D, GDN_D), F32),
                   jax.ShapeDtypeStruct((n_seq, 1, hd, GDN_D, GDN_D), F32)] + cast_shapes,
        scratch_shapes=scratch,
        compiler_params=pltpu.CompilerParams(dimension_semantics=("parallel", "parallel"),
                                             vmem_limit_bytes=VMEM_LIMIT),
        name="gdn_mixer",
    )(*args)


def _softmax_pv(scores, values, sink):
    m = sink
    for s in scores:
        m = jnp.maximum(m, jnp.max(s, axis=-1, keepdims=True))
    den = jnp.exp2(sink - m)
    acc = None
    for s, v in zip(scores, values):
        e = jnp.exp2(s - m)
        den = den + jnp.sum(e, axis=-1, keepdims=True)
        pv = _mm(e, v, prec="bf")
        acc = pv if acc is None else acc + pv
    return acc / den


def _group_sinks(sink_ref, j, rows):
    assert rows & (rows - 1) == 0
    grp = lax.broadcasted_iota(jnp.int32, (ATT_GROUP * rows, 1), 0) >> (rows.bit_length() - 1)
    col = jnp.full(grp.shape, sink_ref[j * ATT_GROUP], F32)
    for gi in range(1, ATT_GROUP):
        col = jnp.where(grp == gi, sink_ref[j * ATT_GROUP + gi], col)
    return col * LOG2E


def _store_group(o_ref, j, o, rows):
    for gi in range(ATT_GROUP):
        hh = j * ATT_GROUP + gi
        o_ref[:, hh * ATT_HD:(hh + 1) * ATT_HD] = o[gi * rows:(gi + 1) * rows]


def _attn_ctx_kernel(*refs, n_cast, seqs):
    sink_ref, p_ref = refs[:2]
    cast_in = refs[2:2 + n_cast]
    o_ref, kc_ref, vc_ref = refs[2 + n_cast:5 + n_cast]
    for src, dst in zip(cast_in, refs[5 + n_cast:]):
        dst[...] = src[...].astype(BF16)
    scale = ATT_HD ** -0.5 * LOG2E
    rows = p_ref.shape[0] // seqs
    chains = []
    for sq in range(seqs):
        tok = slice(sq * rows, (sq + 1) * rows)
        for j in range(ATT_KV_HEADS):
            k = p_ref[tok, ATT_Q_W + j * ATT_HD:ATT_Q_W + (j + 1) * ATT_HD]
            v = p_ref[tok, ATT_Q_W + ATT_KV_W + j * ATT_HD:ATT_Q_W + ATT_KV_W + (j + 1) * ATT_HD]
            kc_ref[sq, j] = k
            vc_ref[sq, j] = v
            q = jnp.concatenate([p_ref[tok, hh * ATT_HD:(hh + 1) * ATT_HD]
                                 for hh in range(j * ATT_GROUP, (j + 1) * ATT_GROUP)], axis=0) * scale
            chains.append((tok, j, q, k, v))
    scores = [_mm_nt(q, k, prec="bf") for _, _, q, k, _ in chains]
    outs = [_softmax_pv([s], [v], _group_sinks(sink_ref, j, rows)) for s, (_, j, _, _, v) in zip(scores, chains)]
    for o, (tok, j, _, _, _) in zip(outs, chains):
        _store_group(o_ref.at[tok, :], j, o, rows)


def _attn_context(proj_att, sink, n_seq, seq_len, cast=()):
    seqs = ATT_CTX_SEQS
    assert n_seq % seqs == 0
    kv = pl.BlockSpec((seqs, None, ATT_KV_HEADS, seq_len, ATT_HD), lambda b: (b, 0, 0, 0, 0))
    cast_in, cast_out, cast_shapes = _cast_side_job(cast, n_seq // seqs, lambda b: b)
    return pl.pallas_call(
        functools.partial(_attn_ctx_kernel, n_cast=len(cast), seqs=seqs),
        grid=(n_seq // seqs,),
        in_specs=[pl.BlockSpec(memory_space=pltpu.SMEM),
                  pl.BlockSpec((seqs * seq_len, ATT_W), lambda b: (b, 0))] + cast_in,
        out_specs=[pl.BlockSpec((seqs * seq_len, ATT_Q_W), lambda b: (b, 0)), kv, kv] + cast_out,
        out_shape=[jax.ShapeDtypeStruct((n_seq * seq_len, ATT_Q_W), F32),
                   jax.ShapeDtypeStruct((n_seq, 1, ATT_KV_HEADS, seq_len, ATT_HD), F32),
                   jax.ShapeDtypeStruct((n_seq, 1, ATT_KV_HEADS, seq_len, ATT_HD), F32)] + cast_shapes,
        compiler_params=pltpu.CompilerParams(dimension_semantics=("parallel",), vmem_limit_bytes=VMEM_LIMIT),
        name="attn_context",
    )(sink, proj_att, *[arr for arr, _ in cast])


def _rope_tables(seq_len):
    pos = np.arange(seq_len)
    half = ATT_HD // 2
    inv = ROPE_BASE ** (-np.arange(0, half, 2, dtype=np.float32) / half)
    ang_r = (pos // GRID_W).astype(np.float32)[:, None] * inv
    ang_c = (pos % GRID_W).astype(np.float32)[:, None] * inv
    cos = np.concatenate([np.cos(ang_r), np.cos(ang_r), np.cos(ang_c), np.cos(ang_c)], axis=1)
    sin = np.concatenate([-np.sin(ang_r), np.sin(ang_r), -np.sin(ang_c), np.sin(ang_c)], axis=1)
    return (jnp.asarray(np.tile(cos, (1, 2)), F32), jnp.asarray(np.tile(sin, (1, 2)), F32))


def _rope(x, cos, sin):
    lane = lax.broadcasted_iota(jnp.int32, x.shape, 1)
    partner = jnp.where((lane & 31) < 16, pltpu.roll(x, 128 - 16, 1), pltpu.roll(x, 16, 1))
    return x * cos + partner * sin


def _attn_lat_kernel(sink_ref, p_ref, ck_ref, cv_ref, cos_ref, sin_ref, o_ref, *, seq_len):
    scale = ATT_HD ** -0.5 * LOG2E
    qb = pl.program_id(1)
    span = 3 * ATT_BLOCK
    q0 = pl.multiple_of(qb * ATT_BLOCK, ATT_BLOCK)
    k0 = pl.multiple_of(jnp.clip((qb - 1) * ATT_BLOCK, 0, seq_len - span), ATT_BLOCK)
    qrows = pl.ds(q0, ATT_BLOCK)
    krows = pl.ds(k0, span)
    kwin = _rope(p_ref[krows, ATT_Q_W:ATT_Q_W + ATT_KV_W], cos_ref[krows, :], sin_ref[krows, :])
    vwin = p_ref[krows, ATT_Q_W + ATT_KV_W:ATT_W]
    stacked = ATT_GROUP * ATT_BLOCK
    qpos = q0 + (lax.broadcasted_iota(jnp.int32, (stacked, span), 0) & (ATT_BLOCK - 1))
    kpos = k0 + lax.broadcasted_iota(jnp.int32, (stacked, span), 1)
    valid = jnp.abs(qpos - kpos) <= WINDOW
    cos_q = cos_ref[qrows, :]
    sin_q = sin_ref[qrows, :]
    heads = []
    for pair in range(ATT_HEADS // 2):
        qpair = _rope(p_ref[qrows, pair * 128:(pair + 1) * 128], cos_q, sin_q) * scale
        heads += [qpair[:, :ATT_HD], qpair[:, ATT_HD:]]
    kv = range(ATT_KV_HEADS)
    qs = [jnp.concatenate(heads[j * ATT_GROUP:(j + 1) * ATT_GROUP], axis=0) for j in kv]
    s_locs = [jnp.where(valid, _mm_nt(qs[j], kwin[:, j * ATT_HD:(j + 1) * ATT_HD], prec="bf"), NEG_INF)
              for j in kv]
    s_ctxs = [_mm_nt(qs[j], ck_ref[j], prec="bf") for j in kv]
    outs = [_softmax_pv([s_locs[j], s_ctxs[j]], [vwin[:, j * ATT_HD:(j + 1) * ATT_HD], cv_ref[j]],
                        _group_sinks(sink_ref, j, ATT_BLOCK)) for j in kv]
    for j in kv:
        _store_group(o_ref, j, outs[j], ATT_BLOCK)


def _attn_latent(proj_att, sink, cache_k, cache_v, layer, n_seq, seq_len):
    cos, sin = _rope_tables(seq_len)
    past = cache_k.shape[3]
    nqb = seq_len // ATT_BLOCK
    cache = pl.BlockSpec((None, None, ATT_KV_HEADS, past, ATT_HD), lambda b, q: (b, layer, 0, 0, 0))
    table = pl.BlockSpec((seq_len, 128), lambda b, q: (0, 0))
    return pl.pallas_call(
        functools.partial(_attn_lat_kernel, seq_len=seq_len),
        grid=(n_seq, nqb),
        in_specs=[pl.BlockSpec(memory_space=pltpu.SMEM),
                  pl.BlockSpec((seq_len, ATT_W), lambda b, q: (b, 0)),
                  cache, cache, table, table],
        out_specs=pl.BlockSpec((ATT_BLOCK, ATT_Q_W), lambda b, q: (b * nqb + q, 0)),
        out_shape=jax.ShapeDtypeStruct((n_seq * seq_len, ATT_Q_W), F32),
        compiler_params=pltpu.CompilerParams(dimension_semantics=("parallel", "parallel")),
        name="attn_latent",
    )(sink, proj_att, cache_k, cache_v, cos, sin)


def _pair_masks(n, rev):
    r = lax.broadcasted_iota(jnp.int32, (n, 2 * n), 0)
    c = lax.broadcasted_iota(jnp.int32, (n, 2 * n), 1) & (n - 1)
    return (r <= c, r < c) if rev else (r >= c, r > c)


def _bd(x):
    half = x.shape[1] // 2
    lane = lax.broadcasted_iota(jnp.int32, x.shape, 1)
    zero = jnp.zeros_like(x)
    return jnp.concatenate([jnp.where(lane < half, x, zero), jnp.where(lane >= half, x, zero)], axis=0)


def _bd_mask(n):
    r = lax.broadcasted_iota(jnp.int32, (2 * n, 2 * n), 0)
    c = lax.broadcasted_iota(jnp.int32, (2 * n, 2 * n), 1)
    return (r < n) == (c < n)


def _head_sums(x):
    ones = jnp.where(_bd_mask(RWKV_HD), 1.0, 0.0).astype(BF16)
    return _mm(x, ones, "bf")


def _apply_pairs(tb, x):
    hi, lo = _split_bf16(x)
    return _mm(tb, _bd(hi), "bf") + _mm(tb, _bd(lo), "bf")


def _unit_tri_inv_pairs(mats):
    n = mats[0].shape[0]
    r = lax.broadcasted_iota(jnp.int32, (n, 2 * n), 0)
    c = lax.broadcasted_iota(jnp.int32, (n, 2 * n), 1) & (n - 1)
    eye = jnp.where(r == c, 1.0, 0.0)
    within2 = jnp.where((r >> 1) == (c >> 1), 1.0, 0.0)
    ts = [eye - a * within2 for a in mats]
    abs_ = [a.astype(BF16) for a in mats]
    shift = 1
    while (1 << shift) < n:
        join = jnp.where(((r >> (shift + 1)) == (c >> (shift + 1))) & ((r >> shift) != (c >> shift)),
                         1.0, 0.0).astype(BF16)
        tbs = [t.astype(BF16) for t in ts]
        inner = [_mm(ab * join, _bd(tb), "bf") for ab, tb in zip(abs_, tbs)]
        ts = [t - _mm(tb, _bd(w.astype(BF16)), "bf") for t, tb, w in zip(ts, tbs, inner)]
        shift += 1
    return ts


def _rwkv_chunk_operands(x_ref, prm, sc, c, *, seq_len):
    (mu_ref, w0_ref, wup_ref, a0_ref, aup_ref, gup_ref, kk_ref, ka_ref, rk_ref) = prm
    ch = RWKV_CHUNK
    total = x_ref.shape[0]
    r0 = pl.multiple_of(c * ch, ch)
    rows = pl.ds(r0, ch)
    pos0 = r0 & (seq_len - 1)
    x = x_ref[rows, :]
    prev_row = x_ref[pl.ds(jnp.maximum(r0 - 1, 0), 1), :] * jnp.where(pos0 > 0, 1.0, 0.0)
    next_row = x_ref[pl.ds(jnp.minimum(r0 + ch, total - 1), 1), :] * jnp.where(pos0 + ch < seq_len, 1.0, 0.0)
    xp, xn = _shifted_rows(x, prev_row, next_row)
    mu0, mu1 = mu_ref[0:1, :], mu_ref[1:2, :]
    xs = x * (1.0 - mu0 - mu1) + xp * mu0 + xn * mu1
    r = xs[:, 0:RWKV_W]
    k = xs[:, RWKV_W:2 * RWKV_W]
    v = xs[:, 2 * RWKV_W:3 * RWKV_W]
    lo = 3 * RWKV_W
    p = RWKV_PREC
    gl = xs[:, lo + 4 * RWKV_LORA:lo + 6 * RWKV_LORA]
    sc["gate"][rows, :] = _mm(jax.nn.sigmoid(gl), gup_ref[...], p["gate"])
    sc["v"][rows, :] = v.astype(BF16)
    pairs = [slice(i * PAIR_W, (i + 1) * PAIR_W) for i in range(N_PAIRS)]
    kkv = k * kk_ref[...]
    kaps = []
    for cols in pairs:
        kk_p = kkv[:, cols]
        kaps.append(kk_p * lax.rsqrt(_head_sums(kk_p * kk_p) + 1e-6))
    items = []
    bonus = None
    for d in range(2):
        rev = d == 1
        wl = xs[:, lo + d * RWKV_LORA:lo + (d + 1) * RWKV_LORA]
        al = xs[:, lo + 2 * RWKV_LORA + d * RWKV_LORA:lo + 2 * RWKV_LORA + (d + 1) * RWKV_LORA]
        lw = (-DECAY_SCALE * LOG2E) * jax.nn.sigmoid(w0_ref[d:d + 1, :] + _mm(jnp.tanh(wl), wup_ref[d], p["lora"]))
        a = jax.nn.sigmoid(a0_ref[d:d + 1, :] + _mm(al, aup_ref[d], p["lora"]))
        k2 = k * (1.0 + (a - 1.0) * ka_ref[...])
        g_inc = _mm(jnp.where(_tri_masks(ch, rev)[0], 1.0, 0.0), lw, p["cumsum"])
        g_tot = jnp.sum(lw, axis=0, keepdims=True)
        e_neg = jnp.exp2(-g_inc)
        e_end = jnp.exp2(g_tot - g_inc)
        e_exc = jnp.exp2(g_inc - lw)
        r_dec = r * jnp.exp2(g_inc)
        k_neg = k2 * e_neg
        k_end = k2 * e_end
        sc["dec"][d, pl.ds(pl.multiple_of(c * 8, 8), 8), :] = jnp.broadcast_to(jnp.exp2(g_tot), (8, RWKV_W))
        rkr = r * k2 * rk_ref[...]
        bon_d = jnp.concatenate([_head_sums(rkr[:, cols]) for cols in pairs], axis=1) * v
        bonus = bon_d if bonus is None else bonus + bon_d
        for cols, kap in zip(pairs, kaps):
            b_p = kap * a[:, cols]
            items.append(dict(d=d, rows=rows, cols=cols, kap_dec=kap * e_exc[:, cols], r_dec=r_dec[:, cols],
                              b_neg=b_p * e_neg[:, cols], k_neg=k_neg[:, cols],
                              b_end=b_p * e_end[:, cols], k_end=k_end[:, cols], vb=v[:, cols].astype(BF16)))
    sc["bon"][rows, :] = bonus
    return items


def _rwkv_solve(items, sc):
    ch = RWKV_CHUNK
    masks = [_pair_masks(ch, False), _pair_masks(ch, True)]
    ms = [_mm_nt(jnp.concatenate([it["kap_dec"], it["r_dec"]], axis=0),
                 jnp.concatenate([_bd(it["b_neg"].astype(BF16)), _bd(it["k_neg"].astype(BF16))], axis=0), "bf")
          for it in items]
    a_abs = [jnp.where(masks[it["d"]][1], m[:ch, :2 * ch], 0.0) for it, m in zip(items, ms)]
    a_aks = [jnp.where(masks[it["d"]][1], m[:ch, 2 * ch:], 0.0) for it, m in zip(items, ms)]
    a_rbs = [jnp.where(masks[it["d"]][0], m[ch:, :2 * ch], 0.0) for it, m in zip(items, ms)]
    a_rks = [jnp.where(masks[it["d"]][0], m[ch:, 2 * ch:], 0.0) for it, m in zip(items, ms)]
    tbs = [t.astype(BF16) for t in _unit_tri_inv_pairs(a_abs)]
    akvs = [_mm(a_ak, _bd(it["vb"]), "bf") for it, a_ak in zip(items, a_aks)]
    w2s = [_apply_pairs(tb, it["kap_dec"]) for it, tb in zip(items, tbs)]
    u0s = [_apply_pairs(tb, akv) for tb, akv in zip(tbs, akvs)]
    for it, w2, u0, a_rb, a_rk in zip(items, w2s, u0s, a_rbs, a_rks):
        d, rows, cols = it["d"], it["rows"], it["cols"]
        sc["w2"][d, rows, cols] = w2.astype(BF16)
        sc["rd"][d, rows, cols] = it["r_dec"].astype(BF16)
        sc["u0"][d, rows, cols] = u0
        sc["arb"][d, rows, cols] = a_rb.astype(BF16)
        sc["ark"][d, rows, cols] = a_rk.astype(BF16)
        sc["bh"][d, rows, cols] = it["b_end"].astype(BF16)
        sc["kh"][d, rows, cols] = it["k_end"].astype(BF16)


def _rwkv_recur(sc, st, i, *, seq_len, seqs):
    ch = RWKV_CHUNK
    n_chunks = seq_len // ch
    keep = _bd_mask(RWKV_HD)
    cs = []
    for j in range(seqs):
        for d in range(2):
            c = j * n_chunks + (i if d == 0 else n_chunks - 1 - i)
            rows = pl.ds(pl.multiple_of(c * ch, ch), ch)
            dec = sc["dec"][d, pl.ds(pl.multiple_of(c * 8, 8), 1), :]
            for pi in range(N_PAIRS):
                cols = slice(pi * PAIR_W, (pi + 1) * PAIR_W)
                cs.append(dict(d=d, j=j, p=pi, rows=rows, cols=cols, dec=dec[:, cols], s=st[d, j, pi]))
    sbs = [c["s"].astype(BF16) for c in cs]
    lss = [_mm_nt(jnp.concatenate([sc["w2"][c["d"], c["rows"], c["cols"]],
                                   sc["rd"][c["d"], c["rows"], c["cols"]]], axis=0), sb, "bf")
           for c, sb in zip(cs, sbs)]
    ubs = [(-(ls[:ch] + sc["u0"][c["d"], c["rows"], c["cols"]])).astype(BF16) for c, ls in zip(cs, lss)]
    vbs = [sc["v"][c["rows"], c["cols"]] for c in cs]
    ys = [ls[ch:] + _mm(jnp.concatenate([sc["arb"][c["d"], c["rows"], c["cols"]],
                                         sc["ark"][c["d"], c["rows"], c["cols"]]], axis=1),
                        jnp.concatenate([_bd(ub), _bd(vb)], axis=0), "bf")
          for c, ls, ub, vb in zip(cs, lss, ubs, vbs)]
    ups = [_mm_tn(jnp.concatenate([ub, vb], axis=0),
                  jnp.concatenate([sc["bh"][c["d"], c["rows"], c["cols"]],
                                   sc["kh"][c["d"], c["rows"], c["cols"]]], axis=0), "bf")
           for c, ub, vb in zip(cs, ubs, vbs)]
    for c, y, up in zip(cs, ys, ups):
        st[c["d"], c["j"], c["p"]] = c["s"] * c["dec"] + jnp.where(keep, up, 0.0)
        sc["ysum"][c["rows"], c["cols"]] += y


RWKV_SCRATCH = ("ysum", "bon", "gate", "v", "w2", "rd", "u0", "arb", "ark", "bh", "kh", "dec")


def _rwkv_kernel(*refs, seq_len, seqs, has_init):
    x_ref = refs[0]
    prm = refs[1:10]
    lnw_ref, lnb_ref = refs[10:12]
    pos = 12
    if has_init:
        s0f_ref, s0b_ref = refs[pos:pos + 2]
        pos += 2
    o_ref, sf_ref, sb_ref = refs[pos:pos + 3]
    sc = dict(zip(RWKV_SCRATCH, refs[pos + 3:]))
    st = refs[pos + 3 + len(RWKV_SCRATCH)]
    ch = RWKV_CHUNK
    hd = RWKV_HD
    n_chunks = seq_len // ch
    sc["ysum"][...] = jnp.zeros(sc["ysum"].shape, F32)

    def prepare(gi, carry):
        items = []
        for j in range(RWKV_GROUP):
            items += _rwkv_chunk_operands(x_ref, prm, sc, gi * RWKV_GROUP + j, seq_len=seq_len)
        _rwkv_solve(items, sc)
        return carry

    lax.fori_loop(0, seqs * n_chunks // RWKV_GROUP, prepare, 0)

    zero = jnp.zeros((hd, hd), F32)
    for d, s0_ref in enumerate((s0f_ref, s0b_ref) if has_init else (None, None)):
        for j in range(seqs):
            for pi in range(N_PAIRS):
                s_a = s0_ref[j, 2 * pi] if has_init else zero
                s_b = s0_ref[j, 2 * pi + 1] if has_init else zero
                st[d, j, pi] = jnp.concatenate([jnp.concatenate([s_a, zero], axis=1),
                                                jnp.concatenate([zero, s_b], axis=1)], axis=0)

    def recur(i, carry):
        _rwkv_recur(sc, st, i, seq_len=seq_len, seqs=seqs)
        return carry

    lax.fori_loop(0, n_chunks, recur, 0)
    for d, out_ref in enumerate((sf_ref, sb_ref)):
        for j in range(seqs):
            for pi in range(N_PAIRS):
                s = st[d, j, pi]
                out_ref[j, 2 * pi] = s[:hd, :hd]
                out_ref[j, 2 * pi + 1] = s[hd:, hd:]

    tile = RWKV_FINISH_ROWS

    def finish(i, carry):
        rows = pl.ds(pl.multiple_of(i * tile, tile), tile)
        pairs = [slice(pi * PAIR_W, (pi + 1) * PAIR_W) for pi in range(N_PAIRS)]
        ys = [sc["ysum"][rows, cols] for cols in pairs]
        cens = [y - _head_sums(y) * (1.0 / hd) for y in ys]
        vars_ = [_head_sums(cen * cen) * (1.0 / hd) for cen in cens]
        for cols, cen, var in zip(pairs, cens, vars_):
            yn = cen * lax.rsqrt(var + GN_EPS) * lnw_ref[:, cols] + lnb_ref[:, cols]
            o_ref[rows, cols] = (yn + sc["bon"][rows, cols]) * sc["gate"][rows, cols]
        return carry

    lax.fori_loop(0, seqs * seq_len // tile, finish, 0)


def _rwkv_mixer(x_rw, n_seq, seq_len, params, s0_f, s0_b):
    has_init = s0_f is not None
    (mu, w0, w_up, a0, a_up, g_up, k_k, k_a, r_k, ln_w, ln_b) = params
    row = lambda a: a.reshape(1, RWKV_W)
    args = [x_rw, mu, w0, w_up, a0, a_up, g_up, row(k_k), row(k_a), row(r_k), row(ln_w), row(ln_b)]

    def whole(a):
        nd = a.ndim
        return pl.BlockSpec(a.shape, lambda s, nd=nd: (0,) * nd)

    seqs = max(1, RWKV_BLOCK_ROWS // seq_len)
    rows = seqs * seq_len
    assert n_seq % seqs == 0 and (rows // RWKV_CHUNK) % RWKV_GROUP == 0 and seq_len & (seq_len - 1) == 0
    in_specs = [pl.BlockSpec((rows, RWKV_IN), lambda s: (s, 0))] + [whole(a) for a in args[1:]]
    state = pl.BlockSpec((seqs, None, RWKV_HEADS, RWKV_HD, RWKV_HD), lambda s: (s, 0, 0, 0, 0))
    if has_init:
        in_specs += [state, state]
        args += [s0_f, s0_b]
    st_shape = jax.ShapeDtypeStruct((n_seq, 1, RWKV_HEADS, RWKV_HD, RWKV_HD), F32)
    tok = lambda dt: pltpu.VMEM((rows, RWKV_W), dt)
    per_dir = lambda dt: pltpu.VMEM((2, rows, RWKV_W), dt)
    scratch = dict(ysum=tok(F32), bon=tok(F32), gate=tok(F32), v=tok(BF16), w2=per_dir(BF16), rd=per_dir(BF16),
                   u0=per_dir(F32), arb=per_dir(BF16), ark=per_dir(BF16), bh=per_dir(BF16), kh=per_dir(BF16),
                   dec=pltpu.VMEM((2, rows // RWKV_CHUNK * 8, RWKV_W), F32))
    return pl.pallas_call(
        functools.partial(_rwkv_kernel, seq_len=seq_len, seqs=seqs, has_init=has_init),
        grid=(n_seq // seqs,),
        in_specs=in_specs,
        out_specs=[pl.BlockSpec((rows, RWKV_W), lambda s: (s, 0)), state, state],
        out_shape=[jax.ShapeDtypeStruct((n_seq * seq_len, RWKV_W), F32), st_shape, st_shape],
        scratch_shapes=[scratch[name] for name in RWKV_SCRATCH]
        + [pltpu.VMEM((2, seqs, N_PAIRS, PAIR_W, PAIR_W), F32)],
        compiler_params=pltpu.CompilerParams(dimension_semantics=("parallel",), vmem_limit_bytes=VMEM_LIMIT),
        name="rwkv_mixer",
    )(*args)


def kernel(x_prompt, x_sample, state_gdn_fwd, state_gdn_bwd, cache_attn_k, cache_attn_v, state_rwkv_fwd, state_rwkv_bwd, c, c_ctx, mod_w, mod_b, norm_mix, norm_mlp, mlp_w1, mlp_w2, norm_final, ev_w_in, ev_w_out, gdn_conv, gdn_a_log, gdn_dt_bias, gdn_norm, sc_conv, od_w_in, od_w_out, attn_sink, rwkv_mu, rwkv_w0, rwkv_w_up, rwkv_a0, rwkv_a_up, rwkv_g_up, rwkv_k_k, rwkv_k_a, rwkv_r_k, rwkv_ln_w, rwkv_ln_b):
    bp, lp, _ = x_prompt.shape
    bs, ls, _ = x_sample.shape
    depth = mod_w.shape[0]
    c_rows = jnp.concatenate([c_ctx[None, :], c, jnp.zeros((MOD_ROWS - 1 - bs, D_MODEL), F32)], axis=0)
    mods, ev_w_t = _modulation(c_rows, mod_w, mod_b, ev_w_in)

    assert ls % TOKEN_TILE == 0 and (bp * lp) % TOKEN_TILE == 0
    groups = [
        dict(x=x_prompt.reshape(bp * lp, D_MODEL), n=bp, l=lp, latent=False,
             mod=lambda layer: _mod_spec(layer, 1, 0, 0)),
        dict(x=x_sample.reshape(bs * ls, D_MODEL), n=bs, l=ls, latent=True,
             mod=lambda layer: _mod_spec(layer, ls // TOKEN_TILE, 1, 1)),
    ]
    outs = {}
    for layer in range(depth):
        final = layer == depth - 1
        mlp_cast = ((mlp_w1, layer), (mlp_w2, layer))
        if layer % 2 == 0:
            e = layer // 2
            alog_vec = jnp.zeros((1, 128), F32).at[0, 2 * GDN_HEADS:4 * GDN_HEADS].set(gdn_a_log[e].reshape(-1))
            dtb_vec = jnp.zeros((1, 128), F32).at[0, 2 * GDN_HEADS:4 * GDN_HEADS].set(gdn_dt_bias[e].reshape(-1))
            for grp in groups:
                qkv, gz, sc, gates = _inproj_even(grp["x"], mods, norm_mix[layer], ev_w_t[e], gdn_conv[e],
                                                  sc_conv[e], alog_vec, dtb_vec, grp["l"], grp["mod"](layer))
                s0 = (state_gdn_fwd[:, e:e + 1], state_gdn_bwd[:, e:e + 1]) if grp["latent"] else (None, None)
                if grp["latent"]:
                    cast = ((od_w_in, layer // 2), (od_w_out, layer // 2)) if layer + 1 < depth else ()
                    o, _, _, *next_w = _gdn_mixer(qkv, gz, gates, grp["n"], grp["l"], gdn_norm[e], *s0, cast=cast)
                else:
                    o, s_f, s_b, w_out, w1, w2 = _gdn_mixer(qkv, gz, gates, grp["n"], grp["l"], gdn_norm[e], *s0,
                                                            cast=((ev_w_out, e),) + mlp_cast)
                    outs.setdefault("gdn_f", []).append(s_f)
                    outs.setdefault("gdn_b", []).append(s_b)
                grp["x"] = _outproj_mlp(o, sc, grp["x"], mods, norm_mlp[layer], w_out, w1, w2, norm_final,
                                        grp["mod"](layer), final)
        else:
            o_ = layer // 2
            w_in, w_out = [next_w[0]], next_w[1]
            rw = (rwkv_mu[o_], rwkv_w0[o_], rwkv_w_up[o_], rwkv_a0[o_], rwkv_a_up[o_], rwkv_g_up[o_],
                  rwkv_k_k[o_], rwkv_k_a[o_], rwkv_r_k[o_].reshape(-1), rwkv_ln_w[o_], rwkv_ln_b[o_])
            for grp in groups:
                p_att, x_rw = _inproj(grp["x"], mods, norm_mix[layer], w_in, (ATT_W, RWKV_IN), grp["mod"](layer))
                if grp["latent"]:
                    att = _attn_latent(p_att, attn_sink[o_], cache_attn_k, cache_attn_v, o_, grp["n"], grp["l"])
                    rwo, _, _ = _rwkv_mixer(x_rw, grp["n"], grp["l"], rw,
                                            state_rwkv_fwd[:, o_:o_ + 1], state_rwkv_bwd[:, o_:o_ + 1])
                else:
                    att, kc, vc, w1, w2 = _attn_context(p_att, attn_sink[o_], grp["n"], grp["l"], cast=mlp_cast)
                    rwo, s_f, s_b = _rwkv_mixer(x_rw, grp["n"], grp["l"], rw, None, None)
                    outs.setdefault("att_k", []).append(kc)
                    outs.setdefault("att_v", []).append(vc)
                    outs.setdefault("rw_f", []).append(s_f)
                    outs.setdefault("rw_b", []).append(s_b)
                grp["x"] = _outproj_mlp(att, rwo, grp["x"], mods, norm_mlp[layer], w_out, w1, w2, norm_final,
                                        grp["mod"](layer), final)
    cat = lambda key: jnp.concatenate(outs[key], axis=1)
    return (groups[0]["x"].reshape(bp, lp, D_MODEL), groups[1]["x"].reshape(bs, ls, D_MODEL),
            cat("gdn_f"), cat("gdn_b"), cat("att_k"), cat("att_v"), cat("rw_f"), cat("rw_b"))
```

```python
import functools

import jax
import jax.numpy as jnp
import numpy as np
from jax import lax
from jax.experimental import pallas as pl
from jax.experimental.pallas import tpu as pltpu

F32 = jnp.float32
BF16 = jnp.bfloat16

D_MODEL = 1024
N_MOD = 6
D_FF = 4 * D_MODEL
NORM_EPS = 1e-6
TOKEN_TILE = 512
HALO = 8
MOD_ROWS = 8

GDN_HEADS = 4
GDN_D = 128
GDN_CHUNK = 128
GDN_SOLVE_CHAINS = 32
GDN_RECUR_CHAINS = 8
GDN_BLOCK_ROWS = 1024
GDN_QKV_W = 3 * GDN_HEADS * GDN_D
SC_WIDTH = 512
MOD_COL_BLOCKS = 4
EV_CAST_ROWS = 720

ATT_HEADS = 8
ATT_KV_HEADS = 2
ATT_GROUP = ATT_HEADS // ATT_KV_HEADS
ATT_HD = 64
ATT_Q_W = ATT_HEADS * ATT_HD
ATT_KV_W = ATT_KV_HEADS * ATT_HD
ATT_W = ATT_Q_W + 2 * ATT_KV_W
ATT_CTX_SEQS = 2
WINDOW = 128
ATT_BLOCK = 128
GRID_W = 64
ROPE_BASE = 10000.0
NEG_INF = -1e30

RWKV_HEADS = 8
RWKV_HD = 64
RWKV_W = RWKV_HEADS * RWKV_HD
RWKV_LORA = 64
RWKV_IN = 3 * RWKV_W + 3 * 2 * RWKV_LORA
RWKV_CHUNK = 64
RWKV_GROUP = 4
RWKV_FINISH_ROWS = 256
DECAY_SCALE = float(np.exp(-0.5))
LOG2E = float(np.log2(np.e))
RWKV_BLOCK_ROWS = 1024
PAIR_W = 2 * RWKV_HD
N_PAIRS = RWKV_W // PAIR_W
GN_EPS = 64e-5

VMEM_LIMIT = 56 * 1024 * 1024

RWKV_PREC = dict(lora="bf", gate="bf", cumsum="x2r")


def _split_bf16(a):
    hi = a.astype(BF16)
    return hi, (a - hi.astype(F32)).astype(BF16)


def _dot(a, b, dims, prec):
    dn = (dims, ((), ()))
    one = lambda x, y: lax.dot_general(x, y, dn, preferred_element_type=F32)
    if prec == "x2r":
        ah = a.astype(BF16)
        bh, bl = _split_bf16(b)
        return one(ah, bh) + one(ah, bl)
    assert prec == "bf", prec
    return one(a.astype(BF16), b.astype(BF16))


def _mm(a, b, prec):
    return _dot(a, b, ((1,), (0,)), prec)


def _mm_nt(a, b, prec):
    return _dot(a, b, ((1,), (1,)), prec)


def _mm_tn(a, b, prec):
    return _dot(a, b, ((0,), (0,)), prec)


def _silu(x):
    h = 0.5 * x
    return h + h * jnp.tanh(h)


def _softplus(x):
    return jnp.maximum(x, 0.0) + jnp.log1p(jnp.exp(-jnp.abs(x)))


def _rms(x, w):
    return x * lax.rsqrt(jnp.mean(x * x, axis=-1, keepdims=True) + NORM_EPS) * w


def _tri_masks(n, rev):
    r = lax.broadcasted_iota(jnp.int32, (n, n), 0)
    c = lax.broadcasted_iota(jnp.int32, (n, n), 1)
    if rev:
        return r <= c, r < c
    return r >= c, r > c


def _unit_tri_inv_many(mats, prec):
    assert prec == "bf"
    n = mats[0].shape[0]
    r = lax.broadcasted_iota(jnp.int32, (n, n), 0)
    c = lax.broadcasted_iota(jnp.int32, (n, n), 1)
    eye = jnp.where(r == c, 1.0, 0.0)
    within2 = jnp.where((r >> 1) == (c >> 1), 1.0, 0.0)
    ts = [eye - a * within2 for a in mats]
    abs_ = [a.astype(BF16) for a in mats]
    shift = 1
    while (1 << shift) < n:
        join = jnp.where(((r >> (shift + 1)) == (c >> (shift + 1))) & ((r >> shift) != (c >> shift)),
                         1.0, 0.0).astype(BF16)
        tbs = [t.astype(BF16) for t in ts]
        inner = [_mm(ab * join, tb, prec) for ab, tb in zip(abs_, tbs)]
        ts = [t - _mm(tb, w, prec) for t, tb, w in zip(ts, tbs, inner)]
        shift += 1
    return ts


def _shifted_rows(x, prev_row, next_row):
    n = x.shape[0]
    row = lax.broadcasted_iota(jnp.int32, x.shape, 0)
    xp = jnp.where(row == 0, prev_row, pltpu.roll(x, 1, 0))
    xn = jnp.where(row == n - 1, next_row, pltpu.roll(x, n - 1, 0))
    return xp, xn


def _mod_kernel(c_ref, w_ref, b_ref, evt_ref, o_ref, wt_ref):
    s = _silu(c_ref[...])
    o_ref[...] = _mm(s, w_ref[...], prec="bf") + b_ref[...]
    wt_ref[...] = evt_ref[...].astype(BF16)


def _modulation(c_rows, mod_w, mod_b, ev_w_in):
    depth = mod_w.shape[0]
    n_even, _, ev_cols = ev_w_in.shape
    nblk = MOD_COL_BLOCKS
    width = N_MOD * D_MODEL // nblk
    steps = depth * nblk
    rows = n_even * ev_cols
    n_side = rows // EV_CAST_ROWS
    assert rows % EV_CAST_ROWS == 0 and n_side <= steps and EV_CAST_ROWS % 16 == 0 and width % 128 == 0
    side = pl.BlockSpec((EV_CAST_ROWS, D_MODEL), lambda l, j: (jnp.minimum(l * nblk + j, n_side - 1), 0))
    out, w_t = pl.pallas_call(
        _mod_kernel,
        grid=(depth, nblk),
        in_specs=[
            pl.BlockSpec((MOD_ROWS, D_MODEL), lambda l, j: (0, 0)),
            pl.BlockSpec((None, D_MODEL, width), lambda l, j: (l, 0, j)),
            pl.BlockSpec((None, 1, width), lambda l, j: (l, 0, j)),
            side,
        ],
        out_specs=[pl.BlockSpec((None, MOD_ROWS, width), lambda l, j: (l, 0, j)), side],
        out_shape=[jax.ShapeDtypeStruct((depth, MOD_ROWS, N_MOD * D_MODEL), F32),
                   jax.ShapeDtypeStruct((rows, D_MODEL), BF16)],
        compiler_params=pltpu.CompilerParams(dimension_semantics=("arbitrary", "arbitrary"),
                                             vmem_limit_bytes=VMEM_LIMIT),
        name="modulation",
    )(c_rows, mod_w, mod_b.reshape(depth, 1, N_MOD * D_MODEL), jnp.swapaxes(ev_w_in, 1, 2).reshape(rows, D_MODEL))
    return out.reshape(depth, MOD_ROWS, N_MOD, D_MODEL), w_t.reshape(n_even, ev_cols, D_MODEL)


def _mod_spec(layer, tiles_per_seq, row_base, row_step):
    return pl.BlockSpec((None, None, N_MOD, D_MODEL),
                        lambda i: (layer, row_base + (i // tiles_per_seq) * row_step, 0, 0))


def _inproj_kernel(*refs, n_w):
    x_ref, mod_ref, nw_ref = refs[:3]
    w_refs = refs[3:3 + n_w]
    o_refs = refs[3 + n_w:]
    h = _rms(x_ref[...], nw_ref[...])
    h = (h * (1.0 + mod_ref[1:2, :]) + mod_ref[0:1, :]).astype(BF16)
    pieces = [_mm(h, w_ref[...], prec="bf") for w_ref in w_refs]
    y = pieces[0] if n_w == 1 else jnp.concatenate(pieces, axis=1)
    off = 0
    for o_ref in o_refs:
        n = o_ref.shape[-1]
        o_ref[...] = y[:, off:off + n]
        off += n


def _inproj(x, mods, norm_w, ws_bf16, splits, mod_spec):
    t = x.shape[0]
    assert sum(w.shape[1] for w in ws_bf16) == sum(splits) and all(w.shape[1] % 128 == 0 for w in ws_bf16)
    return pl.pallas_call(
        functools.partial(_inproj_kernel, n_w=len(ws_bf16)),
        grid=(t // TOKEN_TILE,),
        in_specs=[
            pl.BlockSpec((TOKEN_TILE, D_MODEL), lambda i: (i, 0)),
            mod_spec,
            pl.BlockSpec((1, D_MODEL), lambda i: (0, 0)),
        ] + [pl.BlockSpec(w.shape, lambda i: (0, 0)) for w in ws_bf16],
        out_specs=[pl.BlockSpec((TOKEN_TILE, n), lambda i: (i, 0)) for n in splits],
        out_shape=[jax.ShapeDtypeStruct((t, n), F32) for n in splits],
        compiler_params=pltpu.CompilerParams(dimension_semantics=("parallel",), vmem_limit_bytes=VMEM_LIMIT),
        name="inproj",
    )(x, mods, norm_w.reshape(1, D_MODEL), *ws_bf16)


def _inproj_even_kernel(xp_ref, x_ref, xn_ref, mod_ref, nw_ref, wt_ref, cqkv_ref, csc_ref,
                        alog_ref, dtb_ref, qkv_ref, gz_ref, sc_ref, gate_ref, *, seq_len):
    tile = x_ref.shape[0]
    proj = lambda row0, width: _mm_nt(h, wt_ref[row0:row0 + width, :], "bf")
    x = jnp.concatenate([xp_ref[...], x_ref[...], xn_ref[...]], axis=0)
    h = _rms(x, nw_ref[...])
    h = (h * (1.0 + mod_ref[1:2, :]) + mod_ref[0:1, :]).astype(BF16)
    n = tile + 2 * HALO
    first = pl.program_id(0) * tile - HALO
    pos = (first + lax.broadcasted_iota(jnp.int32, (n, 1), 0)) & (seq_len - 1)
    at_start = pos == 0
    at_end = pos == seq_len - 1

    def conv3(v, c_ref):
        vp = jnp.where(at_start, 0.0, pltpu.roll(v, 1, 0))
        vn = jnp.where(at_end, 0.0, pltpu.roll(v, n - 1, 0))
        return vp * c_ref[0:1, :] + v * c_ref[1:2, :] + vn * c_ref[2:3, :]

    keep = slice(HALO, HALO + tile)
    wide = 2 * GDN_D
    gate_row = GDN_QKV_W + GDN_HEADS * GDN_D
    sc_row = gate_row + 4 * GDN_HEADS
    for j in range(GDN_QKV_W // wide):
        cols = slice(j * wide, (j + 1) * wide)
        act = _silu(conv3(proj(j * wide, wide), cqkv_ref.at[:, cols]))[keep]
        for i in range(2):
            part = act[:, i * GDN_D:(i + 1) * GDN_D]
            if j < 2 * GDN_HEADS // 2:
                part = part * lax.rsqrt(jnp.sum(part * part, axis=-1, keepdims=True) + 1e-6)
                if j < GDN_HEADS // 2:
                    part = part * (GDN_D ** -0.5)
            qkv_ref[:, j * wide + i * GDN_D:j * wide + (i + 1) * GDN_D] = part.astype(BF16)
    for j in range(GDN_HEADS * GDN_D // wide):
        cols = slice(j * wide, (j + 1) * wide)
        gz_ref[:, cols] = _silu(proj(GDN_QKV_W + j * wide, wide)[keep]).astype(BF16)
    for j in range(SC_WIDTH // wide):
        sc_b, sc_c, sc_h = [proj(sc_row + i * SC_WIDTH + j * wide, wide) for i in range(3)]
        cols = slice(j * wide, (j + 1) * wide)
        sc_ref[:, cols] = (sc_b * conv3(sc_c * sc_h, csc_ref.at[:, cols]))[keep].astype(BF16)
    g = proj(gate_row, 128)[keep]
    lane = lax.broadcasted_iota(jnp.int32, g.shape, 1)
    gate_ref[...] = jnp.where(lane < 2 * GDN_HEADS, jax.nn.sigmoid(g),
                              (-LOG2E * jnp.exp(alog_ref[...])) * _softplus(g + dtb_ref[...]))


def _inproj_even(x, mods, norm_w, w_t, conv_w, sc_conv_w, alog_vec, dtb_vec, seq_len, mod_spec):
    t = x.shape[0]
    tile = TOKEN_TILE
    assert seq_len & (seq_len - 1) == 0 and t % tile == 0
    per = tile // HALO
    last = t // HALO - 1
    const = lambda i: (0, 0)
    tok = lambda n: pl.BlockSpec((tile, n), lambda i: (i, 0))
    return pl.pallas_call(
        functools.partial(_inproj_even_kernel, seq_len=seq_len),
        grid=(t // tile,),
        in_specs=[
            pl.BlockSpec((HALO, D_MODEL), lambda i: (jnp.maximum(i * per - 1, 0), 0)),
            pl.BlockSpec((tile, D_MODEL), lambda i: (i, 0)),
            pl.BlockSpec((HALO, D_MODEL), lambda i: (jnp.minimum((i + 1) * per, last), 0)),
            mod_spec,
            pl.BlockSpec((1, D_MODEL), const),
            pl.BlockSpec(w_t.shape, const),
            pl.BlockSpec(conv_w.shape, const), pl.BlockSpec(sc_conv_w.shape, const),
            pl.BlockSpec((1, 128), const), pl.BlockSpec((1, 128), const),
        ],
        out_specs=[tok(GDN_QKV_W), tok(GDN_HEADS * GDN_D), tok(SC_WIDTH), tok(128)],
        out_shape=[jax.ShapeDtypeStruct((t, GDN_QKV_W), BF16), jax.ShapeDtypeStruct((t, GDN_HEADS * GDN_D), BF16),
                   jax.ShapeDtypeStruct((t, SC_WIDTH), BF16), jax.ShapeDtypeStruct((t, 128), F32)],
        compiler_params=pltpu.CompilerParams(dimension_semantics=("parallel",), vmem_limit_bytes=VMEM_LIMIT),
        name="inproj_even",
    )(x, x, x, mods, norm_w.reshape(1, D_MODEL), w_t, conv_w, sc_conv_w, alog_vec, dtb_vec)


def _mlp_kernel(a_ref, b_ref, x_ref, mod_ref, nw_ref, woa_ref, wob_ref, w1_ref, w2_ref, nf_ref, o_ref, *, final):
    y = _mm(a_ref[...], woa_ref[...], prec="bf") + _mm(b_ref[...], wob_ref[...], prec="bf")
    x1 = x_ref[...] + mod_ref[2:3, :] * y
    h = _rms(x1, nw_ref[...])
    h = (h * (1.0 + mod_ref[4:5, :]) + mod_ref[3:4, :]).astype(BF16)
    acc = jnp.zeros(x1.shape, F32)
    for j in range(D_FF // D_MODEL):
        cols = slice(j * D_MODEL, (j + 1) * D_MODEL)
        u = jnp.maximum(_mm(h, w1_ref[:, cols], prec="bf"), 0.0)
        acc = acc + _mm(u * u, w2_ref[cols, :], prec="bf")
    x2 = x1 + mod_ref[5:6, :] * acc
    if final:
        x2 = _rms(x2, nf_ref[...])
    o_ref[...] = x2


def _outproj_mlp(a, b, x, mods, norm_w, w_out, w1, w2, norm_final, mod_spec, final):
    t = x.shape[0]
    half = a.shape[1]
    const = lambda i: (0, 0)
    return pl.pallas_call(
        functools.partial(_mlp_kernel, final=final),
        grid=(t // TOKEN_TILE,),
        in_specs=[
            pl.BlockSpec((TOKEN_TILE, half), lambda i: (i, 0)),
            pl.BlockSpec((TOKEN_TILE, half), lambda i: (i, 0)),
            pl.BlockSpec((TOKEN_TILE, D_MODEL), lambda i: (i, 0)),
            mod_spec,
            pl.BlockSpec((1, D_MODEL), const),
            pl.BlockSpec((half, D_MODEL), const),
            pl.BlockSpec((half, D_MODEL), lambda i: (1, 0)),
            pl.BlockSpec((D_MODEL, D_FF), const),
            pl.BlockSpec((D_FF, D_MODEL), const),
            pl.BlockSpec((1, D_MODEL), const),
        ],
        out_specs=pl.BlockSpec((TOKEN_TILE, D_MODEL), lambda i: (i, 0)),
        out_shape=jax.ShapeDtypeStruct((t, D_MODEL), F32),
        compiler_params=pltpu.CompilerParams(dimension_semantics=("parallel",), vmem_limit_bytes=VMEM_LIMIT),
        name="outproj_mlp",
    )(a, b, x, mods, norm_w.reshape(1, D_MODEL), w_out, w_out, w1, w2, norm_final.reshape(1, D_MODEL))


def _gdn_decay_terms(g, rev):
    c = g.shape[0]
    incl = _tri_masks(c, rev)[0]
    before_col = _tri_masks(c, not rev)[0]
    eye = lax.broadcasted_iota(jnp.int32, (c, c), 0) == lax.broadcasted_iota(jnp.int32, (c, c), 1)
    gc_row = jnp.sum(jnp.where(before_col, jnp.broadcast_to(g, (c, c)), 0.0), axis=0, keepdims=True)
    gc_col = jnp.sum(jnp.where(eye, jnp.broadcast_to(gc_row, (c, c)), 0.0), axis=1, keepdims=True)
    decay = jnp.where(incl, jnp.exp2(jnp.where(incl, gc_col - gc_row, 0.0)), 0.0)
    g_tot = jnp.sum(g, axis=0, keepdims=True)
    return decay, jnp.exp2(gc_col), jnp.exp2(g_tot - gc_col), jnp.exp2(g_tot)


def _cast_side_job(arrays_layers, n_steps, step_index):
    in_specs, out_specs, out_shapes = [], [], []
    for arr, layer in arrays_layers:
        _, r, c = arr.shape
        blk = r // n_steps
        assert r % n_steps == 0 and blk % 16 == 0
        in_specs.append(pl.BlockSpec((None, blk, c), lambda *g, layer=layer: (layer, step_index(*g), 0)))
        out_specs.append(pl.BlockSpec((blk, c), lambda *g: (step_index(*g), 0)))
        out_shapes.append(jax.ShapeDtypeStruct((r, c), BF16))
    return in_specs, out_specs, out_shapes


def _gdn_kernel(*refs, seq_len, seqs, hps, group, has_init, n_cast):
    q_ref, k_ref, v_ref, gz_ref, gate_ref, gn_ref = refs[:6]
    pos = 6
    if has_init:
        s0f_ref, s0b_ref = refs[pos:pos + 2]
        pos += 2
    cast_in = refs[pos:pos + n_cast]
    pos += n_cast
    o_ref, sf_ref, sb_ref = refs[pos:pos + 3]
    cast_out = refs[pos + 3:pos + 3 + n_cast]
    pos += n_cast
    for src, dst in zip(cast_in, cast_out):
        dst[...] = src[...].astype(BF16)
    osum, u_s, w_s, qd_s, kd_s, in_s, ge_s, st = refs[pos + 3:]
    head0 = pl.program_id(1) * hps
    ch = GDN_CHUNK
    n_chunks = seq_len // ch
    osum[...] = jnp.zeros(osum.shape, F32)
    hcols = [slice(hh * GDN_D, (hh + 1) * GDN_D) for hh in range(hps)]

    lane = lax.broadcasted_iota(jnp.int32, (ch, 128), 1)

    def pick(rows, col):
        return jnp.sum(jnp.where(lane == col, gate_ref[rows, :], 0.0), axis=1, keepdims=True)

    def solve_group(gi, carry):
        items = []
        for j in range(group):
            c = gi * group + j
            rows = pl.ds(pl.multiple_of(c * ch, ch), ch)
            for hh in range(hps):
                items.append(dict(c=c, hh=hh, rows=rows, q=q_ref[rows, hcols[hh]].astype(F32),
                                  k=k_ref[rows, hcols[hh]].astype(F32), v=v_ref[rows, hcols[hh]].astype(F32)))
        kks = [_mm_nt(it["k"], it["k"], "bf") for it in items]
        qks = [_mm_nt(it["q"], it["k"], "bf") for it in items]
        subs = []
        for it, kk, qk in zip(items, kks, qks):
            for d in range(2):
                beta = pick(it["rows"], d * GDN_HEADS + head0 + it["hh"])
                g = pick(it["rows"], 2 * GDN_HEADS + d * GDN_HEADS + head0 + it["hh"])
                decay, e_gc, e_rest, e_tot = _gdn_decay_terms(g, rev=(d == 1))
                strict = _tri_masks(ch, d == 1)[1]
                subs.append(dict(
                    d=d, c=it["c"], hh=it["hh"], rows=it["rows"],
                    a=jnp.where(strict, kk * beta * decay, 0.0),
                    rhs=jnp.concatenate([it["v"] * beta, it["k"] * (beta * e_gc)], axis=1),
                    intra=qk * decay, qd=it["q"] * e_gc, kd=it["k"] * e_rest, ge=e_tot))
        ts = _unit_tri_inv_many([s["a"] for s in subs], "bf")
        uws = [_mm(t, s["rhs"], "bf") for t, s in zip(ts, subs)]
        for s, uw in zip(subs, uws):
            d, hh, rows = s["d"], s["hh"], s["rows"]
            u_s[d, hh, rows, :] = uw[:, :GDN_D]
            w_s[d, hh, rows, :] = uw[:, GDN_D:].astype(BF16)
            qd_s[d, hh, rows, :] = s["qd"].astype(BF16)
            kd_s[d, hh, rows, :] = s["kd"].astype(BF16)
            in_s[d, hh, rows, :] = s["intra"].astype(BF16)
            ge_s[d, hh, pl.ds(pl.multiple_of(s["c"] * 8, 8), 8), :] = jnp.broadcast_to(s["ge"], (8, 128))
        return carry

    lax.fori_loop(0, seqs * n_chunks // group, solve_group, 0)

    for hh in range(hps):
        for j in range(seqs):
            st[0, hh, j] = s0f_ref[j, hh] if has_init else jnp.zeros((GDN_D, GDN_D), F32)
            st[1, hh, j] = s0b_ref[j, hh] if has_init else jnp.zeros((GDN_D, GDN_D), F32)

    def recur(i, carry):
        cs = []
        for hh in range(hps):
            for j in range(seqs):
                for d in range(2):
                    c = j * n_chunks + (i if d == 0 else n_chunks - 1 - i)
                    cs.append(dict(d=d, hh=hh, j=j, rows=pl.ds(pl.multiple_of(c * ch, ch), ch),
                                   ge=ge_s[d, hh, pl.ds(pl.multiple_of(c * 8, 8), 1), :], s=st[d, hh, j]))
        sbs = [c["s"].astype(BF16) for c in cs]
        wss = [_mm(w_s[c["d"], c["hh"], c["rows"], :], sb, "bf") for c, sb in zip(cs, sbs)]
        qss = [_mm(qd_s[c["d"], c["hh"], c["rows"], :], sb, "bf") for c, sb in zip(cs, sbs)]
        ebs = [(u_s[c["d"], c["hh"], c["rows"], :] - ws).astype(BF16) for c, ws in zip(cs, wss)]
        outs = [qs_ + _mm(in_s[c["d"], c["hh"], c["rows"], :], eb, "bf") for c, qs_, eb in zip(cs, qss, ebs)]
        s_news = [c["s"] * c["ge"] + _mm_tn(kd_s[c["d"], c["hh"], c["rows"], :], eb, "bf")
                  for c, eb in zip(cs, ebs)]
        for c, o, s_new in zip(cs, outs, s_news):
            st[c["d"], c["hh"], c["j"]] = s_new
            osum[c["rows"], hcols[c["hh"]]] += o
        return carry

    lax.fori_loop(0, n_chunks, recur, 0)
    for hh in range(hps):
        for j in range(seqs):
            sf_ref[j, hh] = st[0, hh, j]
            sb_ref[j, hh] = st[1, hh, j]
        o_ref[:, hcols[hh]] = (_rms(osum[:, hcols[hh]], gn_ref[...]) * gz_ref[:, hcols[hh]].astype(F32))


def _gdn_mixer(qkv, gz, gates, n_seq, seq_len, gdn_norm, s0_f, s0_b, cast=()):
    has_init = s0_f is not None
    hd = GDN_HEADS
    seqs = max(1, GDN_BLOCK_ROWS // seq_len)
    rows = seqs * seq_len
    hps = min(hd, max(2, GDN_RECUR_CHAINS // (2 * seqs)))
    group = GDN_SOLVE_CHAINS // (2 * hps)
    hgroups = hd // hps
    assert n_seq % seqs == 0 and (rows // GDN_CHUNK) % group == 0 and hd % hps == 0

    def col(block):
        return pl.BlockSpec((rows, hps * GDN_D), lambda s, h, b=block: (s, b * hgroups + h))

    state = pl.BlockSpec((seqs, None, hps, GDN_D, GDN_D), lambda s, h: (s, 0, h, 0, 0))
    in_specs = [col(0), col(1), col(2), col(0), pl.BlockSpec((rows, 128), lambda s, h: (s, 0)),
                pl.BlockSpec((1, 128), lambda s, h: (0, 0))]
    args = [qkv, qkv, qkv, gz, gates, gdn_norm.reshape(1, 128)]
    if has_init:
        in_specs += [state, state]
        args += [s0_f, s0_b]
    n_steps = (n_seq // seqs) * hgroups
    cast_in, cast_out, cast_shapes = _cast_side_job(cast, n_steps, lambda s, h: s * hgroups + h)
    in_specs += cast_in
    args += [arr for arr, _ in cast]
    t = n_seq * seq_len
    scratch = ([pltpu.VMEM((rows, hps * GDN_D), F32)]
               + [pltpu.VMEM((2, hps, rows, GDN_D), F32)]
               + [pltpu.VMEM((2, hps, rows, GDN_D), BF16) for _ in range(3)]
               + [pltpu.VMEM((2, hps, rows, GDN_CHUNK), BF16),
                  pltpu.VMEM((2, hps, rows // GDN_CHUNK * 8, 128), F32),
                  pltpu.VMEM((2, hps, seqs, GDN_D, GDN_D), F32)])
    return pl.pallas_call(
        functools.partial(_gdn_kernel, seq_len=seq_len, seqs=seqs, hps=hps, group=group, has_init=has_init,
                          n_cast=len(cast)),
        grid=(n_seq // seqs, hgroups),
        in_specs=in_specs,
        out_specs=[pl.BlockSpec((rows, hps * GDN_D), lambda s, h: (s, h)), state, state] + cast_out,
        out_shape=[jax.ShapeDtypeStruct((t, hd * GDN_D), F32),
                   jax.ShapeDtypeStruct((n_seq, 1, hd, GDN_D, GDN_D), F32),
                   jax.ShapeDtypeStruct((n_seq, 1, hd, GDN_D, GDN_D), F32)] + cast_shapes,
        scratch_shapes=scratch,
        compiler_params=pltpu.CompilerParams(dimension_semantics=("parallel", "parallel"),
                                             vmem_limit_bytes=VMEM_LIMIT),
        name="gdn_mixer",
    )(*args)


def _softmax_pv(scores, values, sink):
    m = sink
    for s in scores:
        m = jnp.maximum(m, jnp.max(s, axis=-1, keepdims=True))
    den = jnp.exp2(sink - m)
    acc = None
    for s, v in zip(scores, values):
        e = jnp.exp2(s - m)
        den = den + jnp.sum(e, axis=-1, keepdims=True)
        pv = _mm(e, v, prec="bf")
        acc = pv if acc is None else acc + pv
    return acc / den


def _group_sinks(sink_ref, j, rows):
    assert rows & (rows - 1) == 0
    grp = lax.broadcasted_iota(jnp.int32, (ATT_GROUP * rows, 1), 0) >> (rows.bit_length() - 1)
    col = jnp.full(grp.shape, sink_ref[j * ATT_GROUP], F32)
    for gi in range(1, ATT_GROUP):
        col = jnp.where(grp == gi, sink_ref[j * ATT_GROUP + gi], col)
    return col * LOG2E


def _store_group(o_ref, j, o, rows):
    for gi in range(ATT_GROUP):
        hh = j * ATT_GROUP + gi
        o_ref[:, hh * ATT_HD:(hh + 1) * ATT_HD] = o[gi * rows:(gi + 1) * rows]


def _attn_ctx_kernel(*refs, n_cast, seqs):
    sink_ref, p_ref = refs[:2]
    cast_in = refs[2:2 + n_cast]
    o_ref, kc_ref, vc_ref = refs[2 + n_cast:5 + n_cast]
    for src, dst in zip(cast_in, refs[5 + n_cast:]):
        dst[...] = src[...].astype(BF16)
    scale = ATT_HD ** -0.5 * LOG2E
    rows = p_ref.shape[0] // seqs
    chains = []
    for sq in range(seqs):
        tok = slice(sq * rows, (sq + 1) * rows)
        for j in range(ATT_KV_HEADS):
            k = p_ref[tok, ATT_Q_W + j * ATT_HD:ATT_Q_W + (j + 1) * ATT_HD]
            v = p_ref[tok, ATT_Q_W + ATT_KV_W + j * ATT_HD:ATT_Q_W + ATT_KV_W + (j + 1) * ATT_HD]
            kc_ref[sq, j] = k
            vc_ref[sq, j] = v
            q = jnp.concatenate([p_ref[tok, hh * ATT_HD:(hh + 1) * ATT_HD]
                                 for hh in range(j * ATT_GROUP, (j + 1) * ATT_GROUP)], axis=0) * scale
            chains.append((tok, j, q, k, v))
    scores = [_mm_nt(q, k, prec="bf") for _, _, q, k, _ in chains]
    outs = [_softmax_pv([s], [v], _group_sinks(sink_ref, j, rows)) for s, (_, j, _, _, v) in zip(scores, chains)]
    for o, (tok, j, _, _, _) in zip(outs, chains):
        _store_group(o_ref.at[tok, :], j, o, rows)


def _attn_context(proj_att, sink, n_seq, seq_len, cast=()):
    seqs = ATT_CTX_SEQS
    assert n_seq % seqs == 0
    kv = pl.BlockSpec((seqs, None, ATT_KV_HEADS, seq_len, ATT_HD), lambda b: (b, 0, 0, 0, 0))
    cast_in, cast_out, cast_shapes = _cast_side_job(cast, n_seq // seqs, lambda b: b)
    return pl.pallas_call(
        functools.partial(_attn_ctx_kernel, n_cast=len(cast), seqs=seqs),
        grid=(n_seq // seqs,),
        in_specs=[pl.BlockSpec(memory_space=pltpu.SMEM),
                  pl.BlockSpec((seqs * seq_len, ATT_W), lambda b: (b, 0))] + cast_in,
        out_specs=[pl.BlockSpec((seqs * seq_len, ATT_Q_W), lambda b: (b, 0)), kv, kv] + cast_out,
        out_shape=[jax.ShapeDtypeStruct((n_seq * seq_len, ATT_Q_W), F32),
                   jax.ShapeDtypeStruct((n_seq, 1, ATT_KV_HEADS, seq_len, ATT_HD), F32),
                   jax.ShapeDtypeStruct((n_seq, 1, ATT_KV_HEADS, seq_len, ATT_HD), F32)] + cast_shapes,
        compiler_params=pltpu.CompilerParams(dimension_semantics=("parallel",), vmem_limit_bytes=VMEM_LIMIT),
        name="attn_context",
    )(sink, proj_att, *[arr for arr, _ in cast])


def _rope_tables(seq_len):
    pos = np.arange(seq_len)
    half = ATT_HD // 2
    inv = ROPE_BASE ** (-np.arange(0, half, 2, dtype=np.float32) / half)
    ang_r = (pos // GRID_W).astype(np.float32)[:, None] * inv
    ang_c = (pos % GRID_W).astype(np.float32)[:, None] * inv
    cos = np.concatenate([np.cos(ang_r), np.cos(ang_r), np.cos(ang_c), np.cos(ang_c)], axis=1)
    sin = np.concatenate([-np.sin(ang_r), np.sin(ang_r), -np.sin(ang_c), np.sin(ang_c)], axis=1)
    return (jnp.asarray(np.tile(cos, (1, 2)), F32), jnp.asarray(np.tile(sin, (1, 2)), F32))


def _rope(x, cos, sin):
    lane = lax.broadcasted_iota(jnp.int32, x.shape, 1)
    partner = jnp.where((lane & 31) < 16, pltpu.roll(x, 128 - 16, 1), pltpu.roll(x, 16, 1))
    return x * cos + partner * sin


def _attn_lat_kernel(sink_ref, p_ref, ck_ref, cv_ref, cos_ref, sin_ref, o_ref, *, seq_len):
    scale = ATT_HD ** -0.5 * LOG2E
    qb = pl.program_id(1)
    span = 3 * ATT_BLOCK
    q0 = pl.multiple_of(qb * ATT_BLOCK, ATT_BLOCK)
    k0 = pl.multiple_of(jnp.clip((qb - 1) * ATT_BLOCK, 0, seq_len - span), ATT_BLOCK)
    qrows = pl.ds(q0, ATT_BLOCK)
    krows = pl.ds(k0, span)
    kwin = _rope(p_ref[krows, ATT_Q_W:ATT_Q_W + ATT_KV_W], cos_ref[krows, :], sin_ref[krows, :])
    vwin = p_ref[krows, ATT_Q_W + ATT_KV_W:ATT_W]
    stacked = ATT_GROUP * ATT_BLOCK
    qpos = q0 + (lax.broadcasted_iota(jnp.int32, (stacked, span), 0) & (ATT_BLOCK - 1))
    kpos = k0 + lax.broadcasted_iota(jnp.int32, (stacked, span), 1)
    valid = jnp.abs(qpos - kpos) <= WINDOW
    cos_q = cos_ref[qrows, :]
    sin_q = sin_ref[qrows, :]
    heads = []
    for pair in range(ATT_HEADS // 2):
        qpair = _rope(p_ref[qrows, pair * 128:(pair + 1) * 128], cos_q, sin_q) * scale
        heads += [qpair[:, :ATT_HD], qpair[:, ATT_HD:]]
    kv = range(ATT_KV_HEADS)
    qs = [jnp.concatenate(heads[j * ATT_GROUP:(j + 1) * ATT_GROUP], axis=0) for j in kv]
    s_locs = [jnp.where(valid, _mm_nt(qs[j], kwin[:, j * ATT_HD:(j + 1) * ATT_HD], prec="bf"), NEG_INF)
              for j in kv]
    s_ctxs = [_mm_nt(qs[j], ck_ref[j], prec="bf") for j in kv]
    outs = [_softmax_pv([s_locs[j], s_ctxs[j]], [vwin[:, j * ATT_HD:(j + 1) * ATT_HD], cv_ref[j]],
                        _group_sinks(sink_ref, j, ATT_BLOCK)) for j in kv]
    for j in kv:
        _store_group(o_ref, j, outs[j], ATT_BLOCK)


def _attn_latent(proj_att, sink, cache_k, cache_v, layer, n_seq, seq_len):
    cos, sin = _rope_tables(seq_len)
    past = cache_k.shape[3]
    nqb = seq_len // ATT_BLOCK
    cache = pl.BlockSpec((None, None, ATT_KV_HEADS, past, ATT_HD), lambda b, q: (b, layer, 0, 0, 0))
    table = pl.BlockSpec((seq_len, 128), lambda b, q: (0, 0))
    return pl.pallas_call(
        functools.partial(_attn_lat_kernel, seq_len=seq_len),
        grid=(n_seq, nqb),
        in_specs=[pl.BlockSpec(memory_space=pltpu.SMEM),
                  pl.BlockSpec((seq_len, ATT_W), lambda b, q: (b, 0)),
                  cache, cache, table, table],
        out_specs=pl.BlockSpec((ATT_BLOCK, ATT_Q_W), lambda b, q: (b * nqb + q, 0)),
        out_shape=jax.ShapeDtypeStruct((n_seq * seq_len, ATT_Q_W), F32),
        compiler_params=pltpu.CompilerParams(dimension_semantics=("parallel", "parallel")),
        name="attn_latent",
    )(sink, proj_att, cache_k, cache_v, cos, sin)


def _pair_masks(n, rev):
    r = lax.broadcasted_iota(jnp.int32, (n, 2 * n), 0)
    c = lax.broadcasted_iota(jnp.int32, (n, 2 * n), 1) & (n - 1)
    return (r <= c, r < c) if rev else (r >= c, r > c)


def _bd(x):
    half = x.shape[1] // 2
    lane = lax.broadcasted_iota(jnp.int32, x.shape, 1)
    zero = jnp.zeros_like(x)
    return jnp.concatenate([jnp.where(lane < half, x, zero), jnp.where(lane >= half, x, zero)], axis=0)


def _bd_mask(n):
    r = lax.broadcasted_iota(jnp.int32, (2 * n, 2 * n), 0)
    c = lax.broadcasted_iota(jnp.int32, (2 * n, 2 * n), 1)
    return (r < n) == (c < n)


def _head_sums(x):
    ones = jnp.where(_bd_mask(RWKV_HD), 1.0, 0.0).astype(BF16)
    return _mm(x, ones, "bf")


def _apply_pairs(tb, x):
    hi, lo = _split_bf16(x)
    return _mm(tb, _bd(hi), "bf") + _mm(tb, _bd(lo), "bf")


def _unit_tri_inv_pairs(mats):
    n = mats[0].shape[0]
    r = lax.broadcasted_iota(jnp.int32, (n, 2 * n), 0)
    c = lax.broadcasted_iota(jnp.int32, (n, 2 * n), 1) & (n - 1)
    eye = jnp.where(r == c, 1.0, 0.0)
    within2 = jnp.where((r >> 1) == (c >> 1), 1.0, 0.0)
    ts = [eye - a * within2 for a in mats]
    abs_ = [a.astype(BF16) for a in mats]
    shift = 1
    while (1 << shift) < n:
        join = jnp.where(((r >> (shift + 1)) == (c >> (shift + 1))) & ((r >> shift) != (c >> shift)),
                         1.0, 0.0).astype(BF16)
        tbs = [t.astype(BF16) for t in ts]
        inner = [_mm(ab * join, _bd(tb), "bf") for ab, tb in zip(abs_, tbs)]
        ts = [t - _mm(tb, _bd(w.astype(BF16)), "bf") for t, tb, w in zip(ts, tbs, inner)]
        shift += 1
    return ts


def _rwkv_chunk_operands(x_ref, prm, sc, c, *, seq_len):
    (mu_ref, w0_ref, wup_ref, a0_ref, aup_ref, gup_ref, kk_ref, ka_ref, rk_ref) = prm
    ch = RWKV_CHUNK
    total = x_ref.shape[0]
    r0 = pl.multiple_of(c * ch, ch)
    rows = pl.ds(r0, ch)
    pos0 = r0 & (seq_len - 1)
    x = x_ref[rows, :]
    prev_row = x_ref[pl.ds(jnp.maximum(r0 - 1, 0), 1), :] * jnp.where(pos0 > 0, 1.0, 0.0)
    next_row = x_ref[pl.ds(jnp.minimum(r0 + ch, total - 1), 1), :] * jnp.where(pos0 + ch < seq_len, 1.0, 0.0)
    xp, xn = _shifted_rows(x, prev_row, next_row)
    mu0, mu1 = mu_ref[0:1, :], mu_ref[1:2, :]
    xs = x * (1.0 - mu0 - mu1) + xp * mu0 + xn * mu1
    r = xs[:, 0:RWKV_W]
    k = xs[:, RWKV_W:2 * RWKV_W]
    v = xs[:, 2 * RWKV_W:3 * RWKV_W]
    lo = 3 * RWKV_W
    p = RWKV_PREC
    gl = xs[:, lo + 4 * RWKV_LORA:lo + 6 * RWKV_LORA]
    sc["gate"][rows, :] = _mm(jax.nn.sigmoid(gl), gup_ref[...], p["gate"])
    sc["v"][rows, :] = v.astype(BF16)
    pairs = [slice(i * PAIR_W, (i + 1) * PAIR_W) for i in range(N_PAIRS)]
    kkv = k * kk_ref[...]
    kaps = []
    for cols in pairs:
        kk_p = kkv[:, cols]
        kaps.append(kk_p * lax.rsqrt(_head_sums(kk_p * kk_p) + 1e-6))
    items = []
    bonus = None
    for d in range(2):
        rev = d == 1
        wl = xs[:, lo + d * RWKV_LORA:lo + (d + 1) * RWKV_LORA]
        al = xs[:, lo + 2 * RWKV_LORA + d * RWKV_LORA:lo + 2 * RWKV_LORA + (d + 1) * RWKV_LORA]
        lw = (-DECAY_SCALE * LOG2E) * jax.nn.sigmoid(w0_ref[d:d + 1, :] + _mm(jnp.tanh(wl), wup_ref[d], p["lora"]))
        a = jax.nn.sigmoid(a0_ref[d:d + 1, :] + _mm(al, aup_ref[d], p["lora"]))
        k2 = k * (1.0 + (a - 1.0) * ka_ref[...])
        g_inc = _mm(jnp.where(_tri_masks(ch, rev)[0], 1.0, 0.0), lw, p["cumsum"])
        g_tot = jnp.sum(lw, axis=0, keepdims=True)
        e_neg = jnp.exp2(-g_inc)
        e_end = jnp.exp2(g_tot - g_inc)
        e_exc = jnp.exp2(g_inc - lw)
        r_dec = r * jnp.exp2(g_inc)
        k_neg = k2 * e_neg
        k_end = k2 * e_end
        sc["dec"][d, pl.ds(pl.multiple_of(c * 8, 8), 8), :] = jnp.broadcast_to(jnp.exp2(g_tot), (8, RWKV_W))
        rkr = r * k2 * rk_ref[...]
        bon_d = jnp.concatenate([_head_sums(rkr[:, cols]) for cols in pairs], axis=1) * v
        bonus = bon_d if bonus is None else bonus + bon_d
        for cols, kap in zip(pairs, kaps):
            b_p = kap * a[:, cols]
            items.append(dict(d=d, rows=rows, cols=cols, kap_dec=kap * e_exc[:, cols], r_dec=r_dec[:, cols],
                              b_neg=b_p * e_neg[:, cols], k_neg=k_neg[:, cols],
                              b_end=b_p * e_end[:, cols], k_end=k_end[:, cols], vb=v[:, cols].astype(BF16)))
    sc["bon"][rows, :] = bonus
    return items


def _rwkv_solve(items, sc):
    ch = RWKV_CHUNK
    masks = [_pair_masks(ch, False), _pair_masks(ch, True)]
    ms = [_mm_nt(jnp.concatenate([it["kap_dec"], it["r_dec"]], axis=0),
                 jnp.concatenate([_bd(it["b_neg"].astype(BF16)), _bd(it["k_neg"].astype(BF16))], axis=0), "bf")
          for it in items]
    a_abs = [jnp.where(masks[it["d"]][1], m[:ch, :2 * ch], 0.0) for it, m in zip(items, ms)]
    a_aks = [jnp.where(masks[it["d"]][1], m[:ch, 2 * ch:], 0.0) for it, m in zip(items, ms)]
    a_rbs = [jnp.where(masks[it["d"]][0], m[ch:, :2 * ch], 0.0) for it, m in zip(items, ms)]
    a_rks = [jnp.where(masks[it["d"]][0], m[ch:, 2 * ch:], 0.0) for it, m in zip(items, ms)]
    tbs = [t.astype(BF16) for t in _unit_tri_inv_pairs(a_abs)]
    akvs = [_mm(a_ak, _bd(it["vb"]), "bf") for it, a_ak in zip(items, a_aks)]
    w2s = [_apply_pairs(tb, it["kap_dec"]) for it, tb in zip(items, tbs)]
    u0s = [_apply_pairs(tb, akv) for tb, akv in zip(tbs, akvs)]
    for it, w2, u0, a_rb, a_rk in zip(items, w2s, u0s, a_rbs, a_rks):
        d, rows, cols = it["d"], it["rows"], it["cols"]
        sc["w2"][d, rows, cols] = w2.astype(BF16)
        sc["rd"][d, rows, cols] = it["r_dec"].astype(BF16)
        sc["u0"][d, rows, cols] = u0
        sc["arb"][d, rows, cols] = a_rb.astype(BF16)
        sc["ark"][d, rows, cols] = a_rk.astype(BF16)
        sc["bh"][d, rows, cols] = it["b_end"].astype(BF16)
        sc["kh"][d, rows, cols] = it["k_end"].astype(BF16)


def _rwkv_recur(sc, st, i, *, seq_len, seqs):
    ch = RWKV_CHUNK
    n_chunks = seq_len // ch
    keep = _bd_mask(RWKV_HD)
    cs = []
    for j in range(seqs):
        for d in range(2):
            c = j * n_chunks + (i if d == 0 else n_chunks - 1 - i)
            rows = pl.ds(pl.multiple_of(c * ch, ch), ch)
            dec = sc["dec"][d, pl.ds(pl.multiple_of(c * 8, 8), 1), :]
            for pi in range(N_PAIRS):
                cols = slice(pi * PAIR_W, (pi + 1) * PAIR_W)
                cs.append(dict(d=d, j=j, p=pi, rows=rows, cols=cols, dec=dec[:, cols], s=st[d, j, pi]))
    sbs = [c["s"].astype(BF16) for c in cs]
    lss = [_mm_nt(jnp.concatenate([sc["w2"][c["d"], c["rows"], c["cols"]],
                                   sc["rd"][c["d"], c["rows"], c["cols"]]], axis=0), sb, "bf")
           for c, sb in zip(cs, sbs)]
    ubs = [(-(ls[:ch] + sc["u0"][c["d"], c["rows"], c["cols"]])).astype(BF16) for c, ls in zip(cs, lss)]
    vbs = [sc["v"][c["rows"], c["cols"]] for c in cs]
    ys = [ls[ch:] + _mm(jnp.concatenate([sc["arb"][c["d"], c["rows"], c["cols"]],
                                         sc["ark"][c["d"], c["rows"], c["cols"]]], axis=1),
                        jnp.concatenate([_bd(ub), _bd(vb)], axis=0), "bf")
          for c, ls, ub, vb in zip(cs, lss, ubs, vbs)]
    ups = [_mm_tn(jnp.concatenate([ub, vb], axis=0),
                  jnp.concatenate([sc["bh"][c["d"], c["rows"], c["cols"]],
                                   sc["kh"][c["d"], c["rows"], c["cols"]]], axis=0), "bf")
           for c, ub, vb in zip(cs, ubs, vbs)]
    for c, y, up in zip(cs, ys, ups):
        st[c["d"], c["j"], c["p"]] = c["s"] * c["dec"] + jnp.where(keep, up, 0.0)
        sc["ysum"][c["rows"], c["cols"]] += y


RWKV_SCRATCH = ("ysum", "bon", "gate", "v", "w2", "rd", "u0", "arb", "ark", "bh", "kh", "dec")


def _rwkv_kernel(*refs, seq_len, seqs, has_init):
    x_ref = refs[0]
    prm = refs[1:10]
    lnw_ref, lnb_ref = refs[10:12]
    pos = 12
    if has_init:
        s0f_ref, s0b_ref = refs[pos:pos + 2]
        pos += 2
    o_ref, sf_ref, sb_ref = refs[pos:pos + 3]
    sc = dict(zip(RWKV_SCRATCH, refs[pos + 3:]))
    st = refs[pos + 3 + len(RWKV_SCRATCH)]
    ch = RWKV_CHUNK
    hd = RWKV_HD
    n_chunks = seq_len // ch
    sc["ysum"][...] = jnp.zeros(sc["ysum"].shape, F32)

    def prepare(gi, carry):
        items = []
        for j in range(RWKV_GROUP):
            items += _rwkv_chunk_operands(x_ref, prm, sc, gi * RWKV_GROUP + j, seq_len=seq_len)
        _rwkv_solve(items, sc)
        return carry

    lax.fori_loop(0, seqs * n_chunks // RWKV_GROUP, prepare, 0)

    zero = jnp.zeros((hd, hd), F32)
    for d, s0_ref in enumerate((s0f_ref, s0b_ref) if has_init else (None, None)):
        for j in range(seqs):
            for pi in range(N_PAIRS):
                s_a = s0_ref[j, 2 * pi] if has_init else zero
                s_b = s0_ref[j, 2 * pi + 1] if has_init else zero
                st[d, j, pi] = jnp.concatenate([jnp.concatenate([s_a, zero], axis=1),
                                                jnp.concatenate([zero, s_b], axis=1)], axis=0)

    def recur(i, carry):
        _rwkv_recur(sc, st, i, seq_len=seq_len, seqs=seqs)
        return carry

    lax.fori_loop(0, n_chunks, recur, 0)
    for d, out_ref in enumerate((sf_ref, sb_ref)):
        for j in range(seqs):
            for pi in range(N_PAIRS):
                s = st[d, j, pi]
                out_ref[j, 2 * pi] = s[:hd, :hd]
                out_ref[j, 2 * pi + 1] = s[hd:, hd:]

    tile = RWKV_FINISH_ROWS

    def finish(i, carry):
        rows = pl.ds(pl.multiple_of(i * tile, tile), tile)
        pairs = [slice(pi * PAIR_W, (pi + 1) * PAIR_W) for pi in range(N_PAIRS)]
        ys = [sc["ysum"][rows, cols] for cols in pairs]
        cens = [y - _head_sums(y) * (1.0 / hd) for y in ys]
        vars_ = [_head_sums(cen * cen) * (1.0 / hd) for cen in cens]
        for cols, cen, var in zip(pairs, cens, vars_):
            yn = cen * lax.rsqrt(var + GN_EPS) * lnw_ref[:, cols] + lnb_ref[:, cols]
            o_ref[rows, cols] = (yn + sc["bon"][rows, cols]) * sc["gate"][rows, cols]
        return carry

    lax.fori_loop(0, seqs * seq_len // tile, finish, 0)


def _rwkv_mixer(x_rw, n_seq, seq_len, params, s0_f, s0_b):
    has_init = s0_f is not None
    (mu, w0, w_up, a0, a_up, g_up, k_k, k_a, r_k, ln_w, ln_b) = params
    row = lambda a: a.reshape(1, RWKV_W)
    args = [x_rw, mu, w0, w_up, a0, a_up, g_up, row(k_k), row(k_a), row(r_k), row(ln_w), row(ln_b)]

    def whole(a):
        nd = a.ndim
        return pl.BlockSpec(a.shape, lambda s, nd=nd: (0,) * nd)

    seqs = max(1, RWKV_BLOCK_ROWS // seq_len)
    rows = seqs * seq_len
    assert n_seq % seqs == 0 and (rows // RWKV_CHUNK) % RWKV_GROUP == 0 and seq_len & (seq_len - 1) == 0
    in_specs = [pl.BlockSpec((rows, RWKV_IN), lambda s: (s, 0))] + [whole(a) for a in args[1:]]
    state = pl.BlockSpec((seqs, None, RWKV_HEADS, RWKV_HD, RWKV_HD), lambda s: (s, 0, 0, 0, 0))
    if has_init:
        in_specs += [state, state]
        args += [s0_f, s0_b]
    st_shape = jax.ShapeDtypeStruct((n_seq, 1, RWKV_HEADS, RWKV_HD, RWKV_HD), F32)
    tok = lambda dt: pltpu.VMEM((rows, RWKV_W), dt)
    per_dir = lambda dt: pltpu.VMEM((2, rows, RWKV_W), dt)
    scratch = dict(ysum=tok(F32), bon=tok(F32), gate=tok(F32), v=tok(BF16), w2=per_dir(BF16), rd=per_dir(BF16),
                   u0=per_dir(F32), arb=per_dir(BF16), ark=per_dir(BF16), bh=per_dir(BF16), kh=per_dir(BF16),
                   dec=pltpu.VMEM((2, rows // RWKV_CHUNK * 8, RWKV_W), F32))
    return pl.pallas_call(
        functools.partial(_rwkv_kernel, seq_len=seq_len, seqs=seqs, has_init=has_init),
        grid=(n_seq // seqs,),
        in_specs=in_specs,
        out_specs=[pl.BlockSpec((rows, RWKV_W), lambda s: (s, 0)), state, state],
        out_shape=[jax.ShapeDtypeStruct((n_seq * seq_len, RWKV_W), F32), st_shape, st_shape],
        scratch_shapes=[scratch[name] for name in RWKV_SCRATCH]
        + [pltpu.VMEM((2, seqs, N_PAIRS, PAIR_W, PAIR_W), F32)],
        compiler_params=pltpu.CompilerParams(dimension_semantics=("parallel",), vmem_limit_bytes=VMEM_LIMIT),
        name="rwkv_mixer",
    )(*args)


def kernel(x_prompt, x_sample, state_gdn_fwd, state_gdn_bwd, cache_attn_k, cache_attn_v, state_rwkv_fwd, state_rwkv_bwd, c, c_ctx, mod_w, mod_b, norm_mix, norm_mlp, mlp_w1, mlp_w2, norm_final, ev_w_in, ev_w_out, gdn_conv, gdn_a_log, gdn_dt_bias, gdn_norm, sc_conv, od_w_in, od_w_out, attn_sink, rwkv_mu, rwkv_w0, rwkv_w_up, rwkv_a0, rwkv_a_up, rwkv_g_up, rwkv_k_k, rwkv_k_a, rwkv_r_k, rwkv_ln_w, rwkv_ln_b):
    bp, lp, _ = x_prompt.shape
    bs, ls, _ = x_sample.shape
    depth = mod_w.shape[0]
    c_rows = jnp.concatenate([c_ctx[None, :], c, jnp.zeros((MOD_ROWS - 1 - bs, D_MODEL), F32)], axis=0)
    mods, ev_w_t = _modulation(c_rows, mod_w, mod_b, ev_w_in)

    assert ls % TOKEN_TILE == 0 and (bp * lp) % TOKEN_TILE == 0
    groups = [
        dict(x=x_prompt.reshape(bp * lp, D_MODEL), n=bp, l=lp, latent=False,
             mod=lambda layer: _mod_spec(layer, 1, 0, 0)),
        dict(x=x_sample.reshape(bs * ls, D_MODEL), n=bs, l=ls, latent=True,
             mod=lambda layer: _mod_spec(layer, ls // TOKEN_TILE, 1, 1)),
    ]
    outs = {}
    for layer in range(depth):
        final = layer == depth - 1
        mlp_cast = ((mlp_w1, layer), (mlp_w2, layer))
        if layer % 2 == 0:
            e = layer // 2
            alog_vec = jnp.zeros((1, 128), F32).at[0, 2 * GDN_HEADS:4 * GDN_HEADS].set(gdn_a_log[e].reshape(-1))
            dtb_vec = jnp.zeros((1, 128), F32).at[0, 2 * GDN_HEADS:4 * GDN_HEADS].set(gdn_dt_bias[e].reshape(-1))
            for grp in groups:
                qkv, gz, sc, gates = _inproj_even(grp["x"], mods, norm_mix[layer], ev_w_t[e], gdn_conv[e],
                                                  sc_conv[e], alog_vec, dtb_vec, grp["l"], grp["mod"](layer))
                s0 = (state_gdn_fwd[:, e:e + 1], state_gdn_bwd[:, e:e + 1]) if grp["latent"] else (None, None)
                if grp["latent"]:
                    cast = ((od_w_in, layer // 2), (od_w_out, layer // 2)) if layer + 1 < depth else ()
                    o, _, _, *next_w = _gdn_mixer(qkv, gz, gates, grp["n"], grp["l"], gdn_norm[e], *s0, cast=cast)
                else:
                    o, s_f, s_b, w_out, w1, w2 = _gdn_mixer(qkv, gz, gates, grp["n"], grp["l"], gdn_norm[e], *s0,
                                                            cast=((ev_w_out, e),) + mlp_cast)
                    outs.setdefault("gdn_f", []).append(s_f)
                    outs.setdefault("gdn_b", []).append(s_b)
                grp["x"] = _outproj_mlp(o, sc, grp["x"], mods, norm_mlp[layer], w_out, w1, w2, norm_final,
                                        grp["mod"](layer), final)
        else:
            o_ = layer // 2
            w_in, w_out = [next_w[0]], next_w[1]
            rw = (rwkv_mu[o_], rwkv_w0[o_], rwkv_w_up[o_], rwkv_a0[o_], rwkv_a_up[o_], rwkv_g_up[o_],
                  rwkv_k_k[o_], rwkv_k_a[o_], rwkv_r_k[o_].reshape(-1), rwkv_ln_w[o_], rwkv_ln_b[o_])
            for grp in groups:
                p_att, x_rw = _inproj(grp["x"], mods, norm_mix[layer], w_in, (ATT_W, RWKV_IN), grp["mod"](layer))
                if grp["latent"]:
                    att = _attn_latent(p_att, attn_sink[o_], cache_attn_k, cache_attn_v, o_, grp["n"], grp["l"])
                    rwo, _, _ = _rwkv_mixer(x_rw, grp["n"], grp["l"], rw,
                                            state_rwkv_fwd[:, o_:o_ + 1], state_rwkv_bwd[:, o_:o_ + 1])
                else:
                    att, kc, vc, w1, w2 = _attn_context(p_att, attn_sink[o_], grp["n"], grp["l"], cast=mlp_cast)
                    rwo, s_f, s_b = _rwkv_mixer(x_rw, grp["n"], grp["l"], rw, None, None)
                    outs.setdefault("att_k", []).append(kc)
                    outs.setdefault("att_v", []).append(vc)
                    outs.setdefault("rw_f", []).append(s_f)
                    outs.setdefault("rw_b", []).append(s_b)
                grp["x"] = _outproj_mlp(att, rwo, grp["x"], mods, norm_mlp[layer], w_out, w1, w2, norm_final,
                                        grp["mod"](layer), final)
    cat = lambda key: jnp.concatenate(outs[key], axis=1)
    return (groups[0]["x"].reshape(bp, lp, D_MODEL), groups[1]["x"].reshape(bs, ls, D_MODEL),
            cat("gdn_f"), cat("gdn_b"), cat("att_k"), cat("att_v"), cat("rw_f"), cat("rw_b"))
```

```python
import functools

import jax
import jax.numpy as jnp
import numpy as np
from jax import lax
from jax.experimental import pallas as pl
from jax.experimental.pallas import tpu as pltpu

F32 = jnp.float32
BF16 = jnp.bfloat16

D_MODEL = 1024
N_MOD = 6
D_FF = 4 * D_MODEL
NORM_EPS = 1e-6
TOKEN_TILE = 512
HALO = 8
MOD_ROWS = 8

GDN_HEADS = 4
GDN_D = 128
GDN_CHUNK = 128
GDN_SOLVE_CHAINS = 32
GDN_RECUR_CHAINS = 8
GDN_BLOCK_ROWS = 1024
GDN_QKV_W = 3 * GDN_HEADS * GDN_D
SC_WIDTH = 512
MOD_COL_BLOCKS = 4
EV_CAST_ROWS = 720

ATT_HEADS = 8
ATT_KV_HEADS = 2
ATT_GROUP = ATT_HEADS // ATT_KV_HEADS
ATT_HD = 64
ATT_Q_W = ATT_HEADS * ATT_HD
ATT_KV_W = ATT_KV_HEADS * ATT_HD
ATT_W = ATT_Q_W + 2 * ATT_KV_W
ATT_CTX_SEQS = 2
WINDOW = 128
ATT_BLOCK = 128
GRID_W = 64
ROPE_BASE = 10000.0
NEG_INF = -1e30

RWKV_HEADS = 8
RWKV_HD = 64
RWKV_W = RWKV_HEADS * RWKV_HD
RWKV_LORA = 64
RWKV_IN = 3 * RWKV_W + 3 * 2 * RWKV_LORA
RWKV_CHUNK = 64
RWKV_GROUP = 4
RWKV_FINISH_ROWS = 256
DECAY_SCALE = float(np.exp(-0.5))
LOG2E = float(np.log2(np.e))
RWKV_BLOCK_ROWS = 1024
PAIR_W = 2 * RWKV_HD
N_PAIRS = RWKV_W // PAIR_W
GN_EPS = 64e-5

VMEM_LIMIT = 56 * 1024 * 1024

RWKV_PREC = dict(lora="bf", gate="bf", cumsum="x2r")


def _split_bf16(a):
    hi = a.astype(BF16)
    return hi, (a - hi.astype(F32)).astype(BF16)


def _dot(a, b, dims, prec):
    dn = (dims, ((), ()))
    one = lambda x, y: lax.dot_general(x, y, dn, preferred_element_type=F32)
    if prec == "x2r":
        ah = a.astype(BF16)
        bh, bl = _split_bf16(b)
        return one(ah, bh) + one(ah, bl)
    assert prec == "bf", prec
    return one(a.astype(BF16), b.astype(BF16))


def _mm(a, b, prec):
    return _dot(a, b, ((1,), (0,)), prec)


def _mm_nt(a, b, prec):
    return _dot(a, b, ((1,), (1,)), prec)


def _mm_tn(a, b, prec):
    return _dot(a, b, ((0,), (0,)), prec)


def _silu(x):
    h = 0.5 * x
    return h + h * jnp.tanh(h)


def _softplus(x):
    return jnp.maximum(x, 0.0) + jnp.log1p(jnp.exp(-jnp.abs(x)))


def _rms(x, w):
    return x * lax.rsqrt(jnp.mean(x * x, axis=-1, keepdims=True) + NORM_EPS) * w


def _tri_masks(n, rev):
    r = lax.broadcasted_iota(jnp.int32, (n, n), 0)
    c = lax.broadcasted_iota(jnp.int32, (n, n), 1)
    if rev:
        return r <= c, r < c
    return r >= c, r > c


def _unit_tri_inv_many(mats, prec):
    assert prec == "bf"
    n = mats[0].shape[0]
    r = lax.broadcasted_iota(jnp.int32, (n, n), 0)
    c = lax.broadcasted_iota(jnp.int32, (n, n), 1)
    eye = jnp.where(r == c, 1.0, 0.0)
    within2 = jnp.where((r >> 1) == (c >> 1), 1.0, 0.0)
    ts = [eye - a * within2 for a in mats]
    abs_ = [a.astype(BF16) for a in mats]
    shift = 1
    while (1 << shift) < n:
        join = jnp.where(((r >> (shift + 1)) == (c >> (shift + 1))) & ((r >> shift) != (c >> shift)),
                         1.0, 0.0).astype(BF16)
        tbs = [t.astype(BF16) for t in ts]
        inner = [_mm(ab * join, tb, prec) for ab, tb in zip(abs_, tbs)]
        ts = [t - _mm(tb, w, prec) for t, tb, w in zip(ts, tbs, inner)]
        shift += 1
    return ts


def _shifted_rows(x, prev_row, next_row):
    n = x.shape[0]
    row = lax.broadcasted_iota(jnp.int32, x.shape, 0)
    xp = jnp.where(row == 0, prev_row, pltpu.roll(x, 1, 0))
    xn = jnp.where(row == n - 1, next_row, pltpu.roll(x, n - 1, 0))
    return xp, xn


def _mod_kernel(c_ref, w_ref, b_ref, evt_ref, o_ref, wt_ref):
    s = _silu(c_ref[...])
    o_ref[...] = _mm(s, w_ref[...], prec="bf") + b_ref[...]
    wt_ref[...] = evt_ref[...].astype(BF16)


def _modulation(c_rows, mod_w, mod_b, ev_w_in):
    depth = mod_w.shape[0]
    n_even, _, ev_cols = ev_w_in.shape
    nblk = MOD_COL_BLOCKS
    width = N_MOD * D_MODEL // nblk
    steps = depth * nblk
    rows = n_even * ev_cols
    n_side = rows // EV_CAST_ROWS
    assert rows % EV_CAST_ROWS == 0 and n_side <= steps and EV_CAST_ROWS % 16 == 0 and width % 128 == 0
    side = pl.BlockSpec((EV_CAST_ROWS, D_MODEL), lambda l, j: (jnp.minimum(l * nblk + j, n_side - 1), 0))
    out, w_t = pl.pallas_call(
        _mod_kernel,
        grid=(depth, nblk),
        in_specs=[
            pl.BlockSpec((MOD_ROWS, D_MODEL), lambda l, j: (0, 0)),
            pl.BlockSpec((None, D_MODEL, width), lambda l, j: (l, 0, j)),
            pl.BlockSpec((None, 1, width), lambda l, j: (l, 0, j)),
            side,
        ],
        out_specs=[pl.BlockSpec((None, MOD_ROWS, width), lambda l, j: (l, 0, j)), side],
        out_shape=[jax.ShapeDtypeStruct((depth, MOD_ROWS, N_MOD * D_MODEL), F32),
                   jax.ShapeDtypeStruct((rows, D_MODEL), BF16)],
        compiler_params=pltpu.CompilerParams(dimension_semantics=("arbitrary", "arbitrary"),
                                             vmem_limit_bytes=VMEM_LIMIT),
        name="modulation",
    )(c_rows, mod_w, mod_b.reshape(depth, 1, N_MOD * D_MODEL), jnp.swapaxes(ev_w_in, 1, 2).reshape(rows, D_MODEL))
    return out.reshape(depth, MOD_ROWS, N_MOD, D_MODEL), w_t.reshape(n_even, ev_cols, D_MODEL)


def _mod_spec(layer, tiles_per_seq, row_base, row_step):
    return pl.BlockSpec((None, None, N_MOD, D_MODEL),
                        lambda i: (layer, row_base + (i // tiles_per_seq) * row_step, 0, 0))


def _inproj_kernel(*refs, n_w):
    x_ref, mod_ref, nw_ref = refs[:3]
    w_refs = refs[3:3 + n_w]
    o_refs = refs[3 + n_w:]
    h = _rms(x_ref[...], nw_ref[...])
    h = (h * (1.0 + mod_ref[1:2, :]) + mod_ref[0:1, :]).astype(BF16)
    pieces = [_mm(h, w_ref[...], prec="bf") for w_ref in w_refs]
    y = pieces[0] if n_w == 1 else jnp.concatenate(pieces, axis=1)
    off = 0
    for o_ref in o_refs:
        n = o_ref.shape[-1]
        o_ref[...] = y[:, off:off + n]
        off += n


def _inproj(x, mods, norm_w, ws_bf16, splits, mod_spec):
    t = x.shape[0]
    assert sum(w.shape[1] for w in ws_bf16) == sum(splits) and all(w.shape[1] % 128 == 0 for w in ws_bf16)
    return pl.pallas_call(
        functools.partial(_inproj_kernel, n_w=len(ws_bf16)),
        grid=(t // TOKEN_TILE,),
        in_specs=[
            pl.BlockSpec((TOKEN_TILE, D_MODEL), lambda i: (i, 0)),
            mod_spec,
            pl.BlockSpec((1, D_MODEL), lambda i: (0, 0)),
        ] + [pl.BlockSpec(w.shape, lambda i: (0, 0)) for w in ws_bf16],
        out_specs=[pl.BlockSpec((TOKEN_TILE, n), lambda i: (i, 0)) for n in splits],
        out_shape=[jax.ShapeDtypeStruct((t, n), F32) for n in splits],
        compiler_params=pltpu.CompilerParams(dimension_semantics=("parallel",), vmem_limit_bytes=VMEM_LIMIT),
        name="inproj",
    )(x, mods, norm_w.reshape(1, D_MODEL), *ws_bf16)


def _inproj_even_kernel(xp_ref, x_ref, xn_ref, mod_ref, nw_ref, wt_ref, cqkv_ref, csc_ref,
                        alog_ref, dtb_ref, qkv_ref, gz_ref, sc_ref, gate_ref, *, seq_len):
    tile = x_ref.shape[0]
    proj = lambda row0, width: _mm_nt(h, wt_ref[row0:row0 + width, :], "bf")
    x = jnp.concatenate([xp_ref[...], x_ref[...], xn_ref[...]], axis=0)
    h = _rms(x, nw_ref[...])
    h = (h * (1.0 + mod_ref[1:2, :]) + mod_ref[0:1, :]).astype(BF16)
    n = tile + 2 * HALO
    first = pl.program_id(0) * tile - HALO
    pos = (first + lax.broadcasted_iota(jnp.int32, (n, 1), 0)) & (seq_len - 1)
    at_start = pos == 0
    at_end = pos == seq_len - 1

    def conv3(v, c_ref):
        vp = jnp.where(at_start, 0.0, pltpu.roll(v, 1, 0))
        vn = jnp.where(at_end, 0.0, pltpu.roll(v, n - 1, 0))
        return vp * c_ref[0:1, :] + v * c_ref[1:2, :] + vn * c_ref[2:3, :]

    keep = slice(HALO, HALO + tile)
    wide = 2 * GDN_D
    gate_row = GDN_QKV_W + GDN_HEADS * GDN_D
    sc_row = gate_row + 4 * GDN_HEADS
    for j in range(GDN_QKV_W // wide):
        cols = slice(j * wide, (j + 1) * wide)
        act = _silu(conv3(proj(j * wide, wide), cqkv_ref.at[:, cols]))[keep]
        for i in range(2):
            part = act[:, i * GDN_D:(i + 1) * GDN_D]
            if j < 2 * GDN_HEADS // 2:
                part = part * lax.rsqrt(jnp.sum(part * part, axis=-1, keepdims=True) + 1e-6)
                if j < GDN_HEADS // 2:
                    part = part * (GDN_D ** -0.5)
            qkv_ref[:, j * wide + i * GDN_D:j * wide + (i + 1) * GDN_D] = part.astype(BF16)
    for j in range(GDN_HEADS * GDN_D // wide):
        cols = slice(j * wide, (j + 1) * wide)
        gz_ref[:, cols] = _silu(proj(GDN_QKV_W + j * wide, wide)[keep]).astype(BF16)
    for j in range(SC_WIDTH // wide):
        sc_b, sc_c, sc_h = [proj(sc_row + i * SC_WIDTH + j * wide, wide) for i in range(3)]
        cols = slice(j * wide, (j + 1) * wide)
        sc_ref[:, cols] = (sc_b * conv3(sc_c * sc_h, csc_ref.at[:, cols]))[keep].astype(BF16)
    g = proj(gate_row, 128)[keep]
    lane = lax.broadcasted_iota(jnp.int32, g.shape, 1)
    gate_ref[...] = jnp.where(lane < 2 * GDN_HEADS, jax.nn.sigmoid(g),
                              (-LOG2E * jnp.exp(alog_ref[...])) * _softplus(g + dtb_ref[...]))


def _inproj_even(x, mods, norm_w, w_t, conv_w, sc_conv_w, alog_vec, dtb_vec, seq_len, mod_spec):
    t = x.shape[0]
    tile = TOKEN_TILE
    assert seq_len & (seq_len - 1) == 0 and t % tile == 0
    per = tile // HALO
    last = t // HALO - 1
    const = lambda i: (0, 0)
    tok = lambda n: pl.BlockSpec((tile, n), lambda i: (i, 0))
    return pl.pallas_call(
        functools.partial(_inproj_even_kernel, seq_len=seq_len),
        grid=(t // tile,),
        in_specs=[
            pl.BlockSpec((HALO, D_MODEL), lambda i: (jnp.maximum(i * per - 1, 0), 0)),
            pl.BlockSpec((tile, D_MODEL), lambda i: (i, 0)),
            pl.BlockSpec((HALO, D_MODEL), lambda i: (jnp.minimum((i + 1) * per, last), 0)),
            mod_spec,
            pl.BlockSpec((1, D_MODEL), const),
            pl.BlockSpec(w_t.shape, const),
            pl.BlockSpec(conv_w.shape, const), pl.BlockSpec(sc_conv_w.shape, const),
            pl.BlockSpec((1, 128), const), pl.BlockSpec((1, 128), const),
        ],
        out_specs=[tok(GDN_QKV_W), tok(GDN_HEADS * GDN_D), tok(SC_WIDTH), tok(128)],
        out_shape=[jax.ShapeDtypeStruct((t, GDN_QKV_W), BF16), jax.ShapeDtypeStruct((t, GDN_HEADS * GDN_D), BF16),
                   jax.ShapeDtypeStruct((t, SC_WIDTH), BF16), jax.ShapeDtypeStruct((t, 128), F32)],
        compiler_params=pltpu.CompilerParams(dimension_semantics=("parallel",), vmem_limit_bytes=VMEM_LIMIT),
        name="inproj_even",
    )(x, x, x, mods, norm_w.reshape(1, D_MODEL), w_t, conv_w, sc_conv_w, alog_vec, dtb_vec)


def _mlp_kernel(*refs, final, n_first):
    groups = (refs[0:3], refs[3:6])
    mod_ref, nw_ref, woa_ref, wob_ref, w1_ref, w2_ref, nf_ref = refs[6:13]
    o_refs = refs[13:15]

    def tile(a_ref, b_ref, x_ref, o_ref):
        y = _mm(a_ref[...], woa_ref[...], prec="bf") + _mm(b_ref[...], wob_ref[...], prec="bf")
        x1 = x_ref[...] + mod_ref[2:3, :] * y
        h = _rms(x1, nw_ref[...])
        h = (h * (1.0 + mod_ref[4:5, :]) + mod_ref[3:4, :]).astype(BF16)
        acc = jnp.zeros(x1.shape, F32)
        for j in range(D_FF // D_MODEL):
            cols = slice(j * D_MODEL, (j + 1) * D_MODEL)
            u = jnp.maximum(_mm(h, w1_ref[:, cols], prec="bf"), 0.0)
            acc = acc + _mm(u * u, w2_ref[cols, :], prec="bf")
        x2 = x1 + mod_ref[5:6, :] * acc
        if final:
            x2 = _rms(x2, nf_ref[...])
        o_ref[...] = x2

    step = pl.program_id(0)

    @pl.when(step < n_first)
    def _():
        tile(*groups[0], o_refs[0])

    @pl.when(step >= n_first)
    def _():
        tile(*groups[1], o_refs[1])


def _outproj_mlp(abx_first, abx_second, mods, norm_w, w_out, w1, w2, norm_final, layer, tiles_per_seq, final):
    t_first, t_second = abx_first[2].shape[0], abx_second[2].shape[0]
    assert t_first % TOKEN_TILE == 0 and t_second % TOKEN_TILE == 0
    n_first, n_second = t_first // TOKEN_TILE, t_second // TOKEN_TILE
    half = abx_first[0].shape[1]
    const = lambda i: (0, 0)
    first = lambda i: (jnp.minimum(i, n_first - 1), 0)
    second = lambda i: (jnp.maximum(i - n_first, 0), 0)
    tok = lambda width, index: pl.BlockSpec((TOKEN_TILE, width), index)
    mod_spec = pl.BlockSpec(
        (None, None, N_MOD, D_MODEL),
        lambda i: (layer, jnp.where(i < n_first, 0, 1 + jnp.maximum(i - n_first, 0) // tiles_per_seq), 0, 0))
    return pl.pallas_call(
        functools.partial(_mlp_kernel, final=final, n_first=n_first),
        grid=(n_first + n_second,),
        in_specs=[
            tok(half, first), tok(half, first), tok(D_MODEL, first),
            tok(half, second), tok(half, second), tok(D_MODEL, second),
            mod_spec,
            pl.BlockSpec((1, D_MODEL), const),
            pl.BlockSpec((half, D_MODEL), const),
            pl.BlockSpec((half, D_MODEL), lambda i: (1, 0)),
            pl.BlockSpec((D_MODEL, D_FF), const),
            pl.BlockSpec((D_FF, D_MODEL), const),
            pl.BlockSpec((1, D_MODEL), const),
        ],
        out_specs=[tok(D_MODEL, first), tok(D_MODEL, second)],
        out_shape=[jax.ShapeDtypeStruct((t_first, D_MODEL), F32), jax.ShapeDtypeStruct((t_second, D_MODEL), F32)],
        compiler_params=pltpu.CompilerParams(dimension_semantics=("arbitrary",), vmem_limit_bytes=VMEM_LIMIT),
        name="outproj_mlp",
    )(*abx_first, *abx_second, mods, norm_w.reshape(1, D_MODEL), w_out, w_out, w1, w2,
      norm_final.reshape(1, D_MODEL))


def _gdn_decay_terms(g, rev):
    c = g.shape[0]
    incl = _tri_masks(c, rev)[0]
    before_col = _tri_masks(c, not rev)[0]
    eye = lax.broadcasted_iota(jnp.int32, (c, c), 0) == lax.broadcasted_iota(jnp.int32, (c, c), 1)
    gc_row = jnp.sum(jnp.where(before_col, jnp.broadcast_to(g, (c, c)), 0.0), axis=0, keepdims=True)
    gc_col = jnp.sum(jnp.where(eye, jnp.broadcast_to(gc_row, (c, c)), 0.0), axis=1, keepdims=True)
    decay = jnp.where(incl, jnp.exp2(jnp.where(incl, gc_col - gc_row, 0.0)), 0.0)
    g_tot = jnp.sum(g, axis=0, keepdims=True)
    return decay, jnp.exp2(gc_col), jnp.exp2(g_tot - gc_col), jnp.exp2(g_tot)


def _cast_side_job(arrays_layers, n_steps, step_index):
    in_specs, out_specs, out_shapes = [], [], []
    for arr, layer in arrays_layers:
        _, r, c = arr.shape
        blk = r // n_steps
        assert r % n_steps == 0 and blk % 16 == 0
        in_specs.append(pl.BlockSpec((None, blk, c), lambda *g, layer=layer: (layer, step_index(*g), 0)))
        out_specs.append(pl.BlockSpec((blk, c), lambda *g: (step_index(*g), 0)))
        out_shapes.append(jax.ShapeDtypeStruct((r, c), BF16))
    return in_specs, out_specs, out_shapes


def _gdn_kernel(*refs, seq_len, seqs, hps, group, has_init, n_cast):
    q_ref, k_ref, v_ref, gz_ref, gate_ref, gn_ref = refs[:6]
    pos = 6
    if has_init:
        s0f_ref, s0b_ref = refs[pos:pos + 2]
        pos += 2
    cast_in = refs[pos:pos + n_cast]
    pos += n_cast
    o_ref, sf_ref, sb_ref = refs[pos:pos + 3]
    cast_out = refs[pos + 3:pos + 3 + n_cast]
    pos += n_cast
    for src, dst in zip(cast_in, cast_out):
        dst[...] = src[...].astype(BF16)
    osum, u_s, w_s, qd_s, kd_s, in_s, ge_s, st = refs[pos + 3:]
    head0 = pl.program_id(1) * hps
    ch = GDN_CHUNK
    n_chunks = seq_len // ch
    osum[...] = jnp.zeros(osum.shape, F32)
    hcols = [slice(hh * GDN_D, (hh + 1) * GDN_D) for hh in range(hps)]

    lane = lax.broadcasted_iota(jnp.int32, (ch, 128), 1)

    def pick(rows, col):
        return jnp.sum(jnp.where(lane == col, gate_ref[rows, :], 0.0), axis=1, keepdims=True)

    def solve_group(gi, carry):
        items = []
        for j in range(group):
            c = gi * group + j
            rows = pl.ds(pl.multiple_of(c * ch, ch), ch)
            for hh in range(hps):
                items.append(dict(c=c, hh=hh, rows=rows, q=q_ref[rows, hcols[hh]].astype(F32),
                                  k=k_ref[rows, hcols[hh]].astype(F32), v=v_ref[rows, hcols[hh]].astype(F32)))
        kks = [_mm_nt(it["k"], it["k"], "bf") for it in items]
        qks = [_mm_nt(it["q"], it["k"], "bf") for it in items]
        subs = []
        for it, kk, qk in zip(items, kks, qks):
            for d in range(2):
                beta = pick(it["rows"], d * GDN_HEADS + head0 + it["hh"])
                g = pick(it["rows"], 2 * GDN_HEADS + d * GDN_HEADS + head0 + it["hh"])
                decay, e_gc, e_rest, e_tot = _gdn_decay_terms(g, rev=(d == 1))
                strict = _tri_masks(ch, d == 1)[1]
                subs.append(dict(
                    d=d, c=it["c"], hh=it["hh"], rows=it["rows"],
                    a=jnp.where(strict, kk * beta * decay, 0.0),
                    rhs=jnp.concatenate([it["v"] * beta, it["k"] * (beta * e_gc)], axis=1),
                    intra=qk * decay, qd=it["q"] * e_gc, kd=it["k"] * e_rest, ge=e_tot))
        ts = _unit_tri_inv_many([s["a"] for s in subs], "bf")
        uws = [_mm(t, s["rhs"], "bf") for t, s in zip(ts, subs)]
        for s, uw in zip(subs, uws):
            d, hh, rows = s["d"], s["hh"], s["rows"]
            u_s[d, hh, rows, :] = uw[:, :GDN_D]
            w_s[d, hh, rows, :] = uw[:, GDN_D:].astype(BF16)
            qd_s[d, hh, rows, :] = s["qd"].astype(BF16)
            kd_s[d, hh, rows, :] = s["kd"].astype(BF16)
            in_s[d, hh, rows, :] = s["intra"].astype(BF16)
            ge_s[d, hh, pl.ds(pl.multiple_of(s["c"] * 8, 8), 8), :] = jnp.broadcast_to(s["ge"], (8, 128))
        return carry

    lax.fori_loop(0, seqs * n_chunks // group, solve_group, 0)

    for hh in range(hps):
        for j in range(seqs):
            st[0, hh, j] = s0f_ref[j, hh] if has_init else jnp.zeros((GDN_D, GDN_D), F32)
            st[1, hh, j] = s0b_ref[j, hh] if has_init else jnp.zeros((GDN_D, GDN_D), F32)

    def recur(i, carry):
        cs = []
        for hh in range(hps):
            for j in range(seqs):
                for d in range(2):
                    c = j * n_chunks + (i if d == 0 else n_chunks - 1 - i)
                    cs.append(dict(d=d, hh=hh, j=j, rows=pl.ds(pl.multiple_of(c * ch, ch), ch),
                                   ge=ge_s[d, hh, pl.ds(pl.multiple_of(c * 8, 8), 1), :], s=st[d, hh, j]))
        sbs = [c["s"].astype(BF16) for c in cs]
        wss = [_mm(w_s[c["d"], c["hh"], c["rows"], :], sb, "bf") for c, sb in zip(cs, sbs)]
        qss = [_mm(qd_s[c["d"], c["hh"], c["rows"], :], sb, "bf") for c, sb in zip(cs, sbs)]
        ebs = [(u_s[c["d"], c["hh"], c["rows"], :] - ws).astype(BF16) for c, ws in zip(cs, wss)]
        outs = [qs_ + _mm(in_s[c["d"], c["hh"], c["rows"], :], eb, "bf") for c, qs_, eb in zip(cs, qss, ebs)]
        s_news = [c["s"] * c["ge"] + _mm_tn(kd_s[c["d"], c["hh"], c["rows"], :], eb, "bf")
                  for c, eb in zip(cs, ebs)]
        for c, o, s_new in zip(cs, outs, s_news):
            st[c["d"], c["hh"], c["j"]] = s_new
            osum[c["rows"], hcols[c["hh"]]] += o
        return carry

    lax.fori_loop(0, n_chunks, recur, 0)
    for hh in range(hps):
        for j in range(seqs):
            sf_ref[j, hh] = st[0, hh, j]
            sb_ref[j, hh] = st[1, hh, j]
        o_ref[:, hcols[hh]] = (_rms(osum[:, hcols[hh]], gn_ref[...]) * gz_ref[:, hcols[hh]].astype(F32))


def _gdn_mixer(qkv, gz, gates, n_seq, seq_len, gdn_norm, s0_f, s0_b, cast=()):
    has_init = s0_f is not None
    hd = GDN_HEADS
    seqs = max(1, GDN_BLOCK_ROWS // seq_len)
    rows = seqs * seq_len
    hps = min(hd, max(2, GDN_RECUR_CHAINS // (2 * seqs)))
    group = GDN_SOLVE_CHAINS // (2 * hps)
    hgroups = hd // hps
    assert n_seq % seqs == 0 and (rows // GDN_CHUNK) % group == 0 and hd % hps == 0

    def col(block):
        return pl.BlockSpec((rows, hps * GDN_D), lambda s, h, b=block: (s, b * hgroups + h))

    state = pl.BlockSpec((seqs, None, hps, GDN_D, GDN_D), lambda s, h: (s, 0, h, 0, 0))
    in_specs = [col(0), col(1), col(2), col(0), pl.BlockSpec((rows, 128), lambda s, h: (s, 0)),
                pl.BlockSpec((1, 128), lambda s, h: (0, 0))]
    args = [qkv, qkv, qkv, gz, gates, gdn_norm.reshape(1, 128)]
    if has_init:
        in_specs += [state, state]
        args += [s0_f, s0_b]
    n_steps = (n_seq // seqs) * hgroups
    cast_in, cast_out, cast_shapes = _cast_side_job(cast, n_steps, lambda s, h: s * hgroups + h)
    in_specs += cast_in
    args += [arr for arr, _ in cast]
    t = n_seq * seq_len
    scratch = ([pltpu.VMEM((rows, hps * GDN_D), F32)]
               + [pltpu.VMEM((2, hps, rows, GDN_D), F32)]
               + [pltpu.VMEM((2, hps, rows, GDN_D), BF16) for _ in range(3)]
               + [pltpu.VMEM((2, hps, rows, GDN_CHUNK), BF16),
                  pltpu.VMEM((2, hps, rows // GDN_CHUNK * 8, 128), F32),
                  pltpu.VMEM((2, hps, seqs, GDN_D, GDN_D), F32)])
    return pl.pallas_call(
        functools.partial(_gdn_kernel, seq_len=seq_len, seqs=seqs, hps=hps, group=group, has_init=has_init,
                          n_cast=len(cast)),
        grid=(n_seq // seqs, hgroups),
        in_specs=in_specs,
        out_specs=[pl.BlockSpec((rows, hps * GDN_D), lambda s, h: (s, h)), state, state] + cast_out,
        out_shape=[jax.ShapeDtypeStruct((t, hd * GDN_D), F32),
                   jax.ShapeDtypeStruct((n_seq, 1, hd, GDN_D, GDN_D), F32),
                   jax.ShapeDtypeStruct((n_seq, 1, hd, GDN_D, GDN_D), F32)] + cast_shapes,
        scratch_shapes=scratch,
        compiler_params=pltpu.CompilerParams(dimension_semantics=("parallel", "parallel"),
                                             vmem_limit_bytes=VMEM_LIMIT),
        name="gdn_mixer",
    )(*args)


def _softmax_pv(scores, values, sink):
    m = sink
    for s in scores:
        m = jnp.maximum(m, jnp.max(s, axis=-1, keepdims=True))
    den = jnp.exp2(sink - m)
    acc = None
    for s, v in zip(scores, values):
        e = jnp.exp2(s - m)
        den = den + jnp.sum(e, axis=-1, keepdims=True)
        pv = _mm(e, v, prec="bf")
        acc = pv if acc is None else acc + pv
    return acc / den


def _group_sinks(sink_ref, j, rows):
    assert rows & (rows - 1) == 0
    grp = lax.broadcasted_iota(jnp.int32, (ATT_GROUP * rows, 1), 0) >> (rows.bit_length() - 1)
    col = jnp.full(grp.shape, sink_ref[j * ATT_GROUP], F32)
    for gi in range(1, ATT_GROUP):
        col = jnp.where(grp == gi, sink_ref[j * ATT_GROUP + gi], col)
    return col * LOG2E


def _store_group(o_ref, j, o, rows):
    for gi in range(ATT_GROUP):
        hh = j * ATT_GROUP + gi
        o_ref[:, hh * ATT_HD:(hh + 1) * ATT_HD] = o[gi * rows:(gi + 1) * rows]


def _attn_ctx_kernel(*refs, n_cast, seqs):
    sink_ref, p_ref = refs[:2]
    cast_in = refs[2:2 + n_cast]
    o_ref, kc_ref, vc_ref = refs[2 + n_cast:5 + n_cast]
    for src, dst in zip(cast_in, refs[5 + n_cast:]):
        dst[...] = src[...].astype(BF16)
    scale = ATT_HD ** -0.5 * LOG2E
    rows = p_ref.shape[0] // seqs
    chains = []
    for sq in range(seqs):
        tok = slice(sq * rows, (sq + 1) * rows)
        for j in range(ATT_KV_HEADS):
            k = p_ref[tok, ATT_Q_W + j * ATT_HD:ATT_Q_W + (j + 1) * ATT_HD]
            v = p_ref[tok, ATT_Q_W + ATT_KV_W + j * ATT_HD:ATT_Q_W + ATT_KV_W + (j + 1) * ATT_HD]
            kc_ref[sq, j] = k
            vc_ref[sq, j] = v
            q = jnp.concatenate([p_ref[tok, hh * ATT_HD:(hh + 1) * ATT_HD]
                                 for hh in range(j * ATT_GROUP, (j + 1) * ATT_GROUP)], axis=0) * scale
            chains.append((tok, j, q, k, v))
    scores = [_mm_nt(q, k, prec="bf") for _, _, q, k, _ in chains]
    outs = [_softmax_pv([s], [v], _group_sinks(sink_ref, j, rows)) for s, (_, j, _, _, v) in zip(scores, chains)]
    for o, (tok, j, _, _, _) in zip(outs, chains):
        _store_group(o_ref.at[tok, :], j, o, rows)


def _attn_context(proj_att, sink, n_seq, seq_len, cast=()):
    seqs = ATT_CTX_SEQS
    assert n_seq % seqs == 0
    kv = pl.BlockSpec((seqs, None, ATT_KV_HEADS, seq_len, ATT_HD), lambda b: (b, 0, 0, 0, 0))
    cast_in, cast_out, cast_shapes = _cast_side_job(cast, n_seq // seqs, lambda b: b)
    return pl.pallas_call(
        functools.partial(_attn_ctx_kernel, n_cast=len(cast), seqs=seqs),
        grid=(n_seq // seqs,),
        in_specs=[pl.BlockSpec(memory_space=pltpu.SMEM),
                  pl.BlockSpec((seqs * seq_len, ATT_W), lambda b: (b, 0))] + cast_in,
        out_specs=[pl.BlockSpec((seqs * seq_len, ATT_Q_W), lambda b: (b, 0)), kv, kv] + cast_out,
        out_shape=[jax.ShapeDtypeStruct((n_seq * seq_len, ATT_Q_W), F32),
                   jax.ShapeDtypeStruct((n_seq, 1, ATT_KV_HEADS, seq_len, ATT_HD), F32),
                   jax.ShapeDtypeStruct((n_seq, 1, ATT_KV_HEADS, seq_len, ATT_HD), F32)] + cast_shapes,
        compiler_params=pltpu.CompilerParams(dimension_semantics=("parallel",), vmem_limit_bytes=VMEM_LIMIT),
        name="attn_context",
    )(sink, proj_att, *[arr for arr, _ in cast])


def _rope_tables(seq_len):
    pos = np.arange(seq_len)
    half = ATT_HD // 2
    inv = ROPE_BASE ** (-np.arange(0, half, 2, dtype=np.float32) / half)
    ang_r = (pos // GRID_W).astype(np.float32)[:, None] * inv
    ang_c = (pos % GRID_W).astype(np.float32)[:, None] * inv
    cos = np.concatenate([np.cos(ang_r), np.cos(ang_r), np.cos(ang_c), np.cos(ang_c)], axis=1)
    sin = np.concatenate([-np.sin(ang_r), np.sin(ang_r), -np.sin(ang_c), np.sin(ang_c)], axis=1)
    return (jnp.asarray(np.tile(cos, (1, 2)), F32), jnp.asarray(np.tile(sin, (1, 2)), F32))


def _rope(x, cos, sin):
    lane = lax.broadcasted_iota(jnp.int32, x.shape, 1)
    partner = jnp.where((lane & 31) < 16, pltpu.roll(x, 128 - 16, 1), pltpu.roll(x, 16, 1))
    return x * cos + partner * sin


def _attn_lat_kernel(sink_ref, p_ref, ck_ref, cv_ref, cos_ref, sin_ref, o_ref, *, seq_len):
    scale = ATT_HD ** -0.5 * LOG2E
    qb = pl.program_id(1)
    span = 3 * ATT_BLOCK
    q0 = pl.multiple_of(qb * ATT_BLOCK, ATT_BLOCK)
    k0 = pl.multiple_of(jnp.clip((qb - 1) * ATT_BLOCK, 0, seq_len - span), ATT_BLOCK)
    qrows = pl.ds(q0, ATT_BLOCK)
    krows = pl.ds(k0, span)
    kwin = _rope(p_ref[krows, ATT_Q_W:ATT_Q_W + ATT_KV_W], cos_ref[krows, :], sin_ref[krows, :])
    vwin = p_ref[krows, ATT_Q_W + ATT_KV_W:ATT_W]
    stacked = ATT_GROUP * ATT_BLOCK
    qpos = q0 + (lax.broadcasted_iota(jnp.int32, (stacked, span), 0) & (ATT_BLOCK - 1))
    kpos = k0 + lax.broadcasted_iota(jnp.int32, (stacked, span), 1)
    valid = jnp.abs(qpos - kpos) <= WINDOW
    cos_q = cos_ref[qrows, :]
    sin_q = sin_ref[qrows, :]
    heads = []
    for pair in range(ATT_HEADS // 2):
        qpair = _rope(p_ref[qrows, pair * 128:(pair + 1) * 128], cos_q, sin_q) * scale
        heads += [qpair[:, :ATT_HD], qpair[:, ATT_HD:]]
    kv = range(ATT_KV_HEADS)
    qs = [jnp.concatenate(heads[j * ATT_GROUP:(j + 1) * ATT_GROUP], axis=0) for j in kv]
    s_locs = [jnp.where(valid, _mm_nt(qs[j], kwin[:, j * ATT_HD:(j + 1) * ATT_HD], prec="bf"), NEG_INF)
              for j in kv]
    s_ctxs = [_mm_nt(qs[j], ck_ref[j], prec="bf") for j in kv]
    outs = [_softmax_pv([s_locs[j], s_ctxs[j]], [vwin[:, j * ATT_HD:(j + 1) * ATT_HD], cv_ref[j]],
                        _group_sinks(sink_ref, j, ATT_BLOCK)) for j in kv]
    for j in kv:
        _store_group(o_ref, j, outs[j], ATT_BLOCK)


def _attn_latent(proj_att, sink, cache_k, cache_v, layer, n_seq, seq_len):
    cos, sin = _rope_tables(seq_len)
    past = cache_k.shape[3]
    nqb = seq_len // ATT_BLOCK
    cache = pl.BlockSpec((None, None, ATT_KV_HEADS, past, ATT_HD), lambda b, q: (b, layer, 0, 0, 0))
    table = pl.BlockSpec((seq_len, 128), lambda b, q: (0, 0))
    return pl.pallas_call(
        functools.partial(_attn_lat_kernel, seq_len=seq_len),
        grid=(n_seq, nqb),
        in_specs=[pl.BlockSpec(memory_space=pltpu.SMEM),
                  pl.BlockSpec((seq_len, ATT_W), lambda b, q: (b, 0)),
                  cache, cache, table, table],
        out_specs=pl.BlockSpec((ATT_BLOCK, ATT_Q_W), lambda b, q: (b * nqb + q, 0)),
        out_shape=jax.ShapeDtypeStruct((n_seq * seq_len, ATT_Q_W), F32),
        compiler_params=pltpu.CompilerParams(dimension_semantics=("parallel", "parallel")),
        name="attn_latent",
    )(sink, proj_att, cache_k, cache_v, cos, sin)


def _pair_masks(n, rev):
    r = lax.broadcasted_iota(jnp.int32, (n, 2 * n), 0)
    c = lax.broadcasted_iota(jnp.int32, (n, 2 * n), 1) & (n - 1)
    return (r <= c, r < c) if rev else (r >= c, r > c)


def _bd(x):
    half = x.shape[1] // 2
    lane = lax.broadcasted_iota(jnp.int32, x.shape, 1)
    zero = jnp.zeros_like(x)
    return jnp.concatenate([jnp.where(lane < half, x, zero), jnp.where(lane >= half, x, zero)], axis=0)


def _bd_mask(n):
    r = lax.broadcasted_iota(jnp.int32, (2 * n, 2 * n), 0)
    c = lax.broadcasted_iota(jnp.int32, (2 * n, 2 * n), 1)
    return (r < n) == (c < n)


def _head_sums(x):
    ones = jnp.where(_bd_mask(RWKV_HD), 1.0, 0.0).astype(BF16)
    return _mm(x, ones, "bf")


def _apply_pairs(tb, x):
    hi, lo = _split_bf16(x)
    return _mm(tb, _bd(hi), "bf") + _mm(tb, _bd(lo), "bf")


def _unit_tri_inv_pairs(mats):
    n = mats[0].shape[0]
    r = lax.broadcasted_iota(jnp.int32, (n, 2 * n), 0)
    c = lax.broadcasted_iota(jnp.int32, (n, 2 * n), 1) & (n - 1)
    eye = jnp.where(r == c, 1.0, 0.0)
    within2 = jnp.where((r >> 1) == (c >> 1), 1.0, 0.0)
    ts = [eye - a * within2 for a in mats]
    abs_ = [a.astype(BF16) for a in mats]
    shift = 1
    while (1 << shift) < n:
        join = jnp.where(((r >> (shift + 1)) == (c >> (shift + 1))) & ((r >> shift) != (c >> shift)),
                         1.0, 0.0).astype(BF16)
        tbs = [t.astype(BF16) for t in ts]
        inner = [_mm(ab * join, _bd(tb), "bf") for ab, tb in zip(abs_, tbs)]
        ts = [t - _mm(tb, _bd(w.astype(BF16)), "bf") for t, tb, w in zip(ts, tbs, inner)]
        shift += 1
    return ts


def _rwkv_chunk_operands(x_ref, prm, sc, c, *, seq_len):
    (mu_ref, w0_ref, wup_ref, a0_ref, aup_ref, gup_ref, kk_ref, ka_ref, rk_ref) = prm
    ch = RWKV_CHUNK
    total = x_ref.shape[0]
    r0 = pl.multiple_of(c * ch, ch)
    rows = pl.ds(r0, ch)
    pos0 = r0 & (seq_len - 1)
    x = x_ref[rows, :]
    prev_row = x_ref[pl.ds(jnp.maximum(r0 - 1, 0), 1), :] * jnp.where(pos0 > 0, 1.0, 0.0)
    next_row = x_ref[pl.ds(jnp.minimum(r0 + ch, total - 1), 1), :] * jnp.where(pos0 + ch < seq_len, 1.0, 0.0)
    xp, xn = _shifted_rows(x, prev_row, next_row)
    mu0, mu1 = mu_ref[0:1, :], mu_ref[1:2, :]
    xs = x * (1.0 - mu0 - mu1) + xp * mu0 + xn * mu1
    r = xs[:, 0:RWKV_W]
    k = xs[:, RWKV_W:2 * RWKV_W]
    v = xs[:, 2 * RWKV_W:3 * RWKV_W]
    lo = 3 * RWKV_W
    p = RWKV_PREC
    gl = xs[:, lo + 4 * RWKV_LORA:lo + 6 * RWKV_LORA]
    sc["gate"][rows, :] = _mm(jax.nn.sigmoid(gl), gup_ref[...], p["gate"])
    sc["v"][rows, :] = v.astype(BF16)
    pairs = [slice(i * PAIR_W, (i + 1) * PAIR_W) for i in range(N_PAIRS)]
    kkv = k * kk_ref[...]
    kaps = []
    for cols in pairs:
        kk_p = kkv[:, cols]
        kaps.append(kk_p * lax.rsqrt(_head_sums(kk_p * kk_p) + 1e-6))
    items = []
    bonus = None
    for d in range(2):
        rev = d == 1
        wl = xs[:, lo + d * RWKV_LORA:lo + (d + 1) * RWKV_LORA]
        al = xs[:, lo + 2 * RWKV_LORA + d * RWKV_LORA:lo + 2 * RWKV_LORA + (d + 1) * RWKV_LORA]
        lw = (-DECAY_SCALE * LOG2E) * jax.nn.sigmoid(w0_ref[d:d + 1, :] + _mm(jnp.tanh(wl), wup_ref[d], p["lora"]))
        a = jax.nn.sigmoid(a0_ref[d:d + 1, :] + _mm(al, aup_ref[d], p["lora"]))
        k2 = k * (1.0 + (a - 1.0) * ka_ref[...])
        g_inc = _mm(jnp.where(_tri_masks(ch, rev)[0], 1.0, 0.0), lw, p["cumsum"])
        g_tot = jnp.sum(lw, axis=0, keepdims=True)
        e_neg = jnp.exp2(-g_inc)
        e_end = jnp.exp2(g_tot - g_inc)
        e_exc = jnp.exp2(g_inc - lw)
        r_dec = r * jnp.exp2(g_inc)
        k_neg = k2 * e_neg
        k_end = k2 * e_end
        sc["dec"][d, pl.ds(pl.multiple_of(c * 8, 8), 8), :] = jnp.broadcast_to(jnp.exp2(g_tot), (8, RWKV_W))
        rkr = r * k2 * rk_ref[...]
        bon_d = jnp.concatenate([_head_sums(rkr[:, cols]) for cols in pairs], axis=1) * v
        bonus = bon_d if bonus is None else bonus + bon_d
        for cols, kap in zip(pairs, kaps):
            b_p = kap * a[:, cols]
            items.append(dict(d=d, rows=rows, cols=cols, kap_dec=kap * e_exc[:, cols], r_dec=r_dec[:, cols],
                              b_neg=b_p * e_neg[:, cols], k_neg=k_neg[:, cols],
                              b_end=b_p * e_end[:, cols], k_end=k_end[:, cols], vb=v[:, cols].astype(BF16)))
    sc["bon"][rows, :] = bonus
    return items


def _rwkv_solve(items, sc):
    ch = RWKV_CHUNK
    masks = [_pair_masks(ch, False), _pair_masks(ch, True)]
    ms = [_mm_nt(jnp.concatenate([it["kap_dec"], it["r_dec"]], axis=0),
                 jnp.concatenate([_bd(it["b_neg"].astype(BF16)), _bd(it["k_neg"].astype(BF16))], axis=0), "bf")
          for it in items]
    a_abs = [jnp.where(masks[it["d"]][1], m[:ch, :2 * ch], 0.0) for it, m in zip(items, ms)]
    a_aks = [jnp.where(masks[it["d"]][1], m[:ch, 2 * ch:], 0.0) for it, m in zip(items, ms)]
    a_rbs = [jnp.where(masks[it["d"]][0], m[ch:, :2 * ch], 0.0) for it, m in zip(items, ms)]
    a_rks = [jnp.where(masks[it["d"]][0], m[ch:, 2 * ch:], 0.0) for it, m in zip(items, ms)]
    tbs = [t.astype(BF16) for t in _unit_tri_inv_pairs(a_abs)]
    akvs = [_mm(a_ak, _bd(it["vb"]), "bf") for it, a_ak in zip(items, a_aks)]
    w2s = [_apply_pairs(tb, it["kap_dec"]) for it, tb in zip(items, tbs)]
    u0s = [_apply_pairs(tb, akv) for tb, akv in zip(tbs, akvs)]
    for it, w2, u0, a_rb, a_rk in zip(items, w2s, u0s, a_rbs, a_rks):
        d, rows, cols = it["d"], it["rows"], it["cols"]
        sc["w2"][d, rows, cols] = w2.astype(BF16)
        sc["rd"][d, rows, cols] = it["r_dec"].astype(BF16)
        sc["u0"][d, rows, cols] = u0
        sc["arb"][d, rows, cols] = a_rb.astype(BF16)
        sc["ark"][d, rows, cols] = a_rk.astype(BF16)
        sc["bh"][d, rows, cols] = it["b_end"].astype(BF16)
        sc["kh"][d, rows, cols] = it["k_end"].astype(BF16)


def _rwkv_recur(sc, st, i, *, seq_len, seqs):
    ch = RWKV_CHUNK
    n_chunks = seq_len // ch
    keep = _bd_mask(RWKV_HD)
    cs = []
    for j in range(seqs):
        for d in range(2):
            c = j * n_chunks + (i if d == 0 else n_chunks - 1 - i)
            rows = pl.ds(pl.multiple_of(c * ch, ch), ch)
            dec = sc["dec"][d, pl.ds(pl.multiple_of(c * 8, 8), 1), :]
            for pi in range(N_PAIRS):
                cols = slice(pi * PAIR_W, (pi + 1) * PAIR_W)
                cs.append(dict(d=d, j=j, p=pi, rows=rows, cols=cols, dec=dec[:, cols], s=st[d, j, pi]))
    sbs = [c["s"].astype(BF16) for c in cs]
    lss = [_mm_nt(jnp.concatenate([sc["w2"][c["d"], c["rows"], c["cols"]],
                                   sc["rd"][c["d"], c["rows"], c["cols"]]], axis=0), sb, "bf")
           for c, sb in zip(cs, sbs)]
    ubs = [(-(ls[:ch] + sc["u0"][c["d"], c["rows"], c["cols"]])).astype(BF16) for c, ls in zip(cs, lss)]
    vbs = [sc["v"][c["rows"], c["cols"]] for c in cs]
    ys = [ls[ch:] + _mm(jnp.concatenate([sc["arb"][c["d"], c["rows"], c["cols"]],
                                         sc["ark"][c["d"], c["rows"], c["cols"]]], axis=1),
                        jnp.concatenate([_bd(ub), _bd(vb)], axis=0), "bf")
          for c, ls, ub, vb in zip(cs, lss, ubs, vbs)]
    ups = [_mm_tn(jnp.concatenate([ub, vb], axis=0),
                  jnp.concatenate([sc["bh"][c["d"], c["rows"], c["cols"]],
                                   sc["kh"][c["d"], c["rows"], c["cols"]]], axis=0), "bf")
           for c, ub, vb in zip(cs, ubs, vbs)]
    for c, y, up in zip(cs, ys, ups):
        st[c["d"], c["j"], c["p"]] = c["s"] * c["dec"] + jnp.where(keep, up, 0.0)
        sc["ysum"][c["rows"], c["cols"]] += y


RWKV_SCRATCH = ("ysum", "bon", "gate", "v", "w2", "rd", "u0", "arb", "ark", "bh", "kh", "dec")


def _rwkv_kernel(*refs, seq_len, seqs, has_init):
    x_ref = refs[0]
    prm = refs[1:10]
    lnw_ref, lnb_ref = refs[10:12]
    pos = 12
    if has_init:
        s0f_ref, s0b_ref = refs[pos:pos + 2]
        pos += 2
    o_ref, sf_ref, sb_ref = refs[pos:pos + 3]
    sc = dict(zip(RWKV_SCRATCH, refs[pos + 3:]))
    st = refs[pos + 3 + len(RWKV_SCRATCH)]
    ch = RWKV_CHUNK
    hd = RWKV_HD
    n_chunks = seq_len // ch
    sc["ysum"][...] = jnp.zeros(sc["ysum"].shape, F32)

    def prepare(gi, carry):
        items = []
        for j in range(RWKV_GROUP):
            items += _rwkv_chunk_operands(x_ref, prm, sc, gi * RWKV_GROUP + j, seq_len=seq_len)
        _rwkv_solve(items, sc)
        return carry

    lax.fori_loop(0, seqs * n_chunks // RWKV_GROUP, prepare, 0)

    zero = jnp.zeros((hd, hd), F32)
    for d, s0_ref in enumerate((s0f_ref, s0b_ref) if has_init else (None, None)):
        for j in range(seqs):
            for pi in range(N_PAIRS):
                s_a = s0_ref[j, 2 * pi] if has_init else zero
                s_b = s0_ref[j, 2 * pi + 1] if has_init else zero
                st[d, j, pi] = jnp.concatenate([jnp.concatenate([s_a, zero], axis=1),
                                                jnp.concatenate([zero, s_b], axis=1)], axis=0)

    def recur(i, carry):
        _rwkv_recur(sc, st, i, seq_len=seq_len, seqs=seqs)
        return carry

    lax.fori_loop(0, n_chunks, recur, 0)
    for d, out_ref in enumerate((sf_ref, sb_ref)):
        for j in range(seqs):
            for pi in range(N_PAIRS):
                s = st[d, j, pi]
                out_ref[j, 2 * pi] = s[:hd, :hd]
                out_ref[j, 2 * pi + 1] = s[hd:, hd:]

    tile = RWKV_FINISH_ROWS

    def finish(i, carry):
        rows = pl.ds(pl.multiple_of(i * tile, tile), tile)
        pairs = [slice(pi * PAIR_W, (pi + 1) * PAIR_W) for pi in range(N_PAIRS)]
        ys = [sc["ysum"][rows, cols] for cols in pairs]
        cens = [y - _head_sums(y) * (1.0 / hd) for y in ys]
        vars_ = [_head_sums(cen * cen) * (1.0 / hd) for cen in cens]
        for cols, cen, var in zip(pairs, cens, vars_):
            yn = cen * lax.rsqrt(var + GN_EPS) * lnw_ref[:, cols] + lnb_ref[:, cols]
            o_ref[rows, cols] = (yn + sc["bon"][rows, cols]) * sc["gate"][rows, cols]
        return carry

    lax.fori_loop(0, seqs * seq_len // tile, finish, 0)


def _rwkv_mixer(x_rw, n_seq, seq_len, params, s0_f, s0_b):
    has_init = s0_f is not None
    (mu, w0, w_up, a0, a_up, g_up, k_k, k_a, r_k, ln_w, ln_b) = params
    row = lambda a: a.reshape(1, RWKV_W)
    args = [x_rw, mu, w0, w_up, a0, a_up, g_up, row(k_k), row(k_a), row(r_k), row(ln_w), row(ln_b)]

    def whole(a):
        nd = a.ndim
        return pl.BlockSpec(a.shape, lambda s, nd=nd: (0,) * nd)

    seqs = max(1, RWKV_BLOCK_ROWS // seq_len)
    rows = seqs * seq_len
    assert n_seq % seqs == 0 and (rows // RWKV_CHUNK) % RWKV_GROUP == 0 and seq_len & (seq_len - 1) == 0
    in_specs = [pl.BlockSpec((rows, RWKV_IN), lambda s: (s, 0))] + [whole(a) for a in args[1:]]
    state = pl.BlockSpec((seqs, None, RWKV_HEADS, RWKV_HD, RWKV_HD), lambda s: (s, 0, 0, 0, 0))
    if has_init:
        in_specs += [state, state]
        args += [s0_f, s0_b]
    st_shape = jax.ShapeDtypeStruct((n_seq, 1, RWKV_HEADS, RWKV_HD, RWKV_HD), F32)
    tok = lambda dt: pltpu.VMEM((rows, RWKV_W), dt)
    per_dir = lambda dt: pltpu.VMEM((2, rows, RWKV_W), dt)
    scratch = dict(ysum=tok(F32), bon=tok(F32), gate=tok(F32), v=tok(BF16), w2=per_dir(BF16), rd=per_dir(BF16),
                   u0=per_dir(F32), arb=per_dir(BF16), ark=per_dir(BF16), bh=per_dir(BF16), kh=per_dir(BF16),
                   dec=pltpu.VMEM((2, rows // RWKV_CHUNK * 8, RWKV_W), F32))
    return pl.pallas_call(
        functools.partial(_rwkv_kernel, seq_len=seq_len, seqs=seqs, has_init=has_init),
        grid=(n_seq // seqs,),
        in_specs=in_specs,
        out_specs=[pl.BlockSpec((rows, RWKV_W), lambda s: (s, 0)), state, state],
        out_shape=[jax.ShapeDtypeStruct((n_seq * seq_len, RWKV_W), F32), st_shape, st_shape],
        scratch_shapes=[scratch[name] for name in RWKV_SCRATCH]
        + [pltpu.VMEM((2, seqs, N_PAIRS, PAIR_W, PAIR_W), F32)],
        compiler_params=pltpu.CompilerParams(dimension_semantics=("parallel",), vmem_limit_bytes=VMEM_LIMIT),
        name="rwkv_mixer",
    )(*args)


def kernel(x_prompt, x_sample, state_gdn_fwd, state_gdn_bwd, cache_attn_k, cache_attn_v, state_rwkv_fwd, state_rwkv_bwd, c, c_ctx, mod_w, mod_b, norm_mix, norm_mlp, mlp_w1, mlp_w2, norm_final, ev_w_in, ev_w_out, gdn_conv, gdn_a_log, gdn_dt_bias, gdn_norm, sc_conv, od_w_in, od_w_out, attn_sink, rwkv_mu, rwkv_w0, rwkv_w_up, rwkv_a0, rwkv_a_up, rwkv_g_up, rwkv_k_k, rwkv_k_a, rwkv_r_k, rwkv_ln_w, rwkv_ln_b):
    bp, lp, _ = x_prompt.shape
    bs, ls, _ = x_sample.shape
    depth = mod_w.shape[0]
    c_rows = jnp.concatenate([c_ctx[None, :], c, jnp.zeros((MOD_ROWS - 1 - bs, D_MODEL), F32)], axis=0)
    mods, ev_w_t = _modulation(c_rows, mod_w, mod_b, ev_w_in)

    assert ls % TOKEN_TILE == 0 and (bp * lp) % TOKEN_TILE == 0
    groups = [
        dict(x=x_prompt.reshape(bp * lp, D_MODEL), n=bp, l=lp, latent=False,
             mod=lambda layer: _mod_spec(layer, 1, 0, 0)),
        dict(x=x_sample.reshape(bs * ls, D_MODEL), n=bs, l=ls, latent=True,
             mod=lambda layer: _mod_spec(layer, ls // TOKEN_TILE, 1, 1)),
    ]
    outs = {}
    for layer in range(depth):
        final = layer == depth - 1
        mlp_cast = ((mlp_w1, layer), (mlp_w2, layer))
        if layer % 2 == 0:
            e = layer // 2
            alog_vec = jnp.zeros((1, 128), F32).at[0, 2 * GDN_HEADS:4 * GDN_HEADS].set(gdn_a_log[e].reshape(-1))
            dtb_vec = jnp.zeros((1, 128), F32).at[0, 2 * GDN_HEADS:4 * GDN_HEADS].set(gdn_dt_bias[e].reshape(-1))
            for grp in groups:
                qkv, gz, sc, gates = _inproj_even(grp["x"], mods, norm_mix[layer], ev_w_t[e], gdn_conv[e],
                                                  sc_conv[e], alog_vec, dtb_vec, grp["l"], grp["mod"](layer))
                s0 = (state_gdn_fwd[:, e:e + 1], state_gdn_bwd[:, e:e + 1]) if grp["latent"] else (None, None)
                if grp["latent"]:
                    cast = ((od_w_in, layer // 2), (od_w_out, layer // 2)) if layer + 1 < depth else ()
                    o, _, _, *next_w = _gdn_mixer(qkv, gz, gates, grp["n"], grp["l"], gdn_norm[e], *s0, cast=cast)
                else:
                    o, s_f, s_b, w_out, w1, w2 = _gdn_mixer(qkv, gz, gates, grp["n"], grp["l"], gdn_norm[e], *s0,
                                                            cast=((ev_w_out, e),) + mlp_cast)
                    outs.setdefault("gdn_f", []).append(s_f)
                    outs.setdefault("gdn_b", []).append(s_b)
                grp["mixed"] = (o, sc)
        else:
            o_ = layer // 2
            w_in, w_out = [next_w[0]], next_w[1]
            rw = (rwkv_mu[o_], rwkv_w0[o_], rwkv_w_up[o_], rwkv_a0[o_], rwkv_a_up[o_], rwkv_g_up[o_],
                  rwkv_k_k[o_], rwkv_k_a[o_], rwkv_r_k[o_].reshape(-1), rwkv_ln_w[o_], rwkv_ln_b[o_])
            for grp in groups:
                p_att, x_rw = _inproj(grp["x"], mods, norm_mix[layer], w_in, (ATT_W, RWKV_IN), grp["mod"](layer))
                if grp["latent"]:
                    att = _attn_latent(p_att, attn_sink[o_], cache_attn_k, cache_attn_v, o_, grp["n"], grp["l"])
                    rwo, _, _ = _rwkv_mixer(x_rw, grp["n"], grp["l"], rw,
                                            state_rwkv_fwd[:, o_:o_ + 1], state_rwkv_bwd[:, o_:o_ + 1])
                else:
                    att, kc, vc, w1, w2 = _attn_context(p_att, attn_sink[o_], grp["n"], grp["l"], cast=mlp_cast)
                    rwo, s_f, s_b = _rwkv_mixer(x_rw, grp["n"], grp["l"], rw, None, None)
                    outs.setdefault("att_k", []).append(kc)
                    outs.setdefault("att_v", []).append(vc)
                    outs.setdefault("rw_f", []).append(s_f)
                    outs.setdefault("rw_b", []).append(s_b)
                grp["mixed"] = (att, rwo)
        groups[0]["x"], groups[1]["x"] = _outproj_mlp(
            *[(*grp["mixed"], grp["x"]) for grp in groups], mods, norm_mlp[layer], w_out, w1, w2, norm_final,
            layer, ls // TOKEN_TILE, final)
    cat = lambda key: jnp.concatenate(outs[key], axis=1)
    return (groups[0]["x"].reshape(bp, lp, D_MODEL), groups[1]["x"].reshape(bs, ls, D_MODEL),
            cat("gdn_f"), cat("gdn_b"), cat("att_k"), cat("att_v"), cat("rw_f"), cat("rw_b"))
```

```python
import functools

import jax
import jax.numpy as jnp
import numpy as np
from jax import lax
from jax.experimental import pallas as pl
from jax.experimental.pallas import tpu as pltpu

F32 = jnp.float32
BF16 = jnp.bfloat16

D_MODEL = 1024
N_MOD = 6
D_FF = 4 * D_MODEL
NORM_EPS = 1e-6
TOKEN_TILE = 512
HALO = 8
MOD_ROWS = 8

GDN_HEADS = 4
GDN_D = 128
GDN_CHUNK = 128
GDN_SOLVE_CHAINS = 32
GDN_RECUR_CHAINS = 8
GDN_BLOCK_ROWS = 1024
GDN_QKV_W = 3 * GDN_HEADS * GDN_D
SC_WIDTH = 512
MOD_COL_BLOCKS = 4
EV_CAST_ROWS = 720

ATT_HEADS = 8
ATT_KV_HEADS = 2
ATT_GROUP = ATT_HEADS // ATT_KV_HEADS
ATT_HD = 64
ATT_Q_W = ATT_HEADS * ATT_HD
ATT_KV_W = ATT_KV_HEADS * ATT_HD
ATT_W = ATT_Q_W + 2 * ATT_KV_W
ATT_CTX_SEQS = 2
WINDOW = 128
ATT_BLOCK = 128
GRID_W = 64
ROPE_BASE = 10000.0
NEG_INF = -1e30

RWKV_HEADS = 8
RWKV_HD = 64
RWKV_W = RWKV_HEADS * RWKV_HD
RWKV_LORA = 64
RWKV_IN = 3 * RWKV_W + 3 * 2 * RWKV_LORA
RWKV_CHUNK = 64
RWKV_GROUP = 4
RWKV_FINISH_ROWS = 256
DECAY_SCALE = float(np.exp(-0.5))
LOG2E = float(np.log2(np.e))
RWKV_BLOCK_ROWS = 1024
PAIR_W = 2 * RWKV_HD
N_PAIRS = RWKV_W // PAIR_W
GN_EPS = 64e-5

VMEM_LIMIT = 56 * 1024 * 1024

RWKV_PREC = dict(lora="bf", gate="bf", cumsum="x2r")


def _split_bf16(a):
    hi = a.astype(BF16)
    return hi, (a - hi.astype(F32)).astype(BF16)


def _dot(a, b, dims, prec):
    dn = (dims, ((), ()))
    one = lambda x, y: lax.dot_general(x, y, dn, preferred_element_type=F32)
    if prec == "x2r":
        ah = a.astype(BF16)
        bh, bl = _split_bf16(b)
        return one(ah, bh) + one(ah, bl)
    assert prec == "bf", prec
    return one(a.astype(BF16), b.astype(BF16))


def _mm(a, b, prec):
    return _dot(a, b, ((1,), (0,)), prec)


def _mm_nt(a, b, prec):
    return _dot(a, b, ((1,), (1,)), prec)


def _mm_tn(a, b, prec):
    return _dot(a, b, ((0,), (0,)), prec)


def _silu(x):
    h = 0.5 * x
    return h + h * jnp.tanh(h)


def _softplus(x):
    return jnp.maximum(x, 0.0) + jnp.log1p(jnp.exp(-jnp.abs(x)))


def _rms(x, w):
    return x * lax.rsqrt(jnp.mean(x * x, axis=-1, keepdims=True) + NORM_EPS) * w


def _tri_masks(n, rev):
    r = lax.broadcasted_iota(jnp.int32, (n, n), 0)
    c = lax.broadcasted_iota(jnp.int32, (n, n), 1)
    if rev:
        return r <= c, r < c
    return r >= c, r > c


def _unit_tri_inv_many(mats, prec):
    assert prec == "bf"
    n = mats[0].shape[0]
    r = lax.broadcasted_iota(jnp.int32, (n, n), 0)
    c = lax.broadcasted_iota(jnp.int32, (n, n), 1)
    eye = jnp.where(r == c, 1.0, 0.0)
    within2 = jnp.where((r >> 1) == (c >> 1), 1.0, 0.0)
    ts = [eye - a * within2 for a in mats]
    abs_ = [a.astype(BF16) for a in mats]
    shift = 1
    while (1 << shift) < n:
        join = jnp.where(((r >> (shift + 1)) == (c >> (shift + 1))) & ((r >> shift) != (c >> shift)),
                         1.0, 0.0).astype(BF16)
        tbs = [t.astype(BF16) for t in ts]
        inner = [_mm(ab * join, tb, prec) for ab, tb in zip(abs_, tbs)]
        ts = [t - _mm(tb, w, prec) for t, tb, w in zip(ts, tbs, inner)]
        shift += 1
    return ts


def _shifted_rows(x, prev_row, next_row):
    n = x.shape[0]
    row = lax.broadcasted_iota(jnp.int32, x.shape, 0)
    xp = jnp.where(row == 0, prev_row, pltpu.roll(x, 1, 0))
    xn = jnp.where(row == n - 1, next_row, pltpu.roll(x, n - 1, 0))
    return xp, xn


def _mod_kernel(c_ref, w_ref, b_ref, evt_ref, o_ref, wt_ref):
    s = _silu(c_ref[...])
    o_ref[...] = _mm(s, w_ref[...], prec="bf") + b_ref[...]
    wt_ref[...] = evt_ref[...].astype(BF16)


def _modulation(c_rows, mod_w, mod_b, ev_w_in):
    depth = mod_w.shape[0]
    n_even, _, ev_cols = ev_w_in.shape
    nblk = MOD_COL_BLOCKS
    width = N_MOD * D_MODEL // nblk
    steps = depth * nblk
    rows = n_even * ev_cols
    n_side = rows // EV_CAST_ROWS
    assert rows % EV_CAST_ROWS == 0 and n_side <= steps and EV_CAST_ROWS % 16 == 0 and width % 128 == 0
    side = pl.BlockSpec((EV_CAST_ROWS, D_MODEL), lambda l, j: (jnp.minimum(l * nblk + j, n_side - 1), 0))
    out, w_t = pl.pallas_call(
        _mod_kernel,
        grid=(depth, nblk),
        in_specs=[
            pl.BlockSpec((MOD_ROWS, D_MODEL), lambda l, j: (0, 0)),
            pl.BlockSpec((None, D_MODEL, width), lambda l, j: (l, 0, j)),
            pl.BlockSpec((None, 1, width), lambda l, j: (l, 0, j)),
            side,
        ],
        out_specs=[pl.BlockSpec((None, MOD_ROWS, width), lambda l, j: (l, 0, j)), side],
        out_shape=[jax.ShapeDtypeStruct((depth, MOD_ROWS, N_MOD * D_MODEL), F32),
                   jax.ShapeDtypeStruct((rows, D_MODEL), BF16)],
        compiler_params=pltpu.CompilerParams(dimension_semantics=("arbitrary", "arbitrary"),
                                             vmem_limit_bytes=VMEM_LIMIT),
        name="modulation",
    )(c_rows, mod_w, mod_b.reshape(depth, 1, N_MOD * D_MODEL), jnp.swapaxes(ev_w_in, 1, 2).reshape(rows, D_MODEL))
    return out.reshape(depth, MOD_ROWS, N_MOD, D_MODEL), w_t.reshape(n_even, ev_cols, D_MODEL)


def _mod_spec(layer, tiles_per_seq, row_base, row_step):
    return pl.BlockSpec((None, None, N_MOD, D_MODEL),
                        lambda i: (layer, row_base + (i // tiles_per_seq) * row_step, 0, 0))


def _two_group_maps(layer, n_first, tiles_per_seq):
    first = lambda i: (jnp.minimum(i, n_first - 1), 0)
    second = lambda i: (jnp.maximum(i - n_first, 0), 0)
    mod_spec = pl.BlockSpec(
        (None, None, N_MOD, D_MODEL),
        lambda i: (layer, jnp.where(i < n_first, 0, 1 + jnp.maximum(i - n_first, 0) // tiles_per_seq), 0, 0))
    return first, second, mod_spec


def _inproj_kernel(*refs, n_w, n_first):
    x_refs = refs[0:2]
    mod_ref, nw_ref = refs[2:4]
    w_refs = refs[4:4 + n_w]
    o_refs = refs[4 + n_w:]
    n_out = len(o_refs) // 2

    def tile(x_ref, outs):
        h = _rms(x_ref[...], nw_ref[...])
        h = (h * (1.0 + mod_ref[1:2, :]) + mod_ref[0:1, :]).astype(BF16)
        pieces = [_mm(h, w_ref[...], prec="bf") for w_ref in w_refs]
        y = pieces[0] if n_w == 1 else jnp.concatenate(pieces, axis=1)
        off = 0
        for o_ref in outs:
            n = o_ref.shape[-1]
            o_ref[...] = y[:, off:off + n]
            off += n

    step = pl.program_id(0)

    @pl.when(step < n_first)
    def _():
        tile(x_refs[0], o_refs[:n_out])

    @pl.when(step >= n_first)
    def _():
        tile(x_refs[1], o_refs[n_out:])


def _inproj(x_first, x_second, mods, norm_w, ws_bf16, splits, layer, tiles_per_seq):
    t_first, t_second = x_first.shape[0], x_second.shape[0]
    assert t_first % TOKEN_TILE == 0 and t_second % TOKEN_TILE == 0
    assert sum(w.shape[1] for w in ws_bf16) == sum(splits) and all(w.shape[1] % 128 == 0 for w in ws_bf16)
    n_first = t_first // TOKEN_TILE
    first, second, mod_spec = _two_group_maps(layer, n_first, tiles_per_seq)
    return pl.pallas_call(
        functools.partial(_inproj_kernel, n_w=len(ws_bf16), n_first=n_first),
        grid=(n_first + t_second // TOKEN_TILE,),
        in_specs=[
            pl.BlockSpec((TOKEN_TILE, D_MODEL), first),
            pl.BlockSpec((TOKEN_TILE, D_MODEL), second),
            mod_spec,
            pl.BlockSpec((1, D_MODEL), lambda i: (0, 0)),
        ] + [pl.BlockSpec(w.shape, lambda i: (0, 0)) for w in ws_bf16],
        out_specs=[pl.BlockSpec((TOKEN_TILE, n), first) for n in splits]
        + [pl.BlockSpec((TOKEN_TILE, n), second) for n in splits],
        out_shape=[jax.ShapeDtypeStruct((t_first, n), F32) for n in splits]
        + [jax.ShapeDtypeStruct((t_second, n), F32) for n in splits],
        compiler_params=pltpu.CompilerParams(dimension_semantics=("arbitrary",), vmem_limit_bytes=VMEM_LIMIT),
        name="inproj",
    )(x_first, x_second, mods, norm_w.reshape(1, D_MODEL), *ws_bf16)


def _inproj_even_kernel(xp_ref, x_ref, xn_ref, mod_ref, nw_ref, wt_ref, cqkv_ref, csc_ref,
                        alog_ref, dtb_ref, qkv_ref, gz_ref, sc_ref, gate_ref, *, seq_len):
    tile = x_ref.shape[0]
    proj = lambda row0, width: _mm_nt(h, wt_ref[row0:row0 + width, :], "bf")
    x = jnp.concatenate([xp_ref[...], x_ref[...], xn_ref[...]], axis=0)
    h = _rms(x, nw_ref[...])
    h = (h * (1.0 + mod_ref[1:2, :]) + mod_ref[0:1, :]).astype(BF16)
    n = tile + 2 * HALO
    first = pl.program_id(0) * tile - HALO
    pos = (first + lax.broadcasted_iota(jnp.int32, (n, 1), 0)) & (seq_len - 1)
    at_start = pos == 0
    at_end = pos == seq_len - 1

    def conv3(v, c_ref):
        vp = jnp.where(at_start, 0.0, pltpu.roll(v, 1, 0))
        vn = jnp.where(at_end, 0.0, pltpu.roll(v, n - 1, 0))
        return vp * c_ref[0:1, :] + v * c_ref[1:2, :] + vn * c_ref[2:3, :]

    keep = slice(HALO, HALO + tile)
    wide = 2 * GDN_D
    gate_row = GDN_QKV_W + GDN_HEADS * GDN_D
    sc_row = gate_row + 4 * GDN_HEADS
    for j in range(GDN_QKV_W // wide):
        cols = slice(j * wide, (j + 1) * wide)
        act = _silu(conv3(proj(j * wide, wide), cqkv_ref.at[:, cols]))[keep]
        for i in range(2):
            part = act[:, i * GDN_D:(i + 1) * GDN_D]
            if j < 2 * GDN_HEADS // 2:
                part = part * lax.rsqrt(jnp.sum(part * part, axis=-1, keepdims=True) + 1e-6)
                if j < GDN_HEADS // 2:
                    part = part * (GDN_D ** -0.5)
            qkv_ref[:, j * wide + i * GDN_D:j * wide + (i + 1) * GDN_D] = part.astype(BF16)
    for j in range(GDN_HEADS * GDN_D // wide):
        cols = slice(j * wide, (j + 1) * wide)
        gz_ref[:, cols] = _silu(proj(GDN_QKV_W + j * wide, wide)[keep]).astype(BF16)
    for j in range(SC_WIDTH // wide):
        sc_b, sc_c, sc_h = [proj(sc_row + i * SC_WIDTH + j * wide, wide) for i in range(3)]
        cols = slice(j * wide, (j + 1) * wide)
        sc_ref[:, cols] = (sc_b * conv3(sc_c * sc_h, csc_ref.at[:, cols]))[keep].astype(BF16)
    g = proj(gate_row, 128)[keep]
    lane = lax.broadcasted_iota(jnp.int32, g.shape, 1)
    gate_ref[...] = jnp.where(lane < 2 * GDN_HEADS, jax.nn.sigmoid(g),
                              (-LOG2E * jnp.exp(alog_ref[...])) * _softplus(g + dtb_ref[...]))


def _inproj_even(x, mods, norm_w, w_t, conv_w, sc_conv_w, alog_vec, dtb_vec, seq_len, mod_spec):
    t = x.shape[0]
    tile = TOKEN_TILE
    assert seq_len & (seq_len - 1) == 0 and t % tile == 0
    per = tile // HALO
    last = t // HALO - 1
    const = lambda i: (0, 0)
    tok = lambda n: pl.BlockSpec((tile, n), lambda i: (i, 0))
    return pl.pallas_call(
        functools.partial(_inproj_even_kernel, seq_len=seq_len),
        grid=(t // tile,),
        in_specs=[
            pl.BlockSpec((HALO, D_MODEL), lambda i: (jnp.maximum(i * per - 1, 0), 0)),
            pl.BlockSpec((tile, D_MODEL), lambda i: (i, 0)),
            pl.BlockSpec((HALO, D_MODEL), lambda i: (jnp.minimum((i + 1) * per, last), 0)),
            mod_spec,
            pl.BlockSpec((1, D_MODEL), const),
            pl.BlockSpec(w_t.shape, const),
            pl.BlockSpec(conv_w.shape, const), pl.BlockSpec(sc_conv_w.shape, const),
            pl.BlockSpec((1, 128), const), pl.BlockSpec((1, 128), const),
        ],
        out_specs=[tok(GDN_QKV_W), tok(GDN_HEADS * GDN_D), tok(SC_WIDTH), tok(128)],
        out_shape=[jax.ShapeDtypeStruct((t, GDN_QKV_W), BF16), jax.ShapeDtypeStruct((t, GDN_HEADS * GDN_D), BF16),
                   jax.ShapeDtypeStruct((t, SC_WIDTH), BF16), jax.ShapeDtypeStruct((t, 128), F32)],
        compiler_params=pltpu.CompilerParams(dimension_semantics=("parallel",), vmem_limit_bytes=VMEM_LIMIT),
        name="inproj_even",
    )(x, x, x, mods, norm_w.reshape(1, D_MODEL), w_t, conv_w, sc_conv_w, alog_vec, dtb_vec)


def _mlp_kernel(*refs, final, n_first):
    groups = (refs[0:3], refs[3:6])
    mod_ref, nw_ref, woa_ref, wob_ref, w1_ref, w2_ref, nf_ref = refs[6:13]
    o_refs = refs[13:15]

    def tile(a_ref, b_ref, x_ref, o_ref):
        y = _mm(a_ref[...], woa_ref[...], prec="bf") + _mm(b_ref[...], wob_ref[...], prec="bf")
        x1 = x_ref[...] + mod_ref[2:3, :] * y
        h = _rms(x1, nw_ref[...])
        h = (h * (1.0 + mod_ref[4:5, :]) + mod_ref[3:4, :]).astype(BF16)
        acc = jnp.zeros(x1.shape, F32)
        for j in range(D_FF // D_MODEL):
            cols = slice(j * D_MODEL, (j + 1) * D_MODEL)
            u = jnp.maximum(_mm(h, w1_ref[:, cols], prec="bf"), 0.0)
            acc = acc + _mm(u * u, w2_ref[cols, :], prec="bf")
        x2 = x1 + mod_ref[5:6, :] * acc
        if final:
            x2 = _rms(x2, nf_ref[...])
        o_ref[...] = x2

    step = pl.program_id(0)

    @pl.when(step < n_first)
    def _():
        tile(*groups[0], o_refs[0])

    @pl.when(step >= n_first)
    def _():
        tile(*groups[1], o_refs[1])


def _outproj_mlp(abx_first, abx_second, mods, norm_w, w_out, w1, w2, norm_final, layer, tiles_per_seq, final):
    t_first, t_second = abx_first[2].shape[0], abx_second[2].shape[0]
    assert t_first % TOKEN_TILE == 0 and t_second % TOKEN_TILE == 0
    n_first, n_second = t_first // TOKEN_TILE, t_second // TOKEN_TILE
    half = abx_first[0].shape[1]
    const = lambda i: (0, 0)
    first, second, mod_spec = _two_group_maps(layer, n_first, tiles_per_seq)
    tok = lambda width, index: pl.BlockSpec((TOKEN_TILE, width), index)
    return pl.pallas_call(
        functools.partial(_mlp_kernel, final=final, n_first=n_first),
        grid=(n_first + n_second,),
        in_specs=[
            tok(half, first), tok(half, first), tok(D_MODEL, first),
            tok(half, second), tok(half, second), tok(D_MODEL, second),
            mod_spec,
            pl.BlockSpec((1, D_MODEL), const),
            pl.BlockSpec((half, D_MODEL), const),
            pl.BlockSpec((half, D_MODEL), lambda i: (1, 0)),
            pl.BlockSpec((D_MODEL, D_FF), const),
            pl.BlockSpec((D_FF, D_MODEL), const),
            pl.BlockSpec((1, D_MODEL), const),
        ],
        out_specs=[tok(D_MODEL, first), tok(D_MODEL, second)],
        out_shape=[jax.ShapeDtypeStruct((t_first, D_MODEL), F32), jax.ShapeDtypeStruct((t_second, D_MODEL), F32)],
        compiler_params=pltpu.CompilerParams(dimension_semantics=("arbitrary",), vmem_limit_bytes=VMEM_LIMIT),
        name="outproj_mlp",
    )(*abx_first, *abx_second, mods, norm_w.reshape(1, D_MODEL), w_out, w_out, w1, w2,
      norm_final.reshape(1, D_MODEL))


def _gdn_decay_terms(g, rev):
    c = g.shape[0]
    incl = _tri_masks(c, rev)[0]
    before_col = _tri_masks(c, not rev)[0]
    eye = lax.broadcasted_iota(jnp.int32, (c, c), 0) == lax.broadcasted_iota(jnp.int32, (c, c), 1)
    gc_row = jnp.sum(jnp.where(before_col, jnp.broadcast_to(g, (c, c)), 0.0), axis=0, keepdims=True)
    gc_col = jnp.sum(jnp.where(eye, jnp.broadcast_to(gc_row, (c, c)), 0.0), axis=1, keepdims=True)
    decay = jnp.where(incl, jnp.exp2(jnp.where(incl, gc_col - gc_row, 0.0)), 0.0)
    g_tot = jnp.sum(g, axis=0, keepdims=True)
    return decay, jnp.exp2(gc_col), jnp.exp2(g_tot - gc_col), jnp.exp2(g_tot)


def _cast_side_job(arrays_layers, n_steps, step_index):
    in_specs, out_specs, out_shapes = [], [], []
    for arr, layer in arrays_layers:
        _, r, c = arr.shape
        blk = r // n_steps
        assert r % n_steps == 0 and blk % 16 == 0
        in_specs.append(pl.BlockSpec((None, blk, c), lambda *g, layer=layer: (layer, step_index(*g), 0)))
        out_specs.append(pl.BlockSpec((blk, c), lambda *g: (step_index(*g), 0)))
        out_shapes.append(jax.ShapeDtypeStruct((r, c), BF16))
    return in_specs, out_specs, out_shapes


def _gdn_kernel(*refs, seq_len, seqs, hps, group, has_init, n_cast):
    q_ref, k_ref, v_ref, gz_ref, gate_ref, gn_ref = refs[:6]
    pos = 6
    if has_init:
        s0f_ref, s0b_ref = refs[pos:pos + 2]
        pos += 2
    cast_in = refs[pos:pos + n_cast]
    pos += n_cast
    o_ref, sf_ref, sb_ref = refs[pos:pos + 3]
    cast_out = refs[pos + 3:pos + 3 + n_cast]
    pos += n_cast
    for src, dst in zip(cast_in, cast_out):
        dst[...] = src[...].astype(BF16)
    osum, u_s, w_s, qd_s, kd_s, in_s, ge_s, st = refs[pos + 3:]
    head0 = pl.program_id(1) * hps
    ch = GDN_CHUNK
    n_chunks = seq_len // ch
    osum[...] = jnp.zeros(osum.shape, F32)
    hcols = [slice(hh * GDN_D, (hh + 1) * GDN_D) for hh in range(hps)]

    lane = lax.broadcasted_iota(jnp.int32, (ch, 128), 1)

    def pick(rows, col):
        return jnp.sum(jnp.where(lane == col, gate_ref[rows, :], 0.0), axis=1, keepdims=True)

    def solve_group(gi, carry):
        items = []
        for j in range(group):
            c = gi * group + j
            rows = pl.ds(pl.multiple_of(c * ch, ch), ch)
            for hh in range(hps):
                items.append(dict(c=c, hh=hh, rows=rows, q=q_ref[rows, hcols[hh]].astype(F32),
                                  k=k_ref[rows, hcols[hh]].astype(F32), v=v_ref[rows, hcols[hh]].astype(F32)))
        kks = [_mm_nt(it["k"], it["k"], "bf") for it in items]
        qks = [_mm_nt(it["q"], it["k"], "bf") for it in items]
        subs = []
        for it, kk, qk in zip(items, kks, qks):
            for d in range(2):
                beta = pick(it["rows"], d * GDN_HEADS + head0 + it["hh"])
                g = pick(it["rows"], 2 * GDN_HEADS + d * GDN_HEADS + head0 + it["hh"])
                decay, e_gc, e_rest, e_tot = _gdn_decay_terms(g, rev=(d == 1))
                strict = _tri_masks(ch, d == 1)[1]
                subs.append(dict(
                    d=d, c=it["c"], hh=it["hh"], rows=it["rows"],
                    a=jnp.where(strict, kk * beta * decay, 0.0),
                    rhs=jnp.concatenate([it["v"] * beta, it["k"] * (beta * e_gc)], axis=1),
                    intra=qk * decay, qd=it["q"] * e_gc, kd=it["k"] * e_rest, ge=e_tot))
        ts = _unit_tri_inv_many([s["a"] for s in subs], "bf")
        uws = [_mm(t, s["rhs"], "bf") for t, s in zip(ts, subs)]
        for s, uw in zip(subs, uws):
            d, hh, rows = s["d"], s["hh"], s["rows"]
            u_s[d, hh, rows, :] = uw[:, :GDN_D]
            w_s[d, hh, rows, :] = uw[:, GDN_D:].astype(BF16)
            qd_s[d, hh, rows, :] = s["qd"].astype(BF16)
            kd_s[d, hh, rows, :] = s["kd"].astype(BF16)
            in_s[d, hh, rows, :] = s["intra"].astype(BF16)
            ge_s[d, hh, pl.ds(pl.multiple_of(s["c"] * 8, 8), 8), :] = jnp.broadcast_to(s["ge"], (8, 128))
        return carry

    lax.fori_loop(0, seqs * n_chunks // group, solve_group, 0)

    for hh in range(hps):
        for j in range(seqs):
            st[0, hh, j] = s0f_ref[j, hh] if has_init else jnp.zeros((GDN_D, GDN_D), F32)
            st[1, hh, j] = s0b_ref[j, hh] if has_init else jnp.zeros((GDN_D, GDN_D), F32)

    def recur(i, carry):
        cs = []
        for hh in range(hps):
            for j in range(seqs):
                for d in range(2):
                    c = j * n_chunks + (i if d == 0 else n_chunks - 1 - i)
                    cs.append(dict(d=d, hh=hh, j=j, rows=pl.ds(pl.multiple_of(c * ch, ch), ch),
                                   ge=ge_s[d, hh, pl.ds(pl.multiple_of(c * 8, 8), 1), :], s=st[d, hh, j]))
        sbs = [c["s"].astype(BF16) for c in cs]
        wss = [_mm(w_s[c["d"], c["hh"], c["rows"], :], sb, "bf") for c, sb in zip(cs, sbs)]
        qss = [_mm(qd_s[c["d"], c["hh"], c["rows"], :], sb, "bf") for c, sb in zip(cs, sbs)]
        ebs = [(u_s[c["d"], c["hh"], c["rows"], :] - ws).astype(BF16) for c, ws in zip(cs, wss)]
        outs = [qs_ + _mm(in_s[c["d"], c["hh"], c["rows"], :], eb, "bf") for c, qs_, eb in zip(cs, qss, ebs)]
        s_news = [c["s"] * c["ge"] + _mm_tn(kd_s[c["d"], c["hh"], c["rows"], :], eb, "bf")
                  for c, eb in zip(cs, ebs)]
        for c, o, s_new in zip(cs, outs, s_news):
            st[c["d"], c["hh"], c["j"]] = s_new
            osum[c["rows"], hcols[c["hh"]]] += o
        return carry

    lax.fori_loop(0, n_chunks, recur, 0)
    for hh in range(hps):
        for j in range(seqs):
            sf_ref[j, hh] = st[0, hh, j]
            sb_ref[j, hh] = st[1, hh, j]
        o_ref[:, hcols[hh]] = (_rms(osum[:, hcols[hh]], gn_ref[...]) * gz_ref[:, hcols[hh]].astype(F32))


def _gdn_mixer(qkv, gz, gates, n_seq, seq_len, gdn_norm, s0_f, s0_b, cast=()):
    has_init = s0_f is not None
    hd = GDN_HEADS
    seqs = max(1, GDN_BLOCK_ROWS // seq_len)
    rows = seqs * seq_len
    hps = min(hd, max(2, GDN_RECUR_CHAINS // (2 * seqs)))
    group = GDN_SOLVE_CHAINS // (2 * hps)
    hgroups = hd // hps
    assert n_seq % seqs == 0 and (rows // GDN_CHUNK) % group == 0 and hd % hps == 0

    def col(block):
        return pl.BlockSpec((rows, hps * GDN_D), lambda s, h, b=block: (s, b * hgroups + h))

    state = pl.BlockSpec((seqs, None, hps, GDN_D, GDN_D), lambda s, h: (s, 0, h, 0, 0))
    in_specs = [col(0), col(1), col(2), col(0), pl.BlockSpec((rows, 128), lambda s, h: (s, 0)),
                pl.BlockSpec((1, 128), lambda s, h: (0, 0))]
    args = [qkv, qkv, qkv, gz, gates, gdn_norm.reshape(1, 128)]
    if has_init:
        in_specs += [state, state]
        args += [s0_f, s0_b]
    n_steps = (n_seq // seqs) * hgroups
    cast_in, cast_out, cast_shapes = _cast_side_job(cast, n_steps, lambda s, h: s * hgroups + h)
    in_specs += cast_in
    args += [arr for arr, _ in cast]
    t = n_seq * seq_len
    scratch = ([pltpu.VMEM((rows, hps * GDN_D), F32)]
               + [pltpu.VMEM((2, hps, rows, GDN_D), F32)]
               + [pltpu.VMEM((2, hps, rows, GDN_D), BF16) for _ in range(3)]
               + [pltpu.VMEM((2, hps, rows, GDN_CHUNK), BF16),
                  pltpu.VMEM((2, hps, rows // GDN_CHUNK * 8, 128), F32),
                  pltpu.VMEM((2, hps, seqs, GDN_D, GDN_D), F32)])
    return pl.pallas_call(
        functools.partial(_gdn_kernel, seq_len=seq_len, seqs=seqs, hps=hps, group=group, has_init=has_init,
                          n_cast=len(cast)),
        grid=(n_seq // seqs, hgroups),
        in_specs=in_specs,
        out_specs=[pl.BlockSpec((rows, hps * GDN_D), lambda s, h: (s, h)), state, state] + cast_out,
        out_shape=[jax.ShapeDtypeStruct((t, hd * GDN_D), F32),
                   jax.ShapeDtypeStruct((n_seq, 1, hd, GDN_D, GDN_D), F32),
                   jax.ShapeDtypeStruct((n_seq, 1, hd, GDN_D, GDN_D), F32)] + cast_shapes,
        scratch_shapes=scratch,
        compiler_params=pltpu.CompilerParams(dimension_semantics=("parallel", "parallel"),
                                             vmem_limit_bytes=VMEM_LIMIT),
        name="gdn_mixer",
    )(*args)


def _softmax_pv(scores, values, sink):
    m = sink
    for s in scores:
        m = jnp.maximum(m, jnp.max(s, axis=-1, keepdims=True))
    den = jnp.exp2(sink - m)
    acc = None
    for s, v in zip(scores, values):
        e = jnp.exp2(s - m)
        den = den + jnp.sum(e, axis=-1, keepdims=True)
        pv = _mm(e, v, prec="bf")
        acc = pv if acc is None else acc + pv
    return acc / den


def _group_sinks(sink_ref, j, rows):
    assert rows & (rows - 1) == 0
    grp = lax.broadcasted_iota(jnp.int32, (ATT_GROUP * rows, 1), 0) >> (rows.bit_length() - 1)
    col = jnp.full(grp.shape, sink_ref[j * ATT_GROUP], F32)
    for gi in range(1, ATT_GROUP):
        col = jnp.where(grp == gi, sink_ref[j * ATT_GROUP + gi], col)
    return col * LOG2E


def _store_group(o_ref, j, o, rows):
    for gi in range(ATT_GROUP):
        hh = j * ATT_GROUP + gi
        o_ref[:, hh * ATT_HD:(hh + 1) * ATT_HD] = o[gi * rows:(gi + 1) * rows]


def _attn_ctx_kernel(*refs, n_cast, seqs):
    sink_ref, p_ref = refs[:2]
    cast_in = refs[2:2 + n_cast]
    o_ref, kc_ref, vc_ref = refs[2 + n_cast:5 + n_cast]
    for src, dst in zip(cast_in, refs[5 + n_cast:]):
        dst[...] = src[...].astype(BF16)
    scale = ATT_HD ** -0.5 * LOG2E
    rows = p_ref.shape[0] // seqs
    chains = []
    for sq in range(seqs):
        tok = slice(sq * rows, (sq + 1) * rows)
        for j in range(ATT_KV_HEADS):
            k = p_ref[tok, ATT_Q_W + j * ATT_HD:ATT_Q_W + (j + 1) * ATT_HD]
            v = p_ref[tok, ATT_Q_W + ATT_KV_W + j * ATT_HD:ATT_Q_W + ATT_KV_W + (j + 1) * ATT_HD]
            kc_ref[sq, j] = k
            vc_ref[sq, j] = v
            q = jnp.concatenate([p_ref[tok, hh * ATT_HD:(hh + 1) * ATT_HD]
                                 for hh in range(j * ATT_GROUP, (j + 1) * ATT_GROUP)], axis=0) * scale
            chains.append((tok, j, q, k, v))
    scores = [_mm_nt(q, k, prec="bf") for _, _, q, k, _ in chains]
    outs = [_softmax_pv([s], [v], _group_sinks(sink_ref, j, rows)) for s, (_, j, _, _, v) in zip(scores, chains)]
    for o, (tok, j, _, _, _) in zip(outs, chains):
        _store_group(o_ref.at[tok, :], j, o, rows)


def _attn_context(proj_att, sink, n_seq, seq_len, cast=()):
    seqs = ATT_CTX_SEQS
    assert n_seq % seqs == 0
    kv = pl.BlockSpec((seqs, None, ATT_KV_HEADS, seq_len, ATT_HD), lambda b: (b, 0, 0, 0, 0))
    cast_in, cast_out, cast_shapes = _cast_side_job(cast, n_seq // seqs, lambda b: b)
    return pl.pallas_call(
        functools.partial(_attn_ctx_kernel, n_cast=len(cast), seqs=seqs),
        grid=(n_seq // seqs,),
        in_specs=[pl.BlockSpec(memory_space=pltpu.SMEM),
                  pl.BlockSpec((seqs * seq_len, ATT_W), lambda b: (b, 0))] + cast_in,
        out_specs=[pl.BlockSpec((seqs * seq_len, ATT_Q_W), lambda b: (b, 0)), kv, kv] + cast_out,
        out_shape=[jax.ShapeDtypeStruct((n_seq * seq_len, ATT_Q_W), F32),
                   jax.ShapeDtypeStruct((n_seq, 1, ATT_KV_HEADS, seq_len, ATT_HD), F32),
                   jax.ShapeDtypeStruct((n_seq, 1, ATT_KV_HEADS, seq_len, ATT_HD), F32)] + cast_shapes,
        compiler_params=pltpu.CompilerParams(dimension_semantics=("parallel",), vmem_limit_bytes=VMEM_LIMIT),
        name="attn_context",
    )(sink, proj_att, *[arr for arr, _ in cast])


def _rope_tables(seq_len):
    pos = np.arange(seq_len)
    half = ATT_HD // 2
    inv = ROPE_BASE ** (-np.arange(0, half, 2, dtype=np.float32) / half)
    ang_r = (pos // GRID_W).astype(np.float32)[:, None] * inv
    ang_c = (pos % GRID_W).astype(np.float32)[:, None] * inv
    cos = np.concatenate([np.cos(ang_r), np.cos(ang_r), np.cos(ang_c), np.cos(ang_c)], axis=1)
    sin = np.concatenate([-np.sin(ang_r), np.sin(ang_r), -np.sin(ang_c), np.sin(ang_c)], axis=1)
    return (jnp.asarray(np.tile(cos, (1, 2)), F32), jnp.asarray(np.tile(sin, (1, 2)), F32))


def _rope(x, cos, sin):
    lane = lax.broadcasted_iota(jnp.int32, x.shape, 1)
    partner = jnp.where((lane & 31) < 16, pltpu.roll(x, 128 - 16, 1), pltpu.roll(x, 16, 1))
    return x * cos + partner * sin


def _attn_lat_kernel(sink_ref, p_ref, ck_ref, cv_ref, cos_ref, sin_ref, o_ref, *, seq_len):
    scale = ATT_HD ** -0.5 * LOG2E
    qb = pl.program_id(1)
    span = 3 * ATT_BLOCK
    q0 = pl.multiple_of(qb * ATT_BLOCK, ATT_BLOCK)
    k0 = pl.multiple_of(jnp.clip((qb - 1) * ATT_BLOCK, 0, seq_len - span), ATT_BLOCK)
    qrows = pl.ds(q0, ATT_BLOCK)
    krows = pl.ds(k0, span)
    kwin = _rope(p_ref[krows, ATT_Q_W:ATT_Q_W + ATT_KV_W], cos_ref[krows, :], sin_ref[krows, :])
    vwin = p_ref[krows, ATT_Q_W + ATT_KV_W:ATT_W]
    stacked = ATT_GROUP * ATT_BLOCK
    qpos = q0 + (lax.broadcasted_iota(jnp.int32, (stacked, span), 0) & (ATT_BLOCK - 1))
    kpos = k0 + lax.broadcasted_iota(jnp.int32, (stacked, span), 1)
    valid = jnp.abs(qpos - kpos) <= WINDOW
    cos_q = cos_ref[qrows, :]
    sin_q = sin_ref[qrows, :]
    heads = []
    for pair in range(ATT_HEADS // 2):
        qpair = _rope(p_ref[qrows, pair * 128:(pair + 1) * 128], cos_q, sin_q) * scale
        heads += [qpair[:, :ATT_HD], qpair[:, ATT_HD:]]
    kv = range(ATT_KV_HEADS)
    qs = [jnp.concatenate(heads[j * ATT_GROUP:(j + 1) * ATT_GROUP], axis=0) for j in kv]
    s_locs = [jnp.where(valid, _mm_nt(qs[j], kwin[:, j * ATT_HD:(j + 1) * ATT_HD], prec="bf"), NEG_INF)
              for j in kv]
    s_ctxs = [_mm_nt(qs[j], ck_ref[j], prec="bf") for j in kv]
    outs = [_softmax_pv([s_locs[j], s_ctxs[j]], [vwin[:, j * ATT_HD:(j + 1) * ATT_HD], cv_ref[j]],
                        _group_sinks(sink_ref, j, ATT_BLOCK)) for j in kv]
    for j in kv:
        _store_group(o_ref, j, outs[j], ATT_BLOCK)


def _attn_latent(proj_att, sink, cache_k, cache_v, layer, n_seq, seq_len):
    cos, sin = _rope_tables(seq_len)
    past = cache_k.shape[3]
    nqb = seq_len // ATT_BLOCK
    cache = pl.BlockSpec((None, None, ATT_KV_HEADS, past, ATT_HD), lambda b, q: (b, layer, 0, 0, 0))
    table = pl.BlockSpec((seq_len, 128), lambda b, q: (0, 0))
    return pl.pallas_call(
        functools.partial(_attn_lat_kernel, seq_len=seq_len),
        grid=(n_seq, nqb),
        in_specs=[pl.BlockSpec(memory_space=pltpu.SMEM),
                  pl.BlockSpec((seq_len, ATT_W), lambda b, q: (b, 0)),
                  cache, cache, table, table],
        out_specs=pl.BlockSpec((ATT_BLOCK, ATT_Q_W), lambda b, q: (b * nqb + q, 0)),
        out_shape=jax.ShapeDtypeStruct((n_seq * seq_len, ATT_Q_W), F32),
        compiler_params=pltpu.CompilerParams(dimension_semantics=("parallel", "parallel")),
        name="attn_latent",
    )(sink, proj_att, cache_k, cache_v, cos, sin)


def _pair_masks(n, rev):
    r = lax.broadcasted_iota(jnp.int32, (n, 2 * n), 0)
    c = lax.broadcasted_iota(jnp.int32, (n, 2 * n), 1) & (n - 1)
    return (r <= c, r < c) if rev else (r >= c, r > c)


def _bd(x):
    half = x.shape[1] // 2
    lane = lax.broadcasted_iota(jnp.int32, x.shape, 1)
    zero = jnp.zeros_like(x)
    return jnp.concatenate([jnp.where(lane < half, x, zero), jnp.where(lane >= half, x, zero)], axis=0)


def _bd_mask(n):
    r = lax.broadcasted_iota(jnp.int32, (2 * n, 2 * n), 0)
    c = lax.broadcasted_iota(jnp.int32, (2 * n, 2 * n), 1)
    return (r < n) == (c < n)


def _head_sums(x):
    ones = jnp.where(_bd_mask(RWKV_HD), 1.0, 0.0).astype(BF16)
    return _mm(x, ones, "bf")


def _apply_pairs(tb, x):
    hi, lo = _split_bf16(x)
    return _mm(tb, _bd(hi), "bf") + _mm(tb, _bd(lo), "bf")


def _unit_tri_inv_pairs(mats):
    n = mats[0].shape[0]
    r = lax.broadcasted_iota(jnp.int32, (n, 2 * n), 0)
    c = lax.broadcasted_iota(jnp.int32, (n, 2 * n), 1) & (n - 1)
    eye = jnp.where(r == c, 1.0, 0.0)
    within2 = jnp.where((r >> 1) == (c >> 1), 1.0, 0.0)
    ts = [eye - a * within2 for a in mats]
    abs_ = [a.astype(BF16) for a in mats]
    shift = 1
    while (1 << shift) < n:
        join = jnp.where(((r >> (shift + 1)) == (c >> (shift + 1))) & ((r >> shift) != (c >> shift)),
                         1.0, 0.0).astype(BF16)
        tbs = [t.astype(BF16) for t in ts]
        inner = [_mm(ab * join, _bd(tb), "bf") for ab, tb in zip(abs_, tbs)]
        ts = [t - _mm(tb, _bd(w.astype(BF16)), "bf") for t, tb, w in zip(ts, tbs, inner)]
        shift += 1
    return ts


def _rwkv_chunk_operands(x_ref, prm, sc, c, *, seq_len):
    (mu_ref, w0_ref, wup_ref, a0_ref, aup_ref, gup_ref, kk_ref, ka_ref, rk_ref) = prm
    ch = RWKV_CHUNK
    total = x_ref.shape[0]
    r0 = pl.multiple_of(c * ch, ch)
    rows = pl.ds(r0, ch)
    pos0 = r0 & (seq_len - 1)
    x = x_ref[rows, :]
    prev_row = x_ref[pl.ds(jnp.maximum(r0 - 1, 0), 1), :] * jnp.where(pos0 > 0, 1.0, 0.0)
    next_row = x_ref[pl.ds(jnp.minimum(r0 + ch, total - 1), 1), :] * jnp.where(pos0 + ch < seq_len, 1.0, 0.0)
    xp, xn = _shifted_rows(x, prev_row, next_row)
    mu0, mu1 = mu_ref[0:1, :], mu_ref[1:2, :]
    xs = x * (1.0 - mu0 - mu1) + xp * mu0 + xn * mu1
    r = xs[:, 0:RWKV_W]
    k = xs[:, RWKV_W:2 * RWKV_W]
    v = xs[:, 2 * RWKV_W:3 * RWKV_W]
    lo = 3 * RWKV_W
    p = RWKV_PREC
    gl = xs[:, lo + 4 * RWKV_LORA:lo + 6 * RWKV_LORA]
    sc["gate"][rows, :] = _mm(jax.nn.sigmoid(gl), gup_ref[...], p["gate"])
    sc["v"][rows, :] = v.astype(BF16)
    pairs = [slice(i * PAIR_W, (i + 1) * PAIR_W) for i in range(N_PAIRS)]
    kkv = k * kk_ref[...]
    kaps = []
    for cols in pairs:
        kk_p = kkv[:, cols]
        kaps.append(kk_p * lax.rsqrt(_head_sums(kk_p * kk_p) + 1e-6))
    items = []
    bonus = None
    for d in range(2):
        rev = d == 1
        wl = xs[:, lo + d * RWKV_LORA:lo + (d + 1) * RWKV_LORA]
        al = xs[:, lo + 2 * RWKV_LORA + d * RWKV_LORA:lo + 2 * RWKV_LORA + (d + 1) * RWKV_LORA]
        lw = (-DECAY_SCALE * LOG2E) * jax.nn.sigmoid(w0_ref[d:d + 1, :] + _mm(jnp.tanh(wl), wup_ref[d], p["lora"]))
        a = jax.nn.sigmoid(a0_ref[d:d + 1, :] + _mm(al, aup_ref[d], p["lora"]))
        k2 = k * (1.0 + (a - 1.0) * ka_ref[...])
        g_inc = _mm(jnp.where(_tri_masks(ch, rev)[0], 1.0, 0.0), lw, p["cumsum"])
        g_tot = jnp.sum(lw, axis=0, keepdims=True)
        e_neg = jnp.exp2(-g_inc)
        e_end = jnp.exp2(g_tot - g_inc)
        e_exc = jnp.exp2(g_inc - lw)
        r_dec = r * jnp.exp2(g_inc)
        k_neg = k2 * e_neg
        k_end = k2 * e_end
        sc["dec"][d, pl.ds(pl.multiple_of(c * 8, 8), 8), :] = jnp.broadcast_to(jnp.exp2(g_tot), (8, RWKV_W))
        rkr = r * k2 * rk_ref[...]
        bon_d = jnp.concatenate([_head_sums(rkr[:, cols]) for cols in pairs], axis=1) * v
        bonus = bon_d if bonus is None else bonus + bon_d
        for cols, kap in zip(pairs, kaps):
            b_p = kap * a[:, cols]
            items.append(dict(d=d, rows=rows, cols=cols, kap_dec=kap * e_exc[:, cols], r_dec=r_dec[:, cols],
                              b_neg=b_p * e_neg[:, cols], k_neg=k_neg[:, cols],
                              b_end=b_p * e_end[:, cols], k_end=k_end[:, cols], vb=v[:, cols].astype(BF16)))
    sc["bon"][rows, :] = bonus
    return items


def _rwkv_solve(items, sc):
    ch = RWKV_CHUNK
    masks = [_pair_masks(ch, False), _pair_masks(ch, True)]
    ms = [_mm_nt(jnp.concatenate([it["kap_dec"], it["r_dec"]], axis=0),
                 jnp.concatenate([_bd(it["b_neg"].astype(BF16)), _bd(it["k_neg"].astype(BF16))], axis=0), "bf")
          for it in items]
    a_abs = [jnp.where(masks[it["d"]][1], m[:ch, :2 * ch], 0.0) for it, m in zip(items, ms)]
    a_aks = [jnp.where(masks[it["d"]][1], m[:ch, 2 * ch:], 0.0) for it, m in zip(items, ms)]
    a_rbs = [jnp.where(masks[it["d"]][0], m[ch:, :2 * ch], 0.0) for it, m in zip(items, ms)]
    a_rks = [jnp.where(masks[it["d"]][0], m[ch:, 2 * ch:], 0.0) for it, m in zip(items, ms)]
    tbs = [t.astype(BF16) for t in _unit_tri_inv_pairs(a_abs)]
    akvs = [_mm(a_ak, _bd(it["vb"]), "bf") for it, a_ak in zip(items, a_aks)]
    w2s = [_apply_pairs(tb, it["kap_dec"]) for it, tb in zip(items, tbs)]
    u0s = [_apply_pairs(tb, akv) for tb, akv in zip(tbs, akvs)]
    for it, w2, u0, a_rb, a_rk in zip(items, w2s, u0s, a_rbs, a_rks):
        d, rows, cols = it["d"], it["rows"], it["cols"]
        sc["w2"][d, rows, cols] = w2.astype(BF16)
        sc["rd"][d, rows, cols] = it["r_dec"].astype(BF16)
        sc["u0"][d, rows, cols] = u0
        sc["arb"][d, rows, cols] = a_rb.astype(BF16)
        sc["ark"][d, rows, cols] = a_rk.astype(BF16)
        sc["bh"][d, rows, cols] = it["b_end"].astype(BF16)
        sc["kh"][d, rows, cols] = it["k_end"].astype(BF16)


def _rwkv_recur(sc, st, i, *, seq_len, seqs):
    ch = RWKV_CHUNK
    n_chunks = seq_len // ch
    keep = _bd_mask(RWKV_HD)
    cs = []
    for j in range(seqs):
        for d in range(2):
            c = j * n_chunks + (i if d == 0 else n_chunks - 1 - i)
            rows = pl.ds(pl.multiple_of(c * ch, ch), ch)
            dec = sc["dec"][d, pl.ds(pl.multiple_of(c * 8, 8), 1), :]
            for pi in range(N_PAIRS):
                cols = slice(pi * PAIR_W, (pi + 1) * PAIR_W)
                cs.append(dict(d=d, j=j, p=pi, rows=rows, cols=cols, dec=dec[:, cols], s=st[d, j, pi]))
    sbs = [c["s"].astype(BF16) for c in cs]
    lss = [_mm_nt(jnp.concatenate([sc["w2"][c["d"], c["rows"], c["cols"]],
                                   sc["rd"][c["d"], c["rows"], c["cols"]]], axis=0), sb, "bf")
           for c, sb in zip(cs, sbs)]
    ubs = [(-(ls[:ch] + sc["u0"][c["d"], c["rows"], c["cols"]])).astype(BF16) for c, ls in zip(cs, lss)]
    vbs = [sc["v"][c["rows"], c["cols"]] for c in cs]
    ys = [ls[ch:] + _mm(jnp.concatenate([sc["arb"][c["d"], c["rows"], c["cols"]],
                                         sc["ark"][c["d"], c["rows"], c["cols"]]], axis=1),
                        jnp.concatenate([_bd(ub), _bd(vb)], axis=0), "bf")
          for c, ls, ub, vb in zip(cs, lss, ubs, vbs)]
    ups = [_mm_tn(jnp.concatenate([ub, vb], axis=0),
                  jnp.concatenate([sc["bh"][c["d"], c["rows"], c["cols"]],
                                   sc["kh"][c["d"], c["rows"], c["cols"]]], axis=0), "bf")
           for c, ub, vb in zip(cs, ubs, vbs)]
    for c, y, up in zip(cs, ys, ups):
        st[c["d"], c["j"], c["p"]] = c["s"] * c["dec"] + jnp.where(keep, up, 0.0)
        sc["ysum"][c["rows"], c["cols"]] += y


RWKV_SCRATCH = ("ysum", "bon", "gate", "v", "w2", "rd", "u0", "arb", "ark", "bh", "kh", "dec")


def _rwkv_kernel(*refs, seq_len, seqs, has_init):
    x_ref = refs[0]
    prm = refs[1:10]
    lnw_ref, lnb_ref = refs[10:12]
    pos = 12
    if has_init:
        s0f_ref, s0b_ref = refs[pos:pos + 2]
        pos += 2
    o_ref, sf_ref, sb_ref = refs[pos:pos + 3]
    sc = dict(zip(RWKV_SCRATCH, refs[pos + 3:]))
    st = refs[pos + 3 + len(RWKV_SCRATCH)]
    ch = RWKV_CHUNK
    hd = RWKV_HD
    n_chunks = seq_len // ch
    sc["ysum"][...] = jnp.zeros(sc["ysum"].shape, F32)

    def prepare(gi, carry):
        items = []
        for j in range(RWKV_GROUP):
            items += _rwkv_chunk_operands(x_ref, prm, sc, gi * RWKV_GROUP + j, seq_len=seq_len)
        _rwkv_solve(items, sc)
        return carry

    lax.fori_loop(0, seqs * n_chunks // RWKV_GROUP, prepare, 0)

    zero = jnp.zeros((hd, hd), F32)
    for d, s0_ref in enumerate((s0f_ref, s0b_ref) if has_init else (None, None)):
        for j in range(seqs):
            for pi in range(N_PAIRS):
                s_a = s0_ref[j, 2 * pi] if has_init else zero
                s_b = s0_ref[j, 2 * pi + 1] if has_init else zero
                st[d, j, pi] = jnp.concatenate([jnp.concatenate([s_a, zero], axis=1),
                                                jnp.concatenate([zero, s_b], axis=1)], axis=0)

    def recur(i, carry):
        _rwkv_recur(sc, st, i, seq_len=seq_len, seqs=seqs)
        return carry

    lax.fori_loop(0, n_chunks, recur, 0)
    for d, out_ref in enumerate((sf_ref, sb_ref)):
        for j in range(seqs):
            for pi in range(N_PAIRS):
                s = st[d, j, pi]
                out_ref[j, 2 * pi] = s[:hd, :hd]
                out_ref[j, 2 * pi + 1] = s[hd:, hd:]

    tile = RWKV_FINISH_ROWS

    def finish(i, carry):
        rows = pl.ds(pl.multiple_of(i * tile, tile), tile)
        pairs = [slice(pi * PAIR_W, (pi + 1) * PAIR_W) for pi in range(N_PAIRS)]
        ys = [sc["ysum"][rows, cols] for cols in pairs]
        cens = [y - _head_sums(y) * (1.0 / hd) for y in ys]
        vars_ = [_head_sums(cen * cen) * (1.0 / hd) for cen in cens]
        for cols, cen, var in zip(pairs, cens, vars_):
            yn = cen * lax.rsqrt(var + GN_EPS) * lnw_ref[:, cols] + lnb_ref[:, cols]
            o_ref[rows, cols] = (yn + sc["bon"][rows, cols]) * sc["gate"][rows, cols]
        return carry

    lax.fori_loop(0, seqs * seq_len // tile, finish, 0)


def _rwkv_mixer(x_rw, n_seq, seq_len, params, s0_f, s0_b):
    has_init = s0_f is not None
    (mu, w0, w_up, a0, a_up, g_up, k_k, k_a, r_k, ln_w, ln_b) = params
    row = lambda a: a.reshape(1, RWKV_W)
    args = [x_rw, mu, w0, w_up, a0, a_up, g_up, row(k_k), row(k_a), row(r_k), row(ln_w), row(ln_b)]

    def whole(a):
        nd = a.ndim
        return pl.BlockSpec(a.shape, lambda s, nd=nd: (0,) * nd)

    seqs = max(1, RWKV_BLOCK_ROWS // seq_len)
    rows = seqs * seq_len
    assert n_seq % seqs == 0 and (rows // RWKV_CHUNK) % RWKV_GROUP == 0 and seq_len & (seq_len - 1) == 0
    in_specs = [pl.BlockSpec((rows, RWKV_IN), lambda s: (s, 0))] + [whole(a) for a in args[1:]]
    state = pl.BlockSpec((seqs, None, RWKV_HEADS, RWKV_HD, RWKV_HD), lambda s: (s, 0, 0, 0, 0))
    if has_init:
        in_specs += [state, state]
        args += [s0_f, s0_b]
    st_shape = jax.ShapeDtypeStruct((n_seq, 1, RWKV_HEADS, RWKV_HD, RWKV_HD), F32)
    tok = lambda dt: pltpu.VMEM((rows, RWKV_W), dt)
    per_dir = lambda dt: pltpu.VMEM((2, rows, RWKV_W), dt)
    scratch = dict(ysum=tok(F32), bon=tok(F32), gate=tok(F32), v=tok(BF16), w2=per_dir(BF16), rd=per_dir(BF16),
                   u0=per_dir(F32), arb=per_dir(BF16), ark=per_dir(BF16), bh=per_dir(BF16), kh=per_dir(BF16),
                   dec=pltpu.VMEM((2, rows // RWKV_CHUNK * 8, RWKV_W), F32))
    return pl.pallas_call(
        functools.partial(_rwkv_kernel, seq_len=seq_len, seqs=seqs, has_init=has_init),
        grid=(n_seq // seqs,),
        in_specs=in_specs,
        out_specs=[pl.BlockSpec((rows, RWKV_W), lambda s: (s, 0)), state, state],
        out_shape=[jax.ShapeDtypeStruct((n_seq * seq_len, RWKV_W), F32), st_shape, st_shape],
        scratch_shapes=[scratch[name] for name in RWKV_SCRATCH]
        + [pltpu.VMEM((2, seqs, N_PAIRS, PAIR_W, PAIR_W), F32)],
        compiler_params=pltpu.CompilerParams(dimension_semantics=("parallel",), vmem_limit_bytes=VMEM_LIMIT),
        name="rwkv_mixer",
    )(*args)


def kernel(x_prompt, x_sample, state_gdn_fwd, state_gdn_bwd, cache_attn_k, cache_attn_v, state_rwkv_fwd, state_rwkv_bwd, c, c_ctx, mod_w, mod_b, norm_mix, norm_mlp, mlp_w1, mlp_w2, norm_final, ev_w_in, ev_w_out, gdn_conv, gdn_a_log, gdn_dt_bias, gdn_norm, sc_conv, od_w_in, od_w_out, attn_sink, rwkv_mu, rwkv_w0, rwkv_w_up, rwkv_a0, rwkv_a_up, rwkv_g_up, rwkv_k_k, rwkv_k_a, rwkv_r_k, rwkv_ln_w, rwkv_ln_b):
    bp, lp, _ = x_prompt.shape
    bs, ls, _ = x_sample.shape
    depth = mod_w.shape[0]
    c_rows = jnp.concatenate([c_ctx[None, :], c, jnp.zeros((MOD_ROWS - 1 - bs, D_MODEL), F32)], axis=0)
    mods, ev_w_t = _modulation(c_rows, mod_w, mod_b, ev_w_in)

    assert ls % TOKEN_TILE == 0 and (bp * lp) % TOKEN_TILE == 0
    groups = [
        dict(x=x_prompt.reshape(bp * lp, D_MODEL), n=bp, l=lp, latent=False,
             mod=lambda layer: _mod_spec(layer, 1, 0, 0)),
        dict(x=x_sample.reshape(bs * ls, D_MODEL), n=bs, l=ls, latent=True,
             mod=lambda layer: _mod_spec(layer, ls // TOKEN_TILE, 1, 1)),
    ]
    outs = {}
    for layer in range(depth):
        final = layer == depth - 1
        mlp_cast = ((mlp_w1, layer), (mlp_w2, layer))
        if layer % 2 == 0:
            e = layer // 2
            alog_vec = jnp.zeros((1, 128), F32).at[0, 2 * GDN_HEADS:4 * GDN_HEADS].set(gdn_a_log[e].reshape(-1))
            dtb_vec = jnp.zeros((1, 128), F32).at[0, 2 * GDN_HEADS:4 * GDN_HEADS].set(gdn_dt_bias[e].reshape(-1))
            for grp in groups:
                qkv, gz, sc, gates = _inproj_even(grp["x"], mods, norm_mix[layer], ev_w_t[e], gdn_conv[e],
                                                  sc_conv[e], alog_vec, dtb_vec, grp["l"], grp["mod"](layer))
                s0 = (state_gdn_fwd[:, e:e + 1], state_gdn_bwd[:, e:e + 1]) if grp["latent"] else (None, None)
                if grp["latent"]:
                    cast = ((od_w_in, layer // 2), (od_w_out, layer // 2)) if layer + 1 < depth else ()
                    o, _, _, *next_w = _gdn_mixer(qkv, gz, gates, grp["n"], grp["l"], gdn_norm[e], *s0, cast=cast)
                else:
                    o, s_f, s_b, w_out, w1, w2 = _gdn_mixer(qkv, gz, gates, grp["n"], grp["l"], gdn_norm[e], *s0,
                                                            cast=((ev_w_out, e),) + mlp_cast)
                    outs.setdefault("gdn_f", []).append(s_f)
                    outs.setdefault("gdn_b", []).append(s_b)
                grp["mixed"] = (o, sc)
        else:
            o_ = layer // 2
            w_in, w_out = [next_w[0]], next_w[1]
            rw = (rwkv_mu[o_], rwkv_w0[o_], rwkv_w_up[o_], rwkv_a0[o_], rwkv_a_up[o_], rwkv_g_up[o_],
                  rwkv_k_k[o_], rwkv_k_a[o_], rwkv_r_k[o_].reshape(-1), rwkv_ln_w[o_], rwkv_ln_b[o_])
            projs = _inproj(groups[0]["x"], groups[1]["x"], mods, norm_mix[layer], w_in, (ATT_W, RWKV_IN),
                            layer, ls // TOKEN_TILE)
            for grp, (p_att, x_rw) in zip(groups, (projs[:2], projs[2:])):
                if grp["latent"]:
                    att = _attn_latent(p_att, attn_sink[o_], cache_attn_k, cache_attn_v, o_, grp["n"], grp["l"])
                    rwo, _, _ = _rwkv_mixer(x_rw, grp["n"], grp["l"], rw,
                                            state_rwkv_fwd[:, o_:o_ + 1], state_rwkv_bwd[:, o_:o_ + 1])
                else:
                    att, kc, vc, w1, w2 = _attn_context(p_att, attn_sink[o_], grp["n"], grp["l"], cast=mlp_cast)
                    rwo, s_f, s_b = _rwkv_mixer(x_rw, grp["n"], grp["l"], rw, None, None)
                    outs.setdefault("att_k", []).append(kc)
                    outs.setdefault("att_v", []).append(vc)
                    outs.setdefault("rw_f", []).append(s_f)
                    outs.setdefault("rw_b", []).append(s_b)
                grp["mixed"] = (att, rwo)
        groups[0]["x"], groups[1]["x"] = _outproj_mlp(
            *[(*grp["mixed"], grp["x"]) for grp in groups], mods, norm_mlp[layer], w_out, w1, w2, norm_final,
            layer, ls // TOKEN_TILE, final)
    cat = lambda key: jnp.concatenate(outs[key], axis=1)
    return (groups[0]["x"].reshape(bp, lp, D_MODEL), groups[1]["x"].reshape(bs, ls, D_MODEL),
            cat("gdn_f"), cat("gdn_b"), cat("att_k"), cat("att_v"), cat("rw_f"), cat("rw_b"))
```
